```python
import math
import jax
import jax.numpy as jnp
from jax import lax
import numpy as np

D_MODEL = 2048
BATCH = 2
SEQ = 8192
DEPTH = 1

D_MIX = D_MODEL
HEAD_DIM = 64
NSA_HEADS = 16
NSA_KV_GROUPS = 2
NSA_REP = NSA_HEADS // NSA_KV_GROUPS
NSA_WIDTH = NSA_HEADS * HEAD_DIM
KV_WIDTH = NSA_KV_GROUPS * HEAD_DIM
CMP_BLOCK = 32
CMP_STRIDE = 16
CMP_HIDDEN = 256
SEL_BLOCK = 64
SEL_TOPK = 16
WINDOW = 512
Q_BLOCK = 128
FORCED_SCORE = 1.0e4
SSM_WIDTH = D_MIX - NSA_WIDTH
SSM_HEADDIM = 64
SSM_HEADS = SSM_WIDTH // SSM_HEADDIM
SSM_GROUPS = 4
SSM_REP = SSM_HEADS // SSM_GROUPS
SSM_STATE = 128
CONV_WIDTH = 4
CHUNK = 256
XBC_WIDTH = SSM_WIDTH + 2 * SSM_GROUPS * SSM_STATE
ROPE_THETA = 500000.0
ROT_DIM = HEAD_DIM // 4
N_EXPERT_GROUPS = 4
EXPERTS_PER_GROUP = 8
N_EXPERTS = N_EXPERT_GROUPS * EXPERTS_PER_GROUP
EXPERT_TOPK = 2
D_EXPERT = 512
MOE_BLOCK = 128
DEEPNORM_ALPHA = (2 * DEPTH) ** 0.25
DEEPNORM_BETA = (8 * DEPTH) ** -0.25
NORM_EPS = 1e-5
PROJ_WIDTHS = (NSA_WIDTH, 6 * KV_WIDTH, 3 * NSA_HEADS, SSM_WIDTH, XBC_WIDTH, SSM_HEADS)
D_IN_PROJ = sum(PROJ_WIDTHS)

kernel_name = "hymba_nsa_mamba2_hmoe_deepnorm_block"


def layer_norm(x, g, b):
    xf = x.astype(jnp.float32)
    mu = xf.mean(-1, keepdims=True)
    var = jnp.square(xf - mu).mean(-1, keepdims=True)
    return ((xf - mu) * lax.rsqrt(var + NORM_EPS) * g + b).astype(x.dtype)


def rope_tables(pos):
    inv_freq = ROPE_THETA ** (-jnp.arange(0, ROT_DIM, 2, dtype=jnp.float32) / ROT_DIM)
    ang = pos.astype(jnp.float32)[..., None] * inv_freq
    return jnp.cos(ang), jnp.sin(ang)


def apply_partial_rope(x, cos, sin):
    half = ROT_DIM // 2
    x1, x2, rest = x[..., :half], x[..., half:ROT_DIM], x[..., ROT_DIM:]
    out = jnp.concatenate([x1 * cos - x2 * sin, x2 * cos + x1 * sin, rest.astype(jnp.float32)], axis=-1)
    return out.astype(x.dtype)


def masked_softmax(s, mask):
    s = jnp.where(mask, s.astype(jnp.float32), jnp.finfo(jnp.float32).min)
    e = jnp.where(mask, jnp.exp(s - s.max(-1, keepdims=True)), 0.0)
    return e / jnp.maximum(e.sum(-1, keepdims=True), jnp.finfo(jnp.float32).tiny)


def compress_blocks(blocks, pe, w1, b1, w2):
    bsz, g, n = blocks.shape[:3]
    h = (blocks + pe).reshape(bsz, g, n, CMP_BLOCK * HEAD_DIM)
    return jax.nn.gelu(h @ w1 + b1) @ w2


def nsa_mixer(q, k_cmp, v_cmp, k_sel, v_sel, k_win, v_win, gate_logits, positions,
              cmp_k_pe, cmp_k_w1, cmp_k_b1, cmp_k_w2, cmp_v_pe, cmp_v_w1, cmp_v_b1, cmp_v_w2):
    b, t = q.shape[:2]
    cos, sin = rope_tables(positions)
    cos_h, sin_h = cos[:, :, None], sin[:, :, None]
    q = apply_partial_rope(q, cos_h, sin_h)
    k_sel = apply_partial_rope(k_sel, cos_h, sin_h)
    k_win = apply_partial_rope(k_win, cos_h, sin_h)
    q = q.reshape(b, t, NSA_KV_GROUPS, NSA_REP, HEAD_DIM).transpose(0, 2, 3, 1, 4)
    k_cmp, v_cmp, k_sel, v_sel, k_win, v_win = [a.transpose(0, 2, 1, 3) for a in (k_cmp, v_cmp, k_sel, v_sel, k_win, v_win)]
    gates = jax.nn.sigmoid(gate_logits.astype(jnp.float32)).reshape(b, t, NSA_KV_GROUPS, NSA_REP, 3).transpose(0, 2, 3, 1, 4)
    scale = HEAD_DIM ** -0.5

    n_cmp = (t - CMP_BLOCK) // CMP_STRIDE + 1
    cmp_idx = jnp.arange(n_cmp)[:, None] * CMP_STRIDE + jnp.arange(CMP_BLOCK)[None, :]
    cmp_end = cmp_idx[:, -1]
    kc = compress_blocks(k_cmp[:, :, cmp_idx], cmp_k_pe, cmp_k_w1, cmp_k_b1, cmp_k_w2)
    vc = compress_blocks(v_cmp[:, :, cmp_idx], cmp_v_pe, cmp_v_w1, cmp_v_b1, cmp_v_w2)
    cos_c, sin_c = rope_tables(positions[:, cmp_end])
    kc = apply_partial_rope(kc, cos_c[:, None], sin_c[:, None])

    n_sel = t // SEL_BLOCK
    sel_k = min(SEL_TOPK, n_sel)
    c_start = jnp.arange(n_cmp) * CMP_STRIDE
    s_start = jnp.arange(n_sel) * SEL_BLOCK
    cover = ((c_start[:, None] < s_start[None, :] + SEL_BLOCK) &
             (c_start[:, None] + CMP_BLOCK > s_start[None, :])).astype(jnp.float32)
    k_blocks = k_sel.reshape(b, NSA_KV_GROUPS, n_sel, SEL_BLOCK * HEAD_DIM)
    v_blocks = v_sel.reshape(b, NSA_KV_GROUPS, n_sel, SEL_BLOCK * HEAD_DIM)

    k_pad = jnp.pad(k_win, ((0, 0), (0, 0), (WINDOW, 0), (0, 0)))
    v_pad = jnp.pad(v_win, ((0, 0), (0, 0), (WINDOW, 0), (0, 0)))
    sel_j = jnp.arange(n_sel)

    def query_block(start):
        tq = start + jnp.arange(Q_BLOCK)
        qb = lax.dynamic_slice_in_dim(q, start, Q_BLOCK, axis=3)
        s_c = jnp.einsum('bgrqd,bgcd->bgrqc', qb, kc) * scale
        p_c = masked_softmax(s_c, cmp_end[None, :] <= tq[:, None])
        o_c = jnp.einsum('bgrqc,bgcd->bgrqd', p_c, vc)
        imp = jnp.einsum('bgrqc,cj->bgqj', p_c, cover)
        cur = tq // SEL_BLOCK
        forced = (sel_j[None] == 0) | (sel_j[None] == cur[:, None]) | (sel_j[None] == cur[:, None] - 1)
        valid = sel_j[None] * SEL_BLOCK <= tq[:, None]
        imp = jnp.where(valid, jnp.where(forced, FORCED_SCORE, imp), -FORCED_SCORE)
        _, sel_idx = lax.top_k(imp, sel_k)
        flat = sel_idx.reshape(b, NSA_KV_GROUPS, Q_BLOCK * sel_k)[..., None]
        ks = jnp.take_along_axis(k_blocks, flat, axis=2).reshape(b, NSA_KV_GROUPS, Q_BLOCK, sel_k * SEL_BLOCK, HEAD_DIM)
        vs = jnp.take_along_axis(v_blocks, flat, axis=2).reshape(b, NSA_KV_GROUPS, Q_BLOCK, sel_k * SEL_BLOCK, HEAD_DIM)
        pos_s = (sel_idx[..., None] * SEL_BLOCK + jnp.arange(SEL_BLOCK)).reshape(b, NSA_KV_GROUPS, Q_BLOCK, sel_k * SEL_BLOCK)
        s_s = jnp.einsum('bgrqd,bgqkd->bgrqk', qb, ks) * scale
        p_s = masked_softmax(s_s, (pos_s <= tq[None, None, :, None])[:, :, None])
        o_s = jnp.einsum('bgrqk,bgqkd->bgrqd', p_s, vs)
        kw = lax.dynamic_slice_in_dim(k_pad, start, WINDOW + Q_BLOCK, axis=2)
        vw = lax.dynamic_slice_in_dim(v_pad, start, WINDOW + Q_BLOCK, axis=2)
        pos_w = start - WINDOW + jnp.arange(WINDOW + Q_BLOCK)
        m_w = (pos_w[None] <= tq[:, None]) & (pos_w[None] > tq[:, None] - WINDOW) & (pos_w[None] >= 0)
        s_w = jnp.einsum('bgrqd,bgkd->bgrqk', qb, kw) * scale
        o_w = jnp.einsum('bgrqk,bgkd->bgrqd', masked_softmax(s_w, m_w), vw)
        gb = lax.dynamic_slice_in_dim(gates, start, Q_BLOCK, axis=3)
        return gb[..., 0:1] * o_c + gb[..., 1:2] * o_s + gb[..., 2:3] * o_w

    starts = jnp.arange(t // Q_BLOCK) * Q_BLOCK
    out = lax.map(query_block, starts)
    return out.transpose(1, 0, 4, 2, 3, 5).reshape(b, t, NSA_WIDTH)


def ssd_chunked(xs, dt, a, bm, cm):
    b, t = xs.shape[:2]
    q_len = math.gcd(CHUNK, t)
    c = t // q_len
    x = xs.astype(jnp.float32).reshape(b, c, q_len, SSM_GROUPS, SSM_REP, SSM_HEADDIM)
    bc = bm.astype(jnp.float32).reshape(b, c, q_len, SSM_GROUPS, SSM_STATE)
    cc = cm.astype(jnp.float32).reshape(b, c, q_len, SSM_GROUPS, SSM_STATE)
    dtc = dt.reshape(b, c, q_len, SSM_GROUPS, SSM_REP).transpose(0, 1, 3, 4, 2)
    a_cs = jnp.cumsum(dtc * a.reshape(SSM_GROUPS, SSM_REP)[:, :, None], axis=-1)
    seg = a_cs[..., :, None] - a_cs[..., None, :]
    causal = jnp.tril(jnp.ones((q_len, q_len), dtype=bool))
    decay = jnp.where(causal, jnp.exp(jnp.where(causal, seg, 0.0)), 0.0)
    cb = jnp.einsum('bclgn,bcsgn->bcgls', cc, bc)
    w_diag = cb[:, :, :, None] * decay * dtc[..., None, :]
    y_diag = jnp.einsum('bcgrls,bcsgrp->bclgrp', w_diag, x)
    decay_to_end = jnp.exp(a_cs[..., -1:] - a_cs)
    states = jnp.einsum('bcsgn,bcgrs,bcsgrp->bcgrpn', bc, decay_to_end * dtc, x)
    chunk_decay = jnp.exp(a_cs[..., -1])

    def step(h, inp):
        dec, st = inp
        return dec[..., None, None] * h + st, h

    h0 = jnp.zeros((b, SSM_GROUPS, SSM_REP, SSM_HEADDIM, SSM_STATE), jnp.float32)
    _, prev = lax.scan(step, h0, (chunk_decay.transpose(1, 0, 2, 3), states.transpose(1, 0, 2, 3, 4, 5)))
    prev = prev.transpose(1, 0, 2, 3, 4, 5)
    y_off = jnp.einsum('bclgn,bcgrpn,bcgrl->bclgrp', cc, prev, jnp.exp(a_cs))
    return (y_diag + y_off).reshape(b, t, SSM_HEADS, SSM_HEADDIM)


def mamba2_mixer(z, xbc, dt_raw, conv_w, conv_b, dt_bias, a_log, d_skip, norm_w):
    b, t = z.shape[:2]
    xbc = lax.conv_general_dilated(xbc, conv_w, window_strides=(1,), padding=[(CONV_WIDTH - 1, 0)],
                                   dimension_numbers=('NWC', 'WIO', 'NWC'), feature_group_count=XBC_WIDTH) + conv_b
    xbc = jax.nn.silu(xbc)
    xs, bm, cm = jnp.split(xbc, [SSM_WIDTH, SSM_WIDTH + SSM_GROUPS * SSM_STATE], axis=-1)
    xs = xs.reshape(b, t, SSM_HEADS, SSM_HEADDIM)
    bm = bm.reshape(b, t, SSM_GROUPS, SSM_STATE)
    cm = cm.reshape(b, t, SSM_GROUPS, SSM_STATE)
    dt = jax.nn.softplus(dt_raw.astype(jnp.float32) + dt_bias.astype(jnp.float32))
    a = -jnp.exp(a_log.astype(jnp.float32))
    y = ssd_chunked(xs, dt, a, bm, cm) + d_skip[:, None] * xs
    y = y.reshape(b, t, SSM_WIDTH) * jax.nn.silu(z.astype(jnp.float32))
    yg = y.reshape(b, t, SSM_GROUPS, SSM_WIDTH // SSM_GROUPS)
    yg = yg * lax.rsqrt(jnp.mean(jnp.square(yg), axis=-1, keepdims=True) + NORM_EPS)
    return (yg.reshape(b, t, SSM_WIDTH) * norm_w).astype(z.dtype)


def hierarchical_moe(h, w_router_group, b_router_group, w_router_expert, b_router_expert, w_gate, w_up, w_down):
    n, d = h.shape
    p_group = jax.nn.softmax((h @ w_router_group).astype(jnp.float32) + b_router_group, axis=-1)
    g_sel = jnp.argmax(p_group, axis=-1)
    g_gate = jnp.take_along_axis(p_group, g_sel[:, None], axis=-1)
    e_logits = ((h @ w_router_expert).astype(jnp.float32) + b_router_expert).reshape(n, N_EXPERT_GROUPS, EXPERTS_PER_GROUP)
    e_logits = jnp.take_along_axis(e_logits, g_sel[:, None, None], axis=1)[:, 0]
    top_p, top_i = lax.top_k(jax.nn.softmax(e_logits, axis=-1), EXPERT_TOPK)
    weights = g_gate * top_p / top_p.sum(-1, keepdims=True)
    flat_e = (g_sel[:, None] * EXPERTS_PER_GROUP + top_i).reshape(-1).astype(jnp.int32)
    flat_w = weights.reshape(-1)
    flat_tok = jnp.repeat(jnp.arange(n, dtype=jnp.int32), EXPERT_TOPK)
    n_assign = n * EXPERT_TOPK
    order = jnp.argsort(flat_e)
    sorted_e = flat_e[order]
    counts = jax.ops.segment_sum(jnp.ones_like(flat_e), flat_e, num_segments=N_EXPERTS)
    starts = jnp.cumsum(counts) - counts
    padded = (counts + MOE_BLOCK - 1) // MOE_BLOCK * MOE_BLOCK
    pad_ends = jnp.cumsum(padded)
    pad_starts = pad_ends - padded
    dest = pad_starts[sorted_e] + jnp.arange(n_assign, dtype=jnp.int32) - starts[sorted_e]
    buf_len = -(-n_assign // MOE_BLOCK) * MOE_BLOCK + N_EXPERTS * MOE_BLOCK
    n_blocks = buf_len // MOE_BLOCK
    buf_tok = jnp.zeros((buf_len,), jnp.int32).at[dest].set(flat_tok[order])
    buf_w = jnp.zeros((buf_len,), jnp.float32).at[dest].set(flat_w[order])
    block_start = jnp.arange(n_blocks, dtype=jnp.int32) * MOE_BLOCK
    block_e = jnp.minimum((block_start[:, None] >= pad_ends[None, :]).sum(-1), N_EXPERTS - 1)

    def expert_block(args):
        tok, e = args
        xb = h[tok]
        return (jax.nn.silu(xb @ w_gate[e]) * (xb @ w_up[e])) @ w_down[e]

    y = lax.map(expert_block, (buf_tok.reshape(n_blocks, MOE_BLOCK), block_e))
    out = jnp.zeros((n, d), jnp.float32).at[buf_tok].add(y.reshape(buf_len, d) * buf_w[:, None])
    return out.astype(h.dtype)


def hybrid_layer(x, positions, w_in, cmp_k_pe, cmp_k_w1, cmp_k_b1, cmp_k_w2, cmp_v_pe, cmp_v_w1, cmp_v_b1, cmp_v_w2,
                 conv_w, conv_b, dt_bias, a_log, d_skip, ssm_norm_w, w_out, ln1_g, ln1_b,
                 w_router_group, b_router_group, w_router_expert, b_router_expert, w_gate, w_up, w_down, ln2_g, ln2_b):
    b, t, d = x.shape
    proj = x @ w_in
    cuts = np.cumsum(PROJ_WIDTHS)[:-1].tolist()
    q, kv, gate_logits, z, xbc, dt_raw = jnp.split(proj, cuts, axis=-1)
    q = q.reshape(b, t, NSA_HEADS, HEAD_DIM)
    k_cmp, v_cmp, k_sel, v_sel, k_win, v_win = [a.reshape(b, t, NSA_KV_GROUPS, HEAD_DIM) for a in jnp.split(kv, 6, axis=-1)]
    y_nsa = nsa_mixer(q, k_cmp, v_cmp, k_sel, v_sel, k_win, v_win, gate_logits, positions,
                      cmp_k_pe, cmp_k_w1, cmp_k_b1, cmp_k_w2, cmp_v_pe, cmp_v_w1, cmp_v_b1, cmp_v_w2)
    y_ssm = mamba2_mixer(z, xbc, dt_raw, conv_w, conv_b, dt_bias, a_log, d_skip, ssm_norm_w)
    mix = jnp.concatenate([y_nsa.astype(x.dtype), y_ssm.astype(x.dtype)], axis=-1) @ w_out
    h = layer_norm(DEEPNORM_ALPHA * x + mix, ln1_g, ln1_b)
    ffn = hierarchical_moe(h.reshape(b * t, d), w_router_group, b_router_group, w_router_expert, b_router_expert,
                           w_gate, w_up, w_down).reshape(b, t, d)
    return layer_norm(DEEPNORM_ALPHA * h + ffn, ln2_g, ln2_b)


def setup_inputs(seed: int = 0) -> dict:
    key = jax.random.key(seed)
    ks = jax.random.split(key, 32)
    f32 = jnp.float32
    nrm = lambda k, shape, s: jax.random.normal(k, shape, f32) * s
    x = jax.random.normal(ks[0], (BATCH, SEQ, D_MODEL), f32)
    offset = jax.random.randint(ks[1], (BATCH, 1), 0, 4096, dtype=jnp.int32)
    positions = offset + jnp.arange(SEQ, dtype=jnp.int32)[None, :]
    dt_init = jnp.exp(jax.random.uniform(ks[2], (DEPTH, SSM_HEADS), f32, math.log(1e-3), math.log(1e-1)))
    return {
        'x': x,
        'positions': positions,
        'w_in': nrm(ks[3], (DEPTH, D_MODEL, D_IN_PROJ), D_MODEL ** -0.5),
        'cmp_k_pe': nrm(ks[4], (DEPTH, CMP_BLOCK, HEAD_DIM), 0.1),
        'cmp_k_w1': nrm(ks[5], (DEPTH, CMP_BLOCK * HEAD_DIM, CMP_HIDDEN), (CMP_BLOCK * HEAD_DIM) ** -0.5),
        'cmp_k_b1': nrm(ks[6], (DEPTH, CMP_HIDDEN), 0.01),
        'cmp_k_w2': nrm(ks[7], (DEPTH, CMP_HIDDEN, HEAD_DIM), CMP_HIDDEN ** -0.5),
        'cmp_v_pe': nrm(ks[8], (DEPTH, CMP_BLOCK, HEAD_DIM), 0.1),
        'cmp_v_w1': nrm(ks[9], (DEPTH, CMP_BLOCK * HEAD_DIM, CMP_HIDDEN), (CMP_BLOCK * HEAD_DIM) ** -0.5),
        'cmp_v_b1': nrm(ks[10], (DEPTH, CMP_HIDDEN), 0.01),
        'cmp_v_w2': nrm(ks[11], (DEPTH, CMP_HIDDEN, HEAD_DIM), CMP_HIDDEN ** -0.5),
        'conv_w': nrm(ks[12], (DEPTH, CONV_WIDTH, 1, XBC_WIDTH), CONV_WIDTH ** -0.5),
        'conv_b': nrm(ks[13], (DEPTH, XBC_WIDTH), 0.01),
        'dt_bias': dt_init + jnp.log(-jnp.expm1(-dt_init)),
        'a_log': jnp.log(jax.random.uniform(ks[14], (DEPTH, SSM_HEADS), f32, 1.0, 16.0)),
        'd_skip': 1.0 + nrm(ks[15], (DEPTH, SSM_HEADS), 0.1),
        'ssm_norm_w': 1.0 + nrm(ks[16], (DEPTH, SSM_WIDTH), 0.02),
        'w_out': nrm(ks[17], (DEPTH, D_MIX, D_MODEL), D_MIX ** -0.5 * DEEPNORM_BETA),
        'ln1_g': 1.0 + nrm(ks[18], (DEPTH, D_MODEL), 0.02),
        'ln1_b': nrm(ks[19], (DEPTH, D_MODEL), 0.02),
        'w_router_group': nrm(ks[20], (DEPTH, D_MODEL, N_EXPERT_GROUPS), D_MODEL ** -0.5),
        'b_router_group': nrm(ks[21], (DEPTH, N_EXPERT_GROUPS), 0.01),
        'w_router_expert': nrm(ks[22], (DEPTH, D_MODEL, N_EXPERTS), D_MODEL ** -0.5),
        'b_router_expert': nrm(ks[23], (DEPTH, N_EXPERTS), 0.01),
        'w_gate': nrm(ks[24], (DEPTH, N_EXPERTS, D_MODEL, D_EXPERT), D_MODEL ** -0.5),
        'w_up': nrm(ks[25], (DEPTH, N_EXPERTS, D_MODEL, D_EXPERT), D_MODEL ** -0.5),
        'w_down': nrm(ks[26], (DEPTH, N_EXPERTS, D_EXPERT, D_MODEL), D_EXPERT ** -0.5 * DEEPNORM_BETA),
        'ln2_g': 1.0 + nrm(ks[27], (DEPTH, D_MODEL), 0.02),
        'ln2_b': nrm(ks[28], (DEPTH, D_MODEL), 0.02),
    }


def reference(x, positions, w_in, cmp_k_pe, cmp_k_w1, cmp_k_b1, cmp_k_w2, cmp_v_pe, cmp_v_w1, cmp_v_b1, cmp_v_w2,
              conv_w, conv_b, dt_bias, a_log, d_skip, ssm_norm_w, w_out, ln1_g, ln1_b,
              w_router_group, b_router_group, w_router_expert, b_router_expert, w_gate, w_up, w_down, ln2_g, ln2_b):
    for l in range(DEPTH):
        x = hybrid_layer(x, positions, w_in[l], cmp_k_pe[l], cmp_k_w1[l], cmp_k_b1[l], cmp_k_w2[l],
                         cmp_v_pe[l], cmp_v_w1[l], cmp_v_b1[l], cmp_v_w2[l],
                         conv_w[l], conv_b[l], dt_bias[l], a_log[l], d_skip[l], ssm_norm_w[l], w_out[l],
                         ln1_g[l], ln1_b[l], w_router_group[l], b_router_group[l], w_router_expert[l],
                         b_router_expert[l], w_gate[l], w_up[l], w_down[l], ln2_g[l], ln2_b[l])
    return x
```

```python
import functools
import math

import jax
import jax.numpy as jnp
import numpy as np
from jax import lax
from jax.experimental import pallas as pl
from jax.experimental.pallas import tpu as pltpu

f32 = jnp.float32
bf16 = jnp.bfloat16
i32 = jnp.int32

HEAD_DIM = 64
NSA_HEADS = 16
NSA_KV_GROUPS = 2
NSA_REP = NSA_HEADS // NSA_KV_GROUPS
NSA_WIDTH = NSA_HEADS * HEAD_DIM
KV_WIDTH = NSA_KV_GROUPS * HEAD_DIM
CMP_BLOCK = 32
CMP_STRIDE = 16
CMP_HIDDEN = 256
SEL_BLOCK = 64
SEL_TOPK = 16
WINDOW = 512
FORCED_SCORE = 1.0e4
SSM_HEADDIM = 64
SSM_HEADS = 16
SSM_WIDTH = SSM_HEADS * SSM_HEADDIM
SSM_GROUPS = 4
SSM_REP = SSM_HEADS // SSM_GROUPS
SSM_STATE = 128
CONV_WIDTH = 4
CHUNK = 256
XBC_WIDTH = SSM_WIDTH + 2 * SSM_GROUPS * SSM_STATE
ROPE_THETA = 500000.0
ROT_DIM = HEAD_DIM // 4
N_EXPERT_GROUPS = 4
EXPERTS_PER_GROUP = 8
N_EXPERTS = N_EXPERT_GROUPS * EXPERTS_PER_GROUP
D_EXPERT = 512
NORM_EPS = 1e-5

LANES = 128
MASK_NEG = -1.0e30

PROJ_TM = 1024
PROJ_TN = 512
PREP_TM = 256
ATT_TQ = 128
ATT_TK = 512
OUT_TM = 256
MOE_TM = 256
DISP_TM = 256
COMB_TM = 128
VMEM_LIMIT = 56 * 1024 * 1024

COL_Q = 0
COL_Z = NSA_WIDTH
COL_XBC = COL_Z + SSM_WIDTH
COL_KV = COL_XBC + XBC_WIDTH
COL_SMALL = COL_KV + 6 * KV_WIDTH
PROJ_COLS = 5120
SMALL_DT_OFF = 3 * NSA_HEADS


def _cparams(sem, vmem=VMEM_LIMIT):
    return pltpu.CompilerParams(dimension_semantics=sem, vmem_limit_bytes=vmem)


def _dot(a, b):
    return jnp.dot(a, b, preferred_element_type=f32)


def _dot_t(a, b):
    return lax.dot_general(a, b, (((1,), (1,)), ((), ())), preferred_element_type=f32)


def _dot_hi(a, b):
    return jnp.dot(a, b, preferred_element_type=f32, precision=lax.Precision.HIGHEST)


def _proj_kernel(x_ref, w_ref, o_ref, xb_ref):
    @pl.when(pl.program_id(1) == 0)
    def _():
        xb_ref[...] = x_ref[...].astype(bf16)

    o_ref[...] = _dot(xb_ref[...], w_ref[...])


def _proj(x2, w_cat):
    n, d = x2.shape
    cols = w_cat.shape[1]
    tm = min(PROJ_TM, n)
    return pl.pallas_call(
        _proj_kernel,
        grid=(n // tm, cols // PROJ_TN),
        in_specs=[pl.BlockSpec((tm, d), lambda i, j: (i, 0)),
                  pl.BlockSpec((d, PROJ_TN), lambda i, j: (0, j))],
        out_specs=pl.BlockSpec((tm, PROJ_TN), lambda i, j: (i, j)),
        out_shape=jax.ShapeDtypeStruct((n, cols), f32),
        scratch_shapes=[pltpu.VMEM((tm, d), bf16)],
        compiler_params=_cparams(("parallel", "arbitrary")),
        name="proj",
    )(x2, w_cat)


def _rope_tables(pos_i32, invf):
    ang = pos_i32.astype(f32) * invf
    return jnp.cos(ang), jnp.sin(ang)


def _rope128(x, cos, sin):
    half = ROT_DIM // 2
    d = lax.broadcasted_iota(i32, x.shape, 1) % HEAD_DIM
    up = pltpu.roll(x, LANES - half, 1)
    dn = pltpu.roll(x, half, 1)
    rot = jnp.where(d < half, -up, dn)
    return x * cos + rot * sin


def _nsa_prep_kernel(pos_ref, invf_ref, q_ref, kc_ref, vc_ref, ks_ref, vs_ref, kw_ref, vw_ref,
                     qo_ref, kco_ref, vco_ref, kso_ref, vso_ref, kwo_ref, vwo_ref):
    cos, sin = _rope_tables(pos_ref[...], invf_ref[...])
    scale = HEAD_DIM ** -0.5
    for c in range(NSA_WIDTH // LANES):
        t = _rope128(q_ref[:, c * LANES:(c + 1) * LANES], cos, sin) * scale
        qo_ref[0, 2 * c] = t[:, :HEAD_DIM].astype(bf16)
        qo_ref[0, 2 * c + 1] = t[:, HEAD_DIM:].astype(bf16)

    def split(src, dst, rope, dt):
        t = src[...]
        if rope:
            t = _rope128(t, cos, sin)
        for g in range(NSA_KV_GROUPS):
            dst[0, g] = t[:, g * HEAD_DIM:(g + 1) * HEAD_DIM].astype(dt)

    split(kc_ref, kco_ref, False, f32)
    split(vc_ref, vco_ref, False, f32)
    split(ks_ref, kso_ref, True, bf16)
    split(vs_ref, vso_ref, False, bf16)
    split(kw_ref, kwo_ref, True, bf16)
    split(vw_ref, vwo_ref, False, bf16)


def _nsa_prep(proj, pos128, invf, b, t):
    tm = PREP_TM
    nt = t // tm
    row = lambda bi, ti: (bi * nt + ti, 0)
    kv0 = COL_KV // LANES
    in_specs = [pl.BlockSpec((tm, LANES), row),
                pl.BlockSpec((1, LANES), lambda bi, ti: (0, 0)),
                pl.BlockSpec((tm, NSA_WIDTH), lambda bi, ti: (bi * nt + ti, COL_Q // NSA_WIDTH))]
    for k in range(6):
        in_specs.append(pl.BlockSpec((tm, LANES), functools.partial(lambda bi, ti, k: (bi * nt + ti, kv0 + k), k=k)))
    head = lambda bi, ti: (bi, 0, ti, 0)
    out_specs = [pl.BlockSpec((1, NSA_HEADS, tm, HEAD_DIM), head)]
    out_shape = [jax.ShapeDtypeStruct((b, NSA_HEADS, t, HEAD_DIM), bf16)]
    for dt in (f32, f32, bf16, bf16, bf16, bf16):
        out_specs.append(pl.BlockSpec((1, NSA_KV_GROUPS, tm, HEAD_DIM), head))
        out_shape.append(jax.ShapeDtypeStruct((b, NSA_KV_GROUPS, t, HEAD_DIM), dt))
    return pl.pallas_call(
        _nsa_prep_kernel,
        grid=(b, nt),
        in_specs=in_specs,
        out_specs=out_specs,
        out_shape=out_shape,
        compiler_params=_cparams(("parallel", "parallel")),
        name="nsa_prep",
    )(pos128, invf, proj, proj, proj, proj, proj, proj, proj)


def _cmp_mlp_kernel(a_ref, pe_ref, w1_ref, b1_ref, w2_ref, pos_ref, invf_ref, o_ref):
    kind = pl.program_id(0)
    a = a_ref[0, 0]
    nc = a.shape[0]
    u = _dot((a + pe_ref[0, 0]).astype(bf16), w1_ref[0, 0])
    v = _dot((a + pe_ref[0, 1]).astype(bf16), w1_ref[0, 1])
    v_next = pltpu.roll(v, nc - 1, 0)
    hid = jax.nn.gelu(u + v_next + b1_ref[0])
    out = _dot(hid.astype(bf16), w2_ref[0])
    cos, sin = _rope_tables(pos_ref[0], invf_ref[...])
    roped = _rope128(out, cos, sin)
    out = jnp.where(kind == 0, roped, out)
    o_ref[0, 0] = out[:, :HEAD_DIM].astype(bf16)


def _cmp_mlp(a, pe, w1, b1, w2, posc, invf, b):
    _, bg, nc, hw = a.shape
    g = bg // b
    return pl.pallas_call(
        _cmp_mlp_kernel,
        grid=(2, bg),
        in_specs=[pl.BlockSpec((1, 1, nc, hw), lambda k, i: (k, i, 0, 0)),
                  pl.BlockSpec((1, 2, 1, hw), lambda k, i: (k, 0, 0, 0)),
                  pl.BlockSpec((1, 2, hw, CMP_HIDDEN), lambda k, i: (k, 0, 0, 0)),
                  pl.BlockSpec((1, 1, CMP_HIDDEN), lambda k, i: (k, 0, 0)),
                  pl.BlockSpec((1, CMP_HIDDEN, LANES), lambda k, i: (k, 0, 0)),
                  pl.BlockSpec((1, nc, LANES), lambda k, i: (i // g, 0, 0)),
                  pl.BlockSpec((1, LANES), lambda k, i: (0, 0))],
        out_specs=pl.BlockSpec((1, 1, nc, HEAD_DIM), lambda k, i: (k, i, 0, 0)),
        out_shape=jax.ShapeDtypeStruct((2, bg, nc, HEAD_DIM), bf16),
        compiler_params=_cparams(("parallel", "parallel")),
        name="cmp_mlp",
    )(a, pe, w1, b1, w2, posc, invf)


def _cmp_attn_kernel(q_ref, kc_ref, vc_ref, cover_ref, oc_ref, sel_ref):
    qi = pl.program_id(2)
    kc = kc_ref[0, 0]
    vc = vc_ref[0, 0]
    nc = kc.shape[0]
    tq = qi * ATT_TQ + lax.broadcasted_iota(i32, (ATT_TQ, 1), 0)
    cend = lax.broadcasted_iota(i32, (1, nc), 1) * CMP_STRIDE + (CMP_BLOCK - 1)
    mask = cend <= tq
    fmin = jnp.finfo(f32).min
    tiny = jnp.finfo(f32).tiny
    psum = jnp.zeros((ATT_TQ, nc), f32)
    for r in range(NSA_REP):
        s = jnp.where(mask, _dot_t(q_ref[0, r], kc), fmin)
        e = jnp.where(mask, jnp.exp(s - jnp.max(s, axis=-1, keepdims=True)), 0.0)
        p = e / jnp.maximum(jnp.sum(e, axis=-1, keepdims=True), tiny)
        oc_ref[0, r] = _dot(p.astype(bf16), vc)
        psum = psum + p
    hi = psum.astype(bf16)
    lo = (psum - hi.astype(f32)).astype(bf16)
    cover = cover_ref[...]
    imp = _dot(hi, cover) + _dot(lo, cover)

    j = lax.broadcasted_iota(i32, (ATT_TQ, LANES), 1)
    cur = tq // SEL_BLOCK
    forced = (j == 0) | (j == cur) | (j == cur - 1)
    valid = j * SEL_BLOCK <= tq
    imp = jnp.where(valid, jnp.where(forced, FORCED_SCORE, imp), -FORCED_SCORE)

    def pick(_, carry):
        work, sel = carry
        m = jnp.max(work, axis=-1, keepdims=True)
        first = jnp.min(jnp.where(work == m, j, LANES), axis=-1, keepdims=True)
        hit = j == first
        return jnp.where(hit, -jnp.inf, work), jnp.where(hit, 1.0, sel)

    _, sel = lax.fori_loop(0, SEL_TOPK, pick, (imp, jnp.zeros((ATT_TQ, LANES), f32)))
    sel_ref[0, 0] = jnp.where(valid, jnp.where(sel > 0.0, 0.0, MASK_NEG), MASK_NEG).astype(bf16)


def _cmp_attn(q_r, kvc, cover, b, t):
    g = NSA_KV_GROUPS
    nc = kvc.shape[2]
    nq = t // ATT_TQ
    return pl.pallas_call(
        _cmp_attn_kernel,
        grid=(b, g, nq),
        in_specs=[pl.BlockSpec((1, NSA_REP, ATT_TQ, HEAD_DIM), lambda bi, gi, qi: (bi, gi, qi, 0)),
                  pl.BlockSpec((1, 1, nc, HEAD_DIM), lambda bi, gi, qi: (0, bi * g + gi, 0, 0)),
                  pl.BlockSpec((1, 1, nc, HEAD_DIM), lambda bi, gi, qi: (1, bi * g + gi, 0, 0)),
                  pl.BlockSpec((nc, LANES), lambda bi, gi, qi: (0, 0))],
        out_specs=[pl.BlockSpec((1, NSA_REP, ATT_TQ, HEAD_DIM), lambda bi, gi, qi: (bi, gi, qi, 0)),
                   pl.BlockSpec((1, 1, ATT_TQ, LANES), lambda bi, gi, qi: (bi, gi, qi, 0))],
        out_shape=[jax.ShapeDtypeStruct((b, NSA_HEADS, t, HEAD_DIM), f32),
                   jax.ShapeDtypeStruct((b, g, t, LANES), bf16)],
        compiler_params=_cparams(("parallel", "parallel", "parallel")),
        name="cmp_attn",
    )(q_r, kvc, kvc, cover)


def _nsa_attn_kernel(q_ref, ks_ref, vs_ref, kw_ref, vw_ref, sel_ref, bmap_ref, oc_ref, gate_ref,
                     o_ref, m_ref, l_ref, acc_ref):
    gi = pl.program_id(1)
    qi = pl.program_id(2)
    rows = NSA_REP * ATT_TQ
    start = qi * ATT_TQ
    q = q_ref[0].reshape(rows, HEAD_DIM)
    tq = start + lax.broadcasted_iota(i32, (ATT_TQ, 1), 0)
    selb = sel_ref[0, 0]

    m_ref[...] = jnp.full((rows, 1), MASK_NEG, f32)
    l_ref[...] = jnp.zeros((rows, 1), f32)
    acc_ref[...] = jnp.zeros((rows, HEAD_DIM), f32)

    def tile(kt, _):
        k0 = pl.multiple_of(kt * ATT_TK, ATT_TK)
        k = ks_ref[0, 0, pl.ds(k0, ATT_TK), :]
        v = vs_ref[0, 0, pl.ds(k0, ATT_TK), :]
        bm = bmap_ref[pl.ds(k0, ATT_TK), :]
        bias = _dot_t(selb, bm)
        kpos = k0 + lax.broadcasted_iota(i32, (1, ATT_TK), 1)
        bias = jnp.where(kpos <= tq, bias, MASK_NEG)
        s = _dot_t(q, k).reshape(NSA_REP, ATT_TQ, ATT_TK) + bias[None]
        s = s.reshape(rows, ATT_TK)
        m_old = m_ref[...]
        m_new = jnp.maximum(m_old, jnp.max(s, axis=-1, keepdims=True))
        alpha = jnp.exp(m_old - m_new)
        p = jnp.exp(s - m_new)
        l_ref[...] = alpha * l_ref[...] + jnp.sum(p, axis=-1, keepdims=True)
        acc_ref[...] = alpha * acc_ref[...] + _dot(p.astype(bf16), v)
        m_ref[...] = m_new
        return 0

    n_tiles = (start + ATT_TQ + ATT_TK - 1) // ATT_TK
    lax.fori_loop(0, n_tiles, tile, 0)

    span = WINDOW + ATT_TQ
    w0 = pl.multiple_of(jnp.maximum(start - WINDOW, 0), ATT_TQ)
    kw = kw_ref[0, 0, pl.ds(w0, span), :]
    vw = vw_ref[0, 0, pl.ds(w0, span), :]
    kpos = w0 + lax.broadcasted_iota(i32, (1, span), 1)
    wmask = jnp.where(kpos <= tq, jnp.where(kpos > tq - WINDOW, 0.0, MASK_NEG), MASK_NEG)
    sw = _dot_t(q, kw).reshape(NSA_REP, ATT_TQ, span) + wmask[None]
    sw = sw.reshape(rows, span)
    pw = jnp.exp(sw - jnp.max(sw, axis=-1, keepdims=True))
    lw = jnp.sum(pw, axis=-1, keepdims=True)
    ow = _dot(pw.astype(bf16), vw) / lw
    osel = acc_ref[...] / l_ref[...]

    gates = jax.nn.sigmoid(gate_ref[...])
    for r in range(NSA_REP):
        def gate(br):
            out = None
            for gg in range(NSA_KV_GROUPS):
                c = (gg * NSA_REP + r) * 3 + br
                col = gates[:, c:c + 1]
                out = col if out is None else jnp.where(gi == gg, col, out)
            return out
        sl = slice(r * ATT_TQ, (r + 1) * ATT_TQ)
        y = gate(0) * oc_ref[0, r] + gate(1) * osel[sl] + gate(2) * ow[sl]
        o_ref[:, r * HEAD_DIM:(r + 1) * HEAD_DIM] = y


def _nsa_attn(q_r, ks, vs, kw, vw, selb, bmap, oc, proj, b, t):
    g = NSA_KV_GROUPS
    nq = t // ATT_TQ
    rows = NSA_REP * ATT_TQ
    kvspec = pl.BlockSpec((1, 1, t, HEAD_DIM), lambda bi, gi, qi: (bi, gi, 0, 0))
    hspec = pl.BlockSpec((1, NSA_REP, ATT_TQ, HEAD_DIM), lambda bi, gi, qi: (bi, gi, qi, 0))
    return pl.pallas_call(
        _nsa_attn_kernel,
        grid=(b, g, nq),
        in_specs=[hspec, kvspec, kvspec, kvspec, kvspec,
                  pl.BlockSpec((1, 1, ATT_TQ, LANES), lambda bi, gi, qi: (bi, gi, qi, 0)),
                  pl.BlockSpec((t, LANES), lambda bi, gi, qi: (0, 0)),
                  hspec,
                  pl.BlockSpec((ATT_TQ, LANES), lambda bi, gi, qi: (bi * nq + qi, COL_SMALL // LANES))],
        out_specs=pl.BlockSpec((ATT_TQ, NSA_REP * HEAD_DIM), lambda bi, gi, qi: (bi * nq + qi, gi)),
        out_shape=jax.ShapeDtypeStruct((b * t, NSA_WIDTH), f32),
        scratch_shapes=[pltpu.VMEM((rows, 1), f32), pltpu.VMEM((rows, 1), f32),
                        pltpu.VMEM((rows, HEAD_DIM), f32)],
        compiler_params=_cparams(("parallel", "parallel", "arbitrary")),
        name="nsa_attn",
    )(q_r, ks, vs, kw, vw, selb, bmap, oc, proj)


HALO = 8


def _ssd_kernel(xbc_ref, z_ref, small_ref, dtt_ref, cw_ref, cb_ref, dtb_r_ref, dtb_c_ref,
                alog_r_ref, alog_c_ref, dskip_ref, nw_ref, o_ref, ext_ref, st_ref):
    c = pl.program_id(1)
    L = CHUNK

    @pl.when(c == 0)
    def _():
        ext_ref[0:HALO, :] = jnp.zeros((HALO, XBC_WIDTH), f32)
        st_ref[...] = jnp.zeros_like(st_ref)

    ext_ref[HALO:HALO + L, :] = xbc_ref[...]
    conv = cb_ref[...]
    for k in range(CONV_WIDTH):
        off = HALO - (CONV_WIDTH - 1) + k
        conv = conv + cw_ref[k:k + 1, :] * ext_ref[off:off + L, :]
    ext_ref[0:HALO, :] = ext_ref[L:L + HALO, :]
    act = conv * jax.nn.sigmoid(conv)
    xs = act[:, :SSM_WIDTH]
    bm = act[:, SSM_WIDTH:SSM_WIDTH + SSM_GROUPS * SSM_STATE]
    cm = act[:, SSM_WIDTH + SSM_GROUPS * SSM_STATE:]

    dt_c = jax.nn.softplus(small_ref[:, SMALL_DT_OFF:SMALL_DT_OFF + SSM_HEADS] + dtb_r_ref[...])
    dt_r = jax.nn.softplus(dtt_ref[0] + dtb_c_ref[...])
    a_r = -jnp.exp(alog_r_ref[...])
    a_c = -jnp.exp(alog_c_ref[...])
    row = lax.broadcasted_iota(i32, (L, L), 0)
    col = lax.broadcasted_iota(i32, (L, L), 1)
    causal = col <= row
    tri = jnp.where(causal, 1.0, 0.0)
    acs_c = _dot_hi(tri, dt_c * a_r)
    acs_r = _dot_hi(dt_r * a_c, jnp.where(row <= col, 1.0, 0.0))

    z = z_ref[...]
    ys = []
    for g in range(SSM_GROUPS):
        cg = cm[:, g * SSM_STATE:(g + 1) * SSM_STATE].astype(bf16)
        bg = bm[:, g * SSM_STATE:(g + 1) * SSM_STATE]
        cb = _dot_t(cg, bg.astype(bf16))
        bg_t = bg.T
        ssq = jnp.zeros((L, 1), f32)
        yg = []
        for r in range(SSM_REP):
            h = g * SSM_REP + r
            hs = slice(h * SSM_HEADDIM, (h + 1) * SSM_HEADDIM)
            a_col = acs_c[:, h:h + 1]
            a_row = acs_r[h:h + 1, :]
            dt_row = dt_r[h:h + 1, :]
            a_last = acs_r[h:h + 1, L - 1:L]
            seg = a_col - a_row
            decay = jnp.where(causal, jnp.exp(jnp.where(causal, seg, 0.0)), 0.0)
            w = cb * decay * dt_row
            x_h = xs[:, hs]
            xb = x_h.astype(bf16)
            st = st_ref[h]
            y = _dot(w.astype(bf16), xb)
            y = y + _dot(cg, st.astype(bf16)) * jnp.exp(a_col)
            y = y + dskip_ref[:, hs] * x_h
            bscaled = bg_t * (jnp.exp(a_last - a_row) * dt_row)
            st_ref[h] = jnp.exp(a_last) * st + _dot(bscaled.astype(bf16), xb)
            zh = z[:, hs]
            y = y * (zh * jax.nn.sigmoid(zh))
            ssq = ssq + jnp.sum(y * y, axis=-1, keepdims=True)
            yg.append(y)
        rs = lax.rsqrt(ssq / (SSM_REP * SSM_HEADDIM) + NORM_EPS)
        for r in range(SSM_REP):
            h = g * SSM_REP + r
            hs = slice(h * SSM_HEADDIM, (h + 1) * SSM_HEADDIM)
            o_ref[:, hs] = yg[r] * rs * nw_ref[:, hs]


def _ssd(proj, dtt, cw, cb, dtb_r, dtb_c, alog_r, alog_c, dskip, nw, b, t):
    nch = t // CHUNK
    row = lambda bi, ci: bi * nch + ci
    const2 = lambda bi, ci: (0, 0)
    return pl.pallas_call(
        _ssd_kernel,
        grid=(b, nch),
        in_specs=[pl.BlockSpec((CHUNK, XBC_WIDTH), lambda bi, ci: (row(bi, ci), COL_XBC // XBC_WIDTH)),
                  pl.BlockSpec((CHUNK, SSM_WIDTH), lambda bi, ci: (row(bi, ci), COL_Z // SSM_WIDTH)),
                  pl.BlockSpec((CHUNK, LANES), lambda bi, ci: (row(bi, ci), COL_SMALL // LANES)),
                  pl.BlockSpec((1, SSM_HEADS, CHUNK), lambda bi, ci: (bi, 0, ci)),
                  pl.BlockSpec((CONV_WIDTH, XBC_WIDTH), const2),
                  pl.BlockSpec((1, XBC_WIDTH), const2),
                  pl.BlockSpec((1, SSM_HEADS), const2),
                  pl.BlockSpec((SSM_HEADS, 1), const2),
                  pl.BlockSpec((1, SSM_HEADS), const2),
                  pl.BlockSpec((SSM_HEADS, 1), const2),
                  pl.BlockSpec((1, SSM_WIDTH), const2),
                  pl.BlockSpec((1, SSM_WIDTH), const2)],
        out_specs=pl.BlockSpec((CHUNK, SSM_WIDTH), lambda bi, ci: (row(bi, ci), 0)),
        out_shape=jax.ShapeDtypeStruct((b * t, SSM_WIDTH), f32),
        scratch_shapes=[pltpu.VMEM((HALO + CHUNK, XBC_WIDTH), f32),
                        pltpu.VMEM((SSM_HEADS, SSM_STATE, SSM_HEADDIM), f32)],
        compiler_params=_cparams(("parallel", "arbitrary")),
        name="ssd",
    )(proj, proj, proj, dtt, cw, cb, dtb_r, dtb_c, alog_r, alog_c, dskip, nw)


def _layer_norm(v, g, b):
    mu = jnp.mean(v, axis=-1, keepdims=True)
    d = v - mu
    var = jnp.mean(d * d, axis=-1, keepdims=True)
    return d * lax.rsqrt(var + NORM_EPS) * g + b


def _outproj_kernel(alpha, ya_ref, yb_ref, x_ref, wa_ref, wb_ref, g_ref, b_ref, wr_ref, br_ref,
                    h_ref, route_ref, cnt_ref):
    i = pl.program_id(0)
    tm = x_ref.shape[0]
    mix = _dot(ya_ref[...].astype(bf16), wa_ref[...]) + _dot(yb_ref[...].astype(bf16), wb_ref[...])
    h = _layer_norm(alpha * x_ref[...] + mix, g_ref[...], b_ref[...])
    h_ref[...] = h

    logits = _dot_hi(h, wr_ref[...]) + br_ref[...]
    lane = lax.broadcasted_iota(i32, (tm, LANES), 1)
    ninf = -jnp.inf
    gmask = lane < N_EXPERT_GROUPS
    gl = jnp.where(gmask, logits, ninf)
    ge = jnp.where(gmask, jnp.exp(gl - jnp.max(gl, axis=-1, keepdims=True)), 0.0)
    pg = ge / jnp.sum(ge, axis=-1, keepdims=True)
    g_gate = jnp.max(pg, axis=-1, keepdims=True)
    g_sel = jnp.min(jnp.where(gmask & (pg == g_gate), lane, LANES), axis=-1, keepdims=True)
    lo = N_EXPERT_GROUPS + g_sel * EXPERTS_PER_GROUP
    emask = (lane >= lo) & (lane < lo + EXPERTS_PER_GROUP)
    el = jnp.where(emask, logits, ninf)
    ee = jnp.where(emask, jnp.exp(el - jnp.max(el, axis=-1, keepdims=True)), 0.0)
    pe = ee / jnp.sum(ee, axis=-1, keepdims=True)
    p0 = jnp.max(pe, axis=-1, keepdims=True)
    l0 = jnp.min(jnp.where(emask & (pe == p0), lane, LANES), axis=-1, keepdims=True)
    rest = jnp.where(emask & (lane != l0), pe, ninf)
    p1 = jnp.max(rest, axis=-1, keepdims=True)
    l1 = jnp.min(jnp.where(rest == p1, lane, LANES), axis=-1, keepdims=True)
    psum = p0 + p1
    w0 = g_gate * p0 / psum
    w1 = g_gate * p1 / psum
    e0 = l0 - N_EXPERT_GROUPS
    e1 = l1 - N_EXPERT_GROUPS

    @pl.when(i == 0)
    def _():
        cnt_ref[...] = jnp.zeros_like(cnt_ref)

    oh0 = lane == e0
    oh1 = lane == e1
    both = jnp.where(oh0, 1.0, 0.0) + jnp.where(oh1, 1.0, 0.0)
    r_i = lax.broadcasted_iota(i32, (tm, tm), 0)
    c_i = lax.broadcasted_iota(i32, (tm, tm), 1)
    strict = jnp.where(c_i < r_i, 1.0, 0.0).astype(bf16)
    before = _dot(strict, both.astype(bf16)) + cnt_ref[...]
    rank0 = jnp.sum(jnp.where(oh0, before, 0.0), axis=-1, keepdims=True)
    rank1 = jnp.sum(jnp.where(oh1, before, 0.0), axis=-1, keepdims=True)
    cnt_ref[...] = cnt_ref[...] + jnp.sum(both, axis=0, keepdims=True)

    out = jnp.where(lane == 0, e0.astype(f32), 0.0)
    out = jnp.where(lane == 1, e1.astype(f32), out)
    out = jnp.where(lane == 2, w0, out)
    out = jnp.where(lane == 3, w1, out)
    out = jnp.where(lane == 4, rank0, out)
    out = jnp.where(lane == 5, rank1, out)
    route_ref[...] = out


def _outproj(y_nsa, y_ssm, x2, wa, wb, g, bta, wr, br, alpha):
    n, d = x2.shape
    tm = OUT_TM
    const = lambda i: (0, 0)
    rowb = lambda i: (i, 0)
    return pl.pallas_call(
        functools.partial(_outproj_kernel, alpha),
        grid=(n // tm,),
        in_specs=[pl.BlockSpec((tm, NSA_WIDTH), rowb), pl.BlockSpec((tm, SSM_WIDTH), rowb),
                  pl.BlockSpec((tm, d), rowb),
                  pl.BlockSpec((NSA_WIDTH, d), const), pl.BlockSpec((SSM_WIDTH, d), const),
                  pl.BlockSpec((1, d), const), pl.BlockSpec((1, d), const),
                  pl.BlockSpec((d, LANES), const), pl.BlockSpec((1, LANES), const)],
        out_specs=[pl.BlockSpec((tm, d), rowb), pl.BlockSpec((tm, LANES), rowb),
                   pl.BlockSpec((1, LANES), const)],
        out_shape=[jax.ShapeDtypeStruct((n, d), f32), jax.ShapeDtypeStruct((n, LANES), f32),
                   jax.ShapeDtypeStruct((1, LANES), f32)],
        compiler_params=_cparams(("arbitrary",)),
        name="outproj",
    )(y_nsa, y_ssm, x2, wa, wb, g, bta, wr, br)


def _row_copy(src_ref, src_row, dst_ref, dst_row, sem):
    return pltpu.make_async_copy(src_ref.at[pl.ds(src_row, 1), :], dst_ref.at[pl.ds(dst_row, 1), :], sem)


def _dispatch_kernel(dest_ref, zflag_ref, h_ref, xs_ref, zero_ref, sem, zsem):
    tm = h_ref.shape[0]
    nb = zflag_ref.shape[0]

    @pl.when(pl.program_id(0) == 0)
    def _():
        zero_ref[...] = jnp.zeros_like(zero_ref)

        def zblock(i):
            rows = pl.ds(pl.multiple_of(i * MOE_TM, MOE_TM), MOE_TM)
            return pltpu.make_async_copy(zero_ref, xs_ref.at[rows, :], zsem)

        def zstart(i, _):
            @pl.when(zflag_ref[i] != 0)
            def _():
                zblock(i).start()
            return 0

        def zwait(i, _):
            @pl.when(zflag_ref[i] != 0)
            def _():
                zblock(i).wait()
            return 0

        lax.fori_loop(0, nb, zstart, 0)
        lax.fori_loop(0, nb, zwait, 0)

    def issue(r, _):
        for k in range(2):
            _row_copy(h_ref, r, xs_ref, dest_ref[0, 0, k * tm + r], sem).start()
        return 0

    lax.fori_loop(0, tm, issue, 0)

    def drain(r, _):
        for k in range(2):
            _row_copy(h_ref, 0, xs_ref, 0, sem).wait()
        return 0

    lax.fori_loop(0, tm, drain, 0)


def _dispatch(dest_t, zflag, h, p_rows):
    n, d = h.shape
    tm = DISP_TM
    return pl.pallas_call(
        _dispatch_kernel,
        grid=(n // tm,),
        in_specs=[pl.BlockSpec((1, 1, 2 * tm), lambda i: (i, 0, 0), memory_space=pltpu.SMEM),
                  pl.BlockSpec(memory_space=pltpu.SMEM),
                  pl.BlockSpec((tm, d), lambda i: (i, 0))],
        out_specs=pl.BlockSpec(memory_space=pl.ANY),
        out_shape=jax.ShapeDtypeStruct((p_rows, d), f32),
        scratch_shapes=[pltpu.VMEM((MOE_TM, d), f32), pltpu.SemaphoreType.DMA(()), pltpu.SemaphoreType.DMA(())],
        compiler_params=_cparams(("arbitrary",)),
        name="dispatch",
    )(dest_t, zflag, h)


def _experts_kernel(be_ref, nu_ref, xs_ref, wg_ref, wu_ref, wd_ref, y_ref, wgb_ref, wub_ref, wdb_ref):
    i = pl.program_id(0)
    e = be_ref[i]
    prev = be_ref[jnp.maximum(i - 1, 0)]

    @pl.when((i == 0) | (e != prev))
    def _():
        wgb_ref[...] = wg_ref[0].astype(bf16)
        wub_ref[...] = wu_ref[0].astype(bf16)
        wdb_ref[...] = wd_ref[0].astype(bf16)

    @pl.when(i < nu_ref[0])
    def _():
        xb = xs_ref[...].astype(bf16)
        gte = _dot(xb, wgb_ref[...])
        up = _dot(xb, wub_ref[...])
        act = gte * jax.nn.sigmoid(gte) * up
        y_ref[...] = _dot(act.astype(bf16), wdb_ref[...])

    @pl.when(i >= nu_ref[0])
    def _():
        y_ref[...] = jnp.zeros_like(y_ref)


def _experts(block_e, n_used, xs, w_gate, w_up, w_down):
    p_rows, d = xs.shape
    tm = MOE_TM
    nb = p_rows // tm
    de = w_gate.shape[-1]
    xmap = lambda i, be, nu: (jnp.minimum(i, nu[0] - 1), 0)
    return pl.pallas_call(
        _experts_kernel,
        grid_spec=pltpu.PrefetchScalarGridSpec(
            num_scalar_prefetch=2,
            grid=(nb,),
            in_specs=[pl.BlockSpec((tm, d), xmap),
                      pl.BlockSpec((1, d, de), lambda i, be, nu: (be[i], 0, 0)),
                      pl.BlockSpec((1, d, de), lambda i, be, nu: (be[i], 0, 0)),
                      pl.BlockSpec((1, de, d), lambda i, be, nu: (be[i], 0, 0))],
            out_specs=pl.BlockSpec((tm, d), lambda i, be, nu: (i, 0)),
            scratch_shapes=[pltpu.VMEM((d, de), bf16), pltpu.VMEM((d, de), bf16), pltpu.VMEM((de, d), bf16)],
        ),
        out_shape=jax.ShapeDtypeStruct((p_rows, d), f32),
        compiler_params=_cparams(("arbitrary",)),
        name="experts",
    )(block_e, n_used, xs, w_gate, w_up, w_down)


def _combine_kernel(alpha, dest_ref, route_ref, h_ref, g_ref, b_ref, y_ref, o_ref, buf_ref, sem):
    tm = h_ref.shape[0]

    def issue(r, _):
        for k in range(2):
            _row_copy(y_ref, dest_ref[0, 0, k * tm + r], buf_ref.at[k], r, sem).start()
        return 0

    lax.fori_loop(0, tm, issue, 0)

    def drain(r, _):
        for k in range(2):
            _row_copy(y_ref, 0, buf_ref.at[k], 0, sem).wait()
        return 0

    lax.fori_loop(0, tm, drain, 0)

    route = route_ref[...]
    ffn = route[:, 2:3] * buf_ref[0] + route[:, 3:4] * buf_ref[1]
    o_ref[...] = _layer_norm(alpha * h_ref[...] + ffn, g_ref[...], b_ref[...])


def _combine(dest_t, route, h, g, bta, y, alpha):
    n, d = h.shape
    tm = COMB_TM
    const = lambda i: (0, 0)
    return pl.pallas_call(
        functools.partial(_combine_kernel, alpha),
        grid=(n // tm,),
        in_specs=[pl.BlockSpec((1, 1, 2 * tm), lambda i: (i, 0, 0), memory_space=pltpu.SMEM),
                  pl.BlockSpec((tm, LANES), lambda i: (i, 0)),
                  pl.BlockSpec((tm, d), lambda i: (i, 0)),
                  pl.BlockSpec((1, d), const), pl.BlockSpec((1, d), const),
                  pl.BlockSpec(memory_space=pl.ANY)],
        out_specs=pl.BlockSpec((tm, d), lambda i: (i, 0)),
        out_shape=jax.ShapeDtypeStruct((n, d), f32),
        scratch_shapes=[pltpu.VMEM((2, tm, d), f32), pltpu.SemaphoreType.DMA(())],
        compiler_params=_cparams(("arbitrary",)),
        name="combine",
    )(dest_t, route, h, g, bta, y)


def _tile_dest(dest, tm):
    n = dest.shape[1]
    return dest.reshape(2, n // tm, tm).transpose(1, 0, 2).reshape(n // tm, 1, 2 * tm)


def _layer(x, positions, w_in, cmp_k_pe, cmp_k_w1, cmp_k_b1, cmp_k_w2, cmp_v_pe, cmp_v_w1, cmp_v_b1, cmp_v_w2,
           conv_w, conv_b, dt_bias, a_log, d_skip, ssm_norm_w, w_out, ln1_g, ln1_b,
           w_router_group, b_router_group, w_router_expert, b_router_expert, w_gate, w_up, w_down, ln2_g, ln2_b,
           alpha):
    b, t, d = x.shape
    n = b * t
    x2 = x.reshape(n, d)

    c0 = NSA_WIDTH
    c1 = c0 + 6 * KV_WIDTH
    c2 = c1 + 3 * NSA_HEADS
    c3 = c2 + SSM_WIDTH
    c4 = c3 + XBC_WIDTH
    w_small = jnp.concatenate([w_in[:, c1:c2], w_in[:, c4:], jnp.zeros((d, LANES - 3 * NSA_HEADS - SSM_HEADS), f32)], axis=1)
    w_cat = jnp.concatenate([w_in[:, :c0], w_in[:, c2:c3], w_in[:, c3:c4], w_in[:, c0:c1], w_small,
                             jnp.zeros((d, PROJ_COLS - COL_SMALL - LANES), f32)], axis=1).astype(bf16)
    proj = _proj(x2, w_cat)

    lane = np.arange(LANES) % HEAD_DIM
    inv_freq = ROPE_THETA ** (-jnp.arange(0, ROT_DIM, 2, dtype=f32) / ROT_DIM)
    invf = jnp.where(lane < ROT_DIM, inv_freq[lane % (ROT_DIM // 2)], 0.0).astype(f32)[None, :]
    pos128 = jnp.broadcast_to(positions.reshape(n, 1), (n, LANES))
    q_r, k_cmp, v_cmp, k_sel, v_sel, k_win, v_win = _nsa_prep(proj, pos128, invf, b, t)

    nc = t // CMP_STRIDE
    half_w = CMP_STRIDE * HEAD_DIM
    a = jnp.stack([k_cmp, v_cmp]).reshape(2, b * NSA_KV_GROUPS, nc, half_w)
    pe = jnp.stack([cmp_k_pe, cmp_v_pe]).reshape(2, 2, 1, half_w)
    w1 = jnp.stack([cmp_k_w1, cmp_v_w1]).reshape(2, 2, half_w, CMP_HIDDEN).astype(bf16)
    b1 = jnp.stack([cmp_k_b1, cmp_v_b1]).reshape(2, 1, CMP_HIDDEN)
    w2 = jnp.pad(jnp.stack([cmp_k_w2, cmp_v_w2]), ((0, 0), (0, 0), (0, LANES - HEAD_DIM))).astype(bf16)
    cend = jnp.minimum(jnp.arange(nc) * CMP_STRIDE + CMP_BLOCK - 1, t - 1)
    posc = jnp.broadcast_to(positions[:, cend][:, :, None], (b, nc, LANES))
    kvc = _cmp_mlp(a, pe, w1, b1, w2, posc, invf, b)

    c_start = np.arange(nc)[:, None] * CMP_STRIDE
    s_start = np.arange(LANES)[None, :] * SEL_BLOCK
    cover = ((c_start < s_start + SEL_BLOCK) & (c_start + CMP_BLOCK > s_start)
             & (np.arange(nc)[:, None] < nc - 1) & (np.arange(LANES)[None, :] < t // SEL_BLOCK))
    cover = jnp.asarray(cover, bf16)
    o_cmp, selb = _cmp_attn(q_r, kvc, cover, b, t)

    bmap = jnp.asarray(np.arange(t)[:, None] // SEL_BLOCK == np.arange(LANES)[None, :], bf16)
    y_nsa = _nsa_attn(q_r, k_sel, v_sel, k_win, v_win, selb, bmap, o_cmp, proj, b, t)

    dt_raw = proj[:, COL_SMALL + SMALL_DT_OFF:COL_SMALL + SMALL_DT_OFF + SSM_HEADS]
    dtt = dt_raw.reshape(b, t, SSM_HEADS).transpose(0, 2, 1)
    y_ssm = _ssd(proj, dtt, conv_w.reshape(CONV_WIDTH, XBC_WIDTH), conv_b.reshape(1, XBC_WIDTH),
                 dt_bias.reshape(1, SSM_HEADS), dt_bias.reshape(SSM_HEADS, 1),
                 a_log.reshape(1, SSM_HEADS), a_log.reshape(SSM_HEADS, 1),
                 jnp.repeat(d_skip, SSM_HEADDIM).reshape(1, SSM_WIDTH), ssm_norm_w.reshape(1, SSM_WIDTH), b, t)

    wr = jnp.concatenate([w_router_group, w_router_expert,
                          jnp.zeros((d, LANES - N_EXPERT_GROUPS - N_EXPERTS), f32)], axis=1)
    br = jnp.concatenate([b_router_group, b_router_expert,
                          jnp.zeros((LANES - N_EXPERT_GROUPS - N_EXPERTS,), f32)])[None, :]
    wo = w_out.astype(bf16)
    h, route, counts = _outproj(y_nsa, y_ssm, x2, wo[:NSA_WIDTH], wo[NSA_WIDTH:], ln1_g[None, :], ln1_b[None, :],
                                wr, br, alpha)

    cnt = counts[0, :N_EXPERTS].astype(i32)
    padded = (cnt + MOE_TM - 1) // MOE_TM * MOE_TM
    pad_ends = jnp.cumsum(padded)
    pad_starts = pad_ends - padded
    e01 = route[:, 0:2].astype(i32).T
    dest = pad_starts[e01] + route[:, 4:6].astype(i32).T
    p_rows = 2 * n + N_EXPERTS * MOE_TM
    nb = p_rows // MOE_TM
    block_e = jnp.minimum(jnp.sum(jnp.arange(nb, dtype=i32)[:, None] * MOE_TM >= pad_ends[None, :], axis=-1),
                          N_EXPERTS - 1).astype(i32)
    n_used = (pad_ends[-1] // MOE_TM).astype(i32).reshape(1)
    blk = jnp.arange(nb, dtype=i32)
    last_of_expert = jnp.any((blk[:, None] + 1) * MOE_TM == pad_ends[None, :], axis=-1)
    zflag = (last_of_expert | (blk >= n_used[0])).astype(i32)
    block_e = jnp.where(blk < n_used[0], block_e, block_e[jnp.maximum(n_used[0] - 1, 0)])

    xs = _dispatch(_tile_dest(dest, DISP_TM), zflag, h, p_rows)
    y = _experts(block_e, n_used, xs, w_gate, w_up, w_down)
    out = _combine(_tile_dest(dest, COMB_TM), route, h, ln2_g[None, :], ln2_b[None, :], y, alpha)
    return out.reshape(b, t, d)


def kernel(x, positions, w_in, cmp_k_pe, cmp_k_w1, cmp_k_b1, cmp_k_w2, cmp_v_pe, cmp_v_w1, cmp_v_b1, cmp_v_w2, conv_w, conv_b, dt_bias, a_log, d_skip, ssm_norm_w, w_out, ln1_g, ln1_b, w_router_group, b_router_group, w_router_expert, b_router_expert, w_gate, w_up, w_down, ln2_g, ln2_b):
    depth = w_in.shape[0]
    alpha = (2 * depth) ** 0.25
    params = (w_in, cmp_k_pe, cmp_k_w1, cmp_k_b1, cmp_k_w2, cmp_v_pe, cmp_v_w1, cmp_v_b1, cmp_v_w2, conv_w, conv_b,
              dt_bias, a_log, d_skip, ssm_norm_w, w_out, ln1_g, ln1_b, w_router_group, b_router_group,
              w_router_expert, b_router_expert, w_gate, w_up, w_down, ln2_g, ln2_b)
    for l in range(depth):
        x = _layer(x, positions, *[p[l] for p in params], alpha)
    return x
```

```python
import functools
import math

import jax
import jax.numpy as jnp
import numpy as np
from jax import lax
from jax.experimental import pallas as pl
from jax.experimental.pallas import tpu as pltpu

f32 = jnp.float32
bf16 = jnp.bfloat16
i32 = jnp.int32

HEAD_DIM = 64
NSA_HEADS = 16
NSA_KV_GROUPS = 2
NSA_REP = NSA_HEADS // NSA_KV_GROUPS
NSA_WIDTH = NSA_HEADS * HEAD_DIM
KV_WIDTH = NSA_KV_GROUPS * HEAD_DIM
CMP_BLOCK = 32
CMP_STRIDE = 16
CMP_HIDDEN = 256
SEL_BLOCK = 64
SEL_TOPK = 16
WINDOW = 512
FORCED_SCORE = 1.0e4
SSM_HEADDIM = 64
SSM_HEADS = 16
SSM_WIDTH = SSM_HEADS * SSM_HEADDIM
SSM_GROUPS = 4
SSM_REP = SSM_HEADS // SSM_GROUPS
SSM_STATE = 128
CONV_WIDTH = 4
CHUNK = 256
XBC_WIDTH = SSM_WIDTH + 2 * SSM_GROUPS * SSM_STATE
ROPE_THETA = 500000.0
ROT_DIM = HEAD_DIM // 4
N_EXPERT_GROUPS = 4
EXPERTS_PER_GROUP = 8
N_EXPERTS = N_EXPERT_GROUPS * EXPERTS_PER_GROUP
D_EXPERT = 512
NORM_EPS = 1e-5

LANES = 128
MASK_NEG = -1.0e30

PROJ_TM = 1024
PROJ_TN = 512
PREP_TM = 256
ATT_TQ = 128
ATT_TK = 256
KX_WIDTH = 2 * LANES
WIN_CHUNKS = (256, 256, 128)
OUT_TM = 256
MOE_TM = 256
DISP_TM = 256
COMB_TM = 128
VMEM_LIMIT = 56 * 1024 * 1024

COL_Q = 0
COL_Z = NSA_WIDTH
COL_XBC = COL_Z + SSM_WIDTH
COL_KV = COL_XBC + XBC_WIDTH
COL_SMALL = COL_KV + 6 * KV_WIDTH
PROJ_COLS = 5120
SMALL_DT_OFF = 3 * NSA_HEADS


def _cparams(sem, vmem=VMEM_LIMIT):
    return pltpu.CompilerParams(dimension_semantics=sem, vmem_limit_bytes=vmem)


def _dot(a, b):
    return jnp.dot(a, b, preferred_element_type=f32)


def _dot_t(a, b):
    return lax.dot_general(a, b, (((1,), (1,)), ((), ())), preferred_element_type=f32)


def _dot_hi(a, b):
    return jnp.dot(a, b, preferred_element_type=f32, precision=lax.Precision.HIGHEST)


def _proj_kernel(x_ref, w_ref, o_ref, xb_ref):
    @pl.when(pl.program_id(1) == 0)
    def _():
        xb_ref[...] = x_ref[...].astype(bf16)

    o_ref[...] = _dot(xb_ref[...], w_ref[...])


def _proj(x2, w_cat):
    n, d = x2.shape
    cols = w_cat.shape[1]
    tm = min(PROJ_TM, n)
    return pl.pallas_call(
        _proj_kernel,
        grid=(n // tm, cols // PROJ_TN),
        in_specs=[pl.BlockSpec((tm, d), lambda i, j: (i, 0)),
                  pl.BlockSpec((d, PROJ_TN), lambda i, j: (0, j))],
        out_specs=pl.BlockSpec((tm, PROJ_TN), lambda i, j: (i, j)),
        out_shape=jax.ShapeDtypeStruct((n, cols), f32),
        scratch_shapes=[pltpu.VMEM((tm, d), bf16)],
        compiler_params=_cparams(("parallel", "arbitrary")),
        name="proj",
    )(x2, w_cat)


def _rope_tables(pos_i32, invf):
    ang = pos_i32.astype(f32) * invf
    return jnp.cos(ang), jnp.sin(ang)


def _rope128(x, cos, sin):
    half = ROT_DIM // 2
    d = lax.broadcasted_iota(i32, x.shape, 1) % HEAD_DIM
    up = pltpu.roll(x, LANES - half, 1)
    dn = pltpu.roll(x, half, 1)
    rot = jnp.where(d < half, -up, dn)
    return x * cos + rot * sin


def _nsa_prep_kernel(pos_ref, invf_ref, q_ref, kc_ref, vc_ref, ks_ref, vs_ref, kw_ref, vw_ref,
                     qo_ref, kco_ref, vco_ref, kso_ref, vso_ref, kwo_ref, vwo_ref):
    cos, sin = _rope_tables(pos_ref[...], invf_ref[...])
    scale = HEAD_DIM ** -0.5
    for c in range(NSA_WIDTH // LANES):
        t = _rope128(q_ref[:, c * LANES:(c + 1) * LANES], cos, sin) * scale
        qo_ref[0, 2 * c] = t[:, :HEAD_DIM].astype(bf16)
        qo_ref[0, 2 * c + 1] = t[:, HEAD_DIM:].astype(bf16)

    def split(src, dst, rope, dt):
        t = src[...]
        if rope:
            t = _rope128(t, cos, sin)
        for g in range(NSA_KV_GROUPS):
            dst[0, g] = t[:, g * HEAD_DIM:(g + 1) * HEAD_DIM].astype(dt)

    split(kc_ref, kco_ref, False, f32)
    split(vc_ref, vco_ref, False, f32)

    tm = pos_ref.shape[0]

    def values_with_ones(src, dst):
        t = src[...]
        lane = lax.broadcasted_iota(i32, (tm, HEAD_DIM), 1)
        tail = jnp.where(lane == 0, 1.0, 0.0).astype(bf16)
        for g in range(NSA_KV_GROUPS):
            vg = t[:, g * HEAD_DIM:(g + 1) * HEAD_DIM].astype(bf16)
            dst[0, g] = jnp.concatenate([vg, tail], axis=1)

    values_with_ones(vs_ref, vso_ref)
    values_with_ones(vw_ref, vwo_ref)

    kw_t = _rope128(kw_ref[...], cos, sin).T
    ks_t = _rope128(ks_ref[...], cos, sin).T
    for g in range(NSA_KV_GROUPS):
        for c in range(tm // LANES):
            kwo_ref[0, g, c] = kw_t[g * HEAD_DIM:(g + 1) * HEAD_DIM, c * LANES:(c + 1) * LANES].astype(bf16)
    blk = lax.broadcasted_iota(i32, (LANES, tm), 0)
    tok = pl.program_id(1) * tm + lax.broadcasted_iota(i32, (LANES, tm), 1)
    onehot_t = jnp.where(tok // SEL_BLOCK == blk, 1.0, 0.0).astype(bf16)
    for g in range(NSA_KV_GROUPS):
        kso_ref[0, g, 0, 0:LANES, :] = onehot_t
        kg = ks_t[g * HEAD_DIM:(g + 1) * HEAD_DIM, :].astype(bf16)
        kso_ref[0, g, 0, LANES:2 * LANES, :] = jnp.concatenate([kg, jnp.zeros_like(kg)], axis=0)


def _nsa_prep(proj, pos128, invf, b, t):
    tm = PREP_TM
    assert tm == ATT_TK
    nt = t // tm
    row = lambda bi, ti: (bi * nt + ti, 0)
    kv0 = COL_KV // LANES
    in_specs = [pl.BlockSpec((tm, LANES), row),
                pl.BlockSpec((1, LANES), lambda bi, ti: (0, 0)),
                pl.BlockSpec((tm, NSA_WIDTH), lambda bi, ti: (bi * nt + ti, COL_Q // NSA_WIDTH))]
    for k in range(6):
        in_specs.append(pl.BlockSpec((tm, LANES), functools.partial(lambda bi, ti, k: (bi * nt + ti, kv0 + k), k=k)))
    head = lambda bi, ti: (bi, 0, ti, 0)
    tile5 = lambda bi, ti: (bi, 0, ti, 0, 0)
    g = NSA_KV_GROUPS
    out_specs = [pl.BlockSpec((1, NSA_HEADS, tm, HEAD_DIM), head),
                 pl.BlockSpec((1, g, tm, HEAD_DIM), head), pl.BlockSpec((1, g, tm, HEAD_DIM), head),
                 pl.BlockSpec((1, g, 1, KX_WIDTH, tm), tile5), pl.BlockSpec((1, g, tm, LANES), head),
                 pl.BlockSpec((1, g, tm // LANES, HEAD_DIM, LANES), tile5), pl.BlockSpec((1, g, tm, LANES), head)]
    out_shape = [jax.ShapeDtypeStruct((b, NSA_HEADS, t, HEAD_DIM), bf16),
                 jax.ShapeDtypeStruct((b, g, t, HEAD_DIM), f32), jax.ShapeDtypeStruct((b, g, t, HEAD_DIM), f32),
                 jax.ShapeDtypeStruct((b, g, nt, KX_WIDTH, tm), bf16), jax.ShapeDtypeStruct((b, g, t, LANES), bf16),
                 jax.ShapeDtypeStruct((b, g, t // LANES, HEAD_DIM, LANES), bf16),
                 jax.ShapeDtypeStruct((b, g, t, LANES), bf16)]
    return pl.pallas_call(
        _nsa_prep_kernel,
        grid=(b, nt),
        in_specs=in_specs,
        out_specs=out_specs,
        out_shape=out_shape,
        compiler_params=_cparams(("parallel", "parallel")),
        name="nsa_prep",
    )(pos128, invf, proj, proj, proj, proj, proj, proj, proj)


def _cmp_mlp_kernel(a_ref, pe_ref, w1_ref, b1_ref, w2_ref, pos_ref, invf_ref, o_ref):
    kind = pl.program_id(0)
    a = a_ref[0, 0]
    nc = a.shape[0]
    u = _dot((a + pe_ref[0, 0]).astype(bf16), w1_ref[0, 0])
    v = _dot((a + pe_ref[0, 1]).astype(bf16), w1_ref[0, 1])
    v_next = pltpu.roll(v, nc - 1, 0)
    hid = jax.nn.gelu(u + v_next + b1_ref[0])
    out = _dot(hid.astype(bf16), w2_ref[0])
    cos, sin = _rope_tables(pos_ref[0], invf_ref[...])
    roped = _rope128(out, cos, sin)
    out = jnp.where(kind == 0, roped, out)
    o_ref[0, 0] = out[:, :HEAD_DIM].astype(bf16)


def _cmp_mlp(a, pe, w1, b1, w2, posc, invf, b):
    _, bg, nc, hw = a.shape
    g = bg // b
    return pl.pallas_call(
        _cmp_mlp_kernel,
        grid=(2, bg),
        in_specs=[pl.BlockSpec((1, 1, nc, hw), lambda k, i: (k, i, 0, 0)),
                  pl.BlockSpec((1, 2, 1, hw), lambda k, i: (k, 0, 0, 0)),
                  pl.BlockSpec((1, 2, hw, CMP_HIDDEN), lambda k, i: (k, 0, 0, 0)),
                  pl.BlockSpec((1, 1, CMP_HIDDEN), lambda k, i: (k, 0, 0)),
                  pl.BlockSpec((1, CMP_HIDDEN, LANES), lambda k, i: (k, 0, 0)),
                  pl.BlockSpec((1, nc, LANES), lambda k, i: (i // g, 0, 0)),
                  pl.BlockSpec((1, LANES), lambda k, i: (0, 0))],
        out_specs=pl.BlockSpec((1, 1, nc, HEAD_DIM), lambda k, i: (k, i, 0, 0)),
        out_shape=jax.ShapeDtypeStruct((2, bg, nc, HEAD_DIM), bf16),
        compiler_params=_cparams(("parallel", "parallel")),
        name="cmp_mlp",
    )(a, pe, w1, b1, w2, posc, invf)


def _cmp_attn_kernel(q_ref, kc_ref, vc_ref, cover_ref, oc_ref, sel_ref):
    qi = pl.program_id(2)
    kc = kc_ref[0, 0]
    vc = vc_ref[0, 0]
    nc = kc.shape[0]
    tq = qi * ATT_TQ + lax.broadcasted_iota(i32, (ATT_TQ, 1), 0)
    cend = lax.broadcasted_iota(i32, (1, nc), 1) * CMP_STRIDE + (CMP_BLOCK - 1)
    mask = cend <= tq
    fmin = jnp.finfo(f32).min
    tiny = jnp.finfo(f32).tiny
    psum = jnp.zeros((ATT_TQ, nc), f32)
    for r in range(NSA_REP):
        s = jnp.where(mask, _dot_t(q_ref[0, r], kc), fmin)
        e = jnp.where(mask, jnp.exp(s - jnp.max(s, axis=-1, keepdims=True)), 0.0)
        p = e / jnp.maximum(jnp.sum(e, axis=-1, keepdims=True), tiny)
        oc_ref[0, r] = _dot(p.astype(bf16), vc)
        psum = psum + p
    hi = psum.astype(bf16)
    lo = (psum - hi.astype(f32)).astype(bf16)
    cover = cover_ref[...]
    imp = _dot(hi, cover) + _dot(lo, cover)

    j = lax.broadcasted_iota(i32, (ATT_TQ, LANES), 1)
    cur = tq // SEL_BLOCK
    forced = (j == 0) | (j == cur) | (j == cur - 1)
    valid = j * SEL_BLOCK <= tq
    imp = jnp.where(valid, jnp.where(forced, FORCED_SCORE, imp), -FORCED_SCORE)

    jt = lax.broadcasted_iota(i32, (LANES, ATT_TQ), 0)

    def pick(_, carry):
        work, sel = carry
        m = jnp.max(work, axis=0, keepdims=True)
        first = jnp.min(jnp.where(work == m, jt, LANES), axis=0, keepdims=True)
        hit = jt == first
        return jnp.where(hit, -jnp.inf, work), jnp.where(hit, 1.0, sel)

    _, sel_t = lax.fori_loop(0, SEL_TOPK, pick, (imp.T, jnp.zeros((LANES, ATT_TQ), f32)))
    sel_ref[0, 0] = jnp.where(valid, jnp.where(sel_t.T > 0.0, 0.0, MASK_NEG), MASK_NEG).astype(bf16)


def _cmp_attn(q_r, kvc, cover, b, t):
    g = NSA_KV_GROUPS
    nc = kvc.shape[2]
    nq = t // ATT_TQ
    return pl.pallas_call(
        _cmp_attn_kernel,
        grid=(b, g, nq),
        in_specs=[pl.BlockSpec((1, NSA_REP, ATT_TQ, HEAD_DIM), lambda bi, gi, qi: (bi, gi, qi, 0)),
                  pl.BlockSpec((1, 1, nc, HEAD_DIM), lambda bi, gi, qi: (0, bi * g + gi, 0, 0)),
                  pl.BlockSpec((1, 1, nc, HEAD_DIM), lambda bi, gi, qi: (1, bi * g + gi, 0, 0)),
                  pl.BlockSpec((nc, LANES), lambda bi, gi, qi: (0, 0))],
        out_specs=[pl.BlockSpec((1, NSA_REP, ATT_TQ, HEAD_DIM), lambda bi, gi, qi: (bi, gi, qi, 0)),
                   pl.BlockSpec((1, 1, ATT_TQ, LANES), lambda bi, gi, qi: (bi, gi, qi, 0))],
        out_shape=[jax.ShapeDtypeStruct((b, NSA_HEADS, t, HEAD_DIM), f32),
                   jax.ShapeDtypeStruct((b, g, t, LANES), bf16)],
        compiler_params=_cparams(("parallel", "parallel", "parallel")),
        name="cmp_attn",
    )(q_r, kvc, kvc, cover)


def _nsa_attn_kernel(q_ref, kx_ref, vs_ref, kw_ref, vw_ref, sel_ref, oc_ref, gate_ref,
                     o_ref, qx_ref, m_ref, acc_ref):
    gi = pl.program_id(1)
    qi = pl.program_id(2)
    start = qi * ATT_TQ
    tq = start + lax.broadcasted_iota(i32, (ATT_TQ, 1), 0)

    selb = sel_ref[0, 0]
    for r in range(NSA_REP):
        qx_ref[r, :, 0:LANES] = selb
        qr = q_ref[0, r]
        qx_ref[r, :, LANES:2 * LANES] = jnp.concatenate([qr, jnp.zeros_like(qr)], axis=1)

    m_ref[...] = jnp.full(m_ref.shape, MASK_NEG, f32)
    acc_ref[...] = jnp.zeros(acc_ref.shape, f32)

    def flash_update(br, r, s, v):
        m_prev = m_ref[br, r]
        m_new = jnp.maximum(m_prev, jnp.max(s, axis=-1, keepdims=True))
        alpha = jnp.exp(m_prev - m_new)
        p = jnp.exp(s - jnp.concatenate([m_new] * (s.shape[1] // LANES), axis=1))
        acc_ref[br, r] = alpha * acc_ref[br, r] + _dot(p.astype(bf16), v)
        m_ref[br, r] = m_new

    def sel_tile(kt, causal):
        k0 = pl.multiple_of(kt * ATT_TK, ATT_TK)
        kx = kx_ref[0, 0, kt]
        v = vs_ref[0, 0, pl.ds(k0, ATT_TK), :]
        if causal:
            cbias = jnp.where(k0 + lax.broadcasted_iota(i32, (1, ATT_TK), 1) <= tq, 0.0, MASK_NEG)
        for r in range(NSA_REP):
            s = _dot(qx_ref[r], kx)
            if causal:
                s = s + cbias
            flash_update(0, r, s, v)

    last = start // ATT_TK

    def full_tile(kt, _):
        sel_tile(kt, False)
        return 0

    lax.fori_loop(0, last, full_tile, 0)
    sel_tile(last, True)

    w0 = jnp.maximum(start - WINDOW, 0)
    off = 0
    for width in WIN_CHUNKS:
        c0 = pl.multiple_of(w0 + off, LANES)
        kw = jnp.concatenate([kw_ref[0, 0, c0 // LANES + i] for i in range(width // LANES)], axis=1)
        vw = vw_ref[0, 0, pl.ds(c0, width), :]
        kpos = c0 + lax.broadcasted_iota(i32, (1, width), 1)
        wbias = jnp.where(kpos <= tq, jnp.where(kpos > tq - WINDOW, 0.0, MASK_NEG), MASK_NEG)
        for r in range(NSA_REP):
            flash_update(1, r, _dot(q_ref[0, r], kw) + wbias, vw)
        off += width

    gates = jax.nn.sigmoid(gate_ref[...])
    for r in range(NSA_REP):
        def gate(br):
            out = None
            for gg in range(NSA_KV_GROUPS):
                c = (gg * NSA_REP + r) * 3 + br
                col = gates[:, c:c + 1]
                out = col if out is None else jnp.where(gi == gg, col, out)
            return out
        asel = acc_ref[0, r]
        awin = acc_ref[1, r]
        osel = asel[:, :HEAD_DIM] / asel[:, HEAD_DIM:HEAD_DIM + 1]
        owin = awin[:, :HEAD_DIM] / awin[:, HEAD_DIM:HEAD_DIM + 1]
        o_ref[:, r * HEAD_DIM:(r + 1) * HEAD_DIM] = gate(0) * oc_ref[0, r] + gate(1) * osel + gate(2) * owin


def _nsa_attn(q_r, kx, vs, kw, vw, selb, oc, proj, b, t):
    g = NSA_KV_GROUPS
    nq = t // ATT_TQ
    vspec = pl.BlockSpec((1, 1, t, LANES), lambda bi, gi, qi: (bi, gi, 0, 0))
    hspec = pl.BlockSpec((1, NSA_REP, ATT_TQ, HEAD_DIM), lambda bi, gi, qi: (bi, gi, qi, 0))
    return pl.pallas_call(
        _nsa_attn_kernel,
        grid=(b, g, nq),
        in_specs=[hspec,
                  pl.BlockSpec((1, 1, t // ATT_TK, KX_WIDTH, ATT_TK), lambda bi, gi, qi: (bi, gi, 0, 0, 0)),
                  vspec,
                  pl.BlockSpec((1, 1, t // LANES, HEAD_DIM, LANES), lambda bi, gi, qi: (bi, gi, 0, 0, 0)),
                  vspec,
                  pl.BlockSpec((1, 1, ATT_TQ, LANES), lambda bi, gi, qi: (bi, gi, qi, 0)),
                  hspec,
                  pl.BlockSpec((ATT_TQ, LANES), lambda bi, gi, qi: (bi * nq + qi, COL_SMALL // LANES))],
        out_specs=pl.BlockSpec((ATT_TQ, NSA_REP * HEAD_DIM), lambda bi, gi, qi: (bi * nq + qi, gi)),
        out_shape=jax.ShapeDtypeStruct((b * t, NSA_WIDTH), f32),
        scratch_shapes=[pltpu.VMEM((NSA_REP, ATT_TQ, KX_WIDTH), bf16),
                        pltpu.VMEM((2, NSA_REP, ATT_TQ, LANES), f32),
                        pltpu.VMEM((2, NSA_REP, ATT_TQ, LANES), f32)],
        compiler_params=_cparams(("parallel", "parallel", "arbitrary")),
        name="nsa_attn",
    )(q_r, kx, vs, kw, vw, selb, oc, proj)


HALO = 8


def _ssd_kernel(xbc_ref, z_ref, small_ref, dtt_ref, cw_ref, cb_ref, dtb_r_ref, dtb_c_ref,
                alog_r_ref, alog_c_ref, dskip_ref, nw_ref, o_ref, ext_ref, st_ref):
    c = pl.program_id(1)
    L = CHUNK

    @pl.when(c == 0)
    def _():
        ext_ref[0:HALO, :] = jnp.zeros((HALO, XBC_WIDTH), f32)
        st_ref[...] = jnp.zeros_like(st_ref)

    ext_ref[HALO:HALO + L, :] = xbc_ref[...]
    conv = cb_ref[...]
    for k in range(CONV_WIDTH):
        off = HALO - (CONV_WIDTH - 1) + k
        conv = conv + cw_ref[k:k + 1, :] * ext_ref[off:off + L, :]
    ext_ref[0:HALO, :] = ext_ref[L:L + HALO, :]
    act = conv * jax.nn.sigmoid(conv)
    xs = act[:, :SSM_WIDTH]
    bm = act[:, SSM_WIDTH:SSM_WIDTH + SSM_GROUPS * SSM_STATE]
    cm = act[:, SSM_WIDTH + SSM_GROUPS * SSM_STATE:]

    dt_c = jax.nn.softplus(small_ref[:, SMALL_DT_OFF:SMALL_DT_OFF + SSM_HEADS] + dtb_r_ref[...])
    dt_r = jax.nn.softplus(dtt_ref[0] + dtb_c_ref[...])
    a_r = -jnp.exp(alog_r_ref[...])
    a_c = -jnp.exp(alog_c_ref[...])
    row = lax.broadcasted_iota(i32, (L, L), 0)
    col = lax.broadcasted_iota(i32, (L, L), 1)
    causal = col <= row
    tri = jnp.where(causal, 1.0, 0.0)
    acs_c = _dot_hi(tri, dt_c * a_r)
    acs_r = _dot_hi(dt_r * a_c, jnp.where(row <= col, 1.0, 0.0))

    z = z_ref[...]
    ys = []
    for g in range(SSM_GROUPS):
        cg = cm[:, g * SSM_STATE:(g + 1) * SSM_STATE].astype(bf16)
        bg = bm[:, g * SSM_STATE:(g + 1) * SSM_STATE]
        cb = _dot_t(cg, bg.astype(bf16))
        bg_t = bg.T
        ssq = jnp.zeros((L, 1), f32)
        yg = []
        for r in range(SSM_REP):
            h = g * SSM_REP + r
            hs = slice(h * SSM_HEADDIM, (h + 1) * SSM_HEADDIM)
            a_col = acs_c[:, h:h + 1]
            a_row = acs_r[h:h + 1, :]
            dt_row = dt_r[h:h + 1, :]
            a_last = acs_r[h:h + 1, L - 1:L]
            seg = a_col - a_row
            decay = jnp.where(causal, jnp.exp(jnp.where(causal, seg, 0.0)), 0.0)
            w = cb * decay * dt_row
            x_h = xs[:, hs]
            xb = x_h.astype(bf16)
            st = st_ref[h]
            y = _dot(w.astype(bf16), xb)
            y = y + _dot(cg, st.astype(bf16)) * jnp.exp(a_col)
            y = y + dskip_ref[:, hs] * x_h
            bscaled = bg_t * (jnp.exp(a_last - a_row) * dt_row)
            st_ref[h] = jnp.exp(a_last) * st + _dot(bscaled.astype(bf16), xb)
            zh = z[:, hs]
            y = y * (zh * jax.nn.sigmoid(zh))
            ssq = ssq + jnp.sum(y * y, axis=-1, keepdims=True)
            yg.append(y)
        rs = lax.rsqrt(ssq / (SSM_REP * SSM_HEADDIM) + NORM_EPS)
        for r in range(SSM_REP):
            h = g * SSM_REP + r
            hs = slice(h * SSM_HEADDIM, (h + 1) * SSM_HEADDIM)
            o_ref[:, hs] = yg[r] * rs * nw_ref[:, hs]


def _ssd(proj, dtt, cw, cb, dtb_r, dtb_c, alog_r, alog_c, dskip, nw, b, t):
    nch = t // CHUNK
    row = lambda bi, ci: bi * nch + ci
    const2 = lambda bi, ci: (0, 0)
    return pl.pallas_call(
        _ssd_kernel,
        grid=(b, nch),
        in_specs=[pl.BlockSpec((CHUNK, XBC_WIDTH), lambda bi, ci: (row(bi, ci), COL_XBC // XBC_WIDTH)),
                  pl.BlockSpec((CHUNK, SSM_WIDTH), lambda bi, ci: (row(bi, ci), COL_Z // SSM_WIDTH)),
                  pl.BlockSpec((CHUNK, LANES), lambda bi, ci: (row(bi, ci), COL_SMALL // LANES)),
                  pl.BlockSpec((1, SSM_HEADS, CHUNK), lambda bi, ci: (bi, 0, ci)),
                  pl.BlockSpec((CONV_WIDTH, XBC_WIDTH), const2),
                  pl.BlockSpec((1, XBC_WIDTH), const2),
                  pl.BlockSpec((1, SSM_HEADS), const2),
                  pl.BlockSpec((SSM_HEADS, 1), const2),
                  pl.BlockSpec((1, SSM_HEADS), const2),
                  pl.BlockSpec((SSM_HEADS, 1), const2),
                  pl.BlockSpec((1, SSM_WIDTH), const2),
                  pl.BlockSpec((1, SSM_WIDTH), const2)],
        out_specs=pl.BlockSpec((CHUNK, SSM_WIDTH), lambda bi, ci: (row(bi, ci), 0)),
        out_shape=jax.ShapeDtypeStruct((b * t, SSM_WIDTH), f32),
        scratch_shapes=[pltpu.VMEM((HALO + CHUNK, XBC_WIDTH), f32),
                        pltpu.VMEM((SSM_HEADS, SSM_STATE, SSM_HEADDIM), f32)],
        compiler_params=_cparams(("parallel", "arbitrary")),
        name="ssd",
    )(proj, proj, proj, dtt, cw, cb, dtb_r, dtb_c, alog_r, alog_c, dskip, nw)


def _layer_norm(v, g, b):
    mu = jnp.mean(v, axis=-1, keepdims=True)
    d = v - mu
    var = jnp.mean(d * d, axis=-1, keepdims=True)
    return d * lax.rsqrt(var + NORM_EPS) * g + b


def _outproj_kernel(alpha, ya_ref, yb_ref, x_ref, wa_ref, wb_ref, g_ref, b_ref, wr_ref, br_ref,
                    h_ref, route_ref, cnt_ref):
    i = pl.program_id(0)
    tm = x_ref.shape[0]
    mix = _dot(ya_ref[...].astype(bf16), wa_ref[...]) + _dot(yb_ref[...].astype(bf16), wb_ref[...])
    h = _layer_norm(alpha * x_ref[...] + mix, g_ref[...], b_ref[...])
    h_ref[...] = h

    logits = _dot_hi(h, wr_ref[...]) + br_ref[...]
    lane = lax.broadcasted_iota(i32, (tm, LANES), 1)
    ninf = -jnp.inf
    gmask = lane < N_EXPERT_GROUPS
    gl = jnp.where(gmask, logits, ninf)
    ge = jnp.where(gmask, jnp.exp(gl - jnp.max(gl, axis=-1, keepdims=True)), 0.0)
    pg = ge / jnp.sum(ge, axis=-1, keepdims=True)
    g_gate = jnp.max(pg, axis=-1, keepdims=True)
    g_sel = jnp.min(jnp.where(gmask & (pg == g_gate), lane, LANES), axis=-1, keepdims=True)
    lo = N_EXPERT_GROUPS + g_sel * EXPERTS_PER_GROUP
    emask = (lane >= lo) & (lane < lo + EXPERTS_PER_GROUP)
    el = jnp.where(emask, logits, ninf)
    ee = jnp.where(emask, jnp.exp(el - jnp.max(el, axis=-1, keepdims=True)), 0.0)
    pe = ee / jnp.sum(ee, axis=-1, keepdims=True)
    p0 = jnp.max(pe, axis=-1, keepdims=True)
    l0 = jnp.min(jnp.where(emask & (pe == p0), lane, LANES), axis=-1, keepdims=True)
    rest = jnp.where(emask & (lane != l0), pe, ninf)
    p1 = jnp.max(rest, axis=-1, keepdims=True)
    l1 = jnp.min(jnp.where(rest == p1, lane, LANES), axis=-1, keepdims=True)
    psum = p0 + p1
    w0 = g_gate * p0 / psum
    w1 = g_gate * p1 / psum
    e0 = l0 - N_EXPERT_GROUPS
    e1 = l1 - N_EXPERT_GROUPS

    @pl.when(i == 0)
    def _():
        cnt_ref[...] = jnp.zeros_like(cnt_ref)

    oh0 = lane == e0
    oh1 = lane == e1
    both = jnp.where(oh0, 1.0, 0.0) + jnp.where(oh1, 1.0, 0.0)
    r_i = lax.broadcasted_iota(i32, (tm, tm), 0)
    c_i = lax.broadcasted_iota(i32, (tm, tm), 1)
    strict = jnp.where(c_i < r_i, 1.0, 0.0).astype(bf16)
    before = _dot(strict, both.astype(bf16)) + cnt_ref[...]
    rank0 = jnp.sum(jnp.where(oh0, before, 0.0), axis=-1, keepdims=True)
    rank1 = jnp.sum(jnp.where(oh1, before, 0.0), axis=-1, keepdims=True)
    cnt_ref[...] = cnt_ref[...] + jnp.sum(both, axis=0, keepdims=True)

    out = jnp.where(lane == 0, e0.astype(f32), 0.0)
    out = jnp.where(lane == 1, e1.astype(f32), out)
    out = jnp.where(lane == 2, w0, out)
    out = jnp.where(lane == 3, w1, out)
    out = jnp.where(lane == 4, rank0, out)
    out = jnp.where(lane == 5, rank1, out)
    route_ref[...] = out


def _outproj(y_nsa, y_ssm, x2, wa, wb, g, bta, wr, br, alpha):
    n, d = x2.shape
    tm = OUT_TM
    const = lambda i: (0, 0)
    rowb = lambda i: (i, 0)
    return pl.pallas_call(
        functools.partial(_outproj_kernel, alpha),
        grid=(n // tm,),
        in_specs=[pl.BlockSpec((tm, NSA_WIDTH), rowb), pl.BlockSpec((tm, SSM_WIDTH), rowb),
                  pl.BlockSpec((tm, d), rowb),
                  pl.BlockSpec((NSA_WIDTH, d), const), pl.BlockSpec((SSM_WIDTH, d), const),
                  pl.BlockSpec((1, d), const), pl.BlockSpec((1, d), const),
                  pl.BlockSpec((d, LANES), const), pl.BlockSpec((1, LANES), const)],
        out_specs=[pl.BlockSpec((tm, d), rowb), pl.BlockSpec((tm, LANES), rowb),
                   pl.BlockSpec((1, LANES), const)],
        out_shape=[jax.ShapeDtypeStruct((n, d), f32), jax.ShapeDtypeStruct((n, LANES), f32),
                   jax.ShapeDtypeStruct((1, LANES), f32)],
        compiler_params=_cparams(("arbitrary",)),
        name="outproj",
    )(y_nsa, y_ssm, x2, wa, wb, g, bta, wr, br)


def _row_copy(src_ref, src_row, dst_ref, dst_row, sem):
    return pltpu.make_async_copy(src_ref.at[pl.ds(src_row, 1), :], dst_ref.at[pl.ds(dst_row, 1), :], sem)


def _dest_row(idx_ref, ps_ref, tm, k, r):
    return ps_ref[idx_ref[0, 0, k * tm + r]] + idx_ref[0, 0, (2 + k) * tm + r]


def _dispatch_kernel(idx_ref, ps_ref, zflag_ref, h_ref, xs_ref, zero_ref, sem, zsem):
    tm = h_ref.shape[0]
    nb = zflag_ref.shape[0]

    @pl.when(pl.program_id(0) == 0)
    def _():
        zero_ref[...] = jnp.zeros_like(zero_ref)

        def zblock(i):
            rows = pl.ds(pl.multiple_of(i * MOE_TM, MOE_TM), MOE_TM)
            return pltpu.make_async_copy(zero_ref, xs_ref.at[rows, :], zsem)

        def zstart(i, _):
            @pl.when(zflag_ref[i] != 0)
            def _():
                zblock(i).start()
            return 0

        def zwait(i, _):
            @pl.when(zflag_ref[i] != 0)
            def _():
                zblock(i).wait()
            return 0

        lax.fori_loop(0, nb, zstart, 0)
        lax.fori_loop(0, nb, zwait, 0)

    def issue(r, _):
        for k in range(2):
            _row_copy(h_ref, r, xs_ref, _dest_row(idx_ref, ps_ref, tm, k, r), sem).start()
        return 0

    lax.fori_loop(0, tm, issue, 0)

    def drain(r, _):
        for k in range(2):
            _row_copy(h_ref, 0, xs_ref, 0, sem).wait()
        return 0

    lax.fori_loop(0, tm, drain, 0)


def _dispatch(idx_t, pad_starts, zflag, h, p_rows):
    n, d = h.shape
    tm = DISP_TM
    return pl.pallas_call(
        _dispatch_kernel,
        grid=(n // tm,),
        in_specs=[pl.BlockSpec((1, 1, 4 * tm), lambda i: (i, 0, 0), memory_space=pltpu.SMEM),
                  pl.BlockSpec(memory_space=pltpu.SMEM),
                  pl.BlockSpec(memory_space=pltpu.SMEM),
                  pl.BlockSpec((tm, d), lambda i: (i, 0))],
        out_specs=pl.BlockSpec(memory_space=pl.ANY),
        out_shape=jax.ShapeDtypeStruct((p_rows, d), f32),
        scratch_shapes=[pltpu.VMEM((MOE_TM, d), f32), pltpu.SemaphoreType.DMA(()), pltpu.SemaphoreType.DMA(())],
        compiler_params=_cparams(("arbitrary",)),
        name="dispatch",
    )(idx_t, pad_starts, zflag, h)


def _experts_kernel(be_ref, nu_ref, xs_ref, wg_ref, wu_ref, wd_ref, y_ref, wgb_ref, wub_ref, wdb_ref):
    i = pl.program_id(0)
    e = be_ref[i]
    prev = be_ref[jnp.maximum(i - 1, 0)]

    @pl.when((i == 0) | (e != prev))
    def _():
        wgb_ref[...] = wg_ref[0].astype(bf16)
        wub_ref[...] = wu_ref[0].astype(bf16)
        wdb_ref[...] = wd_ref[0].astype(bf16)

    @pl.when(i < nu_ref[0])
    def _():
        xb = xs_ref[...].astype(bf16)
        gte = _dot(xb, wgb_ref[...])
        up = _dot(xb, wub_ref[...])
        act = gte * jax.nn.sigmoid(gte) * up
        y_ref[...] = _dot(act.astype(bf16), wdb_ref[...])

    @pl.when(i >= nu_ref[0])
    def _():
        y_ref[...] = jnp.zeros_like(y_ref)


def _experts(block_e, n_used, xs, w_gate, w_up, w_down):
    p_rows, d = xs.shape
    tm = MOE_TM
    nb = p_rows // tm
    de = w_gate.shape[-1]
    xmap = lambda i, be, nu: (jnp.minimum(i, nu[0] - 1), 0)
    return pl.pallas_call(
        _experts_kernel,
        grid_spec=pltpu.PrefetchScalarGridSpec(
            num_scalar_prefetch=2,
            grid=(nb,),
            in_specs=[pl.BlockSpec((tm, d), xmap),
                      pl.BlockSpec((1, d, de), lambda i, be, nu: (be[i], 0, 0)),
                      pl.BlockSpec((1, d, de), lambda i, be, nu: (be[i], 0, 0)),
                      pl.BlockSpec((1, de, d), lambda i, be, nu: (be[i], 0, 0))],
            out_specs=pl.BlockSpec((tm, d), lambda i, be, nu: (i, 0)),
            scratch_shapes=[pltpu.VMEM((d, de), bf16), pltpu.VMEM((d, de), bf16), pltpu.VMEM((de, d), bf16)],
        ),
        out_shape=jax.ShapeDtypeStruct((p_rows, d), f32),
        compiler_params=_cparams(("arbitrary",)),
        name="experts",
    )(block_e, n_used, xs, w_gate, w_up, w_down)


def _combine_kernel(alpha, idx_ref, ps_ref, route_ref, h_ref, g_ref, b_ref, y_ref, o_ref, buf_ref, sem):
    tm = h_ref.shape[0]

    def issue(r, _):
        for k in range(2):
            _row_copy(y_ref, _dest_row(idx_ref, ps_ref, tm, k, r), buf_ref.at[k], r, sem).start()
        return 0

    lax.fori_loop(0, tm, issue, 0)

    def drain(r, _):
        for k in range(2):
            _row_copy(y_ref, 0, buf_ref.at[k], 0, sem).wait()
        return 0

    lax.fori_loop(0, tm, drain, 0)

    route = route_ref[...]
    ffn = route[:, 2:3] * buf_ref[0] + route[:, 3:4] * buf_ref[1]
    o_ref[...] = _layer_norm(alpha * h_ref[...] + ffn, g_ref[...], b_ref[...])


def _combine(idx_t, pad_starts, route, h, g, bta, y, alpha):
    n, d = h.shape
    tm = COMB_TM
    const = lambda i: (0, 0)
    return pl.pallas_call(
        functools.partial(_combine_kernel, alpha),
        grid=(n // tm,),
        in_specs=[pl.BlockSpec((1, 1, 4 * tm), lambda i: (i, 0, 0), memory_space=pltpu.SMEM),
                  pl.BlockSpec(memory_space=pltpu.SMEM),
                  pl.BlockSpec((tm, LANES), lambda i: (i, 0)),
                  pl.BlockSpec((tm, d), lambda i: (i, 0)),
                  pl.BlockSpec((1, d), const), pl.BlockSpec((1, d), const),
                  pl.BlockSpec(memory_space=pl.ANY)],
        out_specs=pl.BlockSpec((tm, d), lambda i: (i, 0)),
        out_shape=jax.ShapeDtypeStruct((n, d), f32),
        scratch_shapes=[pltpu.VMEM((2, tm, d), f32), pltpu.SemaphoreType.DMA(())],
        compiler_params=_cparams(("arbitrary",)),
        name="combine",
    )(idx_t, pad_starts, route, h, g, bta, y)


def _tile_idx(idx, tm):
    n = idx.shape[1]
    return idx.reshape(4, n // tm, tm).transpose(1, 0, 2).reshape(n // tm, 1, 4 * tm)


def _layer(x, positions, w_in, cmp_k_pe, cmp_k_w1, cmp_k_b1, cmp_k_w2, cmp_v_pe, cmp_v_w1, cmp_v_b1, cmp_v_w2,
           conv_w, conv_b, dt_bias, a_log, d_skip, ssm_norm_w, w_out, ln1_g, ln1_b,
           w_router_group, b_router_group, w_router_expert, b_router_expert, w_gate, w_up, w_down, ln2_g, ln2_b,
           alpha):
    b, t, d = x.shape
    n = b * t
    x2 = x.reshape(n, d)

    c0 = NSA_WIDTH
    c1 = c0 + 6 * KV_WIDTH
    c2 = c1 + 3 * NSA_HEADS
    c3 = c2 + SSM_WIDTH
    c4 = c3 + XBC_WIDTH
    w_small = jnp.concatenate([w_in[:, c1:c2], w_in[:, c4:], jnp.zeros((d, LANES - 3 * NSA_HEADS - SSM_HEADS), f32)], axis=1)
    w_cat = jnp.concatenate([w_in[:, :c0], w_in[:, c2:c3], w_in[:, c3:c4], w_in[:, c0:c1], w_small,
                             jnp.zeros((d, PROJ_COLS - COL_SMALL - LANES), f32)], axis=1).astype(bf16)
    proj = _proj(x2, w_cat)

    lane = np.arange(LANES) % HEAD_DIM
    inv_freq = ROPE_THETA ** (-jnp.arange(0, ROT_DIM, 2, dtype=f32) / ROT_DIM)
    invf = jnp.where(lane < ROT_DIM, inv_freq[lane % (ROT_DIM // 2)], 0.0).astype(f32)[None, :]
    pos128 = jnp.broadcast_to(positions.reshape(n, 1), (n, LANES))
    q_r, k_cmp, v_cmp, k_sel, v_sel, k_win, v_win = _nsa_prep(proj, pos128, invf, b, t)

    nc = t // CMP_STRIDE
    half_w = CMP_STRIDE * HEAD_DIM
    a = jnp.stack([k_cmp, v_cmp]).reshape(2, b * NSA_KV_GROUPS, nc, half_w)
    pe = jnp.stack([cmp_k_pe, cmp_v_pe]).reshape(2, 2, 1, half_w)
    w1 = jnp.stack([cmp_k_w1, cmp_v_w1]).reshape(2, 2, half_w, CMP_HIDDEN).astype(bf16)
    b1 = jnp.stack([cmp_k_b1, cmp_v_b1]).reshape(2, 1, CMP_HIDDEN)
    w2 = jnp.pad(jnp.stack([cmp_k_w2, cmp_v_w2]), ((0, 0), (0, 0), (0, LANES - HEAD_DIM))).astype(bf16)
    cend = jnp.minimum(jnp.arange(nc) * CMP_STRIDE + CMP_BLOCK - 1, t - 1)
    posc = jnp.broadcast_to(positions[:, cend][:, :, None], (b, nc, LANES))
    kvc = _cmp_mlp(a, pe, w1, b1, w2, posc, invf, b)

    c_start = np.arange(nc)[:, None] * CMP_STRIDE
    s_start = np.arange(LANES)[None, :] * SEL_BLOCK
    cover = ((c_start < s_start + SEL_BLOCK) & (c_start + CMP_BLOCK > s_start)
             & (np.arange(nc)[:, None] < nc - 1) & (np.arange(LANES)[None, :] < t // SEL_BLOCK))
    cover = jnp.asarray(cover, bf16)
    o_cmp, selb = _cmp_attn(q_r, kvc, cover, b, t)

    y_nsa = _nsa_attn(q_r, k_sel, v_sel, k_win, v_win, selb, o_cmp, proj, b, t)

    dt_raw = proj[:, COL_SMALL + SMALL_DT_OFF:COL_SMALL + SMALL_DT_OFF + SSM_HEADS]
    dtt = dt_raw.reshape(b, t, SSM_HEADS).transpose(0, 2, 1)
    y_ssm = _ssd(proj, dtt, conv_w.reshape(CONV_WIDTH, XBC_WIDTH), conv_b.reshape(1, XBC_WIDTH),
                 dt_bias.reshape(1, SSM_HEADS), dt_bias.reshape(SSM_HEADS, 1),
                 a_log.reshape(1, SSM_HEADS), a_log.reshape(SSM_HEADS, 1),
                 jnp.repeat(d_skip, SSM_HEADDIM).reshape(1, SSM_WIDTH), ssm_norm_w.reshape(1, SSM_WIDTH), b, t)

    wr = jnp.concatenate([w_router_group, w_router_expert,
                          jnp.zeros((d, LANES - N_EXPERT_GROUPS - N_EXPERTS), f32)], axis=1)
    br = jnp.concatenate([b_router_group, b_router_expert,
                          jnp.zeros((LANES - N_EXPERT_GROUPS - N_EXPERTS,), f32)])[None, :]
    wo = w_out.astype(bf16)
    h, route, counts = _outproj(y_nsa, y_ssm, x2, wo[:NSA_WIDTH], wo[NSA_WIDTH:], ln1_g[None, :], ln1_b[None, :],
                                wr, br, alpha)

    cnt = counts[0, :N_EXPERTS].astype(i32)
    padded = (cnt + MOE_TM - 1) // MOE_TM * MOE_TM
    pad_ends = jnp.cumsum(padded)
    pad_starts = pad_ends - padded
    idx = jnp.concatenate([route[:, 0:2], route[:, 4:6]], axis=1).astype(i32).T
    p_rows = 2 * n + N_EXPERTS * MOE_TM
    nb = p_rows // MOE_TM
    block_e = jnp.minimum(jnp.sum(jnp.arange(nb, dtype=i32)[:, None] * MOE_TM >= pad_ends[None, :], axis=-1),
                          N_EXPERTS - 1).astype(i32)
    n_used = (pad_ends[-1] // MOE_TM).astype(i32).reshape(1)
    blk = jnp.arange(nb, dtype=i32)
    last_of_expert = jnp.any((blk[:, None] + 1) * MOE_TM == pad_ends[None, :], axis=-1)
    zflag = (last_of_expert | (blk >= n_used[0])).astype(i32)
    block_e = jnp.where(blk < n_used[0], block_e, block_e[jnp.maximum(n_used[0] - 1, 0)])

    xs = _dispatch(_tile_idx(idx, DISP_TM), pad_starts, zflag, h, p_rows)
    y = _experts(block_e, n_used, xs, w_gate, w_up, w_down)
    out = _combine(_tile_idx(idx, COMB_TM), pad_starts, route, h, ln2_g[None, :], ln2_b[None, :], y, alpha)
    return out.reshape(b, t, d)


def kernel(x, positions, w_in, cmp_k_pe, cmp_k_w1, cmp_k_b1, cmp_k_w2, cmp_v_pe, cmp_v_w1, cmp_v_b1, cmp_v_w2, conv_w, conv_b, dt_bias, a_log, d_skip, ssm_norm_w, w_out, ln1_g, ln1_b, w_router_group, b_router_group, w_router_expert, b_router_expert, w_gate, w_up, w_down, ln2_g, ln2_b):
    depth = w_in.shape[0]
    alpha = (2 * depth) ** 0.25
    params = (w_in, cmp_k_pe, cmp_k_w1, cmp_k_b1, cmp_k_w2, cmp_v_pe, cmp_v_w1, cmp_v_b1, cmp_v_w2, conv_w, conv_b,
              dt_bias, a_log, d_skip, ssm_norm_w, w_out, ln1_g, ln1_b, w_router_group, b_router_group,
              w_router_expert, b_router_expert, w_gate, w_up, w_down, ln2_g, ln2_b)
    for l in range(depth):
        x = _layer(x, positions, *[p[l] for p in params], alpha)
    return x
```

```python
import functools
import math

import jax
import jax.numpy as jnp
import numpy as np
from jax import lax
from jax.experimental import pallas as pl
from jax.experimental.pallas import tpu as pltpu

f32 = jnp.float32
bf16 = jnp.bfloat16
i32 = jnp.int32

HEAD_DIM = 64
NSA_HEADS = 16
NSA_KV_GROUPS = 2
NSA_REP = NSA_HEADS // NSA_KV_GROUPS
NSA_WIDTH = NSA_HEADS * HEAD_DIM
KV_WIDTH = NSA_KV_GROUPS * HEAD_DIM
CMP_BLOCK = 32
CMP_STRIDE = 16
CMP_HIDDEN = 256
SEL_BLOCK = 64
SEL_TOPK = 16
WINDOW = 512
FORCED_SCORE = 1.0e4
SSM_HEADDIM = 64
SSM_HEADS = 16
SSM_WIDTH = SSM_HEADS * SSM_HEADDIM
SSM_GROUPS = 4
SSM_REP = SSM_HEADS // SSM_GROUPS
SSM_STATE = 128
CONV_WIDTH = 4
CHUNK = 256
XBC_WIDTH = SSM_WIDTH + 2 * SSM_GROUPS * SSM_STATE
ROPE_THETA = 500000.0
ROT_DIM = HEAD_DIM // 4
N_EXPERT_GROUPS = 4
EXPERTS_PER_GROUP = 8
N_EXPERTS = N_EXPERT_GROUPS * EXPERTS_PER_GROUP
D_EXPERT = 512
NORM_EPS = 1e-5

LANES = 128
MASK_NEG = -1.0e30

PROJ_TM = 1024
PROJ_TN = 512
PREP_TM = 512
ATT_TQ = 128
ATT_TK = 512
KX_WIDTH = 2 * LANES
OUT_TM = 256
MOE_TM = 256
DISP_TM = 256
COMB_TM = 128
VMEM_LIMIT = 56 * 1024 * 1024

COL_Q = 0
COL_Z = NSA_WIDTH
COL_XBC = COL_Z + SSM_WIDTH
COL_KV = COL_XBC + XBC_WIDTH
COL_SMALL = COL_KV + 6 * KV_WIDTH
PROJ_COLS = 5120
SMALL_DT_OFF = 3 * NSA_HEADS


def _cparams(sem, vmem=VMEM_LIMIT):
    return pltpu.CompilerParams(dimension_semantics=sem, vmem_limit_bytes=vmem)


def _dot(a, b):
    return jnp.dot(a, b, preferred_element_type=f32)


def _dot_t(a, b):
    return lax.dot_general(a, b, (((1,), (1,)), ((), ())), preferred_element_type=f32)


def _dot_hi(a, b):
    return jnp.dot(a, b, preferred_element_type=f32, precision=lax.Precision.HIGHEST)


def _proj_kernel(x_ref, w_ref, o_ref, xb_ref):
    @pl.when(pl.program_id(1) == 0)
    def _():
        xb_ref[...] = x_ref[...].astype(bf16)

    o_ref[...] = _dot(xb_ref[...], w_ref[...])


def _proj(x2, w_cat):
    n, d = x2.shape
    cols = w_cat.shape[1]
    tm = min(PROJ_TM, n)
    return pl.pallas_call(
        _proj_kernel,
        grid=(n // tm, cols // PROJ_TN),
        in_specs=[pl.BlockSpec((tm, d), lambda i, j: (i, 0)),
                  pl.BlockSpec((d, PROJ_TN), lambda i, j: (0, j))],
        out_specs=pl.BlockSpec((tm, PROJ_TN), lambda i, j: (i, j)),
        out_shape=jax.ShapeDtypeStruct((n, cols), f32),
        scratch_shapes=[pltpu.VMEM((tm, d), bf16)],
        compiler_params=_cparams(("parallel", "arbitrary")),
        name="proj",
    )(x2, w_cat)


def _rope_tables(pos_i32, invf):
    ang = pos_i32.astype(f32) * invf
    return jnp.cos(ang), jnp.sin(ang)


def _rope128(x, cos, sin):
    half = ROT_DIM // 2
    d = lax.broadcasted_iota(i32, x.shape, 1) % HEAD_DIM
    up = pltpu.roll(x, LANES - half, 1)
    dn = pltpu.roll(x, half, 1)
    rot = jnp.where(d < half, -up, dn)
    return x * cos + rot * sin


def _nsa_prep_kernel(pos_ref, invf_ref, q_ref, kc_ref, vc_ref, ks_ref, vs_ref, kw_ref, vw_ref,
                     qo_ref, kco_ref, vco_ref, kso_ref, vso_ref, kwo_ref, vwo_ref):
    cos, sin = _rope_tables(pos_ref[...], invf_ref[...])
    scale = HEAD_DIM ** -0.5
    for c in range(NSA_WIDTH // LANES):
        t = _rope128(q_ref[:, c * LANES:(c + 1) * LANES], cos, sin) * scale
        qo_ref[0, 2 * c] = t[:, :HEAD_DIM].astype(bf16)
        qo_ref[0, 2 * c + 1] = t[:, HEAD_DIM:].astype(bf16)

    def split(src, dst, rope, dt):
        t = src[...]
        if rope:
            t = _rope128(t, cos, sin)
        for g in range(NSA_KV_GROUPS):
            dst[0, g] = t[:, g * HEAD_DIM:(g + 1) * HEAD_DIM].astype(dt)

    split(kc_ref, kco_ref, False, f32)
    split(vc_ref, vco_ref, False, f32)

    tm = pos_ref.shape[0]

    def values_with_ones(src, dst):
        t = src[...]
        lane = lax.broadcasted_iota(i32, (tm, HEAD_DIM), 1)
        tail = jnp.where(lane == 0, 1.0, 0.0).astype(bf16)
        for g in range(NSA_KV_GROUPS):
            vg = t[:, g * HEAD_DIM:(g + 1) * HEAD_DIM].astype(bf16)
            dst[0, g] = jnp.concatenate([vg, tail], axis=1)

    values_with_ones(vs_ref, vso_ref)
    values_with_ones(vw_ref, vwo_ref)

    kw_t = _rope128(kw_ref[...], cos, sin).T
    ks_t = _rope128(ks_ref[...], cos, sin).T
    for g in range(NSA_KV_GROUPS):
        kwo_ref[0, g, 0] = kw_t[g * HEAD_DIM:(g + 1) * HEAD_DIM, :].astype(bf16)
    blk = lax.broadcasted_iota(i32, (LANES, tm), 0)
    tok = pl.program_id(1) * tm + lax.broadcasted_iota(i32, (LANES, tm), 1)
    onehot_t = jnp.where(tok // SEL_BLOCK == blk, 1.0, 0.0).astype(bf16)
    for g in range(NSA_KV_GROUPS):
        kso_ref[0, g, 0, 0:LANES, :] = onehot_t
        kg = ks_t[g * HEAD_DIM:(g + 1) * HEAD_DIM, :].astype(bf16)
        kso_ref[0, g, 0, LANES:2 * LANES, :] = jnp.concatenate([kg, jnp.zeros_like(kg)], axis=0)


def _nsa_prep(proj, pos128, invf, b, t):
    tm = PREP_TM
    assert tm == ATT_TK
    nt = t // tm
    row = lambda bi, ti: (bi * nt + ti, 0)
    kv0 = COL_KV // LANES
    in_specs = [pl.BlockSpec((tm, LANES), row),
                pl.BlockSpec((1, LANES), lambda bi, ti: (0, 0)),
                pl.BlockSpec((tm, NSA_WIDTH), lambda bi, ti: (bi * nt + ti, COL_Q // NSA_WIDTH))]
    for k in range(6):
        in_specs.append(pl.BlockSpec((tm, LANES), functools.partial(lambda bi, ti, k: (bi * nt + ti, kv0 + k), k=k)))
    head = lambda bi, ti: (bi, 0, ti, 0)
    tile5 = lambda bi, ti: (bi, 0, ti, 0, 0)
    g = NSA_KV_GROUPS
    out_specs = [pl.BlockSpec((1, NSA_HEADS, tm, HEAD_DIM), head),
                 pl.BlockSpec((1, g, tm, HEAD_DIM), head), pl.BlockSpec((1, g, tm, HEAD_DIM), head),
                 pl.BlockSpec((1, g, 1, KX_WIDTH, tm), tile5), pl.BlockSpec((1, g, tm, LANES), head),
                 pl.BlockSpec((1, g, 1, HEAD_DIM, tm), tile5), pl.BlockSpec((1, g, tm, LANES), head)]
    out_shape = [jax.ShapeDtypeStruct((b, NSA_HEADS, t, HEAD_DIM), bf16),
                 jax.ShapeDtypeStruct((b, g, t, HEAD_DIM), f32), jax.ShapeDtypeStruct((b, g, t, HEAD_DIM), f32),
                 jax.ShapeDtypeStruct((b, g, nt, KX_WIDTH, tm), bf16), jax.ShapeDtypeStruct((b, g, t, LANES), bf16),
                 jax.ShapeDtypeStruct((b, g, nt, HEAD_DIM, tm), bf16),
                 jax.ShapeDtypeStruct((b, g, t, LANES), bf16)]
    return pl.pallas_call(
        _nsa_prep_kernel,
        grid=(b, nt),
        in_specs=in_specs,
        out_specs=out_specs,
        out_shape=out_shape,
        compiler_params=_cparams(("parallel", "parallel")),
        name="nsa_prep",
    )(pos128, invf, proj, proj, proj, proj, proj, proj, proj)


def _cmp_mlp_kernel(a_ref, pe_ref, w1_ref, b1_ref, w2_ref, pos_ref, invf_ref, o_ref, ot_ref):
    kind = pl.program_id(0)
    a = a_ref[0, 0]
    nc = a.shape[0]
    u = _dot((a + pe_ref[0, 0]).astype(bf16), w1_ref[0, 0])
    v = _dot((a + pe_ref[0, 1]).astype(bf16), w1_ref[0, 1])
    v_next = pltpu.roll(v, nc - 1, 0)
    hid = jax.nn.gelu(u + v_next + b1_ref[0])
    out = _dot(hid.astype(bf16), w2_ref[0])
    cos, sin = _rope_tables(pos_ref[0], invf_ref[...])
    roped = _rope128(out, cos, sin)
    out = jnp.where(kind == 0, roped, out)
    o_ref[0, 0] = out[:, :HEAD_DIM].astype(bf16)
    ot_ref[0, 0] = out.T[:HEAD_DIM, :].astype(bf16)


def _cmp_mlp(a, pe, w1, b1, w2, posc, invf, b):
    _, bg, nc, hw = a.shape
    g = bg // b
    return pl.pallas_call(
        _cmp_mlp_kernel,
        grid=(2, bg),
        in_specs=[pl.BlockSpec((1, 1, nc, hw), lambda k, i: (k, i, 0, 0)),
                  pl.BlockSpec((1, 2, 1, hw), lambda k, i: (k, 0, 0, 0)),
                  pl.BlockSpec((1, 2, hw, CMP_HIDDEN), lambda k, i: (k, 0, 0, 0)),
                  pl.BlockSpec((1, 1, CMP_HIDDEN), lambda k, i: (k, 0, 0)),
                  pl.BlockSpec((1, CMP_HIDDEN, LANES), lambda k, i: (k, 0, 0)),
                  pl.BlockSpec((1, nc, LANES), lambda k, i: (i // g, 0, 0)),
                  pl.BlockSpec((1, LANES), lambda k, i: (0, 0))],
        out_specs=[pl.BlockSpec((1, 1, nc, HEAD_DIM), lambda k, i: (k, i, 0, 0)),
                   pl.BlockSpec((1, 1, HEAD_DIM, nc), lambda k, i: (k, i, 0, 0))],
        out_shape=[jax.ShapeDtypeStruct((2, bg, nc, HEAD_DIM), bf16),
                   jax.ShapeDtypeStruct((2, bg, HEAD_DIM, nc), bf16)],
        compiler_params=_cparams(("parallel", "parallel")),
        name="cmp_mlp",
    )(a, pe, w1, b1, w2, posc, invf)


def _cmp_attn_kernel(q_ref, kct_ref, vc_ref, cover_ref, oc_ref, sel_ref):
    qi = pl.program_id(2)
    kct = kct_ref[0, 0]
    vc = vc_ref[0, 0]
    nc = vc.shape[0]
    rows = NSA_REP * ATT_TQ
    tq = qi * ATT_TQ + lax.broadcasted_iota(i32, (ATT_TQ, 1), 0)
    cend = lax.broadcasted_iota(i32, (1, nc), 1) * CMP_STRIDE + (CMP_BLOCK - 1)
    bias = jnp.where(cend <= tq, 0.0, MASK_NEG)
    row_live = jnp.where(tq >= CMP_BLOCK - 1, 1.0, 0.0)
    tiny = jnp.finfo(f32).tiny
    s = _dot(q_ref[0].reshape(rows, HEAD_DIM), kct).reshape(NSA_REP, ATT_TQ, nc) + bias[None]
    e = jnp.exp(s - jnp.max(s, axis=-1, keepdims=True)) * row_live[None]
    p = e / jnp.maximum(jnp.sum(e, axis=-1, keepdims=True), tiny)
    oc_ref[0] = _dot(p.reshape(rows, nc).astype(bf16), vc).reshape(NSA_REP, ATT_TQ, HEAD_DIM)
    psum = jnp.sum(p, axis=0)
    hi = psum.astype(bf16)
    lo = (psum - hi.astype(f32)).astype(bf16)
    cover = cover_ref[...]
    imp = _dot(hi, cover) + _dot(lo, cover)

    j = lax.broadcasted_iota(i32, (ATT_TQ, LANES), 1)
    cur = tq // SEL_BLOCK
    forced = (j == 0) | (j == cur) | (j == cur - 1)
    valid = j * SEL_BLOCK <= tq
    imp = jnp.where(valid, jnp.where(forced, FORCED_SCORE, imp), -FORCED_SCORE)

    jt = lax.broadcasted_iota(i32, (LANES, ATT_TQ), 0)

    def pick(_, carry):
        work, sel = carry
        m = jnp.max(work, axis=0, keepdims=True)
        first = jnp.min(jnp.where(work == m, jt, LANES), axis=0, keepdims=True)
        hit = jt == first
        return jnp.where(hit, -jnp.inf, work), jnp.where(hit, 1.0, sel)

    _, sel_t = lax.fori_loop(0, SEL_TOPK, pick, (imp.T, jnp.zeros((LANES, ATT_TQ), f32)))
    sel_ref[0, 0] = jnp.where(valid, jnp.where(sel_t.T > 0.0, 0.0, MASK_NEG), MASK_NEG).astype(bf16)


def _cmp_attn(q_r, kvc, kvc_t, cover, b, t):
    g = NSA_KV_GROUPS
    nc = kvc.shape[2]
    nq = t // ATT_TQ
    return pl.pallas_call(
        _cmp_attn_kernel,
        grid=(b, g, nq),
        in_specs=[pl.BlockSpec((1, NSA_REP, ATT_TQ, HEAD_DIM), lambda bi, gi, qi: (bi, gi, qi, 0)),
                  pl.BlockSpec((1, 1, HEAD_DIM, nc), lambda bi, gi, qi: (0, bi * g + gi, 0, 0)),
                  pl.BlockSpec((1, 1, nc, HEAD_DIM), lambda bi, gi, qi: (1, bi * g + gi, 0, 0)),
                  pl.BlockSpec((nc, LANES), lambda bi, gi, qi: (0, 0))],
        out_specs=[pl.BlockSpec((1, NSA_REP, ATT_TQ, HEAD_DIM), lambda bi, gi, qi: (bi, gi, qi, 0)),
                   pl.BlockSpec((1, 1, ATT_TQ, LANES), lambda bi, gi, qi: (bi, gi, qi, 0))],
        out_shape=[jax.ShapeDtypeStruct((b, NSA_HEADS, t, HEAD_DIM), f32),
                   jax.ShapeDtypeStruct((b, g, t, LANES), bf16)],
        compiler_params=_cparams(("parallel", "parallel", "parallel")),
        name="cmp_attn",
    )(q_r, kvc_t, kvc, cover)


def _nsa_attn_kernel(q_ref, kx_ref, vs_ref, kw_ref, vw_ref, sel_ref, oc_ref, gate_ref,
                     o_ref, qx_ref, s_ref, m_ref, acc_ref):
    gi = pl.program_id(1)
    qi = pl.program_id(2)
    start = qi * ATT_TQ
    tq = start + lax.broadcasted_iota(i32, (ATT_TQ, 1), 0)

    selb = sel_ref[0, 0]
    for r in range(NSA_REP):
        qx_ref[r * ATT_TQ:(r + 1) * ATT_TQ, 0:LANES] = selb
        qr = q_ref[0, r]
        qx_ref[r * ATT_TQ:(r + 1) * ATT_TQ, LANES:2 * LANES] = jnp.concatenate([qr, jnp.zeros_like(qr)], axis=1)
    rows = NSA_REP * ATT_TQ

    m_ref[...] = jnp.full(m_ref.shape, MASK_NEG, f32)
    acc_ref[...] = jnp.zeros(acc_ref.shape, f32)

    def consume(br, kt, v_ref, bias):
        k0 = pl.multiple_of(kt * ATT_TK, ATT_TK)
        v = v_ref[0, 0, pl.ds(k0, ATT_TK), :]
        s = s_ref[...]
        if bias is not None:
            s = (s.reshape(NSA_REP, ATT_TQ, ATT_TK) + bias[None]).reshape(rows, ATT_TK)
        m_prev = m_ref[br]
        m_new = jnp.maximum(m_prev, jnp.max(s, axis=-1, keepdims=True))
        alpha = jnp.exp(m_prev - m_new)
        p = jnp.exp(s - jnp.concatenate([m_new] * (ATT_TK // LANES), axis=1))
        acc_ref[br] = alpha * acc_ref[br] + _dot(p.astype(bf16), v)
        m_ref[br] = m_new

    def kpos(kt):
        return kt * ATT_TK + lax.broadcasted_iota(i32, (1, ATT_TK), 1)

    last = start // ATT_TK
    s_ref[...] = _dot(qx_ref[...], kx_ref[0, 0, 0])

    def sel_step(kt, _):
        consume(0, kt, vs_ref, None)
        s_ref[...] = _dot(qx_ref[...], kx_ref[0, 0, kt + 1])
        return 0

    lax.fori_loop(0, last, sel_step, 0)
    consume(0, last, vs_ref, jnp.where(kpos(last) <= tq, 0.0, MASK_NEG))

    q_all = q_ref[0].reshape(rows, HEAD_DIM)

    def win_step(kt, _):
        s_ref[...] = _dot(q_all, kw_ref[0, 0, kt])
        kp = kpos(kt)
        consume(1, kt, vw_ref, jnp.where(kp <= tq, jnp.where(kp > tq - WINDOW, 0.0, MASK_NEG), MASK_NEG))
        return 0

    lax.fori_loop(jnp.maximum(last - WINDOW // ATT_TK, 0), last + 1, win_step, 0)

    gates = jax.nn.sigmoid(gate_ref[...])
    per_group = 3 * NSA_REP
    shifted = gates
    for gg in range(1, NSA_KV_GROUPS):
        shifted = jnp.where(gi == gg, pltpu.roll(gates, LANES - gg * per_group, 1), shifted)
    for r in range(NSA_REP):
        def gate(br):
            c = r * 3 + br
            return shifted[:, c:c + 1]
        asel = acc_ref[0, r * ATT_TQ:(r + 1) * ATT_TQ]
        awin = acc_ref[1, r * ATT_TQ:(r + 1) * ATT_TQ]
        osel = asel[:, :HEAD_DIM] / asel[:, HEAD_DIM:HEAD_DIM + 1]
        owin = awin[:, :HEAD_DIM] / awin[:, HEAD_DIM:HEAD_DIM + 1]
        o_ref[:, r * HEAD_DIM:(r + 1) * HEAD_DIM] = gate(0) * oc_ref[0, r] + gate(1) * osel + gate(2) * owin


def _nsa_attn(q_r, kx, vs, kw, vw, selb, oc, proj, b, t):
    g = NSA_KV_GROUPS
    nq = t // ATT_TQ
    vspec = pl.BlockSpec((1, 1, t, LANES), lambda bi, gi, qi: (bi, gi, 0, 0))
    hspec = pl.BlockSpec((1, NSA_REP, ATT_TQ, HEAD_DIM), lambda bi, gi, qi: (bi, gi, qi, 0))
    return pl.pallas_call(
        _nsa_attn_kernel,
        grid=(b, g, nq),
        in_specs=[hspec,
                  pl.BlockSpec((1, 1, t // ATT_TK, KX_WIDTH, ATT_TK), lambda bi, gi, qi: (bi, gi, 0, 0, 0)),
                  vspec,
                  pl.BlockSpec((1, 1, t // ATT_TK, HEAD_DIM, ATT_TK), lambda bi, gi, qi: (bi, gi, 0, 0, 0)),
                  vspec,
                  pl.BlockSpec((1, 1, ATT_TQ, LANES), lambda bi, gi, qi: (bi, gi, qi, 0)),
                  hspec,
                  pl.BlockSpec((ATT_TQ, LANES), lambda bi, gi, qi: (bi * nq + qi, COL_SMALL // LANES))],
        out_specs=pl.BlockSpec((ATT_TQ, NSA_REP * HEAD_DIM), lambda bi, gi, qi: (bi * nq + qi, gi)),
        out_shape=jax.ShapeDtypeStruct((b * t, NSA_WIDTH), f32),
        scratch_shapes=[pltpu.VMEM((NSA_REP * ATT_TQ, KX_WIDTH), bf16),
                        pltpu.VMEM((NSA_REP * ATT_TQ, ATT_TK), f32),
                        pltpu.VMEM((2, NSA_REP * ATT_TQ, LANES), f32),
                        pltpu.VMEM((2, NSA_REP * ATT_TQ, LANES), f32)],
        compiler_params=_cparams(("parallel", "parallel", "arbitrary")),
        name="nsa_attn",
    )(q_r, kx, vs, kw, vw, selb, oc, proj)


HALO = 8


def _ssd_kernel(xbc_ref, z_ref, small_ref, dtt_ref, cw_ref, cb_ref, dtb_r_ref, dtb_c_ref,
                alog_r_ref, alog_c_ref, dskip_ref, nw_ref, o_ref, ext_ref, st_ref):
    c = pl.program_id(1)
    L = CHUNK

    @pl.when(c == 0)
    def _():
        ext_ref[0:HALO, :] = jnp.zeros((HALO, XBC_WIDTH), f32)
        st_ref[...] = jnp.zeros_like(st_ref)

    ext_ref[HALO:HALO + L, :] = xbc_ref[...]
    conv = cb_ref[...]
    for k in range(CONV_WIDTH):
        off = HALO - (CONV_WIDTH - 1) + k
        conv = conv + cw_ref[k:k + 1, :] * ext_ref[off:off + L, :]
    ext_ref[0:HALO, :] = ext_ref[L:L + HALO, :]
    act = conv * jax.nn.sigmoid(conv)
    xs = act[:, :SSM_WIDTH]
    bm = act[:, SSM_WIDTH:SSM_WIDTH + SSM_GROUPS * SSM_STATE]
    cm = act[:, SSM_WIDTH + SSM_GROUPS * SSM_STATE:]

    dt_c = jax.nn.softplus(small_ref[:, SMALL_DT_OFF:SMALL_DT_OFF + SSM_HEADS] + dtb_r_ref[...])
    dt_r = jax.nn.softplus(dtt_ref[0] + dtb_c_ref[...])
    a_r = -jnp.exp(alog_r_ref[...])
    a_c = -jnp.exp(alog_c_ref[...])
    row = lax.broadcasted_iota(i32, (L, L), 0)
    col = lax.broadcasted_iota(i32, (L, L), 1)
    causal = col <= row
    tri = jnp.where(causal, 1.0, 0.0)
    acs_c = _dot_hi(tri, dt_c * a_r)
    acs_r = _dot_hi(dt_r * a_c, jnp.where(row <= col, 1.0, 0.0))

    z = z_ref[...]
    ys = []
    for g in range(SSM_GROUPS):
        cg = cm[:, g * SSM_STATE:(g + 1) * SSM_STATE].astype(bf16)
        bg = bm[:, g * SSM_STATE:(g + 1) * SSM_STATE]
        cb = _dot_t(cg, bg.astype(bf16))
        bg_t = bg.T
        ssq = jnp.zeros((L, 1), f32)
        yg = []
        for r in range(SSM_REP):
            h = g * SSM_REP + r
            hs = slice(h * SSM_HEADDIM, (h + 1) * SSM_HEADDIM)
            a_col = acs_c[:, h:h + 1]
            a_row = acs_r[h:h + 1, :]
            dt_row = dt_r[h:h + 1, :]
            a_last = acs_r[h:h + 1, L - 1:L]
            seg = a_col - a_row
            decay = jnp.where(causal, jnp.exp(jnp.where(causal, seg, 0.0)), 0.0)
            w = cb * decay * dt_row
            x_h = xs[:, hs]
            xb = x_h.astype(bf16)
            st = st_ref[h]
            y = _dot(w.astype(bf16), xb)
            y = y + _dot(cg, st.astype(bf16)) * jnp.exp(a_col)
            y = y + dskip_ref[:, hs] * x_h
            bscaled = bg_t * (jnp.exp(a_last - a_row) * dt_row)
            st_ref[h] = jnp.exp(a_last) * st + _dot(bscaled.astype(bf16), xb)
            zh = z[:, hs]
            y = y * (zh * jax.nn.sigmoid(zh))
            ssq = ssq + jnp.sum(y * y, axis=-1, keepdims=True)
            yg.append(y)
        rs = lax.rsqrt(ssq / (SSM_REP * SSM_HEADDIM) + NORM_EPS)
        for r in range(SSM_REP):
            h = g * SSM_REP + r
            hs = slice(h * SSM_HEADDIM, (h + 1) * SSM_HEADDIM)
            o_ref[:, hs] = yg[r] * rs * nw_ref[:, hs]


def _ssd(proj, dtt, cw, cb, dtb_r, dtb_c, alog_r, alog_c, dskip, nw, b, t):
    nch = t // CHUNK
    row = lambda bi, ci: bi * nch + ci
    const2 = lambda bi, ci: (0, 0)
    return pl.pallas_call(
        _ssd_kernel,
        grid=(b, nch),
        in_specs=[pl.BlockSpec((CHUNK, XBC_WIDTH), lambda bi, ci: (row(bi, ci), COL_XBC // XBC_WIDTH)),
                  pl.BlockSpec((CHUNK, SSM_WIDTH), lambda bi, ci: (row(bi, ci), COL_Z // SSM_WIDTH)),
                  pl.BlockSpec((CHUNK, LANES), lambda bi, ci: (row(bi, ci), COL_SMALL // LANES)),
                  pl.BlockSpec((1, SSM_HEADS, CHUNK), lambda bi, ci: (bi, 0, ci)),
                  pl.BlockSpec((CONV_WIDTH, XBC_WIDTH), const2),
                  pl.BlockSpec((1, XBC_WIDTH), const2),
                  pl.BlockSpec((1, SSM_HEADS), const2),
                  pl.BlockSpec((SSM_HEADS, 1), const2),
                  pl.BlockSpec((1, SSM_HEADS), const2),
                  pl.BlockSpec((SSM_HEADS, 1), const2),
                  pl.BlockSpec((1, SSM_WIDTH), const2),
                  pl.BlockSpec((1, SSM_WIDTH), const2)],
        out_specs=pl.BlockSpec((CHUNK, SSM_WIDTH), lambda bi, ci: (row(bi, ci), 0)),
        out_shape=jax.ShapeDtypeStruct((b * t, SSM_WIDTH), f32),
        scratch_shapes=[pltpu.VMEM((HALO + CHUNK, XBC_WIDTH), f32),
                        pltpu.VMEM((SSM_HEADS, SSM_STATE, SSM_HEADDIM), f32)],
        compiler_params=_cparams(("parallel", "arbitrary")),
        name="ssd",
    )(proj, proj, proj, dtt, cw, cb, dtb_r, dtb_c, alog_r, alog_c, dskip, nw)


def _layer_norm(v, g, b):
    mu = jnp.mean(v, axis=-1, keepdims=True)
    d = v - mu
    var = jnp.mean(d * d, axis=-1, keepdims=True)
    return d * lax.rsqrt(var + NORM_EPS) * g + b


def _outproj_kernel(alpha, ya_ref, yb_ref, x_ref, wa_ref, wb_ref, g_ref, b_ref, wr_ref, br_ref,
                    h_ref, route_ref, cnt_ref):
    i = pl.program_id(0)
    tm = x_ref.shape[0]
    mix = _dot(ya_ref[...].astype(bf16), wa_ref[...]) + _dot(yb_ref[...].astype(bf16), wb_ref[...])
    h = _layer_norm(alpha * x_ref[...] + mix, g_ref[...], b_ref[...])
    h_ref[...] = h

    h_hi = h.astype(bf16)
    h_lo = (h - h_hi.astype(f32)).astype(bf16)
    t = _dot(h_hi, wr_ref[...])
    logits = t[:, :LANES] + t[:, LANES:] + _dot(h_lo, wr_ref[:, :LANES]) + br_ref[...]
    lane = lax.broadcasted_iota(i32, (tm, LANES), 1)
    ninf = -jnp.inf
    gmask = lane < N_EXPERT_GROUPS
    gl = jnp.where(gmask, logits, ninf)
    ge = jnp.where(gmask, jnp.exp(gl - jnp.max(gl, axis=-1, keepdims=True)), 0.0)
    pg = ge / jnp.sum(ge, axis=-1, keepdims=True)
    g_gate = jnp.max(pg, axis=-1, keepdims=True)
    g_sel = jnp.min(jnp.where(gmask & (pg == g_gate), lane, LANES), axis=-1, keepdims=True)
    lo = N_EXPERT_GROUPS + g_sel * EXPERTS_PER_GROUP
    emask = (lane >= lo) & (lane < lo + EXPERTS_PER_GROUP)
    el = jnp.where(emask, logits, ninf)
    ee = jnp.where(emask, jnp.exp(el - jnp.max(el, axis=-1, keepdims=True)), 0.0)
    pe = ee / jnp.sum(ee, axis=-1, keepdims=True)
    p0 = jnp.max(pe, axis=-1, keepdims=True)
    l0 = jnp.min(jnp.where(emask & (pe == p0), lane, LANES), axis=-1, keepdims=True)
    rest = jnp.where(emask & (lane != l0), pe, ninf)
    p1 = jnp.max(rest, axis=-1, keepdims=True)
    l1 = jnp.min(jnp.where(rest == p1, lane, LANES), axis=-1, keepdims=True)
    psum = p0 + p1
    w0 = g_gate * p0 / psum
    w1 = g_gate * p1 / psum
    e0 = l0 - N_EXPERT_GROUPS
    e1 = l1 - N_EXPERT_GROUPS

    @pl.when(i == 0)
    def _():
        cnt_ref[...] = jnp.zeros_like(cnt_ref)

    oh0 = lane == e0
    oh1 = lane == e1
    both = jnp.where(oh0, 1.0, 0.0) + jnp.where(oh1, 1.0, 0.0)
    r_i = lax.broadcasted_iota(i32, (tm, tm), 0)
    c_i = lax.broadcasted_iota(i32, (tm, tm), 1)
    strict = jnp.where(c_i < r_i, 1.0, 0.0).astype(bf16)
    before = _dot(strict, both.astype(bf16)) + cnt_ref[...]
    rank0 = jnp.sum(jnp.where(oh0, before, 0.0), axis=-1, keepdims=True)
    rank1 = jnp.sum(jnp.where(oh1, before, 0.0), axis=-1, keepdims=True)
    cnt_ref[...] = cnt_ref[...] + jnp.sum(both, axis=0, keepdims=True)

    out = jnp.where(lane == 0, e0.astype(f32), 0.0)
    out = jnp.where(lane == 1, e1.astype(f32), out)
    out = jnp.where(lane == 2, w0, out)
    out = jnp.where(lane == 3, w1, out)
    out = jnp.where(lane == 4, rank0, out)
    out = jnp.where(lane == 5, rank1, out)
    route_ref[...] = out


def _outproj(y_nsa, y_ssm, x2, wa, wb, g, bta, wr, br, alpha):
    n, d = x2.shape
    tm = OUT_TM
    const = lambda i: (0, 0)
    rowb = lambda i: (i, 0)
    return pl.pallas_call(
        functools.partial(_outproj_kernel, alpha),
        grid=(n // tm,),
        in_specs=[pl.BlockSpec((tm, NSA_WIDTH), rowb), pl.BlockSpec((tm, SSM_WIDTH), rowb),
                  pl.BlockSpec((tm, d), rowb),
                  pl.BlockSpec((NSA_WIDTH, d), const), pl.BlockSpec((SSM_WIDTH, d), const),
                  pl.BlockSpec((1, d), const), pl.BlockSpec((1, d), const),
                  pl.BlockSpec((d, 2 * LANES), const), pl.BlockSpec((1, LANES), const)],
        out_specs=[pl.BlockSpec((tm, d), rowb), pl.BlockSpec((tm, LANES), rowb),
                   pl.BlockSpec((1, LANES), const)],
        out_shape=[jax.ShapeDtypeStruct((n, d), f32), jax.ShapeDtypeStruct((n, LANES), f32),
                   jax.ShapeDtypeStruct((1, LANES), f32)],
        compiler_params=_cparams(("arbitrary",)),
        name="outproj",
    )(y_nsa, y_ssm, x2, wa, wb, g, bta, wr, br)


def _row_copy(src_ref, src_row, dst_ref, dst_row, sem):
    return pltpu.make_async_copy(src_ref.at[pl.ds(src_row, 1), :], dst_ref.at[pl.ds(dst_row, 1), :], sem)


def _dest_row(idx_ref, ps_ref, tm, k, r):
    return ps_ref[idx_ref[0, 0, k * tm + r]] + idx_ref[0, 0, (2 + k) * tm + r]


def _dispatch_kernel(idx_ref, ps_ref, zflag_ref, h_ref, xs_ref, zero_ref, sem, zsem):
    tm = h_ref.shape[0]
    nb = zflag_ref.shape[0]

    @pl.when(pl.program_id(0) == 0)
    def _():
        zero_ref[...] = jnp.zeros_like(zero_ref)

        def zblock(i):
            rows = pl.ds(pl.multiple_of(i * MOE_TM, MOE_TM), MOE_TM)
            return pltpu.make_async_copy(zero_ref, xs_ref.at[rows, :], zsem)

        def zstart(i, _):
            @pl.when(zflag_ref[i] != 0)
            def _():
                zblock(i).start()
            return 0

        def zwait(i, _):
            @pl.when(zflag_ref[i] != 0)
            def _():
                zblock(i).wait()
            return 0

        lax.fori_loop(0, nb, zstart, 0)
        lax.fori_loop(0, nb, zwait, 0)

    def issue(r, _):
        for k in range(2):
            _row_copy(h_ref, r, xs_ref, _dest_row(idx_ref, ps_ref, tm, k, r), sem).start()
        return 0

    lax.fori_loop(0, tm, issue, 0)

    def drain(r, _):
        for k in range(2):
            _row_copy(h_ref, 0, xs_ref, 0, sem).wait()
        return 0

    lax.fori_loop(0, tm, drain, 0)


def _dispatch(idx_t, pad_starts, zflag, h, p_rows):
    n, d = h.shape
    tm = DISP_TM
    return pl.pallas_call(
        _dispatch_kernel,
        grid=(n // tm,),
        in_specs=[pl.BlockSpec((1, 1, 4 * tm), lambda i: (i, 0, 0), memory_space=pltpu.SMEM),
                  pl.BlockSpec(memory_space=pltpu.SMEM),
                  pl.BlockSpec(memory_space=pltpu.SMEM),
                  pl.BlockSpec((tm, d), lambda i: (i, 0))],
        out_specs=pl.BlockSpec(memory_space=pl.ANY),
        out_shape=jax.ShapeDtypeStruct((p_rows, d), f32),
        scratch_shapes=[pltpu.VMEM((MOE_TM, d), f32), pltpu.SemaphoreType.DMA(()), pltpu.SemaphoreType.DMA(())],
        compiler_params=_cparams(("arbitrary",)),
        name="dispatch",
    )(idx_t, pad_starts, zflag, h)


def _experts_kernel(be_ref, nu_ref, xs_ref, wg_ref, wu_ref, wd_ref, y_ref, wgb_ref, wub_ref, wdb_ref):
    i = pl.program_id(0)
    e = be_ref[i]
    prev = be_ref[jnp.maximum(i - 1, 0)]

    @pl.when((i == 0) | (e != prev))
    def _():
        wgb_ref[...] = wg_ref[0].astype(bf16)
        wub_ref[...] = wu_ref[0].astype(bf16)
        wdb_ref[...] = wd_ref[0].astype(bf16)

    @pl.when(i < nu_ref[0])
    def _():
        xb = xs_ref[...].astype(bf16)
        gte = _dot(xb, wgb_ref[...])
        up = _dot(xb, wub_ref[...])
        act = gte * jax.nn.sigmoid(gte) * up
        y_ref[...] = _dot(act.astype(bf16), wdb_ref[...])

    @pl.when(i >= nu_ref[0])
    def _():
        y_ref[...] = jnp.zeros_like(y_ref)


def _experts(block_e, n_used, xs, w_gate, w_up, w_down):
    p_rows, d = xs.shape
    tm = MOE_TM
    nb = p_rows // tm
    de = w_gate.shape[-1]
    xmap = lambda i, be, nu: (jnp.maximum(jnp.minimum(i, nu[0] - 1), 0), 0)
    return pl.pallas_call(
        _experts_kernel,
        grid_spec=pltpu.PrefetchScalarGridSpec(
            num_scalar_prefetch=2,
            grid=(nb,),
            in_specs=[pl.BlockSpec((tm, d), xmap),
                      pl.BlockSpec((1, d, de), lambda i, be, nu: (be[i], 0, 0)),
                      pl.BlockSpec((1, d, de), lambda i, be, nu: (be[i], 0, 0)),
                      pl.BlockSpec((1, de, d), lambda i, be, nu: (be[i], 0, 0))],
            out_specs=pl.BlockSpec((tm, d), lambda i, be, nu: (i, 0)),
            scratch_shapes=[pltpu.VMEM((d, de), bf16), pltpu.VMEM((d, de), bf16), pltpu.VMEM((de, d), bf16)],
        ),
        out_shape=jax.ShapeDtypeStruct((p_rows, d), f32),
        compiler_params=_cparams(("arbitrary",)),
        name="experts",
    )(block_e, n_used, xs, w_gate, w_up, w_down)


def _combine_kernel(alpha, idx_ref, ps_ref, route_ref, h_ref, g_ref, b_ref, y_ref, o_ref, buf_ref, sem):
    tm = h_ref.shape[0]

    def issue(r, _):
        for k in range(2):
            _row_copy(y_ref, _dest_row(idx_ref, ps_ref, tm, k, r), buf_ref.at[k], r, sem).start()
        return 0

    lax.fori_loop(0, tm, issue, 0)

    def drain(r, _):
        for k in range(2):
            _row_copy(y_ref, 0, buf_ref.at[k], 0, sem).wait()
        return 0

    lax.fori_loop(0, tm, drain, 0)

    route = route_ref[...]
    ffn = route[:, 2:3] * buf_ref[0] + route[:, 3:4] * buf_ref[1]
    o_ref[...] = _layer_norm(alpha * h_ref[...] + ffn, g_ref[...], b_ref[...])


def _combine(idx_t, pad_starts, route, h, g, bta, y, alpha):
    n, d = h.shape
    tm = COMB_TM
    const = lambda i: (0, 0)
    return pl.pallas_call(
        functools.partial(_combine_kernel, alpha),
        grid=(n // tm,),
        in_specs=[pl.BlockSpec((1, 1, 4 * tm), lambda i: (i, 0, 0), memory_space=pltpu.SMEM),
                  pl.BlockSpec(memory_space=pltpu.SMEM),
                  pl.BlockSpec((tm, LANES), lambda i: (i, 0)),
                  pl.BlockSpec((tm, d), lambda i: (i, 0)),
                  pl.BlockSpec((1, d), const), pl.BlockSpec((1, d), const),
                  pl.BlockSpec(memory_space=pl.ANY)],
        out_specs=pl.BlockSpec((tm, d), lambda i: (i, 0)),
        out_shape=jax.ShapeDtypeStruct((n, d), f32),
        scratch_shapes=[pltpu.VMEM((2, tm, d), f32), pltpu.SemaphoreType.DMA(())],
        compiler_params=_cparams(("arbitrary",)),
        name="combine",
    )(idx_t, pad_starts, route, h, g, bta, y)


def _tile_idx(idx, tm):
    n = idx.shape[1]
    return idx.reshape(4, n // tm, tm).transpose(1, 0, 2).reshape(n // tm, 1, 4 * tm)


def _layer(x, positions, w_in, cmp_k_pe, cmp_k_w1, cmp_k_b1, cmp_k_w2, cmp_v_pe, cmp_v_w1, cmp_v_b1, cmp_v_w2,
           conv_w, conv_b, dt_bias, a_log, d_skip, ssm_norm_w, w_out, ln1_g, ln1_b,
           w_router_group, b_router_group, w_router_expert, b_router_expert, w_gate, w_up, w_down, ln2_g, ln2_b,
           alpha):
    b, t, d = x.shape
    n = b * t
    x2 = x.reshape(n, d)

    c0 = NSA_WIDTH
    c1 = c0 + 6 * KV_WIDTH
    c2 = c1 + 3 * NSA_HEADS
    c3 = c2 + SSM_WIDTH
    c4 = c3 + XBC_WIDTH
    w_small = jnp.concatenate([w_in[:, c1:c2], w_in[:, c4:], jnp.zeros((d, LANES - 3 * NSA_HEADS - SSM_HEADS), f32)], axis=1)
    w_cat = jnp.concatenate([w_in[:, :c0], w_in[:, c2:c3], w_in[:, c3:c4], w_in[:, c0:c1], w_small,
                             jnp.zeros((d, PROJ_COLS - COL_SMALL - LANES), f32)], axis=1).astype(bf16)
    proj = _proj(x2, w_cat)

    lane = np.arange(LANES) % HEAD_DIM
    inv_freq = ROPE_THETA ** (-jnp.arange(0, ROT_DIM, 2, dtype=f32) / ROT_DIM)
    invf = jnp.where(lane < ROT_DIM, inv_freq[lane % (ROT_DIM // 2)], 0.0).astype(f32)[None, :]
    pos128 = jnp.broadcast_to(positions.reshape(n, 1), (n, LANES))
    q_r, k_cmp, v_cmp, k_sel, v_sel, k_win, v_win = _nsa_prep(proj, pos128, invf, b, t)

    nc = t // CMP_STRIDE
    half_w = CMP_STRIDE * HEAD_DIM
    a = jnp.stack([k_cmp, v_cmp]).reshape(2, b * NSA_KV_GROUPS, nc, half_w)
    pe = jnp.stack([cmp_k_pe, cmp_v_pe]).reshape(2, 2, 1, half_w)
    w1 = jnp.stack([cmp_k_w1, cmp_v_w1]).reshape(2, 2, half_w, CMP_HIDDEN).astype(bf16)
    b1 = jnp.stack([cmp_k_b1, cmp_v_b1]).reshape(2, 1, CMP_HIDDEN)
    w2 = jnp.pad(jnp.stack([cmp_k_w2, cmp_v_w2]), ((0, 0), (0, 0), (0, LANES - HEAD_DIM))).astype(bf16)
    cend = jnp.minimum(jnp.arange(nc) * CMP_STRIDE + CMP_BLOCK - 1, t - 1)
    posc = jnp.broadcast_to(positions[:, cend][:, :, None], (b, nc, LANES))
    kvc, kvc_t = _cmp_mlp(a, pe, w1, b1, w2, posc, invf, b)

    c_start = np.arange(nc)[:, None] * CMP_STRIDE
    s_start = np.arange(LANES)[None, :] * SEL_BLOCK
    cover = ((c_start < s_start + SEL_BLOCK) & (c_start + CMP_BLOCK > s_start)
             & (np.arange(nc)[:, None] < nc - 1) & (np.arange(LANES)[None, :] < t // SEL_BLOCK))
    cover = jnp.asarray(cover, bf16)
    o_cmp, selb = _cmp_attn(q_r, kvc, kvc_t, cover, b, t)

    y_nsa = _nsa_attn(q_r, k_sel, v_sel, k_win, v_win, selb, o_cmp, proj, b, t)

    dt_raw = proj[:, COL_SMALL + SMALL_DT_OFF:COL_SMALL + SMALL_DT_OFF + SSM_HEADS]
    dtt = dt_raw.reshape(b, t, SSM_HEADS).transpose(0, 2, 1)
    y_ssm = _ssd(proj, dtt, conv_w.reshape(CONV_WIDTH, XBC_WIDTH), conv_b.reshape(1, XBC_WIDTH),
                 dt_bias.reshape(1, SSM_HEADS), dt_bias.reshape(SSM_HEADS, 1),
                 a_log.reshape(1, SSM_HEADS), a_log.reshape(SSM_HEADS, 1),
                 jnp.repeat(d_skip, SSM_HEADDIM).reshape(1, SSM_WIDTH), ssm_norm_w.reshape(1, SSM_WIDTH), b, t)

    wr = jnp.concatenate([w_router_group, w_router_expert,
                          jnp.zeros((d, LANES - N_EXPERT_GROUPS - N_EXPERTS), f32)], axis=1)
    br = jnp.concatenate([b_router_group, b_router_expert,
                          jnp.zeros((LANES - N_EXPERT_GROUPS - N_EXPERTS,), f32)])[None, :]
    wr_hi = wr.astype(bf16)
    wr = jnp.concatenate([wr_hi, (wr - wr_hi.astype(f32)).astype(bf16)], axis=1)
    wo = w_out.astype(bf16)
    h, route, counts = _outproj(y_nsa, y_ssm, x2, wo[:NSA_WIDTH], wo[NSA_WIDTH:], ln1_g[None, :], ln1_b[None, :],
                                wr, br, alpha)

    cnt = counts[0, :N_EXPERTS].astype(i32)
    padded = (cnt + MOE_TM - 1) // MOE_TM * MOE_TM
    pad_ends = jnp.cumsum(padded)
    pad_starts = pad_ends - padded
    idx = jnp.concatenate([route[:, 0:2], route[:, 4:6]], axis=1).astype(i32).T
    p_rows = 2 * n + N_EXPERTS * MOE_TM
    nb = p_rows // MOE_TM
    block_e = jnp.minimum(jnp.sum(jnp.arange(nb, dtype=i32)[:, None] * MOE_TM >= pad_ends[None, :], axis=-1),
                          N_EXPERTS - 1).astype(i32)
    n_used = (pad_ends[-1] // MOE_TM).astype(i32).reshape(1)
    blk = jnp.arange(nb, dtype=i32)
    last_of_expert = jnp.any((blk[:, None] + 1) * MOE_TM == pad_ends[None, :], axis=-1)
    zflag = (last_of_expert | (blk >= n_used[0])).astype(i32)
    block_e = jnp.where(blk < n_used[0], block_e, block_e[jnp.maximum(n_used[0] - 1, 0)])

    xs = _dispatch(_tile_idx(idx, DISP_TM), pad_starts, zflag, h, p_rows)
    y = _experts(block_e, n_used, xs, w_gate, w_up, w_down)
    out = _combine(_tile_idx(idx, COMB_TM), pad_starts, route, h, ln2_g[None, :], ln2_b[None, :], y, alpha)
    return out.reshape(b, t, d)


def kernel(x, positions, w_in, cmp_k_pe, cmp_k_w1, cmp_k_b1, cmp_k_w2, cmp_v_pe, cmp_v_w1, cmp_v_b1, cmp_v_w2, conv_w, conv_b, dt_bias, a_log, d_skip, ssm_norm_w, w_out, ln1_g, ln1_b, w_router_group, b_router_group, w_router_expert, b_router_expert, w_gate, w_up, w_down, ln2_g, ln2_b):
    depth = w_in.shape[0]
    alpha = (2 * depth) ** 0.25
    params = (w_in, cmp_k_pe, cmp_k_w1, cmp_k_b1, cmp_k_w2, cmp_v_pe, cmp_v_w1, cmp_v_b1, cmp_v_w2, conv_w, conv_b,
              dt_bias, a_log, d_skip, ssm_norm_w, w_out, ln1_g, ln1_b, w_router_group, b_router_group,
              w_router_expert, b_router_expert, w_gate, w_up, w_down, ln2_g, ln2_b)
    for l in range(depth):
        x = _layer(x, positions, *[p[l] for p in params], alpha)
    return x
```

```python
import functools
import math

import jax
import jax.numpy as jnp
import numpy as np
from jax import lax
from jax.experimental import pallas as pl
from jax.experimental.pallas import tpu as pltpu

f32 = jnp.float32
bf16 = jnp.bfloat16
i32 = jnp.int32

HEAD_DIM = 64
NSA_HEADS = 16
NSA_KV_GROUPS = 2
NSA_REP = NSA_HEADS // NSA_KV_GROUPS
NSA_WIDTH = NSA_HEADS * HEAD_DIM
KV_WIDTH = NSA_KV_GROUPS * HEAD_DIM
CMP_BLOCK = 32
CMP_STRIDE = 16
CMP_HIDDEN = 256
SEL_BLOCK = 64
SEL_TOPK = 16
WINDOW = 512
FORCED_SCORE = 1.0e4
SSM_HEADDIM = 64
SSM_HEADS = 16
SSM_WIDTH = SSM_HEADS * SSM_HEADDIM
SSM_GROUPS = 4
SSM_REP = SSM_HEADS // SSM_GROUPS
SSM_STATE = 128
CONV_WIDTH = 4
CHUNK = 256
XBC_WIDTH = SSM_WIDTH + 2 * SSM_GROUPS * SSM_STATE
ROPE_THETA = 500000.0
ROT_DIM = HEAD_DIM // 4
N_EXPERT_GROUPS = 4
EXPERTS_PER_GROUP = 8
N_EXPERTS = N_EXPERT_GROUPS * EXPERTS_PER_GROUP
D_EXPERT = 512
NORM_EPS = 1e-5

LANES = 128
MASK_NEG = -1.0e30

PROJ_TM = 1024
PROJ_TN = 512
PREP_TM = 512
ATT_TQ = 256
ATT_TK = 512
KX_WIDTH = 2 * LANES
OUT_TM = 256
MOE_TM = 256
DISP_TM = 256
COMB_TM = 128
VMEM_LIMIT = 56 * 1024 * 1024

COL_Q = 0
COL_Z = NSA_WIDTH
COL_XBC = COL_Z + SSM_WIDTH
COL_KV = COL_XBC + XBC_WIDTH
COL_SMALL = COL_KV + 6 * KV_WIDTH
PROJ_COLS = 5120
SMALL_DT_OFF = 3 * NSA_HEADS


def _cparams(sem, vmem=VMEM_LIMIT):
    return pltpu.CompilerParams(dimension_semantics=sem, vmem_limit_bytes=vmem)


def _dot(a, b):
    return jnp.dot(a, b, preferred_element_type=f32)


def _dot_t(a, b):
    return lax.dot_general(a, b, (((1,), (1,)), ((), ())), preferred_element_type=f32)


def _dot_hi(a, b):
    return jnp.dot(a, b, preferred_element_type=f32, precision=lax.Precision.HIGHEST)


def _proj_kernel(x_ref, w_ref, o_ref, xb_ref):
    @pl.when(pl.program_id(1) == 0)
    def _():
        xb_ref[...] = x_ref[...].astype(bf16)

    o_ref[...] = _dot(xb_ref[...], w_ref[...])


def _proj(x2, w_cat):
    n, d = x2.shape
    cols = w_cat.shape[1]
    tm = min(PROJ_TM, n)
    return pl.pallas_call(
        _proj_kernel,
        grid=(n // tm, cols // PROJ_TN),
        in_specs=[pl.BlockSpec((tm, d), lambda i, j: (i, 0)),
                  pl.BlockSpec((d, PROJ_TN), lambda i, j: (0, j))],
        out_specs=pl.BlockSpec((tm, PROJ_TN), lambda i, j: (i, j)),
        out_shape=jax.ShapeDtypeStruct((n, cols), f32),
        scratch_shapes=[pltpu.VMEM((tm, d), bf16)],
        compiler_params=_cparams(("parallel", "arbitrary")),
        name="proj",
    )(x2, w_cat)


def _rope_tables(pos_i32, invf):
    ang = pos_i32.astype(f32) * invf
    return jnp.cos(ang), jnp.sin(ang)


def _rope128(x, cos, sin):
    half = ROT_DIM // 2
    d = lax.broadcasted_iota(i32, x.shape, 1) % HEAD_DIM
    up = pltpu.roll(x, LANES - half, 1)
    dn = pltpu.roll(x, half, 1)
    rot = jnp.where(d < half, -up, dn)
    return x * cos + rot * sin


def _nsa_prep_kernel(pos_ref, invf_ref, q_ref, kc_ref, vc_ref, ks_ref, vs_ref, kw_ref, vw_ref,
                     qo_ref, kco_ref, vco_ref, kso_ref, vso_ref, kwo_ref, vwo_ref):
    cos, sin = _rope_tables(pos_ref[...], invf_ref[...])
    scale = HEAD_DIM ** -0.5
    for c in range(NSA_WIDTH // LANES):
        t = _rope128(q_ref[:, c * LANES:(c + 1) * LANES], cos, sin) * scale
        qo_ref[0, 2 * c] = t[:, :HEAD_DIM].astype(bf16)
        qo_ref[0, 2 * c + 1] = t[:, HEAD_DIM:].astype(bf16)

    def split(src, dst, rope, dt):
        t = src[...]
        if rope:
            t = _rope128(t, cos, sin)
        for g in range(NSA_KV_GROUPS):
            dst[0, g] = t[:, g * HEAD_DIM:(g + 1) * HEAD_DIM].astype(dt)

    split(kc_ref, kco_ref, False, f32)
    split(vc_ref, vco_ref, False, f32)

    tm = pos_ref.shape[0]

    def values_with_ones(src, dst):
        t = src[...]
        lane = lax.broadcasted_iota(i32, (tm, HEAD_DIM), 1)
        tail = jnp.where(lane == 0, 1.0, 0.0).astype(bf16)
        for g in range(NSA_KV_GROUPS):
            vg = t[:, g * HEAD_DIM:(g + 1) * HEAD_DIM].astype(bf16)
            dst[0, g] = jnp.concatenate([vg, tail], axis=1)

    values_with_ones(vs_ref, vso_ref)
    values_with_ones(vw_ref, vwo_ref)

    kw_t = _rope128(kw_ref[...], cos, sin).T
    ks_t = _rope128(ks_ref[...], cos, sin).T
    for g in range(NSA_KV_GROUPS):
        kwo_ref[0, g, 0] = kw_t[g * HEAD_DIM:(g + 1) * HEAD_DIM, :].astype(bf16)
    blk = lax.broadcasted_iota(i32, (LANES, tm), 0)
    tok = pl.program_id(1) * tm + lax.broadcasted_iota(i32, (LANES, tm), 1)
    onehot_t = jnp.where(tok // SEL_BLOCK == blk, 1.0, 0.0).astype(bf16)
    for g in range(NSA_KV_GROUPS):
        kso_ref[0, g, 0, 0:LANES, :] = onehot_t
        kg = ks_t[g * HEAD_DIM:(g + 1) * HEAD_DIM, :].astype(bf16)
        kso_ref[0, g, 0, LANES:2 * LANES, :] = jnp.concatenate([kg, jnp.zeros_like(kg)], axis=0)


def _nsa_prep(proj, pos128, invf, b, t):
    tm = PREP_TM
    assert tm == ATT_TK
    nt = t // tm
    row = lambda bi, ti: (bi * nt + ti, 0)
    kv0 = COL_KV // LANES
    in_specs = [pl.BlockSpec((tm, LANES), row),
                pl.BlockSpec((1, LANES), lambda bi, ti: (0, 0)),
                pl.BlockSpec((tm, NSA_WIDTH), lambda bi, ti: (bi * nt + ti, COL_Q // NSA_WIDTH))]
    for k in range(6):
        in_specs.append(pl.BlockSpec((tm, LANES), functools.partial(lambda bi, ti, k: (bi * nt + ti, kv0 + k), k=k)))
    head = lambda bi, ti: (bi, 0, ti, 0)
    tile5 = lambda bi, ti: (bi, 0, ti, 0, 0)
    g = NSA_KV_GROUPS
    out_specs = [pl.BlockSpec((1, NSA_HEADS, tm, HEAD_DIM), head),
                 pl.BlockSpec((1, g, tm, HEAD_DIM), head), pl.BlockSpec((1, g, tm, HEAD_DIM), head),
                 pl.BlockSpec((1, g, 1, KX_WIDTH, tm), tile5), pl.BlockSpec((1, g, tm, LANES), head),
                 pl.BlockSpec((1, g, 1, HEAD_DIM, tm), tile5), pl.BlockSpec((1, g, tm, LANES), head)]
    out_shape = [jax.ShapeDtypeStruct((b, NSA_HEADS, t, HEAD_DIM), bf16),
                 jax.ShapeDtypeStruct((b, g, t, HEAD_DIM), f32), jax.ShapeDtypeStruct((b, g, t, HEAD_DIM), f32),
                 jax.ShapeDtypeStruct((b, g, nt, KX_WIDTH, tm), bf16), jax.ShapeDtypeStruct((b, g, t, LANES), bf16),
                 jax.ShapeDtypeStruct((b, g, nt, HEAD_DIM, tm), bf16),
                 jax.ShapeDtypeStruct((b, g, t, LANES), bf16)]
    return pl.pallas_call(
        _nsa_prep_kernel,
        grid=(b, nt),
        in_specs=in_specs,
        out_specs=out_specs,
        out_shape=out_shape,
        compiler_params=_cparams(("parallel", "parallel")),
        name="nsa_prep",
    )(pos128, invf, proj, proj, proj, proj, proj, proj, proj)


def _cmp_mlp_kernel(a_ref, pe_ref, w1_ref, b1_ref, w2_ref, pos_ref, invf_ref, o_ref, ot_ref):
    kind = pl.program_id(0)
    a = a_ref[0, 0]
    nc = a.shape[0]
    u = _dot((a + pe_ref[0, 0]).astype(bf16), w1_ref[0, 0])
    v = _dot((a + pe_ref[0, 1]).astype(bf16), w1_ref[0, 1])
    v_next = pltpu.roll(v, nc - 1, 0)
    hid = jax.nn.gelu(u + v_next + b1_ref[0])
    out = _dot(hid.astype(bf16), w2_ref[0])
    cos, sin = _rope_tables(pos_ref[0], invf_ref[...])
    roped = _rope128(out, cos, sin)
    out = jnp.where(kind == 0, roped, out)
    o_ref[0, 0] = out[:, :HEAD_DIM].astype(bf16)
    ot_ref[0, 0] = out.T[:HEAD_DIM, :].astype(bf16)


def _cmp_mlp(a, pe, w1, b1, w2, posc, invf, b):
    _, bg, nc, hw = a.shape
    g = bg // b
    return pl.pallas_call(
        _cmp_mlp_kernel,
        grid=(2, bg),
        in_specs=[pl.BlockSpec((1, 1, nc, hw), lambda k, i: (k, i, 0, 0)),
                  pl.BlockSpec((1, 2, 1, hw), lambda k, i: (k, 0, 0, 0)),
                  pl.BlockSpec((1, 2, hw, CMP_HIDDEN), lambda k, i: (k, 0, 0, 0)),
                  pl.BlockSpec((1, 1, CMP_HIDDEN), lambda k, i: (k, 0, 0)),
                  pl.BlockSpec((1, CMP_HIDDEN, LANES), lambda k, i: (k, 0, 0)),
                  pl.BlockSpec((1, nc, LANES), lambda k, i: (i // g, 0, 0)),
                  pl.BlockSpec((1, LANES), lambda k, i: (0, 0))],
        out_specs=[pl.BlockSpec((1, 1, nc, HEAD_DIM), lambda k, i: (k, i, 0, 0)),
                   pl.BlockSpec((1, 1, HEAD_DIM, nc), lambda k, i: (k, i, 0, 0))],
        out_shape=[jax.ShapeDtypeStruct((2, bg, nc, HEAD_DIM), bf16),
                   jax.ShapeDtypeStruct((2, bg, HEAD_DIM, nc), bf16)],
        compiler_params=_cparams(("parallel", "parallel")),
        name="cmp_mlp",
    )(a, pe, w1, b1, w2, posc, invf)


def _cmp_attn_kernel(q_ref, kct_ref, vc_ref, cover_ref, oc_ref, sel_ref, imp_ref):
    qi = pl.program_id(2)
    nc = vc_ref.shape[2]
    rows = NSA_REP * ATT_TQ
    tq = qi * ATT_TQ + lax.broadcasted_iota(i32, (ATT_TQ, 1), 0)
    row_live = jnp.where(tq >= CMP_BLOCK - 1, 1.0, 0.0)
    tiny = jnp.finfo(f32).tiny

    def attend(ncols):
        kct = kct_ref[0, 0, :, :ncols]
        vc = vc_ref[0, 0, :ncols, :]
        cend = lax.broadcasted_iota(i32, (1, ncols), 1) * CMP_STRIDE + (CMP_BLOCK - 1)
        bias = jnp.where(cend <= tq, 0.0, MASK_NEG)
        s = _dot(q_ref[0].reshape(rows, HEAD_DIM), kct).reshape(NSA_REP, ATT_TQ, ncols) + bias[None]
        e = jnp.exp(s - jnp.max(s, axis=-1, keepdims=True)) * row_live[None]
        p = e / jnp.maximum(jnp.sum(e, axis=-1, keepdims=True), tiny)
        oc_ref[0] = _dot(p.reshape(rows, ncols).astype(bf16), vc).reshape(NSA_REP, ATT_TQ, HEAD_DIM)
        psum = jnp.sum(p, axis=0)
        hi = psum.astype(bf16)
        lo = (psum - hi.astype(f32)).astype(bf16)
        cover = cover_ref[:ncols, :]
        imp_ref[...] = _dot(hi, cover) + _dot(lo, cover)

    need = (qi + 1) * (ATT_TQ // CMP_STRIDE)
    for ncols in range(LANES, nc + 1, LANES):
        @pl.when((need > ncols - LANES) & (need <= ncols))
        def _():
            attend(ncols)

    imp = imp_ref[...]
    j = lax.broadcasted_iota(i32, (ATT_TQ, LANES), 1)
    cur = tq // SEL_BLOCK
    forced = (j == 0) | (j == cur) | (j == cur - 1)
    valid = j * SEL_BLOCK <= tq
    imp = jnp.where(valid, jnp.where(forced, FORCED_SCORE, imp), -FORCED_SCORE)

    jt = lax.broadcasted_iota(i32, (LANES, ATT_TQ), 0)

    def pick(_, carry):
        work, sel = carry
        m = jnp.max(work, axis=0, keepdims=True)
        first = jnp.min(jnp.where(work == m, jt, LANES), axis=0, keepdims=True)
        hit = jt == first
        return jnp.where(hit, -jnp.inf, work), jnp.where(hit, 1.0, sel)

    _, sel_t = lax.fori_loop(0, SEL_TOPK, pick, (imp.T, jnp.zeros((LANES, ATT_TQ), f32)))
    sel_ref[0, 0] = jnp.where(valid, jnp.where(sel_t.T > 0.0, 0.0, MASK_NEG), MASK_NEG).astype(bf16)


def _cmp_attn(q_r, kvc, kvc_t, cover, b, t):
    g = NSA_KV_GROUPS
    nc = kvc.shape[2]
    nq = t // ATT_TQ
    return pl.pallas_call(
        _cmp_attn_kernel,
        grid=(b, g, nq),
        in_specs=[pl.BlockSpec((1, NSA_REP, ATT_TQ, HEAD_DIM), lambda bi, gi, qi: (bi, gi, qi, 0)),
                  pl.BlockSpec((1, 1, HEAD_DIM, nc), lambda bi, gi, qi: (0, bi * g + gi, 0, 0)),
                  pl.BlockSpec((1, 1, nc, HEAD_DIM), lambda bi, gi, qi: (1, bi * g + gi, 0, 0)),
                  pl.BlockSpec((nc, LANES), lambda bi, gi, qi: (0, 0))],
        out_specs=[pl.BlockSpec((1, NSA_REP, ATT_TQ, HEAD_DIM), lambda bi, gi, qi: (bi, gi, qi, 0)),
                   pl.BlockSpec((1, 1, ATT_TQ, LANES), lambda bi, gi, qi: (bi, gi, qi, 0))],
        out_shape=[jax.ShapeDtypeStruct((b, NSA_HEADS, t, HEAD_DIM), f32),
                   jax.ShapeDtypeStruct((b, g, t, LANES), bf16)],
        scratch_shapes=[pltpu.VMEM((ATT_TQ, LANES), f32)],
        compiler_params=_cparams(("parallel", "parallel", "parallel")),
        name="cmp_attn",
    )(q_r, kvc_t, kvc, cover)


def _nsa_attn_kernel(q_ref, kx_ref, vs_ref, kw_ref, vw_ref, sel_ref, oc_ref, gate_ref,
                     o_ref, qx_ref, s_ref, m_ref, acc_ref):
    gi = pl.program_id(1)
    qi = pl.program_id(2)
    start = qi * ATT_TQ
    tq = start + lax.broadcasted_iota(i32, (ATT_TQ, 1), 0)

    selb = sel_ref[0, 0]
    for r in range(NSA_REP):
        qx_ref[r * ATT_TQ:(r + 1) * ATT_TQ, 0:LANES] = selb
        qr = q_ref[0, r]
        qx_ref[r * ATT_TQ:(r + 1) * ATT_TQ, LANES:2 * LANES] = jnp.concatenate([qr, jnp.zeros_like(qr)], axis=1)
    rows = NSA_REP * ATT_TQ

    m_ref[...] = jnp.full(m_ref.shape, MASK_NEG, f32)
    acc_ref[...] = jnp.zeros(acc_ref.shape, f32)

    def consume(br, kt, v_ref, bias):
        k0 = pl.multiple_of(kt * ATT_TK, ATT_TK)
        v = v_ref[0, 0, pl.ds(k0, ATT_TK), :]
        s = s_ref[...]
        if bias is not None:
            s = (s.reshape(NSA_REP, ATT_TQ, ATT_TK) + bias[None]).reshape(rows, ATT_TK)
        m_prev = m_ref[br]
        m_new = jnp.maximum(m_prev, jnp.max(s, axis=-1, keepdims=True))
        alpha = jnp.exp(m_prev - m_new)
        p = jnp.exp(s - jnp.concatenate([m_new] * (ATT_TK // LANES), axis=1))
        acc_ref[br] = alpha * acc_ref[br] + _dot(p.astype(bf16), v)
        m_ref[br] = m_new

    def kpos(kt):
        return kt * ATT_TK + lax.broadcasted_iota(i32, (1, ATT_TK), 1)

    last = start // ATT_TK
    s_ref[...] = _dot(qx_ref[...], kx_ref[0, 0, 0])

    def sel_step(kt, _):
        consume(0, kt, vs_ref, None)
        s_ref[...] = _dot(qx_ref[...], kx_ref[0, 0, kt + 1])
        return 0

    lax.fori_loop(0, last, sel_step, 0)
    consume(0, last, vs_ref, jnp.where(kpos(last) <= tq, 0.0, MASK_NEG))

    q_all = q_ref[0].reshape(rows, HEAD_DIM)

    def win_step(kt, _):
        s_ref[...] = _dot(q_all, kw_ref[0, 0, kt])
        kp = kpos(kt)
        consume(1, kt, vw_ref, jnp.where(kp <= tq, jnp.where(kp > tq - WINDOW, 0.0, MASK_NEG), MASK_NEG))
        return 0

    lax.fori_loop(jnp.maximum(last - WINDOW // ATT_TK, 0), last + 1, win_step, 0)

    gates = jax.nn.sigmoid(gate_ref[...])
    per_group = 3 * NSA_REP
    shifted = gates
    for gg in range(1, NSA_KV_GROUPS):
        shifted = jnp.where(gi == gg, pltpu.roll(gates, LANES - gg * per_group, 1), shifted)
    for r in range(NSA_REP):
        def gate(br):
            c = r * 3 + br
            return shifted[:, c:c + 1]
        asel = acc_ref[0, r * ATT_TQ:(r + 1) * ATT_TQ]
        awin = acc_ref[1, r * ATT_TQ:(r + 1) * ATT_TQ]
        osel = asel[:, :HEAD_DIM] / asel[:, HEAD_DIM:HEAD_DIM + 1]
        owin = awin[:, :HEAD_DIM] / awin[:, HEAD_DIM:HEAD_DIM + 1]
        o_ref[:, r * HEAD_DIM:(r + 1) * HEAD_DIM] = gate(0) * oc_ref[0, r] + gate(1) * osel + gate(2) * owin


def _nsa_attn(q_r, kx, vs, kw, vw, selb, oc, proj, b, t):
    g = NSA_KV_GROUPS
    nq = t // ATT_TQ
    vspec = pl.BlockSpec((1, 1, t, LANES), lambda bi, gi, qi: (bi, gi, 0, 0))
    hspec = pl.BlockSpec((1, NSA_REP, ATT_TQ, HEAD_DIM), lambda bi, gi, qi: (bi, gi, qi, 0))
    return pl.pallas_call(
        _nsa_attn_kernel,
        grid=(b, g, nq),
        in_specs=[hspec,
                  pl.BlockSpec((1, 1, t // ATT_TK, KX_WIDTH, ATT_TK), lambda bi, gi, qi: (bi, gi, 0, 0, 0)),
                  vspec,
                  pl.BlockSpec((1, 1, t // ATT_TK, HEAD_DIM, ATT_TK), lambda bi, gi, qi: (bi, gi, 0, 0, 0)),
                  vspec,
                  pl.BlockSpec((1, 1, ATT_TQ, LANES), lambda bi, gi, qi: (bi, gi, qi, 0)),
                  hspec,
                  pl.BlockSpec((ATT_TQ, LANES), lambda bi, gi, qi: (bi * nq + qi, COL_SMALL // LANES))],
        out_specs=pl.BlockSpec((ATT_TQ, NSA_REP * HEAD_DIM), lambda bi, gi, qi: (bi * nq + qi, gi)),
        out_shape=jax.ShapeDtypeStruct((b * t, NSA_WIDTH), f32),
        scratch_shapes=[pltpu.VMEM((NSA_REP * ATT_TQ, KX_WIDTH), bf16),
                        pltpu.VMEM((NSA_REP * ATT_TQ, ATT_TK), f32),
                        pltpu.VMEM((2, NSA_REP * ATT_TQ, LANES), f32),
                        pltpu.VMEM((2, NSA_REP * ATT_TQ, LANES), f32)],
        compiler_params=_cparams(("parallel", "parallel", "arbitrary")),
        name="nsa_attn",
    )(q_r, kx, vs, kw, vw, selb, oc, proj)


HALO = 8


def _ssd_kernel(xbc_ref, z_ref, small_ref, dtt_ref, cw_ref, cb_ref, dtb_r_ref, dtb_c_ref,
                alog_r_ref, alog_c_ref, dskip_ref, nw_ref, o_ref, ext_ref, st_ref):
    c = pl.program_id(1)
    L = CHUNK

    @pl.when(c == 0)
    def _():
        ext_ref[0:HALO, :] = jnp.zeros((HALO, XBC_WIDTH), f32)
        st_ref[...] = jnp.zeros_like(st_ref)

    ext_ref[HALO:HALO + L, :] = xbc_ref[...]
    conv = cb_ref[...]
    for k in range(CONV_WIDTH):
        off = HALO - (CONV_WIDTH - 1) + k
        conv = conv + cw_ref[k:k + 1, :] * ext_ref[off:off + L, :]
    ext_ref[0:HALO, :] = ext_ref[L:L + HALO, :]
    act = conv * jax.nn.sigmoid(conv)
    xs = act[:, :SSM_WIDTH]
    bm = act[:, SSM_WIDTH:SSM_WIDTH + SSM_GROUPS * SSM_STATE]
    cm = act[:, SSM_WIDTH + SSM_GROUPS * SSM_STATE:]

    dt_c = jax.nn.softplus(small_ref[:, SMALL_DT_OFF:SMALL_DT_OFF + SSM_HEADS] + dtb_r_ref[...])
    dt_r = jax.nn.softplus(dtt_ref[0] + dtb_c_ref[...])
    a_r = -jnp.exp(alog_r_ref[...])
    a_c = -jnp.exp(alog_c_ref[...])
    row = lax.broadcasted_iota(i32, (L, L), 0)
    col = lax.broadcasted_iota(i32, (L, L), 1)
    causal = col <= row
    tri = jnp.where(causal, 1.0, 0.0)
    acs_c = _dot_hi(tri, dt_c * a_r)
    acs_r = _dot_hi(dt_r * a_c, jnp.where(row <= col, 1.0, 0.0))

    z = z_ref[...]
    ys = []
    for g in range(SSM_GROUPS):
        cg = cm[:, g * SSM_STATE:(g + 1) * SSM_STATE].astype(bf16)
        bg = bm[:, g * SSM_STATE:(g + 1) * SSM_STATE]
        cb = _dot_t(cg, bg.astype(bf16))
        bg_t = bg.T
        ssq = jnp.zeros((L, 1), f32)
        yg = []
        for r in range(SSM_REP):
            h = g * SSM_REP + r
            hs = slice(h * SSM_HEADDIM, (h + 1) * SSM_HEADDIM)
            a_col = acs_c[:, h:h + 1]
            a_row = acs_r[h:h + 1, :]
            dt_row = dt_r[h:h + 1, :]
            a_last = acs_r[h:h + 1, L - 1:L]
            seg = a_col - a_row
            decay = jnp.where(causal, jnp.exp(jnp.where(causal, seg, 0.0)), 0.0)
            w = cb * decay * dt_row
            x_h = xs[:, hs]
            xb = x_h.astype(bf16)
            st = st_ref[h]
            y = _dot(w.astype(bf16), xb)
            y = y + _dot(cg, st.astype(bf16)) * jnp.exp(a_col)
            y = y + dskip_ref[:, hs] * x_h
            bscaled = bg_t * (jnp.exp(a_last - a_row) * dt_row)
            st_ref[h] = jnp.exp(a_last) * st + _dot(bscaled.astype(bf16), xb)
            zh = z[:, hs]
            y = y * (zh * jax.nn.sigmoid(zh))
            ssq = ssq + jnp.sum(y * y, axis=-1, keepdims=True)
            yg.append(y)
        rs = lax.rsqrt(ssq / (SSM_REP * SSM_HEADDIM) + NORM_EPS)
        for r in range(SSM_REP):
            h = g * SSM_REP + r
            hs = slice(h * SSM_HEADDIM, (h + 1) * SSM_HEADDIM)
            o_ref[:, hs] = yg[r] * rs * nw_ref[:, hs]


def _ssd(proj, dtt, cw, cb, dtb_r, dtb_c, alog_r, alog_c, dskip, nw, b, t):
    nch = t // CHUNK
    row = lambda bi, ci: bi * nch + ci
    const2 = lambda bi, ci: (0, 0)
    return pl.pallas_call(
        _ssd_kernel,
        grid=(b, nch),
        in_specs=[pl.BlockSpec((CHUNK, XBC_WIDTH), lambda bi, ci: (row(bi, ci), COL_XBC // XBC_WIDTH)),
                  pl.BlockSpec((CHUNK, SSM_WIDTH), lambda bi, ci: (row(bi, ci), COL_Z // SSM_WIDTH)),
                  pl.BlockSpec((CHUNK, LANES), lambda bi, ci: (row(bi, ci), COL_SMALL // LANES)),
                  pl.BlockSpec((1, SSM_HEADS, CHUNK), lambda bi, ci: (bi, 0, ci)),
                  pl.BlockSpec((CONV_WIDTH, XBC_WIDTH), const2),
                  pl.BlockSpec((1, XBC_WIDTH), const2),
                  pl.BlockSpec((1, SSM_HEADS), const2),
                  pl.BlockSpec((SSM_HEADS, 1), const2),
                  pl.BlockSpec((1, SSM_HEADS), const2),
                  pl.BlockSpec((SSM_HEADS, 1), const2),
                  pl.BlockSpec((1, SSM_WIDTH), const2),
                  pl.BlockSpec((1, SSM_WIDTH), const2)],
        out_specs=pl.BlockSpec((CHUNK, SSM_WIDTH), lambda bi, ci: (row(bi, ci), 0)),
        out_shape=jax.ShapeDtypeStruct((b * t, SSM_WIDTH), f32),
        scratch_shapes=[pltpu.VMEM((HALO + CHUNK, XBC_WIDTH), f32),
                        pltpu.VMEM((SSM_HEADS, SSM_STATE, SSM_HEADDIM), f32)],
        compiler_params=_cparams(("parallel", "arbitrary")),
        name="ssd",
    )(proj, proj, proj, dtt, cw, cb, dtb_r, dtb_c, alog_r, alog_c, dskip, nw)


def _layer_norm(v, g, b):
    mu = jnp.mean(v, axis=-1, keepdims=True)
    d = v - mu
    var = jnp.mean(d * d, axis=-1, keepdims=True)
    return d * lax.rsqrt(var + NORM_EPS) * g + b


def _outproj_kernel(alpha, ya_ref, yb_ref, x_ref, wa_ref, wb_ref, g_ref, b_ref, wr_ref, br_ref,
                    h_ref, route_ref, cnt_ref):
    i = pl.program_id(0)
    tm = x_ref.shape[0]
    mix = _dot(ya_ref[...].astype(bf16), wa_ref[...]) + _dot(yb_ref[...].astype(bf16), wb_ref[...])
    h = _layer_norm(alpha * x_ref[...] + mix, g_ref[...], b_ref[...])
    h_ref[...] = h

    h_hi = h.astype(bf16)
    h_lo = (h - h_hi.astype(f32)).astype(bf16)
    t = _dot(h_hi, wr_ref[...])
    logits = t[:, :LANES] + t[:, LANES:] + _dot(h_lo, wr_ref[:, :LANES]) + br_ref[...]
    lane = lax.broadcasted_iota(i32, (tm, LANES), 1)
    ninf = -jnp.inf
    gmask = lane < N_EXPERT_GROUPS
    gl = jnp.where(gmask, logits, ninf)
    ge = jnp.where(gmask, jnp.exp(gl - jnp.max(gl, axis=-1, keepdims=True)), 0.0)
    pg = ge / jnp.sum(ge, axis=-1, keepdims=True)
    g_gate = jnp.max(pg, axis=-1, keepdims=True)
    g_sel = jnp.min(jnp.where(gmask & (pg == g_gate), lane, LANES), axis=-1, keepdims=True)
    lo = N_EXPERT_GROUPS + g_sel * EXPERTS_PER_GROUP
    emask = (lane >= lo) & (lane < lo + EXPERTS_PER_GROUP)
    el = jnp.where(emask, logits, ninf)
    ee = jnp.where(emask, jnp.exp(el - jnp.max(el, axis=-1, keepdims=True)), 0.0)
    pe = ee / jnp.sum(ee, axis=-1, keepdims=True)
    p0 = jnp.max(pe, axis=-1, keepdims=True)
    l0 = jnp.min(jnp.where(emask & (pe == p0), lane, LANES), axis=-1, keepdims=True)
    rest = jnp.where(emask & (lane != l0), pe, ninf)
    p1 = jnp.max(rest, axis=-1, keepdims=True)
    l1 = jnp.min(jnp.where(rest == p1, lane, LANES), axis=-1, keepdims=True)
    psum = p0 + p1
    w0 = g_gate * p0 / psum
    w1 = g_gate * p1 / psum
    e0 = l0 - N_EXPERT_GROUPS
    e1 = l1 - N_EXPERT_GROUPS

    @pl.when(i == 0)
    def _():
        cnt_ref[...] = jnp.zeros_like(cnt_ref)

    oh0 = lane == e0
    oh1 = lane == e1
    both = jnp.where(oh0, 1.0, 0.0) + jnp.where(oh1, 1.0, 0.0)
    r_i = lax.broadcasted_iota(i32, (tm, tm), 0)
    c_i = lax.broadcasted_iota(i32, (tm, tm), 1)
    strict = jnp.where(c_i < r_i, 1.0, 0.0).astype(bf16)
    before = _dot(strict, both.astype(bf16)) + cnt_ref[...]
    rank0 = jnp.sum(jnp.where(oh0, before, 0.0), axis=-1, keepdims=True)
    rank1 = jnp.sum(jnp.where(oh1, before, 0.0), axis=-1, keepdims=True)
    cnt_ref[...] = cnt_ref[...] + jnp.sum(both, axis=0, keepdims=True)

    out = jnp.where(lane == 0, e0.astype(f32), 0.0)
    out = jnp.where(lane == 1, e1.astype(f32), out)
    out = jnp.where(lane == 2, w0, out)
    out = jnp.where(lane == 3, w1, out)
    out = jnp.where(lane == 4, rank0, out)
    out = jnp.where(lane == 5, rank1, out)
    route_ref[...] = out


def _outproj(y_nsa, y_ssm, x2, wa, wb, g, bta, wr, br, alpha):
    n, d = x2.shape
    tm = OUT_TM
    const = lambda i: (0, 0)
    rowb = lambda i: (i, 0)
    return pl.pallas_call(
        functools.partial(_outproj_kernel, alpha),
        grid=(n // tm,),
        in_specs=[pl.BlockSpec((tm, NSA_WIDTH), rowb), pl.BlockSpec((tm, SSM_WIDTH), rowb),
                  pl.BlockSpec((tm, d), rowb),
                  pl.BlockSpec((NSA_WIDTH, d), const), pl.BlockSpec((SSM_WIDTH, d), const),
                  pl.BlockSpec((1, d), const), pl.BlockSpec((1, d), const),
                  pl.BlockSpec((d, 2 * LANES), const), pl.BlockSpec((1, LANES), const)],
        out_specs=[pl.BlockSpec((tm, d), rowb), pl.BlockSpec((tm, LANES), rowb),
                   pl.BlockSpec((1, LANES), const)],
        out_shape=[jax.ShapeDtypeStruct((n, d), f32), jax.ShapeDtypeStruct((n, LANES), f32),
                   jax.ShapeDtypeStruct((1, LANES), f32)],
        compiler_params=_cparams(("arbitrary",)),
        name="outproj",
    )(y_nsa, y_ssm, x2, wa, wb, g, bta, wr, br)


def _row_copy(src_ref, src_row, dst_ref, dst_row, sem):
    return pltpu.make_async_copy(src_ref.at[pl.ds(src_row, 1), :], dst_ref.at[pl.ds(dst_row, 1), :], sem)


def _dest_row(idx_ref, ps_ref, tm, k, r):
    return ps_ref[idx_ref[0, 0, k * tm + r]] + idx_ref[0, 0, (2 + k) * tm + r]


def _dispatch_kernel(idx_ref, ps_ref, zflag_ref, h_ref, xs_ref, zero_ref, sem, zsem):
    tm = h_ref.shape[0]
    nb = zflag_ref.shape[0]

    @pl.when(pl.program_id(0) == 0)
    def _():
        zero_ref[...] = jnp.zeros_like(zero_ref)

        def zblock(i):
            rows = pl.ds(pl.multiple_of(i * MOE_TM, MOE_TM), MOE_TM)
            return pltpu.make_async_copy(zero_ref, xs_ref.at[rows, :], zsem)

        def zstart(i, _):
            @pl.when(zflag_ref[i] != 0)
            def _():
                zblock(i).start()
            return 0

        def zwait(i, _):
            @pl.when(zflag_ref[i] != 0)
            def _():
                zblock(i).wait()
            return 0

        lax.fori_loop(0, nb, zstart, 0)
        lax.fori_loop(0, nb, zwait, 0)

    def issue(r, _):
        for k in range(2):
            _row_copy(h_ref, r, xs_ref, _dest_row(idx_ref, ps_ref, tm, k, r), sem).start()
        return 0

    lax.fori_loop(0, tm, issue, 0)

    def drain(r, _):
        for k in range(2):
            _row_copy(h_ref, 0, xs_ref, 0, sem).wait()
        return 0

    lax.fori_loop(0, tm, drain, 0)


def _dispatch(idx_t, pad_starts, zflag, h, p_rows):
    n, d = h.shape
    tm = DISP_TM
    return pl.pallas_call(
        _dispatch_kernel,
        grid=(n // tm,),
        in_specs=[pl.BlockSpec((1, 1, 4 * tm), lambda i: (i, 0, 0), memory_space=pltpu.SMEM),
                  pl.BlockSpec(memory_space=pltpu.SMEM),
                  pl.BlockSpec(memory_space=pltpu.SMEM),
                  pl.BlockSpec((tm, d), lambda i: (i, 0))],
        out_specs=pl.BlockSpec(memory_space=pl.ANY),
        out_shape=jax.ShapeDtypeStruct((p_rows, d), f32),
        scratch_shapes=[pltpu.VMEM((MOE_TM, d), f32), pltpu.SemaphoreType.DMA(()), pltpu.SemaphoreType.DMA(())],
        compiler_params=_cparams(("arbitrary",)),
        name="dispatch",
    )(idx_t, pad_starts, zflag, h)


def _experts_kernel(be_ref, nu_ref, xs_ref, wg_ref, wu_ref, wd_ref, y_ref, wgb_ref, wub_ref, wdb_ref):
    i = pl.program_id(0)
    e = be_ref[i]
    prev = be_ref[jnp.maximum(i - 1, 0)]

    @pl.when((i == 0) | (e != prev))
    def _():
        wgb_ref[...] = wg_ref[0].astype(bf16)
        wub_ref[...] = wu_ref[0].astype(bf16)
        wdb_ref[...] = wd_ref[0].astype(bf16)

    @pl.when(i < nu_ref[0])
    def _():
        xb = xs_ref[...].astype(bf16)
        gte = _dot(xb, wgb_ref[...])
        up = _dot(xb, wub_ref[...])
        act = gte * jax.nn.sigmoid(gte) * up
        y_ref[...] = _dot(act.astype(bf16), wdb_ref[...])

    @pl.when(i >= nu_ref[0])
    def _():
        y_ref[...] = jnp.zeros_like(y_ref)


def _experts(block_e, n_used, xs, w_gate, w_up, w_down):
    p_rows, d = xs.shape
    tm = MOE_TM
    nb = p_rows // tm
    de = w_gate.shape[-1]
    xmap = lambda i, be, nu: (jnp.maximum(jnp.minimum(i, nu[0] - 1), 0), 0)
    return pl.pallas_call(
        _experts_kernel,
        grid_spec=pltpu.PrefetchScalarGridSpec(
            num_scalar_prefetch=2,
            grid=(nb,),
            in_specs=[pl.BlockSpec((tm, d), xmap),
                      pl.BlockSpec((1, d, de), lambda i, be, nu: (be[i], 0, 0)),
                      pl.BlockSpec((1, d, de), lambda i, be, nu: (be[i], 0, 0)),
                      pl.BlockSpec((1, de, d), lambda i, be, nu: (be[i], 0, 0))],
            out_specs=pl.BlockSpec((tm, d), lambda i, be, nu: (i, 0)),
            scratch_shapes=[pltpu.VMEM((d, de), bf16), pltpu.VMEM((d, de), bf16), pltpu.VMEM((de, d), bf16)],
        ),
        out_shape=jax.ShapeDtypeStruct((p_rows, d), f32),
        compiler_params=_cparams(("arbitrary",)),
        name="experts",
    )(block_e, n_used, xs, w_gate, w_up, w_down)


def _combine_kernel(alpha, idx_ref, ps_ref, route_ref, h_ref, g_ref, b_ref, y_ref, o_ref, buf_ref, sem):
    tm = h_ref.shape[0]

    def issue(r, _):
        for k in range(2):
            _row_copy(y_ref, _dest_row(idx_ref, ps_ref, tm, k, r), buf_ref.at[k], r, sem).start()
        return 0

    lax.fori_loop(0, tm, issue, 0)

    def drain(r, _):
        for k in range(2):
            _row_copy(y_ref, 0, buf_ref.at[k], 0, sem).wait()
        return 0

    lax.fori_loop(0, tm, drain, 0)

    route = route_ref[...]
    ffn = route[:, 2:3] * buf_ref[0] + route[:, 3:4] * buf_ref[1]
    o_ref[...] = _layer_norm(alpha * h_ref[...] + ffn, g_ref[...], b_ref[...])


def _combine(idx_t, pad_starts, route, h, g, bta, y, alpha):
    n, d = h.shape
    tm = COMB_TM
    const = lambda i: (0, 0)
    return pl.pallas_call(
        functools.partial(_combine_kernel, alpha),
        grid=(n // tm,),
        in_specs=[pl.BlockSpec((1, 1, 4 * tm), lambda i: (i, 0, 0), memory_space=pltpu.SMEM),
                  pl.BlockSpec(memory_space=pltpu.SMEM),
                  pl.BlockSpec((tm, LANES), lambda i: (i, 0)),
                  pl.BlockSpec((tm, d), lambda i: (i, 0)),
                  pl.BlockSpec((1, d), const), pl.BlockSpec((1, d), const),
                  pl.BlockSpec(memory_space=pl.ANY)],
        out_specs=pl.BlockSpec((tm, d), lambda i: (i, 0)),
        out_shape=jax.ShapeDtypeStruct((n, d), f32),
        scratch_shapes=[pltpu.VMEM((2, tm, d), f32), pltpu.SemaphoreType.DMA(())],
        compiler_params=_cparams(("arbitrary",)),
        name="combine",
    )(idx_t, pad_starts, route, h, g, bta, y)


def _tile_idx(idx, tm):
    n = idx.shape[1]
    return idx.reshape(4, n // tm, tm).transpose(1, 0, 2).reshape(n // tm, 1, 4 * tm)


def _layer(x, positions, w_in, cmp_k_pe, cmp_k_w1, cmp_k_b1, cmp_k_w2, cmp_v_pe, cmp_v_w1, cmp_v_b1, cmp_v_w2,
           conv_w, conv_b, dt_bias, a_log, d_skip, ssm_norm_w, w_out, ln1_g, ln1_b,
           w_router_group, b_router_group, w_router_expert, b_router_expert, w_gate, w_up, w_down, ln2_g, ln2_b,
           alpha):
    b, t, d = x.shape
    n = b * t
    x2 = x.reshape(n, d)

    c0 = NSA_WIDTH
    c1 = c0 + 6 * KV_WIDTH
    c2 = c1 + 3 * NSA_HEADS
    c3 = c2 + SSM_WIDTH
    c4 = c3 + XBC_WIDTH
    w_small = jnp.concatenate([w_in[:, c1:c2], w_in[:, c4:], jnp.zeros((d, LANES - 3 * NSA_HEADS - SSM_HEADS), f32)], axis=1)
    w_cat = jnp.concatenate([w_in[:, :c0], w_in[:, c2:c3], w_in[:, c3:c4], w_in[:, c0:c1], w_small,
                             jnp.zeros((d, PROJ_COLS - COL_SMALL - LANES), f32)], axis=1).astype(bf16)
    proj = _proj(x2, w_cat)

    lane = np.arange(LANES) % HEAD_DIM
    inv_freq = ROPE_THETA ** (-jnp.arange(0, ROT_DIM, 2, dtype=f32) / ROT_DIM)
    invf = jnp.where(lane < ROT_DIM, inv_freq[lane % (ROT_DIM // 2)], 0.0).astype(f32)[None, :]
    pos128 = jnp.broadcast_to(positions.reshape(n, 1), (n, LANES))
    q_r, k_cmp, v_cmp, k_sel, v_sel, k_win, v_win = _nsa_prep(proj, pos128, invf, b, t)

    nc = t // CMP_STRIDE
    half_w = CMP_STRIDE * HEAD_DIM
    a = jnp.stack([k_cmp, v_cmp]).reshape(2, b * NSA_KV_GROUPS, nc, half_w)
    pe = jnp.stack([cmp_k_pe, cmp_v_pe]).reshape(2, 2, 1, half_w)
    w1 = jnp.stack([cmp_k_w1, cmp_v_w1]).reshape(2, 2, half_w, CMP_HIDDEN).astype(bf16)
    b1 = jnp.stack([cmp_k_b1, cmp_v_b1]).reshape(2, 1, CMP_HIDDEN)
    w2 = jnp.pad(jnp.stack([cmp_k_w2, cmp_v_w2]), ((0, 0), (0, 0), (0, LANES - HEAD_DIM))).astype(bf16)
    cend = jnp.minimum(jnp.arange(nc) * CMP_STRIDE + CMP_BLOCK - 1, t - 1)
    posc = jnp.broadcast_to(positions[:, cend][:, :, None], (b, nc, LANES))
    kvc, kvc_t = _cmp_mlp(a, pe, w1, b1, w2, posc, invf, b)

    c_start = np.arange(nc)[:, None] * CMP_STRIDE
    s_start = np.arange(LANES)[None, :] * SEL_BLOCK
    cover = ((c_start < s_start + SEL_BLOCK) & (c_start + CMP_BLOCK > s_start)
             & (np.arange(nc)[:, None] < nc - 1) & (np.arange(LANES)[None, :] < t // SEL_BLOCK))
    cover = jnp.asarray(cover, bf16)
    o_cmp, selb = _cmp_attn(q_r, kvc, kvc_t, cover, b, t)

    y_nsa = _nsa_attn(q_r, k_sel, v_sel, k_win, v_win, selb, o_cmp, proj, b, t)

    dt_raw = proj[:, COL_SMALL + SMALL_DT_OFF:COL_SMALL + SMALL_DT_OFF + SSM_HEADS]
    dtt = dt_raw.reshape(b, t, SSM_HEADS).transpose(0, 2, 1)
    y_ssm = _ssd(proj, dtt, conv_w.reshape(CONV_WIDTH, XBC_WIDTH), conv_b.reshape(1, XBC_WIDTH),
                 dt_bias.reshape(1, SSM_HEADS), dt_bias.reshape(SSM_HEADS, 1),
                 a_log.reshape(1, SSM_HEADS), a_log.reshape(SSM_HEADS, 1),
                 jnp.repeat(d_skip, SSM_HEADDIM).reshape(1, SSM_WIDTH), ssm_norm_w.reshape(1, SSM_WIDTH), b, t)

    wr = jnp.concatenate([w_router_group, w_router_expert,
                          jnp.zeros((d, LANES - N_EXPERT_GROUPS - N_EXPERTS), f32)], axis=1)
    br = jnp.concatenate([b_router_group, b_router_expert,
                          jnp.zeros((LANES - N_EXPERT_GROUPS - N_EXPERTS,), f32)])[None, :]
    wr_hi = wr.astype(bf16)
    wr = jnp.concatenate([wr_hi, (wr - wr_hi.astype(f32)).astype(bf16)], axis=1)
    wo = w_out.astype(bf16)
    h, route, counts = _outproj(y_nsa, y_ssm, x2, wo[:NSA_WIDTH], wo[NSA_WIDTH:], ln1_g[None, :], ln1_b[None, :],
                                wr, br, alpha)

    cnt = counts[0, :N_EXPERTS].astype(i32)
    padded = (cnt + MOE_TM - 1) // MOE_TM * MOE_TM
    pad_ends = jnp.cumsum(padded)
    pad_starts = pad_ends - padded
    idx = jnp.concatenate([route[:, 0:2], route[:, 4:6]], axis=1).astype(i32).T
    p_rows = 2 * n + N_EXPERTS * MOE_TM
    nb = p_rows // MOE_TM
    block_e = jnp.minimum(jnp.sum(jnp.arange(nb, dtype=i32)[:, None] * MOE_TM >= pad_ends[None, :], axis=-1),
                          N_EXPERTS - 1).astype(i32)
    n_used = (pad_ends[-1] // MOE_TM).astype(i32).reshape(1)
    blk = jnp.arange(nb, dtype=i32)
    last_of_expert = jnp.any((blk[:, None] + 1) * MOE_TM == pad_ends[None, :], axis=-1)
    zflag = (last_of_expert | (blk >= n_used[0])).astype(i32)
    block_e = jnp.where(blk < n_used[0], block_e, block_e[jnp.maximum(n_used[0] - 1, 0)])

    xs = _dispatch(_tile_idx(idx, DISP_TM), pad_starts, zflag, h, p_rows)
    y = _experts(block_e, n_used, xs, w_gate, w_up, w_down)
    out = _combine(_tile_idx(idx, COMB_TM), pad_starts, route, h, ln2_g[None, :], ln2_b[None, :], y, alpha)
    return out.reshape(b, t, d)


def kernel(x, positions, w_in, cmp_k_pe, cmp_k_w1, cmp_k_b1, cmp_k_w2, cmp_v_pe, cmp_v_w1, cmp_v_b1, cmp_v_w2, conv_w, conv_b, dt_bias, a_log, d_skip, ssm_norm_w, w_out, ln1_g, ln1_b, w_router_group, b_router_group, w_router_expert, b_router_expert, w_gate, w_up, w_down, ln2_g, ln2_b):
    depth = w_in.shape[0]
    alpha = (2 * depth) ** 0.25
    params = (w_in, cmp_k_pe, cmp_k_w1, cmp_k_b1, cmp_k_w2, cmp_v_pe, cmp_v_w1, cmp_v_b1, cmp_v_w2, conv_w, conv_b,
              dt_bias, a_log, d_skip, ssm_norm_w, w_out, ln1_g, ln1_b, w_router_group, b_router_group,
              w_router_expert, b_router_expert, w_gate, w_up, w_down, ln2_g, ln2_b)
    for l in range(depth):
        x = _layer(x, positions, *[p[l] for p in params], alpha)
    return x
```

```python
import functools
import math

import jax
import jax.numpy as jnp
import numpy as np
from jax import lax
from jax.experimental import pallas as pl
from jax.experimental.pallas import tpu as pltpu

f32 = jnp.float32
bf16 = jnp.bfloat16
i32 = jnp.int32

HEAD_DIM = 64
NSA_HEADS = 16
NSA_KV_GROUPS = 2
NSA_REP = NSA_HEADS // NSA_KV_GROUPS
NSA_WIDTH = NSA_HEADS * HEAD_DIM
KV_WIDTH = NSA_KV_GROUPS * HEAD_DIM
CMP_BLOCK = 32
CMP_STRIDE = 16
CMP_HIDDEN = 256
SEL_BLOCK = 64
SEL_TOPK = 16
WINDOW = 512
FORCED_SCORE = 1.0e4
SSM_HEADDIM = 64
SSM_HEADS = 16
SSM_WIDTH = SSM_HEADS * SSM_HEADDIM
SSM_GROUPS = 4
SSM_REP = SSM_HEADS // SSM_GROUPS
SSM_STATE = 128
CONV_WIDTH = 4
CHUNK = 256
XBC_WIDTH = SSM_WIDTH + 2 * SSM_GROUPS * SSM_STATE
ROPE_THETA = 500000.0
ROT_DIM = HEAD_DIM // 4
N_EXPERT_GROUPS = 4
EXPERTS_PER_GROUP = 8
N_EXPERTS = N_EXPERT_GROUPS * EXPERTS_PER_GROUP
D_EXPERT = 512
NORM_EPS = 1e-5

LANES = 128
MASK_NEG = -1.0e30

PROJ_TM = 1024
PROJ_TN = 512
PREP_TM = 512
ATT_TQ = 256
ATT_TK = 512
WIN_TK = 256
KX_WIDTH = 2 * LANES
OUT_TM = 256
MOE_TM = 256
DISP_TM = 256
COMB_TM = 128
VMEM_LIMIT = 56 * 1024 * 1024

COL_Q = 0
COL_Z = NSA_WIDTH
COL_XBC = COL_Z + SSM_WIDTH
COL_KV = COL_XBC + XBC_WIDTH
COL_SMALL = COL_KV + 6 * KV_WIDTH
PROJ_COLS = 5120
SMALL_DT_OFF = 3 * NSA_HEADS


def _cparams(sem, vmem=VMEM_LIMIT):
    return pltpu.CompilerParams(dimension_semantics=sem, vmem_limit_bytes=vmem)


def _dot(a, b):
    return jnp.dot(a, b, preferred_element_type=f32)


def _dot_t(a, b):
    return lax.dot_general(a, b, (((1,), (1,)), ((), ())), preferred_element_type=f32)


def _dot_hi(a, b):
    return jnp.dot(a, b, preferred_element_type=f32, precision=lax.Precision.HIGHEST)


def _proj_kernel(x_ref, w_ref, o_ref, xb_ref):
    @pl.when(pl.program_id(1) == 0)
    def _():
        xb_ref[...] = x_ref[...].astype(bf16)

    o_ref[...] = _dot(xb_ref[...], w_ref[...])


def _proj(x2, w_cat):
    n, d = x2.shape
    cols = w_cat.shape[1]
    tm = min(PROJ_TM, n)
    return pl.pallas_call(
        _proj_kernel,
        grid=(n // tm, cols // PROJ_TN),
        in_specs=[pl.BlockSpec((tm, d), lambda i, j: (i, 0)),
                  pl.BlockSpec((d, PROJ_TN), lambda i, j: (0, j))],
        out_specs=pl.BlockSpec((tm, PROJ_TN), lambda i, j: (i, j)),
        out_shape=jax.ShapeDtypeStruct((n, cols), f32),
        scratch_shapes=[pltpu.VMEM((tm, d), bf16)],
        compiler_params=_cparams(("parallel", "arbitrary")),
        name="proj",
    )(x2, w_cat)


def _rope_tables(pos_i32, invf):
    ang = pos_i32.astype(f32) * invf
    return jnp.cos(ang), jnp.sin(ang)


def _rope128(x, cos, sin):
    half = ROT_DIM // 2
    d = lax.broadcasted_iota(i32, x.shape, 1) % HEAD_DIM
    up = pltpu.roll(x, LANES - half, 1)
    dn = pltpu.roll(x, half, 1)
    rot = jnp.where(d < half, -up, dn)
    return x * cos + rot * sin


def _nsa_prep_kernel(pos_ref, invf_ref, q_ref, kc_ref, vc_ref, ks_ref, vs_ref, kw_ref, vw_ref,
                     qo_ref, kco_ref, vco_ref, kso_ref, vso_ref, kwo_ref, vwo_ref):
    cos, sin = _rope_tables(pos_ref[...], invf_ref[...])
    scale = HEAD_DIM ** -0.5
    for c in range(NSA_WIDTH // LANES):
        t = _rope128(q_ref[:, c * LANES:(c + 1) * LANES], cos, sin) * scale
        qo_ref[0, 2 * c] = t[:, :HEAD_DIM].astype(bf16)
        qo_ref[0, 2 * c + 1] = t[:, HEAD_DIM:].astype(bf16)

    def split(src, dst, rope, dt):
        t = src[...]
        if rope:
            t = _rope128(t, cos, sin)
        for g in range(NSA_KV_GROUPS):
            dst[0, g] = t[:, g * HEAD_DIM:(g + 1) * HEAD_DIM].astype(dt)

    split(kc_ref, kco_ref, False, f32)
    split(vc_ref, vco_ref, False, f32)

    tm = pos_ref.shape[0]

    def values_with_ones(src, dst):
        t = src[...]
        lane = lax.broadcasted_iota(i32, (tm, HEAD_DIM), 1)
        tail = jnp.where(lane == 0, 1.0, 0.0).astype(bf16)
        for g in range(NSA_KV_GROUPS):
            vg = t[:, g * HEAD_DIM:(g + 1) * HEAD_DIM].astype(bf16)
            dst[0, g] = jnp.concatenate([vg, tail], axis=1)

    values_with_ones(vs_ref, vso_ref)
    values_with_ones(vw_ref, vwo_ref)

    kw_t = _rope128(kw_ref[...], cos, sin).T
    ks_t = _rope128(ks_ref[...], cos, sin).T
    for g in range(NSA_KV_GROUPS):
        for c in range(tm // WIN_TK):
            kwo_ref[0, g, c] = kw_t[g * HEAD_DIM:(g + 1) * HEAD_DIM, c * WIN_TK:(c + 1) * WIN_TK].astype(bf16)
    blk = lax.broadcasted_iota(i32, (LANES, tm), 0)
    tok = pl.program_id(1) * tm + lax.broadcasted_iota(i32, (LANES, tm), 1)
    onehot_t = jnp.where(tok // SEL_BLOCK == blk, 1.0, 0.0).astype(bf16)
    for g in range(NSA_KV_GROUPS):
        kso_ref[0, g, 0, 0:LANES, :] = onehot_t
        kg = ks_t[g * HEAD_DIM:(g + 1) * HEAD_DIM, :].astype(bf16)
        kso_ref[0, g, 0, LANES:2 * LANES, :] = jnp.concatenate([kg, jnp.zeros_like(kg)], axis=0)


def _nsa_prep(proj, pos128, invf, b, t):
    tm = PREP_TM
    assert tm == ATT_TK
    nt = t // tm
    row = lambda bi, ti: (bi * nt + ti, 0)
    kv0 = COL_KV // LANES
    in_specs = [pl.BlockSpec((tm, LANES), row),
                pl.BlockSpec((1, LANES), lambda bi, ti: (0, 0)),
                pl.BlockSpec((tm, NSA_WIDTH), lambda bi, ti: (bi * nt + ti, COL_Q // NSA_WIDTH))]
    for k in range(6):
        in_specs.append(pl.BlockSpec((tm, LANES), functools.partial(lambda bi, ti, k: (bi * nt + ti, kv0 + k), k=k)))
    head = lambda bi, ti: (bi, 0, ti, 0)
    tile5 = lambda bi, ti: (bi, 0, ti, 0, 0)
    g = NSA_KV_GROUPS
    out_specs = [pl.BlockSpec((1, NSA_HEADS, tm, HEAD_DIM), head),
                 pl.BlockSpec((1, g, tm, HEAD_DIM), head), pl.BlockSpec((1, g, tm, HEAD_DIM), head),
                 pl.BlockSpec((1, g, 1, KX_WIDTH, tm), tile5), pl.BlockSpec((1, g, tm, LANES), head),
                 pl.BlockSpec((1, g, tm // WIN_TK, HEAD_DIM, WIN_TK), tile5), pl.BlockSpec((1, g, tm, LANES), head)]
    out_shape = [jax.ShapeDtypeStruct((b, NSA_HEADS, t, HEAD_DIM), bf16),
                 jax.ShapeDtypeStruct((b, g, t, HEAD_DIM), f32), jax.ShapeDtypeStruct((b, g, t, HEAD_DIM), f32),
                 jax.ShapeDtypeStruct((b, g, nt, KX_WIDTH, tm), bf16), jax.ShapeDtypeStruct((b, g, t, LANES), bf16),
                 jax.ShapeDtypeStruct((b, g, t // WIN_TK, HEAD_DIM, WIN_TK), bf16),
                 jax.ShapeDtypeStruct((b, g, t, LANES), bf16)]
    return pl.pallas_call(
        _nsa_prep_kernel,
        grid=(b, nt),
        in_specs=in_specs,
        out_specs=out_specs,
        out_shape=out_shape,
        compiler_params=_cparams(("parallel", "parallel")),
        name="nsa_prep",
    )(pos128, invf, proj, proj, proj, proj, proj, proj, proj)


def _cmp_mlp_kernel(a_ref, pe_ref, w1_ref, b1_ref, w2_ref, pos_ref, invf_ref, o_ref, ot_ref):
    kind = pl.program_id(0)
    a = a_ref[0, 0]
    nc = a.shape[0]
    u = _dot((a + pe_ref[0, 0]).astype(bf16), w1_ref[0, 0])
    v = _dot((a + pe_ref[0, 1]).astype(bf16), w1_ref[0, 1])
    v_next = pltpu.roll(v, nc - 1, 0)
    hid = jax.nn.gelu(u + v_next + b1_ref[0])
    out = _dot(hid.astype(bf16), w2_ref[0])
    cos, sin = _rope_tables(pos_ref[0], invf_ref[...])
    roped = _rope128(out, cos, sin)
    out = jnp.where(kind == 0, roped, out)
    o_ref[0, 0] = out[:, :HEAD_DIM].astype(bf16)
    ot_ref[0, 0] = out.T[:HEAD_DIM, :].astype(bf16)


def _cmp_mlp(a, pe, w1, b1, w2, posc, invf, b):
    _, bg, nc, hw = a.shape
    g = bg // b
    return pl.pallas_call(
        _cmp_mlp_kernel,
        grid=(2, bg),
        in_specs=[pl.BlockSpec((1, 1, nc, hw), lambda k, i: (k, i, 0, 0)),
                  pl.BlockSpec((1, 2, 1, hw), lambda k, i: (k, 0, 0, 0)),
                  pl.BlockSpec((1, 2, hw, CMP_HIDDEN), lambda k, i: (k, 0, 0, 0)),
                  pl.BlockSpec((1, 1, CMP_HIDDEN), lambda k, i: (k, 0, 0)),
                  pl.BlockSpec((1, CMP_HIDDEN, LANES), lambda k, i: (k, 0, 0)),
                  pl.BlockSpec((1, nc, LANES), lambda k, i: (i // g, 0, 0)),
                  pl.BlockSpec((1, LANES), lambda k, i: (0, 0))],
        out_specs=[pl.BlockSpec((1, 1, nc, HEAD_DIM), lambda k, i: (k, i, 0, 0)),
                   pl.BlockSpec((1, 1, HEAD_DIM, nc), lambda k, i: (k, i, 0, 0))],
        out_shape=[jax.ShapeDtypeStruct((2, bg, nc, HEAD_DIM), bf16),
                   jax.ShapeDtypeStruct((2, bg, HEAD_DIM, nc), bf16)],
        compiler_params=_cparams(("parallel", "parallel")),
        name="cmp_mlp",
    )(a, pe, w1, b1, w2, posc, invf)


def _cmp_attn_kernel(q_ref, kct_ref, vc_ref, cover_ref, oc_ref, sel_ref, imp_ref):
    qi = pl.program_id(2)
    nc = vc_ref.shape[2]
    rows = NSA_REP * ATT_TQ
    tq = qi * ATT_TQ + lax.broadcasted_iota(i32, (ATT_TQ, 1), 0)
    row_live = jnp.where(tq >= CMP_BLOCK - 1, 1.0, 0.0)
    tiny = jnp.finfo(f32).tiny

    def attend(ncols):
        kct = kct_ref[0, 0, :, :ncols]
        vc = vc_ref[0, 0, :ncols, :]
        cend = lax.broadcasted_iota(i32, (1, ncols), 1) * CMP_STRIDE + (CMP_BLOCK - 1)
        bias = jnp.where(cend <= tq, 0.0, MASK_NEG)
        s = _dot(q_ref[0].reshape(rows, HEAD_DIM), kct).reshape(NSA_REP, ATT_TQ, ncols) + bias[None]
        e = jnp.exp(s - jnp.max(s, axis=-1, keepdims=True)) * row_live[None]
        p = e / jnp.maximum(jnp.sum(e, axis=-1, keepdims=True), tiny)
        oc_ref[0] = _dot(p.reshape(rows, ncols).astype(bf16), vc).reshape(NSA_REP, ATT_TQ, HEAD_DIM)
        psum = jnp.sum(p, axis=0)
        hi = psum.astype(bf16)
        lo = (psum - hi.astype(f32)).astype(bf16)
        cover = cover_ref[:ncols, :]
        imp_ref[...] = _dot(hi, cover) + _dot(lo, cover)

    need = (qi + 1) * (ATT_TQ // CMP_STRIDE)
    for ncols in range(LANES, nc + 1, LANES):
        @pl.when((need > ncols - LANES) & (need <= ncols))
        def _():
            attend(ncols)

    imp = imp_ref[...]
    j = lax.broadcasted_iota(i32, (ATT_TQ, LANES), 1)
    cur = tq // SEL_BLOCK
    forced = (j == 0) | (j == cur) | (j == cur - 1)
    valid = j * SEL_BLOCK <= tq
    imp = jnp.where(valid, jnp.where(forced, FORCED_SCORE, imp), -FORCED_SCORE)

    jt = lax.broadcasted_iota(i32, (LANES, ATT_TQ), 0)

    def pick(_, carry):
        work, sel = carry
        m = jnp.max(work, axis=0, keepdims=True)
        first = jnp.min(jnp.where(work == m, jt, LANES), axis=0, keepdims=True)
        hit = jt == first
        return jnp.where(hit, -jnp.inf, work), jnp.where(hit, 1.0, sel)

    _, sel_t = lax.fori_loop(0, SEL_TOPK, pick, (imp.T, jnp.zeros((LANES, ATT_TQ), f32)))
    sel_ref[0, 0] = jnp.where(valid, jnp.where(sel_t.T > 0.0, 0.0, MASK_NEG), MASK_NEG).astype(bf16)


def _cmp_attn(q_r, kvc, kvc_t, cover, b, t):
    g = NSA_KV_GROUPS
    nc = kvc.shape[2]
    nq = t // ATT_TQ
    return pl.pallas_call(
        _cmp_attn_kernel,
        grid=(b, g, nq),
        in_specs=[pl.BlockSpec((1, NSA_REP, ATT_TQ, HEAD_DIM), lambda bi, gi, qi: (bi, gi, qi, 0)),
                  pl.BlockSpec((1, 1, HEAD_DIM, nc), lambda bi, gi, qi: (0, bi * g + gi, 0, 0)),
                  pl.BlockSpec((1, 1, nc, HEAD_DIM), lambda bi, gi, qi: (1, bi * g + gi, 0, 0)),
                  pl.BlockSpec((nc, LANES), lambda bi, gi, qi: (0, 0))],
        out_specs=[pl.BlockSpec((1, NSA_REP, ATT_TQ, HEAD_DIM), lambda bi, gi, qi: (bi, gi, qi, 0)),
                   pl.BlockSpec((1, 1, ATT_TQ, LANES), lambda bi, gi, qi: (bi, gi, qi, 0))],
        out_shape=[jax.ShapeDtypeStruct((b, NSA_HEADS, t, HEAD_DIM), f32),
                   jax.ShapeDtypeStruct((b, g, t, LANES), bf16)],
        scratch_shapes=[pltpu.VMEM((ATT_TQ, LANES), f32)],
        compiler_params=_cparams(("parallel", "parallel", "parallel")),
        name="cmp_attn",
    )(q_r, kvc_t, kvc, cover)


def _nsa_attn_kernel(q_ref, kx_ref, vs_ref, kw_ref, vw_ref, sel_ref, oc_ref, gate_ref,
                     o_ref, qx_ref, s_ref, m_ref, acc_ref, gx_ref, yw_ref):
    gi = pl.program_id(1)
    qi = pl.program_id(2)
    start = qi * ATT_TQ
    tq = start + lax.broadcasted_iota(i32, (ATT_TQ, 1), 0)

    selb = sel_ref[0, 0]
    for r in range(NSA_REP):
        qx_ref[r * ATT_TQ:(r + 1) * ATT_TQ, 0:LANES] = selb
        qr = q_ref[0, r]
        qx_ref[r * ATT_TQ:(r + 1) * ATT_TQ, LANES:2 * LANES] = jnp.concatenate([qr, jnp.zeros_like(qr)], axis=1)
    rows = NSA_REP * ATT_TQ

    m_ref[...] = jnp.full(m_ref.shape, MASK_NEG, f32)
    acc_ref[...] = jnp.zeros(acc_ref.shape, f32)

    def consume(kt, v_ref, bias):
        k0 = pl.multiple_of(kt * ATT_TK, ATT_TK)
        v = v_ref[0, 0, pl.ds(k0, ATT_TK), :]
        s = s_ref[...]
        if bias is not None:
            s = (s.reshape(NSA_REP, ATT_TQ, ATT_TK) + bias[None]).reshape(rows, ATT_TK)
        m_prev = m_ref[...]
        m_new = jnp.maximum(m_prev, jnp.max(s, axis=-1, keepdims=True))
        alpha = jnp.exp(m_prev - m_new)
        p = jnp.exp(s - jnp.concatenate([m_new] * (ATT_TK // LANES), axis=1))
        acc_ref[...] = alpha * acc_ref[...] + _dot(p.astype(bf16), v)
        m_ref[...] = m_new

    def kpos(kt):
        return kt * ATT_TK + lax.broadcasted_iota(i32, (1, ATT_TK), 1)

    last = start // ATT_TK
    s_ref[...] = _dot(qx_ref[...], kx_ref[0, 0, 0])

    def sel_step(kt, _):
        consume(kt, vs_ref, None)
        s_ref[...] = _dot(qx_ref[...], kx_ref[0, 0, kt + 1])
        return 0

    n_win = (WINDOW + ATT_TQ) // WIN_TK
    w_first = jnp.maximum(start // WIN_TK - WINDOW // WIN_TK, 0)
    kw = jnp.concatenate([kw_ref[0, 0, w_first + i] for i in range(n_win)], axis=1)
    w0 = pl.multiple_of(w_first * WIN_TK, WIN_TK)
    vw = vw_ref[0, 0, pl.ds(w0, n_win * WIN_TK), :]
    kp = w0 + lax.broadcasted_iota(i32, (1, n_win * WIN_TK), 1)
    wbias = jnp.where(kp <= tq, jnp.where(kp > tq - WINDOW, 0.0, MASK_NEG), MASK_NEG)
    hh = NSA_REP // 2
    gates = jax.nn.sigmoid(gate_ref[...])
    per_group = 3 * NSA_REP
    shifted = gates
    for gg in range(1, NSA_KV_GROUPS):
        shifted = jnp.where(gi == gg, pltpu.roll(gates, LANES - gg * per_group, 1), shifted)

    def gate(r, br):
        c = r * 3 + br
        return jnp.broadcast_to(shifted[:, c:c + 1], (ATT_TQ, HEAD_DIM))

    for half in range(2):
        qh = q_ref[0, half * hh:(half + 1) * hh].reshape(hh * ATT_TQ, HEAD_DIM)
        sw = _dot(qh, kw).reshape(hh, ATT_TQ, n_win * WIN_TK) + wbias[None]
        sw = sw.reshape(hh * ATT_TQ, n_win * WIN_TK)
        pw = jnp.exp(sw - jnp.max(sw, axis=-1, keepdims=True))
        aw = _dot(pw.astype(bf16), vw)
        for i in range(hh):
            r = half * hh + i
            a = aw[i * ATT_TQ:(i + 1) * ATT_TQ]
            yw_ref[r] = a[:, :HEAD_DIM] * (gate(r, 2) / a[:, HEAD_DIM:HEAD_DIM + 1])
            gx_ref[0, r] = gate(r, 0)
            gx_ref[1, r] = gate(r, 1)

    lax.fori_loop(0, last, sel_step, 0)
    consume(last, vs_ref, jnp.where(kpos(last) <= tq, 0.0, MASK_NEG))

    for r in range(NSA_REP):
        asel = acc_ref[r * ATT_TQ:(r + 1) * ATT_TQ]
        osel = asel[:, :HEAD_DIM] / asel[:, HEAD_DIM:HEAD_DIM + 1]
        o_ref[:, r * HEAD_DIM:(r + 1) * HEAD_DIM] = gx_ref[0, r] * oc_ref[0, r] + gx_ref[1, r] * osel + yw_ref[r]


def _nsa_attn(q_r, kx, vs, kw, vw, selb, oc, proj, b, t):
    g = NSA_KV_GROUPS
    nq = t // ATT_TQ
    vspec = pl.BlockSpec((1, 1, t, LANES), lambda bi, gi, qi: (bi, gi, 0, 0))
    hspec = pl.BlockSpec((1, NSA_REP, ATT_TQ, HEAD_DIM), lambda bi, gi, qi: (bi, gi, qi, 0))
    return pl.pallas_call(
        _nsa_attn_kernel,
        grid=(b, g, nq),
        in_specs=[hspec,
                  pl.BlockSpec((1, 1, t // ATT_TK, KX_WIDTH, ATT_TK), lambda bi, gi, qi: (bi, gi, 0, 0, 0)),
                  vspec,
                  pl.BlockSpec((1, 1, t // WIN_TK, HEAD_DIM, WIN_TK), lambda bi, gi, qi: (bi, gi, 0, 0, 0)),
                  vspec,
                  pl.BlockSpec((1, 1, ATT_TQ, LANES), lambda bi, gi, qi: (bi, gi, qi, 0)),
                  hspec,
                  pl.BlockSpec((ATT_TQ, LANES), lambda bi, gi, qi: (bi * nq + qi, COL_SMALL // LANES))],
        out_specs=pl.BlockSpec((ATT_TQ, NSA_REP * HEAD_DIM), lambda bi, gi, qi: (bi * nq + qi, gi)),
        out_shape=jax.ShapeDtypeStruct((b * t, NSA_WIDTH), f32),
        scratch_shapes=[pltpu.VMEM((NSA_REP * ATT_TQ, KX_WIDTH), bf16),
                        pltpu.VMEM((NSA_REP * ATT_TQ, ATT_TK), f32),
                        pltpu.VMEM((NSA_REP * ATT_TQ, LANES), f32),
                        pltpu.VMEM((NSA_REP * ATT_TQ, LANES), f32),
                        pltpu.VMEM((2, NSA_REP, ATT_TQ, HEAD_DIM), f32),
                        pltpu.VMEM((NSA_REP, ATT_TQ, HEAD_DIM), f32)],
        compiler_params=_cparams(("parallel", "parallel", "arbitrary")),
        name="nsa_attn",
    )(q_r, kx, vs, kw, vw, selb, oc, proj)


HALO = 8


def _ssd_kernel(xbc_ref, z_ref, small_ref, dtt_ref, cw_ref, cb_ref, dtb_r_ref, dtb_c_ref,
                alog_r_ref, alog_c_ref, dskip_ref, nw_ref, o_ref, ext_ref, st_ref):
    c = pl.program_id(1)
    L = CHUNK

    @pl.when(c == 0)
    def _():
        ext_ref[0:HALO, :] = jnp.zeros((HALO, XBC_WIDTH), f32)
        st_ref[...] = jnp.zeros_like(st_ref)

    ext_ref[HALO:HALO + L, :] = xbc_ref[...]
    conv = cb_ref[...]
    for k in range(CONV_WIDTH):
        off = HALO - (CONV_WIDTH - 1) + k
        conv = conv + cw_ref[k:k + 1, :] * ext_ref[off:off + L, :]
    ext_ref[0:HALO, :] = ext_ref[L:L + HALO, :]
    act = conv * jax.nn.sigmoid(conv)
    xs = act[:, :SSM_WIDTH]
    bm = act[:, SSM_WIDTH:SSM_WIDTH + SSM_GROUPS * SSM_STATE]
    cm = act[:, SSM_WIDTH + SSM_GROUPS * SSM_STATE:]

    dt_c = jax.nn.softplus(small_ref[:, SMALL_DT_OFF:SMALL_DT_OFF + SSM_HEADS] + dtb_r_ref[...])
    dt_r = jax.nn.softplus(dtt_ref[0] + dtb_c_ref[...])
    a_r = -jnp.exp(alog_r_ref[...])
    a_c = -jnp.exp(alog_c_ref[...])
    row = lax.broadcasted_iota(i32, (L, L), 0)
    col = lax.broadcasted_iota(i32, (L, L), 1)
    causal = col <= row
    tri = jnp.where(causal, 1.0, 0.0)
    acs_c = _dot_hi(tri, dt_c * a_r)
    acs_r = _dot_hi(dt_r * a_c, jnp.where(row <= col, 1.0, 0.0))

    z = z_ref[...]
    ys = []
    for g in range(SSM_GROUPS):
        cg = cm[:, g * SSM_STATE:(g + 1) * SSM_STATE].astype(bf16)
        bg = bm[:, g * SSM_STATE:(g + 1) * SSM_STATE]
        cb = _dot_t(cg, bg.astype(bf16))
        bg_t = bg.T
        ssq = jnp.zeros((L, 1), f32)
        yg = []
        for r in range(SSM_REP):
            h = g * SSM_REP + r
            hs = slice(h * SSM_HEADDIM, (h + 1) * SSM_HEADDIM)
            a_col = acs_c[:, h:h + 1]
            a_row = acs_r[h:h + 1, :]
            dt_row = dt_r[h:h + 1, :]
            a_last = acs_r[h:h + 1, L - 1:L]
            seg = a_col - a_row
            decay = jnp.where(causal, jnp.exp(jnp.where(causal, seg, 0.0)), 0.0)
            w = cb * decay * dt_row
            x_h = xs[:, hs]
            xb = x_h.astype(bf16)
            st = st_ref[h]
            y = _dot(w.astype(bf16), xb)
            y = y + _dot(cg, st.astype(bf16)) * jnp.exp(a_col)
            y = y + dskip_ref[:, hs] * x_h
            bscaled = bg_t * (jnp.exp(a_last - a_row) * dt_row)
            st_ref[h] = jnp.exp(a_last) * st + _dot(bscaled.astype(bf16), xb)
            zh = z[:, hs]
            y = y * (zh * jax.nn.sigmoid(zh))
            ssq = ssq + jnp.sum(y * y, axis=-1, keepdims=True)
            yg.append(y)
        rs = lax.rsqrt(ssq / (SSM_REP * SSM_HEADDIM) + NORM_EPS)
        for r in range(SSM_REP):
            h = g * SSM_REP + r
            hs = slice(h * SSM_HEADDIM, (h + 1) * SSM_HEADDIM)
            o_ref[:, hs] = yg[r] * rs * nw_ref[:, hs]


def _ssd(proj, dtt, cw, cb, dtb_r, dtb_c, alog_r, alog_c, dskip, nw, b, t):
    nch = t // CHUNK
    row = lambda bi, ci: bi * nch + ci
    const2 = lambda bi, ci: (0, 0)
    return pl.pallas_call(
        _ssd_kernel,
        grid=(b, nch),
        in_specs=[pl.BlockSpec((CHUNK, XBC_WIDTH), lambda bi, ci: (row(bi, ci), COL_XBC // XBC_WIDTH)),
                  pl.BlockSpec((CHUNK, SSM_WIDTH), lambda bi, ci: (row(bi, ci), COL_Z // SSM_WIDTH)),
                  pl.BlockSpec((CHUNK, LANES), lambda bi, ci: (row(bi, ci), COL_SMALL // LANES)),
                  pl.BlockSpec((1, SSM_HEADS, CHUNK), lambda bi, ci: (bi, 0, ci)),
                  pl.BlockSpec((CONV_WIDTH, XBC_WIDTH), const2),
                  pl.BlockSpec((1, XBC_WIDTH), const2),
                  pl.BlockSpec((1, SSM_HEADS), const2),
                  pl.BlockSpec((SSM_HEADS, 1), const2),
                  pl.BlockSpec((1, SSM_HEADS), const2),
                  pl.BlockSpec((SSM_HEADS, 1), const2),
                  pl.BlockSpec((1, SSM_WIDTH), const2),
                  pl.BlockSpec((1, SSM_WIDTH), const2)],
        out_specs=pl.BlockSpec((CHUNK, SSM_WIDTH), lambda bi, ci: (row(bi, ci), 0)),
        out_shape=jax.ShapeDtypeStruct((b * t, SSM_WIDTH), f32),
        scratch_shapes=[pltpu.VMEM((HALO + CHUNK, XBC_WIDTH), f32),
                        pltpu.VMEM((SSM_HEADS, SSM_STATE, SSM_HEADDIM), f32)],
        compiler_params=_cparams(("parallel", "arbitrary")),
        name="ssd",
    )(proj, proj, proj, dtt, cw, cb, dtb_r, dtb_c, alog_r, alog_c, dskip, nw)


def _layer_norm(v, g, b):
    mu = jnp.mean(v, axis=-1, keepdims=True)
    d = v - mu
    var = jnp.mean(d * d, axis=-1, keepdims=True)
    return d * lax.rsqrt(var + NORM_EPS) * g + b


def _outproj_kernel(alpha, ya_ref, yb_ref, x_ref, wa_ref, wb_ref, g_ref, b_ref, wr_ref, br_ref,
                    h_ref, route_ref, cnt_ref):
    i = pl.program_id(0)
    tm = x_ref.shape[0]
    mix = _dot(ya_ref[...].astype(bf16), wa_ref[...]) + _dot(yb_ref[...].astype(bf16), wb_ref[...])
    h = _layer_norm(alpha * x_ref[...] + mix, g_ref[...], b_ref[...])
    h_ref[...] = h

    h_hi = h.astype(bf16)
    h_lo = (h - h_hi.astype(f32)).astype(bf16)
    t = _dot(h_hi, wr_ref[...])
    logits = t[:, :LANES] + t[:, LANES:] + _dot(h_lo, wr_ref[:, :LANES]) + br_ref[...]
    lane = lax.broadcasted_iota(i32, (tm, LANES), 1)
    ninf = -jnp.inf
    gmask = lane < N_EXPERT_GROUPS
    gl = jnp.where(gmask, logits, ninf)
    ge = jnp.where(gmask, jnp.exp(gl - jnp.max(gl, axis=-1, keepdims=True)), 0.0)
    pg = ge / jnp.sum(ge, axis=-1, keepdims=True)
    g_gate = jnp.max(pg, axis=-1, keepdims=True)
    g_sel = jnp.min(jnp.where(gmask & (pg == g_gate), lane, LANES), axis=-1, keepdims=True)
    lo = N_EXPERT_GROUPS + g_sel * EXPERTS_PER_GROUP
    emask = (lane >= lo) & (lane < lo + EXPERTS_PER_GROUP)
    el = jnp.where(emask, logits, ninf)
    ee = jnp.where(emask, jnp.exp(el - jnp.max(el, axis=-1, keepdims=True)), 0.0)
    pe = ee / jnp.sum(ee, axis=-1, keepdims=True)
    p0 = jnp.max(pe, axis=-1, keepdims=True)
    l0 = jnp.min(jnp.where(emask & (pe == p0), lane, LANES), axis=-1, keepdims=True)
    rest = jnp.where(emask & (lane != l0), pe, ninf)
    p1 = jnp.max(rest, axis=-1, keepdims=True)
    l1 = jnp.min(jnp.where(rest == p1, lane, LANES), axis=-1, keepdims=True)
    psum = p0 + p1
    w0 = g_gate * p0 / psum
    w1 = g_gate * p1 / psum
    e0 = l0 - N_EXPERT_GROUPS
    e1 = l1 - N_EXPERT_GROUPS

    @pl.when(i == 0)
    def _():
        cnt_ref[...] = jnp.zeros_like(cnt_ref)

    oh0 = lane == e0
    oh1 = lane == e1
    both = jnp.where(oh0, 1.0, 0.0) + jnp.where(oh1, 1.0, 0.0)
    r_i = lax.broadcasted_iota(i32, (tm, tm), 0)
    c_i = lax.broadcasted_iota(i32, (tm, tm), 1)
    strict = jnp.where(c_i < r_i, 1.0, 0.0).astype(bf16)
    before = _dot(strict, both.astype(bf16)) + cnt_ref[...]
    rank0 = jnp.sum(jnp.where(oh0, before, 0.0), axis=-1, keepdims=True)
    rank1 = jnp.sum(jnp.where(oh1, before, 0.0), axis=-1, keepdims=True)
    cnt_ref[...] = cnt_ref[...] + jnp.sum(both, axis=0, keepdims=True)

    out = jnp.where(lane == 0, e0.astype(f32), 0.0)
    out = jnp.where(lane == 1, e1.astype(f32), out)
    out = jnp.where(lane == 2, w0, out)
    out = jnp.where(lane == 3, w1, out)
    out = jnp.where(lane == 4, rank0, out)
    out = jnp.where(lane == 5, rank1, out)
    route_ref[...] = out


def _outproj(y_nsa, y_ssm, x2, wa, wb, g, bta, wr, br, alpha):
    n, d = x2.shape
    tm = OUT_TM
    const = lambda i: (0, 0)
    rowb = lambda i: (i, 0)
    return pl.pallas_call(
        functools.partial(_outproj_kernel, alpha),
        grid=(n // tm,),
        in_specs=[pl.BlockSpec((tm, NSA_WIDTH), rowb), pl.BlockSpec((tm, SSM_WIDTH), rowb),
                  pl.BlockSpec((tm, d), rowb),
                  pl.BlockSpec((NSA_WIDTH, d), const), pl.BlockSpec((SSM_WIDTH, d), const),
                  pl.BlockSpec((1, d), const), pl.BlockSpec((1, d), const),
                  pl.BlockSpec((d, 2 * LANES), const), pl.BlockSpec((1, LANES), const)],
        out_specs=[pl.BlockSpec((tm, d), rowb), pl.BlockSpec((tm, LANES), rowb),
                   pl.BlockSpec((1, LANES), const)],
        out_shape=[jax.ShapeDtypeStruct((n, d), f32), jax.ShapeDtypeStruct((n, LANES), f32),
                   jax.ShapeDtypeStruct((1, LANES), f32)],
        compiler_params=_cparams(("arbitrary",)),
        name="outproj",
    )(y_nsa, y_ssm, x2, wa, wb, g, bta, wr, br)


def _row_copy(src_ref, src_row, dst_ref, dst_row, sem):
    return pltpu.make_async_copy(src_ref.at[pl.ds(src_row, 1), :], dst_ref.at[pl.ds(dst_row, 1), :], sem)


def _dest_row(idx_ref, ps_ref, tm, k, r):
    return ps_ref[idx_ref[0, 0, k * tm + r]] + idx_ref[0, 0, (2 + k) * tm + r]


def _dispatch_kernel(idx_ref, ps_ref, zflag_ref, h_ref, xs_ref, zero_ref, sem, zsem):
    tm = DISP_TM
    nb = zflag_ref.shape[0]
    i = pl.program_id(0)

    @pl.when(i == 0)
    def _():
        zero_ref[...] = jnp.zeros_like(zero_ref)

        def zblock(i):
            rows = pl.ds(pl.multiple_of(i * MOE_TM, MOE_TM), MOE_TM)
            return pltpu.make_async_copy(zero_ref, xs_ref.at[rows, :], zsem)

        def zstart(i, _):
            @pl.when(zflag_ref[i] != 0)
            def _():
                zblock(i).start()
            return 0

        def zwait(i, _):
            @pl.when(zflag_ref[i] != 0)
            def _():
                zblock(i).wait()
            return 0

        lax.fori_loop(0, nb, zstart, 0)
        lax.fori_loop(0, nb, zwait, 0)

    base = i * tm

    def issue(r, _):
        for k in range(2):
            _row_copy(h_ref, base + r, xs_ref, _dest_row(idx_ref, ps_ref, tm, k, r), sem).start()
        return 0

    lax.fori_loop(0, tm, issue, 0, unroll=4)

    def wait_step_rows():
        for k in range(2):
            pltpu.make_async_copy(h_ref.at[pl.ds(0, tm), :], xs_ref.at[pl.ds(0, tm), :], sem).wait()

    @pl.when(i > 0)
    def _():
        wait_step_rows()

    @pl.when(i == pl.num_programs(0) - 1)
    def _():
        wait_step_rows()


def _dispatch(idx_t, pad_starts, zflag, h, p_rows):
    n, d = h.shape
    tm = DISP_TM
    return pl.pallas_call(
        _dispatch_kernel,
        grid=(n // tm,),
        in_specs=[pl.BlockSpec((1, 1, 4 * tm), lambda i: (i, 0, 0), memory_space=pltpu.SMEM),
                  pl.BlockSpec(memory_space=pltpu.SMEM),
                  pl.BlockSpec(memory_space=pltpu.SMEM),
                  pl.BlockSpec(memory_space=pl.ANY)],
        out_specs=pl.BlockSpec(memory_space=pl.ANY),
        out_shape=jax.ShapeDtypeStruct((p_rows, d), f32),
        scratch_shapes=[pltpu.VMEM((MOE_TM, d), f32), pltpu.SemaphoreType.DMA(()), pltpu.SemaphoreType.DMA(())],
        compiler_params=_cparams(("arbitrary",)),
        name="dispatch",
    )(idx_t, pad_starts, zflag, h)


def _experts_kernel(be_ref, nu_ref, xs_ref, wg_ref, wu_ref, wd_ref, y_ref, wgb_ref, wub_ref, wdb_ref):
    i = pl.program_id(0)
    e = be_ref[i]
    prev = be_ref[jnp.maximum(i - 1, 0)]

    @pl.when((i == 0) | (e != prev))
    def _():
        wgb_ref[...] = wg_ref[0].astype(bf16)
        wub_ref[...] = wu_ref[0].astype(bf16)
        wdb_ref[...] = wd_ref[0].astype(bf16)

    @pl.when(i < nu_ref[0])
    def _():
        xb = xs_ref[...].astype(bf16)
        gte = _dot(xb, wgb_ref[...])
        up = _dot(xb, wub_ref[...])
        act = gte * jax.nn.sigmoid(gte) * up
        y_ref[...] = _dot(act.astype(bf16), wdb_ref[...])

    @pl.when(i >= nu_ref[0])
    def _():
        y_ref[...] = jnp.zeros_like(y_ref)


def _experts(block_e, n_used, xs, w_gate, w_up, w_down):
    p_rows, d = xs.shape
    tm = MOE_TM
    nb = p_rows // tm
    de = w_gate.shape[-1]
    xmap = lambda i, be, nu: (jnp.maximum(jnp.minimum(i, nu[0] - 1), 0), 0)
    return pl.pallas_call(
        _experts_kernel,
        grid_spec=pltpu.PrefetchScalarGridSpec(
            num_scalar_prefetch=2,
            grid=(nb,),
            in_specs=[pl.BlockSpec((tm, d), xmap),
                      pl.BlockSpec((1, d, de), lambda i, be, nu: (be[i], 0, 0)),
                      pl.BlockSpec((1, d, de), lambda i, be, nu: (be[i], 0, 0)),
                      pl.BlockSpec((1, de, d), lambda i, be, nu: (be[i], 0, 0))],
            out_specs=pl.BlockSpec((tm, d), lambda i, be, nu: (i, 0)),
            scratch_shapes=[pltpu.VMEM((d, de), bf16), pltpu.VMEM((d, de), bf16), pltpu.VMEM((de, d), bf16)],
        ),
        out_shape=jax.ShapeDtypeStruct((p_rows, d), f32),
        compiler_params=_cparams(("arbitrary",)),
        name="experts",
    )(block_e, n_used, xs, w_gate, w_up, w_down)


def _combine_kernel(alpha, idx_ref, nidx_ref, ps_ref, route_ref, h_ref, g_ref, b_ref, y_ref, o_ref, buf_ref, sems):
    tm = h_ref.shape[0]
    i = pl.program_id(0)
    slot = i % 2

    def gather(ids_ref, s):
        def issue(r, _):
            for k in range(2):
                _row_copy(y_ref, _dest_row(ids_ref, ps_ref, tm, k, r), buf_ref.at[s, k], r, sems.at[s]).start()
            return 0
        lax.fori_loop(0, tm, issue, 0, unroll=4)

    @pl.when(i == 0)
    def _():
        gather(idx_ref, 0)

    @pl.when(i + 1 < pl.num_programs(0))
    def _():
        gather(nidx_ref, 1 - slot)

    for k in range(2):
        pltpu.make_async_copy(y_ref.at[pl.ds(0, tm), :], buf_ref.at[slot, k], sems.at[slot]).wait()

    route = route_ref[...]
    ffn = route[:, 2:3] * buf_ref[slot, 0] + route[:, 3:4] * buf_ref[slot, 1]
    o_ref[...] = _layer_norm(alpha * h_ref[...] + ffn, g_ref[...], b_ref[...])


def _combine(idx_t, pad_starts, route, h, g, bta, y, alpha):
    n, d = h.shape
    tm = COMB_TM
    const = lambda i: (0, 0)
    return pl.pallas_call(
        functools.partial(_combine_kernel, alpha),
        grid=(n // tm,),
        in_specs=[pl.BlockSpec((1, 1, 4 * tm), lambda i: (i, 0, 0), memory_space=pltpu.SMEM),
                  pl.BlockSpec((1, 1, 4 * tm), lambda i: (jnp.minimum(i + 1, n // tm - 1), 0, 0),
                               memory_space=pltpu.SMEM),
                  pl.BlockSpec(memory_space=pltpu.SMEM),
                  pl.BlockSpec((tm, LANES), lambda i: (i, 0)),
                  pl.BlockSpec((tm, d), lambda i: (i, 0)),
                  pl.BlockSpec((1, d), const), pl.BlockSpec((1, d), const),
                  pl.BlockSpec(memory_space=pl.ANY)],
        out_specs=pl.BlockSpec((tm, d), lambda i: (i, 0)),
        out_shape=jax.ShapeDtypeStruct((n, d), f32),
        scratch_shapes=[pltpu.VMEM((2, 2, tm, d), f32), pltpu.SemaphoreType.DMA((2,))],
        compiler_params=_cparams(("arbitrary",)),
        name="combine",
    )(idx_t, idx_t, pad_starts, route, h, g, bta, y)


def _tile_idx(idx, tm):
    n = idx.shape[1]
    return idx.reshape(4, n // tm, tm).transpose(1, 0, 2).reshape(n // tm, 1, 4 * tm)


def _layer(x, positions, w_in, cmp_k_pe, cmp_k_w1, cmp_k_b1, cmp_k_w2, cmp_v_pe, cmp_v_w1, cmp_v_b1, cmp_v_w2,
           conv_w, conv_b, dt_bias, a_log, d_skip, ssm_norm_w, w_out, ln1_g, ln1_b,
           w_router_group, b_router_group, w_router_expert, b_router_expert, w_gate, w_up, w_down, ln2_g, ln2_b,
           alpha):
    b, t, d = x.shape
    n = b * t
    x2 = x.reshape(n, d)

    c0 = NSA_WIDTH
    c1 = c0 + 6 * KV_WIDTH
    c2 = c1 + 3 * NSA_HEADS
    c3 = c2 + SSM_WIDTH
    c4 = c3 + XBC_WIDTH
    w_small = jnp.concatenate([w_in[:, c1:c2], w_in[:, c4:], jnp.zeros((d, LANES - 3 * NSA_HEADS - SSM_HEADS), f32)], axis=1)
    w_cat = jnp.concatenate([w_in[:, :c0], w_in[:, c2:c3], w_in[:, c3:c4], w_in[:, c0:c1], w_small,
                             jnp.zeros((d, PROJ_COLS - COL_SMALL - LANES), f32)], axis=1).astype(bf16)
    proj = _proj(x2, w_cat)

    lane = np.arange(LANES) % HEAD_DIM
    inv_freq = ROPE_THETA ** (-jnp.arange(0, ROT_DIM, 2, dtype=f32) / ROT_DIM)
    invf = jnp.where(lane < ROT_DIM, inv_freq[lane % (ROT_DIM // 2)], 0.0).astype(f32)[None, :]
    pos128 = jnp.broadcast_to(positions.reshape(n, 1), (n, LANES))
    q_r, k_cmp, v_cmp, k_sel, v_sel, k_win, v_win = _nsa_prep(proj, pos128, invf, b, t)

    nc = t // CMP_STRIDE
    half_w = CMP_STRIDE * HEAD_DIM
    a = jnp.stack([k_cmp, v_cmp]).reshape(2, b * NSA_KV_GROUPS, nc, half_w)
    pe = jnp.stack([cmp_k_pe, cmp_v_pe]).reshape(2, 2, 1, half_w)
    w1 = jnp.stack([cmp_k_w1, cmp_v_w1]).reshape(2, 2, half_w, CMP_HIDDEN).astype(bf16)
    b1 = jnp.stack([cmp_k_b1, cmp_v_b1]).reshape(2, 1, CMP_HIDDEN)
    w2 = jnp.pad(jnp.stack([cmp_k_w2, cmp_v_w2]), ((0, 0), (0, 0), (0, LANES - HEAD_DIM))).astype(bf16)
    cend = jnp.minimum(jnp.arange(nc) * CMP_STRIDE + CMP_BLOCK - 1, t - 1)
    posc = jnp.broadcast_to(positions[:, cend][:, :, None], (b, nc, LANES))
    kvc, kvc_t = _cmp_mlp(a, pe, w1, b1, w2, posc, invf, b)

    c_start = np.arange(nc)[:, None] * CMP_STRIDE
    s_start = np.arange(LANES)[None, :] * SEL_BLOCK
    cover = ((c_start < s_start + SEL_BLOCK) & (c_start + CMP_BLOCK > s_start)
             & (np.arange(nc)[:, None] < nc - 1) & (np.arange(LANES)[None, :] < t // SEL_BLOCK))
    cover = jnp.asarray(cover, bf16)
    o_cmp, selb = _cmp_attn(q_r, kvc, kvc_t, cover, b, t)

    y_nsa = _nsa_attn(q_r, k_sel, v_sel, k_win, v_win, selb, o_cmp, proj, b, t)

    dt_raw = proj[:, COL_SMALL + SMALL_DT_OFF:COL_SMALL + SMALL_DT_OFF + SSM_HEADS]
    dtt = dt_raw.reshape(b, t, SSM_HEADS).transpose(0, 2, 1)
    y_ssm = _ssd(proj, dtt, conv_w.reshape(CONV_WIDTH, XBC_WIDTH), conv_b.reshape(1, XBC_WIDTH),
                 dt_bias.reshape(1, SSM_HEADS), dt_bias.reshape(SSM_HEADS, 1),
                 a_log.reshape(1, SSM_HEADS), a_log.reshape(SSM_HEADS, 1),
                 jnp.repeat(d_skip, SSM_HEADDIM).reshape(1, SSM_WIDTH), ssm_norm_w.reshape(1, SSM_WIDTH), b, t)

    wr = jnp.concatenate([w_router_group, w_router_expert,
                          jnp.zeros((d, LANES - N_EXPERT_GROUPS - N_EXPERTS), f32)], axis=1)
    br = jnp.concatenate([b_router_group, b_router_expert,
                          jnp.zeros((LANES - N_EXPERT_GROUPS - N_EXPERTS,), f32)])[None, :]
    wr_hi = wr.astype(bf16)
    wr = jnp.concatenate([wr_hi, (wr - wr_hi.astype(f32)).astype(bf16)], axis=1)
    wo = w_out.astype(bf16)
    h, route, counts = _outproj(y_nsa, y_ssm, x2, wo[:NSA_WIDTH], wo[NSA_WIDTH:], ln1_g[None, :], ln1_b[None, :],
                                wr, br, alpha)

    cnt = counts[0, :N_EXPERTS].astype(i32)
    padded = (cnt + MOE_TM - 1) // MOE_TM * MOE_TM
    pad_ends = jnp.cumsum(padded)
    pad_starts = pad_ends - padded
    idx = jnp.concatenate([route[:, 0:2], route[:, 4:6]], axis=1).astype(i32).T
    p_rows = 2 * n + N_EXPERTS * MOE_TM
    nb = p_rows // MOE_TM
    block_e = jnp.minimum(jnp.sum(jnp.arange(nb, dtype=i32)[:, None] * MOE_TM >= pad_ends[None, :], axis=-1),
                          N_EXPERTS - 1).astype(i32)
    n_used = (pad_ends[-1] // MOE_TM).astype(i32).reshape(1)
    blk = jnp.arange(nb, dtype=i32)
    last_of_expert = jnp.any((blk[:, None] + 1) * MOE_TM == pad_ends[None, :], axis=-1)
    zflag = (last_of_expert | (blk >= n_used[0])).astype(i32)
    block_e = jnp.where(blk < n_used[0], block_e, block_e[jnp.maximum(n_used[0] - 1, 0)])

    xs = _dispatch(_tile_idx(idx, DISP_TM), pad_starts, zflag, h, p_rows)
    y = _experts(block_e, n_used, xs, w_gate, w_up, w_down)
    out = _combine(_tile_idx(idx, COMB_TM), pad_starts, route, h, ln2_g[None, :], ln2_b[None, :], y, alpha)
    return out.reshape(b, t, d)


def kernel(x, positions, w_in, cmp_k_pe, cmp_k_w1, cmp_k_b1, cmp_k_w2, cmp_v_pe, cmp_v_w1, cmp_v_b1, cmp_v_w2, conv_w, conv_b, dt_bias, a_log, d_skip, ssm_norm_w, w_out, ln1_g, ln1_b, w_router_group, b_router_group, w_router_expert, b_router_expert, w_gate, w_up, w_down, ln2_g, ln2_b):
    depth = w_in.shape[0]
    alpha = (2 * depth) ** 0.25
    params = (w_in, cmp_k_pe, cmp_k_w1, cmp_k_b1, cmp_k_w2, cmp_v_pe, cmp_v_w1, cmp_v_b1, cmp_v_w2, conv_w, conv_b,
              dt_bias, a_log, d_skip, ssm_norm_w, w_out, ln1_g, ln1_b, w_router_group, b_router_group,
              w_router_expert, b_router_expert, w_gate, w_up, w_down, ln2_g, ln2_b)
    for l in range(depth):
        x = _layer(x, positions, *[p[l] for p in params], alpha)
    return x
```

```python
import functools
import math

import jax
import jax.numpy as jnp
import numpy as np
from jax import lax
from jax.experimental import pallas as pl
from jax.experimental.pallas import tpu as pltpu

f32 = jnp.float32
bf16 = jnp.bfloat16
i32 = jnp.int32

HEAD_DIM = 64
NSA_HEADS = 16
NSA_KV_GROUPS = 2
NSA_REP = NSA_HEADS // NSA_KV_GROUPS
NSA_WIDTH = NSA_HEADS * HEAD_DIM
KV_WIDTH = NSA_KV_GROUPS * HEAD_DIM
CMP_BLOCK = 32
CMP_STRIDE = 16
CMP_HIDDEN = 256
SEL_BLOCK = 64
SEL_TOPK = 16
WINDOW = 512
FORCED_SCORE = 1.0e4
SSM_HEADDIM = 64
SSM_HEADS = 16
SSM_WIDTH = SSM_HEADS * SSM_HEADDIM
SSM_GROUPS = 4
SSM_REP = SSM_HEADS // SSM_GROUPS
SSM_STATE = 128
CONV_WIDTH = 4
CHUNK = 256
XBC_WIDTH = SSM_WIDTH + 2 * SSM_GROUPS * SSM_STATE
ROPE_THETA = 500000.0
ROT_DIM = HEAD_DIM // 4
N_EXPERT_GROUPS = 4
EXPERTS_PER_GROUP = 8
N_EXPERTS = N_EXPERT_GROUPS * EXPERTS_PER_GROUP
D_EXPERT = 512
NORM_EPS = 1e-5

LANES = 128
MASK_NEG = -1.0e30

PROJ_TM = 1024
PROJ_TN = 512
PREP_TM = 512
ATT_TQ = 256
ATT_TK = 512
WIN_TK = 256
KX_WIDTH = 2 * LANES
OUT_TM = 256
MOE_TM = 256
DISP_TM = 256
COMB_TM = 128
VMEM_LIMIT = 56 * 1024 * 1024

COL_Q = 0
COL_Z = NSA_WIDTH
COL_XBC = COL_Z + SSM_WIDTH
COL_KV = COL_XBC + XBC_WIDTH
COL_SMALL = COL_KV + 6 * KV_WIDTH
PROJ_COLS = 5120
SMALL_DT_OFF = 3 * NSA_HEADS


def _cparams(sem, vmem=VMEM_LIMIT):
    return pltpu.CompilerParams(dimension_semantics=sem, vmem_limit_bytes=vmem)


def _dot(a, b):
    return jnp.dot(a, b, preferred_element_type=f32)


def _dot_t(a, b):
    return lax.dot_general(a, b, (((1,), (1,)), ((), ())), preferred_element_type=f32)


def _dot_hi(a, b):
    return jnp.dot(a, b, preferred_element_type=f32, precision=lax.Precision.HIGHEST)


def _proj_kernel(x_ref, w_ref, o_ref, xb_ref):
    @pl.when(pl.program_id(1) == 0)
    def _():
        xb_ref[...] = x_ref[...].astype(bf16)

    o_ref[...] = _dot(xb_ref[...], w_ref[...])


def _proj(x2, w_cat):
    n, d = x2.shape
    cols = w_cat.shape[1]
    tm = min(PROJ_TM, n)
    return pl.pallas_call(
        _proj_kernel,
        grid=(n // tm, cols // PROJ_TN),
        in_specs=[pl.BlockSpec((tm, d), lambda i, j: (i, 0)),
                  pl.BlockSpec((d, PROJ_TN), lambda i, j: (0, j))],
        out_specs=pl.BlockSpec((tm, PROJ_TN), lambda i, j: (i, j)),
        out_shape=jax.ShapeDtypeStruct((n, cols), f32),
        scratch_shapes=[pltpu.VMEM((tm, d), bf16)],
        compiler_params=_cparams(("parallel", "arbitrary")),
        name="proj",
    )(x2, w_cat)


def _rope_tables(pos_i32, invf):
    ang = pos_i32.astype(f32) * invf
    return jnp.cos(ang), jnp.sin(ang)


def _rope128(x, cos, sin):
    half = ROT_DIM // 2
    d = lax.broadcasted_iota(i32, x.shape, 1) % HEAD_DIM
    up = pltpu.roll(x, LANES - half, 1)
    dn = pltpu.roll(x, half, 1)
    rot = jnp.where(d < half, -up, dn)
    return x * cos + rot * sin


def _nsa_prep_kernel(pos_ref, invf_ref, q_ref, kc_ref, vc_ref, ks_ref, vs_ref, kw_ref, vw_ref,
                     qo_ref, kco_ref, vco_ref, kso_ref, vso_ref, kwo_ref, vwo_ref):
    cos, sin = _rope_tables(pos_ref[...], invf_ref[...])
    scale = HEAD_DIM ** -0.5
    for c in range(NSA_WIDTH // LANES):
        t = _rope128(q_ref[:, c * LANES:(c + 1) * LANES], cos, sin) * scale
        qo_ref[0, 2 * c] = t[:, :HEAD_DIM].astype(bf16)
        qo_ref[0, 2 * c + 1] = t[:, HEAD_DIM:].astype(bf16)

    def split(src, dst, rope, dt):
        t = src[...]
        if rope:
            t = _rope128(t, cos, sin)
        for g in range(NSA_KV_GROUPS):
            dst[0, g] = t[:, g * HEAD_DIM:(g + 1) * HEAD_DIM].astype(dt)

    split(kc_ref, kco_ref, False, f32)
    split(vc_ref, vco_ref, False, f32)

    tm = pos_ref.shape[0]

    def values_with_ones(src, dst):
        t = src[...]
        lane = lax.broadcasted_iota(i32, (tm, HEAD_DIM), 1)
        tail = jnp.where(lane == 0, 1.0, 0.0).astype(bf16)
        for g in range(NSA_KV_GROUPS):
            vg = t[:, g * HEAD_DIM:(g + 1) * HEAD_DIM].astype(bf16)
            dst[0, g] = jnp.concatenate([vg, tail], axis=1)

    values_with_ones(vs_ref, vso_ref)
    values_with_ones(vw_ref, vwo_ref)

    kw_t = _rope128(kw_ref[...], cos, sin).T
    ks_t = _rope128(ks_ref[...], cos, sin).T
    for g in range(NSA_KV_GROUPS):
        for c in range(tm // WIN_TK):
            kwo_ref[0, g, c] = kw_t[g * HEAD_DIM:(g + 1) * HEAD_DIM, c * WIN_TK:(c + 1) * WIN_TK].astype(bf16)
    blk = lax.broadcasted_iota(i32, (LANES, tm), 0)
    tok = pl.program_id(1) * tm + lax.broadcasted_iota(i32, (LANES, tm), 1)
    onehot_t = jnp.where(tok // SEL_BLOCK == blk, 1.0, 0.0).astype(bf16)
    for g in range(NSA_KV_GROUPS):
        kso_ref[0, g, 0, 0:LANES, :] = onehot_t
        kg = ks_t[g * HEAD_DIM:(g + 1) * HEAD_DIM, :].astype(bf16)
        kso_ref[0, g, 0, LANES:2 * LANES, :] = jnp.concatenate([kg, jnp.zeros_like(kg)], axis=0)


def _nsa_prep(proj, pos128, invf, b, t):
    tm = PREP_TM
    assert tm == ATT_TK
    nt = t // tm
    row = lambda bi, ti: (bi * nt + ti, 0)
    kv0 = COL_KV // LANES
    in_specs = [pl.BlockSpec((tm, LANES), row),
                pl.BlockSpec((1, LANES), lambda bi, ti: (0, 0)),
                pl.BlockSpec((tm, NSA_WIDTH), lambda bi, ti: (bi * nt + ti, COL_Q // NSA_WIDTH))]
    for k in range(6):
        in_specs.append(pl.BlockSpec((tm, LANES), functools.partial(lambda bi, ti, k: (bi * nt + ti, kv0 + k), k=k)))
    head = lambda bi, ti: (bi, 0, ti, 0)
    tile5 = lambda bi, ti: (bi, 0, ti, 0, 0)
    g = NSA_KV_GROUPS
    out_specs = [pl.BlockSpec((1, NSA_HEADS, tm, HEAD_DIM), head),
                 pl.BlockSpec((1, g, tm, HEAD_DIM), head), pl.BlockSpec((1, g, tm, HEAD_DIM), head),
                 pl.BlockSpec((1, g, 1, KX_WIDTH, tm), tile5), pl.BlockSpec((1, g, tm, LANES), head),
                 pl.BlockSpec((1, g, tm // WIN_TK, HEAD_DIM, WIN_TK), tile5), pl.BlockSpec((1, g, tm, LANES), head)]
    out_shape = [jax.ShapeDtypeStruct((b, NSA_HEADS, t, HEAD_DIM), bf16),
                 jax.ShapeDtypeStruct((b, g, t, HEAD_DIM), f32), jax.ShapeDtypeStruct((b, g, t, HEAD_DIM), f32),
                 jax.ShapeDtypeStruct((b, g, nt, KX_WIDTH, tm), bf16), jax.ShapeDtypeStruct((b, g, t, LANES), bf16),
                 jax.ShapeDtypeStruct((b, g, t // WIN_TK, HEAD_DIM, WIN_TK), bf16),
                 jax.ShapeDtypeStruct((b, g, t, LANES), bf16)]
    return pl.pallas_call(
        _nsa_prep_kernel,
        grid=(b, nt),
        in_specs=in_specs,
        out_specs=out_specs,
        out_shape=out_shape,
        compiler_params=_cparams(("parallel", "parallel")),
        name="nsa_prep",
    )(pos128, invf, proj, proj, proj, proj, proj, proj, proj)


def _cmp_mlp_kernel(a_ref, pe_ref, w1_ref, b1_ref, w2_ref, pos_ref, invf_ref, o_ref, ot_ref):
    kind = pl.program_id(0)
    a = a_ref[0, 0]
    nc = a.shape[0]
    u = _dot((a + pe_ref[0, 0]).astype(bf16), w1_ref[0, 0])
    v = _dot((a + pe_ref[0, 1]).astype(bf16), w1_ref[0, 1])
    v_next = pltpu.roll(v, nc - 1, 0)
    hid = jax.nn.gelu(u + v_next + b1_ref[0])
    out = _dot(hid.astype(bf16), w2_ref[0])
    cos, sin = _rope_tables(pos_ref[0], invf_ref[...])
    roped = _rope128(out, cos, sin)
    out = jnp.where(kind == 0, roped, out)
    o_ref[0, 0] = out[:, :HEAD_DIM].astype(bf16)
    ot_ref[0, 0] = out.T[:HEAD_DIM, :].astype(bf16)


def _cmp_mlp(a, pe, w1, b1, w2, posc, invf, b):
    _, bg, nc, hw = a.shape
    g = bg // b
    return pl.pallas_call(
        _cmp_mlp_kernel,
        grid=(2, bg),
        in_specs=[pl.BlockSpec((1, 1, nc, hw), lambda k, i: (k, i, 0, 0)),
                  pl.BlockSpec((1, 2, 1, hw), lambda k, i: (k, 0, 0, 0)),
                  pl.BlockSpec((1, 2, hw, CMP_HIDDEN), lambda k, i: (k, 0, 0, 0)),
                  pl.BlockSpec((1, 1, CMP_HIDDEN), lambda k, i: (k, 0, 0)),
                  pl.BlockSpec((1, CMP_HIDDEN, LANES), lambda k, i: (k, 0, 0)),
                  pl.BlockSpec((1, nc, LANES), lambda k, i: (i // g, 0, 0)),
                  pl.BlockSpec((1, LANES), lambda k, i: (0, 0))],
        out_specs=[pl.BlockSpec((1, 1, nc, HEAD_DIM), lambda k, i: (k, i, 0, 0)),
                   pl.BlockSpec((1, 1, HEAD_DIM, nc), lambda k, i: (k, i, 0, 0))],
        out_shape=[jax.ShapeDtypeStruct((2, bg, nc, HEAD_DIM), bf16),
                   jax.ShapeDtypeStruct((2, bg, HEAD_DIM, nc), bf16)],
        compiler_params=_cparams(("parallel", "parallel")),
        name="cmp_mlp",
    )(a, pe, w1, b1, w2, posc, invf)


def _cmp_attn_kernel(q_ref, kct_ref, vc_ref, cover_ref, oc_ref, sel_ref, imp_ref):
    qi = pl.program_id(2)
    nc = vc_ref.shape[2]
    rows = NSA_REP * ATT_TQ
    tq = qi * ATT_TQ + lax.broadcasted_iota(i32, (ATT_TQ, 1), 0)
    row_live = jnp.where(tq >= CMP_BLOCK - 1, 1.0, 0.0)
    tiny = jnp.finfo(f32).tiny

    def attend(ncols):
        kct = kct_ref[0, 0, :, :ncols]
        vc = vc_ref[0, 0, :ncols, :]
        cend = lax.broadcasted_iota(i32, (1, ncols), 1) * CMP_STRIDE + (CMP_BLOCK - 1)
        bias = jnp.where(cend <= tq, 0.0, MASK_NEG)
        s = _dot(q_ref[0].reshape(rows, HEAD_DIM), kct).reshape(NSA_REP, ATT_TQ, ncols) + bias[None]
        e = jnp.exp(s - jnp.max(s, axis=-1, keepdims=True)) * row_live[None]
        p = e / jnp.maximum(jnp.sum(e, axis=-1, keepdims=True), tiny)
        oc_ref[0] = _dot(p.reshape(rows, ncols).astype(bf16), vc).reshape(NSA_REP, ATT_TQ, HEAD_DIM)
        psum = jnp.sum(p, axis=0)
        hi = psum.astype(bf16)
        lo = (psum - hi.astype(f32)).astype(bf16)
        cover = cover_ref[:ncols, :]
        imp_ref[...] = _dot(hi, cover) + _dot(lo, cover)

    need = (qi + 1) * (ATT_TQ // CMP_STRIDE)
    for ncols in range(LANES, nc + 1, LANES):
        @pl.when((need > ncols - LANES) & (need <= ncols))
        def _():
            attend(ncols)

    imp = imp_ref[...]
    j = lax.broadcasted_iota(i32, (ATT_TQ, LANES), 1)
    cur = tq // SEL_BLOCK
    forced = (j == 0) | (j == cur) | (j == cur - 1)
    valid = j * SEL_BLOCK <= tq
    imp = jnp.where(valid, jnp.where(forced, FORCED_SCORE, imp), -FORCED_SCORE)

    jt = lax.broadcasted_iota(i32, (LANES, ATT_TQ), 0)

    def pick(_, carry):
        work, sel = carry
        m = jnp.max(work, axis=0, keepdims=True)
        first = jnp.min(jnp.where(work == m, jt, LANES), axis=0, keepdims=True)
        hit = jt == first
        return jnp.where(hit, -jnp.inf, work), jnp.where(hit, 1.0, sel)

    _, sel_t = lax.fori_loop(0, SEL_TOPK, pick, (imp.T, jnp.zeros((LANES, ATT_TQ), f32)))
    sel_ref[0, 0] = jnp.where(valid, jnp.where(sel_t.T > 0.0, 0.0, MASK_NEG), MASK_NEG).astype(bf16)


def _cmp_attn(q_r, kvc, kvc_t, cover, b, t):
    g = NSA_KV_GROUPS
    nc = kvc.shape[2]
    nq = t // ATT_TQ
    return pl.pallas_call(
        _cmp_attn_kernel,
        grid=(b, g, nq),
        in_specs=[pl.BlockSpec((1, NSA_REP, ATT_TQ, HEAD_DIM), lambda bi, gi, qi: (bi, gi, qi, 0)),
                  pl.BlockSpec((1, 1, HEAD_DIM, nc), lambda bi, gi, qi: (0, bi * g + gi, 0, 0)),
                  pl.BlockSpec((1, 1, nc, HEAD_DIM), lambda bi, gi, qi: (1, bi * g + gi, 0, 0)),
                  pl.BlockSpec((nc, LANES), lambda bi, gi, qi: (0, 0))],
        out_specs=[pl.BlockSpec((1, NSA_REP, ATT_TQ, HEAD_DIM), lambda bi, gi, qi: (bi, gi, qi, 0)),
                   pl.BlockSpec((1, 1, ATT_TQ, LANES), lambda bi, gi, qi: (bi, gi, qi, 0))],
        out_shape=[jax.ShapeDtypeStruct((b, NSA_HEADS, t, HEAD_DIM), f32),
                   jax.ShapeDtypeStruct((b, g, t, LANES), bf16)],
        scratch_shapes=[pltpu.VMEM((ATT_TQ, LANES), f32)],
        compiler_params=_cparams(("parallel", "parallel", "parallel")),
        name="cmp_attn",
    )(q_r, kvc_t, kvc, cover)


def _nsa_attn_kernel(q_ref, kx_ref, vs_ref, kw_ref, vw_ref, sel_ref, oc_ref, gate_ref,
                     o_ref, qx_ref, s_ref, m_ref, acc_ref, gx_ref, yw_ref):
    gi = pl.program_id(1)
    qi = pl.program_id(2)
    start = qi * ATT_TQ
    tq = start + lax.broadcasted_iota(i32, (ATT_TQ, 1), 0)

    selb = sel_ref[0, 0]
    for r in range(NSA_REP):
        qx_ref[r * ATT_TQ:(r + 1) * ATT_TQ, 0:LANES] = selb
        qr = q_ref[0, r]
        qx_ref[r * ATT_TQ:(r + 1) * ATT_TQ, LANES:2 * LANES] = jnp.concatenate([qr, jnp.zeros_like(qr)], axis=1)
    rows = NSA_REP * ATT_TQ

    m_ref[...] = jnp.full(m_ref.shape, MASK_NEG, f32)
    acc_ref[...] = jnp.zeros(acc_ref.shape, f32)

    def consume(kt, v_ref, bias):
        k0 = pl.multiple_of(kt * ATT_TK, ATT_TK)
        v = v_ref[0, 0, pl.ds(k0, ATT_TK), :]
        s = s_ref[...]
        if bias is not None:
            s = (s.reshape(NSA_REP, ATT_TQ, ATT_TK) + bias[None]).reshape(rows, ATT_TK)
        m_prev = m_ref[...]
        m_new = jnp.maximum(m_prev, jnp.max(s, axis=-1, keepdims=True))
        alpha = jnp.exp(m_prev - m_new)
        p = jnp.exp(s - jnp.concatenate([m_new] * (ATT_TK // LANES), axis=1))
        acc_ref[...] = alpha * acc_ref[...] + _dot(p.astype(bf16), v)
        m_ref[...] = m_new

    def kpos(kt):
        return kt * ATT_TK + lax.broadcasted_iota(i32, (1, ATT_TK), 1)

    last = start // ATT_TK
    s_ref[...] = _dot(qx_ref[...], kx_ref[0, 0, 0])

    def sel_step(kt, _):
        consume(kt, vs_ref, None)
        s_ref[...] = _dot(qx_ref[...], kx_ref[0, 0, kt + 1])
        return 0

    n_win = (WINDOW + ATT_TQ) // WIN_TK
    w_first = jnp.maximum(start // WIN_TK - WINDOW // WIN_TK, 0)
    kw = jnp.concatenate([kw_ref[0, 0, w_first + i] for i in range(n_win)], axis=1)
    w0 = pl.multiple_of(w_first * WIN_TK, WIN_TK)
    vw = vw_ref[0, 0, pl.ds(w0, n_win * WIN_TK), :]
    kp = w0 + lax.broadcasted_iota(i32, (1, n_win * WIN_TK), 1)
    wbias = jnp.where(kp <= tq, jnp.where(kp > tq - WINDOW, 0.0, MASK_NEG), MASK_NEG)
    hh = NSA_REP // 2
    gates = jax.nn.sigmoid(gate_ref[...])
    per_group = 3 * NSA_REP
    shifted = gates
    for gg in range(1, NSA_KV_GROUPS):
        shifted = jnp.where(gi == gg, pltpu.roll(gates, LANES - gg * per_group, 1), shifted)

    def gate(r, br):
        c = r * 3 + br
        return jnp.broadcast_to(shifted[:, c:c + 1], (ATT_TQ, HEAD_DIM))

    for half in range(2):
        qh = q_ref[0, half * hh:(half + 1) * hh].reshape(hh * ATT_TQ, HEAD_DIM)
        sw = _dot(qh, kw).reshape(hh, ATT_TQ, n_win * WIN_TK) + wbias[None]
        sw = sw.reshape(hh * ATT_TQ, n_win * WIN_TK)
        pw = jnp.exp(sw - jnp.max(sw, axis=-1, keepdims=True))
        aw = _dot(pw.astype(bf16), vw)
        for i in range(hh):
            r = half * hh + i
            a = aw[i * ATT_TQ:(i + 1) * ATT_TQ]
            yw_ref[r] = a[:, :HEAD_DIM] * (gate(r, 2) / a[:, HEAD_DIM:HEAD_DIM + 1])
            gx_ref[0, r] = gate(r, 0)
            gx_ref[1, r] = gate(r, 1)

    def sel_pair(j, _):
        sel_step(2 * j, 0)
        sel_step(2 * j + 1, 0)
        return 0

    lax.fori_loop(0, last // 2, sel_pair, 0)

    @pl.when(last % 2 == 1)
    def _():
        sel_step(last - 1, 0)

    consume(last, vs_ref, jnp.where(kpos(last) <= tq, 0.0, MASK_NEG))

    for r in range(NSA_REP):
        asel = acc_ref[r * ATT_TQ:(r + 1) * ATT_TQ]
        osel = asel[:, :HEAD_DIM] / asel[:, HEAD_DIM:HEAD_DIM + 1]
        o_ref[:, r * HEAD_DIM:(r + 1) * HEAD_DIM] = gx_ref[0, r] * oc_ref[0, r] + gx_ref[1, r] * osel + yw_ref[r]


def _nsa_attn(q_r, kx, vs, kw, vw, selb, oc, proj, b, t):
    g = NSA_KV_GROUPS
    nq = t // ATT_TQ
    vspec = pl.BlockSpec((1, 1, t, LANES), lambda bi, gi, qi: (bi, gi, 0, 0))
    hspec = pl.BlockSpec((1, NSA_REP, ATT_TQ, HEAD_DIM), lambda bi, gi, qi: (bi, gi, qi, 0))
    return pl.pallas_call(
        _nsa_attn_kernel,
        grid=(b, g, nq),
        in_specs=[hspec,
                  pl.BlockSpec((1, 1, t // ATT_TK, KX_WIDTH, ATT_TK), lambda bi, gi, qi: (bi, gi, 0, 0, 0)),
                  vspec,
                  pl.BlockSpec((1, 1, t // WIN_TK, HEAD_DIM, WIN_TK), lambda bi, gi, qi: (bi, gi, 0, 0, 0)),
                  vspec,
                  pl.BlockSpec((1, 1, ATT_TQ, LANES), lambda bi, gi, qi: (bi, gi, qi, 0)),
                  hspec,
                  pl.BlockSpec((ATT_TQ, LANES), lambda bi, gi, qi: (bi * nq + qi, COL_SMALL // LANES))],
        out_specs=pl.BlockSpec((ATT_TQ, NSA_REP * HEAD_DIM), lambda bi, gi, qi: (bi * nq + qi, gi)),
        out_shape=jax.ShapeDtypeStruct((b * t, NSA_WIDTH), f32),
        scratch_shapes=[pltpu.VMEM((NSA_REP * ATT_TQ, KX_WIDTH), bf16),
                        pltpu.VMEM((NSA_REP * ATT_TQ, ATT_TK), f32),
                        pltpu.VMEM((NSA_REP * ATT_TQ, LANES), f32),
                        pltpu.VMEM((NSA_REP * ATT_TQ, LANES), f32),
                        pltpu.VMEM((2, NSA_REP, ATT_TQ, HEAD_DIM), f32),
                        pltpu.VMEM((NSA_REP, ATT_TQ, HEAD_DIM), f32)],
        compiler_params=_cparams(("parallel", "parallel", "arbitrary")),
        name="nsa_attn",
    )(q_r, kx, vs, kw, vw, selb, oc, proj)


HALO = 8


def _ssd_kernel(xbc_ref, z_ref, small_ref, dtt_ref, cw_ref, cb_ref, dtb_r_ref, dtb_c_ref,
                alog_r_ref, alog_c_ref, dskip_ref, nw_ref, o_ref, ext_ref, st_ref):
    c = pl.program_id(1)
    L = CHUNK

    @pl.when(c == 0)
    def _():
        ext_ref[0:HALO, :] = jnp.zeros((HALO, XBC_WIDTH), f32)
        st_ref[...] = jnp.zeros_like(st_ref)

    ext_ref[HALO:HALO + L, :] = xbc_ref[...]
    conv = cb_ref[...]
    for k in range(CONV_WIDTH):
        off = HALO - (CONV_WIDTH - 1) + k
        conv = conv + cw_ref[k:k + 1, :] * ext_ref[off:off + L, :]
    ext_ref[0:HALO, :] = ext_ref[L:L + HALO, :]
    act = conv * jax.nn.sigmoid(conv)
    xs = act[:, :SSM_WIDTH]
    bm = act[:, SSM_WIDTH:SSM_WIDTH + SSM_GROUPS * SSM_STATE]
    cm = act[:, SSM_WIDTH + SSM_GROUPS * SSM_STATE:]

    dt_c = jax.nn.softplus(small_ref[:, SMALL_DT_OFF:SMALL_DT_OFF + SSM_HEADS] + dtb_r_ref[...])
    dt_r = jax.nn.softplus(dtt_ref[0] + dtb_c_ref[...])
    a_r = -jnp.exp(alog_r_ref[...])
    a_c = -jnp.exp(alog_c_ref[...])
    row = lax.broadcasted_iota(i32, (L, L), 0)
    col = lax.broadcasted_iota(i32, (L, L), 1)
    causal = col <= row
    tri = jnp.where(causal, 1.0, 0.0)
    acs_c = _dot_hi(tri, dt_c * a_r)
    acs_r = _dot_hi(dt_r * a_c, jnp.where(row <= col, 1.0, 0.0))

    z = z_ref[...]
    ys = []
    for g in range(SSM_GROUPS):
        cg = cm[:, g * SSM_STATE:(g + 1) * SSM_STATE].astype(bf16)
        bg = bm[:, g * SSM_STATE:(g + 1) * SSM_STATE]
        cb = _dot_t(cg, bg.astype(bf16))
        bg_t = bg.T
        ssq = jnp.zeros((L, 1), f32)
        yg = []
        for r in range(SSM_REP):
            h = g * SSM_REP + r
            hs = slice(h * SSM_HEADDIM, (h + 1) * SSM_HEADDIM)
            a_col = acs_c[:, h:h + 1]
            a_row = acs_r[h:h + 1, :]
            dt_row = dt_r[h:h + 1, :]
            a_last = acs_r[h:h + 1, L - 1:L]
            seg = a_col - a_row
            decay = jnp.where(causal, jnp.exp(jnp.where(causal, seg, 0.0)), 0.0)
            w = cb * decay * dt_row
            x_h = xs[:, hs]
            xb = x_h.astype(bf16)
            st = st_ref[h]
            y = _dot(w.astype(bf16), xb)
            y = y + _dot(cg, st.astype(bf16)) * jnp.exp(a_col)
            y = y + dskip_ref[:, hs] * x_h
            bscaled = bg_t * (jnp.exp(a_last - a_row) * dt_row)
            st_ref[h] = jnp.exp(a_last) * st + _dot(bscaled.astype(bf16), xb)
            zh = z[:, hs]
            y = y * (zh * jax.nn.sigmoid(zh))
            ssq = ssq + jnp.sum(y * y, axis=-1, keepdims=True)
            yg.append(y)
        rs = lax.rsqrt(ssq / (SSM_REP * SSM_HEADDIM) + NORM_EPS)
        for r in range(SSM_REP):
            h = g * SSM_REP + r
            hs = slice(h * SSM_HEADDIM, (h + 1) * SSM_HEADDIM)
            o_ref[:, hs] = yg[r] * rs * nw_ref[:, hs]


def _ssd(proj, dtt, cw, cb, dtb_r, dtb_c, alog_r, alog_c, dskip, nw, b, t):
    nch = t // CHUNK
    row = lambda bi, ci: bi * nch + ci
    const2 = lambda bi, ci: (0, 0)
    return pl.pallas_call(
        _ssd_kernel,
        grid=(b, nch),
        in_specs=[pl.BlockSpec((CHUNK, XBC_WIDTH), lambda bi, ci: (row(bi, ci), COL_XBC // XBC_WIDTH)),
                  pl.BlockSpec((CHUNK, SSM_WIDTH), lambda bi, ci: (row(bi, ci), COL_Z // SSM_WIDTH)),
                  pl.BlockSpec((CHUNK, LANES), lambda bi, ci: (row(bi, ci), COL_SMALL // LANES)),
                  pl.BlockSpec((1, SSM_HEADS, CHUNK), lambda bi, ci: (bi, 0, ci)),
                  pl.BlockSpec((CONV_WIDTH, XBC_WIDTH), const2),
                  pl.BlockSpec((1, XBC_WIDTH), const2),
                  pl.BlockSpec((1, SSM_HEADS), const2),
                  pl.BlockSpec((SSM_HEADS, 1), const2),
                  pl.BlockSpec((1, SSM_HEADS), const2),
                  pl.BlockSpec((SSM_HEADS, 1), const2),
                  pl.BlockSpec((1, SSM_WIDTH), const2),
                  pl.BlockSpec((1, SSM_WIDTH), const2)],
        out_specs=pl.BlockSpec((CHUNK, SSM_WIDTH), lambda bi, ci: (row(bi, ci), 0)),
        out_shape=jax.ShapeDtypeStruct((b * t, SSM_WIDTH), f32),
        scratch_shapes=[pltpu.VMEM((HALO + CHUNK, XBC_WIDTH), f32),
                        pltpu.VMEM((SSM_HEADS, SSM_STATE, SSM_HEADDIM), f32)],
        compiler_params=_cparams(("parallel", "arbitrary")),
        name="ssd",
    )(proj, proj, proj, dtt, cw, cb, dtb_r, dtb_c, alog_r, alog_c, dskip, nw)


def _layer_norm(v, g, b):
    mu = jnp.mean(v, axis=-1, keepdims=True)
    d = v - mu
    var = jnp.mean(d * d, axis=-1, keepdims=True)
    return d * lax.rsqrt(var + NORM_EPS) * g + b


def _outproj_kernel(alpha, ya_ref, yb_ref, x_ref, wa_ref, wb_ref, g_ref, b_ref, wr_ref, br_ref,
                    h_ref, route_ref, cnt_ref):
    i = pl.program_id(0)
    tm = x_ref.shape[0]
    mix = _dot(ya_ref[...].astype(bf16), wa_ref[...]) + _dot(yb_ref[...].astype(bf16), wb_ref[...])
    h = _layer_norm(alpha * x_ref[...] + mix, g_ref[...], b_ref[...])
    h_ref[...] = h

    h_hi = h.astype(bf16)
    h_lo = (h - h_hi.astype(f32)).astype(bf16)
    t = _dot(h_hi, wr_ref[...])
    logits = t[:, :LANES] + t[:, LANES:] + _dot(h_lo, wr_ref[:, :LANES]) + br_ref[...]
    lane = lax.broadcasted_iota(i32, (tm, LANES), 1)
    ninf = -jnp.inf
    gmask = lane < N_EXPERT_GROUPS
    gl = jnp.where(gmask, logits, ninf)
    ge = jnp.where(gmask, jnp.exp(gl - jnp.max(gl, axis=-1, keepdims=True)), 0.0)
    pg = ge / jnp.sum(ge, axis=-1, keepdims=True)
    g_gate = jnp.max(pg, axis=-1, keepdims=True)
    g_sel = jnp.min(jnp.where(gmask & (pg == g_gate), lane, LANES), axis=-1, keepdims=True)
    lo = N_EXPERT_GROUPS + g_sel * EXPERTS_PER_GROUP
    emask = (lane >= lo) & (lane < lo + EXPERTS_PER_GROUP)
    el = jnp.where(emask, logits, ninf)
    ee = jnp.where(emask, jnp.exp(el - jnp.max(el, axis=-1, keepdims=True)), 0.0)
    pe = ee / jnp.sum(ee, axis=-1, keepdims=True)
    p0 = jnp.max(pe, axis=-1, keepdims=True)
    l0 = jnp.min(jnp.where(emask & (pe == p0), lane, LANES), axis=-1, keepdims=True)
    rest = jnp.where(emask & (lane != l0), pe, ninf)
    p1 = jnp.max(rest, axis=-1, keepdims=True)
    l1 = jnp.min(jnp.where(rest == p1, lane, LANES), axis=-1, keepdims=True)
    psum = p0 + p1
    w0 = g_gate * p0 / psum
    w1 = g_gate * p1 / psum
    e0 = l0 - N_EXPERT_GROUPS
    e1 = l1 - N_EXPERT_GROUPS

    @pl.when(i == 0)
    def _():
        cnt_ref[...] = jnp.zeros_like(cnt_ref)

    oh0 = lane == e0
    oh1 = lane == e1
    both = jnp.where(oh0, 1.0, 0.0) + jnp.where(oh1, 1.0, 0.0)
    r_i = lax.broadcasted_iota(i32, (tm, tm), 0)
    c_i = lax.broadcasted_iota(i32, (tm, tm), 1)
    strict = jnp.where(c_i < r_i, 1.0, 0.0).astype(bf16)
    before = _dot(strict, both.astype(bf16)) + cnt_ref[...]
    rank0 = jnp.sum(jnp.where(oh0, before, 0.0), axis=-1, keepdims=True)
    rank1 = jnp.sum(jnp.where(oh1, before, 0.0), axis=-1, keepdims=True)
    cnt_ref[...] = cnt_ref[...] + jnp.sum(both, axis=0, keepdims=True)

    out = jnp.where(lane == 0, e0.astype(f32), 0.0)
    out = jnp.where(lane == 1, e1.astype(f32), out)
    out = jnp.where(lane == 2, w0, out)
    out = jnp.where(lane == 3, w1, out)
    out = jnp.where(lane == 4, rank0, out)
    out = jnp.where(lane == 5, rank1, out)
    route_ref[...] = out


def _outproj(y_nsa, y_ssm, x2, wa, wb, g, bta, wr, br, alpha):
    n, d = x2.shape
    tm = OUT_TM
    const = lambda i: (0, 0)
    rowb = lambda i: (i, 0)
    return pl.pallas_call(
        functools.partial(_outproj_kernel, alpha),
        grid=(n // tm,),
        in_specs=[pl.BlockSpec((tm, NSA_WIDTH), rowb), pl.BlockSpec((tm, SSM_WIDTH), rowb),
                  pl.BlockSpec((tm, d), rowb),
                  pl.BlockSpec((NSA_WIDTH, d), const), pl.BlockSpec((SSM_WIDTH, d), const),
                  pl.BlockSpec((1, d), const), pl.BlockSpec((1, d), const),
                  pl.BlockSpec((d, 2 * LANES), const), pl.BlockSpec((1, LANES), const)],
        out_specs=[pl.BlockSpec((tm, d), rowb), pl.BlockSpec((tm, LANES), rowb),
                   pl.BlockSpec((1, LANES), const)],
        out_shape=[jax.ShapeDtypeStruct((n, d), f32), jax.ShapeDtypeStruct((n, LANES), f32),
                   jax.ShapeDtypeStruct((1, LANES), f32)],
        compiler_params=_cparams(("arbitrary",)),
        name="outproj",
    )(y_nsa, y_ssm, x2, wa, wb, g, bta, wr, br)


def _row_copy(src_ref, src_row, dst_ref, dst_row, sem):
    return pltpu.make_async_copy(src_ref.at[pl.ds(src_row, 1), :], dst_ref.at[pl.ds(dst_row, 1), :], sem)


def _dest_row(idx_ref, ps_ref, tm, k, r):
    return ps_ref[idx_ref[0, 0, k * tm + r]] + idx_ref[0, 0, (2 + k) * tm + r]


def _dispatch_kernel(idx_ref, ps_ref, zflag_ref, h_ref, xs_ref, zero_ref, tile_ref, sem, lsem, zsem):
    tm = DISP_TM
    nb = zflag_ref.shape[0]
    i = pl.program_id(0)

    @pl.when(i == 0)
    def _():
        zero_ref[...] = jnp.zeros_like(zero_ref)

        def zblock(i):
            rows = pl.ds(pl.multiple_of(i * MOE_TM, MOE_TM), MOE_TM)
            return pltpu.make_async_copy(zero_ref, xs_ref.at[rows, :], zsem)

        def zstart(i, _):
            @pl.when(zflag_ref[i] != 0)
            def _():
                zblock(i).start()
            return 0

        def zwait(i, _):
            @pl.when(zflag_ref[i] != 0)
            def _():
                zblock(i).wait()
            return 0

        lax.fori_loop(0, nb, zstart, 0)
        lax.fori_loop(0, nb, zwait, 0)

    nslot = tile_ref.shape[0]
    nsteps = pl.num_programs(0)
    slot = i % nslot

    def tile_load(step, s):
        rows = pl.ds(pl.multiple_of(step * tm, tm), tm)
        return pltpu.make_async_copy(h_ref.at[rows, :], tile_ref.at[s], lsem.at[s])

    @pl.when(i == 0)
    def _():
        tile_load(0, 0).start()

    tile_load(i, slot).wait()

    @pl.when(i + 1 < nsteps)
    def _():
        tile_load(i + 1, (i + 1) % nslot).start()

    def issue(r, _):
        for k in range(2):
            _row_copy(tile_ref.at[slot], r, xs_ref, _dest_row(idx_ref, ps_ref, tm, k, r), sem.at[slot]).start()
        return 0

    lax.fori_loop(0, tm, issue, 0, unroll=4)

    def wait_rows(s):
        for k in range(2):
            pltpu.make_async_copy(tile_ref.at[s], xs_ref.at[pl.ds(0, tm), :], sem.at[s]).wait()

    @pl.when(i > 0)
    def _():
        wait_rows((i - 1) % nslot)

    @pl.when(i == nsteps - 1)
    def _():
        wait_rows(slot)


def _dispatch(idx_t, pad_starts, zflag, h, p_rows):
    n, d = h.shape
    tm = DISP_TM
    return pl.pallas_call(
        _dispatch_kernel,
        grid=(n // tm,),
        in_specs=[pl.BlockSpec((1, 1, 4 * tm), lambda i: (i, 0, 0), memory_space=pltpu.SMEM),
                  pl.BlockSpec(memory_space=pltpu.SMEM),
                  pl.BlockSpec(memory_space=pltpu.SMEM),
                  pl.BlockSpec(memory_space=pl.ANY)],
        out_specs=pl.BlockSpec(memory_space=pl.ANY),
        out_shape=jax.ShapeDtypeStruct((p_rows, d), f32),
        scratch_shapes=[pltpu.VMEM((MOE_TM, d), f32), pltpu.VMEM((3, tm, d), f32),
                        pltpu.SemaphoreType.DMA((3,)), pltpu.SemaphoreType.DMA((3,)), pltpu.SemaphoreType.DMA(())],
        compiler_params=_cparams(("arbitrary",)),
        name="dispatch",
    )(idx_t, pad_starts, zflag, h)


def _experts_kernel(be_ref, nu_ref, xs_ref, wg_ref, wu_ref, wd_ref, y_ref, wgb_ref, wub_ref, wdb_ref):
    i = pl.program_id(0)
    e = be_ref[i]
    prev = be_ref[jnp.maximum(i - 1, 0)]

    @pl.when((i == 0) | (e != prev))
    def _():
        wgb_ref[...] = wg_ref[0].astype(bf16)
        wub_ref[...] = wu_ref[0].astype(bf16)
        wdb_ref[...] = wd_ref[0].astype(bf16)

    @pl.when(i < nu_ref[0])
    def _():
        xb = xs_ref[...].astype(bf16)
        gte = _dot(xb, wgb_ref[...])
        up = _dot(xb, wub_ref[...])
        act = gte * jax.nn.sigmoid(gte) * up
        y_ref[...] = _dot(act.astype(bf16), wdb_ref[...])

    @pl.when(i >= nu_ref[0])
    def _():
        y_ref[...] = jnp.zeros_like(y_ref)


def _experts(block_e, n_used, xs, w_gate, w_up, w_down):
    p_rows, d = xs.shape
    tm = MOE_TM
    nb = p_rows // tm
    de = w_gate.shape[-1]
    xmap = lambda i, be, nu: (jnp.maximum(jnp.minimum(i, nu[0] - 1), 0), 0)
    return pl.pallas_call(
        _experts_kernel,
        grid_spec=pltpu.PrefetchScalarGridSpec(
            num_scalar_prefetch=2,
            grid=(nb,),
            in_specs=[pl.BlockSpec((tm, d), xmap),
                      pl.BlockSpec((1, d, de), lambda i, be, nu: (be[i], 0, 0)),
                      pl.BlockSpec((1, d, de), lambda i, be, nu: (be[i], 0, 0)),
                      pl.BlockSpec((1, de, d), lambda i, be, nu: (be[i], 0, 0))],
            out_specs=pl.BlockSpec((tm, d), lambda i, be, nu: (i, 0)),
            scratch_shapes=[pltpu.VMEM((d, de), bf16), pltpu.VMEM((d, de), bf16), pltpu.VMEM((de, d), bf16)],
        ),
        out_shape=jax.ShapeDtypeStruct((p_rows, d), f32),
        compiler_params=_cparams(("arbitrary",)),
        name="experts",
    )(block_e, n_used, xs, w_gate, w_up, w_down)


def _combine_kernel(alpha, idx_ref, nidx_ref, ps_ref, route_ref, h_ref, g_ref, b_ref, y_ref, o_ref, buf_ref, sems):
    tm = h_ref.shape[0]
    i = pl.program_id(0)
    slot = i % 2

    def gather(ids_ref, s):
        def issue(r, _):
            for k in range(2):
                _row_copy(y_ref, _dest_row(ids_ref, ps_ref, tm, k, r), buf_ref.at[s, k], r, sems.at[s]).start()
            return 0
        lax.fori_loop(0, tm, issue, 0, unroll=4)

    @pl.when(i == 0)
    def _():
        gather(idx_ref, 0)

    @pl.when(i + 1 < pl.num_programs(0))
    def _():
        gather(nidx_ref, 1 - slot)

    for k in range(2):
        pltpu.make_async_copy(y_ref.at[pl.ds(0, tm), :], buf_ref.at[slot, k], sems.at[slot]).wait()

    route = route_ref[...]
    ffn = route[:, 2:3] * buf_ref[slot, 0] + route[:, 3:4] * buf_ref[slot, 1]
    o_ref[...] = _layer_norm(alpha * h_ref[...] + ffn, g_ref[...], b_ref[...])


def _combine(idx_t, pad_starts, route, h, g, bta, y, alpha):
    n, d = h.shape
    tm = COMB_TM
    const = lambda i: (0, 0)
    return pl.pallas_call(
        functools.partial(_combine_kernel, alpha),
        grid=(n // tm,),
        in_specs=[pl.BlockSpec((1, 1, 4 * tm), lambda i: (i, 0, 0), memory_space=pltpu.SMEM),
                  pl.BlockSpec((1, 1, 4 * tm), lambda i: (jnp.minimum(i + 1, n // tm - 1), 0, 0),
                               memory_space=pltpu.SMEM),
                  pl.BlockSpec(memory_space=pltpu.SMEM),
                  pl.BlockSpec((tm, LANES), lambda i: (i, 0)),
                  pl.BlockSpec((tm, d), lambda i: (i, 0)),
                  pl.BlockSpec((1, d), const), pl.BlockSpec((1, d), const),
                  pl.BlockSpec(memory_space=pl.ANY)],
        out_specs=pl.BlockSpec((tm, d), lambda i: (i, 0)),
        out_shape=jax.ShapeDtypeStruct((n, d), f32),
        scratch_shapes=[pltpu.VMEM((2, 2, tm, d), f32), pltpu.SemaphoreType.DMA((2,))],
        compiler_params=_cparams(("arbitrary",)),
        name="combine",
    )(idx_t, idx_t, pad_starts, route, h, g, bta, y)


def _tile_idx(idx, tm):
    n = idx.shape[1]
    return idx.reshape(4, n // tm, tm).transpose(1, 0, 2).reshape(n // tm, 1, 4 * tm)


def _layer(x, positions, w_in, cmp_k_pe, cmp_k_w1, cmp_k_b1, cmp_k_w2, cmp_v_pe, cmp_v_w1, cmp_v_b1, cmp_v_w2,
           conv_w, conv_b, dt_bias, a_log, d_skip, ssm_norm_w, w_out, ln1_g, ln1_b,
           w_router_group, b_router_group, w_router_expert, b_router_expert, w_gate, w_up, w_down, ln2_g, ln2_b,
           alpha):
    b, t, d = x.shape
    n = b * t
    x2 = x.reshape(n, d)

    c0 = NSA_WIDTH
    c1 = c0 + 6 * KV_WIDTH
    c2 = c1 + 3 * NSA_HEADS
    c3 = c2 + SSM_WIDTH
    c4 = c3 + XBC_WIDTH
    w_small = jnp.concatenate([w_in[:, c1:c2], w_in[:, c4:], jnp.zeros((d, LANES - 3 * NSA_HEADS - SSM_HEADS), f32)], axis=1)
    w_cat = jnp.concatenate([w_in[:, :c0], w_in[:, c2:c3], w_in[:, c3:c4], w_in[:, c0:c1], w_small,
                             jnp.zeros((d, PROJ_COLS - COL_SMALL - LANES), f32)], axis=1).astype(bf16)
    proj = _proj(x2, w_cat)

    lane = np.arange(LANES) % HEAD_DIM
    inv_freq = ROPE_THETA ** (-jnp.arange(0, ROT_DIM, 2, dtype=f32) / ROT_DIM)
    invf = jnp.where(lane < ROT_DIM, inv_freq[lane % (ROT_DIM // 2)], 0.0).astype(f32)[None, :]
    pos128 = jnp.broadcast_to(positions.reshape(n, 1), (n, LANES))
    q_r, k_cmp, v_cmp, k_sel, v_sel, k_win, v_win = _nsa_prep(proj, pos128, invf, b, t)

    nc = t // CMP_STRIDE
    half_w = CMP_STRIDE * HEAD_DIM
    a = jnp.stack([k_cmp, v_cmp]).reshape(2, b * NSA_KV_GROUPS, nc, half_w)
    pe = jnp.stack([cmp_k_pe, cmp_v_pe]).reshape(2, 2, 1, half_w)
    w1 = jnp.stack([cmp_k_w1, cmp_v_w1]).reshape(2, 2, half_w, CMP_HIDDEN).astype(bf16)
    b1 = jnp.stack([cmp_k_b1, cmp_v_b1]).reshape(2, 1, CMP_HIDDEN)
    w2 = jnp.pad(jnp.stack([cmp_k_w2, cmp_v_w2]), ((0, 0), (0, 0), (0, LANES - HEAD_DIM))).astype(bf16)
    cend = jnp.minimum(jnp.arange(nc) * CMP_STRIDE + CMP_BLOCK - 1, t - 1)
    posc = jnp.broadcast_to(positions[:, cend][:, :, None], (b, nc, LANES))
    kvc, kvc_t = _cmp_mlp(a, pe, w1, b1, w2, posc, invf, b)

    c_start = np.arange(nc)[:, None] * CMP_STRIDE
    s_start = np.arange(LANES)[None, :] * SEL_BLOCK
    cover = ((c_start < s_start + SEL_BLOCK) & (c_start + CMP_BLOCK > s_start)
             & (np.arange(nc)[:, None] < nc - 1) & (np.arange(LANES)[None, :] < t // SEL_BLOCK))
    cover = jnp.asarray(cover, bf16)
    o_cmp, selb = _cmp_attn(q_r, kvc, kvc_t, cover, b, t)

    y_nsa = _nsa_attn(q_r, k_sel, v_sel, k_win, v_win, selb, o_cmp, proj, b, t)

    dt_raw = proj[:, COL_SMALL + SMALL_DT_OFF:COL_SMALL + SMALL_DT_OFF + SSM_HEADS]
    dtt = dt_raw.reshape(b, t, SSM_HEADS).transpose(0, 2, 1)
    y_ssm = _ssd(proj, dtt, conv_w.reshape(CONV_WIDTH, XBC_WIDTH), conv_b.reshape(1, XBC_WIDTH),
                 dt_bias.reshape(1, SSM_HEADS), dt_bias.reshape(SSM_HEADS, 1),
                 a_log.reshape(1, SSM_HEADS), a_log.reshape(SSM_HEADS, 1),
                 jnp.repeat(d_skip, SSM_HEADDIM).reshape(1, SSM_WIDTH), ssm_norm_w.reshape(1, SSM_WIDTH), b, t)

    wr = jnp.concatenate([w_router_group, w_router_expert,
                          jnp.zeros((d, LANES - N_EXPERT_GROUPS - N_EXPERTS), f32)], axis=1)
    br = jnp.concatenate([b_router_group, b_router_expert,
                          jnp.zeros((LANES - N_EXPERT_GROUPS - N_EXPERTS,), f32)])[None, :]
    wr_hi = wr.astype(bf16)
    wr = jnp.concatenate([wr_hi, (wr - wr_hi.astype(f32)).astype(bf16)], axis=1)
    wo = w_out.astype(bf16)
    h, route, counts = _outproj(y_nsa, y_ssm, x2, wo[:NSA_WIDTH], wo[NSA_WIDTH:], ln1_g[None, :], ln1_b[None, :],
                                wr, br, alpha)

    cnt = counts[0, :N_EXPERTS].astype(i32)
    padded = (cnt + MOE_TM - 1) // MOE_TM * MOE_TM
    pad_ends = jnp.cumsum(padded)
    pad_starts = pad_ends - padded
    idx = jnp.concatenate([route[:, 0:2], route[:, 4:6]], axis=1).astype(i32).T
    p_rows = 2 * n + N_EXPERTS * MOE_TM
    nb = p_rows // MOE_TM
    block_e = jnp.minimum(jnp.sum(jnp.arange(nb, dtype=i32)[:, None] * MOE_TM >= pad_ends[None, :], axis=-1),
                          N_EXPERTS - 1).astype(i32)
    n_used = (pad_ends[-1] // MOE_TM).astype(i32).reshape(1)
    blk = jnp.arange(nb, dtype=i32)
    last_of_expert = jnp.any((blk[:, None] + 1) * MOE_TM == pad_ends[None, :], axis=-1)
    zflag = (last_of_expert | (blk >= n_used[0])).astype(i32)
    block_e = jnp.where(blk < n_used[0], block_e, block_e[jnp.maximum(n_used[0] - 1, 0)])

    xs = _dispatch(_tile_idx(idx, DISP_TM), pad_starts, zflag, h, p_rows)
    y = _experts(block_e, n_used, xs, w_gate, w_up, w_down)
    out = _combine(_tile_idx(idx, COMB_TM), pad_starts, route, h, ln2_g[None, :], ln2_b[None, :], y, alpha)
    return out.reshape(b, t, d)


def kernel(x, positions, w_in, cmp_k_pe, cmp_k_w1, cmp_k_b1, cmp_k_w2, cmp_v_pe, cmp_v_w1, cmp_v_b1, cmp_v_w2, conv_w, conv_b, dt_bias, a_log, d_skip, ssm_norm_w, w_out, ln1_g, ln1_b, w_router_group, b_router_group, w_router_expert, b_router_expert, w_gate, w_up, w_down, ln2_g, ln2_b):
    depth = w_in.shape[0]
    alpha = (2 * depth) ** 0.25
    params = (w_in, cmp_k_pe, cmp_k_w1, cmp_k_b1, cmp_k_w2, cmp_v_pe, cmp_v_w1, cmp_v_b1, cmp_v_w2, conv_w, conv_b,
              dt_bias, a_log, d_skip, ssm_norm_w, w_out, ln1_g, ln1_b, w_router_group, b_router_group,
              w_router_expert, b_router_expert, w_gate, w_up, w_down, ln2_g, ln2_b)
    for l in range(depth):
        x = _layer(x, positions, *[p[l] for p in params], alpha)
    return x
```

```python
import functools
import math

import jax
import jax.numpy as jnp
import numpy as np
from jax import lax
from jax.experimental import pallas as pl
from jax.experimental.pallas import tpu as pltpu

f32 = jnp.float32
bf16 = jnp.bfloat16
i32 = jnp.int32

HEAD_DIM = 64
NSA_HEADS = 16
NSA_KV_GROUPS = 2
NSA_REP = NSA_HEADS // NSA_KV_GROUPS
NSA_WIDTH = NSA_HEADS * HEAD_DIM
KV_WIDTH = NSA_KV_GROUPS * HEAD_DIM
CMP_BLOCK = 32
CMP_STRIDE = 16
CMP_HIDDEN = 256
SEL_BLOCK = 64
SEL_TOPK = 16
WINDOW = 512
FORCED_SCORE = 1.0e4
SSM_HEADDIM = 64
SSM_HEADS = 16
SSM_WIDTH = SSM_HEADS * SSM_HEADDIM
SSM_GROUPS = 4
SSM_REP = SSM_HEADS // SSM_GROUPS
SSM_STATE = 128
CONV_WIDTH = 4
CHUNK = 256
XBC_WIDTH = SSM_WIDTH + 2 * SSM_GROUPS * SSM_STATE
ROPE_THETA = 500000.0
ROT_DIM = HEAD_DIM // 4
N_EXPERT_GROUPS = 4
EXPERTS_PER_GROUP = 8
N_EXPERTS = N_EXPERT_GROUPS * EXPERTS_PER_GROUP
D_EXPERT = 512
NORM_EPS = 1e-5

LANES = 128
MASK_NEG = -1.0e30

PROJ_TM = 1024
PROJ_TN = 512
PREP_TM = 512
ATT_TQ = 256
ATT_TK = 512
WIN_TK = 256
KX_WIDTH = 2 * LANES
OUT_TM = 512
MOE_TM = 256
DISP_TM = 256
COMB_TM = 128
VMEM_LIMIT = 56 * 1024 * 1024

COL_Q = 0
COL_Z = NSA_WIDTH
COL_XBC = COL_Z + SSM_WIDTH
COL_KV = COL_XBC + XBC_WIDTH
COL_SMALL = COL_KV + 6 * KV_WIDTH
PROJ_COLS = 5120
SMALL_DT_OFF = 3 * NSA_HEADS


def _cparams(sem, vmem=VMEM_LIMIT):
    return pltpu.CompilerParams(dimension_semantics=sem, vmem_limit_bytes=vmem)


def _dot(a, b):
    return jnp.dot(a, b, preferred_element_type=f32)


def _dot_t(a, b):
    return lax.dot_general(a, b, (((1,), (1,)), ((), ())), preferred_element_type=f32)


def _dot_hi(a, b):
    return jnp.dot(a, b, preferred_element_type=f32, precision=lax.Precision.HIGHEST)


def _proj_kernel(x_ref, w_ref, o_ref, xb_ref):
    @pl.when(pl.program_id(1) == 0)
    def _():
        xb_ref[...] = x_ref[...].astype(bf16)

    o_ref[...] = _dot(xb_ref[...], w_ref[...])


def _proj(x2, w_cat):
    n, d = x2.shape
    cols = w_cat.shape[1]
    tm = min(PROJ_TM, n)
    return pl.pallas_call(
        _proj_kernel,
        grid=(n // tm, cols // PROJ_TN),
        in_specs=[pl.BlockSpec((tm, d), lambda i, j: (i, 0)),
                  pl.BlockSpec((d, PROJ_TN), lambda i, j: (0, j))],
        out_specs=pl.BlockSpec((tm, PROJ_TN), lambda i, j: (i, j)),
        out_shape=jax.ShapeDtypeStruct((n, cols), f32),
        scratch_shapes=[pltpu.VMEM((tm, d), bf16)],
        compiler_params=_cparams(("parallel", "arbitrary")),
        name="proj",
    )(x2, w_cat)


def _rope_tables(pos_i32, invf):
    ang = pos_i32.astype(f32) * invf
    return jnp.cos(ang), jnp.sin(ang)


def _rope128(x, cos, sin):
    half = ROT_DIM // 2
    d = lax.broadcasted_iota(i32, x.shape, 1) % HEAD_DIM
    up = pltpu.roll(x, LANES - half, 1)
    dn = pltpu.roll(x, half, 1)
    rot = jnp.where(d < half, -up, dn)
    return x * cos + rot * sin


def _nsa_prep_kernel(pos_ref, invf_ref, q_ref, kc_ref, vc_ref, ks_ref, vs_ref, kw_ref, vw_ref,
                     qo_ref, kco_ref, vco_ref, kso_ref, vso_ref, kwo_ref, vwo_ref):
    cos, sin = _rope_tables(pos_ref[...], invf_ref[...])
    scale = HEAD_DIM ** -0.5
    for c in range(NSA_WIDTH // LANES):
        t = _rope128(q_ref[:, c * LANES:(c + 1) * LANES], cos, sin) * scale
        qo_ref[0, 2 * c] = t[:, :HEAD_DIM].astype(bf16)
        qo_ref[0, 2 * c + 1] = t[:, HEAD_DIM:].astype(bf16)

    def split(src, dst, rope, dt):
        t = src[...]
        if rope:
            t = _rope128(t, cos, sin)
        for g in range(NSA_KV_GROUPS):
            dst[0, g] = t[:, g * HEAD_DIM:(g + 1) * HEAD_DIM].astype(dt)

    split(kc_ref, kco_ref, False, f32)
    split(vc_ref, vco_ref, False, f32)

    tm = pos_ref.shape[0]

    def values_with_ones(src, dst):
        t = src[...]
        lane = lax.broadcasted_iota(i32, (tm, HEAD_DIM), 1)
        tail = jnp.where(lane == 0, 1.0, 0.0).astype(bf16)
        for g in range(NSA_KV_GROUPS):
            vg = t[:, g * HEAD_DIM:(g + 1) * HEAD_DIM].astype(bf16)
            dst[0, g] = jnp.concatenate([vg, tail], axis=1)

    values_with_ones(vs_ref, vso_ref)
    values_with_ones(vw_ref, vwo_ref)

    kw_t = _rope128(kw_ref[...], cos, sin).T
    ks_t = _rope128(ks_ref[...], cos, sin).T
    for g in range(NSA_KV_GROUPS):
        for c in range(tm // WIN_TK):
            kwo_ref[0, g, c] = kw_t[g * HEAD_DIM:(g + 1) * HEAD_DIM, c * WIN_TK:(c + 1) * WIN_TK].astype(bf16)
    blk = lax.broadcasted_iota(i32, (LANES, tm), 0)
    tok = pl.program_id(1) * tm + lax.broadcasted_iota(i32, (LANES, tm), 1)
    onehot_t = jnp.where(tok // SEL_BLOCK == blk, 1.0, 0.0).astype(bf16)
    for g in range(NSA_KV_GROUPS):
        kso_ref[0, g, 0, 0:LANES, :] = onehot_t
        kg = ks_t[g * HEAD_DIM:(g + 1) * HEAD_DIM, :].astype(bf16)
        kso_ref[0, g, 0, LANES:2 * LANES, :] = jnp.concatenate([kg, jnp.zeros_like(kg)], axis=0)


def _nsa_prep(proj, pos128, invf, b, t):
    tm = PREP_TM
    assert tm == ATT_TK
    nt = t // tm
    row = lambda bi, ti: (bi * nt + ti, 0)
    kv0 = COL_KV // LANES
    in_specs = [pl.BlockSpec((tm, LANES), row),
                pl.BlockSpec((1, LANES), lambda bi, ti: (0, 0)),
                pl.BlockSpec((tm, NSA_WIDTH), lambda bi, ti: (bi * nt + ti, COL_Q // NSA_WIDTH))]
    for k in range(6):
        in_specs.append(pl.BlockSpec((tm, LANES), functools.partial(lambda bi, ti, k: (bi * nt + ti, kv0 + k), k=k)))
    head = lambda bi, ti: (bi, 0, ti, 0)
    tile5 = lambda bi, ti: (bi, 0, ti, 0, 0)
    g = NSA_KV_GROUPS
    out_specs = [pl.BlockSpec((1, NSA_HEADS, tm, HEAD_DIM), head),
                 pl.BlockSpec((1, g, tm, HEAD_DIM), head), pl.BlockSpec((1, g, tm, HEAD_DIM), head),
                 pl.BlockSpec((1, g, 1, KX_WIDTH, tm), tile5), pl.BlockSpec((1, g, tm, LANES), head),
                 pl.BlockSpec((1, g, tm // WIN_TK, HEAD_DIM, WIN_TK), tile5), pl.BlockSpec((1, g, tm, LANES), head)]
    out_shape = [jax.ShapeDtypeStruct((b, NSA_HEADS, t, HEAD_DIM), bf16),
                 jax.ShapeDtypeStruct((b, g, t, HEAD_DIM), f32), jax.ShapeDtypeStruct((b, g, t, HEAD_DIM), f32),
                 jax.ShapeDtypeStruct((b, g, nt, KX_WIDTH, tm), bf16), jax.ShapeDtypeStruct((b, g, t, LANES), bf16),
                 jax.ShapeDtypeStruct((b, g, t // WIN_TK, HEAD_DIM, WIN_TK), bf16),
                 jax.ShapeDtypeStruct((b, g, t, LANES), bf16)]
    return pl.pallas_call(
        _nsa_prep_kernel,
        grid=(b, nt),
        in_specs=in_specs,
        out_specs=out_specs,
        out_shape=out_shape,
        compiler_params=_cparams(("parallel", "parallel")),
        name="nsa_prep",
    )(pos128, invf, proj, proj, proj, proj, proj, proj, proj)


def _cmp_mlp_kernel(a_ref, pe_ref, w1_ref, b1_ref, w2_ref, pos_ref, invf_ref, o_ref, ot_ref):
    kind = pl.program_id(0)
    a = a_ref[0, 0]
    nc = a.shape[0]
    u = _dot((a + pe_ref[0, 0]).astype(bf16), w1_ref[0, 0])
    v = _dot((a + pe_ref[0, 1]).astype(bf16), w1_ref[0, 1])
    v_next = pltpu.roll(v, nc - 1, 0)
    hid = jax.nn.gelu(u + v_next + b1_ref[0])
    out = _dot(hid.astype(bf16), w2_ref[0])
    cos, sin = _rope_tables(pos_ref[0], invf_ref[...])
    roped = _rope128(out, cos, sin)
    out = jnp.where(kind == 0, roped, out)
    o_ref[0, 0] = out[:, :HEAD_DIM].astype(bf16)
    ot_ref[0, 0] = out.T[:HEAD_DIM, :].astype(bf16)


def _cmp_mlp(a, pe, w1, b1, w2, posc, invf, b):
    _, bg, nc, hw = a.shape
    g = bg // b
    return pl.pallas_call(
        _cmp_mlp_kernel,
        grid=(2, bg),
        in_specs=[pl.BlockSpec((1, 1, nc, hw), lambda k, i: (k, i, 0, 0)),
                  pl.BlockSpec((1, 2, 1, hw), lambda k, i: (k, 0, 0, 0)),
                  pl.BlockSpec((1, 2, hw, CMP_HIDDEN), lambda k, i: (k, 0, 0, 0)),
                  pl.BlockSpec((1, 1, CMP_HIDDEN), lambda k, i: (k, 0, 0)),
                  pl.BlockSpec((1, CMP_HIDDEN, LANES), lambda k, i: (k, 0, 0)),
                  pl.BlockSpec((1, nc, LANES), lambda k, i: (i // g, 0, 0)),
                  pl.BlockSpec((1, LANES), lambda k, i: (0, 0))],
        out_specs=[pl.BlockSpec((1, 1, nc, HEAD_DIM), lambda k, i: (k, i, 0, 0)),
                   pl.BlockSpec((1, 1, HEAD_DIM, nc), lambda k, i: (k, i, 0, 0))],
        out_shape=[jax.ShapeDtypeStruct((2, bg, nc, HEAD_DIM), bf16),
                   jax.ShapeDtypeStruct((2, bg, HEAD_DIM, nc), bf16)],
        compiler_params=_cparams(("parallel", "parallel")),
        name="cmp_mlp",
    )(a, pe, w1, b1, w2, posc, invf)


def _cmp_attn_kernel(q_ref, kct_ref, vc_ref, cover_ref, oc_ref, sel_ref, imp_ref):
    qi = pl.program_id(2)
    nc = vc_ref.shape[2]
    rows = NSA_REP * ATT_TQ
    tq = qi * ATT_TQ + lax.broadcasted_iota(i32, (ATT_TQ, 1), 0)
    row_live = jnp.where(tq >= CMP_BLOCK - 1, 1.0, 0.0)
    tiny = jnp.finfo(f32).tiny

    def attend(ncols):
        kct = kct_ref[0, 0, :, :ncols]
        vc = vc_ref[0, 0, :ncols, :]
        cend = lax.broadcasted_iota(i32, (1, ncols), 1) * CMP_STRIDE + (CMP_BLOCK - 1)
        bias = jnp.where(cend <= tq, 0.0, MASK_NEG)
        s = _dot(q_ref[0].reshape(rows, HEAD_DIM), kct).reshape(NSA_REP, ATT_TQ, ncols) + bias[None]
        e = jnp.exp(s - jnp.max(s, axis=-1, keepdims=True))
        live = row_live[None]
        p = e * (live / jnp.maximum(live * jnp.sum(e, axis=-1, keepdims=True), tiny))
        oc_ref[0] = _dot(p.reshape(rows, ncols).astype(bf16), vc).reshape(NSA_REP, ATT_TQ, HEAD_DIM)
        psum = jnp.sum(p, axis=0)
        hi = psum.astype(bf16)
        lo = (psum - hi.astype(f32)).astype(bf16)
        cover = cover_ref[:ncols, :]
        imp_ref[...] = _dot(hi, cover) + _dot(lo, cover)

    need = (qi + 1) * (ATT_TQ // CMP_STRIDE)
    for ncols in range(LANES, nc + 1, LANES):
        @pl.when((need > ncols - LANES) & (need <= ncols))
        def _():
            attend(ncols)

    imp = imp_ref[...]
    j = lax.broadcasted_iota(i32, (ATT_TQ, LANES), 1)
    cur = tq // SEL_BLOCK
    forced = (j == 0) | (j == cur) | (j == cur - 1)
    valid = j * SEL_BLOCK <= tq
    imp = jnp.where(valid, jnp.where(forced, FORCED_SCORE, imp), -FORCED_SCORE)

    jt = lax.broadcasted_iota(i32, (LANES, ATT_TQ), 0)

    def pick(_, carry):
        work, sel = carry
        m = jnp.max(work, axis=0, keepdims=True)
        first = jnp.min(jnp.where(work == m, jt, LANES), axis=0, keepdims=True)
        hit = jt == first
        return jnp.where(hit, -jnp.inf, work), jnp.where(hit, 1.0, sel)

    _, sel_t = lax.fori_loop(0, SEL_TOPK, pick, (imp.T, jnp.zeros((LANES, ATT_TQ), f32)))
    sel_ref[0, 0] = jnp.where(valid, jnp.where(sel_t.T > 0.0, 0.0, MASK_NEG), MASK_NEG).astype(bf16)


def _cmp_attn(q_r, kvc, kvc_t, cover, b, t):
    g = NSA_KV_GROUPS
    nc = kvc.shape[2]
    nq = t // ATT_TQ
    return pl.pallas_call(
        _cmp_attn_kernel,
        grid=(b, g, nq),
        in_specs=[pl.BlockSpec((1, NSA_REP, ATT_TQ, HEAD_DIM), lambda bi, gi, qi: (bi, gi, qi, 0)),
                  pl.BlockSpec((1, 1, HEAD_DIM, nc), lambda bi, gi, qi: (0, bi * g + gi, 0, 0)),
                  pl.BlockSpec((1, 1, nc, HEAD_DIM), lambda bi, gi, qi: (1, bi * g + gi, 0, 0)),
                  pl.BlockSpec((nc, LANES), lambda bi, gi, qi: (0, 0))],
        out_specs=[pl.BlockSpec((1, NSA_REP, ATT_TQ, HEAD_DIM), lambda bi, gi, qi: (bi, gi, qi, 0)),
                   pl.BlockSpec((1, 1, ATT_TQ, LANES), lambda bi, gi, qi: (bi, gi, qi, 0))],
        out_shape=[jax.ShapeDtypeStruct((b, NSA_HEADS, t, HEAD_DIM), f32),
                   jax.ShapeDtypeStruct((b, g, t, LANES), bf16)],
        scratch_shapes=[pltpu.VMEM((ATT_TQ, LANES), f32)],
        compiler_params=_cparams(("parallel", "parallel", "parallel")),
        name="cmp_attn",
    )(q_r, kvc_t, kvc, cover)


def _nsa_attn_kernel(q_ref, kx_ref, vs_ref, kw_ref, vw_ref, sel_ref, oc_ref, gate_ref,
                     o_ref, qx_ref, s_ref, m_ref, acc_ref, gx_ref, yw_ref):
    gi = pl.program_id(1)
    qi = pl.program_id(2)
    start = qi * ATT_TQ
    tq = start + lax.broadcasted_iota(i32, (ATT_TQ, 1), 0)

    selb = sel_ref[0, 0]
    for r in range(NSA_REP):
        qx_ref[r * ATT_TQ:(r + 1) * ATT_TQ, 0:LANES] = selb
        qr = q_ref[0, r]
        qx_ref[r * ATT_TQ:(r + 1) * ATT_TQ, LANES:2 * LANES] = jnp.concatenate([qr, jnp.zeros_like(qr)], axis=1)
    rows = NSA_REP * ATT_TQ

    m_ref[...] = jnp.full(m_ref.shape, MASK_NEG, f32)
    acc_ref[...] = jnp.zeros(acc_ref.shape, f32)

    def consume(kt, v_ref, bias):
        k0 = pl.multiple_of(kt * ATT_TK, ATT_TK)
        v = v_ref[0, 0, pl.ds(k0, ATT_TK), :]
        s = s_ref[...]
        if bias is not None:
            s = (s.reshape(NSA_REP, ATT_TQ, ATT_TK) + bias[None]).reshape(rows, ATT_TK)
        m_prev = m_ref[...]
        m_new = jnp.maximum(m_prev, jnp.max(s, axis=-1, keepdims=True))
        alpha = jnp.exp(m_prev - m_new)
        p = jnp.exp(s - jnp.concatenate([m_new] * (ATT_TK // LANES), axis=1))
        acc_ref[...] = alpha * acc_ref[...] + _dot(p.astype(bf16), v)
        m_ref[...] = m_new

    def kpos(kt):
        return kt * ATT_TK + lax.broadcasted_iota(i32, (1, ATT_TK), 1)

    last = start // ATT_TK
    s_ref[...] = _dot(qx_ref[...], kx_ref[0, 0, 0])

    def sel_step(kt, _):
        consume(kt, vs_ref, None)
        s_ref[...] = _dot(qx_ref[...], kx_ref[0, 0, kt + 1])
        return 0

    n_win = (WINDOW + ATT_TQ) // WIN_TK
    w_first = jnp.maximum(start // WIN_TK - WINDOW // WIN_TK, 0)
    kw = jnp.concatenate([kw_ref[0, 0, w_first + i] for i in range(n_win)], axis=1)
    w0 = pl.multiple_of(w_first * WIN_TK, WIN_TK)
    vw = vw_ref[0, 0, pl.ds(w0, n_win * WIN_TK), :]
    kp = w0 + lax.broadcasted_iota(i32, (1, n_win * WIN_TK), 1)
    wbias = jnp.where(kp <= tq, jnp.where(kp > tq - WINDOW, 0.0, MASK_NEG), MASK_NEG)
    hh = NSA_REP // 2
    gates = jax.nn.sigmoid(gate_ref[...])
    per_group = 3 * NSA_REP
    shifted = gates
    for gg in range(1, NSA_KV_GROUPS):
        shifted = jnp.where(gi == gg, pltpu.roll(gates, LANES - gg * per_group, 1), shifted)

    def gate(r, br):
        c = r * 3 + br
        return jnp.broadcast_to(shifted[:, c:c + 1], (ATT_TQ, HEAD_DIM))

    for half in range(2):
        qh = q_ref[0, half * hh:(half + 1) * hh].reshape(hh * ATT_TQ, HEAD_DIM)
        sw = _dot(qh, kw).reshape(hh, ATT_TQ, n_win * WIN_TK) + wbias[None]
        sw = sw.reshape(hh * ATT_TQ, n_win * WIN_TK)
        pw = jnp.exp(sw - jnp.max(sw, axis=-1, keepdims=True))
        aw = _dot(pw.astype(bf16), vw)
        for i in range(hh):
            r = half * hh + i
            a = aw[i * ATT_TQ:(i + 1) * ATT_TQ]
            yw_ref[r] = a[:, :HEAD_DIM] * (gate(r, 2) / a[:, HEAD_DIM:HEAD_DIM + 1])
            gx_ref[0, r] = gate(r, 0)
            gx_ref[1, r] = gate(r, 1)

    def sel_pair(j, _):
        sel_step(2 * j, 0)
        sel_step(2 * j + 1, 0)
        return 0

    lax.fori_loop(0, last // 2, sel_pair, 0)

    @pl.when(last % 2 == 1)
    def _():
        sel_step(last - 1, 0)

    consume(last, vs_ref, jnp.where(kpos(last) <= tq, 0.0, MASK_NEG))

    for r in range(NSA_REP):
        asel = acc_ref[r * ATT_TQ:(r + 1) * ATT_TQ]
        osel = asel[:, :HEAD_DIM] / asel[:, HEAD_DIM:HEAD_DIM + 1]
        o_ref[:, r * HEAD_DIM:(r + 1) * HEAD_DIM] = gx_ref[0, r] * oc_ref[0, r] + gx_ref[1, r] * osel + yw_ref[r]


def _nsa_attn(q_r, kx, vs, kw, vw, selb, oc, proj, b, t):
    g = NSA_KV_GROUPS
    nq = t // ATT_TQ
    vspec = pl.BlockSpec((1, 1, t, LANES), lambda bi, gi, qi: (bi, gi, 0, 0))
    hspec = pl.BlockSpec((1, NSA_REP, ATT_TQ, HEAD_DIM), lambda bi, gi, qi: (bi, gi, qi, 0))
    return pl.pallas_call(
        _nsa_attn_kernel,
        grid=(b, g, nq),
        in_specs=[hspec,
                  pl.BlockSpec((1, 1, t // ATT_TK, KX_WIDTH, ATT_TK), lambda bi, gi, qi: (bi, gi, 0, 0, 0)),
                  vspec,
                  pl.BlockSpec((1, 1, t // WIN_TK, HEAD_DIM, WIN_TK), lambda bi, gi, qi: (bi, gi, 0, 0, 0)),
                  vspec,
                  pl.BlockSpec((1, 1, ATT_TQ, LANES), lambda bi, gi, qi: (bi, gi, qi, 0)),
                  hspec,
                  pl.BlockSpec((ATT_TQ, LANES), lambda bi, gi, qi: (bi * nq + qi, COL_SMALL // LANES))],
        out_specs=pl.BlockSpec((ATT_TQ, NSA_REP * HEAD_DIM), lambda bi, gi, qi: (bi * nq + qi, gi)),
        out_shape=jax.ShapeDtypeStruct((b * t, NSA_WIDTH), f32),
        scratch_shapes=[pltpu.VMEM((NSA_REP * ATT_TQ, KX_WIDTH), bf16),
                        pltpu.VMEM((NSA_REP * ATT_TQ, ATT_TK), f32),
                        pltpu.VMEM((NSA_REP * ATT_TQ, LANES), f32),
                        pltpu.VMEM((NSA_REP * ATT_TQ, LANES), f32),
                        pltpu.VMEM((2, NSA_REP, ATT_TQ, HEAD_DIM), f32),
                        pltpu.VMEM((NSA_REP, ATT_TQ, HEAD_DIM), f32)],
        compiler_params=_cparams(("parallel", "parallel", "arbitrary")),
        name="nsa_attn",
    )(q_r, kx, vs, kw, vw, selb, oc, proj)


HALO = 8


def _ssd_kernel(xbc_ref, z_ref, small_ref, dtt_ref, cw_ref, cb_ref, dtb_r_ref, dtb_c_ref,
                alog_r_ref, alog_c_ref, dskip_ref, nw_ref, o_ref, ext_ref, st_ref):
    c = pl.program_id(1)
    L = CHUNK

    @pl.when(c == 0)
    def _():
        ext_ref[0:HALO, :] = jnp.zeros((HALO, XBC_WIDTH), f32)
        st_ref[...] = jnp.zeros_like(st_ref)

    ext_ref[HALO:HALO + L, :] = xbc_ref[...]
    conv = cb_ref[...]
    for k in range(CONV_WIDTH):
        off = HALO - (CONV_WIDTH - 1) + k
        conv = conv + cw_ref[k:k + 1, :] * ext_ref[off:off + L, :]
    ext_ref[0:HALO, :] = ext_ref[L:L + HALO, :]
    act = conv * jax.nn.sigmoid(conv)
    xs = act[:, :SSM_WIDTH]
    bm = act[:, SSM_WIDTH:SSM_WIDTH + SSM_GROUPS * SSM_STATE]
    cm = act[:, SSM_WIDTH + SSM_GROUPS * SSM_STATE:]

    dt_c = jax.nn.softplus(small_ref[:, SMALL_DT_OFF:SMALL_DT_OFF + SSM_HEADS] + dtb_r_ref[...])
    dt_r = jax.nn.softplus(dtt_ref[0] + dtb_c_ref[...])
    a_r = -jnp.exp(alog_r_ref[...])
    a_c = -jnp.exp(alog_c_ref[...])
    row = lax.broadcasted_iota(i32, (L, L), 0)
    col = lax.broadcasted_iota(i32, (L, L), 1)
    causal = col <= row
    tri = jnp.where(causal, 1.0, 0.0)
    acs_c = _dot_hi(tri, dt_c * a_r)
    acs_r = _dot_hi(dt_r * a_c, jnp.where(row <= col, 1.0, 0.0))

    z = z_ref[...]
    ys = []
    for g in range(SSM_GROUPS):
        cg = cm[:, g * SSM_STATE:(g + 1) * SSM_STATE].astype(bf16)
        bg = bm[:, g * SSM_STATE:(g + 1) * SSM_STATE]
        cb = _dot_t(cg, bg.astype(bf16))
        bg_t = bg.T
        ssq = jnp.zeros((L, 1), f32)
        yg = []
        for r in range(SSM_REP):
            h = g * SSM_REP + r
            hs = slice(h * SSM_HEADDIM, (h + 1) * SSM_HEADDIM)
            a_col = acs_c[:, h:h + 1]
            a_row = acs_r[h:h + 1, :]
            dt_row = dt_r[h:h + 1, :]
            a_last = acs_r[h:h + 1, L - 1:L]
            seg = a_col - a_row
            decay = jnp.where(causal, jnp.exp(jnp.where(causal, seg, 0.0)), 0.0)
            w = cb * decay * dt_row
            x_h = xs[:, hs]
            xb = x_h.astype(bf16)
            st = st_ref[h]
            y = _dot(w.astype(bf16), xb)
            y = y + _dot(cg, st.astype(bf16)) * jnp.exp(a_col)
            y = y + dskip_ref[:, hs] * x_h
            bscaled = bg_t * (jnp.exp(a_last - a_row) * dt_row)
            st_ref[h] = jnp.exp(a_last) * st + _dot(bscaled.astype(bf16), xb)
            zh = z[:, hs]
            y = y * (zh * jax.nn.sigmoid(zh))
            ssq = ssq + jnp.sum(y * y, axis=-1, keepdims=True)
            yg.append(y)
        rs = lax.rsqrt(ssq / (SSM_REP * SSM_HEADDIM) + NORM_EPS)
        for r in range(SSM_REP):
            h = g * SSM_REP + r
            hs = slice(h * SSM_HEADDIM, (h + 1) * SSM_HEADDIM)
            o_ref[:, hs] = yg[r] * rs * nw_ref[:, hs]


def _ssd(proj, dtt, cw, cb, dtb_r, dtb_c, alog_r, alog_c, dskip, nw, b, t):
    nch = t // CHUNK
    row = lambda bi, ci: bi * nch + ci
    const2 = lambda bi, ci: (0, 0)
    return pl.pallas_call(
        _ssd_kernel,
        grid=(b, nch),
        in_specs=[pl.BlockSpec((CHUNK, XBC_WIDTH), lambda bi, ci: (row(bi, ci), COL_XBC // XBC_WIDTH)),
                  pl.BlockSpec((CHUNK, SSM_WIDTH), lambda bi, ci: (row(bi, ci), COL_Z // SSM_WIDTH)),
                  pl.BlockSpec((CHUNK, LANES), lambda bi, ci: (row(bi, ci), COL_SMALL // LANES)),
                  pl.BlockSpec((1, SSM_HEADS, CHUNK), lambda bi, ci: (bi, 0, ci)),
                  pl.BlockSpec((CONV_WIDTH, XBC_WIDTH), const2),
                  pl.BlockSpec((1, XBC_WIDTH), const2),
                  pl.BlockSpec((1, SSM_HEADS), const2),
                  pl.BlockSpec((SSM_HEADS, 1), const2),
                  pl.BlockSpec((1, SSM_HEADS), const2),
                  pl.BlockSpec((SSM_HEADS, 1), const2),
                  pl.BlockSpec((1, SSM_WIDTH), const2),
                  pl.BlockSpec((1, SSM_WIDTH), const2)],
        out_specs=pl.BlockSpec((CHUNK, SSM_WIDTH), lambda bi, ci: (row(bi, ci), 0)),
        out_shape=jax.ShapeDtypeStruct((b * t, SSM_WIDTH), f32),
        scratch_shapes=[pltpu.VMEM((HALO + CHUNK, XBC_WIDTH), f32),
                        pltpu.VMEM((SSM_HEADS, SSM_STATE, SSM_HEADDIM), f32)],
        compiler_params=_cparams(("parallel", "arbitrary")),
        name="ssd",
    )(proj, proj, proj, dtt, cw, cb, dtb_r, dtb_c, alog_r, alog_c, dskip, nw)


def _layer_norm(v, g, b):
    mu = jnp.mean(v, axis=-1, keepdims=True)
    d = v - mu
    var = jnp.mean(d * d, axis=-1, keepdims=True)
    return d * lax.rsqrt(var + NORM_EPS) * g + b


def _outproj_kernel(alpha, ya_ref, yb_ref, x_ref, wa_ref, wb_ref, g_ref, b_ref, wr_ref, br_ref,
                    h_ref, route_ref, cnt_ref):
    i = pl.program_id(0)
    tm = x_ref.shape[0]
    mix = _dot(ya_ref[...].astype(bf16), wa_ref[...]) + _dot(yb_ref[...].astype(bf16), wb_ref[...])
    h = _layer_norm(alpha * x_ref[...] + mix, g_ref[...], b_ref[...])
    h_ref[...] = h

    h_hi = h.astype(bf16)
    h_lo = (h - h_hi.astype(f32)).astype(bf16)
    t = _dot(h_hi, wr_ref[...])
    logits = t[:, :LANES] + t[:, LANES:] + _dot(h_lo, wr_ref[:, :LANES]) + br_ref[...]
    lane = lax.broadcasted_iota(i32, (tm, LANES), 1)
    ninf = -jnp.inf
    gmask = lane < N_EXPERT_GROUPS
    gl = jnp.where(gmask, logits, ninf)
    ge = jnp.where(gmask, jnp.exp(gl - jnp.max(gl, axis=-1, keepdims=True)), 0.0)
    pg = ge / jnp.sum(ge, axis=-1, keepdims=True)
    g_gate = jnp.max(pg, axis=-1, keepdims=True)
    g_sel = jnp.min(jnp.where(gmask & (pg == g_gate), lane, LANES), axis=-1, keepdims=True)
    lo = N_EXPERT_GROUPS + g_sel * EXPERTS_PER_GROUP
    emask = (lane >= lo) & (lane < lo + EXPERTS_PER_GROUP)
    el = jnp.where(emask, logits, ninf)
    ee = jnp.where(emask, jnp.exp(el - jnp.max(el, axis=-1, keepdims=True)), 0.0)
    pe = ee / jnp.sum(ee, axis=-1, keepdims=True)
    p0 = jnp.max(pe, axis=-1, keepdims=True)
    l0 = jnp.min(jnp.where(emask & (pe == p0), lane, LANES), axis=-1, keepdims=True)
    rest = jnp.where(emask & (lane != l0), pe, ninf)
    p1 = jnp.max(rest, axis=-1, keepdims=True)
    l1 = jnp.min(jnp.where(rest == p1, lane, LANES), axis=-1, keepdims=True)
    psum = p0 + p1
    w0 = g_gate * p0 / psum
    w1 = g_gate * p1 / psum
    e0 = l0 - N_EXPERT_GROUPS
    e1 = l1 - N_EXPERT_GROUPS

    @pl.when(i == 0)
    def _():
        cnt_ref[...] = jnp.zeros_like(cnt_ref)

    oh0 = lane == e0
    oh1 = lane == e1
    both = jnp.where(oh0, 1.0, 0.0) + jnp.where(oh1, 1.0, 0.0)
    r_i = lax.broadcasted_iota(i32, (tm, tm), 0)
    c_i = lax.broadcasted_iota(i32, (tm, tm), 1)
    strict = jnp.where(c_i < r_i, 1.0, 0.0).astype(bf16)
    before = _dot(strict, both.astype(bf16)) + cnt_ref[...]
    rank0 = jnp.sum(jnp.where(oh0, before, 0.0), axis=-1, keepdims=True)
    rank1 = jnp.sum(jnp.where(oh1, before, 0.0), axis=-1, keepdims=True)
    cnt_ref[...] = cnt_ref[...] + jnp.sum(both, axis=0, keepdims=True)

    out = jnp.where(lane == 0, e0.astype(f32), 0.0)
    out = jnp.where(lane == 1, e1.astype(f32), out)
    out = jnp.where(lane == 2, w0, out)
    out = jnp.where(lane == 3, w1, out)
    out = jnp.where(lane == 4, rank0, out)
    out = jnp.where(lane == 5, rank1, out)
    route_ref[...] = out


def _outproj(y_nsa, y_ssm, x2, wa, wb, g, bta, wr, br, alpha):
    n, d = x2.shape
    tm = OUT_TM
    const = lambda i: (0, 0)
    rowb = lambda i: (i, 0)
    return pl.pallas_call(
        functools.partial(_outproj_kernel, alpha),
        grid=(n // tm,),
        in_specs=[pl.BlockSpec((tm, NSA_WIDTH), rowb), pl.BlockSpec((tm, SSM_WIDTH), rowb),
                  pl.BlockSpec((tm, d), rowb),
                  pl.BlockSpec((NSA_WIDTH, d), const), pl.BlockSpec((SSM_WIDTH, d), const),
                  pl.BlockSpec((1, d), const), pl.BlockSpec((1, d), const),
                  pl.BlockSpec((d, 2 * LANES), const), pl.BlockSpec((1, LANES), const)],
        out_specs=[pl.BlockSpec((tm, d), rowb), pl.BlockSpec((tm, LANES), rowb),
                   pl.BlockSpec((1, LANES), const)],
        out_shape=[jax.ShapeDtypeStruct((n, d), f32), jax.ShapeDtypeStruct((n, LANES), f32),
                   jax.ShapeDtypeStruct((1, LANES), f32)],
        compiler_params=_cparams(("arbitrary",)),
        name="outproj",
    )(y_nsa, y_ssm, x2, wa, wb, g, bta, wr, br)


def _row_copy(src_ref, src_row, dst_ref, dst_row, sem):
    return pltpu.make_async_copy(src_ref.at[pl.ds(src_row, 1), :], dst_ref.at[pl.ds(dst_row, 1), :], sem)


def _dest_row(idx_ref, ps_ref, tm, k, r):
    return ps_ref[idx_ref[0, 0, k * tm + r]] + idx_ref[0, 0, (2 + k) * tm + r]


def _dispatch_kernel(idx_ref, ps_ref, zflag_ref, h_ref, xs_ref, zero_ref, tile_ref, sem, lsem, zsem):
    tm = DISP_TM
    nb = zflag_ref.shape[0]
    i = pl.program_id(0)

    @pl.when(i == 0)
    def _():
        zero_ref[...] = jnp.zeros_like(zero_ref)

        def zblock(i):
            rows = pl.ds(pl.multiple_of(i * MOE_TM, MOE_TM), MOE_TM)
            return pltpu.make_async_copy(zero_ref, xs_ref.at[rows, :], zsem)

        def zstart(i, _):
            @pl.when(zflag_ref[i] != 0)
            def _():
                zblock(i).start()
            return 0

        def zwait(i, _):
            @pl.when(zflag_ref[i] != 0)
            def _():
                zblock(i).wait()
            return 0

        lax.fori_loop(0, nb, zstart, 0)
        lax.fori_loop(0, nb, zwait, 0)

    nslot = tile_ref.shape[0]
    nsteps = pl.num_programs(0)
    slot = i % nslot

    def tile_load(step, s):
        rows = pl.ds(pl.multiple_of(step * tm, tm), tm)
        return pltpu.make_async_copy(h_ref.at[rows, :], tile_ref.at[s], lsem.at[s])

    @pl.when(i == 0)
    def _():
        tile_load(0, 0).start()

    tile_load(i, slot).wait()

    @pl.when(i + 1 < nsteps)
    def _():
        tile_load(i + 1, (i + 1) % nslot).start()

    def issue(r, _):
        for k in range(2):
            _row_copy(tile_ref.at[slot], r, xs_ref, _dest_row(idx_ref, ps_ref, tm, k, r), sem.at[slot]).start()
        return 0

    lax.fori_loop(0, tm, issue, 0, unroll=4)

    def wait_rows(s):
        for k in range(2):
            pltpu.make_async_copy(tile_ref.at[s], xs_ref.at[pl.ds(0, tm), :], sem.at[s]).wait()

    @pl.when(i > 0)
    def _():
        wait_rows((i - 1) % nslot)

    @pl.when(i == nsteps - 1)
    def _():
        wait_rows(slot)


def _dispatch(idx_t, pad_starts, zflag, h, p_rows):
    n, d = h.shape
    tm = DISP_TM
    return pl.pallas_call(
        _dispatch_kernel,
        grid=(n // tm,),
        in_specs=[pl.BlockSpec((1, 1, 4 * tm), lambda i: (i, 0, 0), memory_space=pltpu.SMEM),
                  pl.BlockSpec(memory_space=pltpu.SMEM),
                  pl.BlockSpec(memory_space=pltpu.SMEM),
                  pl.BlockSpec(memory_space=pl.ANY)],
        out_specs=pl.BlockSpec(memory_space=pl.ANY),
        out_shape=jax.ShapeDtypeStruct((p_rows, d), f32),
        scratch_shapes=[pltpu.VMEM((MOE_TM, d), f32), pltpu.VMEM((3, tm, d), f32),
                        pltpu.SemaphoreType.DMA((3,)), pltpu.SemaphoreType.DMA((3,)), pltpu.SemaphoreType.DMA(())],
        compiler_params=_cparams(("arbitrary",)),
        name="dispatch",
    )(idx_t, pad_starts, zflag, h)


def _experts_kernel(be_ref, nu_ref, first_ref, next_ref, slot_ref, xs_ref, wg_ref, wu_ref, wd_ref, y_ref,
                    fg_ref, fu_ref, fd_ref, wgb_ref, wub_ref, wdb_ref, sems):
    i = pl.program_id(0)

    def weight_loads(e, s):
        return (pltpu.make_async_copy(wg_ref.at[e], fg_ref.at[s], sems.at[s, 0]),
                pltpu.make_async_copy(wu_ref.at[e], fu_ref.at[s], sems.at[s, 1]),
                pltpu.make_async_copy(wd_ref.at[e], fd_ref.at[s], sems.at[s, 2]))

    @pl.when(i == 0)
    def _():
        for c in weight_loads(be_ref[0], 0):
            c.start()

    @pl.when(first_ref[i] != 0)
    def _():
        s = slot_ref[i]
        for c in weight_loads(be_ref[i], s):
            c.wait()
        wgb_ref[...] = fg_ref[s].astype(bf16)
        wub_ref[...] = fu_ref[s].astype(bf16)
        wdb_ref[...] = fd_ref[s].astype(bf16)

        @pl.when(next_ref[i] >= 0)
        def _():
            for c in weight_loads(next_ref[i], 1 - s):
                c.start()

    @pl.when(i < nu_ref[0])
    def _():
        xb = xs_ref[...].astype(bf16)
        gte = _dot(xb, wgb_ref[...])
        up = _dot(xb, wub_ref[...])
        act = gte * jax.nn.sigmoid(gte) * up
        y_ref[...] = _dot(act.astype(bf16), wdb_ref[...])

    @pl.when(i >= nu_ref[0])
    def _():
        y_ref[...] = jnp.zeros_like(y_ref)


def _experts(block_e, n_used, first, next_e, slot, xs, w_gate, w_up, w_down):
    p_rows, d = xs.shape
    tm = MOE_TM
    nb = p_rows // tm
    de = w_gate.shape[-1]
    xmap = lambda i, be, nu, fi, ne, sl: (jnp.maximum(jnp.minimum(i, nu[0] - 1), 0), 0)
    hbm = pl.BlockSpec(memory_space=pl.ANY)
    return pl.pallas_call(
        _experts_kernel,
        grid_spec=pltpu.PrefetchScalarGridSpec(
            num_scalar_prefetch=5,
            grid=(nb,),
            in_specs=[pl.BlockSpec((tm, d), xmap), hbm, hbm, hbm],
            out_specs=pl.BlockSpec((tm, d), lambda i, be, nu, fi, ne, sl: (i, 0)),
            scratch_shapes=[pltpu.VMEM((2, d, de), f32), pltpu.VMEM((2, d, de), f32), pltpu.VMEM((2, de, d), f32),
                            pltpu.VMEM((d, de), bf16), pltpu.VMEM((d, de), bf16), pltpu.VMEM((de, d), bf16),
                            pltpu.SemaphoreType.DMA((2, 3))],
        ),
        out_shape=jax.ShapeDtypeStruct((p_rows, d), f32),
        compiler_params=_cparams(("arbitrary",)),
        name="experts",
    )(block_e, n_used, first, next_e, slot, xs, w_gate, w_up, w_down)


def _combine_kernel(alpha, idx_ref, nidx_ref, ps_ref, route_ref, h_ref, g_ref, b_ref, y_ref, o_ref, buf_ref, sems):
    tm = h_ref.shape[0]
    i = pl.program_id(0)
    slot = i % 2

    def gather(ids_ref, s):
        def issue(r, _):
            for k in range(2):
                _row_copy(y_ref, _dest_row(ids_ref, ps_ref, tm, k, r), buf_ref.at[s, k], r, sems.at[s]).start()
            return 0
        lax.fori_loop(0, tm, issue, 0, unroll=4)

    @pl.when(i == 0)
    def _():
        gather(idx_ref, 0)

    @pl.when(i + 1 < pl.num_programs(0))
    def _():
        gather(nidx_ref, 1 - slot)

    for k in range(2):
        pltpu.make_async_copy(y_ref.at[pl.ds(0, tm), :], buf_ref.at[slot, k], sems.at[slot]).wait()

    route = route_ref[...]
    ffn = route[:, 2:3] * buf_ref[slot, 0] + route[:, 3:4] * buf_ref[slot, 1]
    o_ref[...] = _layer_norm(alpha * h_ref[...] + ffn, g_ref[...], b_ref[...])


def _combine(idx_t, pad_starts, route, h, g, bta, y, alpha):
    n, d = h.shape
    tm = COMB_TM
    const = lambda i: (0, 0)
    return pl.pallas_call(
        functools.partial(_combine_kernel, alpha),
        grid=(n // tm,),
        in_specs=[pl.BlockSpec((1, 1, 4 * tm), lambda i: (i, 0, 0), memory_space=pltpu.SMEM),
                  pl.BlockSpec((1, 1, 4 * tm), lambda i: (jnp.minimum(i + 1, n // tm - 1), 0, 0),
                               memory_space=pltpu.SMEM),
                  pl.BlockSpec(memory_space=pltpu.SMEM),
                  pl.BlockSpec((tm, LANES), lambda i: (i, 0)),
                  pl.BlockSpec((tm, d), lambda i: (i, 0)),
                  pl.BlockSpec((1, d), const), pl.BlockSpec((1, d), const),
                  pl.BlockSpec(memory_space=pl.ANY)],
        out_specs=pl.BlockSpec((tm, d), lambda i: (i, 0)),
        out_shape=jax.ShapeDtypeStruct((n, d), f32),
        scratch_shapes=[pltpu.VMEM((2, 2, tm, d), f32), pltpu.SemaphoreType.DMA((2,))],
        compiler_params=_cparams(("arbitrary",)),
        name="combine",
    )(idx_t, idx_t, pad_starts, route, h, g, bta, y)


def _tile_idx(idx, tm):
    n = idx.shape[1]
    return idx.reshape(4, n // tm, tm).transpose(1, 0, 2).reshape(n // tm, 1, 4 * tm)


def _layer(x, positions, w_in, cmp_k_pe, cmp_k_w1, cmp_k_b1, cmp_k_w2, cmp_v_pe, cmp_v_w1, cmp_v_b1, cmp_v_w2,
           conv_w, conv_b, dt_bias, a_log, d_skip, ssm_norm_w, w_out, ln1_g, ln1_b,
           w_router_group, b_router_group, w_router_expert, b_router_expert, w_gate, w_up, w_down, ln2_g, ln2_b,
           alpha):
    b, t, d = x.shape
    n = b * t
    x2 = x.reshape(n, d)

    c0 = NSA_WIDTH
    c1 = c0 + 6 * KV_WIDTH
    c2 = c1 + 3 * NSA_HEADS
    c3 = c2 + SSM_WIDTH
    c4 = c3 + XBC_WIDTH
    w_small = jnp.concatenate([w_in[:, c1:c2], w_in[:, c4:], jnp.zeros((d, LANES - 3 * NSA_HEADS - SSM_HEADS), f32)], axis=1)
    w_cat = jnp.concatenate([w_in[:, :c0], w_in[:, c2:c3], w_in[:, c3:c4], w_in[:, c0:c1], w_small,
                             jnp.zeros((d, PROJ_COLS - COL_SMALL - LANES), f32)], axis=1).astype(bf16)
    proj = _proj(x2, w_cat)

    lane = np.arange(LANES) % HEAD_DIM
    inv_freq = ROPE_THETA ** (-jnp.arange(0, ROT_DIM, 2, dtype=f32) / ROT_DIM)
    invf = jnp.where(lane < ROT_DIM, inv_freq[lane % (ROT_DIM // 2)], 0.0).astype(f32)[None, :]
    pos128 = jnp.broadcast_to(positions.reshape(n, 1), (n, LANES))
    q_r, k_cmp, v_cmp, k_sel, v_sel, k_win, v_win = _nsa_prep(proj, pos128, invf, b, t)

    nc = t // CMP_STRIDE
    half_w = CMP_STRIDE * HEAD_DIM
    a = jnp.stack([k_cmp, v_cmp]).reshape(2, b * NSA_KV_GROUPS, nc, half_w)
    pe = jnp.stack([cmp_k_pe, cmp_v_pe]).reshape(2, 2, 1, half_w)
    w1 = jnp.stack([cmp_k_w1, cmp_v_w1]).reshape(2, 2, half_w, CMP_HIDDEN).astype(bf16)
    b1 = jnp.stack([cmp_k_b1, cmp_v_b1]).reshape(2, 1, CMP_HIDDEN)
    w2 = jnp.pad(jnp.stack([cmp_k_w2, cmp_v_w2]), ((0, 0), (0, 0), (0, LANES - HEAD_DIM))).astype(bf16)
    cend = jnp.minimum(jnp.arange(nc) * CMP_STRIDE + CMP_BLOCK - 1, t - 1)
    posc = jnp.broadcast_to(positions[:, cend][:, :, None], (b, nc, LANES))
    kvc, kvc_t = _cmp_mlp(a, pe, w1, b1, w2, posc, invf, b)

    c_start = np.arange(nc)[:, None] * CMP_STRIDE
    s_start = np.arange(LANES)[None, :] * SEL_BLOCK
    cover = ((c_start < s_start + SEL_BLOCK) & (c_start + CMP_BLOCK > s_start)
             & (np.arange(nc)[:, None] < nc - 1) & (np.arange(LANES)[None, :] < t // SEL_BLOCK))
    cover = jnp.asarray(cover, bf16)
    o_cmp, selb = _cmp_attn(q_r, kvc, kvc_t, cover, b, t)

    y_nsa = _nsa_attn(q_r, k_sel, v_sel, k_win, v_win, selb, o_cmp, proj, b, t)

    dt_raw = proj[:, COL_SMALL + SMALL_DT_OFF:COL_SMALL + SMALL_DT_OFF + SSM_HEADS]
    dtt = dt_raw.reshape(b, t, SSM_HEADS).transpose(0, 2, 1)
    y_ssm = _ssd(proj, dtt, conv_w.reshape(CONV_WIDTH, XBC_WIDTH), conv_b.reshape(1, XBC_WIDTH),
                 dt_bias.reshape(1, SSM_HEADS), dt_bias.reshape(SSM_HEADS, 1),
                 a_log.reshape(1, SSM_HEADS), a_log.reshape(SSM_HEADS, 1),
                 jnp.repeat(d_skip, SSM_HEADDIM).reshape(1, SSM_WIDTH), ssm_norm_w.reshape(1, SSM_WIDTH), b, t)

    wr = jnp.concatenate([w_router_group, w_router_expert,
                          jnp.zeros((d, LANES - N_EXPERT_GROUPS - N_EXPERTS), f32)], axis=1)
    br = jnp.concatenate([b_router_group, b_router_expert,
                          jnp.zeros((LANES - N_EXPERT_GROUPS - N_EXPERTS,), f32)])[None, :]
    wr_hi = wr.astype(bf16)
    wr = jnp.concatenate([wr_hi, (wr - wr_hi.astype(f32)).astype(bf16)], axis=1)
    wo = w_out.astype(bf16)
    h, route, counts = _outproj(y_nsa, y_ssm, x2, wo[:NSA_WIDTH], wo[NSA_WIDTH:], ln1_g[None, :], ln1_b[None, :],
                                wr, br, alpha)

    cnt = counts[0, :N_EXPERTS].astype(i32)
    padded = (cnt + MOE_TM - 1) // MOE_TM * MOE_TM
    pad_ends = jnp.cumsum(padded)
    pad_starts = pad_ends - padded
    idx = jnp.concatenate([route[:, 0:2], route[:, 4:6]], axis=1).astype(i32).T
    p_rows = 2 * n + N_EXPERTS * MOE_TM
    nb = p_rows // MOE_TM
    block_e = jnp.minimum(jnp.sum(jnp.arange(nb, dtype=i32)[:, None] * MOE_TM >= pad_ends[None, :], axis=-1),
                          N_EXPERTS - 1).astype(i32)
    n_used = (pad_ends[-1] // MOE_TM).astype(i32).reshape(1)
    blk = jnp.arange(nb, dtype=i32)
    last_of_expert = jnp.any((blk[:, None] + 1) * MOE_TM == pad_ends[None, :], axis=-1)
    zflag = (last_of_expert | (blk >= n_used[0])).astype(i32)
    block_e = jnp.where(blk < n_used[0], block_e, block_e[jnp.maximum(n_used[0] - 1, 0)])

    xs = _dispatch(_tile_idx(idx, DISP_TM), pad_starts, zflag, h, p_rows)
    prev_e = jnp.concatenate([jnp.full((1,), -1, i32), block_e[:-1]])
    first = ((block_e != prev_e) & (blk < n_used[0])).astype(i32)
    eidx = jnp.arange(N_EXPERTS, dtype=i32)
    later = jnp.where((eidx[None, :] > eidx[:, None]) & (padded[None, :] > 0), eidx[None, :], N_EXPERTS)
    next_of = jnp.min(later, axis=1)
    next_e = jnp.where(next_of[block_e] < N_EXPERTS, next_of[block_e], -1).astype(i32)
    slot = ((jnp.cumsum(first) - 1) % 2).astype(i32)
    y = _experts(block_e, n_used, first, next_e, slot, xs, w_gate, w_up, w_down)
    out = _combine(_tile_idx(idx, COMB_TM), pad_starts, route, h, ln2_g[None, :], ln2_b[None, :], y, alpha)
    return out.reshape(b, t, d)


def kernel(x, positions, w_in, cmp_k_pe, cmp_k_w1, cmp_k_b1, cmp_k_w2, cmp_v_pe, cmp_v_w1, cmp_v_b1, cmp_v_w2, conv_w, conv_b, dt_bias, a_log, d_skip, ssm_norm_w, w_out, ln1_g, ln1_b, w_router_group, b_router_group, w_router_expert, b_router_expert, w_gate, w_up, w_down, ln2_g, ln2_b):
    depth = w_in.shape[0]
    alpha = (2 * depth) ** 0.25
    params = (w_in, cmp_k_pe, cmp_k_w1, cmp_k_b1, cmp_k_w2, cmp_v_pe, cmp_v_w1, cmp_v_b1, cmp_v_w2, conv_w, conv_b,
              dt_bias, a_log, d_skip, ssm_norm_w, w_out, ln1_g, ln1_b, w_router_group, b_router_group,
              w_router_expert, b_router_expert, w_gate, w_up, w_down, ln2_g, ln2_b)
    for l in range(depth):
        x = _layer(x, positions, *[p[l] for p in params], alpha)
    return x
```

```python
import functools
import math

import jax
import jax.numpy as jnp
import numpy as np
from jax import lax
from jax.experimental import pallas as pl
from jax.experimental.pallas import tpu as pltpu

f32 = jnp.float32
bf16 = jnp.bfloat16
i32 = jnp.int32

HEAD_DIM = 64
NSA_HEADS = 16
NSA_KV_GROUPS = 2
NSA_REP = NSA_HEADS // NSA_KV_GROUPS
NSA_WIDTH = NSA_HEADS * HEAD_DIM
KV_WIDTH = NSA_KV_GROUPS * HEAD_DIM
CMP_BLOCK = 32
CMP_STRIDE = 16
CMP_HIDDEN = 256
SEL_BLOCK = 64
SEL_TOPK = 16
WINDOW = 512
FORCED_SCORE = 1.0e4
SSM_HEADDIM = 64
SSM_HEADS = 16
SSM_WIDTH = SSM_HEADS * SSM_HEADDIM
SSM_GROUPS = 4
SSM_REP = SSM_HEADS // SSM_GROUPS
SSM_STATE = 128
CONV_WIDTH = 4
CHUNK = 256
XBC_WIDTH = SSM_WIDTH + 2 * SSM_GROUPS * SSM_STATE
ROPE_THETA = 500000.0
ROT_DIM = HEAD_DIM // 4
N_EXPERT_GROUPS = 4
EXPERTS_PER_GROUP = 8
N_EXPERTS = N_EXPERT_GROUPS * EXPERTS_PER_GROUP
D_EXPERT = 512
NORM_EPS = 1e-5

LANES = 128
MASK_NEG = -1.0e30

PROJ_TM = 1024
PROJ_TN = 1024
PREP_TM = 512
CMP_TQ = 256
ATT_TQ = 512
ATT_TK = 512
WIN_TK = 256
WIN_HEADS = 4
KX_WIDTH = 2 * LANES
OUT_TM = 512
MOE_TM = 256
DISP_TM = 256
COMB_TM = 128
VMEM_LIMIT = 56 * 1024 * 1024

COL_Q = 0
COL_Z = NSA_WIDTH
COL_XBC = COL_Z + SSM_WIDTH
COL_KV = COL_XBC + XBC_WIDTH
COL_SMALL = COL_KV + 6 * KV_WIDTH
PROJ_COLS = 5120
SMALL_DT_OFF = 3 * NSA_HEADS


def _cparams(sem, vmem=VMEM_LIMIT):
    return pltpu.CompilerParams(dimension_semantics=sem, vmem_limit_bytes=vmem)


def _dot(a, b):
    return jnp.dot(a, b, preferred_element_type=f32)


def _dot_t(a, b):
    return lax.dot_general(a, b, (((1,), (1,)), ((), ())), preferred_element_type=f32)


def _dot_hi(a, b):
    return jnp.dot(a, b, preferred_element_type=f32, precision=lax.Precision.HIGHEST)


def _proj_kernel(x_ref, w_ref, o_ref, xb_ref):
    @pl.when(pl.program_id(1) == 0)
    def _():
        xb_ref[...] = x_ref[...].astype(bf16)

    o_ref[...] = _dot(xb_ref[...], w_ref[...])


def _proj(x2, w_cat):
    n, d = x2.shape
    cols = w_cat.shape[1]
    tm = min(PROJ_TM, n)
    return pl.pallas_call(
        _proj_kernel,
        grid=(n // tm, cols // PROJ_TN),
        in_specs=[pl.BlockSpec((tm, d), lambda i, j: (i, 0)),
                  pl.BlockSpec((d, PROJ_TN), lambda i, j: (0, j))],
        out_specs=pl.BlockSpec((tm, PROJ_TN), lambda i, j: (i, j)),
        out_shape=jax.ShapeDtypeStruct((n, cols), f32),
        scratch_shapes=[pltpu.VMEM((tm, d), bf16)],
        compiler_params=_cparams(("parallel", "arbitrary")),
        name="proj",
    )(x2, w_cat)


def _rope_tables(pos_i32, invf):
    ang = pos_i32.astype(f32) * invf
    return jnp.cos(ang), jnp.sin(ang)


def _rope128(x, cos, sin):
    half = ROT_DIM // 2
    d = lax.broadcasted_iota(i32, x.shape, 1) % HEAD_DIM
    up = pltpu.roll(x, LANES - half, 1)
    dn = pltpu.roll(x, half, 1)
    rot = jnp.where(d < half, -up, dn)
    return x * cos + rot * sin


def _nsa_prep_kernel(pos_ref, invf_ref, q_ref, kc_ref, vc_ref, ks_ref, vs_ref, kw_ref, vw_ref,
                     qo_ref, kco_ref, vco_ref, kso_ref, vso_ref, kwo_ref, vwo_ref):
    cos, sin = _rope_tables(pos_ref[...], invf_ref[...])
    scale = HEAD_DIM ** -0.5
    for c in range(NSA_WIDTH // LANES):
        t = _rope128(q_ref[:, c * LANES:(c + 1) * LANES], cos, sin) * scale
        qo_ref[0, 2 * c] = t[:, :HEAD_DIM].astype(bf16)
        qo_ref[0, 2 * c + 1] = t[:, HEAD_DIM:].astype(bf16)

    def split(src, dst, rope, dt):
        t = src[...]
        if rope:
            t = _rope128(t, cos, sin)
        for g in range(NSA_KV_GROUPS):
            dst[0, g] = t[:, g * HEAD_DIM:(g + 1) * HEAD_DIM].astype(dt)

    split(kc_ref, kco_ref, False, f32)
    split(vc_ref, vco_ref, False, f32)

    tm = pos_ref.shape[0]

    def values_with_ones(src, dst):
        t = src[...]
        lane = lax.broadcasted_iota(i32, (tm, HEAD_DIM), 1)
        tail = jnp.where(lane == 0, 1.0, 0.0).astype(bf16)
        for g in range(NSA_KV_GROUPS):
            vg = t[:, g * HEAD_DIM:(g + 1) * HEAD_DIM].astype(bf16)
            dst[0, g] = jnp.concatenate([vg, tail], axis=1)

    values_with_ones(vs_ref, vso_ref)
    values_with_ones(vw_ref, vwo_ref)

    kw_t = _rope128(kw_ref[...], cos, sin).T
    ks_t = _rope128(ks_ref[...], cos, sin).T
    for g in range(NSA_KV_GROUPS):
        for c in range(tm // WIN_TK):
            kwo_ref[0, g, c] = kw_t[g * HEAD_DIM:(g + 1) * HEAD_DIM, c * WIN_TK:(c + 1) * WIN_TK].astype(bf16)
    blk = lax.broadcasted_iota(i32, (LANES, tm), 0)
    tok = pl.program_id(1) * tm + lax.broadcasted_iota(i32, (LANES, tm), 1)
    onehot_t = jnp.where(tok // SEL_BLOCK == blk, 1.0, 0.0).astype(bf16)
    for g in range(NSA_KV_GROUPS):
        kso_ref[0, g, 0, 0:LANES, :] = onehot_t
        kg = ks_t[g * HEAD_DIM:(g + 1) * HEAD_DIM, :].astype(bf16)
        kso_ref[0, g, 0, LANES:2 * LANES, :] = jnp.concatenate([kg, jnp.zeros_like(kg)], axis=0)


def _nsa_prep(proj, pos128, invf, b, t):
    tm = PREP_TM
    assert tm == ATT_TK
    nt = t // tm
    row = lambda bi, ti: (bi * nt + ti, 0)
    kv0 = COL_KV // LANES
    in_specs = [pl.BlockSpec((tm, LANES), row),
                pl.BlockSpec((1, LANES), lambda bi, ti: (0, 0)),
                pl.BlockSpec((tm, NSA_WIDTH), lambda bi, ti: (bi * nt + ti, COL_Q // NSA_WIDTH))]
    for k in range(6):
        in_specs.append(pl.BlockSpec((tm, LANES), functools.partial(lambda bi, ti, k: (bi * nt + ti, kv0 + k), k=k)))
    head = lambda bi, ti: (bi, 0, ti, 0)
    tile5 = lambda bi, ti: (bi, 0, ti, 0, 0)
    g = NSA_KV_GROUPS
    out_specs = [pl.BlockSpec((1, NSA_HEADS, tm, HEAD_DIM), head),
                 pl.BlockSpec((1, g, tm, HEAD_DIM), head), pl.BlockSpec((1, g, tm, HEAD_DIM), head),
                 pl.BlockSpec((1, g, 1, KX_WIDTH, tm), tile5), pl.BlockSpec((1, g, tm, LANES), head),
                 pl.BlockSpec((1, g, tm // WIN_TK, HEAD_DIM, WIN_TK), tile5), pl.BlockSpec((1, g, tm, LANES), head)]
    out_shape = [jax.ShapeDtypeStruct((b, NSA_HEADS, t, HEAD_DIM), bf16),
                 jax.ShapeDtypeStruct((b, g, t, HEAD_DIM), f32), jax.ShapeDtypeStruct((b, g, t, HEAD_DIM), f32),
                 jax.ShapeDtypeStruct((b, g, nt, KX_WIDTH, tm), bf16), jax.ShapeDtypeStruct((b, g, t, LANES), bf16),
                 jax.ShapeDtypeStruct((b, g, t // WIN_TK, HEAD_DIM, WIN_TK), bf16),
                 jax.ShapeDtypeStruct((b, g, t, LANES), bf16)]
    return pl.pallas_call(
        _nsa_prep_kernel,
        grid=(b, nt),
        in_specs=in_specs,
        out_specs=out_specs,
        out_shape=out_shape,
        compiler_params=_cparams(("parallel", "parallel")),
        name="nsa_prep",
    )(pos128, invf, proj, proj, proj, proj, proj, proj, proj)


def _cmp_mlp_kernel(a_ref, pe_ref, w1_ref, b1_ref, w2_ref, pos_ref, invf_ref, o_ref, ot_ref):
    kind = pl.program_id(0)
    a = a_ref[0, 0]
    nc = a.shape[0]
    u = _dot((a + pe_ref[0, 0]).astype(bf16), w1_ref[0, 0])
    v = _dot((a + pe_ref[0, 1]).astype(bf16), w1_ref[0, 1])
    v_next = pltpu.roll(v, nc - 1, 0)
    hid = jax.nn.gelu(u + v_next + b1_ref[0])
    out = _dot(hid.astype(bf16), w2_ref[0])
    cos, sin = _rope_tables(pos_ref[0], invf_ref[...])
    roped = _rope128(out, cos, sin)
    out = jnp.where(kind == 0, roped, out)
    o_ref[0, 0] = out[:, :HEAD_DIM].astype(bf16)
    ot_ref[0, 0] = out.T[:HEAD_DIM, :].astype(bf16)


def _cmp_mlp(a, pe, w1, b1, w2, posc, invf, b):
    _, bg, nc, hw = a.shape
    g = bg // b
    return pl.pallas_call(
        _cmp_mlp_kernel,
        grid=(2, bg),
        in_specs=[pl.BlockSpec((1, 1, nc, hw), lambda k, i: (k, i, 0, 0)),
                  pl.BlockSpec((1, 2, 1, hw), lambda k, i: (k, 0, 0, 0)),
                  pl.BlockSpec((1, 2, hw, CMP_HIDDEN), lambda k, i: (k, 0, 0, 0)),
                  pl.BlockSpec((1, 1, CMP_HIDDEN), lambda k, i: (k, 0, 0)),
                  pl.BlockSpec((1, CMP_HIDDEN, LANES), lambda k, i: (k, 0, 0)),
                  pl.BlockSpec((1, nc, LANES), lambda k, i: (i // g, 0, 0)),
                  pl.BlockSpec((1, LANES), lambda k, i: (0, 0))],
        out_specs=[pl.BlockSpec((1, 1, nc, HEAD_DIM), lambda k, i: (k, i, 0, 0)),
                   pl.BlockSpec((1, 1, HEAD_DIM, nc), lambda k, i: (k, i, 0, 0))],
        out_shape=[jax.ShapeDtypeStruct((2, bg, nc, HEAD_DIM), bf16),
                   jax.ShapeDtypeStruct((2, bg, HEAD_DIM, nc), bf16)],
        compiler_params=_cparams(("parallel", "parallel")),
        name="cmp_mlp",
    )(a, pe, w1, b1, w2, posc, invf)


def _cmp_attn_kernel(q_ref, kct_ref, vc_ref, cover_ref, oc_ref, sel_ref, imp_ref):
    qi = pl.program_id(2)
    nc = vc_ref.shape[2]
    rows = NSA_REP * CMP_TQ
    tq = qi * CMP_TQ + lax.broadcasted_iota(i32, (CMP_TQ, 1), 0)
    row_live = jnp.where(tq >= CMP_BLOCK - 1, 1.0, 0.0)
    tiny = jnp.finfo(f32).tiny

    def attend(ncols):
        kct = kct_ref[0, 0, :, :ncols]
        vc = vc_ref[0, 0, :ncols, :]
        cend = lax.broadcasted_iota(i32, (1, ncols), 1) * CMP_STRIDE + (CMP_BLOCK - 1)
        bias = jnp.where(cend <= tq, 0.0, MASK_NEG)
        s = _dot(q_ref[0].reshape(rows, HEAD_DIM), kct).reshape(NSA_REP, CMP_TQ, ncols) + bias[None]
        e = jnp.exp(s - jnp.max(s, axis=-1, keepdims=True))
        live = row_live[None]
        p = e * (live / jnp.maximum(live * jnp.sum(e, axis=-1, keepdims=True), tiny))
        oc_ref[0] = _dot(p.reshape(rows, ncols).astype(bf16), vc).reshape(NSA_REP, CMP_TQ, HEAD_DIM)
        psum = jnp.sum(p, axis=0)
        hi = psum.astype(bf16)
        lo = (psum - hi.astype(f32)).astype(bf16)
        cover = cover_ref[:ncols, :]
        imp_ref[...] = _dot(hi, cover) + _dot(lo, cover)

    need = (qi + 1) * (CMP_TQ // CMP_STRIDE)
    for ncols in range(LANES, nc + 1, LANES):
        @pl.when((need > ncols - LANES) & (need <= ncols))
        def _():
            attend(ncols)

    imp = imp_ref[...]
    j = lax.broadcasted_iota(i32, (CMP_TQ, LANES), 1)
    cur = tq // SEL_BLOCK
    forced = (j == 0) | (j == cur) | (j == cur - 1)
    valid = j * SEL_BLOCK <= tq
    imp = jnp.where(valid, jnp.where(forced, FORCED_SCORE, imp), -FORCED_SCORE)

    jt = lax.broadcasted_iota(i32, (LANES, CMP_TQ), 0)

    def pick(_, carry):
        work, sel = carry
        m = jnp.max(work, axis=0, keepdims=True)
        first = jnp.min(jnp.where(work == m, jt, LANES), axis=0, keepdims=True)
        hit = jt == first
        return jnp.where(hit, -jnp.inf, work), jnp.where(hit, 1.0, sel)

    _, sel_t = lax.fori_loop(0, SEL_TOPK, pick, (imp.T, jnp.zeros((LANES, CMP_TQ), f32)))
    sel_ref[0, 0] = jnp.where(valid, jnp.where(sel_t.T > 0.0, 0.0, MASK_NEG), MASK_NEG).astype(bf16)


def _cmp_attn(q_r, kvc, kvc_t, cover, b, t):
    g = NSA_KV_GROUPS
    nc = kvc.shape[2]
    nq = t // CMP_TQ
    return pl.pallas_call(
        _cmp_attn_kernel,
        grid=(b, g, nq),
        in_specs=[pl.BlockSpec((1, NSA_REP, CMP_TQ, HEAD_DIM), lambda bi, gi, qi: (bi, gi, qi, 0)),
                  pl.BlockSpec((1, 1, HEAD_DIM, nc), lambda bi, gi, qi: (0, bi * g + gi, 0, 0)),
                  pl.BlockSpec((1, 1, nc, HEAD_DIM), lambda bi, gi, qi: (1, bi * g + gi, 0, 0)),
                  pl.BlockSpec((nc, LANES), lambda bi, gi, qi: (0, 0))],
        out_specs=[pl.BlockSpec((1, NSA_REP, CMP_TQ, HEAD_DIM), lambda bi, gi, qi: (bi, gi, qi, 0)),
                   pl.BlockSpec((1, 1, CMP_TQ, LANES), lambda bi, gi, qi: (bi, gi, qi, 0))],
        out_shape=[jax.ShapeDtypeStruct((b, NSA_HEADS, t, HEAD_DIM), f32),
                   jax.ShapeDtypeStruct((b, g, t, LANES), bf16)],
        scratch_shapes=[pltpu.VMEM((CMP_TQ, LANES), f32)],
        compiler_params=_cparams(("parallel", "parallel", "parallel")),
        name="cmp_attn",
    )(q_r, kvc_t, kvc, cover)


def _nsa_attn_kernel(q_ref, kx_ref, vs_ref, kw_ref, vw_ref, sel_ref, oc_ref, gate_ref,
                     o_ref, qx_ref, s_ref, m_ref, acc_ref, yw_ref):
    gi = pl.program_id(1)
    qi = pl.program_id(2)
    start = qi * ATT_TQ
    tq = start + lax.broadcasted_iota(i32, (ATT_TQ, 1), 0)

    selb = sel_ref[0, 0]
    for r in range(NSA_REP):
        qx_ref[r * ATT_TQ:(r + 1) * ATT_TQ, 0:LANES] = selb
        qr = q_ref[0, r]
        qx_ref[r * ATT_TQ:(r + 1) * ATT_TQ, LANES:2 * LANES] = jnp.concatenate([qr, jnp.zeros_like(qr)], axis=1)
    rows = NSA_REP * ATT_TQ

    m_ref[...] = jnp.full(m_ref.shape, MASK_NEG, f32)
    acc_ref[...] = jnp.zeros(acc_ref.shape, f32)

    def consume(kt, v_ref, bias):
        k0 = pl.multiple_of(kt * ATT_TK, ATT_TK)
        v = v_ref[0, 0, pl.ds(k0, ATT_TK), :]
        s = s_ref[...]
        if bias is not None:
            s = (s.reshape(NSA_REP, ATT_TQ, ATT_TK) + bias[None]).reshape(rows, ATT_TK)
        m_prev = m_ref[...]
        m_new = jnp.maximum(m_prev, jnp.max(s, axis=-1, keepdims=True))
        alpha = jnp.exp(m_prev - m_new)
        p = jnp.exp(s - jnp.concatenate([m_new] * (ATT_TK // LANES), axis=1))
        acc_ref[...] = alpha * acc_ref[...] + _dot(p.astype(bf16), v)
        m_ref[...] = m_new

    def kpos(kt):
        return kt * ATT_TK + lax.broadcasted_iota(i32, (1, ATT_TK), 1)

    last = start // ATT_TK
    s_ref[...] = _dot(qx_ref[...], kx_ref[0, 0, 0])

    def sel_step(kt, _):
        consume(kt, vs_ref, None)
        s_ref[...] = _dot(qx_ref[...], kx_ref[0, 0, kt + 1])
        return 0

    n_win = (WINDOW + ATT_TQ) // WIN_TK
    w_first = jnp.maximum(start // WIN_TK - WINDOW // WIN_TK, 0)
    kw = jnp.concatenate([kw_ref[0, 0, w_first + i] for i in range(n_win)], axis=1)
    w0 = pl.multiple_of(w_first * WIN_TK, WIN_TK)
    vw = vw_ref[0, 0, pl.ds(w0, n_win * WIN_TK), :]
    kp = w0 + lax.broadcasted_iota(i32, (1, n_win * WIN_TK), 1)
    wbias = jnp.where(kp <= tq, jnp.where(kp > tq - WINDOW, 0.0, MASK_NEG), MASK_NEG)
    hh = WIN_HEADS
    gates = jax.nn.sigmoid(gate_ref[...])
    per_group = 3 * NSA_REP
    shifted = gates
    for gg in range(1, NSA_KV_GROUPS):
        shifted = jnp.where(gi == gg, pltpu.roll(gates, LANES - gg * per_group, 1), shifted)

    def gate(r, br):
        c = r * 3 + br
        return jnp.broadcast_to(shifted[:, c:c + 1], (ATT_TQ, HEAD_DIM))

    for half in range(NSA_REP // hh):
        qh = q_ref[0, half * hh:(half + 1) * hh].reshape(hh * ATT_TQ, HEAD_DIM)
        sw = _dot(qh, kw).reshape(hh, ATT_TQ, n_win * WIN_TK) + wbias[None]
        sw = sw.reshape(hh * ATT_TQ, n_win * WIN_TK)
        pw = jnp.exp(sw - jnp.max(sw, axis=-1, keepdims=True))
        aw = _dot(pw.astype(bf16), vw)
        for i in range(hh):
            r = half * hh + i
            a = aw[i * ATT_TQ:(i + 1) * ATT_TQ]
            yw_ref[r] = a[:, :HEAD_DIM] * (gate(r, 2) / a[:, HEAD_DIM:HEAD_DIM + 1])

    def sel_pair(j, _):
        sel_step(2 * j, 0)
        sel_step(2 * j + 1, 0)
        return 0

    lax.fori_loop(0, last // 2, sel_pair, 0)

    @pl.when(last % 2 == 1)
    def _():
        sel_step(last - 1, 0)

    consume(last, vs_ref, jnp.where(kpos(last) <= tq, 0.0, MASK_NEG))

    for r in range(NSA_REP):
        asel = acc_ref[r * ATT_TQ:(r + 1) * ATT_TQ]
        osel = asel[:, :HEAD_DIM] / asel[:, HEAD_DIM:HEAD_DIM + 1]
        o_ref[:, r * HEAD_DIM:(r + 1) * HEAD_DIM] = gate(r, 0) * oc_ref[0, r] + gate(r, 1) * osel + yw_ref[r]


def _nsa_attn(q_r, kx, vs, kw, vw, selb, oc, proj, b, t):
    g = NSA_KV_GROUPS
    nq = t // ATT_TQ
    once = dict(pipeline_mode=pl.Buffered(1))
    vspec = pl.BlockSpec((1, 1, t, LANES), lambda bi, gi, qi: (bi, gi, 0, 0), **once)
    hspec = pl.BlockSpec((1, NSA_REP, ATT_TQ, HEAD_DIM), lambda bi, gi, qi: (bi, gi, qi, 0))
    return pl.pallas_call(
        _nsa_attn_kernel,
        grid=(b, g, nq),
        in_specs=[hspec,
                  pl.BlockSpec((1, 1, t // ATT_TK, KX_WIDTH, ATT_TK), lambda bi, gi, qi: (bi, gi, 0, 0, 0), **once),
                  vspec,
                  pl.BlockSpec((1, 1, t // WIN_TK, HEAD_DIM, WIN_TK), lambda bi, gi, qi: (bi, gi, 0, 0, 0), **once),
                  vspec,
                  pl.BlockSpec((1, 1, ATT_TQ, LANES), lambda bi, gi, qi: (bi, gi, qi, 0)),
                  hspec,
                  pl.BlockSpec((ATT_TQ, LANES), lambda bi, gi, qi: (bi * nq + qi, COL_SMALL // LANES))],
        out_specs=pl.BlockSpec((ATT_TQ, NSA_REP * HEAD_DIM), lambda bi, gi, qi: (bi * nq + qi, gi)),
        out_shape=jax.ShapeDtypeStruct((b * t, NSA_WIDTH), f32),
        scratch_shapes=[pltpu.VMEM((NSA_REP * ATT_TQ, KX_WIDTH), bf16),
                        pltpu.VMEM((NSA_REP * ATT_TQ, ATT_TK), f32),
                        pltpu.VMEM((NSA_REP * ATT_TQ, LANES), f32),
                        pltpu.VMEM((NSA_REP * ATT_TQ, LANES), f32),
                        pltpu.VMEM((NSA_REP, ATT_TQ, HEAD_DIM), f32)],
        compiler_params=_cparams(("parallel", "parallel", "arbitrary")),
        name="nsa_attn",
    )(q_r, kx, vs, kw, vw, selb, oc, proj)


HALO = 8


def _ssd_kernel(xbc_ref, z_ref, small_ref, dtt_ref, cw_ref, cb_ref, dtb_r_ref, dtb_c_ref,
                alog_r_ref, alog_c_ref, dskip_ref, nw_ref, o_ref, ext_ref, st_ref):
    c = pl.program_id(1)
    L = CHUNK

    @pl.when(c == 0)
    def _():
        ext_ref[0:HALO, :] = jnp.zeros((HALO, XBC_WIDTH), f32)
        st_ref[...] = jnp.zeros_like(st_ref)

    ext_ref[HALO:HALO + L, :] = xbc_ref[...]
    conv = cb_ref[...]
    for k in range(CONV_WIDTH):
        off = HALO - (CONV_WIDTH - 1) + k
        conv = conv + cw_ref[k:k + 1, :] * ext_ref[off:off + L, :]
    ext_ref[0:HALO, :] = ext_ref[L:L + HALO, :]
    act = conv * jax.nn.sigmoid(conv)
    xs = act[:, :SSM_WIDTH]
    bm = act[:, SSM_WIDTH:SSM_WIDTH + SSM_GROUPS * SSM_STATE]
    cm = act[:, SSM_WIDTH + SSM_GROUPS * SSM_STATE:]

    dt_c = jax.nn.softplus(small_ref[:, SMALL_DT_OFF:SMALL_DT_OFF + SSM_HEADS] + dtb_r_ref[...])
    dt_r = jax.nn.softplus(dtt_ref[0] + dtb_c_ref[...])
    a_r = -jnp.exp(alog_r_ref[...])
    a_c = -jnp.exp(alog_c_ref[...])
    row = lax.broadcasted_iota(i32, (L, L), 0)
    col = lax.broadcasted_iota(i32, (L, L), 1)
    causal = col <= row
    tri = jnp.where(causal, 1.0, 0.0)
    acs_c = _dot_hi(tri, dt_c * a_r)
    acs_r = _dot_hi(dt_r * a_c, jnp.where(row <= col, 1.0, 0.0))

    z = z_ref[...]
    ys = []
    for g in range(SSM_GROUPS):
        cg = cm[:, g * SSM_STATE:(g + 1) * SSM_STATE].astype(bf16)
        bg = bm[:, g * SSM_STATE:(g + 1) * SSM_STATE]
        cb = _dot_t(cg, bg.astype(bf16))
        bg_t = bg.T
        ssq = jnp.zeros((L, 1), f32)
        yg = []
        for r in range(SSM_REP):
            h = g * SSM_REP + r
            hs = slice(h * SSM_HEADDIM, (h + 1) * SSM_HEADDIM)
            a_col = acs_c[:, h:h + 1]
            a_row = acs_r[h:h + 1, :]
            dt_row = dt_r[h:h + 1, :]
            a_last = acs_r[h:h + 1, L - 1:L]
            seg = a_col - a_row
            decay = jnp.where(causal, jnp.exp(jnp.where(causal, seg, 0.0)), 0.0)
            w = cb * decay * dt_row
            x_h = xs[:, hs]
            xb = x_h.astype(bf16)
            st = st_ref[h]
            y = _dot(w.astype(bf16), xb)
            y = y + _dot(cg, st.astype(bf16)) * jnp.exp(a_col)
            y = y + dskip_ref[:, hs] * x_h
            bscaled = bg_t * (jnp.exp(a_last - a_row) * dt_row)
            st_ref[h] = jnp.exp(a_last) * st + _dot(bscaled.astype(bf16), xb)
            zh = z[:, hs]
            y = y * (zh * jax.nn.sigmoid(zh))
            ssq = ssq + jnp.sum(y * y, axis=-1, keepdims=True)
            yg.append(y)
        rs = lax.rsqrt(ssq / (SSM_REP * SSM_HEADDIM) + NORM_EPS)
        for r in range(SSM_REP):
            h = g * SSM_REP + r
            hs = slice(h * SSM_HEADDIM, (h + 1) * SSM_HEADDIM)
            o_ref[:, hs] = yg[r] * rs * nw_ref[:, hs]


def _ssd(proj, dtt, cw, cb, dtb_r, dtb_c, alog_r, alog_c, dskip, nw, b, t):
    nch = t // CHUNK
    row = lambda bi, ci: bi * nch + ci
    const2 = lambda bi, ci: (0, 0)
    return pl.pallas_call(
        _ssd_kernel,
        grid=(b, nch),
        in_specs=[pl.BlockSpec((CHUNK, XBC_WIDTH), lambda bi, ci: (row(bi, ci), COL_XBC // XBC_WIDTH)),
                  pl.BlockSpec((CHUNK, SSM_WIDTH), lambda bi, ci: (row(bi, ci), COL_Z // SSM_WIDTH)),
                  pl.BlockSpec((CHUNK, LANES), lambda bi, ci: (row(bi, ci), COL_SMALL // LANES)),
                  pl.BlockSpec((1, SSM_HEADS, CHUNK), lambda bi, ci: (bi, 0, ci)),
                  pl.BlockSpec((CONV_WIDTH, XBC_WIDTH), const2),
                  pl.BlockSpec((1, XBC_WIDTH), const2),
                  pl.BlockSpec((1, SSM_HEADS), const2),
                  pl.BlockSpec((SSM_HEADS, 1), const2),
                  pl.BlockSpec((1, SSM_HEADS), const2),
                  pl.BlockSpec((SSM_HEADS, 1), const2),
                  pl.BlockSpec((1, SSM_WIDTH), const2),
                  pl.BlockSpec((1, SSM_WIDTH), const2)],
        out_specs=pl.BlockSpec((CHUNK, SSM_WIDTH), lambda bi, ci: (row(bi, ci), 0)),
        out_shape=jax.ShapeDtypeStruct((b * t, SSM_WIDTH), f32),
        scratch_shapes=[pltpu.VMEM((HALO + CHUNK, XBC_WIDTH), f32),
                        pltpu.VMEM((SSM_HEADS, SSM_STATE, SSM_HEADDIM), f32)],
        compiler_params=_cparams(("parallel", "arbitrary")),
        name="ssd",
    )(proj, proj, proj, dtt, cw, cb, dtb_r, dtb_c, alog_r, alog_c, dskip, nw)


def _layer_norm(v, g, b):
    mu = jnp.mean(v, axis=-1, keepdims=True)
    d = v - mu
    var = jnp.mean(d * d, axis=-1, keepdims=True)
    return d * lax.rsqrt(var + NORM_EPS) * g + b


def _outproj_kernel(alpha, ya_ref, yb_ref, x_ref, wa_ref, wb_ref, g_ref, b_ref, wr_ref, br_ref,
                    h_ref, route_ref, cnt_ref):
    i = pl.program_id(0)
    tm = x_ref.shape[0]
    mix = _dot(ya_ref[...].astype(bf16), wa_ref[...]) + _dot(yb_ref[...].astype(bf16), wb_ref[...])
    h = _layer_norm(alpha * x_ref[...] + mix, g_ref[...], b_ref[...])
    h_ref[...] = h

    h_hi = h.astype(bf16)
    h_lo = (h - h_hi.astype(f32)).astype(bf16)
    t = _dot(h_hi, wr_ref[...])
    logits = t[:, :LANES] + t[:, LANES:] + _dot(h_lo, wr_ref[:, :LANES]) + br_ref[...]
    lane = lax.broadcasted_iota(i32, (tm, LANES), 1)
    ninf = -jnp.inf
    gmask = lane < N_EXPERT_GROUPS
    gl = jnp.where(gmask, logits, ninf)
    ge = jnp.where(gmask, jnp.exp(gl - jnp.max(gl, axis=-1, keepdims=True)), 0.0)
    pg = ge / jnp.sum(ge, axis=-1, keepdims=True)
    g_gate = jnp.max(pg, axis=-1, keepdims=True)
    g_sel = jnp.min(jnp.where(gmask & (pg == g_gate), lane, LANES), axis=-1, keepdims=True)
    lo = N_EXPERT_GROUPS + g_sel * EXPERTS_PER_GROUP
    emask = (lane >= lo) & (lane < lo + EXPERTS_PER_GROUP)
    el = jnp.where(emask, logits, ninf)
    ee = jnp.where(emask, jnp.exp(el - jnp.max(el, axis=-1, keepdims=True)), 0.0)
    pe = ee / jnp.sum(ee, axis=-1, keepdims=True)
    p0 = jnp.max(pe, axis=-1, keepdims=True)
    l0 = jnp.min(jnp.where(emask & (pe == p0), lane, LANES), axis=-1, keepdims=True)
    rest = jnp.where(emask & (lane != l0), pe, ninf)
    p1 = jnp.max(rest, axis=-1, keepdims=True)
    l1 = jnp.min(jnp.where(rest == p1, lane, LANES), axis=-1, keepdims=True)
    psum = p0 + p1
    w0 = g_gate * p0 / psum
    w1 = g_gate * p1 / psum
    e0 = l0 - N_EXPERT_GROUPS
    e1 = l1 - N_EXPERT_GROUPS

    @pl.when(i == 0)
    def _():
        cnt_ref[...] = jnp.zeros_like(cnt_ref)

    oh0 = lane == e0
    oh1 = lane == e1
    both = jnp.where(oh0, 1.0, 0.0) + jnp.where(oh1, 1.0, 0.0)
    r_i = lax.broadcasted_iota(i32, (tm, tm), 0)
    c_i = lax.broadcasted_iota(i32, (tm, tm), 1)
    strict = jnp.where(c_i < r_i, 1.0, 0.0).astype(bf16)
    before = _dot(strict, both.astype(bf16)) + cnt_ref[...]
    rank0 = jnp.sum(jnp.where(oh0, before, 0.0), axis=-1, keepdims=True)
    rank1 = jnp.sum(jnp.where(oh1, before, 0.0), axis=-1, keepdims=True)
    cnt_ref[...] = cnt_ref[...] + jnp.sum(both, axis=0, keepdims=True)

    out = jnp.where(lane == 0, e0.astype(f32), 0.0)
    out = jnp.where(lane == 1, e1.astype(f32), out)
    out = jnp.where(lane == 2, w0, out)
    out = jnp.where(lane == 3, w1, out)
    out = jnp.where(lane == 4, rank0, out)
    out = jnp.where(lane == 5, rank1, out)
    route_ref[...] = out


def _outproj(y_nsa, y_ssm, x2, wa, wb, g, bta, wr, br, alpha):
    n, d = x2.shape
    tm = OUT_TM
    const = lambda i: (0, 0)
    rowb = lambda i: (i, 0)
    return pl.pallas_call(
        functools.partial(_outproj_kernel, alpha),
        grid=(n // tm,),
        in_specs=[pl.BlockSpec((tm, NSA_WIDTH), rowb), pl.BlockSpec((tm, SSM_WIDTH), rowb),
                  pl.BlockSpec((tm, d), rowb),
                  pl.BlockSpec((NSA_WIDTH, d), const), pl.BlockSpec((SSM_WIDTH, d), const),
                  pl.BlockSpec((1, d), const), pl.BlockSpec((1, d), const),
                  pl.BlockSpec((d, 2 * LANES), const), pl.BlockSpec((1, LANES), const)],
        out_specs=[pl.BlockSpec((tm, d), rowb), pl.BlockSpec((tm, LANES), rowb),
                   pl.BlockSpec((1, LANES), const)],
        out_shape=[jax.ShapeDtypeStruct((n, d), f32), jax.ShapeDtypeStruct((n, LANES), f32),
                   jax.ShapeDtypeStruct((1, LANES), f32)],
        compiler_params=_cparams(("arbitrary",)),
        name="outproj",
    )(y_nsa, y_ssm, x2, wa, wb, g, bta, wr, br)


def _row_copy(src_ref, src_row, dst_ref, dst_row, sem):
    return pltpu.make_async_copy(src_ref.at[pl.ds(src_row, 1), :], dst_ref.at[pl.ds(dst_row, 1), :], sem)


def _dest_row(idx_ref, ps_ref, tm, k, r):
    return ps_ref[idx_ref[0, 0, k * tm + r]] + idx_ref[0, 0, (2 + k) * tm + r]


def _dispatch_kernel(idx_ref, ps_ref, zflag_ref, h_ref, xs_ref, zero_ref, tile_ref, sem, lsem, zsem):
    tm = DISP_TM
    nb = zflag_ref.shape[0]
    i = pl.program_id(0)

    @pl.when(i == 0)
    def _():
        zero_ref[...] = jnp.zeros_like(zero_ref)

        def zblock(i):
            rows = pl.ds(pl.multiple_of(i * MOE_TM, MOE_TM), MOE_TM)
            return pltpu.make_async_copy(zero_ref, xs_ref.at[rows, :], zsem)

        def zstart(i, _):
            @pl.when(zflag_ref[i] != 0)
            def _():
                zblock(i).start()
            return 0

        def zwait(i, _):
            @pl.when(zflag_ref[i] != 0)
            def _():
                zblock(i).wait()
            return 0

        lax.fori_loop(0, nb, zstart, 0)
        lax.fori_loop(0, nb, zwait, 0)

    nslot = tile_ref.shape[0]
    nsteps = pl.num_programs(0)
    slot = i % nslot

    def tile_load(step, s):
        rows = pl.ds(pl.multiple_of(step * tm, tm), tm)
        return pltpu.make_async_copy(h_ref.at[rows, :], tile_ref.at[s], lsem.at[s])

    @pl.when(i == 0)
    def _():
        tile_load(0, 0).start()

    tile_load(i, slot).wait()

    @pl.when(i + 1 < nsteps)
    def _():
        tile_load(i + 1, (i + 1) % nslot).start()

    def issue(r, _):
        for k in range(2):
            _row_copy(tile_ref.at[slot], r, xs_ref, _dest_row(idx_ref, ps_ref, tm, k, r), sem.at[slot]).start()
        return 0

    lax.fori_loop(0, tm, issue, 0, unroll=4)

    def wait_rows(s):
        for k in range(2):
            pltpu.make_async_copy(tile_ref.at[s], xs_ref.at[pl.ds(0, tm), :], sem.at[s]).wait()

    @pl.when(i > 0)
    def _():
        wait_rows((i - 1) % nslot)

    @pl.when(i == nsteps - 1)
    def _():
        wait_rows(slot)


def _dispatch(idx_t, pad_starts, zflag, h, p_rows):
    n, d = h.shape
    tm = DISP_TM
    return pl.pallas_call(
        _dispatch_kernel,
        grid=(n // tm,),
        in_specs=[pl.BlockSpec((1, 1, 4 * tm), lambda i: (i, 0, 0), memory_space=pltpu.SMEM),
                  pl.BlockSpec(memory_space=pltpu.SMEM),
                  pl.BlockSpec(memory_space=pltpu.SMEM),
                  pl.BlockSpec(memory_space=pl.ANY)],
        out_specs=pl.BlockSpec(memory_space=pl.ANY),
        out_shape=jax.ShapeDtypeStruct((p_rows, d), f32),
        scratch_shapes=[pltpu.VMEM((MOE_TM, d), f32), pltpu.VMEM((3, tm, d), f32),
                        pltpu.SemaphoreType.DMA((3,)), pltpu.SemaphoreType.DMA((3,)), pltpu.SemaphoreType.DMA(())],
        compiler_params=_cparams(("arbitrary",)),
        name="dispatch",
    )(idx_t, pad_starts, zflag, h)


def _experts_kernel(be_ref, nu_ref, first_ref, next_ref, slot_ref, xs_ref, wg_ref, wu_ref, wd_ref, y_ref,
                    fg_ref, fu_ref, fd_ref, wgb_ref, wub_ref, wdb_ref, sems):
    i = pl.program_id(0)

    def weight_loads(e, s):
        return (pltpu.make_async_copy(wg_ref.at[e], fg_ref.at[s], sems.at[s, 0]),
                pltpu.make_async_copy(wu_ref.at[e], fu_ref.at[s], sems.at[s, 1]),
                pltpu.make_async_copy(wd_ref.at[e], fd_ref.at[s], sems.at[s, 2]))

    @pl.when(i == 0)
    def _():
        for c in weight_loads(be_ref[0], 0):
            c.start()

    @pl.when(first_ref[i] != 0)
    def _():
        s = slot_ref[i]
        for c in weight_loads(be_ref[i], s):
            c.wait()
        wgb_ref[...] = fg_ref[s].astype(bf16)
        wub_ref[...] = fu_ref[s].astype(bf16)
        wdb_ref[...] = fd_ref[s].astype(bf16)

        @pl.when(next_ref[i] >= 0)
        def _():
            for c in weight_loads(next_ref[i], 1 - s):
                c.start()

    @pl.when(i < nu_ref[0])
    def _():
        xb = xs_ref[...].astype(bf16)
        gte = _dot(xb, wgb_ref[...])
        up = _dot(xb, wub_ref[...])
        act = gte * jax.nn.sigmoid(gte) * up
        y_ref[...] = _dot(act.astype(bf16), wdb_ref[...])

    @pl.when(i >= nu_ref[0])
    def _():
        y_ref[...] = jnp.zeros_like(y_ref)


def _experts(block_e, n_used, first, next_e, slot, xs, w_gate, w_up, w_down):
    p_rows, d = xs.shape
    tm = MOE_TM
    nb = p_rows // tm
    de = w_gate.shape[-1]
    xmap = lambda i, be, nu, fi, ne, sl: (jnp.maximum(jnp.minimum(i, nu[0] - 1), 0), 0)
    hbm = pl.BlockSpec(memory_space=pl.ANY)
    return pl.pallas_call(
        _experts_kernel,
        grid_spec=pltpu.PrefetchScalarGridSpec(
            num_scalar_prefetch=5,
            grid=(nb,),
            in_specs=[pl.BlockSpec((tm, d), xmap), hbm, hbm, hbm],
            out_specs=pl.BlockSpec((tm, d), lambda i, be, nu, fi, ne, sl: (i, 0)),
            scratch_shapes=[pltpu.VMEM((2, d, de), f32), pltpu.VMEM((2, d, de), f32), pltpu.VMEM((2, de, d), f32),
                            pltpu.VMEM((d, de), bf16), pltpu.VMEM((d, de), bf16), pltpu.VMEM((de, d), bf16),
                            pltpu.SemaphoreType.DMA((2, 3))],
        ),
        out_shape=jax.ShapeDtypeStruct((p_rows, d), f32),
        compiler_params=_cparams(("arbitrary",)),
        name="experts",
    )(block_e, n_used, first, next_e, slot, xs, w_gate, w_up, w_down)


def _combine_kernel(alpha, idx_ref, nidx_ref, ps_ref, route_ref, h_ref, g_ref, b_ref, y_ref, o_ref, buf_ref, sems):
    tm = h_ref.shape[0]
    i = pl.program_id(0)
    slot = i % 2

    def gather(ids_ref, s):
        def issue(r, _):
            for k in range(2):
                _row_copy(y_ref, _dest_row(ids_ref, ps_ref, tm, k, r), buf_ref.at[s, k], r, sems.at[s]).start()
            return 0
        lax.fori_loop(0, tm, issue, 0, unroll=4)

    @pl.when(i == 0)
    def _():
        gather(idx_ref, 0)

    @pl.when(i + 1 < pl.num_programs(0))
    def _():
        gather(nidx_ref, 1 - slot)

    for k in range(2):
        pltpu.make_async_copy(y_ref.at[pl.ds(0, tm), :], buf_ref.at[slot, k], sems.at[slot]).wait()

    route = route_ref[...]
    ffn = route[:, 2:3] * buf_ref[slot, 0] + route[:, 3:4] * buf_ref[slot, 1]
    o_ref[...] = _layer_norm(alpha * h_ref[...] + ffn, g_ref[...], b_ref[...])


def _combine(idx_t, pad_starts, route, h, g, bta, y, alpha):
    n, d = h.shape
    tm = COMB_TM
    const = lambda i: (0, 0)
    return pl.pallas_call(
        functools.partial(_combine_kernel, alpha),
        grid=(n // tm,),
        in_specs=[pl.BlockSpec((1, 1, 4 * tm), lambda i: (i, 0, 0), memory_space=pltpu.SMEM),
                  pl.BlockSpec((1, 1, 4 * tm), lambda i: (jnp.minimum(i + 1, n // tm - 1), 0, 0),
                               memory_space=pltpu.SMEM),
                  pl.BlockSpec(memory_space=pltpu.SMEM),
                  pl.BlockSpec((tm, LANES), lambda i: (i, 0)),
                  pl.BlockSpec((tm, d), lambda i: (i, 0)),
                  pl.BlockSpec((1, d), const), pl.BlockSpec((1, d), const),
                  pl.BlockSpec(memory_space=pl.ANY)],
        out_specs=pl.BlockSpec((tm, d), lambda i: (i, 0)),
        out_shape=jax.ShapeDtypeStruct((n, d), f32),
        scratch_shapes=[pltpu.VMEM((2, 2, tm, d), f32), pltpu.SemaphoreType.DMA((2,))],
        compiler_params=_cparams(("arbitrary",)),
        name="combine",
    )(idx_t, idx_t, pad_starts, route, h, g, bta, y)


def _tile_idx(idx, tm):
    n = idx.shape[1]
    return idx.reshape(4, n // tm, tm).transpose(1, 0, 2).reshape(n // tm, 1, 4 * tm)


def _layer(x, positions, w_in, cmp_k_pe, cmp_k_w1, cmp_k_b1, cmp_k_w2, cmp_v_pe, cmp_v_w1, cmp_v_b1, cmp_v_w2,
           conv_w, conv_b, dt_bias, a_log, d_skip, ssm_norm_w, w_out, ln1_g, ln1_b,
           w_router_group, b_router_group, w_router_expert, b_router_expert, w_gate, w_up, w_down, ln2_g, ln2_b,
           alpha):
    b, t, d = x.shape
    n = b * t
    x2 = x.reshape(n, d)

    c0 = NSA_WIDTH
    c1 = c0 + 6 * KV_WIDTH
    c2 = c1 + 3 * NSA_HEADS
    c3 = c2 + SSM_WIDTH
    c4 = c3 + XBC_WIDTH
    w_small = jnp.concatenate([w_in[:, c1:c2], w_in[:, c4:], jnp.zeros((d, LANES - 3 * NSA_HEADS - SSM_HEADS), f32)], axis=1)
    w_cat = jnp.concatenate([w_in[:, :c0], w_in[:, c2:c3], w_in[:, c3:c4], w_in[:, c0:c1], w_small,
                             jnp.zeros((d, PROJ_COLS - COL_SMALL - LANES), f32)], axis=1).astype(bf16)
    proj = _proj(x2, w_cat)

    lane = np.arange(LANES) % HEAD_DIM
    inv_freq = ROPE_THETA ** (-jnp.arange(0, ROT_DIM, 2, dtype=f32) / ROT_DIM)
    invf = jnp.where(lane < ROT_DIM, inv_freq[lane % (ROT_DIM // 2)], 0.0).astype(f32)[None, :]
    pos128 = jnp.broadcast_to(positions.reshape(n, 1), (n, LANES))
    q_r, k_cmp, v_cmp, k_sel, v_sel, k_win, v_win = _nsa_prep(proj, pos128, invf, b, t)

    nc = t // CMP_STRIDE
    half_w = CMP_STRIDE * HEAD_DIM
    a = jnp.stack([k_cmp, v_cmp]).reshape(2, b * NSA_KV_GROUPS, nc, half_w)
    pe = jnp.stack([cmp_k_pe, cmp_v_pe]).reshape(2, 2, 1, half_w)
    w1 = jnp.stack([cmp_k_w1, cmp_v_w1]).reshape(2, 2, half_w, CMP_HIDDEN).astype(bf16)
    b1 = jnp.stack([cmp_k_b1, cmp_v_b1]).reshape(2, 1, CMP_HIDDEN)
    w2 = jnp.pad(jnp.stack([cmp_k_w2, cmp_v_w2]), ((0, 0), (0, 0), (0, LANES - HEAD_DIM))).astype(bf16)
    cend = jnp.minimum(jnp.arange(nc) * CMP_STRIDE + CMP_BLOCK - 1, t - 1)
    posc = jnp.broadcast_to(positions[:, cend][:, :, None], (b, nc, LANES))
    kvc, kvc_t = _cmp_mlp(a, pe, w1, b1, w2, posc, invf, b)

    c_start = np.arange(nc)[:, None] * CMP_STRIDE
    s_start = np.arange(LANES)[None, :] * SEL_BLOCK
    cover = ((c_start < s_start + SEL_BLOCK) & (c_start + CMP_BLOCK > s_start)
             & (np.arange(nc)[:, None] < nc - 1) & (np.arange(LANES)[None, :] < t // SEL_BLOCK))
    cover = jnp.asarray(cover, bf16)
    o_cmp, selb = _cmp_attn(q_r, kvc, kvc_t, cover, b, t)

    y_nsa = _nsa_attn(q_r, k_sel, v_sel, k_win, v_win, selb, o_cmp, proj, b, t)

    dt_raw = proj[:, COL_SMALL + SMALL_DT_OFF:COL_SMALL + SMALL_DT_OFF + SSM_HEADS]
    dtt = dt_raw.reshape(b, t, SSM_HEADS).transpose(0, 2, 1)
    y_ssm = _ssd(proj, dtt, conv_w.reshape(CONV_WIDTH, XBC_WIDTH), conv_b.reshape(1, XBC_WIDTH),
                 dt_bias.reshape(1, SSM_HEADS), dt_bias.reshape(SSM_HEADS, 1),
                 a_log.reshape(1, SSM_HEADS), a_log.reshape(SSM_HEADS, 1),
                 jnp.repeat(d_skip, SSM_HEADDIM).reshape(1, SSM_WIDTH), ssm_norm_w.reshape(1, SSM_WIDTH), b, t)

    wr = jnp.concatenate([w_router_group, w_router_expert,
                          jnp.zeros((d, LANES - N_EXPERT_GROUPS - N_EXPERTS), f32)], axis=1)
    br = jnp.concatenate([b_router_group, b_router_expert,
                          jnp.zeros((LANES - N_EXPERT_GROUPS - N_EXPERTS,), f32)])[None, :]
    wr_hi = wr.astype(bf16)
    wr = jnp.concatenate([wr_hi, (wr - wr_hi.astype(f32)).astype(bf16)], axis=1)
    wo = w_out.astype(bf16)
    h, route, counts = _outproj(y_nsa, y_ssm, x2, wo[:NSA_WIDTH], wo[NSA_WIDTH:], ln1_g[None, :], ln1_b[None, :],
                                wr, br, alpha)

    cnt = counts[0, :N_EXPERTS].astype(i32)
    padded = (cnt + MOE_TM - 1) // MOE_TM * MOE_TM
    pad_ends = jnp.cumsum(padded)
    pad_starts = pad_ends - padded
    idx = jnp.concatenate([route[:, 0:2], route[:, 4:6]], axis=1).astype(i32).T
    p_rows = 2 * n + N_EXPERTS * MOE_TM
    nb = p_rows // MOE_TM
    block_e = jnp.minimum(jnp.sum(jnp.arange(nb, dtype=i32)[:, None] * MOE_TM >= pad_ends[None, :], axis=-1),
                          N_EXPERTS - 1).astype(i32)
    n_used = (pad_ends[-1] // MOE_TM).astype(i32).reshape(1)
    blk = jnp.arange(nb, dtype=i32)
    last_of_expert = jnp.any((blk[:, None] + 1) * MOE_TM == pad_ends[None, :], axis=-1)
    zflag = (last_of_expert | (blk >= n_used[0])).astype(i32)
    block_e = jnp.where(blk < n_used[0], block_e, block_e[jnp.maximum(n_used[0] - 1, 0)])

    xs = _dispatch(_tile_idx(idx, DISP_TM), pad_starts, zflag, h, p_rows)
    prev_e = jnp.concatenate([jnp.full((1,), -1, i32), block_e[:-1]])
    first = ((block_e != prev_e) & (blk < n_used[0])).astype(i32)
    eidx = jnp.arange(N_EXPERTS, dtype=i32)
    later = jnp.where((eidx[None, :] > eidx[:, None]) & (padded[None, :] > 0), eidx[None, :], N_EXPERTS)
    next_of = jnp.min(later, axis=1)
    next_e = jnp.where(next_of[block_e] < N_EXPERTS, next_of[block_e], -1).astype(i32)
    slot = ((jnp.cumsum(first) - 1) % 2).astype(i32)
    y = _experts(block_e, n_used, first, next_e, slot, xs, w_gate, w_up, w_down)
    out = _combine(_tile_idx(idx, COMB_TM), pad_starts, route, h, ln2_g[None, :], ln2_b[None, :], y, alpha)
    return out.reshape(b, t, d)


def kernel(x, positions, w_in, cmp_k_pe, cmp_k_w1, cmp_k_b1, cmp_k_w2, cmp_v_pe, cmp_v_w1, cmp_v_b1, cmp_v_w2, conv_w, conv_b, dt_bias, a_log, d_skip, ssm_norm_w, w_out, ln1_g, ln1_b, w_router_group, b_router_group, w_router_expert, b_router_expert, w_gate, w_up, w_down, ln2_g, ln2_b):
    depth = w_in.shape[0]
    alpha = (2 * depth) ** 0.25
    params = (w_in, cmp_k_pe, cmp_k_w1, cmp_k_b1, cmp_k_w2, cmp_v_pe, cmp_v_w1, cmp_v_b1, cmp_v_w2, conv_w, conv_b,
              dt_bias, a_log, d_skip, ssm_norm_w, w_out, ln1_g, ln1_b, w_router_group, b_router_group,
              w_router_expert, b_router_expert, w_gate, w_up, w_down, ln2_g, ln2_b)
    for l in range(depth):
        x = _layer(x, positions, *[p[l] for p in params], alpha)
    return x
```

```python
import functools
import math

import jax
import jax.numpy as jnp
import numpy as np
from jax import lax
from jax.experimental import pallas as pl
from jax.experimental.pallas import tpu as pltpu

f32 = jnp.float32
bf16 = jnp.bfloat16
i32 = jnp.int32

HEAD_DIM = 64
NSA_HEADS = 16
NSA_KV_GROUPS = 2
NSA_REP = NSA_HEADS // NSA_KV_GROUPS
NSA_WIDTH = NSA_HEADS * HEAD_DIM
KV_WIDTH = NSA_KV_GROUPS * HEAD_DIM
CMP_BLOCK = 32
CMP_STRIDE = 16
CMP_HIDDEN = 256
SEL_BLOCK = 64
SEL_TOPK = 16
WINDOW = 512
FORCED_SCORE = 1.0e4
SSM_HEADDIM = 64
SSM_HEADS = 16
SSM_WIDTH = SSM_HEADS * SSM_HEADDIM
SSM_GROUPS = 4
SSM_REP = SSM_HEADS // SSM_GROUPS
SSM_STATE = 128
CONV_WIDTH = 4
CHUNK = 256
XBC_WIDTH = SSM_WIDTH + 2 * SSM_GROUPS * SSM_STATE
ROPE_THETA = 500000.0
ROT_DIM = HEAD_DIM // 4
N_EXPERT_GROUPS = 4
EXPERTS_PER_GROUP = 8
N_EXPERTS = N_EXPERT_GROUPS * EXPERTS_PER_GROUP
D_EXPERT = 512
NORM_EPS = 1e-5

LANES = 128
MASK_NEG = -1.0e30

PROJ_TM = 1024
PROJ_TN = 1024
PREP_TM = 512
CMP_TQ = 256
ATT_TQ = 256
ATT_TK = 512
WIN_TK = 256
WIN_HEADS = 4
KX_WIDTH = 2 * LANES
OUT_TM = 512
MOE_TM = 256
DISP_TM = 256
COMB_TM = 128
VMEM_LIMIT = 56 * 1024 * 1024

COL_Q = 0
COL_Z = NSA_WIDTH
COL_XBC = COL_Z + SSM_WIDTH
COL_KV = COL_XBC + XBC_WIDTH
COL_SMALL = COL_KV + 6 * KV_WIDTH
PROJ_COLS = 5120
SMALL_DT_OFF = 3 * NSA_HEADS


def _cparams(sem, vmem=VMEM_LIMIT):
    return pltpu.CompilerParams(dimension_semantics=sem, vmem_limit_bytes=vmem)


def _dot(a, b):
    return jnp.dot(a, b, preferred_element_type=f32)


def _dot_t(a, b):
    return lax.dot_general(a, b, (((1,), (1,)), ((), ())), preferred_element_type=f32)


def _dot_hi(a, b):
    return jnp.dot(a, b, preferred_element_type=f32, precision=lax.Precision.HIGHEST)


def _proj_kernel(x_ref, w_ref, o_ref, xb_ref):
    @pl.when(pl.program_id(1) == 0)
    def _():
        xb_ref[...] = x_ref[...].astype(bf16)

    o_ref[...] = _dot(xb_ref[...], w_ref[...])


def _proj(x2, w_cat):
    n, d = x2.shape
    cols = w_cat.shape[1]
    tm = min(PROJ_TM, n)
    return pl.pallas_call(
        _proj_kernel,
        grid=(n // tm, cols // PROJ_TN),
        in_specs=[pl.BlockSpec((tm, d), lambda i, j: (i, 0)),
                  pl.BlockSpec((d, PROJ_TN), lambda i, j: (0, j))],
        out_specs=pl.BlockSpec((tm, PROJ_TN), lambda i, j: (i, j)),
        out_shape=jax.ShapeDtypeStruct((n, cols), f32),
        scratch_shapes=[pltpu.VMEM((tm, d), bf16)],
        compiler_params=_cparams(("parallel", "arbitrary")),
        name="proj",
    )(x2, w_cat)


def _rope_tables(pos_i32, invf):
    ang = pos_i32.astype(f32) * invf
    return jnp.cos(ang), jnp.sin(ang)


def _rope128(x, cos, sin):
    half = ROT_DIM // 2
    d = lax.broadcasted_iota(i32, x.shape, 1) % HEAD_DIM
    up = pltpu.roll(x, LANES - half, 1)
    dn = pltpu.roll(x, half, 1)
    rot = jnp.where(d < half, -up, dn)
    return x * cos + rot * sin


def _nsa_prep_kernel(pos_ref, invf_ref, q_ref, kc_ref, vc_ref, ks_ref, vs_ref, kw_ref, vw_ref,
                     qo_ref, kco_ref, vco_ref, kso_ref, vso_ref, kwo_ref, vwo_ref):
    cos, sin = _rope_tables(pos_ref[...], invf_ref[...])
    scale = HEAD_DIM ** -0.5
    for c in range(NSA_WIDTH // LANES):
        t = _rope128(q_ref[:, c * LANES:(c + 1) * LANES], cos, sin) * scale
        qo_ref[0, 2 * c] = t[:, :HEAD_DIM].astype(bf16)
        qo_ref[0, 2 * c + 1] = t[:, HEAD_DIM:].astype(bf16)

    def split(src, dst, rope, dt):
        t = src[...]
        if rope:
            t = _rope128(t, cos, sin)
        for g in range(NSA_KV_GROUPS):
            dst[0, g] = t[:, g * HEAD_DIM:(g + 1) * HEAD_DIM].astype(dt)

    split(kc_ref, kco_ref, False, f32)
    split(vc_ref, vco_ref, False, f32)

    tm = pos_ref.shape[0]

    def values_with_ones(src, dst):
        t = src[...]
        tail = jnp.ones((tm, HEAD_DIM), bf16)
        for g in range(NSA_KV_GROUPS):
            vg = t[:, g * HEAD_DIM:(g + 1) * HEAD_DIM].astype(bf16)
            dst[0, g] = jnp.concatenate([vg, tail], axis=1)

    values_with_ones(vs_ref, vso_ref)
    values_with_ones(vw_ref, vwo_ref)

    kw_t = _rope128(kw_ref[...], cos, sin).T
    ks_t = _rope128(ks_ref[...], cos, sin).T
    for g in range(NSA_KV_GROUPS):
        for c in range(tm // WIN_TK):
            kwo_ref[0, g, c] = kw_t[g * HEAD_DIM:(g + 1) * HEAD_DIM, c * WIN_TK:(c + 1) * WIN_TK].astype(bf16)
    blk = lax.broadcasted_iota(i32, (LANES, tm), 0)
    tok = pl.program_id(1) * tm + lax.broadcasted_iota(i32, (LANES, tm), 1)
    onehot_t = jnp.where(tok // SEL_BLOCK == blk, 1.0, 0.0).astype(bf16)
    for g in range(NSA_KV_GROUPS):
        kso_ref[0, g, 0, 0:LANES, :] = onehot_t
        kg = ks_t[g * HEAD_DIM:(g + 1) * HEAD_DIM, :].astype(bf16)
        kso_ref[0, g, 0, LANES:2 * LANES, :] = jnp.concatenate([kg, jnp.zeros_like(kg)], axis=0)


def _nsa_prep(proj, pos128, invf, b, t):
    tm = PREP_TM
    assert tm == ATT_TK
    nt = t // tm
    row = lambda bi, ti: (bi * nt + ti, 0)
    kv0 = COL_KV // LANES
    in_specs = [pl.BlockSpec((tm, LANES), row),
                pl.BlockSpec((1, LANES), lambda bi, ti: (0, 0)),
                pl.BlockSpec((tm, NSA_WIDTH), lambda bi, ti: (bi * nt + ti, COL_Q // NSA_WIDTH))]
    for k in range(6):
        in_specs.append(pl.BlockSpec((tm, LANES), functools.partial(lambda bi, ti, k: (bi * nt + ti, kv0 + k), k=k)))
    head = lambda bi, ti: (bi, 0, ti, 0)
    tile5 = lambda bi, ti: (bi, 0, ti, 0, 0)
    g = NSA_KV_GROUPS
    out_specs = [pl.BlockSpec((1, NSA_HEADS, tm, HEAD_DIM), head),
                 pl.BlockSpec((1, g, tm, HEAD_DIM), head), pl.BlockSpec((1, g, tm, HEAD_DIM), head),
                 pl.BlockSpec((1, g, 1, KX_WIDTH, tm), tile5), pl.BlockSpec((1, g, tm, LANES), head),
                 pl.BlockSpec((1, g, tm // WIN_TK, HEAD_DIM, WIN_TK), tile5), pl.BlockSpec((1, g, tm, LANES), head)]
    out_shape = [jax.ShapeDtypeStruct((b, NSA_HEADS, t, HEAD_DIM), bf16),
                 jax.ShapeDtypeStruct((b, g, t, HEAD_DIM), f32), jax.ShapeDtypeStruct((b, g, t, HEAD_DIM), f32),
                 jax.ShapeDtypeStruct((b, g, nt, KX_WIDTH, tm), bf16), jax.ShapeDtypeStruct((b, g, t, LANES), bf16),
                 jax.ShapeDtypeStruct((b, g, t // WIN_TK, HEAD_DIM, WIN_TK), bf16),
                 jax.ShapeDtypeStruct((b, g, t, LANES), bf16)]
    return pl.pallas_call(
        _nsa_prep_kernel,
        grid=(b, nt),
        in_specs=in_specs,
        out_specs=out_specs,
        out_shape=out_shape,
        compiler_params=_cparams(("parallel", "parallel")),
        name="nsa_prep",
    )(pos128, invf, proj, proj, proj, proj, proj, proj, proj)


def _cmp_mlp_kernel(a_ref, pe_ref, w1_ref, b1_ref, w2_ref, pos_ref, invf_ref, o_ref, ot_ref):
    kind = pl.program_id(0)
    a = a_ref[0, 0]
    nc = a.shape[0]
    u = _dot((a + pe_ref[0, 0]).astype(bf16), w1_ref[0, 0])
    v = _dot((a + pe_ref[0, 1]).astype(bf16), w1_ref[0, 1])
    v_next = pltpu.roll(v, nc - 1, 0)
    hid = jax.nn.gelu(u + v_next + b1_ref[0])
    out = _dot(hid.astype(bf16), w2_ref[0])
    cos, sin = _rope_tables(pos_ref[0], invf_ref[...])
    roped = _rope128(out, cos, sin)
    out = jnp.where(kind == 0, roped, out)
    o_ref[0, 0] = out[:, :HEAD_DIM].astype(bf16)
    ot_ref[0, 0] = out.T[:HEAD_DIM, :].astype(bf16)


def _cmp_mlp(a, pe, w1, b1, w2, posc, invf, b):
    _, bg, nc, hw = a.shape
    g = bg // b
    return pl.pallas_call(
        _cmp_mlp_kernel,
        grid=(2, bg),
        in_specs=[pl.BlockSpec((1, 1, nc, hw), lambda k, i: (k, i, 0, 0)),
                  pl.BlockSpec((1, 2, 1, hw), lambda k, i: (k, 0, 0, 0)),
                  pl.BlockSpec((1, 2, hw, CMP_HIDDEN), lambda k, i: (k, 0, 0, 0)),
                  pl.BlockSpec((1, 1, CMP_HIDDEN), lambda k, i: (k, 0, 0)),
                  pl.BlockSpec((1, CMP_HIDDEN, LANES), lambda k, i: (k, 0, 0)),
                  pl.BlockSpec((1, nc, LANES), lambda k, i: (i // g, 0, 0)),
                  pl.BlockSpec((1, LANES), lambda k, i: (0, 0))],
        out_specs=[pl.BlockSpec((1, 1, nc, HEAD_DIM), lambda k, i: (k, i, 0, 0)),
                   pl.BlockSpec((1, 1, HEAD_DIM, nc), lambda k, i: (k, i, 0, 0))],
        out_shape=[jax.ShapeDtypeStruct((2, bg, nc, HEAD_DIM), bf16),
                   jax.ShapeDtypeStruct((2, bg, HEAD_DIM, nc), bf16)],
        compiler_params=_cparams(("parallel", "parallel")),
        name="cmp_mlp",
    )(a, pe, w1, b1, w2, posc, invf)


def _cmp_attn_kernel(q_ref, kct_ref, vc_ref, cover_ref, oc_ref, sel_ref, imp_ref):
    qi = pl.program_id(2)
    nc = vc_ref.shape[2]
    rows = NSA_REP * CMP_TQ
    tq = qi * CMP_TQ + lax.broadcasted_iota(i32, (CMP_TQ, 1), 0)
    row_live = jnp.where(tq >= CMP_BLOCK - 1, 1.0, 0.0)
    tiny = jnp.finfo(f32).tiny

    def attend(ncols):
        kct = kct_ref[0, 0, :, :ncols]
        vc = vc_ref[0, 0, :ncols, :]
        cend = lax.broadcasted_iota(i32, (1, ncols), 1) * CMP_STRIDE + (CMP_BLOCK - 1)
        bias = jnp.where(cend <= tq, 0.0, MASK_NEG)
        s = _dot(q_ref[0].reshape(rows, HEAD_DIM), kct).reshape(NSA_REP, CMP_TQ, ncols) + bias[None]
        e = jnp.exp(s - jnp.max(s, axis=-1, keepdims=True))
        live = row_live[None]
        p = e * (live / jnp.maximum(live * jnp.sum(e, axis=-1, keepdims=True), tiny))
        oc_ref[0] = _dot(p.reshape(rows, ncols).astype(bf16), vc).reshape(NSA_REP, CMP_TQ, HEAD_DIM)
        psum = jnp.sum(p, axis=0)
        hi = psum.astype(bf16)
        lo = (psum - hi.astype(f32)).astype(bf16)
        cover = cover_ref[:ncols, :]
        imp_ref[...] = _dot(hi, cover) + _dot(lo, cover)

    need = (qi + 1) * (CMP_TQ // CMP_STRIDE)
    for ncols in range(LANES, nc + 1, LANES):
        @pl.when((need > ncols - LANES) & (need <= ncols))
        def _():
            attend(ncols)

    imp = imp_ref[...]
    j = lax.broadcasted_iota(i32, (CMP_TQ, LANES), 1)
    cur = tq // SEL_BLOCK
    forced = (j == 0) | (j == cur) | (j == cur - 1)
    valid = j * SEL_BLOCK <= tq
    imp = jnp.where(valid, jnp.where(forced, FORCED_SCORE, imp), -FORCED_SCORE)

    jt = lax.broadcasted_iota(i32, (LANES, CMP_TQ), 0)

    def pick(_, carry):
        work, sel = carry
        m = jnp.max(work, axis=0, keepdims=True)
        first = jnp.min(jnp.where(work == m, jt, LANES), axis=0, keepdims=True)
        hit = jt == first
        return jnp.where(hit, -jnp.inf, work), jnp.where(hit, 1.0, sel)

    _, sel_t = lax.fori_loop(0, SEL_TOPK, pick, (imp.T, jnp.zeros((LANES, CMP_TQ), f32)))
    sel_ref[0, 0] = jnp.where(valid, jnp.where(sel_t.T > 0.0, 0.0, MASK_NEG), MASK_NEG).astype(bf16)


def _cmp_attn(q_r, kvc, kvc_t, cover, b, t):
    g = NSA_KV_GROUPS
    nc = kvc.shape[2]
    nq = t // CMP_TQ
    return pl.pallas_call(
        _cmp_attn_kernel,
        grid=(b, g, nq),
        in_specs=[pl.BlockSpec((1, NSA_REP, CMP_TQ, HEAD_DIM), lambda bi, gi, qi: (bi, gi, qi, 0)),
                  pl.BlockSpec((1, 1, HEAD_DIM, nc), lambda bi, gi, qi: (0, bi * g + gi, 0, 0)),
                  pl.BlockSpec((1, 1, nc, HEAD_DIM), lambda bi, gi, qi: (1, bi * g + gi, 0, 0)),
                  pl.BlockSpec((nc, LANES), lambda bi, gi, qi: (0, 0))],
        out_specs=[pl.BlockSpec((1, NSA_REP, CMP_TQ, HEAD_DIM), lambda bi, gi, qi: (bi, gi, qi, 0)),
                   pl.BlockSpec((1, 1, CMP_TQ, LANES), lambda bi, gi, qi: (bi, gi, qi, 0))],
        out_shape=[jax.ShapeDtypeStruct((b, NSA_HEADS, t, HEAD_DIM), f32),
                   jax.ShapeDtypeStruct((b, g, t, LANES), bf16)],
        scratch_shapes=[pltpu.VMEM((CMP_TQ, LANES), f32)],
        compiler_params=_cparams(("parallel", "parallel", "parallel")),
        name="cmp_attn",
    )(q_r, kvc_t, kvc, cover)


def _nsa_attn_kernel(q_ref, kx_ref, vs_ref, kw_ref, vw_ref, sel_ref, oc_ref, gate_ref,
                     o_ref, qx_ref, s_ref, m_ref, acc_ref, yw_ref):
    gi = pl.program_id(1)
    qi = pl.program_id(2)
    start = qi * ATT_TQ
    tq = start + lax.broadcasted_iota(i32, (ATT_TQ, 1), 0)

    selb = sel_ref[0, 0]
    for r in range(NSA_REP):
        qx_ref[r * ATT_TQ:(r + 1) * ATT_TQ, 0:LANES] = selb
        qr = q_ref[0, r]
        qx_ref[r * ATT_TQ:(r + 1) * ATT_TQ, LANES:2 * LANES] = jnp.concatenate([qr, jnp.zeros_like(qr)], axis=1)
    rows = NSA_REP * ATT_TQ

    m_ref[...] = jnp.full(m_ref.shape, MASK_NEG, f32)
    acc_ref[...] = jnp.zeros(acc_ref.shape, f32)

    def consume(kt, v_ref, bias):
        k0 = pl.multiple_of(kt * ATT_TK, ATT_TK)
        v = v_ref[0, 0, pl.ds(k0, ATT_TK), :]
        s = s_ref[...]
        if bias is not None:
            s = (s.reshape(NSA_REP, ATT_TQ, ATT_TK) + bias[None]).reshape(rows, ATT_TK)
        m_prev = m_ref[...]
        m_new = jnp.maximum(m_prev, jnp.max(s, axis=-1, keepdims=True))
        alpha = jnp.exp(m_prev - m_new)
        p = jnp.exp(s - jnp.concatenate([m_new] * (ATT_TK // LANES), axis=1))
        acc_ref[...] = alpha * acc_ref[...] + _dot(p.astype(bf16), v)
        m_ref[...] = m_new

    def kpos(kt):
        return kt * ATT_TK + lax.broadcasted_iota(i32, (1, ATT_TK), 1)

    last = start // ATT_TK
    s_ref[...] = _dot(qx_ref[...], kx_ref[0, 0, 0])

    def sel_step(kt, _):
        consume(kt, vs_ref, None)
        s_ref[...] = _dot(qx_ref[...], kx_ref[0, 0, kt + 1])
        return 0

    n_win = (WINDOW + ATT_TQ) // WIN_TK
    w_first = jnp.maximum(start // WIN_TK - WINDOW // WIN_TK, 0)
    kw = jnp.concatenate([kw_ref[0, 0, w_first + i] for i in range(n_win)], axis=1)
    w0 = pl.multiple_of(w_first * WIN_TK, WIN_TK)
    vw = vw_ref[0, 0, pl.ds(w0, n_win * WIN_TK), :]
    kp = w0 + lax.broadcasted_iota(i32, (1, n_win * WIN_TK), 1)
    wbias = jnp.where(kp <= tq, jnp.where(kp > tq - WINDOW, 0.0, MASK_NEG), MASK_NEG)
    hh = WIN_HEADS
    gates = jax.nn.sigmoid(gate_ref[...])
    per_group = 3 * NSA_REP
    shifted = gates
    for gg in range(1, NSA_KV_GROUPS):
        shifted = jnp.where(gi == gg, pltpu.roll(gates, LANES - gg * per_group, 1), shifted)

    def gate(r, br):
        c = r * 3 + br
        return jnp.broadcast_to(shifted[:, c:c + 1], (ATT_TQ, HEAD_DIM))

    upper = pltpu.roll(shifted, HEAD_DIM, 1)[:, HEAD_DIM:]

    def gated_output(a, r, br):
        c = r * 3 + br
        ratio = upper / a[:, HEAD_DIM:]
        return a[:, :HEAD_DIM] * jnp.broadcast_to(ratio[:, c:c + 1], (ATT_TQ, HEAD_DIM))

    for half in range(NSA_REP // hh):
        qh = q_ref[0, half * hh:(half + 1) * hh].reshape(hh * ATT_TQ, HEAD_DIM)
        sw = _dot(qh, kw).reshape(hh, ATT_TQ, n_win * WIN_TK) + wbias[None]
        sw = sw.reshape(hh * ATT_TQ, n_win * WIN_TK)
        pw = jnp.exp(sw - jnp.max(sw, axis=-1, keepdims=True))
        aw = _dot(pw.astype(bf16), vw)
        for i in range(hh):
            r = half * hh + i
            a = aw[i * ATT_TQ:(i + 1) * ATT_TQ]
            yw_ref[r] = gated_output(a, r, 2)

    def sel_pair(j, _):
        sel_step(2 * j, 0)
        sel_step(2 * j + 1, 0)
        return 0

    lax.fori_loop(0, last // 2, sel_pair, 0)

    @pl.when(last % 2 == 1)
    def _():
        sel_step(last - 1, 0)

    consume(last, vs_ref, jnp.where(kpos(last) <= tq, 0.0, MASK_NEG))

    for r in range(NSA_REP):
        osel = gated_output(acc_ref[r * ATT_TQ:(r + 1) * ATT_TQ], r, 1)
        o_ref[:, r * HEAD_DIM:(r + 1) * HEAD_DIM] = gate(r, 0) * oc_ref[0, r] + osel + yw_ref[r]


def _nsa_attn(q_r, kx, vs, kw, vw, selb, oc, proj, b, t):
    g = NSA_KV_GROUPS
    nq = t // ATT_TQ
    once = dict(pipeline_mode=pl.Buffered(1))
    vspec = pl.BlockSpec((1, 1, t, LANES), lambda bi, gi, qi: (bi, gi, 0, 0), **once)
    hspec = pl.BlockSpec((1, NSA_REP, ATT_TQ, HEAD_DIM), lambda bi, gi, qi: (bi, gi, qi, 0))
    return pl.pallas_call(
        _nsa_attn_kernel,
        grid=(b, g, nq),
        in_specs=[hspec,
                  pl.BlockSpec((1, 1, t // ATT_TK, KX_WIDTH, ATT_TK), lambda bi, gi, qi: (bi, gi, 0, 0, 0), **once),
                  vspec,
                  pl.BlockSpec((1, 1, t // WIN_TK, HEAD_DIM, WIN_TK), lambda bi, gi, qi: (bi, gi, 0, 0, 0), **once),
                  vspec,
                  pl.BlockSpec((1, 1, ATT_TQ, LANES), lambda bi, gi, qi: (bi, gi, qi, 0)),
                  hspec,
                  pl.BlockSpec((ATT_TQ, LANES), lambda bi, gi, qi: (bi * nq + qi, COL_SMALL // LANES))],
        out_specs=pl.BlockSpec((ATT_TQ, NSA_REP * HEAD_DIM), lambda bi, gi, qi: (bi * nq + qi, gi)),
        out_shape=jax.ShapeDtypeStruct((b * t, NSA_WIDTH), f32),
        scratch_shapes=[pltpu.VMEM((NSA_REP * ATT_TQ, KX_WIDTH), bf16),
                        pltpu.VMEM((NSA_REP * ATT_TQ, ATT_TK), f32),
                        pltpu.VMEM((NSA_REP * ATT_TQ, LANES), f32),
                        pltpu.VMEM((NSA_REP * ATT_TQ, LANES), f32),
                        pltpu.VMEM((NSA_REP, ATT_TQ, HEAD_DIM), f32)],
        compiler_params=_cparams(("parallel", "parallel", "arbitrary")),
        name="nsa_attn",
    )(q_r, kx, vs, kw, vw, selb, oc, proj)


HALO = 8


def _ssd_kernel(xbc_ref, z_ref, small_ref, dtt_ref, cw_ref, cb_ref, dtb_r_ref, dtb_c_ref,
                alog_r_ref, alog_c_ref, dskip_ref, nw_ref, o_ref, ext_ref, st_ref):
    c = pl.program_id(1)
    L = CHUNK

    @pl.when(c == 0)
    def _():
        ext_ref[0:HALO, :] = jnp.zeros((HALO, XBC_WIDTH), f32)
        st_ref[...] = jnp.zeros_like(st_ref)

    ext_ref[HALO:HALO + L, :] = xbc_ref[...]
    conv = cb_ref[...]
    for k in range(CONV_WIDTH):
        off = HALO - (CONV_WIDTH - 1) + k
        conv = conv + cw_ref[k:k + 1, :] * ext_ref[off:off + L, :]
    ext_ref[0:HALO, :] = ext_ref[L:L + HALO, :]
    act = conv * jax.nn.sigmoid(conv)
    xs = act[:, :SSM_WIDTH]
    bm = act[:, SSM_WIDTH:SSM_WIDTH + SSM_GROUPS * SSM_STATE]
    cm = act[:, SSM_WIDTH + SSM_GROUPS * SSM_STATE:]

    dt_c = jax.nn.softplus(small_ref[:, SMALL_DT_OFF:SMALL_DT_OFF + SSM_HEADS] + dtb_r_ref[...])
    dt_r = jax.nn.softplus(dtt_ref[0] + dtb_c_ref[...])
    a_r = -jnp.exp(alog_r_ref[...])
    a_c = -jnp.exp(alog_c_ref[...])
    row = lax.broadcasted_iota(i32, (L, L), 0)
    col = lax.broadcasted_iota(i32, (L, L), 1)
    causal = col <= row
    tri = jnp.where(causal, 1.0, 0.0)
    acs_c = _dot_hi(tri, dt_c * a_r)
    acs_r = _dot_hi(dt_r * a_c, jnp.where(row <= col, 1.0, 0.0))

    z = z_ref[...]
    ys = []
    for g in range(SSM_GROUPS):
        cg = cm[:, g * SSM_STATE:(g + 1) * SSM_STATE].astype(bf16)
        bg = bm[:, g * SSM_STATE:(g + 1) * SSM_STATE]
        cb = _dot_t(cg, bg.astype(bf16))
        bg_t = bg.T
        ssq = jnp.zeros((L, 1), f32)
        yg = []
        for r in range(SSM_REP):
            h = g * SSM_REP + r
            hs = slice(h * SSM_HEADDIM, (h + 1) * SSM_HEADDIM)
            a_col = acs_c[:, h:h + 1]
            a_row = acs_r[h:h + 1, :]
            dt_row = dt_r[h:h + 1, :]
            a_last = acs_r[h:h + 1, L - 1:L]
            seg = a_col - a_row
            decay = jnp.where(causal, jnp.exp(jnp.where(causal, seg, 0.0)), 0.0)
            w = cb * decay * dt_row
            x_h = xs[:, hs]
            xb = x_h.astype(bf16)
            st = st_ref[h]
            y = _dot(w.astype(bf16), xb)
            y = y + _dot(cg, st.astype(bf16)) * jnp.exp(a_col)
            y = y + dskip_ref[:, hs] * x_h
            bscaled = bg_t * (jnp.exp(a_last - a_row) * dt_row)
            st_ref[h] = jnp.exp(a_last) * st + _dot(bscaled.astype(bf16), xb)
            zh = z[:, hs]
            y = y * (zh * jax.nn.sigmoid(zh))
            ssq = ssq + jnp.sum(y * y, axis=-1, keepdims=True)
            yg.append(y)
        rs = lax.rsqrt(ssq / (SSM_REP * SSM_HEADDIM) + NORM_EPS)
        for r in range(SSM_REP):
            h = g * SSM_REP + r
            hs = slice(h * SSM_HEADDIM, (h + 1) * SSM_HEADDIM)
            o_ref[:, hs] = yg[r] * rs * nw_ref[:, hs]


def _ssd(proj, dtt, cw, cb, dtb_r, dtb_c, alog_r, alog_c, dskip, nw, b, t):
    nch = t // CHUNK
    row = lambda bi, ci: bi * nch + ci
    const2 = lambda bi, ci: (0, 0)
    return pl.pallas_call(
        _ssd_kernel,
        grid=(b, nch),
        in_specs=[pl.BlockSpec((CHUNK, XBC_WIDTH), lambda bi, ci: (row(bi, ci), COL_XBC // XBC_WIDTH)),
                  pl.BlockSpec((CHUNK, SSM_WIDTH), lambda bi, ci: (row(bi, ci), COL_Z // SSM_WIDTH)),
                  pl.BlockSpec((CHUNK, LANES), lambda bi, ci: (row(bi, ci), COL_SMALL // LANES)),
                  pl.BlockSpec((1, SSM_HEADS, CHUNK), lambda bi, ci: (bi, 0, ci)),
                  pl.BlockSpec((CONV_WIDTH, XBC_WIDTH), const2),
                  pl.BlockSpec((1, XBC_WIDTH), const2),
                  pl.BlockSpec((1, SSM_HEADS), const2),
                  pl.BlockSpec((SSM_HEADS, 1), const2),
                  pl.BlockSpec((1, SSM_HEADS), const2),
                  pl.BlockSpec((SSM_HEADS, 1), const2),
                  pl.BlockSpec((1, SSM_WIDTH), const2),
                  pl.BlockSpec((1, SSM_WIDTH), const2)],
        out_specs=pl.BlockSpec((CHUNK, SSM_WIDTH), lambda bi, ci: (row(bi, ci), 0)),
        out_shape=jax.ShapeDtypeStruct((b * t, SSM_WIDTH), f32),
        scratch_shapes=[pltpu.VMEM((HALO + CHUNK, XBC_WIDTH), f32),
                        pltpu.VMEM((SSM_HEADS, SSM_STATE, SSM_HEADDIM), f32)],
        compiler_params=_cparams(("parallel", "arbitrary")),
        name="ssd",
    )(proj, proj, proj, dtt, cw, cb, dtb_r, dtb_c, alog_r, alog_c, dskip, nw)


def _layer_norm(v, g, b):
    mu = jnp.mean(v, axis=-1, keepdims=True)
    d = v - mu
    var = jnp.mean(d * d, axis=-1, keepdims=True)
    return d * lax.rsqrt(var + NORM_EPS) * g + b


def _outproj_kernel(alpha, ya_ref, yb_ref, x_ref, wa_ref, wb_ref, g_ref, b_ref, wr_ref, br_ref,
                    h_ref, route_ref, cnt_ref):
    i = pl.program_id(0)
    tm = x_ref.shape[0]
    mix = _dot(ya_ref[...].astype(bf16), wa_ref[...]) + _dot(yb_ref[...].astype(bf16), wb_ref[...])
    h = _layer_norm(alpha * x_ref[...] + mix, g_ref[...], b_ref[...])
    h_ref[...] = h

    h_hi = h.astype(bf16)
    h_lo = (h - h_hi.astype(f32)).astype(bf16)
    t = _dot(h_hi, wr_ref[...])
    logits = t[:, :LANES] + t[:, LANES:] + _dot(h_lo, wr_ref[:, :LANES]) + br_ref[...]
    lane = lax.broadcasted_iota(i32, (tm, LANES), 1)
    ninf = -jnp.inf
    gmask = lane < N_EXPERT_GROUPS
    gl = jnp.where(gmask, logits, ninf)
    ge = jnp.where(gmask, jnp.exp(gl - jnp.max(gl, axis=-1, keepdims=True)), 0.0)
    pg = ge / jnp.sum(ge, axis=-1, keepdims=True)
    g_gate = jnp.max(pg, axis=-1, keepdims=True)
    g_sel = jnp.min(jnp.where(gmask & (pg == g_gate), lane, LANES), axis=-1, keepdims=True)
    lo = N_EXPERT_GROUPS + g_sel * EXPERTS_PER_GROUP
    emask = (lane >= lo) & (lane < lo + EXPERTS_PER_GROUP)
    el = jnp.where(emask, logits, ninf)
    ee = jnp.where(emask, jnp.exp(el - jnp.max(el, axis=-1, keepdims=True)), 0.0)
    pe = ee / jnp.sum(ee, axis=-1, keepdims=True)
    p0 = jnp.max(pe, axis=-1, keepdims=True)
    l0 = jnp.min(jnp.where(emask & (pe == p0), lane, LANES), axis=-1, keepdims=True)
    rest = jnp.where(emask & (lane != l0), pe, ninf)
    p1 = jnp.max(rest, axis=-1, keepdims=True)
    l1 = jnp.min(jnp.where(rest == p1, lane, LANES), axis=-1, keepdims=True)
    psum = p0 + p1
    w0 = g_gate * p0 / psum
    w1 = g_gate * p1 / psum
    e0 = l0 - N_EXPERT_GROUPS
    e1 = l1 - N_EXPERT_GROUPS

    @pl.when(i == 0)
    def _():
        cnt_ref[...] = jnp.zeros_like(cnt_ref)

    oh0 = lane == e0
    oh1 = lane == e1
    both = jnp.where(oh0, 1.0, 0.0) + jnp.where(oh1, 1.0, 0.0)
    r_i = lax.broadcasted_iota(i32, (tm, tm), 0)
    c_i = lax.broadcasted_iota(i32, (tm, tm), 1)
    strict = jnp.where(c_i < r_i, 1.0, 0.0).astype(bf16)
    before = _dot(strict, both.astype(bf16)) + cnt_ref[...]
    rank0 = jnp.sum(jnp.where(oh0, before, 0.0), axis=-1, keepdims=True)
    rank1 = jnp.sum(jnp.where(oh1, before, 0.0), axis=-1, keepdims=True)
    cnt_ref[...] = cnt_ref[...] + jnp.sum(both, axis=0, keepdims=True)

    out = jnp.where(lane == 0, e0.astype(f32), 0.0)
    out = jnp.where(lane == 1, e1.astype(f32), out)
    out = jnp.where(lane == 2, w0, out)
    out = jnp.where(lane == 3, w1, out)
    out = jnp.where(lane == 4, rank0, out)
    out = jnp.where(lane == 5, rank1, out)
    route_ref[...] = out


def _outproj(y_nsa, y_ssm, x2, wa, wb, g, bta, wr, br, alpha):
    n, d = x2.shape
    tm = OUT_TM
    const = lambda i: (0, 0)
    rowb = lambda i: (i, 0)
    return pl.pallas_call(
        functools.partial(_outproj_kernel, alpha),
        grid=(n // tm,),
        in_specs=[pl.BlockSpec((tm, NSA_WIDTH), rowb), pl.BlockSpec((tm, SSM_WIDTH), rowb),
                  pl.BlockSpec((tm, d), rowb),
                  pl.BlockSpec((NSA_WIDTH, d), const), pl.BlockSpec((SSM_WIDTH, d), const),
                  pl.BlockSpec((1, d), const), pl.BlockSpec((1, d), const),
                  pl.BlockSpec((d, 2 * LANES), const), pl.BlockSpec((1, LANES), const)],
        out_specs=[pl.BlockSpec((tm, d), rowb), pl.BlockSpec((tm, LANES), rowb),
                   pl.BlockSpec((1, LANES), const)],
        out_shape=[jax.ShapeDtypeStruct((n, d), f32), jax.ShapeDtypeStruct((n, LANES), f32),
                   jax.ShapeDtypeStruct((1, LANES), f32)],
        compiler_params=_cparams(("arbitrary",)),
        name="outproj",
    )(y_nsa, y_ssm, x2, wa, wb, g, bta, wr, br)


def _row_copy(src_ref, src_row, dst_ref, dst_row, sem):
    return pltpu.make_async_copy(src_ref.at[pl.ds(src_row, 1), :], dst_ref.at[pl.ds(dst_row, 1), :], sem)


def _dest_row(idx_ref, ps_ref, tm, k, r):
    return ps_ref[idx_ref[0, 0, k * tm + r]] + idx_ref[0, 0, (2 + k) * tm + r]


def _dispatch_kernel(idx_ref, ps_ref, zflag_ref, h_ref, xs_ref, zero_ref, tile_ref, sem, lsem, zsem):
    tm = DISP_TM
    nb = zflag_ref.shape[0]
    i = pl.program_id(0)

    @pl.when(i == 0)
    def _():
        zero_ref[...] = jnp.zeros_like(zero_ref)

        def zblock(i):
            rows = pl.ds(pl.multiple_of(i * MOE_TM, MOE_TM), MOE_TM)
            return pltpu.make_async_copy(zero_ref, xs_ref.at[rows, :], zsem)

        def zstart(i, _):
            @pl.when(zflag_ref[i] != 0)
            def _():
                zblock(i).start()
            return 0

        def zwait(i, _):
            @pl.when(zflag_ref[i] != 0)
            def _():
                zblock(i).wait()
            return 0

        lax.fori_loop(0, nb, zstart, 0)
        lax.fori_loop(0, nb, zwait, 0)

    nslot = tile_ref.shape[0]
    nsteps = pl.num_programs(0)
    slot = i % nslot

    def tile_load(step, s):
        rows = pl.ds(pl.multiple_of(step * tm, tm), tm)
        return pltpu.make_async_copy(h_ref.at[rows, :], tile_ref.at[s], lsem.at[s])

    @pl.when(i == 0)
    def _():
        tile_load(0, 0).start()

    tile_load(i, slot).wait()

    @pl.when(i + 1 < nsteps)
    def _():
        tile_load(i + 1, (i + 1) % nslot).start()

    def issue(r, _):
        for k in range(2):
            _row_copy(tile_ref.at[slot], r, xs_ref, _dest_row(idx_ref, ps_ref, tm, k, r), sem.at[slot]).start()
        return 0

    lax.fori_loop(0, tm, issue, 0, unroll=4)

    def wait_rows(s):
        for k in range(2):
            pltpu.make_async_copy(tile_ref.at[s], xs_ref.at[pl.ds(0, tm), :], sem.at[s]).wait()

    @pl.when(i > 0)
    def _():
        wait_rows((i - 1) % nslot)

    @pl.when(i == nsteps - 1)
    def _():
        wait_rows(slot)


def _dispatch(idx_t, pad_starts, zflag, h, p_rows):
    n, d = h.shape
    tm = DISP_TM
    return pl.pallas_call(
        _dispatch_kernel,
        grid=(n // tm,),
        in_specs=[pl.BlockSpec((1, 1, 4 * tm), lambda i: (i, 0, 0), memory_space=pltpu.SMEM),
                  pl.BlockSpec(memory_space=pltpu.SMEM),
                  pl.BlockSpec(memory_space=pltpu.SMEM),
                  pl.BlockSpec(memory_space=pl.ANY)],
        out_specs=pl.BlockSpec(memory_space=pl.ANY),
        out_shape=jax.ShapeDtypeStruct((p_rows, d), f32),
        scratch_shapes=[pltpu.VMEM((MOE_TM, d), f32), pltpu.VMEM((3, tm, d), f32),
                        pltpu.SemaphoreType.DMA((3,)), pltpu.SemaphoreType.DMA((3,)), pltpu.SemaphoreType.DMA(())],
        compiler_params=_cparams(("arbitrary",)),
        name="dispatch",
    )(idx_t, pad_starts, zflag, h)


def _experts_kernel(be_ref, nu_ref, first_ref, next_ref, slot_ref, xs_ref, wg_ref, wu_ref, wd_ref, y_ref,
                    fg_ref, fu_ref, fd_ref, wgb_ref, wub_ref, wdb_ref, sems):
    i = pl.program_id(0)

    def weight_loads(e, s):
        return (pltpu.make_async_copy(wg_ref.at[e], fg_ref.at[s], sems.at[s, 0]),
                pltpu.make_async_copy(wu_ref.at[e], fu_ref.at[s], sems.at[s, 1]),
                pltpu.make_async_copy(wd_ref.at[e], fd_ref.at[s], sems.at[s, 2]))

    @pl.when(i == 0)
    def _():
        for c in weight_loads(be_ref[0], 0):
            c.start()

    @pl.when(first_ref[i] != 0)
    def _():
        s = slot_ref[i]
        for c in weight_loads(be_ref[i], s):
            c.wait()
        wgb_ref[...] = fg_ref[s].astype(bf16)
        wub_ref[...] = fu_ref[s].astype(bf16)
        wdb_ref[...] = fd_ref[s].astype(bf16)

        @pl.when(next_ref[i] >= 0)
        def _():
            for c in weight_loads(next_ref[i], 1 - s):
                c.start()

    @pl.when(i < nu_ref[0])
    def _():
        xb = xs_ref[...].astype(bf16)
        gte = _dot(xb, wgb_ref[...])
        up = _dot(xb, wub_ref[...])
        act = gte * jax.nn.sigmoid(gte) * up
        y_ref[...] = _dot(act.astype(bf16), wdb_ref[...])

    @pl.when(i >= nu_ref[0])
    def _():
        y_ref[...] = jnp.zeros_like(y_ref)


def _experts(block_e, n_used, first, next_e, slot, xs, w_gate, w_up, w_down):
    p_rows, d = xs.shape
    tm = MOE_TM
    nb = p_rows // tm
    de = w_gate.shape[-1]
    xmap = lambda i, be, nu, fi, ne, sl: (jnp.maximum(jnp.minimum(i, nu[0] - 1), 0), 0)
    hbm = pl.BlockSpec(memory_space=pl.ANY)
    return pl.pallas_call(
        _experts_kernel,
        grid_spec=pltpu.PrefetchScalarGridSpec(
            num_scalar_prefetch=5,
            grid=(nb,),
            in_specs=[pl.BlockSpec((tm, d), xmap), hbm, hbm, hbm],
            out_specs=pl.BlockSpec((tm, d), lambda i, be, nu, fi, ne, sl: (i, 0)),
            scratch_shapes=[pltpu.VMEM((2, d, de), f32), pltpu.VMEM((2, d, de), f32), pltpu.VMEM((2, de, d), f32),
                            pltpu.VMEM((d, de), bf16), pltpu.VMEM((d, de), bf16), pltpu.VMEM((de, d), bf16),
                            pltpu.SemaphoreType.DMA((2, 3))],
        ),
        out_shape=jax.ShapeDtypeStruct((p_rows, d), f32),
        compiler_params=_cparams(("arbitrary",)),
        name="experts",
    )(block_e, n_used, first, next_e, slot, xs, w_gate, w_up, w_down)


def _combine_kernel(alpha, idx_ref, nidx_ref, ps_ref, route_ref, h_ref, g_ref, b_ref, y_ref, o_ref, buf_ref, sems):
    tm = h_ref.shape[0]
    i = pl.program_id(0)
    slot = i % 2

    def gather(ids_ref, s):
        def issue(r, _):
            for k in range(2):
                _row_copy(y_ref, _dest_row(ids_ref, ps_ref, tm, k, r), buf_ref.at[s, k], r, sems.at[s]).start()
            return 0
        lax.fori_loop(0, tm, issue, 0, unroll=4)

    @pl.when(i == 0)
    def _():
        gather(idx_ref, 0)

    @pl.when(i + 1 < pl.num_programs(0))
    def _():
        gather(nidx_ref, 1 - slot)

    for k in range(2):
        pltpu.make_async_copy(y_ref.at[pl.ds(0, tm), :], buf_ref.at[slot, k], sems.at[slot]).wait()

    route = route_ref[...]
    ffn = route[:, 2:3] * buf_ref[slot, 0] + route[:, 3:4] * buf_ref[slot, 1]
    o_ref[...] = _layer_norm(alpha * h_ref[...] + ffn, g_ref[...], b_ref[...])


def _combine(idx_t, pad_starts, route, h, g, bta, y, alpha):
    n, d = h.shape
    tm = COMB_TM
    const = lambda i: (0, 0)
    return pl.pallas_call(
        functools.partial(_combine_kernel, alpha),
        grid=(n // tm,),
        in_specs=[pl.BlockSpec((1, 1, 4 * tm), lambda i: (i, 0, 0), memory_space=pltpu.SMEM),
                  pl.BlockSpec((1, 1, 4 * tm), lambda i: (jnp.minimum(i + 1, n // tm - 1), 0, 0),
                               memory_space=pltpu.SMEM),
                  pl.BlockSpec(memory_space=pltpu.SMEM),
                  pl.BlockSpec((tm, LANES), lambda i: (i, 0)),
                  pl.BlockSpec((tm, d), lambda i: (i, 0)),
                  pl.BlockSpec((1, d), const), pl.BlockSpec((1, d), const),
                  pl.BlockSpec(memory_space=pl.ANY)],
        out_specs=pl.BlockSpec((tm, d), lambda i: (i, 0)),
        out_shape=jax.ShapeDtypeStruct((n, d), f32),
        scratch_shapes=[pltpu.VMEM((2, 2, tm, d), f32), pltpu.SemaphoreType.DMA((2,))],
        compiler_params=_cparams(("arbitrary",)),
        name="combine",
    )(idx_t, idx_t, pad_starts, route, h, g, bta, y)


def _tile_idx(idx, tm):
    n = idx.shape[1]
    return idx.reshape(4, n // tm, tm).transpose(1, 0, 2).reshape(n // tm, 1, 4 * tm)


def _layer(x, positions, w_in, cmp_k_pe, cmp_k_w1, cmp_k_b1, cmp_k_w2, cmp_v_pe, cmp_v_w1, cmp_v_b1, cmp_v_w2,
           conv_w, conv_b, dt_bias, a_log, d_skip, ssm_norm_w, w_out, ln1_g, ln1_b,
           w_router_group, b_router_group, w_router_expert, b_router_expert, w_gate, w_up, w_down, ln2_g, ln2_b,
           alpha):
    b, t, d = x.shape
    n = b * t
    x2 = x.reshape(n, d)

    c0 = NSA_WIDTH
    c1 = c0 + 6 * KV_WIDTH
    c2 = c1 + 3 * NSA_HEADS
    c3 = c2 + SSM_WIDTH
    c4 = c3 + XBC_WIDTH
    w_small = jnp.concatenate([w_in[:, c1:c2], w_in[:, c4:], jnp.zeros((d, LANES - 3 * NSA_HEADS - SSM_HEADS), f32)], axis=1)
    w_cat = jnp.concatenate([w_in[:, :c0], w_in[:, c2:c3], w_in[:, c3:c4], w_in[:, c0:c1], w_small,
                             jnp.zeros((d, PROJ_COLS - COL_SMALL - LANES), f32)], axis=1).astype(bf16)
    proj = _proj(x2, w_cat)

    lane = np.arange(LANES) % HEAD_DIM
    inv_freq = ROPE_THETA ** (-jnp.arange(0, ROT_DIM, 2, dtype=f32) / ROT_DIM)
    invf = jnp.where(lane < ROT_DIM, inv_freq[lane % (ROT_DIM // 2)], 0.0).astype(f32)[None, :]
    pos128 = jnp.broadcast_to(positions.reshape(n, 1), (n, LANES))
    q_r, k_cmp, v_cmp, k_sel, v_sel, k_win, v_win = _nsa_prep(proj, pos128, invf, b, t)

    nc = t // CMP_STRIDE
    half_w = CMP_STRIDE * HEAD_DIM
    a = jnp.stack([k_cmp, v_cmp]).reshape(2, b * NSA_KV_GROUPS, nc, half_w)
    pe = jnp.stack([cmp_k_pe, cmp_v_pe]).reshape(2, 2, 1, half_w)
    w1 = jnp.stack([cmp_k_w1, cmp_v_w1]).reshape(2, 2, half_w, CMP_HIDDEN).astype(bf16)
    b1 = jnp.stack([cmp_k_b1, cmp_v_b1]).reshape(2, 1, CMP_HIDDEN)
    w2 = jnp.pad(jnp.stack([cmp_k_w2, cmp_v_w2]), ((0, 0), (0, 0), (0, LANES - HEAD_DIM))).astype(bf16)
    cend = jnp.minimum(jnp.arange(nc) * CMP_STRIDE + CMP_BLOCK - 1, t - 1)
    posc = jnp.broadcast_to(positions[:, cend][:, :, None], (b, nc, LANES))
    kvc, kvc_t = _cmp_mlp(a, pe, w1, b1, w2, posc, invf, b)

    c_start = np.arange(nc)[:, None] * CMP_STRIDE
    s_start = np.arange(LANES)[None, :] * SEL_BLOCK
    cover = ((c_start < s_start + SEL_BLOCK) & (c_start + CMP_BLOCK > s_start)
             & (np.arange(nc)[:, None] < nc - 1) & (np.arange(LANES)[None, :] < t // SEL_BLOCK))
    cover = jnp.asarray(cover, bf16)
    o_cmp, selb = _cmp_attn(q_r, kvc, kvc_t, cover, b, t)

    y_nsa = _nsa_attn(q_r, k_sel, v_sel, k_win, v_win, selb, o_cmp, proj, b, t)

    dt_raw = proj[:, COL_SMALL + SMALL_DT_OFF:COL_SMALL + SMALL_DT_OFF + SSM_HEADS]
    dtt = dt_raw.reshape(b, t, SSM_HEADS).transpose(0, 2, 1)
    y_ssm = _ssd(proj, dtt, conv_w.reshape(CONV_WIDTH, XBC_WIDTH), conv_b.reshape(1, XBC_WIDTH),
                 dt_bias.reshape(1, SSM_HEADS), dt_bias.reshape(SSM_HEADS, 1),
                 a_log.reshape(1, SSM_HEADS), a_log.reshape(SSM_HEADS, 1),
                 jnp.repeat(d_skip, SSM_HEADDIM).reshape(1, SSM_WIDTH), ssm_norm_w.reshape(1, SSM_WIDTH), b, t)

    wr = jnp.concatenate([w_router_group, w_router_expert,
                          jnp.zeros((d, LANES - N_EXPERT_GROUPS - N_EXPERTS), f32)], axis=1)
    br = jnp.concatenate([b_router_group, b_router_expert,
                          jnp.zeros((LANES - N_EXPERT_GROUPS - N_EXPERTS,), f32)])[None, :]
    wr_hi = wr.astype(bf16)
    wr = jnp.concatenate([wr_hi, (wr - wr_hi.astype(f32)).astype(bf16)], axis=1)
    wo = w_out.astype(bf16)
    h, route, counts = _outproj(y_nsa, y_ssm, x2, wo[:NSA_WIDTH], wo[NSA_WIDTH:], ln1_g[None, :], ln1_b[None, :],
                                wr, br, alpha)

    cnt = counts[0, :N_EXPERTS].astype(i32)
    padded = (cnt + MOE_TM - 1) // MOE_TM * MOE_TM
    pad_ends = jnp.cumsum(padded)
    pad_starts = pad_ends - padded
    idx = jnp.concatenate([route[:, 0:2], route[:, 4:6]], axis=1).astype(i32).T
    p_rows = 2 * n + N_EXPERTS * MOE_TM
    nb = p_rows // MOE_TM
    block_e = jnp.minimum(jnp.sum(jnp.arange(nb, dtype=i32)[:, None] * MOE_TM >= pad_ends[None, :], axis=-1),
                          N_EXPERTS - 1).astype(i32)
    n_used = (pad_ends[-1] // MOE_TM).astype(i32).reshape(1)
    blk = jnp.arange(nb, dtype=i32)
    last_of_expert = jnp.any((blk[:, None] + 1) * MOE_TM == pad_ends[None, :], axis=-1)
    zflag = (last_of_expert | (blk >= n_used[0])).astype(i32)
    block_e = jnp.where(blk < n_used[0], block_e, block_e[jnp.maximum(n_used[0] - 1, 0)])

    xs = _dispatch(_tile_idx(idx, DISP_TM), pad_starts, zflag, h, p_rows)
    prev_e = jnp.concatenate([jnp.full((1,), -1, i32), block_e[:-1]])
    first = ((block_e != prev_e) & (blk < n_used[0])).astype(i32)
    eidx = jnp.arange(N_EXPERTS, dtype=i32)
    later = jnp.where((eidx[None, :] > eidx[:, None]) & (padded[None, :] > 0), eidx[None, :], N_EXPERTS)
    next_of = jnp.min(later, axis=1)
    next_e = jnp.where(next_of[block_e] < N_EXPERTS, next_of[block_e], -1).astype(i32)
    slot = ((jnp.cumsum(first) - 1) % 2).astype(i32)
    y = _experts(block_e, n_used, first, next_e, slot, xs, w_gate, w_up, w_down)
    out = _combine(_tile_idx(idx, COMB_TM), pad_starts, route, h, ln2_g[None, :], ln2_b[None, :], y, alpha)
    return out.reshape(b, t, d)


def kernel(x, positions, w_in, cmp_k_pe, cmp_k_w1, cmp_k_b1, cmp_k_w2, cmp_v_pe, cmp_v_w1, cmp_v_b1, cmp_v_w2, conv_w, conv_b, dt_bias, a_log, d_skip, ssm_norm_w, w_out, ln1_g, ln1_b, w_router_group, b_router_group, w_router_expert, b_router_expert, w_gate, w_up, w_down, ln2_g, ln2_b):
    depth = w_in.shape[0]
    alpha = (2 * depth) ** 0.25
    params = (w_in, cmp_k_pe, cmp_k_w1, cmp_k_b1, cmp_k_w2, cmp_v_pe, cmp_v_w1, cmp_v_b1, cmp_v_w2, conv_w, conv_b,
              dt_bias, a_log, d_skip, ssm_norm_w, w_out, ln1_g, ln1_b, w_router_group, b_router_group,
              w_router_expert, b_router_expert, w_gate, w_up, w_down, ln2_g, ln2_b)
    for l in range(depth):
        x = _layer(x, positions, *[p[l] for p in params], alpha)
    return x
```

```python
import functools
import math

import jax
import jax.numpy as jnp
import numpy as np
from jax import lax
from jax.experimental import pallas as pl
from jax.experimental.pallas import tpu as pltpu

f32 = jnp.float32
bf16 = jnp.bfloat16
i32 = jnp.int32

HEAD_DIM = 64
NSA_HEADS = 16
NSA_KV_GROUPS = 2
NSA_REP = NSA_HEADS // NSA_KV_GROUPS
NSA_WIDTH = NSA_HEADS * HEAD_DIM
KV_WIDTH = NSA_KV_GROUPS * HEAD_DIM
CMP_BLOCK = 32
CMP_STRIDE = 16
CMP_HIDDEN = 256
SEL_BLOCK = 64
SEL_TOPK = 16
WINDOW = 512
FORCED_SCORE = 1.0e4
SSM_HEADDIM = 64
SSM_HEADS = 16
SSM_WIDTH = SSM_HEADS * SSM_HEADDIM
SSM_GROUPS = 4
SSM_REP = SSM_HEADS // SSM_GROUPS
SSM_STATE = 128
CONV_WIDTH = 4
CHUNK = 256
XBC_WIDTH = SSM_WIDTH + 2 * SSM_GROUPS * SSM_STATE
ROPE_THETA = 500000.0
ROT_DIM = HEAD_DIM // 4
N_EXPERT_GROUPS = 4
EXPERTS_PER_GROUP = 8
N_EXPERTS = N_EXPERT_GROUPS * EXPERTS_PER_GROUP
D_EXPERT = 512
NORM_EPS = 1e-5

LANES = 128
MASK_NEG = -1.0e30

PROJ_TM = 1024
PROJ_TN = 1024
PREP_TM = 512
CMP_TQ = 256
ATT_TQ = 256
ATT_TK = 512
WIN_TK = 256
WIN_HEADS = 4
KX_WIDTH = 2 * LANES
OUT_TM = 512
MOE_TM = 256
DISP_TM = 256
COMB_TM = 128
VMEM_LIMIT = 56 * 1024 * 1024

COL_Q = 0
COL_Z = NSA_WIDTH
COL_XBC = COL_Z + SSM_WIDTH
COL_KV = COL_XBC + XBC_WIDTH
COL_SMALL = COL_KV + 6 * KV_WIDTH
PROJ_COLS = 5120
SMALL_DT_OFF = 3 * NSA_HEADS


def _cparams(sem, vmem=VMEM_LIMIT):
    return pltpu.CompilerParams(dimension_semantics=sem, vmem_limit_bytes=vmem)


def _dot(a, b):
    return jnp.dot(a, b, preferred_element_type=f32)


def _dot_t(a, b):
    return lax.dot_general(a, b, (((1,), (1,)), ((), ())), preferred_element_type=f32)


def _dot_hi(a, b):
    return jnp.dot(a, b, preferred_element_type=f32, precision=lax.Precision.HIGHEST)


def _proj_kernel(x_ref, w_ref, o_ref, xb_ref):
    @pl.when(pl.program_id(1) == 0)
    def _():
        xb_ref[...] = x_ref[...].astype(bf16)

    o_ref[...] = _dot(xb_ref[...], w_ref[...])


def _proj(x2, w_cat):
    n, d = x2.shape
    cols = w_cat.shape[1]
    tm = min(PROJ_TM, n)
    return pl.pallas_call(
        _proj_kernel,
        grid=(n // tm, cols // PROJ_TN),
        in_specs=[pl.BlockSpec((tm, d), lambda i, j: (i, 0)),
                  pl.BlockSpec((d, PROJ_TN), lambda i, j: (0, j))],
        out_specs=pl.BlockSpec((tm, PROJ_TN), lambda i, j: (i, j)),
        out_shape=jax.ShapeDtypeStruct((n, cols), f32),
        scratch_shapes=[pltpu.VMEM((tm, d), bf16)],
        compiler_params=_cparams(("parallel", "arbitrary")),
        name="proj",
    )(x2, w_cat)


def _rope_tables(pos_i32, invf):
    ang = pos_i32.astype(f32) * invf
    return jnp.cos(ang), jnp.sin(ang)


def _rope128(x, cos, sin):
    half = ROT_DIM // 2
    d = lax.broadcasted_iota(i32, x.shape, 1) % HEAD_DIM
    up = pltpu.roll(x, LANES - half, 1)
    dn = pltpu.roll(x, half, 1)
    rot = jnp.where(d < half, -up, dn)
    return x * cos + rot * sin


def _nsa_prep_kernel(pos_ref, invf_ref, q_ref, kc_ref, vc_ref, ks_ref, vs_ref, kw_ref, vw_ref,
                     qo_ref, kco_ref, vco_ref, kso_ref, vso_ref, kwo_ref, vwo_ref):
    cos, sin = _rope_tables(pos_ref[...], invf_ref[...])
    scale = HEAD_DIM ** -0.5
    for c in range(NSA_WIDTH // LANES):
        t = _rope128(q_ref[:, c * LANES:(c + 1) * LANES], cos, sin) * scale
        qo_ref[0, 2 * c] = t[:, :HEAD_DIM].astype(bf16)
        qo_ref[0, 2 * c + 1] = t[:, HEAD_DIM:].astype(bf16)

    def split(src, dst, rope, dt):
        t = src[...]
        if rope:
            t = _rope128(t, cos, sin)
        for g in range(NSA_KV_GROUPS):
            dst[0, g] = t[:, g * HEAD_DIM:(g + 1) * HEAD_DIM].astype(dt)

    split(kc_ref, kco_ref, False, f32)
    split(vc_ref, vco_ref, False, f32)

    tm = pos_ref.shape[0]

    def values_with_ones(src, dst):
        t = src[...]
        tail = jnp.ones((tm, HEAD_DIM), bf16)
        for g in range(NSA_KV_GROUPS):
            vg = t[:, g * HEAD_DIM:(g + 1) * HEAD_DIM].astype(bf16)
            dst[0, g] = jnp.concatenate([vg, tail], axis=1)

    values_with_ones(vs_ref, vso_ref)
    values_with_ones(vw_ref, vwo_ref)

    kw_t = _rope128(kw_ref[...], cos, sin).T
    ks_t = _rope128(ks_ref[...], cos, sin).T
    for g in range(NSA_KV_GROUPS):
        for c in range(tm // WIN_TK):
            kwo_ref[0, g, c] = kw_t[g * HEAD_DIM:(g + 1) * HEAD_DIM, c * WIN_TK:(c + 1) * WIN_TK].astype(bf16)
    blk = lax.broadcasted_iota(i32, (LANES, tm), 0)
    tok = pl.program_id(1) * tm + lax.broadcasted_iota(i32, (LANES, tm), 1)
    onehot_t = jnp.where(tok // SEL_BLOCK == blk, 1.0, 0.0).astype(bf16)
    for g in range(NSA_KV_GROUPS):
        kso_ref[0, g, 0, 0:LANES, :] = onehot_t
        kg = ks_t[g * HEAD_DIM:(g + 1) * HEAD_DIM, :].astype(bf16)
        kso_ref[0, g, 0, LANES:2 * LANES, :] = jnp.concatenate([kg, jnp.zeros_like(kg)], axis=0)


def _nsa_prep(proj, pos128, invf, b, t):
    tm = PREP_TM
    assert tm == ATT_TK
    nt = t // tm
    row = lambda bi, ti: (bi * nt + ti, 0)
    kv0 = COL_KV // LANES
    in_specs = [pl.BlockSpec((tm, LANES), row),
                pl.BlockSpec((1, LANES), lambda bi, ti: (0, 0)),
                pl.BlockSpec((tm, NSA_WIDTH), lambda bi, ti: (bi * nt + ti, COL_Q // NSA_WIDTH))]
    for k in range(6):
        in_specs.append(pl.BlockSpec((tm, LANES), functools.partial(lambda bi, ti, k: (bi * nt + ti, kv0 + k), k=k)))
    head = lambda bi, ti: (bi, 0, ti, 0)
    tile5 = lambda bi, ti: (bi, 0, ti, 0, 0)
    g = NSA_KV_GROUPS
    out_specs = [pl.BlockSpec((1, NSA_HEADS, tm, HEAD_DIM), head),
                 pl.BlockSpec((1, g, tm, HEAD_DIM), head), pl.BlockSpec((1, g, tm, HEAD_DIM), head),
                 pl.BlockSpec((1, g, 1, KX_WIDTH, tm), tile5), pl.BlockSpec((1, g, tm, LANES), head),
                 pl.BlockSpec((1, g, tm // WIN_TK, HEAD_DIM, WIN_TK), tile5), pl.BlockSpec((1, g, tm, LANES), head)]
    out_shape = [jax.ShapeDtypeStruct((b, NSA_HEADS, t, HEAD_DIM), bf16),
                 jax.ShapeDtypeStruct((b, g, t, HEAD_DIM), f32), jax.ShapeDtypeStruct((b, g, t, HEAD_DIM), f32),
                 jax.ShapeDtypeStruct((b, g, nt, KX_WIDTH, tm), bf16), jax.ShapeDtypeStruct((b, g, t, LANES), bf16),
                 jax.ShapeDtypeStruct((b, g, t // WIN_TK, HEAD_DIM, WIN_TK), bf16),
                 jax.ShapeDtypeStruct((b, g, t, LANES), bf16)]
    return pl.pallas_call(
        _nsa_prep_kernel,
        grid=(b, nt),
        in_specs=in_specs,
        out_specs=out_specs,
        out_shape=out_shape,
        compiler_params=_cparams(("parallel", "parallel")),
        name="nsa_prep",
    )(pos128, invf, proj, proj, proj, proj, proj, proj, proj)


def _cmp_mlp_kernel(a_ref, pe_ref, w1_ref, b1_ref, w2_ref, pos_ref, invf_ref, o_ref, ot_ref):
    kind = pl.program_id(0)
    a = a_ref[0, 0]
    nc = a.shape[0]
    u = _dot((a + pe_ref[0, 0]).astype(bf16), w1_ref[0, 0])
    v = _dot((a + pe_ref[0, 1]).astype(bf16), w1_ref[0, 1])
    v_next = pltpu.roll(v, nc - 1, 0)
    hid = jax.nn.gelu(u + v_next + b1_ref[0])
    out = _dot(hid.astype(bf16), w2_ref[0])
    cos, sin = _rope_tables(pos_ref[0], invf_ref[...])
    roped = _rope128(out, cos, sin)
    out = jnp.where(kind == 0, roped, out)
    o_ref[0, 0] = out[:, :HEAD_DIM].astype(bf16)
    ot_ref[0, 0] = out.T[:HEAD_DIM, :].astype(bf16)


def _cmp_mlp(a, pe, w1, b1, w2, posc, invf, b):
    _, bg, nc, hw = a.shape
    g = bg // b
    return pl.pallas_call(
        _cmp_mlp_kernel,
        grid=(2, bg),
        in_specs=[pl.BlockSpec((1, 1, nc, hw), lambda k, i: (k, i, 0, 0)),
                  pl.BlockSpec((1, 2, 1, hw), lambda k, i: (k, 0, 0, 0)),
                  pl.BlockSpec((1, 2, hw, CMP_HIDDEN), lambda k, i: (k, 0, 0, 0)),
                  pl.BlockSpec((1, 1, CMP_HIDDEN), lambda k, i: (k, 0, 0)),
                  pl.BlockSpec((1, CMP_HIDDEN, LANES), lambda k, i: (k, 0, 0)),
                  pl.BlockSpec((1, nc, LANES), lambda k, i: (i // g, 0, 0)),
                  pl.BlockSpec((1, LANES), lambda k, i: (0, 0))],
        out_specs=[pl.BlockSpec((1, 1, nc, HEAD_DIM), lambda k, i: (k, i, 0, 0)),
                   pl.BlockSpec((1, 1, HEAD_DIM, nc), lambda k, i: (k, i, 0, 0))],
        out_shape=[jax.ShapeDtypeStruct((2, bg, nc, HEAD_DIM), bf16),
                   jax.ShapeDtypeStruct((2, bg, HEAD_DIM, nc), bf16)],
        compiler_params=_cparams(("parallel", "parallel")),
        name="cmp_mlp",
    )(a, pe, w1, b1, w2, posc, invf)


def _cmp_attn_kernel(q_ref, kct_ref, vc_ref, cover_ref, oc_ref, sel_ref):
    qi = pl.program_id(2)
    nc = vc_ref.shape[2]
    rows = NSA_REP * CMP_TQ
    tq = qi * CMP_TQ + lax.broadcasted_iota(i32, (CMP_TQ, 1), 0)
    row_live = jnp.where(tq >= CMP_BLOCK - 1, 1.0, 0.0)
    tiny = jnp.finfo(f32).tiny

    def attend(ncols):
        kct = kct_ref[0, 0, :, :ncols]
        vc = vc_ref[0, 0, :ncols, :]
        cend = lax.broadcasted_iota(i32, (1, ncols), 1) * CMP_STRIDE + (CMP_BLOCK - 1)
        bias = jnp.where(cend <= tq, 0.0, MASK_NEG)
        s = _dot(q_ref[0].reshape(rows, HEAD_DIM), kct).reshape(NSA_REP, CMP_TQ, ncols) + bias[None]
        e = jnp.exp(s - jnp.max(s, axis=-1, keepdims=True))
        live = row_live[None]
        p = e * (live / jnp.maximum(live * jnp.sum(e, axis=-1, keepdims=True), tiny))
        oc_ref[0] = _dot(p.reshape(rows, ncols).astype(bf16), vc).reshape(NSA_REP, CMP_TQ, HEAD_DIM)
        psum = jnp.sum(p, axis=0)
        hi = psum.astype(bf16)
        lo = (psum - hi.astype(f32)).astype(bf16)
        cover = cover_ref[:ncols, :]
        imp = _dot(hi, cover) + _dot(lo, cover)

        j = lax.broadcasted_iota(i32, (CMP_TQ, LANES), 1)
        cur = tq // SEL_BLOCK
        forced = (j == 0) | (j == cur) | (j == cur - 1)
        valid = j * SEL_BLOCK <= tq
        imp = jnp.where(valid, jnp.where(forced, FORCED_SCORE, imp), -FORCED_SCORE)

        nblk = ncols * CMP_STRIDE // SEL_BLOCK
        jt = lax.broadcasted_iota(i32, (nblk, CMP_TQ), 0)

        def pick(_, carry):
            work, sel = carry
            m = jnp.max(work, axis=0, keepdims=True)
            first = jnp.min(jnp.where(work == m, jt, LANES), axis=0, keepdims=True)
            hit = jt == first
            return jnp.where(hit, -jnp.inf, work), jnp.where(hit, 1.0, sel)

        _, sel_t = lax.fori_loop(0, SEL_TOPK, pick, (imp.T[:nblk], jnp.zeros((nblk, CMP_TQ), f32)))
        if nblk < LANES:
            sel_t = jnp.concatenate([sel_t, jnp.zeros((LANES - nblk, CMP_TQ), f32)], axis=0)
        sel_ref[0, 0] = jnp.where(valid, jnp.where(sel_t.T > 0.0, 0.0, MASK_NEG), MASK_NEG).astype(bf16)

    need = (qi + 1) * (CMP_TQ // CMP_STRIDE)
    for ncols in range(LANES, nc + 1, LANES):
        @pl.when((need > ncols - LANES) & (need <= ncols))
        def _():
            attend(ncols)


def _cmp_attn(q_r, kvc, kvc_t, cover, b, t):
    g = NSA_KV_GROUPS
    nc = kvc.shape[2]
    nq = t // CMP_TQ
    return pl.pallas_call(
        _cmp_attn_kernel,
        grid=(b, g, nq),
        in_specs=[pl.BlockSpec((1, NSA_REP, CMP_TQ, HEAD_DIM), lambda bi, gi, qi: (bi, gi, qi, 0)),
                  pl.BlockSpec((1, 1, HEAD_DIM, nc), lambda bi, gi, qi: (0, bi * g + gi, 0, 0)),
                  pl.BlockSpec((1, 1, nc, HEAD_DIM), lambda bi, gi, qi: (1, bi * g + gi, 0, 0)),
                  pl.BlockSpec((nc, LANES), lambda bi, gi, qi: (0, 0))],
        out_specs=[pl.BlockSpec((1, NSA_REP, CMP_TQ, HEAD_DIM), lambda bi, gi, qi: (bi, gi, qi, 0)),
                   pl.BlockSpec((1, 1, CMP_TQ, LANES), lambda bi, gi, qi: (bi, gi, qi, 0))],
        out_shape=[jax.ShapeDtypeStruct((b, NSA_HEADS, t, HEAD_DIM), f32),
                   jax.ShapeDtypeStruct((b, g, t, LANES), bf16)],
        compiler_params=_cparams(("parallel", "parallel", "parallel")),
        name="cmp_attn",
    )(q_r, kvc_t, kvc, cover)


def _nsa_attn_kernel(q_ref, kx_ref, vs_ref, kw_ref, vw_ref, sel_ref, oc_ref, gate_ref,
                     o_ref, qx_ref, s_ref, m_ref, acc_ref, yw_ref):
    gi = pl.program_id(1)
    qi = pl.program_id(2)
    start = qi * ATT_TQ
    tq = start + lax.broadcasted_iota(i32, (ATT_TQ, 1), 0)

    selb = sel_ref[0, 0]
    for r in range(NSA_REP):
        qx_ref[r * ATT_TQ:(r + 1) * ATT_TQ, 0:LANES] = selb
        qr = q_ref[0, r]
        qx_ref[r * ATT_TQ:(r + 1) * ATT_TQ, LANES:2 * LANES] = jnp.concatenate([qr, jnp.zeros_like(qr)], axis=1)
    rows = NSA_REP * ATT_TQ

    m_ref[...] = jnp.full(m_ref.shape, MASK_NEG, f32)
    acc_ref[...] = jnp.zeros(acc_ref.shape, f32)

    def consume(kt, v_ref, bias, parts=1):
        k0 = pl.multiple_of(kt * ATT_TK, ATT_TK)
        v = v_ref[0, 0, pl.ds(k0, ATT_TK), :]
        heads = NSA_REP // parts
        for part in range(parts):
            rs = slice(part * heads * ATT_TQ, (part + 1) * heads * ATT_TQ)
            s = s_ref[rs]
            if bias is not None:
                s = (s.reshape(heads, ATT_TQ, ATT_TK) + bias[None]).reshape(heads * ATT_TQ, ATT_TK)
            m_prev = m_ref[rs]
            m_new = jnp.maximum(m_prev, jnp.max(s, axis=-1, keepdims=True))
            alpha = jnp.exp(m_prev - m_new)
            p = jnp.exp(s - jnp.concatenate([m_new] * (ATT_TK // LANES), axis=1))
            acc_ref[rs] = alpha * acc_ref[rs] + _dot(p.astype(bf16), v)
            m_ref[rs] = m_new

    def kpos(kt):
        return kt * ATT_TK + lax.broadcasted_iota(i32, (1, ATT_TK), 1)

    last = start // ATT_TK
    for part in range(2):
        rs = slice(part * rows // 2, (part + 1) * rows // 2)
        s_ref[rs] = _dot(qx_ref[rs], kx_ref[0, 0, 0])

    def sel_step(kt, _):
        consume(kt, vs_ref, None)
        s_ref[...] = _dot(qx_ref[...], kx_ref[0, 0, kt + 1])
        return 0

    n_win = (WINDOW + ATT_TQ) // WIN_TK
    w_first = jnp.maximum(start // WIN_TK - WINDOW // WIN_TK, 0)
    kw = jnp.concatenate([kw_ref[0, 0, w_first + i] for i in range(n_win)], axis=1)
    w0 = pl.multiple_of(w_first * WIN_TK, WIN_TK)
    vw = vw_ref[0, 0, pl.ds(w0, n_win * WIN_TK), :]
    kp = w0 + lax.broadcasted_iota(i32, (1, n_win * WIN_TK), 1)
    wbias = jnp.where(kp <= tq, jnp.where(kp > tq - WINDOW, 0.0, MASK_NEG), MASK_NEG)
    hh = WIN_HEADS
    gates = jax.nn.sigmoid(gate_ref[...])
    per_group = 3 * NSA_REP
    shifted = gates
    for gg in range(1, NSA_KV_GROUPS):
        shifted = jnp.where(gi == gg, pltpu.roll(gates, LANES - gg * per_group, 1), shifted)

    def gate(r, br):
        c = r * 3 + br
        return jnp.broadcast_to(shifted[:, c:c + 1], (ATT_TQ, HEAD_DIM))

    upper = pltpu.roll(shifted, HEAD_DIM, 1)[:, HEAD_DIM:]

    def gated_output(a, r, br):
        c = r * 3 + br
        ratio = upper / a[:, HEAD_DIM:]
        return a[:, :HEAD_DIM] * jnp.broadcast_to(ratio[:, c:c + 1], (ATT_TQ, HEAD_DIM))

    for half in range(NSA_REP // hh):
        qh = q_ref[0, half * hh:(half + 1) * hh].reshape(hh * ATT_TQ, HEAD_DIM)
        sw = _dot(qh, kw).reshape(hh, ATT_TQ, n_win * WIN_TK) + wbias[None]
        sw = sw.reshape(hh * ATT_TQ, n_win * WIN_TK)
        pw = jnp.exp(sw - jnp.max(sw, axis=-1, keepdims=True))
        aw = _dot(pw.astype(bf16), vw)
        for i in range(hh):
            r = half * hh + i
            a = aw[i * ATT_TQ:(i + 1) * ATT_TQ]
            yw_ref[r] = gated_output(a, r, 2)

    def sel_pair(j, _):
        sel_step(2 * j, 0)
        sel_step(2 * j + 1, 0)
        return 0

    lax.fori_loop(0, last // 2, sel_pair, 0)

    @pl.when(last % 2 == 1)
    def _():
        sel_step(last - 1, 0)

    consume(last, vs_ref, jnp.where(kpos(last) <= tq, 0.0, MASK_NEG), parts=2)

    for r in range(NSA_REP):
        osel = gated_output(acc_ref[r * ATT_TQ:(r + 1) * ATT_TQ], r, 1)
        o_ref[:, r * HEAD_DIM:(r + 1) * HEAD_DIM] = gate(r, 0) * oc_ref[0, r] + osel + yw_ref[r]


def _nsa_attn(q_r, kx, vs, kw, vw, selb, oc, proj, b, t):
    g = NSA_KV_GROUPS
    nq = t // ATT_TQ
    once = dict(pipeline_mode=pl.Buffered(1))
    vspec = pl.BlockSpec((1, 1, t, LANES), lambda bi, gi, qi: (bi, gi, 0, 0), **once)
    hspec = pl.BlockSpec((1, NSA_REP, ATT_TQ, HEAD_DIM), lambda bi, gi, qi: (bi, gi, qi, 0))
    return pl.pallas_call(
        _nsa_attn_kernel,
        grid=(b, g, nq),
        in_specs=[hspec,
                  pl.BlockSpec((1, 1, t // ATT_TK, KX_WIDTH, ATT_TK), lambda bi, gi, qi: (bi, gi, 0, 0, 0), **once),
                  vspec,
                  pl.BlockSpec((1, 1, t // WIN_TK, HEAD_DIM, WIN_TK), lambda bi, gi, qi: (bi, gi, 0, 0, 0), **once),
                  vspec,
                  pl.BlockSpec((1, 1, ATT_TQ, LANES), lambda bi, gi, qi: (bi, gi, qi, 0)),
                  hspec,
                  pl.BlockSpec((ATT_TQ, LANES), lambda bi, gi, qi: (bi * nq + qi, COL_SMALL // LANES))],
        out_specs=pl.BlockSpec((ATT_TQ, NSA_REP * HEAD_DIM), lambda bi, gi, qi: (bi * nq + qi, gi)),
        out_shape=jax.ShapeDtypeStruct((b * t, NSA_WIDTH), f32),
        scratch_shapes=[pltpu.VMEM((NSA_REP * ATT_TQ, KX_WIDTH), bf16),
                        pltpu.VMEM((NSA_REP * ATT_TQ, ATT_TK), f32),
                        pltpu.VMEM((NSA_REP * ATT_TQ, LANES), f32),
                        pltpu.VMEM((NSA_REP * ATT_TQ, LANES), f32),
                        pltpu.VMEM((NSA_REP, ATT_TQ, HEAD_DIM), f32)],
        compiler_params=_cparams(("parallel", "parallel", "arbitrary")),
        name="nsa_attn",
    )(q_r, kx, vs, kw, vw, selb, oc, proj)


HALO = 8


def _ssd_kernel(xbc_ref, z_ref, small_ref, dtt_ref, cw_ref, cb_ref, dtb_r_ref, dtb_c_ref,
                alog_r_ref, alog_c_ref, dskip_ref, nw_ref, o_ref, ext_ref, st_ref):
    c = pl.program_id(1)
    L = CHUNK

    @pl.when(c == 0)
    def _():
        ext_ref[0:HALO, :] = jnp.zeros((HALO, XBC_WIDTH), f32)
        st_ref[...] = jnp.zeros_like(st_ref)

    ext_ref[HALO:HALO + L, :] = xbc_ref[...]
    conv = cb_ref[...]
    for k in range(CONV_WIDTH):
        off = HALO - (CONV_WIDTH - 1) + k
        conv = conv + cw_ref[k:k + 1, :] * ext_ref[off:off + L, :]
    ext_ref[0:HALO, :] = ext_ref[L:L + HALO, :]
    act = conv * jax.nn.sigmoid(conv)
    xs = act[:, :SSM_WIDTH]
    bm = act[:, SSM_WIDTH:SSM_WIDTH + SSM_GROUPS * SSM_STATE]
    cm = act[:, SSM_WIDTH + SSM_GROUPS * SSM_STATE:]

    dt_c = jax.nn.softplus(small_ref[:, SMALL_DT_OFF:SMALL_DT_OFF + SSM_HEADS] + dtb_r_ref[...])
    dt_r = jax.nn.softplus(dtt_ref[0] + dtb_c_ref[...])
    a_r = -jnp.exp(alog_r_ref[...])
    a_c = -jnp.exp(alog_c_ref[...])
    row = lax.broadcasted_iota(i32, (L, L), 0)
    col = lax.broadcasted_iota(i32, (L, L), 1)
    causal = col <= row
    tri = jnp.where(causal, 1.0, 0.0)
    acs_c = _dot_hi(tri, dt_c * a_r)
    acs_r = _dot_hi(dt_r * a_c, jnp.where(row <= col, 1.0, 0.0))

    z = z_ref[...]
    ys = []
    for g in range(SSM_GROUPS):
        cg = cm[:, g * SSM_STATE:(g + 1) * SSM_STATE].astype(bf16)
        bg = bm[:, g * SSM_STATE:(g + 1) * SSM_STATE]
        cb = _dot_t(cg, bg.astype(bf16))
        bg_t = bg.T
        ssq = jnp.zeros((L, 1), f32)
        yg = []
        for r in range(SSM_REP):
            h = g * SSM_REP + r
            hs = slice(h * SSM_HEADDIM, (h + 1) * SSM_HEADDIM)
            a_col = acs_c[:, h:h + 1]
            a_row = acs_r[h:h + 1, :]
            dt_row = dt_r[h:h + 1, :]
            a_last = acs_r[h:h + 1, L - 1:L]
            seg = a_col - a_row
            decay = jnp.where(causal, jnp.exp(jnp.where(causal, seg, 0.0)), 0.0)
            w = cb * decay * dt_row
            x_h = xs[:, hs]
            xb = x_h.astype(bf16)
            st = st_ref[h]
            y = _dot(w.astype(bf16), xb)
            y = y + _dot(cg, st.astype(bf16)) * jnp.exp(a_col)
            y = y + dskip_ref[:, hs] * x_h
            bscaled = bg_t * (jnp.exp(a_last - a_row) * dt_row)
            st_ref[h] = jnp.exp(a_last) * st + _dot(bscaled.astype(bf16), xb)
            zh = z[:, hs]
            y = y * (zh * jax.nn.sigmoid(zh))
            ssq = ssq + jnp.sum(y * y, axis=-1, keepdims=True)
            yg.append(y)
        rs = lax.rsqrt(ssq / (SSM_REP * SSM_HEADDIM) + NORM_EPS)
        for r in range(SSM_REP):
            h = g * SSM_REP + r
            hs = slice(h * SSM_HEADDIM, (h + 1) * SSM_HEADDIM)
            o_ref[:, hs] = yg[r] * rs * nw_ref[:, hs]


def _ssd(proj, dtt, cw, cb, dtb_r, dtb_c, alog_r, alog_c, dskip, nw, b, t):
    nch = t // CHUNK
    row = lambda bi, ci: bi * nch + ci
    const2 = lambda bi, ci: (0, 0)
    return pl.pallas_call(
        _ssd_kernel,
        grid=(b, nch),
        in_specs=[pl.BlockSpec((CHUNK, XBC_WIDTH), lambda bi, ci: (row(bi, ci), COL_XBC // XBC_WIDTH)),
                  pl.BlockSpec((CHUNK, SSM_WIDTH), lambda bi, ci: (row(bi, ci), COL_Z // SSM_WIDTH)),
                  pl.BlockSpec((CHUNK, LANES), lambda bi, ci: (row(bi, ci), COL_SMALL // LANES)),
                  pl.BlockSpec((1, SSM_HEADS, CHUNK), lambda bi, ci: (bi, 0, ci)),
                  pl.BlockSpec((CONV_WIDTH, XBC_WIDTH), const2),
                  pl.BlockSpec((1, XBC_WIDTH), const2),
                  pl.BlockSpec((1, SSM_HEADS), const2),
                  pl.BlockSpec((SSM_HEADS, 1), const2),
                  pl.BlockSpec((1, SSM_HEADS), const2),
                  pl.BlockSpec((SSM_HEADS, 1), const2),
                  pl.BlockSpec((1, SSM_WIDTH), const2),
                  pl.BlockSpec((1, SSM_WIDTH), const2)],
        out_specs=pl.BlockSpec((CHUNK, SSM_WIDTH), lambda bi, ci: (row(bi, ci), 0)),
        out_shape=jax.ShapeDtypeStruct((b * t, SSM_WIDTH), f32),
        scratch_shapes=[pltpu.VMEM((HALO + CHUNK, XBC_WIDTH), f32),
                        pltpu.VMEM((SSM_HEADS, SSM_STATE, SSM_HEADDIM), f32)],
        compiler_params=_cparams(("parallel", "arbitrary")),
        name="ssd",
    )(proj, proj, proj, dtt, cw, cb, dtb_r, dtb_c, alog_r, alog_c, dskip, nw)


def _layer_norm(v, g, b):
    mu = jnp.mean(v, axis=-1, keepdims=True)
    d = v - mu
    var = jnp.mean(d * d, axis=-1, keepdims=True)
    return d * lax.rsqrt(var + NORM_EPS) * g + b


def _outproj_kernel(alpha, ya_ref, yb_ref, x_ref, wa_ref, wb_ref, g_ref, b_ref, wr_ref, br_ref,
                    h_ref, route_ref, cnt_ref):
    i = pl.program_id(0)
    tm = x_ref.shape[0]
    mix = _dot(ya_ref[...].astype(bf16), wa_ref[...]) + _dot(yb_ref[...].astype(bf16), wb_ref[...])
    h = _layer_norm(alpha * x_ref[...] + mix, g_ref[...], b_ref[...])
    h_ref[...] = h

    h_hi = h.astype(bf16)
    h_lo = (h - h_hi.astype(f32)).astype(bf16)
    t = _dot(h_hi, wr_ref[...])
    logits = t[:, :LANES] + t[:, LANES:] + _dot(h_lo, wr_ref[:, :LANES]) + br_ref[...]
    lane = lax.broadcasted_iota(i32, (tm, LANES), 1)
    ninf = -jnp.inf
    gmask = lane < N_EXPERT_GROUPS
    gl = jnp.where(gmask, logits, ninf)
    ge = jnp.where(gmask, jnp.exp(gl - jnp.max(gl, axis=-1, keepdims=True)), 0.0)
    pg = ge / jnp.sum(ge, axis=-1, keepdims=True)
    g_gate = jnp.max(pg, axis=-1, keepdims=True)
    g_sel = jnp.min(jnp.where(gmask & (pg == g_gate), lane, LANES), axis=-1, keepdims=True)
    lo = N_EXPERT_GROUPS + g_sel * EXPERTS_PER_GROUP
    emask = (lane >= lo) & (lane < lo + EXPERTS_PER_GROUP)
    el = jnp.where(emask, logits, ninf)
    ee = jnp.where(emask, jnp.exp(el - jnp.max(el, axis=-1, keepdims=True)), 0.0)
    pe = ee / jnp.sum(ee, axis=-1, keepdims=True)
    p0 = jnp.max(pe, axis=-1, keepdims=True)
    l0 = jnp.min(jnp.where(emask & (pe == p0), lane, LANES), axis=-1, keepdims=True)
    rest = jnp.where(emask & (lane != l0), pe, ninf)
    p1 = jnp.max(rest, axis=-1, keepdims=True)
    l1 = jnp.min(jnp.where(rest == p1, lane, LANES), axis=-1, keepdims=True)
    psum = p0 + p1
    w0 = g_gate * p0 / psum
    w1 = g_gate * p1 / psum
    e0 = l0 - N_EXPERT_GROUPS
    e1 = l1 - N_EXPERT_GROUPS

    @pl.when(i == 0)
    def _():
        cnt_ref[...] = jnp.zeros_like(cnt_ref)

    oh0 = lane == e0
    oh1 = lane == e1
    both = jnp.where(oh0, 1.0, 0.0) + jnp.where(oh1, 1.0, 0.0)
    r_i = lax.broadcasted_iota(i32, (tm, tm), 0)
    c_i = lax.broadcasted_iota(i32, (tm, tm), 1)
    strict = jnp.where(c_i < r_i, 1.0, 0.0).astype(bf16)
    before = _dot(strict, both.astype(bf16)) + cnt_ref[...]
    rank0 = jnp.sum(jnp.where(oh0, before, 0.0), axis=-1, keepdims=True)
    rank1 = jnp.sum(jnp.where(oh1, before, 0.0), axis=-1, keepdims=True)
    cnt_ref[...] = cnt_ref[...] + jnp.sum(both, axis=0, keepdims=True)

    out = jnp.where(lane == 0, e0.astype(f32), 0.0)
    out = jnp.where(lane == 1, e1.astype(f32), out)
    out = jnp.where(lane == 2, w0, out)
    out = jnp.where(lane == 3, w1, out)
    out = jnp.where(lane == 4, rank0, out)
    out = jnp.where(lane == 5, rank1, out)
    route_ref[...] = out


def _outproj(y_nsa, y_ssm, x2, wa, wb, g, bta, wr, br, alpha):
    n, d = x2.shape
    tm = OUT_TM
    const = lambda i: (0, 0)
    rowb = lambda i: (i, 0)
    return pl.pallas_call(
        functools.partial(_outproj_kernel, alpha),
        grid=(n // tm,),
        in_specs=[pl.BlockSpec((tm, NSA_WIDTH), rowb), pl.BlockSpec((tm, SSM_WIDTH), rowb),
                  pl.BlockSpec((tm, d), rowb),
                  pl.BlockSpec((NSA_WIDTH, d), const), pl.BlockSpec((SSM_WIDTH, d), const),
                  pl.BlockSpec((1, d), const), pl.BlockSpec((1, d), const),
                  pl.BlockSpec((d, 2 * LANES), const), pl.BlockSpec((1, LANES), const)],
        out_specs=[pl.BlockSpec((tm, d), rowb), pl.BlockSpec((tm, LANES), rowb),
                   pl.BlockSpec((1, LANES), const)],
        out_shape=[jax.ShapeDtypeStruct((n, d), f32), jax.ShapeDtypeStruct((n, LANES), f32),
                   jax.ShapeDtypeStruct((1, LANES), f32)],
        compiler_params=_cparams(("arbitrary",)),
        name="outproj",
    )(y_nsa, y_ssm, x2, wa, wb, g, bta, wr, br)


def _row_copy(src_ref, src_row, dst_ref, dst_row, sem):
    return pltpu.make_async_copy(src_ref.at[pl.ds(src_row, 1), :], dst_ref.at[pl.ds(dst_row, 1), :], sem)


def _dest_row(idx_ref, ps_ref, tm, k, r):
    return ps_ref[idx_ref[0, 0, k * tm + r]] + idx_ref[0, 0, (2 + k) * tm + r]


def _dispatch_kernel(idx_ref, ps_ref, zflag_ref, h_ref, xs_ref, zero_ref, tile_ref, sem, lsem, zsem):
    tm = DISP_TM
    nb = zflag_ref.shape[0]
    i = pl.program_id(0)

    @pl.when(i == 0)
    def _():
        zero_ref[...] = jnp.zeros_like(zero_ref)

        def zblock(i):
            rows = pl.ds(pl.multiple_of(i * MOE_TM, MOE_TM), MOE_TM)
            return pltpu.make_async_copy(zero_ref, xs_ref.at[rows, :], zsem)

        def zstart(i, _):
            @pl.when(zflag_ref[i] != 0)
            def _():
                zblock(i).start()
            return 0

        def zwait(i, _):
            @pl.when(zflag_ref[i] != 0)
            def _():
                zblock(i).wait()
            return 0

        lax.fori_loop(0, nb, zstart, 0)
        lax.fori_loop(0, nb, zwait, 0)

    nslot = tile_ref.shape[0]
    nsteps = pl.num_programs(0)
    slot = i % nslot

    def tile_load(step, s):
        rows = pl.ds(pl.multiple_of(step * tm, tm), tm)
        return pltpu.make_async_copy(h_ref.at[rows, :], tile_ref.at[s], lsem.at[s])

    @pl.when(i == 0)
    def _():
        tile_load(0, 0).start()

    tile_load(i, slot).wait()

    @pl.when(i + 1 < nsteps)
    def _():
        tile_load(i + 1, (i + 1) % nslot).start()

    def issue(r, _):
        for k in range(2):
            _row_copy(tile_ref.at[slot], r, xs_ref, _dest_row(idx_ref, ps_ref, tm, k, r), sem.at[slot]).start()
        return 0

    lax.fori_loop(0, tm, issue, 0, unroll=4)

    def wait_rows(s):
        for k in range(2):
            pltpu.make_async_copy(tile_ref.at[s], xs_ref.at[pl.ds(0, tm), :], sem.at[s]).wait()

    @pl.when(i > 0)
    def _():
        wait_rows((i - 1) % nslot)

    @pl.when(i == nsteps - 1)
    def _():
        wait_rows(slot)


def _dispatch(idx_t, pad_starts, zflag, h, p_rows):
    n, d = h.shape
    tm = DISP_TM
    return pl.pallas_call(
        _dispatch_kernel,
        grid=(n // tm,),
        in_specs=[pl.BlockSpec((1, 1, 4 * tm), lambda i: (i, 0, 0), memory_space=pltpu.SMEM),
                  pl.BlockSpec(memory_space=pltpu.SMEM),
                  pl.BlockSpec(memory_space=pltpu.SMEM),
                  pl.BlockSpec(memory_space=pl.ANY)],
        out_specs=pl.BlockSpec(memory_space=pl.ANY),
        out_shape=jax.ShapeDtypeStruct((p_rows, d), f32),
        scratch_shapes=[pltpu.VMEM((MOE_TM, d), f32), pltpu.VMEM((3, tm, d), f32),
                        pltpu.SemaphoreType.DMA((3,)), pltpu.SemaphoreType.DMA((3,)), pltpu.SemaphoreType.DMA(())],
        compiler_params=_cparams(("arbitrary",)),
        name="dispatch",
    )(idx_t, pad_starts, zflag, h)


def _experts_kernel(be_ref, nu_ref, first_ref, next_ref, slot_ref, xs_ref, wg_ref, wu_ref, wd_ref, y_ref,
                    fg_ref, fu_ref, fd_ref, wgb_ref, wub_ref, wdb_ref, sems):
    i = pl.program_id(0)

    def weight_loads(e, s):
        return (pltpu.make_async_copy(wg_ref.at[e], fg_ref.at[s], sems.at[s, 0]),
                pltpu.make_async_copy(wu_ref.at[e], fu_ref.at[s], sems.at[s, 1]),
                pltpu.make_async_copy(wd_ref.at[e], fd_ref.at[s], sems.at[s, 2]))

    @pl.when(i == 0)
    def _():
        for c in weight_loads(be_ref[0], 0):
            c.start()

    @pl.when(first_ref[i] != 0)
    def _():
        s = slot_ref[i]
        for c in weight_loads(be_ref[i], s):
            c.wait()
        wgb_ref[...] = fg_ref[s].astype(bf16)
        wub_ref[...] = fu_ref[s].astype(bf16)
        wdb_ref[...] = fd_ref[s].astype(bf16)

        @pl.when(next_ref[i] >= 0)
        def _():
            for c in weight_loads(next_ref[i], 1 - s):
                c.start()

    @pl.when(i < nu_ref[0])
    def _():
        xb = xs_ref[...].astype(bf16)
        gte = _dot(xb, wgb_ref[...])
        up = _dot(xb, wub_ref[...])
        act = gte * jax.nn.sigmoid(gte) * up
        y_ref[...] = _dot(act.astype(bf16), wdb_ref[...])

    @pl.when(i >= nu_ref[0])
    def _():
        y_ref[...] = jnp.zeros_like(y_ref)


def _experts(block_e, n_used, first, next_e, slot, xs, w_gate, w_up, w_down):
    p_rows, d = xs.shape
    tm = MOE_TM
    nb = p_rows // tm
    de = w_gate.shape[-1]
    xmap = lambda i, be, nu, fi, ne, sl: (jnp.maximum(jnp.minimum(i, nu[0] - 1), 0), 0)
    hbm = pl.BlockSpec(memory_space=pl.ANY)
    return pl.pallas_call(
        _experts_kernel,
        grid_spec=pltpu.PrefetchScalarGridSpec(
            num_scalar_prefetch=5,
            grid=(nb,),
            in_specs=[pl.BlockSpec((tm, d), xmap), hbm, hbm, hbm],
            out_specs=pl.BlockSpec((tm, d), lambda i, be, nu, fi, ne, sl: (i, 0)),
            scratch_shapes=[pltpu.VMEM((2, d, de), f32), pltpu.VMEM((2, d, de), f32), pltpu.VMEM((2, de, d), f32),
                            pltpu.VMEM((d, de), bf16), pltpu.VMEM((d, de), bf16), pltpu.VMEM((de, d), bf16),
                            pltpu.SemaphoreType.DMA((2, 3))],
        ),
        out_shape=jax.ShapeDtypeStruct((p_rows, d), f32),
        compiler_params=_cparams(("arbitrary",)),
        name="experts",
    )(block_e, n_used, first, next_e, slot, xs, w_gate, w_up, w_down)


def _combine_kernel(alpha, idx_ref, nidx_ref, ps_ref, route_ref, h_ref, g_ref, b_ref, y_ref, o_ref, buf_ref, sems):
    tm = h_ref.shape[0]
    i = pl.program_id(0)
    slot = i % 2

    def gather(ids_ref, s):
        def issue(r, _):
            for k in range(2):
                _row_copy(y_ref, _dest_row(ids_ref, ps_ref, tm, k, r), buf_ref.at[s, k], r, sems.at[s]).start()
            return 0
        lax.fori_loop(0, tm, issue, 0, unroll=4)

    @pl.when(i == 0)
    def _():
        gather(idx_ref, 0)

    @pl.when(i + 1 < pl.num_programs(0))
    def _():
        gather(nidx_ref, 1 - slot)

    for k in range(2):
        pltpu.make_async_copy(y_ref.at[pl.ds(0, tm), :], buf_ref.at[slot, k], sems.at[slot]).wait()

    route = route_ref[...]
    ffn = route[:, 2:3] * buf_ref[slot, 0] + route[:, 3:4] * buf_ref[slot, 1]
    o_ref[...] = _layer_norm(alpha * h_ref[...] + ffn, g_ref[...], b_ref[...])


def _combine(idx_t, pad_starts, route, h, g, bta, y, alpha):
    n, d = h.shape
    tm = COMB_TM
    const = lambda i: (0, 0)
    return pl.pallas_call(
        functools.partial(_combine_kernel, alpha),
        grid=(n // tm,),
        in_specs=[pl.BlockSpec((1, 1, 4 * tm), lambda i: (i, 0, 0), memory_space=pltpu.SMEM),
                  pl.BlockSpec((1, 1, 4 * tm), lambda i: (jnp.minimum(i + 1, n // tm - 1), 0, 0),
                               memory_space=pltpu.SMEM),
                  pl.BlockSpec(memory_space=pltpu.SMEM),
                  pl.BlockSpec((tm, LANES), lambda i: (i, 0)),
                  pl.BlockSpec((tm, d), lambda i: (i, 0)),
                  pl.BlockSpec((1, d), const), pl.BlockSpec((1, d), const),
                  pl.BlockSpec(memory_space=pl.ANY)],
        out_specs=pl.BlockSpec((tm, d), lambda i: (i, 0)),
        out_shape=jax.ShapeDtypeStruct((n, d), f32),
        scratch_shapes=[pltpu.VMEM((2, 2, tm, d), f32), pltpu.SemaphoreType.DMA((2,))],
        compiler_params=_cparams(("arbitrary",)),
        name="combine",
    )(idx_t, idx_t, pad_starts, route, h, g, bta, y)


def _tile_idx(idx, tm):
    n = idx.shape[1]
    return idx.reshape(4, n // tm, tm).transpose(1, 0, 2).reshape(n // tm, 1, 4 * tm)


def _layer(x, positions, w_in, cmp_k_pe, cmp_k_w1, cmp_k_b1, cmp_k_w2, cmp_v_pe, cmp_v_w1, cmp_v_b1, cmp_v_w2,
           conv_w, conv_b, dt_bias, a_log, d_skip, ssm_norm_w, w_out, ln1_g, ln1_b,
           w_router_group, b_router_group, w_router_expert, b_router_expert, w_gate, w_up, w_down, ln2_g, ln2_b,
           alpha):
    b, t, d = x.shape
    n = b * t
    assert t % max(ATT_TK, ATT_TQ, CMP_TQ, CHUNK, PREP_TM) == 0 and n % max(PROJ_TM, OUT_TM, DISP_TM) == 0
    assert t // SEL_BLOCK <= LANES and (t // CMP_STRIDE) % LANES == 0 and t >= WINDOW + ATT_TQ
    x2 = x.reshape(n, d)

    c0 = NSA_WIDTH
    c1 = c0 + 6 * KV_WIDTH
    c2 = c1 + 3 * NSA_HEADS
    c3 = c2 + SSM_WIDTH
    c4 = c3 + XBC_WIDTH
    w_small = jnp.concatenate([w_in[:, c1:c2], w_in[:, c4:], jnp.zeros((d, LANES - 3 * NSA_HEADS - SSM_HEADS), f32)], axis=1)
    w_cat = jnp.concatenate([w_in[:, :c0], w_in[:, c2:c3], w_in[:, c3:c4], w_in[:, c0:c1], w_small,
                             jnp.zeros((d, PROJ_COLS - COL_SMALL - LANES), f32)], axis=1).astype(bf16)
    proj = _proj(x2, w_cat)

    lane = np.arange(LANES) % HEAD_DIM
    inv_freq = ROPE_THETA ** (-jnp.arange(0, ROT_DIM, 2, dtype=f32) / ROT_DIM)
    invf = jnp.where(lane < ROT_DIM, inv_freq[lane % (ROT_DIM // 2)], 0.0).astype(f32)[None, :]
    pos128 = jnp.broadcast_to(positions.reshape(n, 1), (n, LANES))
    q_r, k_cmp, v_cmp, k_sel, v_sel, k_win, v_win = _nsa_prep(proj, pos128, invf, b, t)

    nc = t // CMP_STRIDE
    half_w = CMP_STRIDE * HEAD_DIM
    a = jnp.stack([k_cmp, v_cmp]).reshape(2, b * NSA_KV_GROUPS, nc, half_w)
    pe = jnp.stack([cmp_k_pe, cmp_v_pe]).reshape(2, 2, 1, half_w)
    w1 = jnp.stack([cmp_k_w1, cmp_v_w1]).reshape(2, 2, half_w, CMP_HIDDEN).astype(bf16)
    b1 = jnp.stack([cmp_k_b1, cmp_v_b1]).reshape(2, 1, CMP_HIDDEN)
    w2 = jnp.pad(jnp.stack([cmp_k_w2, cmp_v_w2]), ((0, 0), (0, 0), (0, LANES - HEAD_DIM))).astype(bf16)
    cend = jnp.minimum(jnp.arange(nc) * CMP_STRIDE + CMP_BLOCK - 1, t - 1)
    posc = jnp.broadcast_to(positions[:, cend][:, :, None], (b, nc, LANES))
    kvc, kvc_t = _cmp_mlp(a, pe, w1, b1, w2, posc, invf, b)

    c_start = np.arange(nc)[:, None] * CMP_STRIDE
    s_start = np.arange(LANES)[None, :] * SEL_BLOCK
    cover = ((c_start < s_start + SEL_BLOCK) & (c_start + CMP_BLOCK > s_start)
             & (np.arange(nc)[:, None] < nc - 1) & (np.arange(LANES)[None, :] < t // SEL_BLOCK))
    cover = jnp.asarray(cover, bf16)
    o_cmp, selb = _cmp_attn(q_r, kvc, kvc_t, cover, b, t)

    y_nsa = _nsa_attn(q_r, k_sel, v_sel, k_win, v_win, selb, o_cmp, proj, b, t)

    dt_raw = proj[:, COL_SMALL + SMALL_DT_OFF:COL_SMALL + SMALL_DT_OFF + SSM_HEADS]
    dtt = dt_raw.reshape(b, t, SSM_HEADS).transpose(0, 2, 1)
    y_ssm = _ssd(proj, dtt, conv_w.reshape(CONV_WIDTH, XBC_WIDTH), conv_b.reshape(1, XBC_WIDTH),
                 dt_bias.reshape(1, SSM_HEADS), dt_bias.reshape(SSM_HEADS, 1),
                 a_log.reshape(1, SSM_HEADS), a_log.reshape(SSM_HEADS, 1),
                 jnp.repeat(d_skip, SSM_HEADDIM).reshape(1, SSM_WIDTH), ssm_norm_w.reshape(1, SSM_WIDTH), b, t)

    wr = jnp.concatenate([w_router_group, w_router_expert,
                          jnp.zeros((d, LANES - N_EXPERT_GROUPS - N_EXPERTS), f32)], axis=1)
    br = jnp.concatenate([b_router_group, b_router_expert,
                          jnp.zeros((LANES - N_EXPERT_GROUPS - N_EXPERTS,), f32)])[None, :]
    wr_hi = wr.astype(bf16)
    wr = jnp.concatenate([wr_hi, (wr - wr_hi.astype(f32)).astype(bf16)], axis=1)
    wo = w_out.astype(bf16)
    h, route, counts = _outproj(y_nsa, y_ssm, x2, wo[:NSA_WIDTH], wo[NSA_WIDTH:], ln1_g[None, :], ln1_b[None, :],
                                wr, br, alpha)

    cnt = counts[0, :N_EXPERTS].astype(i32)
    padded = (cnt + MOE_TM - 1) // MOE_TM * MOE_TM
    pad_ends = jnp.cumsum(padded)
    pad_starts = pad_ends - padded
    idx = jnp.concatenate([route[:, 0:2], route[:, 4:6]], axis=1).astype(i32).T
    p_rows = 2 * n + N_EXPERTS * MOE_TM
    nb = p_rows // MOE_TM
    block_e = jnp.minimum(jnp.sum(jnp.arange(nb, dtype=i32)[:, None] * MOE_TM >= pad_ends[None, :], axis=-1),
                          N_EXPERTS - 1).astype(i32)
    n_used = (pad_ends[-1] // MOE_TM).astype(i32).reshape(1)
    blk = jnp.arange(nb, dtype=i32)
    last_of_expert = jnp.any((blk[:, None] + 1) * MOE_TM == pad_ends[None, :], axis=-1)
    zflag = (last_of_expert | (blk >= n_used[0])).astype(i32)
    block_e = jnp.where(blk < n_used[0], block_e, block_e[jnp.maximum(n_used[0] - 1, 0)])

    xs = _dispatch(_tile_idx(idx, DISP_TM), pad_starts, zflag, h, p_rows)
    prev_e = jnp.concatenate([jnp.full((1,), -1, i32), block_e[:-1]])
    first = ((block_e != prev_e) & (blk < n_used[0])).astype(i32)
    eidx = jnp.arange(N_EXPERTS, dtype=i32)
    later = jnp.where((eidx[None, :] > eidx[:, None]) & (padded[None, :] > 0), eidx[None, :], N_EXPERTS)
    next_of = jnp.min(later, axis=1)
    next_e = jnp.where(next_of[block_e] < N_EXPERTS, next_of[block_e], -1).astype(i32)
    slot = ((jnp.cumsum(first) - 1) % 2).astype(i32)
    y = _experts(block_e, n_used, first, next_e, slot, xs, w_gate, w_up, w_down)
    out = _combine(_tile_idx(idx, COMB_TM), pad_starts, route, h, ln2_g[None, :], ln2_b[None, :], y, alpha)
    return out.reshape(b, t, d)


def kernel(x, positions, w_in, cmp_k_pe, cmp_k_w1, cmp_k_b1, cmp_k_w2, cmp_v_pe, cmp_v_w1, cmp_v_b1, cmp_v_w2, conv_w, conv_b, dt_bias, a_log, d_skip, ssm_norm_w, w_out, ln1_g, ln1_b, w_router_group, b_router_group, w_router_expert, b_router_expert, w_gate, w_up, w_down, ln2_g, ln2_b):
    depth = w_in.shape[0]
    alpha = (2 * depth) ** 0.25
    params = (w_in, cmp_k_pe, cmp_k_w1, cmp_k_b1, cmp_k_w2, cmp_v_pe, cmp_v_w1, cmp_v_b1, cmp_v_w2, conv_w, conv_b,
              dt_bias, a_log, d_skip, ssm_norm_w, w_out, ln1_g, ln1_b, w_router_group, b_router_group,
              w_router_expert, b_router_expert, w_gate, w_up, w_down, ln2_g, ln2_b)
    for l in range(depth):
        x = _layer(x, positions, *[p[l] for p in params], alpha)
    return x
```

```python
import functools
import math

import jax
import jax.numpy as jnp
import numpy as np
from jax import lax
from jax.experimental import pallas as pl
from jax.experimental.pallas import tpu as pltpu

f32 = jnp.float32
bf16 = jnp.bfloat16
i32 = jnp.int32

HEAD_DIM = 64
NSA_HEADS = 16
NSA_KV_GROUPS = 2
NSA_REP = NSA_HEADS // NSA_KV_GROUPS
NSA_WIDTH = NSA_HEADS * HEAD_DIM
KV_WIDTH = NSA_KV_GROUPS * HEAD_DIM
CMP_BLOCK = 32
CMP_STRIDE = 16
CMP_HIDDEN = 256
SEL_BLOCK = 64
SEL_TOPK = 16
WINDOW = 512
FORCED_SCORE = 1.0e4
SSM_HEADDIM = 64
SSM_HEADS = 16
SSM_WIDTH = SSM_HEADS * SSM_HEADDIM
SSM_GROUPS = 4
SSM_REP = SSM_HEADS // SSM_GROUPS
SSM_STATE = 128
CONV_WIDTH = 4
CHUNK = 256
XBC_WIDTH = SSM_WIDTH + 2 * SSM_GROUPS * SSM_STATE
ROPE_THETA = 500000.0
ROT_DIM = HEAD_DIM // 4
N_EXPERT_GROUPS = 4
EXPERTS_PER_GROUP = 8
N_EXPERTS = N_EXPERT_GROUPS * EXPERTS_PER_GROUP
D_EXPERT = 512
NORM_EPS = 1e-5

LANES = 128
MASK_NEG = -1.0e30

PROJ_TM = 1024
PROJ_TN = 1024
PREP_TM = 512
CMP_TQ = 256
ATT_TQ = 256
ATT_TK = 512
WIN_TK = 256
WIN_HEADS = 4
KX_WIDTH = 2 * LANES
OUT_TM = 512
MOE_TM = 256
DISP_TM = 256
COMB_TM = 128
VMEM_LIMIT = 56 * 1024 * 1024

COL_Q = 0
COL_Z = NSA_WIDTH
COL_XBC = COL_Z + SSM_WIDTH
COL_KV = COL_XBC + XBC_WIDTH
COL_SMALL = COL_KV + 6 * KV_WIDTH
PROJ_COLS = 5120
SMALL_DT_OFF = 3 * NSA_HEADS


def _cparams(sem, vmem=VMEM_LIMIT):
    return pltpu.CompilerParams(dimension_semantics=sem, vmem_limit_bytes=vmem)


def _dot(a, b):
    return jnp.dot(a, b, preferred_element_type=f32)


def _dot_t(a, b):
    return lax.dot_general(a, b, (((1,), (1,)), ((), ())), preferred_element_type=f32)


def _dot_hi(a, b):
    return jnp.dot(a, b, preferred_element_type=f32, precision=lax.Precision.HIGHEST)


def _proj_kernel(x_ref, w_ref, o_ref, xb_ref):
    @pl.when(pl.program_id(1) == 0)
    def _():
        xb_ref[...] = x_ref[...].astype(bf16)

    o_ref[...] = _dot(xb_ref[...], w_ref[...])


def _proj(x2, w_cat):
    n, d = x2.shape
    cols = w_cat.shape[1]
    tm = min(PROJ_TM, n)
    return pl.pallas_call(
        _proj_kernel,
        grid=(n // tm, cols // PROJ_TN),
        in_specs=[pl.BlockSpec((tm, d), lambda i, j: (i, 0)),
                  pl.BlockSpec((d, PROJ_TN), lambda i, j: (0, j))],
        out_specs=pl.BlockSpec((tm, PROJ_TN), lambda i, j: (i, j)),
        out_shape=jax.ShapeDtypeStruct((n, cols), f32),
        scratch_shapes=[pltpu.VMEM((tm, d), bf16)],
        compiler_params=_cparams(("parallel", "arbitrary")),
        name="proj",
    )(x2, w_cat)


def _rope_tables(pos_i32, invf):
    ang = pos_i32.astype(f32) * invf
    return jnp.cos(ang), jnp.sin(ang)


def _rope128(x, cos, sin):
    half = ROT_DIM // 2
    d = lax.broadcasted_iota(i32, x.shape, 1) % HEAD_DIM
    up = pltpu.roll(x, LANES - half, 1)
    dn = pltpu.roll(x, half, 1)
    rot = jnp.where(d < half, -up, dn)
    return x * cos + rot * sin


def _nsa_prep_kernel(pos_ref, invf_ref, q_ref, kc_ref, vc_ref, ks_ref, vs_ref, kw_ref, vw_ref,
                     qo_ref, kco_ref, vco_ref, kso_ref, vso_ref, kwo_ref, vwo_ref):
    cos, sin = _rope_tables(pos_ref[...], invf_ref[...])
    scale = HEAD_DIM ** -0.5
    for c in range(NSA_WIDTH // LANES):
        t = _rope128(q_ref[:, c * LANES:(c + 1) * LANES], cos, sin) * scale
        qo_ref[0, 2 * c] = t[:, :HEAD_DIM].astype(bf16)
        qo_ref[0, 2 * c + 1] = t[:, HEAD_DIM:].astype(bf16)

    def split(src, dst, rope, dt):
        t = src[...]
        if rope:
            t = _rope128(t, cos, sin)
        for g in range(NSA_KV_GROUPS):
            dst[0, g] = t[:, g * HEAD_DIM:(g + 1) * HEAD_DIM].astype(dt)

    split(kc_ref, kco_ref, False, f32)
    split(vc_ref, vco_ref, False, f32)

    tm = pos_ref.shape[0]

    def values_with_ones(src, dst):
        t = src[...]
        tail = jnp.ones((tm, HEAD_DIM), bf16)
        for g in range(NSA_KV_GROUPS):
            vg = t[:, g * HEAD_DIM:(g + 1) * HEAD_DIM].astype(bf16)
            dst[0, g] = jnp.concatenate([vg, tail], axis=1)

    values_with_ones(vs_ref, vso_ref)
    values_with_ones(vw_ref, vwo_ref)

    kw_t = _rope128(kw_ref[...], cos, sin).T
    ks_t = _rope128(ks_ref[...], cos, sin).T
    for g in range(NSA_KV_GROUPS):
        for c in range(tm // WIN_TK):
            kwo_ref[0, g, c] = kw_t[g * HEAD_DIM:(g + 1) * HEAD_DIM, c * WIN_TK:(c + 1) * WIN_TK].astype(bf16)
    blk = lax.broadcasted_iota(i32, (LANES, tm), 0)
    tok = pl.program_id(1) * tm + lax.broadcasted_iota(i32, (LANES, tm), 1)
    onehot_t = jnp.where(tok // SEL_BLOCK == blk, 1.0, 0.0).astype(bf16)
    for g in range(NSA_KV_GROUPS):
        kso_ref[0, g, 0, 0:LANES, :] = onehot_t
        kg = ks_t[g * HEAD_DIM:(g + 1) * HEAD_DIM, :].astype(bf16)
        kso_ref[0, g, 0, LANES:2 * LANES, :] = jnp.concatenate([kg, jnp.zeros_like(kg)], axis=0)


def _nsa_prep(proj, pos128, invf, b, t):
    tm = PREP_TM
    assert tm == ATT_TK
    nt = t // tm
    row = lambda bi, ti: (bi * nt + ti, 0)
    kv0 = COL_KV // LANES
    in_specs = [pl.BlockSpec((tm, LANES), row),
                pl.BlockSpec((1, LANES), lambda bi, ti: (0, 0)),
                pl.BlockSpec((tm, NSA_WIDTH), lambda bi, ti: (bi * nt + ti, COL_Q // NSA_WIDTH))]
    for k in range(6):
        in_specs.append(pl.BlockSpec((tm, LANES), functools.partial(lambda bi, ti, k: (bi * nt + ti, kv0 + k), k=k)))
    head = lambda bi, ti: (bi, 0, ti, 0)
    tile5 = lambda bi, ti: (bi, 0, ti, 0, 0)
    g = NSA_KV_GROUPS
    out_specs = [pl.BlockSpec((1, NSA_HEADS, tm, HEAD_DIM), head),
                 pl.BlockSpec((1, g, tm, HEAD_DIM), head), pl.BlockSpec((1, g, tm, HEAD_DIM), head),
                 pl.BlockSpec((1, g, 1, KX_WIDTH, tm), tile5), pl.BlockSpec((1, g, tm, LANES), head),
                 pl.BlockSpec((1, g, tm // WIN_TK, HEAD_DIM, WIN_TK), tile5), pl.BlockSpec((1, g, tm, LANES), head)]
    out_shape = [jax.ShapeDtypeStruct((b, NSA_HEADS, t, HEAD_DIM), bf16),
                 jax.ShapeDtypeStruct((b, g, t, HEAD_DIM), f32), jax.ShapeDtypeStruct((b, g, t, HEAD_DIM), f32),
                 jax.ShapeDtypeStruct((b, g, nt, KX_WIDTH, tm), bf16), jax.ShapeDtypeStruct((b, g, t, LANES), bf16),
                 jax.ShapeDtypeStruct((b, g, t // WIN_TK, HEAD_DIM, WIN_TK), bf16),
                 jax.ShapeDtypeStruct((b, g, t, LANES), bf16)]
    return pl.pallas_call(
        _nsa_prep_kernel,
        grid=(b, nt),
        in_specs=in_specs,
        out_specs=out_specs,
        out_shape=out_shape,
        compiler_params=_cparams(("parallel", "parallel")),
        name="nsa_prep",
    )(pos128, invf, proj, proj, proj, proj, proj, proj, proj)


def _cmp_mlp_kernel(a_ref, pe_ref, w1_ref, b1_ref, w2_ref, pos_ref, invf_ref, o_ref, ot_ref):
    kind = pl.program_id(0)
    a = a_ref[0, 0]
    nc = a.shape[0]
    u = _dot((a + pe_ref[0, 0]).astype(bf16), w1_ref[0, 0])
    v = _dot((a + pe_ref[0, 1]).astype(bf16), w1_ref[0, 1])
    v_next = pltpu.roll(v, nc - 1, 0)
    hid = jax.nn.gelu(u + v_next + b1_ref[0])
    out = _dot(hid.astype(bf16), w2_ref[0])
    cos, sin = _rope_tables(pos_ref[0], invf_ref[...])
    roped = _rope128(out, cos, sin)
    out = jnp.where(kind == 0, roped, out)
    o_ref[0, 0] = out[:, :HEAD_DIM].astype(bf16)
    ot_ref[0, 0] = out.T[:HEAD_DIM, :].astype(bf16)


def _cmp_mlp(a, pe, w1, b1, w2, posc, invf, b):
    _, bg, nc, hw = a.shape
    g = bg // b
    return pl.pallas_call(
        _cmp_mlp_kernel,
        grid=(2, bg),
        in_specs=[pl.BlockSpec((1, 1, nc, hw), lambda k, i: (k, i, 0, 0)),
                  pl.BlockSpec((1, 2, 1, hw), lambda k, i: (k, 0, 0, 0)),
                  pl.BlockSpec((1, 2, hw, CMP_HIDDEN), lambda k, i: (k, 0, 0, 0)),
                  pl.BlockSpec((1, 1, CMP_HIDDEN), lambda k, i: (k, 0, 0)),
                  pl.BlockSpec((1, CMP_HIDDEN, LANES), lambda k, i: (k, 0, 0)),
                  pl.BlockSpec((1, nc, LANES), lambda k, i: (i // g, 0, 0)),
                  pl.BlockSpec((1, LANES), lambda k, i: (0, 0))],
        out_specs=[pl.BlockSpec((1, 1, nc, HEAD_DIM), lambda k, i: (k, i, 0, 0)),
                   pl.BlockSpec((1, 1, HEAD_DIM, nc), lambda k, i: (k, i, 0, 0))],
        out_shape=[jax.ShapeDtypeStruct((2, bg, nc, HEAD_DIM), bf16),
                   jax.ShapeDtypeStruct((2, bg, HEAD_DIM, nc), bf16)],
        compiler_params=_cparams(("parallel", "parallel")),
        name="cmp_mlp",
    )(a, pe, w1, b1, w2, posc, invf)


def _cmp_attn_kernel(q_ref, kct_ref, vc_ref, cover_ref, oc_ref, sel_ref):
    qi = pl.program_id(2)
    nc = vc_ref.shape[2]
    rows = NSA_REP * CMP_TQ
    tq = qi * CMP_TQ + lax.broadcasted_iota(i32, (CMP_TQ, 1), 0)
    row_live = jnp.where(tq >= CMP_BLOCK - 1, 1.0, 0.0)
    tiny = jnp.finfo(f32).tiny

    def attend(ncols):
        kct = kct_ref[0, 0, :, :ncols]
        vc = vc_ref[0, 0, :ncols, :]
        cend = lax.broadcasted_iota(i32, (1, ncols), 1) * CMP_STRIDE + (CMP_BLOCK - 1)
        bias = jnp.where(cend <= tq, 0.0, MASK_NEG)
        s = _dot(q_ref[0].reshape(rows, HEAD_DIM), kct).reshape(NSA_REP, CMP_TQ, ncols) + bias[None]
        e = jnp.exp(s - jnp.max(s, axis=-1, keepdims=True))
        live = row_live[None]
        p = e * (live / jnp.maximum(live * jnp.sum(e, axis=-1, keepdims=True), tiny))
        oc_ref[0] = _dot(p.reshape(rows, ncols).astype(bf16), vc).reshape(NSA_REP, CMP_TQ, HEAD_DIM)
        psum = jnp.sum(p, axis=0)
        hi = psum.astype(bf16)
        lo = (psum - hi.astype(f32)).astype(bf16)
        cover = cover_ref[:ncols, :]
        imp = _dot(hi, cover) + _dot(lo, cover)

        j = lax.broadcasted_iota(i32, (CMP_TQ, LANES), 1)
        cur = tq // SEL_BLOCK
        forced = (j == 0) | (j == cur) | (j == cur - 1)
        valid = j * SEL_BLOCK <= tq
        imp = jnp.where(valid, jnp.where(forced, FORCED_SCORE, imp), -FORCED_SCORE)

        nblk = ncols * CMP_STRIDE // SEL_BLOCK
        jt = lax.broadcasted_iota(i32, (nblk, CMP_TQ), 0)

        def pick(_, carry):
            work, sel = carry
            m = jnp.max(work, axis=0, keepdims=True)
            first = jnp.min(jnp.where(work == m, jt, LANES), axis=0, keepdims=True)
            hit = jt == first
            return jnp.where(hit, -jnp.inf, work), jnp.where(hit, 1.0, sel)

        _, sel_t = lax.fori_loop(0, SEL_TOPK, pick, (imp.T[:nblk], jnp.zeros((nblk, CMP_TQ), f32)))
        if nblk < LANES:
            sel_t = jnp.concatenate([sel_t, jnp.zeros((LANES - nblk, CMP_TQ), f32)], axis=0)
        sel_ref[0, 0] = jnp.where(valid, jnp.where(sel_t.T > 0.0, 0.0, MASK_NEG), MASK_NEG).astype(bf16)

    need = (qi + 1) * (CMP_TQ // CMP_STRIDE)
    for ncols in range(LANES, nc + 1, LANES):
        @pl.when((need > ncols - LANES) & (need <= ncols))
        def _():
            attend(ncols)


def _cmp_attn(q_r, kvc, kvc_t, cover, b, t):
    g = NSA_KV_GROUPS
    nc = kvc.shape[2]
    nq = t // CMP_TQ
    return pl.pallas_call(
        _cmp_attn_kernel,
        grid=(b, g, nq),
        in_specs=[pl.BlockSpec((1, NSA_REP, CMP_TQ, HEAD_DIM), lambda bi, gi, qi: (bi, gi, qi, 0)),
                  pl.BlockSpec((1, 1, HEAD_DIM, nc), lambda bi, gi, qi: (0, bi * g + gi, 0, 0)),
                  pl.BlockSpec((1, 1, nc, HEAD_DIM), lambda bi, gi, qi: (1, bi * g + gi, 0, 0)),
                  pl.BlockSpec((nc, LANES), lambda bi, gi, qi: (0, 0))],
        out_specs=[pl.BlockSpec((1, NSA_REP, CMP_TQ, HEAD_DIM), lambda bi, gi, qi: (bi, gi, qi, 0)),
                   pl.BlockSpec((1, 1, CMP_TQ, LANES), lambda bi, gi, qi: (bi, gi, qi, 0))],
        out_shape=[jax.ShapeDtypeStruct((b, NSA_HEADS, t, HEAD_DIM), f32),
                   jax.ShapeDtypeStruct((b, g, t, LANES), bf16)],
        compiler_params=_cparams(("parallel", "parallel", "parallel")),
        name="cmp_attn",
    )(q_r, kvc_t, kvc, cover)


def _nsa_attn_kernel(q_ref, kx_ref, vs_ref, kw_ref, vw_ref, sel_ref, oc_ref, gate_ref,
                     o_ref, qx_ref, s_ref, m_ref, acc_ref, yw_ref):
    gi = pl.program_id(1)
    qi = pl.program_id(2)
    start = qi * ATT_TQ
    tq = start + lax.broadcasted_iota(i32, (ATT_TQ, 1), 0)

    selb = sel_ref[0, 0]
    for r in range(NSA_REP):
        qx_ref[r * ATT_TQ:(r + 1) * ATT_TQ, 0:LANES] = selb
        qr = q_ref[0, r]
        qx_ref[r * ATT_TQ:(r + 1) * ATT_TQ, LANES:2 * LANES] = jnp.concatenate([qr, jnp.zeros_like(qr)], axis=1)
    rows = NSA_REP * ATT_TQ

    m_ref[...] = jnp.full(m_ref.shape, MASK_NEG, f32)
    acc_ref[...] = jnp.zeros(acc_ref.shape, f32)

    def consume(kt, v_ref, bias, parts=1):
        k0 = pl.multiple_of(kt * ATT_TK, ATT_TK)
        v = v_ref[0, 0, pl.ds(k0, ATT_TK), :]
        heads = NSA_REP // parts
        for part in range(parts):
            rs = slice(part * heads * ATT_TQ, (part + 1) * heads * ATT_TQ)
            s = s_ref[rs]
            if bias is not None:
                s = (s.reshape(heads, ATT_TQ, ATT_TK) + bias[None]).reshape(heads * ATT_TQ, ATT_TK)
            m_prev = m_ref[rs]
            m_new = jnp.maximum(m_prev, jnp.max(s, axis=-1, keepdims=True))
            alpha = jnp.exp(m_prev - m_new)
            p = jnp.exp(s - jnp.concatenate([m_new] * (ATT_TK // LANES), axis=1))
            acc_ref[rs] = alpha * acc_ref[rs] + _dot(p.astype(bf16), v)
            m_ref[rs] = m_new

    def kpos(kt):
        return kt * ATT_TK + lax.broadcasted_iota(i32, (1, ATT_TK), 1)

    last = start // ATT_TK
    for part in range(2):
        rs = slice(part * rows // 2, (part + 1) * rows // 2)
        s_ref[rs] = _dot(qx_ref[rs], kx_ref[0, 0, 0])

    def sel_step(kt, _):
        consume(kt, vs_ref, None)
        s_ref[...] = _dot(qx_ref[...], kx_ref[0, 0, kt + 1])
        return 0

    n_win = (WINDOW + ATT_TQ) // WIN_TK
    w_first = jnp.maximum(start // WIN_TK - WINDOW // WIN_TK, 0)
    kw = jnp.concatenate([kw_ref[0, 0, w_first + i] for i in range(n_win)], axis=1)
    w0 = pl.multiple_of(w_first * WIN_TK, WIN_TK)
    vw = vw_ref[0, 0, pl.ds(w0, n_win * WIN_TK), :]
    kp = w0 + lax.broadcasted_iota(i32, (1, n_win * WIN_TK), 1)
    wbias = jnp.where(kp <= tq, jnp.where(kp > tq - WINDOW, 0.0, MASK_NEG), MASK_NEG)
    hh = WIN_HEADS
    gates = jax.nn.sigmoid(gate_ref[...])
    per_group = 3 * NSA_REP
    shifted = gates
    for gg in range(1, NSA_KV_GROUPS):
        shifted = jnp.where(gi == gg, pltpu.roll(gates, LANES - gg * per_group, 1), shifted)

    def gate(r, br):
        c = r * 3 + br
        return jnp.broadcast_to(shifted[:, c:c + 1], (ATT_TQ, HEAD_DIM))

    upper = pltpu.roll(shifted, HEAD_DIM, 1)[:, HEAD_DIM:]

    def gated_output(a, r, br):
        c = r * 3 + br
        ratio = upper / a[:, HEAD_DIM:]
        return a[:, :HEAD_DIM] * jnp.broadcast_to(ratio[:, c:c + 1], (ATT_TQ, HEAD_DIM))

    for half in range(NSA_REP // hh):
        qh = q_ref[0, half * hh:(half + 1) * hh].reshape(hh * ATT_TQ, HEAD_DIM)
        sw = _dot(qh, kw).reshape(hh, ATT_TQ, n_win * WIN_TK) + wbias[None]
        sw = sw.reshape(hh * ATT_TQ, n_win * WIN_TK)
        pw = jnp.exp(sw - jnp.max(sw, axis=-1, keepdims=True))
        aw = _dot(pw.astype(bf16), vw)
        for i in range(hh):
            r = half * hh + i
            a = aw[i * ATT_TQ:(i + 1) * ATT_TQ]
            yw_ref[r] = gated_output(a, r, 2)

    def sel_pair(j, _):
        sel_step(2 * j, 0)
        sel_step(2 * j + 1, 0)
        return 0

    lax.fori_loop(0, last // 2, sel_pair, 0)

    @pl.when(last % 2 == 1)
    def _():
        sel_step(last - 1, 0)

    consume(last, vs_ref, jnp.where(kpos(last) <= tq, 0.0, MASK_NEG), parts=2)

    for r in range(NSA_REP):
        osel = gated_output(acc_ref[r * ATT_TQ:(r + 1) * ATT_TQ], r, 1)
        o_ref[:, r * HEAD_DIM:(r + 1) * HEAD_DIM] = gate(r, 0) * oc_ref[0, r] + osel + yw_ref[r]


def _nsa_attn(q_r, kx, vs, kw, vw, selb, oc, proj, b, t):
    g = NSA_KV_GROUPS
    nq = t // ATT_TQ
    once = dict(pipeline_mode=pl.Buffered(1))
    vspec = pl.BlockSpec((1, 1, t, LANES), lambda bi, gi, qi: (bi, gi, 0, 0), **once)
    hspec = pl.BlockSpec((1, NSA_REP, ATT_TQ, HEAD_DIM), lambda bi, gi, qi: (bi, gi, qi, 0))
    return pl.pallas_call(
        _nsa_attn_kernel,
        grid=(b, g, nq),
        in_specs=[hspec,
                  pl.BlockSpec((1, 1, t // ATT_TK, KX_WIDTH, ATT_TK), lambda bi, gi, qi: (bi, gi, 0, 0, 0), **once),
                  vspec,
                  pl.BlockSpec((1, 1, t // WIN_TK, HEAD_DIM, WIN_TK), lambda bi, gi, qi: (bi, gi, 0, 0, 0), **once),
                  vspec,
                  pl.BlockSpec((1, 1, ATT_TQ, LANES), lambda bi, gi, qi: (bi, gi, qi, 0)),
                  hspec,
                  pl.BlockSpec((ATT_TQ, LANES), lambda bi, gi, qi: (bi * nq + qi, COL_SMALL // LANES))],
        out_specs=pl.BlockSpec((ATT_TQ, NSA_REP * HEAD_DIM), lambda bi, gi, qi: (bi * nq + qi, gi)),
        out_shape=jax.ShapeDtypeStruct((b * t, NSA_WIDTH), f32),
        scratch_shapes=[pltpu.VMEM((NSA_REP * ATT_TQ, KX_WIDTH), bf16),
                        pltpu.VMEM((NSA_REP * ATT_TQ, ATT_TK), f32),
                        pltpu.VMEM((NSA_REP * ATT_TQ, LANES), f32),
                        pltpu.VMEM((NSA_REP * ATT_TQ, LANES), f32),
                        pltpu.VMEM((NSA_REP, ATT_TQ, HEAD_DIM), f32)],
        compiler_params=_cparams(("parallel", "parallel", "arbitrary")),
        name="nsa_attn",
    )(q_r, kx, vs, kw, vw, selb, oc, proj)


HALO = 8


def _ssd_kernel(xbc_ref, z_ref, small_ref, dtt_ref, cw_ref, cb_ref, dtb_r_ref, dtb_c_ref,
                alog_r_ref, alog_c_ref, dskip_ref, nw_ref, o_ref, ext_ref, st_ref):
    c = pl.program_id(1)
    L = CHUNK

    @pl.when(c == 0)
    def _():
        ext_ref[0:HALO, :] = jnp.zeros((HALO, XBC_WIDTH), f32)
        st_ref[...] = jnp.zeros_like(st_ref)

    ext_ref[HALO:HALO + L, :] = xbc_ref[...]
    conv = cb_ref[...]
    for k in range(CONV_WIDTH):
        off = HALO - (CONV_WIDTH - 1) + k
        conv = conv + cw_ref[k:k + 1, :] * ext_ref[off:off + L, :]
    ext_ref[0:HALO, :] = ext_ref[L:L + HALO, :]
    act = conv * jax.nn.sigmoid(conv)
    xs = act[:, :SSM_WIDTH]
    bm = act[:, SSM_WIDTH:SSM_WIDTH + SSM_GROUPS * SSM_STATE]
    cm = act[:, SSM_WIDTH + SSM_GROUPS * SSM_STATE:]

    dt_c = jax.nn.softplus(small_ref[:, SMALL_DT_OFF:SMALL_DT_OFF + SSM_HEADS] + dtb_r_ref[...])
    dt_r = jax.nn.softplus(dtt_ref[0] + dtb_c_ref[...])
    a_r = -jnp.exp(alog_r_ref[...])
    a_c = -jnp.exp(alog_c_ref[...])
    row = lax.broadcasted_iota(i32, (L, L), 0)
    col = lax.broadcasted_iota(i32, (L, L), 1)
    causal = col <= row
    tri = jnp.where(causal, 1.0, 0.0)
    acs_c = _dot_hi(tri, dt_c * a_r)
    acs_r = _dot_hi(dt_r * a_c, jnp.where(row <= col, 1.0, 0.0))

    z = z_ref[...]
    ys = []
    for g in range(SSM_GROUPS):
        cg = cm[:, g * SSM_STATE:(g + 1) * SSM_STATE].astype(bf16)
        bg = bm[:, g * SSM_STATE:(g + 1) * SSM_STATE]
        cb = _dot_t(cg, bg.astype(bf16))
        bg_t = bg.T
        ssq = jnp.zeros((L, 1), f32)
        yg = []
        for r in range(SSM_REP):
            h = g * SSM_REP + r
            hs = slice(h * SSM_HEADDIM, (h + 1) * SSM_HEADDIM)
            a_col = acs_c[:, h:h + 1]
            a_row = acs_r[h:h + 1, :]
            dt_row = dt_r[h:h + 1, :]
            a_last = acs_r[h:h + 1, L - 1:L]
            seg = a_col - a_row
            decay = jnp.where(causal, jnp.exp(jnp.where(causal, seg, 0.0)), 0.0)
            w = cb * decay * dt_row
            x_h = xs[:, hs]
            xb = x_h.astype(bf16)
            st = st_ref[h]
            y = _dot(w.astype(bf16), xb)
            y = y + _dot(cg, st.astype(bf16)) * jnp.exp(a_col)
            y = y + dskip_ref[:, hs] * x_h
            bscaled = bg_t * (jnp.exp(a_last - a_row) * dt_row)
            st_ref[h] = jnp.exp(a_last) * st + _dot(bscaled.astype(bf16), xb)
            zh = z[:, hs]
            y = y * (zh * jax.nn.sigmoid(zh))
            ssq = ssq + jnp.sum(y * y, axis=-1, keepdims=True)
            yg.append(y)
        rs = lax.rsqrt(ssq / (SSM_REP * SSM_HEADDIM) + NORM_EPS)
        for r in range(SSM_REP):
            h = g * SSM_REP + r
            hs = slice(h * SSM_HEADDIM, (h + 1) * SSM_HEADDIM)
            o_ref[:, hs] = yg[r] * rs * nw_ref[:, hs]


def _ssd(proj, dtt, cw, cb, dtb_r, dtb_c, alog_r, alog_c, dskip, nw, b, t):
    nch = t // CHUNK
    row = lambda bi, ci: bi * nch + ci
    const2 = lambda bi, ci: (0, 0)
    return pl.pallas_call(
        _ssd_kernel,
        grid=(b, nch),
        in_specs=[pl.BlockSpec((CHUNK, XBC_WIDTH), lambda bi, ci: (row(bi, ci), COL_XBC // XBC_WIDTH)),
                  pl.BlockSpec((CHUNK, SSM_WIDTH), lambda bi, ci: (row(bi, ci), COL_Z // SSM_WIDTH)),
                  pl.BlockSpec((CHUNK, LANES), lambda bi, ci: (row(bi, ci), COL_SMALL // LANES)),
                  pl.BlockSpec((1, SSM_HEADS, CHUNK), lambda bi, ci: (bi, 0, ci)),
                  pl.BlockSpec((CONV_WIDTH, XBC_WIDTH), const2),
                  pl.BlockSpec((1, XBC_WIDTH), const2),
                  pl.BlockSpec((1, SSM_HEADS), const2),
                  pl.BlockSpec((SSM_HEADS, 1), const2),
                  pl.BlockSpec((1, SSM_HEADS), const2),
                  pl.BlockSpec((SSM_HEADS, 1), const2),
                  pl.BlockSpec((1, SSM_WIDTH), const2),
                  pl.BlockSpec((1, SSM_WIDTH), const2)],
        out_specs=pl.BlockSpec((CHUNK, SSM_WIDTH), lambda bi, ci: (row(bi, ci), 0)),
        out_shape=jax.ShapeDtypeStruct((b * t, SSM_WIDTH), f32),
        scratch_shapes=[pltpu.VMEM((HALO + CHUNK, XBC_WIDTH), f32),
                        pltpu.VMEM((SSM_HEADS, SSM_STATE, SSM_HEADDIM), f32)],
        compiler_params=_cparams(("parallel", "arbitrary")),
        name="ssd",
    )(proj, proj, proj, dtt, cw, cb, dtb_r, dtb_c, alog_r, alog_c, dskip, nw)


def _layer_norm(v, g, b):
    mu = jnp.mean(v, axis=-1, keepdims=True)
    d = v - mu
    var = jnp.mean(d * d, axis=-1, keepdims=True)
    return d * lax.rsqrt(var + NORM_EPS) * g + b


def _outproj_kernel(alpha, ya_ref, yb_ref, x_ref, wa_ref, wb_ref, g_ref, b_ref, wr_ref, br_ref,
                    h_ref, route_ref, cnt_ref):
    i = pl.program_id(0)
    tm = x_ref.shape[0]
    mix = _dot(ya_ref[...].astype(bf16), wa_ref[...]) + _dot(yb_ref[...].astype(bf16), wb_ref[...])
    h = _layer_norm(alpha * x_ref[...] + mix, g_ref[...], b_ref[...])
    h_ref[...] = h

    h_hi = h.astype(bf16)
    h_lo = (h - h_hi.astype(f32)).astype(bf16)
    t = _dot(h_hi, wr_ref[...])
    logits = t[:, :LANES] + t[:, LANES:] + _dot(h_lo, wr_ref[:, :LANES]) + br_ref[...]
    lane = lax.broadcasted_iota(i32, (tm, LANES), 1)
    ninf = -jnp.inf
    gmask = lane < N_EXPERT_GROUPS
    gl = jnp.where(gmask, logits, ninf)
    ge = jnp.where(gmask, jnp.exp(gl - jnp.max(gl, axis=-1, keepdims=True)), 0.0)
    pg = ge / jnp.sum(ge, axis=-1, keepdims=True)
    g_gate = jnp.max(pg, axis=-1, keepdims=True)
    g_sel = jnp.min(jnp.where(gmask & (pg == g_gate), lane, LANES), axis=-1, keepdims=True)
    lo = N_EXPERT_GROUPS + g_sel * EXPERTS_PER_GROUP
    emask = (lane >= lo) & (lane < lo + EXPERTS_PER_GROUP)
    el = jnp.where(emask, logits, ninf)
    ee = jnp.where(emask, jnp.exp(el - jnp.max(el, axis=-1, keepdims=True)), 0.0)
    pe = ee / jnp.sum(ee, axis=-1, keepdims=True)
    p0 = jnp.max(pe, axis=-1, keepdims=True)
    l0 = jnp.min(jnp.where(emask & (pe == p0), lane, LANES), axis=-1, keepdims=True)
    rest = jnp.where(emask & (lane != l0), pe, ninf)
    p1 = jnp.max(rest, axis=-1, keepdims=True)
    l1 = jnp.min(jnp.where(rest == p1, lane, LANES), axis=-1, keepdims=True)
    psum = p0 + p1
    w0 = g_gate * p0 / psum
    w1 = g_gate * p1 / psum
    e0 = l0 - N_EXPERT_GROUPS
    e1 = l1 - N_EXPERT_GROUPS

    @pl.when(i == 0)
    def _():
        cnt_ref[...] = jnp.zeros_like(cnt_ref)

    oh0 = lane == e0
    oh1 = lane == e1
    both = jnp.where(oh0, 1.0, 0.0) + jnp.where(oh1, 1.0, 0.0)
    r_i = lax.broadcasted_iota(i32, (tm, tm), 0)
    c_i = lax.broadcasted_iota(i32, (tm, tm), 1)
    strict = jnp.where(c_i < r_i, 1.0, 0.0).astype(bf16)
    before = _dot(strict, both.astype(bf16)) + cnt_ref[...]
    rank0 = jnp.sum(jnp.where(oh0, before, 0.0), axis=-1, keepdims=True)
    rank1 = jnp.sum(jnp.where(oh1, before, 0.0), axis=-1, keepdims=True)
    cnt_ref[...] = cnt_ref[...] + jnp.sum(both, axis=0, keepdims=True)

    out = jnp.where(lane == 0, e0.astype(f32), 0.0)
    out = jnp.where(lane == 1, e1.astype(f32), out)
    out = jnp.where(lane == 2, w0, out)
    out = jnp.where(lane == 3, w1, out)
    out = jnp.where(lane == 4, rank0, out)
    out = jnp.where(lane == 5, rank1, out)
    route_ref[...] = out


def _outproj(y_nsa, y_ssm, x2, wa, wb, g, bta, wr, br, alpha):
    n, d = x2.shape
    tm = OUT_TM
    const = lambda i: (0, 0)
    rowb = lambda i: (i, 0)
    return pl.pallas_call(
        functools.partial(_outproj_kernel, alpha),
        grid=(n // tm,),
        in_specs=[pl.BlockSpec((tm, NSA_WIDTH), rowb), pl.BlockSpec((tm, SSM_WIDTH), rowb),
                  pl.BlockSpec((tm, d), rowb),
                  pl.BlockSpec((NSA_WIDTH, d), const), pl.BlockSpec((SSM_WIDTH, d), const),
                  pl.BlockSpec((1, d), const), pl.BlockSpec((1, d), const),
                  pl.BlockSpec((d, 2 * LANES), const), pl.BlockSpec((1, LANES), const)],
        out_specs=[pl.BlockSpec((tm, d), rowb), pl.BlockSpec((tm, LANES), rowb),
                   pl.BlockSpec((1, LANES), const)],
        out_shape=[jax.ShapeDtypeStruct((n, d), f32), jax.ShapeDtypeStruct((n, LANES), f32),
                   jax.ShapeDtypeStruct((1, LANES), f32)],
        compiler_params=_cparams(("arbitrary",)),
        name="outproj",
    )(y_nsa, y_ssm, x2, wa, wb, g, bta, wr, br)


def _row_copy(src_ref, src_row, dst_ref, dst_row, sem):
    return pltpu.make_async_copy(src_ref.at[pl.ds(src_row, 1), :], dst_ref.at[pl.ds(dst_row, 1), :], sem)


def _dest_row(dest_ref, tm, k, r):
    return dest_ref[0, 0, k * tm + r]


def _dispatch_kernel(dest_ref, zflag_ref, h_ref, xs_ref, zero_ref, tile_ref, sem, lsem, zsem):
    tm = DISP_TM
    nb = zflag_ref.shape[0]
    i = pl.program_id(0)

    @pl.when(i == 0)
    def _():
        zero_ref[...] = jnp.zeros_like(zero_ref)

        def zblock(i):
            rows = pl.ds(pl.multiple_of(i * MOE_TM, MOE_TM), MOE_TM)
            return pltpu.make_async_copy(zero_ref, xs_ref.at[rows, :], zsem)

        def zstart(i, _):
            @pl.when(zflag_ref[i] != 0)
            def _():
                zblock(i).start()
            return 0

        def zwait(i, _):
            @pl.when(zflag_ref[i] != 0)
            def _():
                zblock(i).wait()
            return 0

        lax.fori_loop(0, nb, zstart, 0)
        lax.fori_loop(0, nb, zwait, 0)

    nslot = tile_ref.shape[0]
    nsteps = pl.num_programs(0)
    slot = i % nslot

    def tile_load(step, s):
        rows = pl.ds(pl.multiple_of(step * tm, tm), tm)
        return pltpu.make_async_copy(h_ref.at[rows, :], tile_ref.at[s], lsem.at[s])

    @pl.when(i == 0)
    def _():
        tile_load(0, 0).start()

    tile_load(i, slot).wait()

    @pl.when(i + 1 < nsteps)
    def _():
        tile_load(i + 1, (i + 1) % nslot).start()

    def issue(r, _):
        for k in range(2):
            _row_copy(tile_ref.at[slot], r, xs_ref, _dest_row(dest_ref, tm, k, r), sem.at[slot]).start()
        return 0

    lax.fori_loop(0, tm, issue, 0, unroll=16)

    def wait_rows(s):
        for k in range(2):
            pltpu.make_async_copy(tile_ref.at[s], xs_ref.at[pl.ds(0, tm), :], sem.at[s]).wait()

    @pl.when(i > 0)
    def _():
        wait_rows((i - 1) % nslot)

    @pl.when(i == nsteps - 1)
    def _():
        wait_rows(slot)


def _dispatch(dest_t, zflag, h, p_rows):
    n, d = h.shape
    tm = DISP_TM
    return pl.pallas_call(
        _dispatch_kernel,
        grid=(n // tm,),
        in_specs=[pl.BlockSpec((1, 1, 2 * tm), lambda i: (i, 0, 0), memory_space=pltpu.SMEM),
                  pl.BlockSpec(memory_space=pltpu.SMEM),
                  pl.BlockSpec(memory_space=pl.ANY)],
        out_specs=pl.BlockSpec(memory_space=pl.ANY),
        out_shape=jax.ShapeDtypeStruct((p_rows, d), f32),
        scratch_shapes=[pltpu.VMEM((MOE_TM, d), f32), pltpu.VMEM((3, tm, d), f32),
                        pltpu.SemaphoreType.DMA((3,)), pltpu.SemaphoreType.DMA((3,)), pltpu.SemaphoreType.DMA(())],
        compiler_params=_cparams(("arbitrary",)),
        name="dispatch",
    )(dest_t, zflag, h)


def _experts_kernel(be_ref, nu_ref, first_ref, next_ref, slot_ref, xs_ref, wg_ref, wu_ref, wd_ref, y_ref,
                    fg_ref, fu_ref, fd_ref, wgb_ref, wub_ref, wdb_ref, sems):
    i = pl.program_id(0)

    def weight_loads(e, s):
        return (pltpu.make_async_copy(wg_ref.at[e], fg_ref.at[s], sems.at[s, 0]),
                pltpu.make_async_copy(wu_ref.at[e], fu_ref.at[s], sems.at[s, 1]),
                pltpu.make_async_copy(wd_ref.at[e], fd_ref.at[s], sems.at[s, 2]))

    @pl.when(i == 0)
    def _():
        for c in weight_loads(be_ref[0], 0):
            c.start()

    @pl.when(first_ref[i] != 0)
    def _():
        s = slot_ref[i]
        for c in weight_loads(be_ref[i], s):
            c.wait()
        wgb_ref[...] = fg_ref[s].astype(bf16)
        wub_ref[...] = fu_ref[s].astype(bf16)
        wdb_ref[...] = fd_ref[s].astype(bf16)

        @pl.when(next_ref[i] >= 0)
        def _():
            for c in weight_loads(next_ref[i], 1 - s):
                c.start()

    @pl.when(i < nu_ref[0])
    def _():
        xb = xs_ref[...].astype(bf16)
        gte = _dot(xb, wgb_ref[...])
        up = _dot(xb, wub_ref[...])
        act = gte * jax.nn.sigmoid(gte) * up
        y_ref[...] = _dot(act.astype(bf16), wdb_ref[...])

    @pl.when(i >= nu_ref[0])
    def _():
        y_ref[...] = jnp.zeros_like(y_ref)


def _experts(block_e, n_used, first, next_e, slot, xs, w_gate, w_up, w_down):
    p_rows, d = xs.shape
    tm = MOE_TM
    nb = p_rows // tm
    de = w_gate.shape[-1]
    xmap = lambda i, be, nu, fi, ne, sl: (jnp.maximum(jnp.minimum(i, nu[0] - 1), 0), 0)
    hbm = pl.BlockSpec(memory_space=pl.ANY)
    return pl.pallas_call(
        _experts_kernel,
        grid_spec=pltpu.PrefetchScalarGridSpec(
            num_scalar_prefetch=5,
            grid=(nb,),
            in_specs=[pl.BlockSpec((tm, d), xmap), hbm, hbm, hbm],
            out_specs=pl.BlockSpec((tm, d), lambda i, be, nu, fi, ne, sl: (i, 0)),
            scratch_shapes=[pltpu.VMEM((2, d, de), f32), pltpu.VMEM((2, d, de), f32), pltpu.VMEM((2, de, d), f32),
                            pltpu.VMEM((d, de), bf16), pltpu.VMEM((d, de), bf16), pltpu.VMEM((de, d), bf16),
                            pltpu.SemaphoreType.DMA((2, 3))],
        ),
        out_shape=jax.ShapeDtypeStruct((p_rows, d), f32),
        compiler_params=_cparams(("arbitrary",)),
        name="experts",
    )(block_e, n_used, first, next_e, slot, xs, w_gate, w_up, w_down)


def _combine_kernel(alpha, dest_ref, ndest_ref, route_ref, h_ref, g_ref, b_ref, y_ref, o_ref, buf_ref, sems):
    tm = h_ref.shape[0]
    i = pl.program_id(0)
    slot = i % 2

    def start_row(ids_ref, s, r):
        for k in range(2):
            _row_copy(y_ref, _dest_row(ids_ref, tm, k, r), buf_ref.at[s, k], r, sems.at[s]).start()

    def wait_tile():
        for k in range(2):
            pltpu.make_async_copy(y_ref.at[pl.ds(0, tm), :], buf_ref.at[slot, k], sems.at[slot]).wait()

    def finish_tile():
        route = route_ref[...]
        ffn = route[:, 2:3] * buf_ref[slot, 0] + route[:, 3:4] * buf_ref[slot, 1]
        o_ref[...] = _layer_norm(alpha * h_ref[...] + ffn, g_ref[...], b_ref[...])

    @pl.when(i == 0)
    def _():
        def issue(r, _):
            start_row(dest_ref, 0, r)
            return 0
        lax.fori_loop(0, tm, issue, 0, unroll=4)

    @pl.when(i + 1 < pl.num_programs(0))
    def _():
        wait_tile()
        for r in range(tm):
            start_row(ndest_ref, 1 - slot, r)
        finish_tile()

    @pl.when(i + 1 == pl.num_programs(0))
    def _():
        wait_tile()
        finish_tile()


def _combine(dest_t, route, h, g, bta, y, alpha):
    n, d = h.shape
    tm = COMB_TM
    const = lambda i: (0, 0)
    return pl.pallas_call(
        functools.partial(_combine_kernel, alpha),
        grid=(n // tm,),
        in_specs=[pl.BlockSpec((1, 1, 2 * tm), lambda i: (i, 0, 0), memory_space=pltpu.SMEM),
                  pl.BlockSpec((1, 1, 2 * tm), lambda i: (jnp.minimum(i + 1, n // tm - 1), 0, 0),
                               memory_space=pltpu.SMEM),
                  pl.BlockSpec((tm, LANES), lambda i: (i, 0)),
                  pl.BlockSpec((tm, d), lambda i: (i, 0)),
                  pl.BlockSpec((1, d), const), pl.BlockSpec((1, d), const),
                  pl.BlockSpec(memory_space=pl.ANY)],
        out_specs=pl.BlockSpec((tm, d), lambda i: (i, 0)),
        out_shape=jax.ShapeDtypeStruct((n, d), f32),
        scratch_shapes=[pltpu.VMEM((2, 2, tm, d), f32), pltpu.SemaphoreType.DMA((2,))],
        compiler_params=_cparams(("arbitrary",)),
        name="combine",
    )(dest_t, dest_t, route, h, g, bta, y)


def _tile_dest(dest, tm):
    n = dest.shape[1]
    return dest.reshape(2, n // tm, tm).transpose(1, 0, 2).reshape(n // tm, 1, 2 * tm)


def _layer(x, positions, w_in, cmp_k_pe, cmp_k_w1, cmp_k_b1, cmp_k_w2, cmp_v_pe, cmp_v_w1, cmp_v_b1, cmp_v_w2,
           conv_w, conv_b, dt_bias, a_log, d_skip, ssm_norm_w, w_out, ln1_g, ln1_b,
           w_router_group, b_router_group, w_router_expert, b_router_expert, w_gate, w_up, w_down, ln2_g, ln2_b,
           alpha):
    b, t, d = x.shape
    n = b * t
    assert t % max(ATT_TK, ATT_TQ, CMP_TQ, CHUNK, PREP_TM) == 0 and n % max(PROJ_TM, OUT_TM, DISP_TM) == 0
    assert t // SEL_BLOCK <= LANES and (t // CMP_STRIDE) % LANES == 0 and t >= WINDOW + ATT_TQ
    x2 = x.reshape(n, d)

    c0 = NSA_WIDTH
    c1 = c0 + 6 * KV_WIDTH
    c2 = c1 + 3 * NSA_HEADS
    c3 = c2 + SSM_WIDTH
    c4 = c3 + XBC_WIDTH
    w_small = jnp.concatenate([w_in[:, c1:c2], w_in[:, c4:], jnp.zeros((d, LANES - 3 * NSA_HEADS - SSM_HEADS), f32)], axis=1)
    w_cat = jnp.concatenate([w_in[:, :c0], w_in[:, c2:c3], w_in[:, c3:c4], w_in[:, c0:c1], w_small,
                             jnp.zeros((d, PROJ_COLS - COL_SMALL - LANES), f32)], axis=1).astype(bf16)
    proj = _proj(x2, w_cat)

    lane = np.arange(LANES) % HEAD_DIM
    inv_freq = ROPE_THETA ** (-jnp.arange(0, ROT_DIM, 2, dtype=f32) / ROT_DIM)
    invf = jnp.where(lane < ROT_DIM, inv_freq[lane % (ROT_DIM // 2)], 0.0).astype(f32)[None, :]
    pos128 = jnp.broadcast_to(positions.reshape(n, 1), (n, LANES))
    q_r, k_cmp, v_cmp, k_sel, v_sel, k_win, v_win = _nsa_prep(proj, pos128, invf, b, t)

    nc = t // CMP_STRIDE
    half_w = CMP_STRIDE * HEAD_DIM
    a = jnp.stack([k_cmp, v_cmp]).reshape(2, b * NSA_KV_GROUPS, nc, half_w)
    pe = jnp.stack([cmp_k_pe, cmp_v_pe]).reshape(2, 2, 1, half_w)
    w1 = jnp.stack([cmp_k_w1, cmp_v_w1]).reshape(2, 2, half_w, CMP_HIDDEN).astype(bf16)
    b1 = jnp.stack([cmp_k_b1, cmp_v_b1]).reshape(2, 1, CMP_HIDDEN)
    w2 = jnp.pad(jnp.stack([cmp_k_w2, cmp_v_w2]), ((0, 0), (0, 0), (0, LANES - HEAD_DIM))).astype(bf16)
    cend = jnp.minimum(jnp.arange(nc) * CMP_STRIDE + CMP_BLOCK - 1, t - 1)
    posc = jnp.broadcast_to(positions[:, cend][:, :, None], (b, nc, LANES))
    kvc, kvc_t = _cmp_mlp(a, pe, w1, b1, w2, posc, invf, b)

    c_start = np.arange(nc)[:, None] * CMP_STRIDE
    s_start = np.arange(LANES)[None, :] * SEL_BLOCK
    cover = ((c_start < s_start + SEL_BLOCK) & (c_start + CMP_BLOCK > s_start)
             & (np.arange(nc)[:, None] < nc - 1) & (np.arange(LANES)[None, :] < t // SEL_BLOCK))
    cover = jnp.asarray(cover, bf16)
    o_cmp, selb = _cmp_attn(q_r, kvc, kvc_t, cover, b, t)

    y_nsa = _nsa_attn(q_r, k_sel, v_sel, k_win, v_win, selb, o_cmp, proj, b, t)

    dt_raw = proj[:, COL_SMALL + SMALL_DT_OFF:COL_SMALL + SMALL_DT_OFF + SSM_HEADS]
    dtt = dt_raw.reshape(b, t, SSM_HEADS).transpose(0, 2, 1)
    y_ssm = _ssd(proj, dtt, conv_w.reshape(CONV_WIDTH, XBC_WIDTH), conv_b.reshape(1, XBC_WIDTH),
                 dt_bias.reshape(1, SSM_HEADS), dt_bias.reshape(SSM_HEADS, 1),
                 a_log.reshape(1, SSM_HEADS), a_log.reshape(SSM_HEADS, 1),
                 jnp.repeat(d_skip, SSM_HEADDIM).reshape(1, SSM_WIDTH), ssm_norm_w.reshape(1, SSM_WIDTH), b, t)

    wr = jnp.concatenate([w_router_group, w_router_expert,
                          jnp.zeros((d, LANES - N_EXPERT_GROUPS - N_EXPERTS), f32)], axis=1)
    br = jnp.concatenate([b_router_group, b_router_expert,
                          jnp.zeros((LANES - N_EXPERT_GROUPS - N_EXPERTS,), f32)])[None, :]
    wr_hi = wr.astype(bf16)
    wr = jnp.concatenate([wr_hi, (wr - wr_hi.astype(f32)).astype(bf16)], axis=1)
    wo = w_out.astype(bf16)
    h, route, counts = _outproj(y_nsa, y_ssm, x2, wo[:NSA_WIDTH], wo[NSA_WIDTH:], ln1_g[None, :], ln1_b[None, :],
                                wr, br, alpha)

    cnt = counts[0, :N_EXPERTS].astype(i32)
    padded = (cnt + MOE_TM - 1) // MOE_TM * MOE_TM
    pad_ends = jnp.cumsum(padded)
    pad_starts = pad_ends - padded
    e01 = route[:, 0:2].astype(i32).T
    start01 = jnp.sum(jnp.where(e01[..., None] == jnp.arange(N_EXPERTS, dtype=i32), pad_starts, 0), axis=-1)
    dest = start01 + route[:, 4:6].astype(i32).T
    p_rows = 2 * n + N_EXPERTS * MOE_TM
    nb = p_rows // MOE_TM
    block_e = jnp.minimum(jnp.sum(jnp.arange(nb, dtype=i32)[:, None] * MOE_TM >= pad_ends[None, :], axis=-1),
                          N_EXPERTS - 1).astype(i32)
    n_used = (pad_ends[-1] // MOE_TM).astype(i32).reshape(1)
    blk = jnp.arange(nb, dtype=i32)
    last_of_expert = jnp.any((blk[:, None] + 1) * MOE_TM == pad_ends[None, :], axis=-1)
    zflag = (last_of_expert | (blk >= n_used[0])).astype(i32)
    block_e = jnp.where(blk < n_used[0], block_e, block_e[jnp.maximum(n_used[0] - 1, 0)])

    xs = _dispatch(_tile_dest(dest, DISP_TM), zflag, h, p_rows)
    prev_e = jnp.concatenate([jnp.full((1,), -1, i32), block_e[:-1]])
    first = ((block_e != prev_e) & (blk < n_used[0])).astype(i32)
    eidx = jnp.arange(N_EXPERTS, dtype=i32)
    later = jnp.where((eidx[None, :] > eidx[:, None]) & (padded[None, :] > 0), eidx[None, :], N_EXPERTS)
    next_of = jnp.min(later, axis=1)
    next_e = jnp.where(next_of[block_e] < N_EXPERTS, next_of[block_e], -1).astype(i32)
    slot = ((jnp.cumsum(first) - 1) % 2).astype(i32)
    y = _experts(block_e, n_used, first, next_e, slot, xs, w_gate, w_up, w_down)
    out = _combine(_tile_dest(dest, COMB_TM), route, h, ln2_g[None, :], ln2_b[None, :], y, alpha)
    return out.reshape(b, t, d)


def kernel(x, positions, w_in, cmp_k_pe, cmp_k_w1, cmp_k_b1, cmp_k_w2, cmp_v_pe, cmp_v_w1, cmp_v_b1, cmp_v_w2, conv_w, conv_b, dt_bias, a_log, d_skip, ssm_norm_w, w_out, ln1_g, ln1_b, w_router_group, b_router_group, w_router_expert, b_router_expert, w_gate, w_up, w_down, ln2_g, ln2_b):
    depth = w_in.shape[0]
    alpha = (2 * depth) ** 0.25
    params = (w_in, cmp_k_pe, cmp_k_w1, cmp_k_b1, cmp_k_w2, cmp_v_pe, cmp_v_w1, cmp_v_b1, cmp_v_w2, conv_w, conv_b,
              dt_bias, a_log, d_skip, ssm_norm_w, w_out, ln1_g, ln1_b, w_router_group, b_router_group,
              w_router_expert, b_router_expert, w_gate, w_up, w_down, ln2_g, ln2_b)
    for l in range(depth):
        x = _layer(x, positions, *[p[l] for p in params], alpha)
    return x
```

```python
import functools
import math

import jax
import jax.numpy as jnp
import numpy as np
from jax import lax
from jax.experimental import pallas as pl
from jax.experimental.pallas import tpu as pltpu

f32 = jnp.float32
bf16 = jnp.bfloat16
i32 = jnp.int32

HEAD_DIM = 64
NSA_HEADS = 16
NSA_KV_GROUPS = 2
NSA_REP = NSA_HEADS // NSA_KV_GROUPS
NSA_WIDTH = NSA_HEADS * HEAD_DIM
KV_WIDTH = NSA_KV_GROUPS * HEAD_DIM
CMP_BLOCK = 32
CMP_STRIDE = 16
CMP_HIDDEN = 256
SEL_BLOCK = 64
SEL_TOPK = 16
WINDOW = 512
FORCED_SCORE = 1.0e4
SSM_HEADDIM = 64
SSM_HEADS = 16
SSM_WIDTH = SSM_HEADS * SSM_HEADDIM
SSM_GROUPS = 4
SSM_REP = SSM_HEADS // SSM_GROUPS
SSM_STATE = 128
CONV_WIDTH = 4
CHUNK = 256
XBC_WIDTH = SSM_WIDTH + 2 * SSM_GROUPS * SSM_STATE
ROPE_THETA = 500000.0
ROT_DIM = HEAD_DIM // 4
N_EXPERT_GROUPS = 4
EXPERTS_PER_GROUP = 8
N_EXPERTS = N_EXPERT_GROUPS * EXPERTS_PER_GROUP
D_EXPERT = 512
NORM_EPS = 1e-5

LANES = 128
MASK_NEG = -1.0e30

PROJ_TM = 1024
PROJ_TN = 1024
PREP_TM = 512
CMP_TQ = 256
ATT_TQ = 256
ATT_TK = 512
WIN_TK = 256
WIN_HEADS = 4
KX_WIDTH = 2 * LANES
OUT_TM = 512
MOE_TM = 256
DISP_TM = 256
COMB_TM = 128
VMEM_LIMIT = 56 * 1024 * 1024

COL_Q = 0
COL_Z = NSA_WIDTH
COL_XBC = COL_Z + SSM_WIDTH
COL_KV = COL_XBC + XBC_WIDTH
COL_SMALL = COL_KV + 6 * KV_WIDTH
PROJ_COLS = 5120
SMALL_DT_OFF = 3 * NSA_HEADS


def _cparams(sem, vmem=VMEM_LIMIT):
    return pltpu.CompilerParams(dimension_semantics=sem, vmem_limit_bytes=vmem)


def _dot(a, b):
    return jnp.dot(a, b, preferred_element_type=f32)


def _dot_t(a, b):
    return lax.dot_general(a, b, (((1,), (1,)), ((), ())), preferred_element_type=f32)


def _dot_hi(a, b):
    return jnp.dot(a, b, preferred_element_type=f32, precision=lax.Precision.HIGHEST)


def _proj_kernel(x_ref, w_ref, o_ref, xb_ref):
    @pl.when(pl.program_id(1) == 0)
    def _():
        xb_ref[...] = x_ref[...].astype(bf16)

    o_ref[...] = _dot(xb_ref[...], w_ref[...])


def _proj(x2, w_cat):
    n, d = x2.shape
    cols = w_cat.shape[1]
    tm = min(PROJ_TM, n)
    return pl.pallas_call(
        _proj_kernel,
        grid=(n // tm, cols // PROJ_TN),
        in_specs=[pl.BlockSpec((tm, d), lambda i, j: (i, 0)),
                  pl.BlockSpec((d, PROJ_TN), lambda i, j: (0, j))],
        out_specs=pl.BlockSpec((tm, PROJ_TN), lambda i, j: (i, j)),
        out_shape=jax.ShapeDtypeStruct((n, cols), f32),
        scratch_shapes=[pltpu.VMEM((tm, d), bf16)],
        compiler_params=_cparams(("parallel", "arbitrary")),
        name="proj",
    )(x2, w_cat)


def _rope_tables(pos_i32, invf):
    ang = pos_i32.astype(f32) * invf
    return jnp.cos(ang), jnp.sin(ang)


def _rope128(x, cos, sin):
    half = ROT_DIM // 2
    d = lax.broadcasted_iota(i32, x.shape, 1) % HEAD_DIM
    up = pltpu.roll(x, LANES - half, 1)
    dn = pltpu.roll(x, half, 1)
    rot = jnp.where(d < half, -up, dn)
    return x * cos + rot * sin


def _nsa_prep_kernel(pos_ref, invf_ref, q_ref, kc_ref, vc_ref, ks_ref, vs_ref, kw_ref, vw_ref,
                     qo_ref, kco_ref, vco_ref, kso_ref, vso_ref, kwo_ref, vwo_ref):
    cos, sin = _rope_tables(pos_ref[...], invf_ref[...])
    scale = HEAD_DIM ** -0.5
    for c in range(NSA_WIDTH // LANES):
        t = _rope128(q_ref[:, c * LANES:(c + 1) * LANES], cos, sin) * scale
        qo_ref[0, 2 * c] = t[:, :HEAD_DIM].astype(bf16)
        qo_ref[0, 2 * c + 1] = t[:, HEAD_DIM:].astype(bf16)

    def split(src, dst, rope, dt):
        t = src[...]
        if rope:
            t = _rope128(t, cos, sin)
        for g in range(NSA_KV_GROUPS):
            dst[0, g] = t[:, g * HEAD_DIM:(g + 1) * HEAD_DIM].astype(dt)

    split(kc_ref, kco_ref, False, f32)
    split(vc_ref, vco_ref, False, f32)

    tm = pos_ref.shape[0]

    def values_with_ones(src, dst):
        t = src[...]
        tail = jnp.ones((tm, HEAD_DIM), bf16)
        for g in range(NSA_KV_GROUPS):
            vg = t[:, g * HEAD_DIM:(g + 1) * HEAD_DIM].astype(bf16)
            dst[0, g] = jnp.concatenate([vg, tail], axis=1)

    values_with_ones(vs_ref, vso_ref)
    values_with_ones(vw_ref, vwo_ref)

    kw_t = _rope128(kw_ref[...], cos, sin).T
    ks_t = _rope128(ks_ref[...], cos, sin).T
    for g in range(NSA_KV_GROUPS):
        for c in range(tm // WIN_TK):
            kwo_ref[0, g, c] = kw_t[g * HEAD_DIM:(g + 1) * HEAD_DIM, c * WIN_TK:(c + 1) * WIN_TK].astype(bf16)
    blk = lax.broadcasted_iota(i32, (LANES, tm), 0)
    tok = pl.program_id(1) * tm + lax.broadcasted_iota(i32, (LANES, tm), 1)
    onehot_t = jnp.where(tok // SEL_BLOCK == blk, 1.0, 0.0).astype(bf16)
    for g in range(NSA_KV_GROUPS):
        kso_ref[0, g, 0, 0:LANES, :] = onehot_t
        kg = ks_t[g * HEAD_DIM:(g + 1) * HEAD_DIM, :].astype(bf16)
        kso_ref[0, g, 0, LANES:2 * LANES, :] = jnp.concatenate([kg, jnp.zeros_like(kg)], axis=0)


def _nsa_prep(proj, pos128, invf, b, t):
    tm = PREP_TM
    assert tm == ATT_TK
    nt = t // tm
    row = lambda bi, ti: (bi * nt + ti, 0)
    kv0 = COL_KV // LANES
    in_specs = [pl.BlockSpec((tm, LANES), row),
                pl.BlockSpec((1, LANES), lambda bi, ti: (0, 0)),
                pl.BlockSpec((tm, NSA_WIDTH), lambda bi, ti: (bi * nt + ti, COL_Q // NSA_WIDTH))]
    for k in range(6):
        in_specs.append(pl.BlockSpec((tm, LANES), functools.partial(lambda bi, ti, k: (bi * nt + ti, kv0 + k), k=k)))
    head = lambda bi, ti: (bi, 0, ti, 0)
    tile5 = lambda bi, ti: (bi, 0, ti, 0, 0)
    g = NSA_KV_GROUPS
    out_specs = [pl.BlockSpec((1, NSA_HEADS, tm, HEAD_DIM), head),
                 pl.BlockSpec((1, g, tm, HEAD_DIM), head), pl.BlockSpec((1, g, tm, HEAD_DIM), head),
                 pl.BlockSpec((1, g, 1, KX_WIDTH, tm), tile5), pl.BlockSpec((1, g, tm, LANES), head),
                 pl.BlockSpec((1, g, tm // WIN_TK, HEAD_DIM, WIN_TK), tile5), pl.BlockSpec((1, g, tm, LANES), head)]
    out_shape = [jax.ShapeDtypeStruct((b, NSA_HEADS, t, HEAD_DIM), bf16),
                 jax.ShapeDtypeStruct((b, g, t, HEAD_DIM), f32), jax.ShapeDtypeStruct((b, g, t, HEAD_DIM), f32),
                 jax.ShapeDtypeStruct((b, g, nt, KX_WIDTH, tm), bf16), jax.ShapeDtypeStruct((b, g, t, LANES), bf16),
                 jax.ShapeDtypeStruct((b, g, t // WIN_TK, HEAD_DIM, WIN_TK), bf16),
                 jax.ShapeDtypeStruct((b, g, t, LANES), bf16)]
    return pl.pallas_call(
        _nsa_prep_kernel,
        grid=(b, nt),
        in_specs=in_specs,
        out_specs=out_specs,
        out_shape=out_shape,
        compiler_params=_cparams(("parallel", "parallel")),
        name="nsa_prep",
    )(pos128, invf, proj, proj, proj, proj, proj, proj, proj)


def _cmp_mlp_kernel(a_ref, pe_ref, w1_ref, b1_ref, w2_ref, pos_ref, invf_ref, o_ref, ot_ref):
    kind = pl.program_id(0)
    a = a_ref[0, 0]
    nc = a.shape[0]
    u = _dot((a + pe_ref[0, 0]).astype(bf16), w1_ref[0, 0])
    v = _dot((a + pe_ref[0, 1]).astype(bf16), w1_ref[0, 1])
    v_next = pltpu.roll(v, nc - 1, 0)
    hid = jax.nn.gelu(u + v_next + b1_ref[0])
    out = _dot(hid.astype(bf16), w2_ref[0])
    cos, sin = _rope_tables(pos_ref[0], invf_ref[...])
    roped = _rope128(out, cos, sin)
    out = jnp.where(kind == 0, roped, out)
    o_ref[0, 0] = out[:, :HEAD_DIM].astype(bf16)
    ot_ref[0, 0] = out.T[:HEAD_DIM, :].astype(bf16)


def _cmp_mlp(a, pe, w1, b1, w2, posc, invf, b):
    _, bg, nc, hw = a.shape
    g = bg // b
    return pl.pallas_call(
        _cmp_mlp_kernel,
        grid=(2, bg),
        in_specs=[pl.BlockSpec((1, 1, nc, hw), lambda k, i: (k, i, 0, 0)),
                  pl.BlockSpec((1, 2, 1, hw), lambda k, i: (k, 0, 0, 0)),
                  pl.BlockSpec((1, 2, hw, CMP_HIDDEN), lambda k, i: (k, 0, 0, 0)),
                  pl.BlockSpec((1, 1, CMP_HIDDEN), lambda k, i: (k, 0, 0)),
                  pl.BlockSpec((1, CMP_HIDDEN, LANES), lambda k, i: (k, 0, 0)),
                  pl.BlockSpec((1, nc, LANES), lambda k, i: (i // g, 0, 0)),
                  pl.BlockSpec((1, LANES), lambda k, i: (0, 0))],
        out_specs=[pl.BlockSpec((1, 1, nc, HEAD_DIM), lambda k, i: (k, i, 0, 0)),
                   pl.BlockSpec((1, 1, HEAD_DIM, nc), lambda k, i: (k, i, 0, 0))],
        out_shape=[jax.ShapeDtypeStruct((2, bg, nc, HEAD_DIM), bf16),
                   jax.ShapeDtypeStruct((2, bg, HEAD_DIM, nc), bf16)],
        compiler_params=_cparams(("parallel", "parallel")),
        name="cmp_mlp",
    )(a, pe, w1, b1, w2, posc, invf)


def _cmp_attn_kernel(q_ref, kct_ref, vc_ref, cover_ref, oc_ref, sel_ref):
    qi = pl.program_id(2)
    nc = vc_ref.shape[2]
    rows = NSA_REP * CMP_TQ
    tq = qi * CMP_TQ + lax.broadcasted_iota(i32, (CMP_TQ, 1), 0)
    row_live = jnp.where(tq >= CMP_BLOCK - 1, 1.0, 0.0)
    tiny = jnp.finfo(f32).tiny

    def attend(ncols):
        kct = kct_ref[0, 0, :, :ncols]
        vc = vc_ref[0, 0, :ncols, :]
        cend = lax.broadcasted_iota(i32, (1, ncols), 1) * CMP_STRIDE + (CMP_BLOCK - 1)
        bias = jnp.where(cend <= tq, 0.0, MASK_NEG)
        s = _dot(q_ref[0].reshape(rows, HEAD_DIM), kct).reshape(NSA_REP, CMP_TQ, ncols) + bias[None]
        e = jnp.exp(s - jnp.max(s, axis=-1, keepdims=True))
        live = row_live[None]
        p = e * (live / jnp.maximum(live * jnp.sum(e, axis=-1, keepdims=True), tiny))
        oc_ref[0] = _dot(p.reshape(rows, ncols).astype(bf16), vc).reshape(NSA_REP, CMP_TQ, HEAD_DIM)
        psum = jnp.sum(p, axis=0)
        hi = psum.astype(bf16)
        lo = (psum - hi.astype(f32)).astype(bf16)
        cover = cover_ref[:ncols, :]
        imp = _dot(hi, cover) + _dot(lo, cover)

        j = lax.broadcasted_iota(i32, (CMP_TQ, LANES), 1)
        cur = tq // SEL_BLOCK
        forced = (j == 0) | (j == cur) | (j == cur - 1)
        valid = j * SEL_BLOCK <= tq
        imp = jnp.where(valid, jnp.where(forced, FORCED_SCORE, imp), -FORCED_SCORE)

        nblk = ncols * CMP_STRIDE // SEL_BLOCK
        jt = lax.broadcasted_iota(i32, (nblk, CMP_TQ), 0)

        def pick(_, carry):
            work, sel = carry
            m = jnp.max(work, axis=0, keepdims=True)
            first = jnp.min(jnp.where(work == m, jt, LANES), axis=0, keepdims=True)
            hit = jt == first
            return jnp.where(hit, -jnp.inf, work), jnp.where(hit, 1.0, sel)

        _, sel_t = lax.fori_loop(0, SEL_TOPK, pick, (imp.T[:nblk], jnp.zeros((nblk, CMP_TQ), f32)))
        if nblk < LANES:
            sel_t = jnp.concatenate([sel_t, jnp.zeros((LANES - nblk, CMP_TQ), f32)], axis=0)
        sel_ref[0, 0] = jnp.where(valid, jnp.where(sel_t.T > 0.0, 0.0, MASK_NEG), MASK_NEG).astype(bf16)

    need = (qi + 1) * (CMP_TQ // CMP_STRIDE)
    for ncols in range(LANES, nc + 1, LANES):
        @pl.when((need > ncols - LANES) & (need <= ncols))
        def _():
            attend(ncols)


def _cmp_attn(q_r, kvc, kvc_t, cover, b, t):
    g = NSA_KV_GROUPS
    nc = kvc.shape[2]
    nq = t // CMP_TQ
    return pl.pallas_call(
        _cmp_attn_kernel,
        grid=(b, g, nq),
        in_specs=[pl.BlockSpec((1, NSA_REP, CMP_TQ, HEAD_DIM), lambda bi, gi, qi: (bi, gi, qi, 0)),
                  pl.BlockSpec((1, 1, HEAD_DIM, nc), lambda bi, gi, qi: (0, bi * g + gi, 0, 0)),
                  pl.BlockSpec((1, 1, nc, HEAD_DIM), lambda bi, gi, qi: (1, bi * g + gi, 0, 0)),
                  pl.BlockSpec((nc, LANES), lambda bi, gi, qi: (0, 0))],
        out_specs=[pl.BlockSpec((1, NSA_REP, CMP_TQ, HEAD_DIM), lambda bi, gi, qi: (bi, gi, qi, 0)),
                   pl.BlockSpec((1, 1, CMP_TQ, LANES), lambda bi, gi, qi: (bi, gi, qi, 0))],
        out_shape=[jax.ShapeDtypeStruct((b, NSA_HEADS, t, HEAD_DIM), f32),
                   jax.ShapeDtypeStruct((b, g, t, LANES), bf16)],
        compiler_params=_cparams(("parallel", "parallel", "parallel")),
        name="cmp_attn",
    )(q_r, kvc_t, kvc, cover)


def _nsa_attn_kernel(q_ref, kx_ref, vs_ref, kw_ref, vw_ref, sel_ref, oc_ref, gate_ref,
                     o_ref, qx_ref, s_ref, m_ref, acc_ref, yw_ref):
    gi = pl.program_id(1)
    qi = pl.program_id(2)
    start = qi * ATT_TQ
    tq = start + lax.broadcasted_iota(i32, (ATT_TQ, 1), 0)

    selb = sel_ref[0, 0]
    for r in range(NSA_REP):
        qx_ref[r * ATT_TQ:(r + 1) * ATT_TQ, 0:LANES] = selb
        qr = q_ref[0, r]
        qx_ref[r * ATT_TQ:(r + 1) * ATT_TQ, LANES:2 * LANES] = jnp.concatenate([qr, jnp.zeros_like(qr)], axis=1)
    rows = NSA_REP * ATT_TQ

    m_ref[...] = jnp.full(m_ref.shape, MASK_NEG, f32)
    acc_ref[...] = jnp.zeros(acc_ref.shape, f32)

    def consume(kt, v_ref, bias, parts=1):
        k0 = pl.multiple_of(kt * ATT_TK, ATT_TK)
        v = v_ref[0, 0, pl.ds(k0, ATT_TK), :]
        heads = NSA_REP // parts
        for part in range(parts):
            rs = slice(part * heads * ATT_TQ, (part + 1) * heads * ATT_TQ)
            s = s_ref[rs]
            if bias is not None:
                s = (s.reshape(heads, ATT_TQ, ATT_TK) + bias[None]).reshape(heads * ATT_TQ, ATT_TK)
            m_prev = m_ref[rs]
            m_new = jnp.maximum(m_prev, jnp.max(s, axis=-1, keepdims=True))
            alpha = jnp.exp(m_prev - m_new)
            p = jnp.exp(s - jnp.concatenate([m_new] * (ATT_TK // LANES), axis=1))
            acc_ref[rs] = alpha * acc_ref[rs] + _dot(p.astype(bf16), v)
            m_ref[rs] = m_new

    def kpos(kt):
        return kt * ATT_TK + lax.broadcasted_iota(i32, (1, ATT_TK), 1)

    last = start // ATT_TK
    for part in range(2):
        rs = slice(part * rows // 2, (part + 1) * rows // 2)
        s_ref[rs] = _dot(qx_ref[rs], kx_ref[0, 0, 0])

    def sel_step(kt, _):
        consume(kt, vs_ref, None)
        s_ref[...] = _dot(qx_ref[...], kx_ref[0, 0, kt + 1])
        return 0

    n_win = (WINDOW + ATT_TQ) // WIN_TK
    w_first = jnp.maximum(start // WIN_TK - WINDOW // WIN_TK, 0)
    kw = jnp.concatenate([kw_ref[0, 0, w_first + i] for i in range(n_win)], axis=1)
    w0 = pl.multiple_of(w_first * WIN_TK, WIN_TK)
    vw = vw_ref[0, 0, pl.ds(w0, n_win * WIN_TK), :]
    kp = w0 + lax.broadcasted_iota(i32, (1, n_win * WIN_TK), 1)
    wbias = jnp.where(kp <= tq, jnp.where(kp > tq - WINDOW, 0.0, MASK_NEG), MASK_NEG)
    hh = WIN_HEADS
    gates = jax.nn.sigmoid(gate_ref[...])
    per_group = 3 * NSA_REP
    shifted = gates
    for gg in range(1, NSA_KV_GROUPS):
        shifted = jnp.where(gi == gg, pltpu.roll(gates, LANES - gg * per_group, 1), shifted)

    def gate(r, br):
        c = r * 3 + br
        return jnp.broadcast_to(shifted[:, c:c + 1], (ATT_TQ, HEAD_DIM))

    upper = pltpu.roll(shifted, HEAD_DIM, 1)[:, HEAD_DIM:]

    def gated_output(a, r, br):
        c = r * 3 + br
        ratio = upper / a[:, HEAD_DIM:]
        return a[:, :HEAD_DIM] * jnp.broadcast_to(ratio[:, c:c + 1], (ATT_TQ, HEAD_DIM))

    for half in range(NSA_REP // hh):
        qh = q_ref[0, half * hh:(half + 1) * hh].reshape(hh * ATT_TQ, HEAD_DIM)
        sw = _dot(qh, kw).reshape(hh, ATT_TQ, n_win * WIN_TK) + wbias[None]
        sw = sw.reshape(hh * ATT_TQ, n_win * WIN_TK)
        pw = jnp.exp(sw - jnp.max(sw, axis=-1, keepdims=True))
        aw = _dot(pw.astype(bf16), vw)
        for i in range(hh):
            r = half * hh + i
            a = aw[i * ATT_TQ:(i + 1) * ATT_TQ]
            yw_ref[r] = gated_output(a, r, 2)

    def sel_pair(j, _):
        sel_step(2 * j, 0)
        sel_step(2 * j + 1, 0)
        return 0

    lax.fori_loop(0, last // 2, sel_pair, 0)

    @pl.when(last % 2 == 1)
    def _():
        sel_step(last - 1, 0)

    consume(last, vs_ref, jnp.where(kpos(last) <= tq, 0.0, MASK_NEG), parts=2)

    for r in range(NSA_REP):
        osel = gated_output(acc_ref[r * ATT_TQ:(r + 1) * ATT_TQ], r, 1)
        o_ref[:, r * HEAD_DIM:(r + 1) * HEAD_DIM] = (gate(r, 0) * oc_ref[0, r] + osel + yw_ref[r]).astype(o_ref.dtype)


def _nsa_attn(q_r, kx, vs, kw, vw, selb, oc, proj, b, t):
    g = NSA_KV_GROUPS
    nq = t // ATT_TQ
    once = dict(pipeline_mode=pl.Buffered(1))
    vspec = pl.BlockSpec((1, 1, t, LANES), lambda bi, gi, qi: (bi, gi, 0, 0), **once)
    hspec = pl.BlockSpec((1, NSA_REP, ATT_TQ, HEAD_DIM), lambda bi, gi, qi: (bi, gi, qi, 0))
    return pl.pallas_call(
        _nsa_attn_kernel,
        grid=(b, g, nq),
        in_specs=[hspec,
                  pl.BlockSpec((1, 1, t // ATT_TK, KX_WIDTH, ATT_TK), lambda bi, gi, qi: (bi, gi, 0, 0, 0), **once),
                  vspec,
                  pl.BlockSpec((1, 1, t // WIN_TK, HEAD_DIM, WIN_TK), lambda bi, gi, qi: (bi, gi, 0, 0, 0), **once),
                  vspec,
                  pl.BlockSpec((1, 1, ATT_TQ, LANES), lambda bi, gi, qi: (bi, gi, qi, 0)),
                  hspec,
                  pl.BlockSpec((ATT_TQ, LANES), lambda bi, gi, qi: (bi * nq + qi, COL_SMALL // LANES))],
        out_specs=pl.BlockSpec((ATT_TQ, NSA_REP * HEAD_DIM), lambda bi, gi, qi: (bi * nq + qi, gi)),
        out_shape=jax.ShapeDtypeStruct((b * t, NSA_WIDTH), bf16),
        scratch_shapes=[pltpu.VMEM((NSA_REP * ATT_TQ, KX_WIDTH), bf16),
                        pltpu.VMEM((NSA_REP * ATT_TQ, ATT_TK), f32),
                        pltpu.VMEM((NSA_REP * ATT_TQ, LANES), f32),
                        pltpu.VMEM((NSA_REP * ATT_TQ, LANES), f32),
                        pltpu.VMEM((NSA_REP, ATT_TQ, HEAD_DIM), f32)],
        compiler_params=_cparams(("parallel", "parallel", "arbitrary")),
        name="nsa_attn",
    )(q_r, kx, vs, kw, vw, selb, oc, proj)


HALO = 8


def _ssd_kernel(xbc_ref, z_ref, small_ref, dtt_ref, cw_ref, cb_ref, dtb_r_ref, dtb_c_ref,
                alog_r_ref, alog_c_ref, dskip_ref, nw_ref, o_ref, ext_ref, st_ref):
    c = pl.program_id(1)
    L = CHUNK

    @pl.when(c == 0)
    def _():
        ext_ref[0:HALO, :] = jnp.zeros((HALO, XBC_WIDTH), f32)
        st_ref[...] = jnp.zeros_like(st_ref)

    ext_ref[HALO:HALO + L, :] = xbc_ref[...]
    conv = cb_ref[...]
    for k in range(CONV_WIDTH):
        off = HALO - (CONV_WIDTH - 1) + k
        conv = conv + cw_ref[k:k + 1, :] * ext_ref[off:off + L, :]
    ext_ref[0:HALO, :] = ext_ref[L:L + HALO, :]
    act = conv * jax.nn.sigmoid(conv)
    xs = act[:, :SSM_WIDTH]
    bm = act[:, SSM_WIDTH:SSM_WIDTH + SSM_GROUPS * SSM_STATE]
    cm = act[:, SSM_WIDTH + SSM_GROUPS * SSM_STATE:]

    dt_c = jax.nn.softplus(small_ref[:, SMALL_DT_OFF:SMALL_DT_OFF + SSM_HEADS] + dtb_r_ref[...])
    dt_r = jax.nn.softplus(dtt_ref[0] + dtb_c_ref[...])
    a_r = -jnp.exp(alog_r_ref[...])
    a_c = -jnp.exp(alog_c_ref[...])
    row = lax.broadcasted_iota(i32, (L, L), 0)
    col = lax.broadcasted_iota(i32, (L, L), 1)
    causal = col <= row
    tri = jnp.where(causal, 1.0, 0.0)
    acs_c = _dot_hi(tri, dt_c * a_r)
    acs_r = _dot_hi(dt_r * a_c, jnp.where(row <= col, 1.0, 0.0))

    z = z_ref[...]
    ys = []
    for g in range(SSM_GROUPS):
        cg = cm[:, g * SSM_STATE:(g + 1) * SSM_STATE].astype(bf16)
        bg = bm[:, g * SSM_STATE:(g + 1) * SSM_STATE]
        cb = _dot_t(cg, bg.astype(bf16))
        bg_t = bg.T
        ssq = jnp.zeros((L, 1), f32)
        yg = []
        for r in range(SSM_REP):
            h = g * SSM_REP + r
            hs = slice(h * SSM_HEADDIM, (h + 1) * SSM_HEADDIM)
            a_col = acs_c[:, h:h + 1]
            a_row = acs_r[h:h + 1, :]
            dt_row = dt_r[h:h + 1, :]
            a_last = acs_r[h:h + 1, L - 1:L]
            seg = a_col - a_row
            decay = jnp.where(causal, jnp.exp(jnp.where(causal, seg, 0.0)), 0.0)
            w = cb * decay * dt_row
            x_h = xs[:, hs]
            xb = x_h.astype(bf16)
            st = st_ref[h]
            y = _dot(w.astype(bf16), xb)
            y = y + _dot(cg, st.astype(bf16)) * jnp.exp(a_col)
            y = y + dskip_ref[:, hs] * x_h
            bscaled = bg_t * (jnp.exp(a_last - a_row) * dt_row)
            st_ref[h] = jnp.exp(a_last) * st + _dot(bscaled.astype(bf16), xb)
            zh = z[:, hs]
            y = y * (zh * jax.nn.sigmoid(zh))
            ssq = ssq + jnp.sum(y * y, axis=-1, keepdims=True)
            yg.append(y)
        rs = lax.rsqrt(ssq / (SSM_REP * SSM_HEADDIM) + NORM_EPS)
        for r in range(SSM_REP):
            h = g * SSM_REP + r
            hs = slice(h * SSM_HEADDIM, (h + 1) * SSM_HEADDIM)
            o_ref[:, hs] = (yg[r] * rs * nw_ref[:, hs]).astype(o_ref.dtype)


def _ssd(proj, dtt, cw, cb, dtb_r, dtb_c, alog_r, alog_c, dskip, nw, b, t):
    nch = t // CHUNK
    row = lambda bi, ci: bi * nch + ci
    const2 = lambda bi, ci: (0, 0)
    return pl.pallas_call(
        _ssd_kernel,
        grid=(b, nch),
        in_specs=[pl.BlockSpec((CHUNK, XBC_WIDTH), lambda bi, ci: (row(bi, ci), COL_XBC // XBC_WIDTH)),
                  pl.BlockSpec((CHUNK, SSM_WIDTH), lambda bi, ci: (row(bi, ci), COL_Z // SSM_WIDTH)),
                  pl.BlockSpec((CHUNK, LANES), lambda bi, ci: (row(bi, ci), COL_SMALL // LANES)),
                  pl.BlockSpec((1, SSM_HEADS, CHUNK), lambda bi, ci: (bi, 0, ci)),
                  pl.BlockSpec((CONV_WIDTH, XBC_WIDTH), const2),
                  pl.BlockSpec((1, XBC_WIDTH), const2),
                  pl.BlockSpec((1, SSM_HEADS), const2),
                  pl.BlockSpec((SSM_HEADS, 1), const2),
                  pl.BlockSpec((1, SSM_HEADS), const2),
                  pl.BlockSpec((SSM_HEADS, 1), const2),
                  pl.BlockSpec((1, SSM_WIDTH), const2),
                  pl.BlockSpec((1, SSM_WIDTH), const2)],
        out_specs=pl.BlockSpec((CHUNK, SSM_WIDTH), lambda bi, ci: (row(bi, ci), 0)),
        out_shape=jax.ShapeDtypeStruct((b * t, SSM_WIDTH), bf16),
        scratch_shapes=[pltpu.VMEM((HALO + CHUNK, XBC_WIDTH), f32),
                        pltpu.VMEM((SSM_HEADS, SSM_STATE, SSM_HEADDIM), f32)],
        compiler_params=_cparams(("parallel", "arbitrary")),
        name="ssd",
    )(proj, proj, proj, dtt, cw, cb, dtb_r, dtb_c, alog_r, alog_c, dskip, nw)


def _layer_norm(v, g, b):
    mu = jnp.mean(v, axis=-1, keepdims=True)
    d = v - mu
    var = jnp.mean(d * d, axis=-1, keepdims=True)
    return d * lax.rsqrt(var + NORM_EPS) * g + b


def _outproj_kernel(alpha, ya_ref, yb_ref, x_ref, wa_ref, wb_ref, g_ref, b_ref, wr_ref, br_ref,
                    h_ref, route_ref, cnt_ref):
    i = pl.program_id(0)
    tm = x_ref.shape[0]
    mix = _dot(ya_ref[...], wa_ref[...]) + _dot(yb_ref[...], wb_ref[...])
    h = _layer_norm(alpha * x_ref[...] + mix, g_ref[...], b_ref[...])
    h_ref[...] = h

    h_hi = h.astype(bf16)
    h_lo = (h - h_hi.astype(f32)).astype(bf16)
    t = _dot(h_hi, wr_ref[...])
    logits = t[:, :LANES] + t[:, LANES:] + _dot(h_lo, wr_ref[:, :LANES]) + br_ref[...]
    lane = lax.broadcasted_iota(i32, (tm, LANES), 1)
    ninf = -jnp.inf
    gmask = lane < N_EXPERT_GROUPS
    gl = jnp.where(gmask, logits, ninf)
    ge = jnp.where(gmask, jnp.exp(gl - jnp.max(gl, axis=-1, keepdims=True)), 0.0)
    pg = ge / jnp.sum(ge, axis=-1, keepdims=True)
    g_gate = jnp.max(pg, axis=-1, keepdims=True)
    g_sel = jnp.min(jnp.where(gmask & (pg == g_gate), lane, LANES), axis=-1, keepdims=True)
    lo = N_EXPERT_GROUPS + g_sel * EXPERTS_PER_GROUP
    emask = (lane >= lo) & (lane < lo + EXPERTS_PER_GROUP)
    el = jnp.where(emask, logits, ninf)
    ee = jnp.where(emask, jnp.exp(el - jnp.max(el, axis=-1, keepdims=True)), 0.0)
    pe = ee / jnp.sum(ee, axis=-1, keepdims=True)
    p0 = jnp.max(pe, axis=-1, keepdims=True)
    l0 = jnp.min(jnp.where(emask & (pe == p0), lane, LANES), axis=-1, keepdims=True)
    rest = jnp.where(emask & (lane != l0), pe, ninf)
    p1 = jnp.max(rest, axis=-1, keepdims=True)
    l1 = jnp.min(jnp.where(rest == p1, lane, LANES), axis=-1, keepdims=True)
    psum = p0 + p1
    w0 = g_gate * p0 / psum
    w1 = g_gate * p1 / psum
    e0 = l0 - N_EXPERT_GROUPS
    e1 = l1 - N_EXPERT_GROUPS

    @pl.when(i == 0)
    def _():
        cnt_ref[...] = jnp.zeros_like(cnt_ref)

    oh0 = lane == e0
    oh1 = lane == e1
    both = jnp.where(oh0, 1.0, 0.0) + jnp.where(oh1, 1.0, 0.0)
    r_i = lax.broadcasted_iota(i32, (tm, tm), 0)
    c_i = lax.broadcasted_iota(i32, (tm, tm), 1)
    strict = jnp.where(c_i < r_i, 1.0, 0.0).astype(bf16)
    before = _dot(strict, both.astype(bf16)) + cnt_ref[...]
    rank0 = jnp.sum(jnp.where(oh0, before, 0.0), axis=-1, keepdims=True)
    rank1 = jnp.sum(jnp.where(oh1, before, 0.0), axis=-1, keepdims=True)
    cnt_ref[...] = cnt_ref[...] + jnp.sum(both, axis=0, keepdims=True)

    out = jnp.where(lane == 0, e0.astype(f32), 0.0)
    out = jnp.where(lane == 1, e1.astype(f32), out)
    out = jnp.where(lane == 2, w0, out)
    out = jnp.where(lane == 3, w1, out)
    out = jnp.where(lane == 4, rank0, out)
    out = jnp.where(lane == 5, rank1, out)
    route_ref[...] = out


def _outproj(y_nsa, y_ssm, x2, wa, wb, g, bta, wr, br, alpha):
    n, d = x2.shape
    tm = OUT_TM
    const = lambda i: (0, 0)
    rowb = lambda i: (i, 0)
    return pl.pallas_call(
        functools.partial(_outproj_kernel, alpha),
        grid=(n // tm,),
        in_specs=[pl.BlockSpec((tm, NSA_WIDTH), rowb), pl.BlockSpec((tm, SSM_WIDTH), rowb),
                  pl.BlockSpec((tm, d), rowb),
                  pl.BlockSpec((NSA_WIDTH, d), const), pl.BlockSpec((SSM_WIDTH, d), const),
                  pl.BlockSpec((1, d), const), pl.BlockSpec((1, d), const),
                  pl.BlockSpec((d, 2 * LANES), const), pl.BlockSpec((1, LANES), const)],
        out_specs=[pl.BlockSpec((tm, d), rowb), pl.BlockSpec((tm, LANES), rowb),
                   pl.BlockSpec((1, LANES), const)],
        out_shape=[jax.ShapeDtypeStruct((n, d), f32), jax.ShapeDtypeStruct((n, LANES), f32),
                   jax.ShapeDtypeStruct((1, LANES), f32)],
        compiler_params=_cparams(("arbitrary",)),
        name="outproj",
    )(y_nsa, y_ssm, x2, wa, wb, g, bta, wr, br)


def _row_copy(src_ref, src_row, dst_ref, dst_row, sem):
    return pltpu.make_async_copy(src_ref.at[pl.ds(src_row, 1), :], dst_ref.at[pl.ds(dst_row, 1), :], sem)


def _dest_row(dest_ref, tm, k, r):
    return dest_ref[0, 0, k * tm + r]


def _dispatch_kernel(dest_ref, zflag_ref, h_ref, xs_ref, zero_ref, tile_ref, sem, lsem, zsem):
    tm = DISP_TM
    nb = zflag_ref.shape[0]
    i = pl.program_id(0)

    @pl.when(i == 0)
    def _():
        zero_ref[...] = jnp.zeros_like(zero_ref)

        def zblock(i):
            rows = pl.ds(pl.multiple_of(i * MOE_TM, MOE_TM), MOE_TM)
            return pltpu.make_async_copy(zero_ref, xs_ref.at[rows, :], zsem)

        def zstart(i, _):
            @pl.when(zflag_ref[i] != 0)
            def _():
                zblock(i).start()
            return 0

        def zwait(i, _):
            @pl.when(zflag_ref[i] != 0)
            def _():
                zblock(i).wait()
            return 0

        lax.fori_loop(0, nb, zstart, 0)
        lax.fori_loop(0, nb, zwait, 0)

    nslot = tile_ref.shape[0]
    nsteps = pl.num_programs(0)
    slot = i % nslot

    def tile_load(step, s):
        rows = pl.ds(pl.multiple_of(step * tm, tm), tm)
        return pltpu.make_async_copy(h_ref.at[rows, :], tile_ref.at[s], lsem.at[s])

    @pl.when(i == 0)
    def _():
        tile_load(0, 0).start()

    tile_load(i, slot).wait()

    @pl.when(i + 1 < nsteps)
    def _():
        tile_load(i + 1, (i + 1) % nslot).start()

    def issue(r, _):
        for k in range(2):
            _row_copy(tile_ref.at[slot], r, xs_ref, _dest_row(dest_ref, tm, k, r), sem.at[slot]).start()
        return 0

    lax.fori_loop(0, tm, issue, 0, unroll=16)

    def wait_rows(s):
        for k in range(2):
            pltpu.make_async_copy(tile_ref.at[s], xs_ref.at[pl.ds(0, tm), :], sem.at[s]).wait()

    @pl.when(i > 0)
    def _():
        wait_rows((i - 1) % nslot)

    @pl.when(i == nsteps - 1)
    def _():
        wait_rows(slot)


def _dispatch(dest_t, zflag, h, p_rows):
    n, d = h.shape
    tm = DISP_TM
    return pl.pallas_call(
        _dispatch_kernel,
        grid=(n // tm,),
        in_specs=[pl.BlockSpec((1, 1, 2 * tm), lambda i: (i, 0, 0), memory_space=pltpu.SMEM),
                  pl.BlockSpec(memory_space=pltpu.SMEM),
                  pl.BlockSpec(memory_space=pl.ANY)],
        out_specs=pl.BlockSpec(memory_space=pl.ANY),
        out_shape=jax.ShapeDtypeStruct((p_rows, d), f32),
        scratch_shapes=[pltpu.VMEM((MOE_TM, d), f32), pltpu.VMEM((3, tm, d), f32),
                        pltpu.SemaphoreType.DMA((3,)), pltpu.SemaphoreType.DMA((3,)), pltpu.SemaphoreType.DMA(())],
        compiler_params=_cparams(("arbitrary",)),
        name="dispatch",
    )(dest_t, zflag, h)


def _experts_kernel(be_ref, nu_ref, first_ref, next_ref, slot_ref, xs_ref, wg_ref, wu_ref, wd_ref, y_ref,
                    fg_ref, fu_ref, fd_ref, wgb_ref, wub_ref, wdb_ref, sems):
    i = pl.program_id(0)

    def weight_loads(e, s):
        return (pltpu.make_async_copy(wg_ref.at[e], fg_ref.at[s], sems.at[s, 0]),
                pltpu.make_async_copy(wu_ref.at[e], fu_ref.at[s], sems.at[s, 1]),
                pltpu.make_async_copy(wd_ref.at[e], fd_ref.at[s], sems.at[s, 2]))

    @pl.when(i == 0)
    def _():
        for c in weight_loads(be_ref[0], 0):
            c.start()

    @pl.when(first_ref[i] != 0)
    def _():
        s = slot_ref[i]
        for c in weight_loads(be_ref[i], s):
            c.wait()
        wgb_ref[...] = fg_ref[s].astype(bf16)
        wub_ref[...] = fu_ref[s].astype(bf16)
        wdb_ref[...] = fd_ref[s].astype(bf16)

        @pl.when(next_ref[i] >= 0)
        def _():
            for c in weight_loads(next_ref[i], 1 - s):
                c.start()

    @pl.when(i < nu_ref[0])
    def _():
        xb = xs_ref[...].astype(bf16)
        gte = _dot(xb, wgb_ref[...])
        up = _dot(xb, wub_ref[...])
        act = gte * jax.nn.sigmoid(gte) * up
        y_ref[...] = _dot(act.astype(bf16), wdb_ref[...])

    @pl.when(i >= nu_ref[0])
    def _():
        y_ref[...] = jnp.zeros_like(y_ref)


def _experts(block_e, n_used, first, next_e, slot, xs, w_gate, w_up, w_down):
    p_rows, d = xs.shape
    tm = MOE_TM
    nb = p_rows // tm
    de = w_gate.shape[-1]
    xmap = lambda i, be, nu, fi, ne, sl: (jnp.maximum(jnp.minimum(i, nu[0] - 1), 0), 0)
    hbm = pl.BlockSpec(memory_space=pl.ANY)
    return pl.pallas_call(
        _experts_kernel,
        grid_spec=pltpu.PrefetchScalarGridSpec(
            num_scalar_prefetch=5,
            grid=(nb,),
            in_specs=[pl.BlockSpec((tm, d), xmap), hbm, hbm, hbm],
            out_specs=pl.BlockSpec((tm, d), lambda i, be, nu, fi, ne, sl: (i, 0)),
            scratch_shapes=[pltpu.VMEM((2, d, de), f32), pltpu.VMEM((2, d, de), f32), pltpu.VMEM((2, de, d), f32),
                            pltpu.VMEM((d, de), bf16), pltpu.VMEM((d, de), bf16), pltpu.VMEM((de, d), bf16),
                            pltpu.SemaphoreType.DMA((2, 3))],
        ),
        out_shape=jax.ShapeDtypeStruct((p_rows, d), f32),
        compiler_params=_cparams(("arbitrary",)),
        name="experts",
    )(block_e, n_used, first, next_e, slot, xs, w_gate, w_up, w_down)


def _combine_kernel(alpha, dest_ref, ndest_ref, route_ref, h_ref, g_ref, b_ref, y_ref, o_ref, buf_ref, sems):
    tm = h_ref.shape[0]
    i = pl.program_id(0)
    slot = i % 2

    def start_row(ids_ref, s, r):
        for k in range(2):
            _row_copy(y_ref, _dest_row(ids_ref, tm, k, r), buf_ref.at[s, k], r, sems.at[s]).start()

    def wait_tile():
        for k in range(2):
            pltpu.make_async_copy(y_ref.at[pl.ds(0, tm), :], buf_ref.at[slot, k], sems.at[slot]).wait()

    def finish_tile():
        route = route_ref[...]
        ffn = route[:, 2:3] * buf_ref[slot, 0] + route[:, 3:4] * buf_ref[slot, 1]
        o_ref[...] = _layer_norm(alpha * h_ref[...] + ffn, g_ref[...], b_ref[...])

    @pl.when(i == 0)
    def _():
        def issue(r, _):
            start_row(dest_ref, 0, r)
            return 0
        lax.fori_loop(0, tm, issue, 0, unroll=4)

    @pl.when(i + 1 < pl.num_programs(0))
    def _():
        wait_tile()
        for r in range(tm):
            start_row(ndest_ref, 1 - slot, r)
        finish_tile()

    @pl.when(i + 1 == pl.num_programs(0))
    def _():
        wait_tile()
        finish_tile()


def _combine(dest_t, route, h, g, bta, y, alpha):
    n, d = h.shape
    tm = COMB_TM
    const = lambda i: (0, 0)
    return pl.pallas_call(
        functools.partial(_combine_kernel, alpha),
        grid=(n // tm,),
        in_specs=[pl.BlockSpec((1, 1, 2 * tm), lambda i: (i, 0, 0), memory_space=pltpu.SMEM),
                  pl.BlockSpec((1, 1, 2 * tm), lambda i: (jnp.minimum(i + 1, n // tm - 1), 0, 0),
                               memory_space=pltpu.SMEM),
                  pl.BlockSpec((tm, LANES), lambda i: (i, 0)),
                  pl.BlockSpec((tm, d), lambda i: (i, 0)),
                  pl.BlockSpec((1, d), const), pl.BlockSpec((1, d), const),
                  pl.BlockSpec(memory_space=pl.ANY)],
        out_specs=pl.BlockSpec((tm, d), lambda i: (i, 0)),
        out_shape=jax.ShapeDtypeStruct((n, d), f32),
        scratch_shapes=[pltpu.VMEM((2, 2, tm, d), f32), pltpu.SemaphoreType.DMA((2,))],
        compiler_params=_cparams(("arbitrary",)),
        name="combine",
    )(dest_t, dest_t, route, h, g, bta, y)


def _tile_dest(dest, tm):
    n = dest.shape[1]
    return dest.reshape(2, n // tm, tm).transpose(1, 0, 2).reshape(n // tm, 1, 2 * tm)


def _layer(x, positions, w_in, cmp_k_pe, cmp_k_w1, cmp_k_b1, cmp_k_w2, cmp_v_pe, cmp_v_w1, cmp_v_b1, cmp_v_w2,
           conv_w, conv_b, dt_bias, a_log, d_skip, ssm_norm_w, w_out, ln1_g, ln1_b,
           w_router_group, b_router_group, w_router_expert, b_router_expert, w_gate, w_up, w_down, ln2_g, ln2_b,
           alpha):
    b, t, d = x.shape
    n = b * t
    assert t % max(ATT_TK, ATT_TQ, CMP_TQ, CHUNK, PREP_TM) == 0 and n % max(PROJ_TM, OUT_TM, DISP_TM) == 0
    assert t // SEL_BLOCK <= LANES and (t // CMP_STRIDE) % LANES == 0 and t >= WINDOW + ATT_TQ
    x2 = x.reshape(n, d)

    c0 = NSA_WIDTH
    c1 = c0 + 6 * KV_WIDTH
    c2 = c1 + 3 * NSA_HEADS
    c3 = c2 + SSM_WIDTH
    c4 = c3 + XBC_WIDTH
    w_small = jnp.concatenate([w_in[:, c1:c2], w_in[:, c4:], jnp.zeros((d, LANES - 3 * NSA_HEADS - SSM_HEADS), f32)], axis=1)
    w_cat = jnp.concatenate([w_in[:, :c0], w_in[:, c2:c3], w_in[:, c3:c4], w_in[:, c0:c1], w_small,
                             jnp.zeros((d, PROJ_COLS - COL_SMALL - LANES), f32)], axis=1).astype(bf16)
    proj = _proj(x2, w_cat)

    lane = np.arange(LANES) % HEAD_DIM
    inv_freq = ROPE_THETA ** (-jnp.arange(0, ROT_DIM, 2, dtype=f32) / ROT_DIM)
    invf = jnp.where(lane < ROT_DIM, inv_freq[lane % (ROT_DIM // 2)], 0.0).astype(f32)[None, :]
    pos128 = jnp.broadcast_to(positions.reshape(n, 1), (n, LANES))
    q_r, k_cmp, v_cmp, k_sel, v_sel, k_win, v_win = _nsa_prep(proj, pos128, invf, b, t)

    nc = t // CMP_STRIDE
    half_w = CMP_STRIDE * HEAD_DIM
    a = jnp.stack([k_cmp, v_cmp]).reshape(2, b * NSA_KV_GROUPS, nc, half_w)
    pe = jnp.stack([cmp_k_pe, cmp_v_pe]).reshape(2, 2, 1, half_w)
    w1 = jnp.stack([cmp_k_w1, cmp_v_w1]).reshape(2, 2, half_w, CMP_HIDDEN).astype(bf16)
    b1 = jnp.stack([cmp_k_b1, cmp_v_b1]).reshape(2, 1, CMP_HIDDEN)
    w2 = jnp.pad(jnp.stack([cmp_k_w2, cmp_v_w2]), ((0, 0), (0, 0), (0, LANES - HEAD_DIM))).astype(bf16)
    cend = jnp.minimum(jnp.arange(nc) * CMP_STRIDE + CMP_BLOCK - 1, t - 1)
    posc = jnp.broadcast_to(positions[:, cend][:, :, None], (b, nc, LANES))
    kvc, kvc_t = _cmp_mlp(a, pe, w1, b1, w2, posc, invf, b)

    c_start = np.arange(nc)[:, None] * CMP_STRIDE
    s_start = np.arange(LANES)[None, :] * SEL_BLOCK
    cover = ((c_start < s_start + SEL_BLOCK) & (c_start + CMP_BLOCK > s_start)
             & (np.arange(nc)[:, None] < nc - 1) & (np.arange(LANES)[None, :] < t // SEL_BLOCK))
    cover = jnp.asarray(cover, bf16)
    o_cmp, selb = _cmp_attn(q_r, kvc, kvc_t, cover, b, t)

    y_nsa = _nsa_attn(q_r, k_sel, v_sel, k_win, v_win, selb, o_cmp, proj, b, t)

    dt_raw = proj[:, COL_SMALL + SMALL_DT_OFF:COL_SMALL + SMALL_DT_OFF + SSM_HEADS]
    dtt = dt_raw.reshape(b, t, SSM_HEADS).transpose(0, 2, 1)
    y_ssm = _ssd(proj, dtt, conv_w.reshape(CONV_WIDTH, XBC_WIDTH), conv_b.reshape(1, XBC_WIDTH),
                 dt_bias.reshape(1, SSM_HEADS), dt_bias.reshape(SSM_HEADS, 1),
                 a_log.reshape(1, SSM_HEADS), a_log.reshape(SSM_HEADS, 1),
                 jnp.repeat(d_skip, SSM_HEADDIM).reshape(1, SSM_WIDTH), ssm_norm_w.reshape(1, SSM_WIDTH), b, t)

    wr = jnp.concatenate([w_router_group, w_router_expert,
                          jnp.zeros((d, LANES - N_EXPERT_GROUPS - N_EXPERTS), f32)], axis=1)
    br = jnp.concatenate([b_router_group, b_router_expert,
                          jnp.zeros((LANES - N_EXPERT_GROUPS - N_EXPERTS,), f32)])[None, :]
    wr_hi = wr.astype(bf16)
    wr = jnp.concatenate([wr_hi, (wr - wr_hi.astype(f32)).astype(bf16)], axis=1)
    wo = w_out.astype(bf16)
    h, route, counts = _outproj(y_nsa, y_ssm, x2, wo[:NSA_WIDTH], wo[NSA_WIDTH:], ln1_g[None, :], ln1_b[None, :],
                                wr, br, alpha)

    cnt = counts[0, :N_EXPERTS].astype(i32)
    padded = (cnt + MOE_TM - 1) // MOE_TM * MOE_TM
    pad_ends = jnp.cumsum(padded)
    pad_starts = pad_ends - padded
    e01 = route[:, 0:2].astype(i32).T
    start01 = jnp.sum(jnp.where(e01[..., None] == jnp.arange(N_EXPERTS, dtype=i32), pad_starts, 0), axis=-1)
    dest = start01 + route[:, 4:6].astype(i32).T
    p_rows = 2 * n + N_EXPERTS * MOE_TM
    nb = p_rows // MOE_TM
    block_e = jnp.minimum(jnp.sum(jnp.arange(nb, dtype=i32)[:, None] * MOE_TM >= pad_ends[None, :], axis=-1),
                          N_EXPERTS - 1).astype(i32)
    n_used = (pad_ends[-1] // MOE_TM).astype(i32).reshape(1)
    blk = jnp.arange(nb, dtype=i32)
    last_of_expert = jnp.any((blk[:, None] + 1) * MOE_TM == pad_ends[None, :], axis=-1)
    zflag = (last_of_expert | (blk >= n_used[0])).astype(i32)
    block_e = jnp.where(blk < n_used[0], block_e, block_e[jnp.maximum(n_used[0] - 1, 0)])

    xs = _dispatch(_tile_dest(dest, DISP_TM), zflag, h, p_rows)
    prev_e = jnp.concatenate([jnp.full((1,), -1, i32), block_e[:-1]])
    first = ((block_e != prev_e) & (blk < n_used[0])).astype(i32)
    eidx = jnp.arange(N_EXPERTS, dtype=i32)
    later = jnp.where((eidx[None, :] > eidx[:, None]) & (padded[None, :] > 0), eidx[None, :], N_EXPERTS)
    next_of = jnp.min(later, axis=1)
    next_e = jnp.where(next_of[block_e] < N_EXPERTS, next_of[block_e], -1).astype(i32)
    slot = ((jnp.cumsum(first) - 1) % 2).astype(i32)
    y = _experts(block_e, n_used, first, next_e, slot, xs, w_gate, w_up, w_down)
    out = _combine(_tile_dest(dest, COMB_TM), route, h, ln2_g[None, :], ln2_b[None, :], y, alpha)
    return out.reshape(b, t, d)


def kernel(x, positions, w_in, cmp_k_pe, cmp_k_w1, cmp_k_b1, cmp_k_w2, cmp_v_pe, cmp_v_w1, cmp_v_b1, cmp_v_w2, conv_w, conv_b, dt_bias, a_log, d_skip, ssm_norm_w, w_out, ln1_g, ln1_b, w_router_group, b_router_group, w_router_expert, b_router_expert, w_gate, w_up, w_down, ln2_g, ln2_b):
    depth = w_in.shape[0]
    alpha = (2 * depth) ** 0.25
    params = (w_in, cmp_k_pe, cmp_k_w1, cmp_k_b1, cmp_k_w2, cmp_v_pe, cmp_v_w1, cmp_v_b1, cmp_v_w2, conv_w, conv_b,
              dt_bias, a_log, d_skip, ssm_norm_w, w_out, ln1_g, ln1_b, w_router_group, b_router_group,
              w_router_expert, b_router_expert, w_gate, w_up, w_down, ln2_g, ln2_b)
    for l in range(depth):
        x = _layer(x, positions, *[p[l] for p in params], alpha)
    return x
```

```python
import functools

import jax
import jax.numpy as jnp
import numpy as np
from jax import lax
from jax.experimental import pallas as pl
from jax.experimental.pallas import tpu as pltpu

f32 = jnp.float32
bf16 = jnp.bfloat16
i32 = jnp.int32

HEAD_DIM = 64
NSA_HEADS = 16
NSA_KV_GROUPS = 2
NSA_REP = NSA_HEADS // NSA_KV_GROUPS
NSA_WIDTH = NSA_HEADS * HEAD_DIM
KV_WIDTH = NSA_KV_GROUPS * HEAD_DIM
CMP_BLOCK = 32
CMP_STRIDE = 16
CMP_HIDDEN = 256
SEL_BLOCK = 64
SEL_TOPK = 16
WINDOW = 512
FORCED_SCORE = 1.0e4
SSM_HEADDIM = 64
SSM_HEADS = 16
SSM_WIDTH = SSM_HEADS * SSM_HEADDIM
SSM_GROUPS = 4
SSM_REP = SSM_HEADS // SSM_GROUPS
SSM_STATE = 128
CONV_WIDTH = 4
CHUNK = 256
XBC_WIDTH = SSM_WIDTH + 2 * SSM_GROUPS * SSM_STATE
ROPE_THETA = 500000.0
ROT_DIM = HEAD_DIM // 4
N_EXPERT_GROUPS = 4
EXPERTS_PER_GROUP = 8
N_EXPERTS = N_EXPERT_GROUPS * EXPERTS_PER_GROUP
NORM_EPS = 1e-5

LANES = 128
V7X_VMEM_BYTES = 64 * 1024 * 1024
MASK_NEG = -1.0e30

PROJ_TM = 1024
PROJ_TN = 1024
PREP_TM = 512
CMP_TQ = 256
ATT_TQ = 256
ATT_TK = 512
WIN_TK = 256
WIN_HEADS = 2
KX_WIDTH = 2 * LANES
OUT_TM = 512
MOE_TM = 256
DISP_TM = 256
COMB_TM = 128
VMEM_LIMIT = V7X_VMEM_BYTES - 8 * 1024 * 1024

COL_Q = 0
COL_Z = NSA_WIDTH
COL_XBC = COL_Z + SSM_WIDTH
COL_KV = COL_XBC + XBC_WIDTH
COL_SMALL = COL_KV + 6 * KV_WIDTH
PROJ_COLS = -(-(COL_SMALL + LANES) // PROJ_TN) * PROJ_TN
SMALL_DT_OFF = 3 * NSA_HEADS


def _cparams(sem, vmem=VMEM_LIMIT):
    return pltpu.CompilerParams(dimension_semantics=sem, vmem_limit_bytes=vmem)


def _dot(a, b):
    return jnp.dot(a, b, preferred_element_type=f32)


def _dot_t(a, b):
    return lax.dot_general(a, b, (((1,), (1,)), ((), ())), preferred_element_type=f32)


def _dot_hi(a, b):
    return jnp.dot(a, b, preferred_element_type=f32, precision=lax.Precision.HIGHEST)


def _proj_kernel(x_ref, w_ref, o_ref, xb_ref):
    @pl.when(pl.program_id(1) == 0)
    def _():
        xb_ref[...] = x_ref[...].astype(bf16)

    o_ref[...] = _dot(xb_ref[...], w_ref[...])


def _proj(x2, w_cat):
    n, d = x2.shape
    cols = w_cat.shape[1]
    tm = min(PROJ_TM, n)
    return pl.pallas_call(
        _proj_kernel,
        grid=(n // tm, cols // PROJ_TN),
        in_specs=[pl.BlockSpec((tm, d), lambda i, j: (i, 0)),
                  pl.BlockSpec((d, PROJ_TN), lambda i, j: (0, j))],
        out_specs=pl.BlockSpec((tm, PROJ_TN), lambda i, j: (i, j)),
        out_shape=jax.ShapeDtypeStruct((n, cols), f32),
        scratch_shapes=[pltpu.VMEM((tm, d), bf16)],
        compiler_params=_cparams(("parallel", "arbitrary")),
        name="proj",
    )(x2, w_cat)


def _rope_tables(pos_i32, invf):
    ang = pos_i32.astype(f32) * invf
    return jnp.cos(ang), jnp.sin(ang)


def _rope128(x, cos, sin):
    half = ROT_DIM // 2
    d = lax.broadcasted_iota(i32, x.shape, 1) % HEAD_DIM
    up = pltpu.roll(x, LANES - half, 1)
    dn = pltpu.roll(x, half, 1)
    rot = jnp.where(d < half, -up, dn)
    return x * cos + rot * sin


def _nsa_prep_kernel(pos_ref, invf_ref, q_ref, kc_ref, vc_ref, ks_ref, vs_ref, kw_ref, vw_ref,
                     qo_ref, kco_ref, vco_ref, kso_ref, vso_ref, kwo_ref, vwo_ref):
    cos, sin = _rope_tables(pos_ref[...], invf_ref[...])
    scale = HEAD_DIM ** -0.5
    for c in range(NSA_WIDTH // LANES):
        t = _rope128(q_ref[:, c * LANES:(c + 1) * LANES], cos, sin) * scale
        qo_ref[0, 2 * c] = t[:, :HEAD_DIM].astype(bf16)
        qo_ref[0, 2 * c + 1] = t[:, HEAD_DIM:].astype(bf16)

    def split(src, dst, rope, dt):
        t = src[...]
        if rope:
            t = _rope128(t, cos, sin)
        for g in range(NSA_KV_GROUPS):
            dst[0, g] = t[:, g * HEAD_DIM:(g + 1) * HEAD_DIM].astype(dt)

    split(kc_ref, kco_ref, False, f32)
    split(vc_ref, vco_ref, False, f32)

    tm = pos_ref.shape[0]

    def values_with_ones(src, dst):
        t = src[...]
        tail = jnp.ones((tm, HEAD_DIM), bf16)
        for g in range(NSA_KV_GROUPS):
            vg = t[:, g * HEAD_DIM:(g + 1) * HEAD_DIM].astype(bf16)
            dst[0, g] = jnp.concatenate([vg, tail], axis=1)

    values_with_ones(vs_ref, vso_ref)
    values_with_ones(vw_ref, vwo_ref)

    kw_t = _rope128(kw_ref[...], cos, sin).T
    ks_t = _rope128(ks_ref[...], cos, sin).T
    for g in range(NSA_KV_GROUPS):
        for c in range(tm // WIN_TK):
            kwo_ref[0, g, c] = kw_t[g * HEAD_DIM:(g + 1) * HEAD_DIM, c * WIN_TK:(c + 1) * WIN_TK].astype(bf16)
    blk = lax.broadcasted_iota(i32, (LANES, tm), 0)
    tok = pl.program_id(1) * tm + lax.broadcasted_iota(i32, (LANES, tm), 1)
    onehot_t = jnp.where(tok // SEL_BLOCK == blk, 1.0, 0.0).astype(bf16)
    for g in range(NSA_KV_GROUPS):
        kso_ref[0, g, 0, 0:LANES, :] = onehot_t
        kg = ks_t[g * HEAD_DIM:(g + 1) * HEAD_DIM, :].astype(bf16)
        kso_ref[0, g, 0, LANES:2 * LANES, :] = jnp.concatenate([kg, jnp.zeros_like(kg)], axis=0)


def _nsa_prep(proj, pos128, invf, b, t):
    tm = PREP_TM
    assert tm == ATT_TK
    nt = t // tm
    row = lambda bi, ti: (bi * nt + ti, 0)
    kv0 = COL_KV // LANES
    in_specs = [pl.BlockSpec((tm, LANES), row),
                pl.BlockSpec((1, LANES), lambda bi, ti: (0, 0)),
                pl.BlockSpec((tm, NSA_WIDTH), lambda bi, ti: (bi * nt + ti, COL_Q // NSA_WIDTH))]
    for k in range(6):
        in_specs.append(pl.BlockSpec((tm, LANES), functools.partial(lambda bi, ti, k: (bi * nt + ti, kv0 + k), k=k)))
    head = lambda bi, ti: (bi, 0, ti, 0)
    tile5 = lambda bi, ti: (bi, 0, ti, 0, 0)
    g = NSA_KV_GROUPS
    out_specs = [pl.BlockSpec((1, NSA_HEADS, tm, HEAD_DIM), head),
                 pl.BlockSpec((1, g, tm, HEAD_DIM), head), pl.BlockSpec((1, g, tm, HEAD_DIM), head),
                 pl.BlockSpec((1, g, 1, KX_WIDTH, tm), tile5), pl.BlockSpec((1, g, tm, LANES), head),
                 pl.BlockSpec((1, g, tm // WIN_TK, HEAD_DIM, WIN_TK), tile5), pl.BlockSpec((1, g, tm, LANES), head)]
    out_shape = [jax.ShapeDtypeStruct((b, NSA_HEADS, t, HEAD_DIM), bf16),
                 jax.ShapeDtypeStruct((b, g, t, HEAD_DIM), f32), jax.ShapeDtypeStruct((b, g, t, HEAD_DIM), f32),
                 jax.ShapeDtypeStruct((b, g, nt, KX_WIDTH, tm), bf16), jax.ShapeDtypeStruct((b, g, t, LANES), bf16),
                 jax.ShapeDtypeStruct((b, g, t // WIN_TK, HEAD_DIM, WIN_TK), bf16),
                 jax.ShapeDtypeStruct((b, g, t, LANES), bf16)]
    return pl.pallas_call(
        _nsa_prep_kernel,
        grid=(b, nt),
        in_specs=in_specs,
        out_specs=out_specs,
        out_shape=out_shape,
        compiler_params=_cparams(("parallel", "parallel")),
        name="nsa_prep",
    )(pos128, invf, proj, proj, proj, proj, proj, proj, proj)


def _cmp_mlp_kernel(a_ref, pe_ref, w1_ref, b1_ref, w2_ref, pos_ref, invf_ref, o_ref, ot_ref):
    kind = pl.program_id(0)
    a = a_ref[0, 0]
    nc = a.shape[0]
    u = _dot((a + pe_ref[0, 0]).astype(bf16), w1_ref[0, 0])
    v = _dot((a + pe_ref[0, 1]).astype(bf16), w1_ref[0, 1])
    v_next = pltpu.roll(v, nc - 1, 0)
    hid = jax.nn.gelu(u + v_next + b1_ref[0])
    out = _dot(hid.astype(bf16), w2_ref[0])
    cos, sin = _rope_tables(pos_ref[0], invf_ref[...])
    roped = _rope128(out, cos, sin)
    out = jnp.where(kind == 0, roped, out)
    o_ref[0, 0] = out[:, :HEAD_DIM].astype(bf16)
    ot_ref[0, 0] = out.T[:HEAD_DIM, :].astype(bf16)


def _cmp_mlp(a, pe, w1, b1, w2, posc, invf, b):
    _, bg, nc, hw = a.shape
    g = bg // b
    return pl.pallas_call(
        _cmp_mlp_kernel,
        grid=(2, bg),
        in_specs=[pl.BlockSpec((1, 1, nc, hw), lambda k, i: (k, i, 0, 0)),
                  pl.BlockSpec((1, 2, 1, hw), lambda k, i: (k, 0, 0, 0)),
                  pl.BlockSpec((1, 2, hw, CMP_HIDDEN), lambda k, i: (k, 0, 0, 0)),
                  pl.BlockSpec((1, 1, CMP_HIDDEN), lambda k, i: (k, 0, 0)),
                  pl.BlockSpec((1, CMP_HIDDEN, LANES), lambda k, i: (k, 0, 0)),
                  pl.BlockSpec((1, nc, LANES), lambda k, i: (i // g, 0, 0)),
                  pl.BlockSpec((1, LANES), lambda k, i: (0, 0))],
        out_specs=[pl.BlockSpec((1, 1, nc, HEAD_DIM), lambda k, i: (k, i, 0, 0)),
                   pl.BlockSpec((1, 1, HEAD_DIM, nc), lambda k, i: (k, i, 0, 0))],
        out_shape=[jax.ShapeDtypeStruct((2, bg, nc, HEAD_DIM), bf16),
                   jax.ShapeDtypeStruct((2, bg, HEAD_DIM, nc), bf16)],
        compiler_params=_cparams(("parallel", "parallel")),
        name="cmp_mlp",
    )(a, pe, w1, b1, w2, posc, invf)


def _cmp_attn_kernel(q_ref, kct_ref, vc_ref, cover_ref, oc_ref, sel_ref):
    qi = pl.program_id(2)
    nc = vc_ref.shape[2]
    rows = NSA_REP * CMP_TQ
    tq = qi * CMP_TQ + lax.broadcasted_iota(i32, (CMP_TQ, 1), 0)
    row_live = jnp.where(tq >= CMP_BLOCK - 1, 1.0, 0.0)
    tiny = jnp.finfo(f32).tiny

    def attend(ncols):
        kct = kct_ref[0, 0, :, :ncols]
        vc = vc_ref[0, 0, :ncols, :]
        cend = lax.broadcasted_iota(i32, (1, ncols), 1) * CMP_STRIDE + (CMP_BLOCK - 1)
        bias = jnp.where(cend <= tq, 0.0, MASK_NEG)
        s = _dot(q_ref[0].reshape(rows, HEAD_DIM), kct).reshape(NSA_REP, CMP_TQ, ncols) + bias[None]
        e = jnp.exp(s - jnp.max(s, axis=-1, keepdims=True))
        live = row_live[None]
        p = e * (live / jnp.maximum(live * jnp.sum(e, axis=-1, keepdims=True), tiny))
        oc_ref[0] = _dot(p.reshape(rows, ncols).astype(bf16), vc).reshape(NSA_REP, CMP_TQ, HEAD_DIM)
        psum = jnp.sum(p, axis=0)
        hi = psum.astype(bf16)
        lo = (psum - hi.astype(f32)).astype(bf16)
        cover = cover_ref[:ncols, :]
        imp = _dot(hi, cover) + _dot(lo, cover)

        j = lax.broadcasted_iota(i32, (CMP_TQ, LANES), 1)
        cur = tq // SEL_BLOCK
        forced = (j == 0) | (j == cur) | (j == cur - 1)
        valid = j * SEL_BLOCK <= tq
        imp = jnp.where(valid, jnp.where(forced, FORCED_SCORE, imp), -FORCED_SCORE)

        nblk = ncols * CMP_STRIDE // SEL_BLOCK
        jt = lax.broadcasted_iota(i32, (nblk, CMP_TQ), 0)

        def pick(_, carry):
            work, sel = carry
            m = jnp.max(work, axis=0, keepdims=True)
            first = jnp.min(jnp.where(work == m, jt, LANES), axis=0, keepdims=True)
            hit = jt == first
            return jnp.where(hit, -jnp.inf, work), jnp.where(hit, 1.0, sel)

        _, sel_t = lax.fori_loop(0, SEL_TOPK, pick, (imp.T[:nblk], jnp.zeros((nblk, CMP_TQ), f32)))
        if nblk < LANES:
            sel_t = jnp.concatenate([sel_t, jnp.zeros((LANES - nblk, CMP_TQ), f32)], axis=0)
        sel_ref[0, 0] = jnp.where(valid, jnp.where(sel_t.T > 0.0, 0.0, MASK_NEG), MASK_NEG).astype(bf16)

    need = (qi + 1) * (CMP_TQ // CMP_STRIDE)
    for ncols in range(LANES, nc + 1, LANES):
        @pl.when((need > ncols - LANES) & (need <= ncols))
        def _():
            attend(ncols)


def _cmp_attn(q_r, kvc, kvc_t, cover, b, t):
    g = NSA_KV_GROUPS
    nc = kvc.shape[2]
    nq = t // CMP_TQ
    return pl.pallas_call(
        _cmp_attn_kernel,
        grid=(b, g, nq),
        in_specs=[pl.BlockSpec((1, NSA_REP, CMP_TQ, HEAD_DIM), lambda bi, gi, qi: (bi, gi, qi, 0)),
                  pl.BlockSpec((1, 1, HEAD_DIM, nc), lambda bi, gi, qi: (0, bi * g + gi, 0, 0)),
                  pl.BlockSpec((1, 1, nc, HEAD_DIM), lambda bi, gi, qi: (1, bi * g + gi, 0, 0)),
                  pl.BlockSpec((nc, LANES), lambda bi, gi, qi: (0, 0))],
        out_specs=[pl.BlockSpec((1, NSA_REP, CMP_TQ, HEAD_DIM), lambda bi, gi, qi: (bi, gi, qi, 0)),
                   pl.BlockSpec((1, 1, CMP_TQ, LANES), lambda bi, gi, qi: (bi, gi, qi, 0))],
        out_shape=[jax.ShapeDtypeStruct((b, NSA_HEADS, t, HEAD_DIM), f32),
                   jax.ShapeDtypeStruct((b, g, t, LANES), bf16)],
        compiler_params=_cparams(("parallel", "parallel", "parallel")),
        name="cmp_attn",
    )(q_r, kvc_t, kvc, cover)


def _nsa_attn_kernel(q_ref, kx_ref, vs_ref, kw_ref, vw_ref, sel_ref, oc_ref, gate_ref,
                     o_ref, qx_ref, s_ref, m_ref, acc_ref, yw_ref):
    gi = pl.program_id(1)
    qi = pl.program_id(2)
    start = qi * ATT_TQ
    tq = start + lax.broadcasted_iota(i32, (ATT_TQ, 1), 0)

    selb = sel_ref[0, 0]
    for r in range(NSA_REP):
        qx_ref[r * ATT_TQ:(r + 1) * ATT_TQ, 0:LANES] = selb
        qr = q_ref[0, r]
        qx_ref[r * ATT_TQ:(r + 1) * ATT_TQ, LANES:2 * LANES] = jnp.concatenate([qr, jnp.zeros_like(qr)], axis=1)
    rows = NSA_REP * ATT_TQ

    m_ref[...] = jnp.full(m_ref.shape, MASK_NEG, f32)
    acc_ref[...] = jnp.zeros(acc_ref.shape, f32)

    def consume(kt, v_ref, bias, parts=1):
        k0 = pl.multiple_of(kt * ATT_TK, ATT_TK)
        v = v_ref[0, 0, pl.ds(k0, ATT_TK), :]
        heads = NSA_REP // parts
        for part in range(parts):
            rs = slice(part * heads * ATT_TQ, (part + 1) * heads * ATT_TQ)
            s = s_ref[rs]
            if bias is not None:
                s = (s.reshape(heads, ATT_TQ, ATT_TK) + bias[None]).reshape(heads * ATT_TQ, ATT_TK)
            m_prev = m_ref[rs]
            m_new = jnp.maximum(m_prev, jnp.max(s, axis=-1, keepdims=True))
            alpha = jnp.exp(m_prev - m_new)
            p = jnp.exp(s - jnp.concatenate([m_new] * (ATT_TK // LANES), axis=1))
            acc_ref[rs] = alpha * acc_ref[rs] + _dot(p.astype(bf16), v)
            m_ref[rs] = m_new

    def kpos(kt):
        return kt * ATT_TK + lax.broadcasted_iota(i32, (1, ATT_TK), 1)

    last = start // ATT_TK
    for part in range(2):
        rs = slice(part * rows // 2, (part + 1) * rows // 2)
        s_ref[rs] = _dot(qx_ref[rs], kx_ref[0, 0, 0])

    def sel_step(kt, _):
        consume(kt, vs_ref, None)
        s_ref[...] = _dot(qx_ref[...], kx_ref[0, 0, kt + 1])
        return 0

    n_win = (WINDOW + ATT_TQ) // WIN_TK
    w_first = jnp.maximum(start // WIN_TK - WINDOW // WIN_TK, 0)
    kw = jnp.concatenate([kw_ref[0, 0, w_first + i] for i in range(n_win)], axis=1)
    w0 = pl.multiple_of(w_first * WIN_TK, WIN_TK)
    vw = vw_ref[0, 0, pl.ds(w0, n_win * WIN_TK), :]
    kp = w0 + lax.broadcasted_iota(i32, (1, n_win * WIN_TK), 1)
    wbias = jnp.where(kp <= tq, jnp.where(kp > tq - WINDOW, 0.0, MASK_NEG), MASK_NEG)
    hh = WIN_HEADS
    gates = jax.nn.sigmoid(gate_ref[...])
    per_group = 3 * NSA_REP
    shifted = gates
    for gg in range(1, NSA_KV_GROUPS):
        shifted = jnp.where(gi == gg, pltpu.roll(gates, LANES - gg * per_group, 1), shifted)

    def gate(r, br):
        c = r * 3 + br
        return jnp.broadcast_to(shifted[:, c:c + 1], (ATT_TQ, HEAD_DIM))

    upper = pltpu.roll(shifted, HEAD_DIM, 1)[:, HEAD_DIM:]

    def gated_output(a, r, br):
        c = r * 3 + br
        ratio = upper / a[:, HEAD_DIM:]
        return a[:, :HEAD_DIM] * jnp.broadcast_to(ratio[:, c:c + 1], (ATT_TQ, HEAD_DIM))

    for half in range(NSA_REP // hh):
        qh = q_ref[0, half * hh:(half + 1) * hh].reshape(hh * ATT_TQ, HEAD_DIM)
        sw = _dot(qh, kw).reshape(hh, ATT_TQ, n_win * WIN_TK) + wbias[None]
        sw = sw.reshape(hh * ATT_TQ, n_win * WIN_TK)
        pw = jnp.exp(sw - jnp.max(sw, axis=-1, keepdims=True))
        aw = _dot(pw.astype(bf16), vw)
        for i in range(hh):
            r = half * hh + i
            a = aw[i * ATT_TQ:(i + 1) * ATT_TQ]
            yw_ref[r] = gated_output(a, r, 2)

    def sel_pair(j, _):
        sel_step(2 * j, 0)
        sel_step(2 * j + 1, 0)
        return 0

    lax.fori_loop(0, last // 2, sel_pair, 0)

    @pl.when(last % 2 == 1)
    def _():
        sel_step(last - 1, 0)

    consume(last, vs_ref, jnp.where(kpos(last) <= tq, 0.0, MASK_NEG), parts=2)

    for r in range(NSA_REP):
        osel = gated_output(acc_ref[r * ATT_TQ:(r + 1) * ATT_TQ], r, 1)
        o_ref[:, r * HEAD_DIM:(r + 1) * HEAD_DIM] = (gate(r, 0) * oc_ref[0, r] + osel + yw_ref[r]).astype(o_ref.dtype)


def _nsa_attn(q_r, kx, vs, kw, vw, selb, oc, proj, b, t):
    g = NSA_KV_GROUPS
    nq = t // ATT_TQ
    once = dict(pipeline_mode=pl.Buffered(1))
    vspec = pl.BlockSpec((1, 1, t, LANES), lambda bi, gi, qi: (bi, gi, 0, 0), **once)
    hspec = pl.BlockSpec((1, NSA_REP, ATT_TQ, HEAD_DIM), lambda bi, gi, qi: (bi, gi, qi, 0))
    return pl.pallas_call(
        _nsa_attn_kernel,
        grid=(b, g, nq),
        in_specs=[hspec,
                  pl.BlockSpec((1, 1, t // ATT_TK, KX_WIDTH, ATT_TK), lambda bi, gi, qi: (bi, gi, 0, 0, 0), **once),
                  vspec,
                  pl.BlockSpec((1, 1, t // WIN_TK, HEAD_DIM, WIN_TK), lambda bi, gi, qi: (bi, gi, 0, 0, 0), **once),
                  vspec,
                  pl.BlockSpec((1, 1, ATT_TQ, LANES), lambda bi, gi, qi: (bi, gi, qi, 0)),
                  hspec,
                  pl.BlockSpec((ATT_TQ, LANES), lambda bi, gi, qi: (bi * nq + qi, COL_SMALL // LANES))],
        out_specs=pl.BlockSpec((ATT_TQ, NSA_REP * HEAD_DIM), lambda bi, gi, qi: (bi * nq + qi, gi)),
        out_shape=jax.ShapeDtypeStruct((b * t, NSA_WIDTH), bf16),
        scratch_shapes=[pltpu.VMEM((NSA_REP * ATT_TQ, KX_WIDTH), bf16),
                        pltpu.VMEM((NSA_REP * ATT_TQ, ATT_TK), f32),
                        pltpu.VMEM((NSA_REP * ATT_TQ, LANES), f32),
                        pltpu.VMEM((NSA_REP * ATT_TQ, LANES), f32),
                        pltpu.VMEM((NSA_REP, ATT_TQ, HEAD_DIM), f32)],
        compiler_params=_cparams(("parallel", "parallel", "arbitrary")),
        name="nsa_attn",
    )(q_r, kx, vs, kw, vw, selb, oc, proj)


HALO = 8


def _ssd_kernel(xbc_ref, z_ref, small_ref, dtt_ref, cw_ref, cb_ref, dtb_r_ref, dtb_c_ref,
                alog_r_ref, alog_c_ref, dskip_ref, nw_ref, o_ref, ext_ref, st_ref):
    c = pl.program_id(1)
    L = CHUNK

    @pl.when(c == 0)
    def _():
        ext_ref[0:HALO, :] = jnp.zeros((HALO, XBC_WIDTH), f32)
        st_ref[...] = jnp.zeros_like(st_ref)

    ext_ref[HALO:HALO + L, :] = xbc_ref[...]
    conv = cb_ref[...]
    for k in range(CONV_WIDTH):
        off = HALO - (CONV_WIDTH - 1) + k
        conv = conv + cw_ref[k:k + 1, :] * ext_ref[off:off + L, :]
    ext_ref[0:HALO, :] = ext_ref[L:L + HALO, :]
    act = conv * jax.nn.sigmoid(conv)
    xs = act[:, :SSM_WIDTH]
    bm = act[:, SSM_WIDTH:SSM_WIDTH + SSM_GROUPS * SSM_STATE]
    cm = act[:, SSM_WIDTH + SSM_GROUPS * SSM_STATE:]

    dt_c = jax.nn.softplus(small_ref[:, SMALL_DT_OFF:SMALL_DT_OFF + SSM_HEADS] + dtb_r_ref[...])
    dt_r = jax.nn.softplus(dtt_ref[0] + dtb_c_ref[...])
    a_r = -jnp.exp(alog_r_ref[...])
    a_c = -jnp.exp(alog_c_ref[...])
    row = lax.broadcasted_iota(i32, (L, L), 0)
    col = lax.broadcasted_iota(i32, (L, L), 1)
    causal = col <= row
    tri = jnp.where(causal, 1.0, 0.0)
    acs_c = _dot_hi(tri, dt_c * a_r)
    acs_r = _dot_hi(dt_r * a_c, jnp.where(row <= col, 1.0, 0.0))

    z = z_ref[...]
    ys = []
    for g in range(SSM_GROUPS):
        cg = cm[:, g * SSM_STATE:(g + 1) * SSM_STATE].astype(bf16)
        bg = bm[:, g * SSM_STATE:(g + 1) * SSM_STATE]
        cb = _dot_t(cg, bg.astype(bf16))
        bg_t = bg.T
        ssq = jnp.zeros((L, 1), f32)
        yg = []
        for r in range(SSM_REP):
            h = g * SSM_REP + r
            hs = slice(h * SSM_HEADDIM, (h + 1) * SSM_HEADDIM)
            a_col = acs_c[:, h:h + 1]
            a_row = acs_r[h:h + 1, :]
            dt_row = dt_r[h:h + 1, :]
            a_last = acs_r[h:h + 1, L - 1:L]
            seg = a_col - a_row
            decay = jnp.where(causal, jnp.exp(jnp.where(causal, seg, 0.0)), 0.0)
            w = cb * decay * dt_row
            x_h = xs[:, hs]
            xb = x_h.astype(bf16)
            st = st_ref[h]
            y = _dot(w.astype(bf16), xb)
            y = y + _dot(cg, st.astype(bf16)) * jnp.exp(a_col)
            y = y + dskip_ref[:, hs] * x_h
            bscaled = bg_t * (jnp.exp(a_last - a_row) * dt_row)
            st_ref[h] = jnp.exp(a_last) * st + _dot(bscaled.astype(bf16), xb)
            zh = z[:, hs]
            y = y * (zh * jax.nn.sigmoid(zh))
            ssq = ssq + jnp.sum(y * y, axis=-1, keepdims=True)
            yg.append(y)
        rs = lax.rsqrt(ssq / (SSM_REP * SSM_HEADDIM) + NORM_EPS)
        for r in range(SSM_REP):
            h = g * SSM_REP + r
            hs = slice(h * SSM_HEADDIM, (h + 1) * SSM_HEADDIM)
            o_ref[:, hs] = (yg[r] * rs * nw_ref[:, hs]).astype(o_ref.dtype)


def _ssd(proj, dtt, cw, cb, dtb_r, dtb_c, alog_r, alog_c, dskip, nw, b, t):
    nch = t // CHUNK
    row = lambda bi, ci: bi * nch + ci
    const2 = lambda bi, ci: (0, 0)
    return pl.pallas_call(
        _ssd_kernel,
        grid=(b, nch),
        in_specs=[pl.BlockSpec((CHUNK, XBC_WIDTH), lambda bi, ci: (row(bi, ci), COL_XBC // XBC_WIDTH)),
                  pl.BlockSpec((CHUNK, SSM_WIDTH), lambda bi, ci: (row(bi, ci), COL_Z // SSM_WIDTH)),
                  pl.BlockSpec((CHUNK, LANES), lambda bi, ci: (row(bi, ci), COL_SMALL // LANES)),
                  pl.BlockSpec((1, SSM_HEADS, CHUNK), lambda bi, ci: (bi, 0, ci)),
                  pl.BlockSpec((CONV_WIDTH, XBC_WIDTH), const2),
                  pl.BlockSpec((1, XBC_WIDTH), const2),
                  pl.BlockSpec((1, SSM_HEADS), const2),
                  pl.BlockSpec((SSM_HEADS, 1), const2),
                  pl.BlockSpec((1, SSM_HEADS), const2),
                  pl.BlockSpec((SSM_HEADS, 1), const2),
                  pl.BlockSpec((1, SSM_WIDTH), const2),
                  pl.BlockSpec((1, SSM_WIDTH), const2)],
        out_specs=pl.BlockSpec((CHUNK, SSM_WIDTH), lambda bi, ci: (row(bi, ci), 0)),
        out_shape=jax.ShapeDtypeStruct((b * t, SSM_WIDTH), bf16),
        scratch_shapes=[pltpu.VMEM((HALO + CHUNK, XBC_WIDTH), f32),
                        pltpu.VMEM((SSM_HEADS, SSM_STATE, SSM_HEADDIM), f32)],
        compiler_params=_cparams(("parallel", "arbitrary")),
        name="ssd",
    )(proj, proj, proj, dtt, cw, cb, dtb_r, dtb_c, alog_r, alog_c, dskip, nw)


def _layer_norm(v, g, b):
    mu = jnp.mean(v, axis=-1, keepdims=True)
    d = v - mu
    var = jnp.mean(d * d, axis=-1, keepdims=True)
    return d * lax.rsqrt(var + NORM_EPS) * g + b


def _outproj_kernel(alpha, ya_ref, yb_ref, x_ref, wa_ref, wb_ref, g_ref, b_ref, wr_ref, br_ref,
                    h_ref, route_ref, cnt_ref):
    i = pl.program_id(0)
    tm = x_ref.shape[0]
    mix = _dot(ya_ref[...], wa_ref[...]) + _dot(yb_ref[...], wb_ref[...])
    h = _layer_norm(alpha * x_ref[...] + mix, g_ref[...], b_ref[...])
    h_ref[...] = h

    h_hi = h.astype(bf16)
    h_lo = (h - h_hi.astype(f32)).astype(bf16)
    t = _dot(h_hi, wr_ref[...])
    logits = t[:, :LANES] + t[:, LANES:] + _dot(h_lo, wr_ref[:, :LANES]) + br_ref[...]
    lane = lax.broadcasted_iota(i32, (tm, LANES), 1)
    ninf = -jnp.inf
    gmask = lane < N_EXPERT_GROUPS
    gl = jnp.where(gmask, logits, ninf)
    ge = jnp.where(gmask, jnp.exp(gl - jnp.max(gl, axis=-1, keepdims=True)), 0.0)
    pg = ge / jnp.sum(ge, axis=-1, keepdims=True)
    g_gate = jnp.max(pg, axis=-1, keepdims=True)
    g_sel = jnp.min(jnp.where(gmask & (pg == g_gate), lane, LANES), axis=-1, keepdims=True)
    lo = N_EXPERT_GROUPS + g_sel * EXPERTS_PER_GROUP
    emask = (lane >= lo) & (lane < lo + EXPERTS_PER_GROUP)
    el = jnp.where(emask, logits, ninf)
    ee = jnp.where(emask, jnp.exp(el - jnp.max(el, axis=-1, keepdims=True)), 0.0)
    pe = ee / jnp.sum(ee, axis=-1, keepdims=True)
    p0 = jnp.max(pe, axis=-1, keepdims=True)
    l0 = jnp.min(jnp.where(emask & (pe == p0), lane, LANES), axis=-1, keepdims=True)
    rest = jnp.where(emask & (lane != l0), pe, ninf)
    p1 = jnp.max(rest, axis=-1, keepdims=True)
    l1 = jnp.min(jnp.where(rest == p1, lane, LANES), axis=-1, keepdims=True)
    psum = p0 + p1
    w0 = g_gate * p0 / psum
    w1 = g_gate * p1 / psum
    e0 = l0 - N_EXPERT_GROUPS
    e1 = l1 - N_EXPERT_GROUPS

    @pl.when(i == 0)
    def _():
        cnt_ref[...] = jnp.zeros_like(cnt_ref)

    oh0 = lane == e0
    oh1 = lane == e1
    both = jnp.where(oh0, 1.0, 0.0) + jnp.where(oh1, 1.0, 0.0)
    r_i = lax.broadcasted_iota(i32, (tm, tm), 0)
    c_i = lax.broadcasted_iota(i32, (tm, tm), 1)
    strict = jnp.where(c_i < r_i, 1.0, 0.0).astype(bf16)
    before = _dot(strict, both.astype(bf16)) + cnt_ref[...]
    rank0 = jnp.sum(jnp.where(oh0, before, 0.0), axis=-1, keepdims=True)
    rank1 = jnp.sum(jnp.where(oh1, before, 0.0), axis=-1, keepdims=True)
    cnt_ref[...] = cnt_ref[...] + jnp.sum(both, axis=0, keepdims=True)

    out = jnp.where(lane == 0, e0.astype(f32), 0.0)
    out = jnp.where(lane == 1, e1.astype(f32), out)
    out = jnp.where(lane == 2, w0, out)
    out = jnp.where(lane == 3, w1, out)
    out = jnp.where(lane == 4, rank0, out)
    out = jnp.where(lane == 5, rank1, out)
    route_ref[...] = out


def _outproj(y_nsa, y_ssm, x2, wa, wb, g, bta, wr, br, alpha):
    n, d = x2.shape
    tm = OUT_TM
    const = lambda i: (0, 0)
    rowb = lambda i: (i, 0)
    return pl.pallas_call(
        functools.partial(_outproj_kernel, alpha),
        grid=(n // tm,),
        in_specs=[pl.BlockSpec((tm, NSA_WIDTH), rowb), pl.BlockSpec((tm, SSM_WIDTH), rowb),
                  pl.BlockSpec((tm, d), rowb),
                  pl.BlockSpec((NSA_WIDTH, d), const), pl.BlockSpec((SSM_WIDTH, d), const),
                  pl.BlockSpec((1, d), const), pl.BlockSpec((1, d), const),
                  pl.BlockSpec((d, 2 * LANES), const), pl.BlockSpec((1, LANES), const)],
        out_specs=[pl.BlockSpec((tm, d), rowb), pl.BlockSpec((tm, LANES), rowb),
                   pl.BlockSpec((1, LANES), const)],
        out_shape=[jax.ShapeDtypeStruct((n, d), f32), jax.ShapeDtypeStruct((n, LANES), f32),
                   jax.ShapeDtypeStruct((1, LANES), f32)],
        compiler_params=_cparams(("arbitrary",)),
        name="outproj",
    )(y_nsa, y_ssm, x2, wa, wb, g, bta, wr, br)


def _row_copy(src_ref, src_row, dst_ref, dst_row, sem):
    return pltpu.make_async_copy(src_ref.at[pl.ds(src_row, 1), :], dst_ref.at[pl.ds(dst_row, 1), :], sem)


def _dest_row(dest_ref, tm, k, r):
    return dest_ref[0, 0, k * tm + r]


def _dispatch_kernel(dest_ref, zflag_ref, h_ref, xs_ref, zero_ref, tile_ref, sem, lsem, zsem):
    tm = DISP_TM
    nb = zflag_ref.shape[0]
    i = pl.program_id(0)

    @pl.when(i == 0)
    def _():
        zero_ref[...] = jnp.zeros_like(zero_ref)

        def zblock(i):
            rows = pl.ds(pl.multiple_of(i * MOE_TM, MOE_TM), MOE_TM)
            return pltpu.make_async_copy(zero_ref, xs_ref.at[rows, :], zsem)

        def zstart(i, _):
            @pl.when(zflag_ref[i] != 0)
            def _():
                zblock(i).start()
            return 0

        def zwait(i, _):
            @pl.when(zflag_ref[i] != 0)
            def _():
                zblock(i).wait()
            return 0

        lax.fori_loop(0, nb, zstart, 0)
        lax.fori_loop(0, nb, zwait, 0)

    nslot = tile_ref.shape[0]
    nsteps = pl.num_programs(0)
    slot = i % nslot

    def tile_load(step, s):
        rows = pl.ds(pl.multiple_of(step * tm, tm), tm)
        return pltpu.make_async_copy(h_ref.at[rows, :], tile_ref.at[s], lsem.at[s])

    @pl.when(i == 0)
    def _():
        tile_load(0, 0).start()

    tile_load(i, slot).wait()

    @pl.when(i + 1 < nsteps)
    def _():
        tile_load(i + 1, (i + 1) % nslot).start()

    def issue(r, _):
        for k in range(2):
            _row_copy(tile_ref.at[slot], r, xs_ref, _dest_row(dest_ref, tm, k, r), sem.at[slot]).start()
        return 0

    lax.fori_loop(0, tm, issue, 0, unroll=16)

    def wait_rows(s):
        for k in range(2):
            pltpu.make_async_copy(tile_ref.at[s], xs_ref.at[pl.ds(0, tm), :], sem.at[s]).wait()

    @pl.when(i > 0)
    def _():
        wait_rows((i - 1) % nslot)

    @pl.when(i == nsteps - 1)
    def _():
        wait_rows(slot)


def _dispatch(dest_t, zflag, h, p_rows):
    n, d = h.shape
    tm = DISP_TM
    return pl.pallas_call(
        _dispatch_kernel,
        grid=(n // tm,),
        in_specs=[pl.BlockSpec((1, 1, 2 * tm), lambda i: (i, 0, 0), memory_space=pltpu.SMEM),
                  pl.BlockSpec(memory_space=pltpu.SMEM),
                  pl.BlockSpec(memory_space=pl.ANY)],
        out_specs=pl.BlockSpec(memory_space=pl.ANY),
        out_shape=jax.ShapeDtypeStruct((p_rows, d), f32),
        scratch_shapes=[pltpu.VMEM((MOE_TM, d), f32), pltpu.VMEM((3, tm, d), f32),
                        pltpu.SemaphoreType.DMA((3,)), pltpu.SemaphoreType.DMA((3,)), pltpu.SemaphoreType.DMA(())],
        compiler_params=_cparams(("arbitrary",)),
        name="dispatch",
    )(dest_t, zflag, h)


def _experts_kernel(be_ref, nu_ref, first_ref, next_ref, slot_ref, xs_ref, wg_ref, wu_ref, wd_ref, y_ref,
                    fg_ref, fu_ref, fd_ref, wgb_ref, wub_ref, wdb_ref, sems):
    i = pl.program_id(0)

    def weight_loads(e, s):
        return (pltpu.make_async_copy(wg_ref.at[e], fg_ref.at[s], sems.at[s, 0]),
                pltpu.make_async_copy(wu_ref.at[e], fu_ref.at[s], sems.at[s, 1]),
                pltpu.make_async_copy(wd_ref.at[e], fd_ref.at[s], sems.at[s, 2]))

    @pl.when(i == 0)
    def _():
        for c in weight_loads(be_ref[0], 0):
            c.start()

    @pl.when(first_ref[i] != 0)
    def _():
        s = slot_ref[i]
        for c in weight_loads(be_ref[i], s):
            c.wait()
        wgb_ref[...] = fg_ref[s].astype(bf16)
        wub_ref[...] = fu_ref[s].astype(bf16)
        wdb_ref[...] = fd_ref[s].astype(bf16)

        @pl.when(next_ref[i] >= 0)
        def _():
            for c in weight_loads(next_ref[i], 1 - s):
                c.start()

    @pl.when(i < nu_ref[0])
    def _():
        xb = xs_ref[...].astype(bf16)
        gte = _dot(xb, wgb_ref[...])
        up = _dot(xb, wub_ref[...])
        act = gte * jax.nn.sigmoid(gte) * up
        y_ref[...] = _dot(act.astype(bf16), wdb_ref[...])

    @pl.when(i >= nu_ref[0])
    def _():
        y_ref[...] = jnp.zeros_like(y_ref)


def _experts(block_e, n_used, first, next_e, slot, xs, w_gate, w_up, w_down):
    p_rows, d = xs.shape
    tm = MOE_TM
    nb = p_rows // tm
    de = w_gate.shape[-1]
    xmap = lambda i, be, nu, fi, ne, sl: (jnp.maximum(jnp.minimum(i, nu[0] - 1), 0), 0)
    hbm = pl.BlockSpec(memory_space=pl.ANY)
    return pl.pallas_call(
        _experts_kernel,
        grid_spec=pltpu.PrefetchScalarGridSpec(
            num_scalar_prefetch=5,
            grid=(nb,),
            in_specs=[pl.BlockSpec((tm, d), xmap), hbm, hbm, hbm],
            out_specs=pl.BlockSpec((tm, d), lambda i, be, nu, fi, ne, sl: (i, 0)),
            scratch_shapes=[pltpu.VMEM((2, d, de), f32), pltpu.VMEM((2, d, de), f32), pltpu.VMEM((2, de, d), f32),
                            pltpu.VMEM((d, de), bf16), pltpu.VMEM((d, de), bf16), pltpu.VMEM((de, d), bf16),
                            pltpu.SemaphoreType.DMA((2, 3))],
        ),
        out_shape=jax.ShapeDtypeStruct((p_rows, d), f32),
        compiler_params=_cparams(("arbitrary",)),
        name="experts",
    )(block_e, n_used, first, next_e, slot, xs, w_gate, w_up, w_down)


def _combine_kernel(alpha, dest_ref, ndest_ref, route_ref, h_ref, g_ref, b_ref, y_ref, o_ref, buf_ref, sems):
    tm = h_ref.shape[0]
    i = pl.program_id(0)
    slot = i % 2

    def start_row(ids_ref, s, r):
        for k in range(2):
            _row_copy(y_ref, _dest_row(ids_ref, tm, k, r), buf_ref.at[s, k], r, sems.at[s]).start()

    def wait_tile():
        for k in range(2):
            pltpu.make_async_copy(y_ref.at[pl.ds(0, tm), :], buf_ref.at[slot, k], sems.at[slot]).wait()

    def finish_tile():
        route = route_ref[...]
        ffn = route[:, 2:3] * buf_ref[slot, 0] + route[:, 3:4] * buf_ref[slot, 1]
        o_ref[...] = _layer_norm(alpha * h_ref[...] + ffn, g_ref[...], b_ref[...])

    @pl.when(i == 0)
    def _():
        def issue(r, _):
            start_row(dest_ref, 0, r)
            return 0
        lax.fori_loop(0, tm, issue, 0, unroll=4)

    @pl.when(i + 1 < pl.num_programs(0))
    def _():
        wait_tile()
        for r in range(tm):
            start_row(ndest_ref, 1 - slot, r)
        finish_tile()

    @pl.when(i + 1 == pl.num_programs(0))
    def _():
        wait_tile()
        finish_tile()


def _combine(dest_t, route, h, g, bta, y, alpha):
    n, d = h.shape
    tm = COMB_TM
    const = lambda i: (0, 0)
    return pl.pallas_call(
        functools.partial(_combine_kernel, alpha),
        grid=(n // tm,),
        in_specs=[pl.BlockSpec((1, 1, 2 * tm), lambda i: (i, 0, 0), memory_space=pltpu.SMEM),
                  pl.BlockSpec((1, 1, 2 * tm), lambda i: (jnp.minimum(i + 1, n // tm - 1), 0, 0),
                               memory_space=pltpu.SMEM),
                  pl.BlockSpec((tm, LANES), lambda i: (i, 0)),
                  pl.BlockSpec((tm, d), lambda i: (i, 0)),
                  pl.BlockSpec((1, d), const), pl.BlockSpec((1, d), const),
                  pl.BlockSpec(memory_space=pl.ANY)],
        out_specs=pl.BlockSpec((tm, d), lambda i: (i, 0)),
        out_shape=jax.ShapeDtypeStruct((n, d), f32),
        scratch_shapes=[pltpu.VMEM((2, 2, tm, d), f32), pltpu.SemaphoreType.DMA((2,))],
        compiler_params=_cparams(("arbitrary",)),
        name="combine",
    )(dest_t, dest_t, route, h, g, bta, y)


def _tile_dest(dest, tm):
    n = dest.shape[1]
    return dest.reshape(2, n // tm, tm).transpose(1, 0, 2).reshape(n // tm, 1, 2 * tm)


def _layer(x, positions, w_in, cmp_k_pe, cmp_k_w1, cmp_k_b1, cmp_k_w2, cmp_v_pe, cmp_v_w1, cmp_v_b1, cmp_v_w2,
           conv_w, conv_b, dt_bias, a_log, d_skip, ssm_norm_w, w_out, ln1_g, ln1_b,
           w_router_group, b_router_group, w_router_expert, b_router_expert, w_gate, w_up, w_down, ln2_g, ln2_b,
           alpha):
    b, t, d = x.shape
    n = b * t
    assert t % max(ATT_TK, ATT_TQ, CMP_TQ, CHUNK, PREP_TM) == 0 and n % max(PROJ_TM, OUT_TM, DISP_TM) == 0
    assert t // SEL_BLOCK <= LANES and (t // CMP_STRIDE) % LANES == 0 and t >= WINDOW + ATT_TQ
    x2 = x.reshape(n, d)

    c0 = NSA_WIDTH
    c1 = c0 + 6 * KV_WIDTH
    c2 = c1 + 3 * NSA_HEADS
    c3 = c2 + SSM_WIDTH
    c4 = c3 + XBC_WIDTH
    w_small = jnp.concatenate([w_in[:, c1:c2], w_in[:, c4:], jnp.zeros((d, LANES - 3 * NSA_HEADS - SSM_HEADS), f32)], axis=1)
    w_cat = jnp.concatenate([w_in[:, :c0], w_in[:, c2:c3], w_in[:, c3:c4], w_in[:, c0:c1], w_small,
                             jnp.zeros((d, PROJ_COLS - COL_SMALL - LANES), f32)], axis=1).astype(bf16)
    proj = _proj(x2, w_cat)

    lane = np.arange(LANES) % HEAD_DIM
    inv_freq = ROPE_THETA ** (-jnp.arange(0, ROT_DIM, 2, dtype=f32) / ROT_DIM)
    invf = jnp.where(lane < ROT_DIM, inv_freq[lane % (ROT_DIM // 2)], 0.0).astype(f32)[None, :]
    pos128 = jnp.broadcast_to(positions.reshape(n, 1), (n, LANES))
    q_r, k_cmp, v_cmp, k_sel, v_sel, k_win, v_win = _nsa_prep(proj, pos128, invf, b, t)

    nc = t // CMP_STRIDE
    half_w = CMP_STRIDE * HEAD_DIM
    a = jnp.stack([k_cmp, v_cmp]).reshape(2, b * NSA_KV_GROUPS, nc, half_w)
    pe = jnp.stack([cmp_k_pe, cmp_v_pe]).reshape(2, 2, 1, half_w)
    w1 = jnp.stack([cmp_k_w1, cmp_v_w1]).reshape(2, 2, half_w, CMP_HIDDEN).astype(bf16)
    b1 = jnp.stack([cmp_k_b1, cmp_v_b1]).reshape(2, 1, CMP_HIDDEN)
    w2 = jnp.pad(jnp.stack([cmp_k_w2, cmp_v_w2]), ((0, 0), (0, 0), (0, LANES - HEAD_DIM))).astype(bf16)
    cend = jnp.minimum(jnp.arange(nc) * CMP_STRIDE + CMP_BLOCK - 1, t - 1)
    posc = jnp.broadcast_to(positions[:, cend][:, :, None], (b, nc, LANES))
    kvc, kvc_t = _cmp_mlp(a, pe, w1, b1, w2, posc, invf, b)

    c_start = np.arange(nc)[:, None] * CMP_STRIDE
    s_start = np.arange(LANES)[None, :] * SEL_BLOCK
    cover = ((c_start < s_start + SEL_BLOCK) & (c_start + CMP_BLOCK > s_start)
             & (np.arange(nc)[:, None] < nc - 1) & (np.arange(LANES)[None, :] < t // SEL_BLOCK))
    cover = jnp.asarray(cover, bf16)
    o_cmp, selb = _cmp_attn(q_r, kvc, kvc_t, cover, b, t)

    y_nsa = _nsa_attn(q_r, k_sel, v_sel, k_win, v_win, selb, o_cmp, proj, b, t)

    dt_raw = proj[:, COL_SMALL + SMALL_DT_OFF:COL_SMALL + SMALL_DT_OFF + SSM_HEADS]
    dtt = dt_raw.reshape(b, t, SSM_HEADS).transpose(0, 2, 1)
    y_ssm = _ssd(proj, dtt, conv_w.reshape(CONV_WIDTH, XBC_WIDTH), conv_b.reshape(1, XBC_WIDTH),
                 dt_bias.reshape(1, SSM_HEADS), dt_bias.reshape(SSM_HEADS, 1),
                 a_log.reshape(1, SSM_HEADS), a_log.reshape(SSM_HEADS, 1),
                 jnp.repeat(d_skip, SSM_HEADDIM).reshape(1, SSM_WIDTH), ssm_norm_w.reshape(1, SSM_WIDTH), b, t)

    wr = jnp.concatenate([w_router_group, w_router_expert,
                          jnp.zeros((d, LANES - N_EXPERT_GROUPS - N_EXPERTS), f32)], axis=1)
    br = jnp.concatenate([b_router_group, b_router_expert,
                          jnp.zeros((LANES - N_EXPERT_GROUPS - N_EXPERTS,), f32)])[None, :]
    wr_hi = wr.astype(bf16)
    wr = jnp.concatenate([wr_hi, (wr - wr_hi.astype(f32)).astype(bf16)], axis=1)
    wo = w_out.astype(bf16)
    h, route, counts = _outproj(y_nsa, y_ssm, x2, wo[:NSA_WIDTH], wo[NSA_WIDTH:], ln1_g[None, :], ln1_b[None, :],
                                wr, br, alpha)

    cnt = counts[0, :N_EXPERTS].astype(i32)
    padded = (cnt + MOE_TM - 1) // MOE_TM * MOE_TM
    pad_ends = jnp.cumsum(padded)
    pad_starts = pad_ends - padded
    e01 = route[:, 0:2].astype(i32).T
    start01 = jnp.sum(jnp.where(e01[..., None] == jnp.arange(N_EXPERTS, dtype=i32), pad_starts, 0), axis=-1)
    dest = start01 + route[:, 4:6].astype(i32).T
    p_rows = 2 * n + N_EXPERTS * MOE_TM
    nb = p_rows // MOE_TM
    block_e = jnp.minimum(jnp.sum(jnp.arange(nb, dtype=i32)[:, None] * MOE_TM >= pad_ends[None, :], axis=-1),
                          N_EXPERTS - 1).astype(i32)
    n_used = (pad_ends[-1] // MOE_TM).astype(i32).reshape(1)
    blk = jnp.arange(nb, dtype=i32)
    last_of_expert = jnp.any((blk[:, None] + 1) * MOE_TM == pad_ends[None, :], axis=-1)
    zflag = (last_of_expert | (blk >= n_used[0])).astype(i32)
    block_e = jnp.where(blk < n_used[0], block_e, block_e[jnp.maximum(n_used[0] - 1, 0)])

    xs = _dispatch(_tile_dest(dest, DISP_TM), zflag, h, p_rows)
    prev_e = jnp.concatenate([jnp.full((1,), -1, i32), block_e[:-1]])
    first = ((block_e != prev_e) & (blk < n_used[0])).astype(i32)
    eidx = jnp.arange(N_EXPERTS, dtype=i32)
    later = jnp.where((eidx[None, :] > eidx[:, None]) & (padded[None, :] > 0), eidx[None, :], N_EXPERTS)
    next_of = jnp.min(later, axis=1)
    next_e = jnp.where(next_of[block_e] < N_EXPERTS, next_of[block_e], -1).astype(i32)
    slot = ((jnp.cumsum(first) - 1) % 2).astype(i32)
    y = _experts(block_e, n_used, first, next_e, slot, xs, w_gate, w_up, w_down)
    out = _combine(_tile_dest(dest, COMB_TM), route, h, ln2_g[None, :], ln2_b[None, :], y, alpha)
    return out.reshape(b, t, d)


def kernel(x, positions, w_in, cmp_k_pe, cmp_k_w1, cmp_k_b1, cmp_k_w2, cmp_v_pe, cmp_v_w1, cmp_v_b1, cmp_v_w2, conv_w, conv_b, dt_bias, a_log, d_skip, ssm_norm_w, w_out, ln1_g, ln1_b, w_router_group, b_router_group, w_router_expert, b_router_expert, w_gate, w_up, w_down, ln2_g, ln2_b):
    depth = w_in.shape[0]
    alpha = (2 * depth) ** 0.25
    params = (w_in, cmp_k_pe, cmp_k_w1, cmp_k_b1, cmp_k_w2, cmp_v_pe, cmp_v_w1, cmp_v_b1, cmp_v_w2, conv_w, conv_b,
              dt_bias, a_log, d_skip, ssm_norm_w, w_out, ln1_g, ln1_b, w_router_group, b_router_group,
              w_router_expert, b_router_expert, w_gate, w_up, w_down, ln2_g, ln2_b)
    for l in range(depth):
        x = _layer(x, positions, *[p[l] for p in params], alpha)
    return x
```

```python
import functools

import jax
import jax.numpy as jnp
import numpy as np
from jax import lax
from jax.experimental import pallas as pl
from jax.experimental.pallas import tpu as pltpu

f32 = jnp.float32
bf16 = jnp.bfloat16
i32 = jnp.int32

HEAD_DIM = 64
NSA_HEADS = 16
NSA_KV_GROUPS = 2
NSA_REP = NSA_HEADS // NSA_KV_GROUPS
NSA_WIDTH = NSA_HEADS * HEAD_DIM
KV_WIDTH = NSA_KV_GROUPS * HEAD_DIM
CMP_BLOCK = 32
CMP_STRIDE = 16
CMP_HIDDEN = 256
SEL_BLOCK = 64
SEL_TOPK = 16
WINDOW = 512
FORCED_SCORE = 1.0e4
SSM_HEADDIM = 64
SSM_HEADS = 16
SSM_WIDTH = SSM_HEADS * SSM_HEADDIM
SSM_GROUPS = 4
SSM_REP = SSM_HEADS // SSM_GROUPS
SSM_STATE = 128
CONV_WIDTH = 4
CHUNK = 256
XBC_WIDTH = SSM_WIDTH + 2 * SSM_GROUPS * SSM_STATE
ROPE_THETA = 500000.0
ROT_DIM = HEAD_DIM // 4
N_EXPERT_GROUPS = 4
EXPERTS_PER_GROUP = 8
N_EXPERTS = N_EXPERT_GROUPS * EXPERTS_PER_GROUP
NORM_EPS = 1e-5

LANES = 128
V7X_VMEM_BYTES = 64 * 1024 * 1024
MASK_NEG = -1.0e30

PROJ_TM = 1024
PROJ_TN = 1024
PREP_TM = 512
CMP_TQ = 256
ATT_TQ = 256
ATT_TK = 512
WIN_TK = 256
WIN_HEADS = 2
KX_WIDTH = 2 * LANES
OUT_TM = 512
MOE_TM = 256
DISP_TM = 256
COMB_TM = 128
VMEM_LIMIT = V7X_VMEM_BYTES - 8 * 1024 * 1024

COL_Q = 0
COL_Z = NSA_WIDTH
COL_XBC = COL_Z + SSM_WIDTH
COL_KV = COL_XBC + XBC_WIDTH
COL_SMALL = COL_KV + 6 * KV_WIDTH
PROJ_COLS = -(-(COL_SMALL + LANES) // PROJ_TN) * PROJ_TN
SMALL_DT_OFF = 3 * NSA_HEADS


def _cparams(sem, vmem=VMEM_LIMIT):
    return pltpu.CompilerParams(dimension_semantics=sem, vmem_limit_bytes=vmem)


def _dot(a, b):
    return jnp.dot(a, b, preferred_element_type=f32)


def _dot_t(a, b):
    return lax.dot_general(a, b, (((1,), (1,)), ((), ())), preferred_element_type=f32)


def _dot_hi(a, b):
    return jnp.dot(a, b, preferred_element_type=f32, precision=lax.Precision.HIGHEST)


def _proj_kernel(x_ref, w_ref, o_ref, xb_ref):
    @pl.when(pl.program_id(1) == 0)
    def _():
        xb_ref[...] = x_ref[...].astype(bf16)

    o_ref[...] = _dot(xb_ref[...], w_ref[...])


def _proj(x2, w_cat):
    n, d = x2.shape
    cols = w_cat.shape[1]
    tm = min(PROJ_TM, n)
    return pl.pallas_call(
        _proj_kernel,
        grid=(n // tm, cols // PROJ_TN),
        in_specs=[pl.BlockSpec((tm, d), lambda i, j: (i, 0)),
                  pl.BlockSpec((d, PROJ_TN), lambda i, j: (0, j))],
        out_specs=pl.BlockSpec((tm, PROJ_TN), lambda i, j: (i, j)),
        out_shape=jax.ShapeDtypeStruct((n, cols), f32),
        scratch_shapes=[pltpu.VMEM((tm, d), bf16)],
        compiler_params=_cparams(("parallel", "arbitrary")),
        name="proj",
    )(x2, w_cat)


def _rope_tables(pos_i32, invf):
    ang = pos_i32.astype(f32) * invf
    return jnp.cos(ang), jnp.sin(ang)


def _rope128(x, cos, sin):
    half = ROT_DIM // 2
    d = lax.broadcasted_iota(i32, x.shape, 1) % HEAD_DIM
    up = pltpu.roll(x, LANES - half, 1)
    dn = pltpu.roll(x, half, 1)
    rot = jnp.where(d < half, -up, dn)
    return x * cos + rot * sin


def _nsa_prep_kernel(pos_ref, invf_ref, q_ref, kc_ref, vc_ref, ks_ref, vs_ref, kw_ref, vw_ref,
                     qo_ref, kco_ref, vco_ref, kso_ref, vso_ref, kwo_ref, vwo_ref):
    cos, sin = _rope_tables(pos_ref[...], invf_ref[...])
    scale = HEAD_DIM ** -0.5
    for c in range(NSA_WIDTH // LANES):
        t = _rope128(q_ref[:, c * LANES:(c + 1) * LANES], cos, sin) * scale
        qo_ref[0, 2 * c] = t[:, :HEAD_DIM].astype(bf16)
        qo_ref[0, 2 * c + 1] = t[:, HEAD_DIM:].astype(bf16)

    def split(src, dst, rope, dt):
        t = src[...]
        if rope:
            t = _rope128(t, cos, sin)
        for g in range(NSA_KV_GROUPS):
            dst[0, g] = t[:, g * HEAD_DIM:(g + 1) * HEAD_DIM].astype(dt)

    split(kc_ref, kco_ref, False, f32)
    split(vc_ref, vco_ref, False, f32)

    tm = pos_ref.shape[0]

    def values_with_ones(src, dst):
        t = src[...]
        tail = jnp.ones((tm, HEAD_DIM), bf16)
        for g in range(NSA_KV_GROUPS):
            vg = t[:, g * HEAD_DIM:(g + 1) * HEAD_DIM].astype(bf16)
            dst[0, g] = jnp.concatenate([vg, tail], axis=1)

    values_with_ones(vs_ref, vso_ref)
    values_with_ones(vw_ref, vwo_ref)

    kw_t = _rope128(kw_ref[...], cos, sin).T
    ks_t = _rope128(ks_ref[...], cos, sin).T
    for g in range(NSA_KV_GROUPS):
        for c in range(tm // WIN_TK):
            kwo_ref[0, g, c] = kw_t[g * HEAD_DIM:(g + 1) * HEAD_DIM, c * WIN_TK:(c + 1) * WIN_TK].astype(bf16)
    blk = lax.broadcasted_iota(i32, (LANES, tm), 0)
    tok = pl.program_id(1) * tm + lax.broadcasted_iota(i32, (LANES, tm), 1)
    onehot_t = jnp.where(tok // SEL_BLOCK == blk, 1.0, 0.0).astype(bf16)
    for g in range(NSA_KV_GROUPS):
        kso_ref[0, g, 0, 0:LANES, :] = onehot_t
        kg = ks_t[g * HEAD_DIM:(g + 1) * HEAD_DIM, :].astype(bf16)
        kso_ref[0, g, 0, LANES:2 * LANES, :] = jnp.concatenate([kg, jnp.zeros_like(kg)], axis=0)


def _nsa_prep(proj, pos128, invf, b, t):
    tm = PREP_TM
    assert tm == ATT_TK
    nt = t // tm
    row = lambda bi, ti: (bi * nt + ti, 0)
    kv0 = COL_KV // LANES
    in_specs = [pl.BlockSpec((tm, LANES), row),
                pl.BlockSpec((1, LANES), lambda bi, ti: (0, 0)),
                pl.BlockSpec((tm, NSA_WIDTH), lambda bi, ti: (bi * nt + ti, COL_Q // NSA_WIDTH))]
    for k in range(6):
        in_specs.append(pl.BlockSpec((tm, LANES), functools.partial(lambda bi, ti, k: (bi * nt + ti, kv0 + k), k=k)))
    head = lambda bi, ti: (bi, 0, ti, 0)
    tile5 = lambda bi, ti: (bi, 0, ti, 0, 0)
    g = NSA_KV_GROUPS
    out_specs = [pl.BlockSpec((1, NSA_HEADS, tm, HEAD_DIM), head),
                 pl.BlockSpec((1, g, tm, HEAD_DIM), head), pl.BlockSpec((1, g, tm, HEAD_DIM), head),
                 pl.BlockSpec((1, g, 1, KX_WIDTH, tm), tile5), pl.BlockSpec((1, g, tm, LANES), head),
                 pl.BlockSpec((1, g, tm // WIN_TK, HEAD_DIM, WIN_TK), tile5), pl.BlockSpec((1, g, tm, LANES), head)]
    out_shape = [jax.ShapeDtypeStruct((b, NSA_HEADS, t, HEAD_DIM), bf16),
                 jax.ShapeDtypeStruct((b, g, t, HEAD_DIM), f32), jax.ShapeDtypeStruct((b, g, t, HEAD_DIM), f32),
                 jax.ShapeDtypeStruct((b, g, nt, KX_WIDTH, tm), bf16), jax.ShapeDtypeStruct((b, g, t, LANES), bf16),
                 jax.ShapeDtypeStruct((b, g, t // WIN_TK, HEAD_DIM, WIN_TK), bf16),
                 jax.ShapeDtypeStruct((b, g, t, LANES), bf16)]
    return pl.pallas_call(
        _nsa_prep_kernel,
        grid=(b, nt),
        in_specs=in_specs,
        out_specs=out_specs,
        out_shape=out_shape,
        compiler_params=_cparams(("parallel", "parallel")),
        name="nsa_prep",
    )(pos128, invf, proj, proj, proj, proj, proj, proj, proj)


def _cmp_mlp_kernel(a_ref, pe_ref, w1_ref, b1_ref, w2_ref, pos_ref, invf_ref, o_ref, ot_ref):
    kind = pl.program_id(0)
    a = a_ref[0, 0]
    nc = a.shape[0]
    u = _dot((a + pe_ref[0, 0]).astype(bf16), w1_ref[0, 0])
    v = _dot((a + pe_ref[0, 1]).astype(bf16), w1_ref[0, 1])
    v_next = pltpu.roll(v, nc - 1, 0)
    hid = jax.nn.gelu(u + v_next + b1_ref[0])
    out = _dot(hid.astype(bf16), w2_ref[0])
    cos, sin = _rope_tables(pos_ref[0], invf_ref[...])
    roped = _rope128(out, cos, sin)
    out = jnp.where(kind == 0, roped, out)
    o_ref[0, 0] = out[:, :HEAD_DIM].astype(bf16)
    ot_ref[0, 0] = out.T[:HEAD_DIM, :].astype(bf16)


def _cmp_mlp(a, pe, w1, b1, w2, posc, invf, b):
    _, bg, nc, hw = a.shape
    g = bg // b
    return pl.pallas_call(
        _cmp_mlp_kernel,
        grid=(2, bg),
        in_specs=[pl.BlockSpec((1, 1, nc, hw), lambda k, i: (k, i, 0, 0)),
                  pl.BlockSpec((1, 2, 1, hw), lambda k, i: (k, 0, 0, 0)),
                  pl.BlockSpec((1, 2, hw, CMP_HIDDEN), lambda k, i: (k, 0, 0, 0)),
                  pl.BlockSpec((1, 1, CMP_HIDDEN), lambda k, i: (k, 0, 0)),
                  pl.BlockSpec((1, CMP_HIDDEN, LANES), lambda k, i: (k, 0, 0)),
                  pl.BlockSpec((1, nc, LANES), lambda k, i: (i // g, 0, 0)),
                  pl.BlockSpec((1, LANES), lambda k, i: (0, 0))],
        out_specs=[pl.BlockSpec((1, 1, nc, HEAD_DIM), lambda k, i: (k, i, 0, 0)),
                   pl.BlockSpec((1, 1, HEAD_DIM, nc), lambda k, i: (k, i, 0, 0))],
        out_shape=[jax.ShapeDtypeStruct((2, bg, nc, HEAD_DIM), bf16),
                   jax.ShapeDtypeStruct((2, bg, HEAD_DIM, nc), bf16)],
        compiler_params=_cparams(("parallel", "parallel")),
        name="cmp_mlp",
    )(a, pe, w1, b1, w2, posc, invf)


def _cmp_attn_kernel(q_ref, kct_ref, vc_ref, cover_ref, oc_ref, sel_ref):
    qi = pl.program_id(2)
    nc = vc_ref.shape[2]
    rows = NSA_REP * CMP_TQ
    tq = qi * CMP_TQ + lax.broadcasted_iota(i32, (CMP_TQ, 1), 0)
    row_live = jnp.where(tq >= CMP_BLOCK - 1, 1.0, 0.0)
    tiny = jnp.finfo(f32).tiny

    def attend(ncols):
        kct = kct_ref[0, 0, :, :ncols]
        vc = vc_ref[0, 0, :ncols, :]
        cend = lax.broadcasted_iota(i32, (1, ncols), 1) * CMP_STRIDE + (CMP_BLOCK - 1)
        bias = jnp.where(cend <= tq, 0.0, MASK_NEG)
        s = _dot(q_ref[0].reshape(rows, HEAD_DIM), kct).reshape(NSA_REP, CMP_TQ, ncols) + bias[None]
        e = jnp.exp(s - jnp.max(s, axis=-1, keepdims=True))
        live = row_live[None]
        p = e * (live / jnp.maximum(live * jnp.sum(e, axis=-1, keepdims=True), tiny))
        oc_ref[0] = _dot(p.reshape(rows, ncols).astype(bf16), vc).reshape(NSA_REP, CMP_TQ, HEAD_DIM)
        psum = jnp.sum(p, axis=0)
        hi = psum.astype(bf16)
        lo = (psum - hi.astype(f32)).astype(bf16)
        cover = cover_ref[:ncols, :]
        imp = _dot(hi, cover) + _dot(lo, cover)

        j = lax.broadcasted_iota(i32, (CMP_TQ, LANES), 1)
        cur = tq // SEL_BLOCK
        forced = (j == 0) | (j == cur) | (j == cur - 1)
        valid = j * SEL_BLOCK <= tq
        imp = jnp.where(valid, jnp.where(forced, FORCED_SCORE, imp), -FORCED_SCORE)

        nblk = ncols * CMP_STRIDE // SEL_BLOCK
        jt = lax.broadcasted_iota(i32, (nblk, CMP_TQ), 0)

        def pick(_, carry):
            work, sel = carry
            m = jnp.max(work, axis=0, keepdims=True)
            first = jnp.min(jnp.where(work == m, jt, LANES), axis=0, keepdims=True)
            hit = jt == first
            return jnp.where(hit, -jnp.inf, work), jnp.where(hit, 1.0, sel)

        _, sel_t = lax.fori_loop(0, SEL_TOPK, pick, (imp.T[:nblk], jnp.zeros((nblk, CMP_TQ), f32)))
        if nblk < LANES:
            sel_t = jnp.concatenate([sel_t, jnp.zeros((LANES - nblk, CMP_TQ), f32)], axis=0)
        sel_ref[0, 0] = jnp.where(valid, jnp.where(sel_t.T > 0.0, 0.0, MASK_NEG), MASK_NEG).astype(bf16)

    need = (qi + 1) * (CMP_TQ // CMP_STRIDE)
    for ncols in range(LANES, nc + 1, LANES):
        @pl.when((need > ncols - LANES) & (need <= ncols))
        def _():
            attend(ncols)


def _cmp_attn(q_r, kvc, kvc_t, cover, b, t):
    g = NSA_KV_GROUPS
    nc = kvc.shape[2]
    nq = t // CMP_TQ
    return pl.pallas_call(
        _cmp_attn_kernel,
        grid=(b, g, nq),
        in_specs=[pl.BlockSpec((1, NSA_REP, CMP_TQ, HEAD_DIM), lambda bi, gi, qi: (bi, gi, qi, 0)),
                  pl.BlockSpec((1, 1, HEAD_DIM, nc), lambda bi, gi, qi: (0, bi * g + gi, 0, 0)),
                  pl.BlockSpec((1, 1, nc, HEAD_DIM), lambda bi, gi, qi: (1, bi * g + gi, 0, 0)),
                  pl.BlockSpec((nc, LANES), lambda bi, gi, qi: (0, 0))],
        out_specs=[pl.BlockSpec((1, NSA_REP, CMP_TQ, HEAD_DIM), lambda bi, gi, qi: (bi, gi, qi, 0)),
                   pl.BlockSpec((1, 1, CMP_TQ, LANES), lambda bi, gi, qi: (bi, gi, qi, 0))],
        out_shape=[jax.ShapeDtypeStruct((b, NSA_HEADS, t, HEAD_DIM), f32),
                   jax.ShapeDtypeStruct((b, g, t, LANES), bf16)],
        compiler_params=_cparams(("parallel", "parallel", "parallel")),
        name="cmp_attn",
    )(q_r, kvc_t, kvc, cover)


def _nsa_attn_kernel(q_ref, kx_ref, vs_ref, kw_ref, vw_ref, sel_ref, oc_ref, gate_ref,
                     o_ref, qx_ref, s_ref, m_ref, acc_ref, yw_ref):
    gi = pl.program_id(1)
    qi = pl.program_id(2)
    start = qi * ATT_TQ
    tq = start + lax.broadcasted_iota(i32, (ATT_TQ, 1), 0)

    selb = sel_ref[0, 0]
    for r in range(NSA_REP):
        qx_ref[r * ATT_TQ:(r + 1) * ATT_TQ, 0:LANES] = selb
        qr = q_ref[0, r]
        qx_ref[r * ATT_TQ:(r + 1) * ATT_TQ, LANES:2 * LANES] = jnp.concatenate([qr, jnp.zeros_like(qr)], axis=1)
    rows = NSA_REP * ATT_TQ

    m_ref[...] = jnp.full(m_ref.shape, MASK_NEG, f32)
    acc_ref[...] = jnp.zeros(acc_ref.shape, f32)

    def consume(kt, v_ref, bias, parts=1):
        k0 = pl.multiple_of(kt * ATT_TK, ATT_TK)
        v = v_ref[0, 0, pl.ds(k0, ATT_TK), :]
        heads = NSA_REP // parts
        for part in range(parts):
            rs = slice(part * heads * ATT_TQ, (part + 1) * heads * ATT_TQ)
            s = s_ref[rs]
            if bias is not None:
                s = (s.reshape(heads, ATT_TQ, ATT_TK) + bias[None]).reshape(heads * ATT_TQ, ATT_TK)
            m_prev = m_ref[rs]
            m_new = jnp.maximum(m_prev, jnp.max(s, axis=-1, keepdims=True))
            alpha = jnp.exp(m_prev - m_new)
            p = jnp.exp(s - jnp.concatenate([m_new] * (ATT_TK // LANES), axis=1))
            acc_ref[rs] = alpha * acc_ref[rs] + _dot(p.astype(bf16), v)
            m_ref[rs] = m_new

    def kpos(kt):
        return kt * ATT_TK + lax.broadcasted_iota(i32, (1, ATT_TK), 1)

    last = start // ATT_TK
    for part in range(2):
        rs = slice(part * rows // 2, (part + 1) * rows // 2)
        s_ref[rs] = _dot(qx_ref[rs], kx_ref[0, 0, 0])

    def sel_step(kt, _):
        consume(kt, vs_ref, None)
        s_ref[...] = _dot(qx_ref[...], kx_ref[0, 0, kt + 1])
        return 0

    n_win = (WINDOW + ATT_TQ) // WIN_TK
    w_first = jnp.maximum(start // WIN_TK - WINDOW // WIN_TK, 0)
    kw = jnp.concatenate([kw_ref[0, 0, w_first + i] for i in range(n_win)], axis=1)
    w0 = pl.multiple_of(w_first * WIN_TK, WIN_TK)
    vw = vw_ref[0, 0, pl.ds(w0, n_win * WIN_TK), :]
    kp = w0 + lax.broadcasted_iota(i32, (1, n_win * WIN_TK), 1)
    wbias = jnp.where(kp <= tq, jnp.where(kp > tq - WINDOW, 0.0, MASK_NEG), MASK_NEG)
    hh = WIN_HEADS
    gates = jax.nn.sigmoid(gate_ref[...])
    per_group = 3 * NSA_REP
    shifted = gates
    for gg in range(1, NSA_KV_GROUPS):
        shifted = jnp.where(gi == gg, pltpu.roll(gates, LANES - gg * per_group, 1), shifted)

    def gate(r, br):
        c = r * 3 + br
        return jnp.broadcast_to(shifted[:, c:c + 1], (ATT_TQ, HEAD_DIM))

    upper = pltpu.roll(shifted, HEAD_DIM, 1)[:, HEAD_DIM:]

    def gated_output(a, r, br):
        c = r * 3 + br
        ratio = upper / a[:, HEAD_DIM:]
        return a[:, :HEAD_DIM] * jnp.broadcast_to(ratio[:, c:c + 1], (ATT_TQ, HEAD_DIM))

    for half in range(NSA_REP // hh):
        qh = q_ref[0, half * hh:(half + 1) * hh].reshape(hh * ATT_TQ, HEAD_DIM)
        sw = _dot(qh, kw).reshape(hh, ATT_TQ, n_win * WIN_TK) + wbias[None]
        sw = sw.reshape(hh * ATT_TQ, n_win * WIN_TK)
        pw = jnp.exp(sw - jnp.max(sw, axis=-1, keepdims=True))
        aw = _dot(pw.astype(bf16), vw)
        for i in range(hh):
            r = half * hh + i
            a = aw[i * ATT_TQ:(i + 1) * ATT_TQ]
            yw_ref[r] = gated_output(a, r, 2)

    def sel_pair(j, _):
        sel_step(2 * j, 0)
        sel_step(2 * j + 1, 0)
        return 0

    lax.fori_loop(0, last // 2, sel_pair, 0)

    @pl.when(last % 2 == 1)
    def _():
        sel_step(last - 1, 0)

    consume(last, vs_ref, jnp.where(kpos(last) <= tq, 0.0, MASK_NEG), parts=2)

    for r in range(NSA_REP):
        osel = gated_output(acc_ref[r * ATT_TQ:(r + 1) * ATT_TQ], r, 1)
        o_ref[:, r * HEAD_DIM:(r + 1) * HEAD_DIM] = (gate(r, 0) * oc_ref[0, r] + osel + yw_ref[r]).astype(o_ref.dtype)


def _nsa_attn(q_r, kx, vs, kw, vw, selb, oc, proj, b, t):
    g = NSA_KV_GROUPS
    nq = t // ATT_TQ
    once = dict(pipeline_mode=pl.Buffered(1))
    vspec = pl.BlockSpec((1, 1, t, LANES), lambda bi, gi, qi: (bi, gi, 0, 0), **once)
    hspec = pl.BlockSpec((1, NSA_REP, ATT_TQ, HEAD_DIM), lambda bi, gi, qi: (bi, gi, qi, 0))
    return pl.pallas_call(
        _nsa_attn_kernel,
        grid=(b, g, nq),
        in_specs=[hspec,
                  pl.BlockSpec((1, 1, t // ATT_TK, KX_WIDTH, ATT_TK), lambda bi, gi, qi: (bi, gi, 0, 0, 0), **once),
                  vspec,
                  pl.BlockSpec((1, 1, t // WIN_TK, HEAD_DIM, WIN_TK), lambda bi, gi, qi: (bi, gi, 0, 0, 0), **once),
                  vspec,
                  pl.BlockSpec((1, 1, ATT_TQ, LANES), lambda bi, gi, qi: (bi, gi, qi, 0)),
                  hspec,
                  pl.BlockSpec((ATT_TQ, LANES), lambda bi, gi, qi: (bi * nq + qi, COL_SMALL // LANES))],
        out_specs=pl.BlockSpec((ATT_TQ, NSA_REP * HEAD_DIM), lambda bi, gi, qi: (bi * nq + qi, gi)),
        out_shape=jax.ShapeDtypeStruct((b * t, NSA_WIDTH), bf16),
        scratch_shapes=[pltpu.VMEM((NSA_REP * ATT_TQ, KX_WIDTH), bf16),
                        pltpu.VMEM((NSA_REP * ATT_TQ, ATT_TK), f32),
                        pltpu.VMEM((NSA_REP * ATT_TQ, LANES), f32),
                        pltpu.VMEM((NSA_REP * ATT_TQ, LANES), f32),
                        pltpu.VMEM((NSA_REP, ATT_TQ, HEAD_DIM), f32)],
        compiler_params=_cparams(("parallel", "parallel", "arbitrary")),
        name="nsa_attn",
    )(q_r, kx, vs, kw, vw, selb, oc, proj)


HALO = 8


def _ssd_kernel(xbc_ref, z_ref, small_ref, dtt_ref, cw_ref, cb_ref, dtb_r_ref, dtb_c_ref,
                alog_r_ref, alog_c_ref, dskip_ref, nw_ref, o_ref, ext_ref, st_ref):
    c = pl.program_id(1)
    L = CHUNK

    @pl.when(c == 0)
    def _():
        ext_ref[0:HALO, :] = jnp.zeros((HALO, XBC_WIDTH), f32)
        st_ref[...] = jnp.zeros_like(st_ref)

    ext_ref[HALO:HALO + L, :] = xbc_ref[...]
    conv = cb_ref[...]
    for k in range(CONV_WIDTH):
        off = HALO - (CONV_WIDTH - 1) + k
        conv = conv + cw_ref[k:k + 1, :] * ext_ref[off:off + L, :]
    ext_ref[0:HALO, :] = ext_ref[L:L + HALO, :]
    act = conv * jax.nn.sigmoid(conv)
    xs = act[:, :SSM_WIDTH]
    bm = act[:, SSM_WIDTH:SSM_WIDTH + SSM_GROUPS * SSM_STATE]
    cm = act[:, SSM_WIDTH + SSM_GROUPS * SSM_STATE:]

    dt_c = jax.nn.softplus(small_ref[:, SMALL_DT_OFF:SMALL_DT_OFF + SSM_HEADS] + dtb_r_ref[...])
    dt_r = jax.nn.softplus(dtt_ref[0] + dtb_c_ref[...])
    a_r = -jnp.exp(alog_r_ref[...])
    a_c = -jnp.exp(alog_c_ref[...])
    row = lax.broadcasted_iota(i32, (L, L), 0)
    col = lax.broadcasted_iota(i32, (L, L), 1)
    causal = col <= row
    tri = jnp.where(causal, 1.0, 0.0)
    acs_c = _dot_hi(tri, dt_c * a_r)
    acs_r = _dot_hi(dt_r * a_c, jnp.where(row <= col, 1.0, 0.0))

    z = z_ref[...]
    for g in range(SSM_GROUPS):
        cg = cm[:, g * SSM_STATE:(g + 1) * SSM_STATE].astype(bf16)
        bg = bm[:, g * SSM_STATE:(g + 1) * SSM_STATE]
        cb = _dot_t(cg, bg.astype(bf16))
        bg_t = bg.T
        ssq = jnp.zeros((L, 1), f32)
        yg = []
        for pp in range(SSM_REP // 2):
            pair = g * (SSM_REP // 2) + pp
            ps = slice(pair * LANES, (pair + 1) * LANES)
            x_p = xs[:, ps]
            low = lax.broadcasted_iota(i32, (L, LANES), 1) < SSM_HEADDIM
            low_n = lax.broadcasted_iota(i32, (SSM_STATE, LANES), 1) < SSM_HEADDIM
            x_own = (jnp.where(low, x_p, 0.0).astype(bf16), jnp.where(low, 0.0, x_p).astype(bf16))
            st = st_ref[pair]
            y = jnp.zeros((L, LANES), f32)
            st_in = jnp.zeros((SSM_STATE, LANES), f32)
            grow, keep = [], []
            for j in range(2):
                h = 2 * pair + j
                a_col = acs_c[:, h:h + 1]
                a_row = acs_r[h:h + 1, :]
                dt_row = dt_r[h:h + 1, :]
                a_last = acs_r[h:h + 1, L - 1:L]
                seg = a_col - a_row
                decay = jnp.where(causal, jnp.exp(jnp.where(causal, seg, 0.0)), 0.0)
                w = cb * decay * dt_row
                y = y + _dot(w.astype(bf16), x_own[j])
                bscaled = bg_t * (jnp.exp(a_last - a_row) * dt_row)
                st_in = st_in + _dot(bscaled.astype(bf16), x_own[j])
                grow.append(jnp.exp(a_col))
                keep.append(jnp.exp(a_last))
            y = y + _dot(cg, st.astype(bf16)) * jnp.where(low, grow[0], grow[1])
            st_ref[pair] = jnp.where(low_n, keep[0], keep[1]) * st + st_in
            y = y + dskip_ref[:, ps] * x_p
            zp = z[:, ps]
            y = y * (zp * jax.nn.sigmoid(zp))
            ssq = ssq + jnp.sum(y * y, axis=-1, keepdims=True)
            yg.append(y)
        rs = lax.rsqrt(ssq / (SSM_REP * SSM_HEADDIM) + NORM_EPS)
        for pp in range(SSM_REP // 2):
            pair = g * (SSM_REP // 2) + pp
            ps = slice(pair * LANES, (pair + 1) * LANES)
            o_ref[:, ps] = (yg[pp] * rs * nw_ref[:, ps]).astype(o_ref.dtype)


def _ssd(proj, dtt, cw, cb, dtb_r, dtb_c, alog_r, alog_c, dskip, nw, b, t):
    nch = t // CHUNK
    row = lambda bi, ci: bi * nch + ci
    const2 = lambda bi, ci: (0, 0)
    return pl.pallas_call(
        _ssd_kernel,
        grid=(b, nch),
        in_specs=[pl.BlockSpec((CHUNK, XBC_WIDTH), lambda bi, ci: (row(bi, ci), COL_XBC // XBC_WIDTH)),
                  pl.BlockSpec((CHUNK, SSM_WIDTH), lambda bi, ci: (row(bi, ci), COL_Z // SSM_WIDTH)),
                  pl.BlockSpec((CHUNK, LANES), lambda bi, ci: (row(bi, ci), COL_SMALL // LANES)),
                  pl.BlockSpec((1, SSM_HEADS, CHUNK), lambda bi, ci: (bi, 0, ci)),
                  pl.BlockSpec((CONV_WIDTH, XBC_WIDTH), const2),
                  pl.BlockSpec((1, XBC_WIDTH), const2),
                  pl.BlockSpec((1, SSM_HEADS), const2),
                  pl.BlockSpec((SSM_HEADS, 1), const2),
                  pl.BlockSpec((1, SSM_HEADS), const2),
                  pl.BlockSpec((SSM_HEADS, 1), const2),
                  pl.BlockSpec((1, SSM_WIDTH), const2),
                  pl.BlockSpec((1, SSM_WIDTH), const2)],
        out_specs=pl.BlockSpec((CHUNK, SSM_WIDTH), lambda bi, ci: (row(bi, ci), 0)),
        out_shape=jax.ShapeDtypeStruct((b * t, SSM_WIDTH), bf16),
        scratch_shapes=[pltpu.VMEM((HALO + CHUNK, XBC_WIDTH), f32),
                        pltpu.VMEM((SSM_HEADS // 2, SSM_STATE, 2 * SSM_HEADDIM), f32)],
        compiler_params=_cparams(("parallel", "arbitrary")),
        name="ssd",
    )(proj, proj, proj, dtt, cw, cb, dtb_r, dtb_c, alog_r, alog_c, dskip, nw)


def _layer_norm(v, g, b):
    mu = jnp.mean(v, axis=-1, keepdims=True)
    d = v - mu
    var = jnp.mean(d * d, axis=-1, keepdims=True)
    return d * lax.rsqrt(var + NORM_EPS) * g + b


def _outproj_kernel(alpha, ya_ref, yb_ref, x_ref, wa_ref, wb_ref, g_ref, b_ref, wr_ref, br_ref,
                    h_ref, route_ref, cnt_ref):
    i = pl.program_id(0)
    tm = x_ref.shape[0]
    mix = _dot(ya_ref[...], wa_ref[...]) + _dot(yb_ref[...], wb_ref[...])
    h = _layer_norm(alpha * x_ref[...] + mix, g_ref[...], b_ref[...])
    h_ref[...] = h

    h_hi = h.astype(bf16)
    h_lo = (h - h_hi.astype(f32)).astype(bf16)
    t = _dot(h_hi, wr_ref[...])
    logits = t[:, :LANES] + t[:, LANES:] + _dot(h_lo, wr_ref[:, :LANES]) + br_ref[...]
    lane = lax.broadcasted_iota(i32, (tm, LANES), 1)
    ninf = -jnp.inf
    gmask = lane < N_EXPERT_GROUPS
    gl = jnp.where(gmask, logits, ninf)
    ge = jnp.where(gmask, jnp.exp(gl - jnp.max(gl, axis=-1, keepdims=True)), 0.0)
    pg = ge / jnp.sum(ge, axis=-1, keepdims=True)
    g_gate = jnp.max(pg, axis=-1, keepdims=True)
    g_sel = jnp.min(jnp.where(gmask & (pg == g_gate), lane, LANES), axis=-1, keepdims=True)
    lo = N_EXPERT_GROUPS + g_sel * EXPERTS_PER_GROUP
    emask = (lane >= lo) & (lane < lo + EXPERTS_PER_GROUP)
    el = jnp.where(emask, logits, ninf)
    ee = jnp.where(emask, jnp.exp(el - jnp.max(el, axis=-1, keepdims=True)), 0.0)
    pe = ee / jnp.sum(ee, axis=-1, keepdims=True)
    p0 = jnp.max(pe, axis=-1, keepdims=True)
    l0 = jnp.min(jnp.where(emask & (pe == p0), lane, LANES), axis=-1, keepdims=True)
    rest = jnp.where(emask & (lane != l0), pe, ninf)
    p1 = jnp.max(rest, axis=-1, keepdims=True)
    l1 = jnp.min(jnp.where(rest == p1, lane, LANES), axis=-1, keepdims=True)
    psum = p0 + p1
    w0 = g_gate * p0 / psum
    w1 = g_gate * p1 / psum
    e0 = l0 - N_EXPERT_GROUPS
    e1 = l1 - N_EXPERT_GROUPS

    @pl.when(i == 0)
    def _():
        cnt_ref[...] = jnp.zeros_like(cnt_ref)

    oh0 = lane == e0
    oh1 = lane == e1
    both = jnp.where(oh0, 1.0, 0.0) + jnp.where(oh1, 1.0, 0.0)
    r_i = lax.broadcasted_iota(i32, (tm, tm), 0)
    c_i = lax.broadcasted_iota(i32, (tm, tm), 1)
    strict = jnp.where(c_i < r_i, 1.0, 0.0).astype(bf16)
    before = _dot(strict, both.astype(bf16)) + cnt_ref[...]
    rank0 = jnp.sum(jnp.where(oh0, before, 0.0), axis=-1, keepdims=True)
    rank1 = jnp.sum(jnp.where(oh1, before, 0.0), axis=-1, keepdims=True)
    cnt_ref[...] = cnt_ref[...] + jnp.sum(both, axis=0, keepdims=True)

    out = jnp.where(lane == 0, e0.astype(f32), 0.0)
    out = jnp.where(lane == 1, e1.astype(f32), out)
    out = jnp.where(lane == 2, w0, out)
    out = jnp.where(lane == 3, w1, out)
    out = jnp.where(lane == 4, rank0, out)
    out = jnp.where(lane == 5, rank1, out)
    route_ref[...] = out


def _outproj(y_nsa, y_ssm, x2, wa, wb, g, bta, wr, br, alpha):
    n, d = x2.shape
    tm = OUT_TM
    const = lambda i: (0, 0)
    rowb = lambda i: (i, 0)
    return pl.pallas_call(
        functools.partial(_outproj_kernel, alpha),
        grid=(n // tm,),
        in_specs=[pl.BlockSpec((tm, NSA_WIDTH), rowb), pl.BlockSpec((tm, SSM_WIDTH), rowb),
                  pl.BlockSpec((tm, d), rowb),
                  pl.BlockSpec((NSA_WIDTH, d), const), pl.BlockSpec((SSM_WIDTH, d), const),
                  pl.BlockSpec((1, d), const), pl.BlockSpec((1, d), const),
                  pl.BlockSpec((d, 2 * LANES), const), pl.BlockSpec((1, LANES), const)],
        out_specs=[pl.BlockSpec((tm, d), rowb), pl.BlockSpec((tm, LANES), rowb),
                   pl.BlockSpec((1, LANES), const)],
        out_shape=[jax.ShapeDtypeStruct((n, d), f32), jax.ShapeDtypeStruct((n, LANES), f32),
                   jax.ShapeDtypeStruct((1, LANES), f32)],
        compiler_params=_cparams(("arbitrary",)),
        name="outproj",
    )(y_nsa, y_ssm, x2, wa, wb, g, bta, wr, br)


def _row_copy(src_ref, src_row, dst_ref, dst_row, sem):
    return pltpu.make_async_copy(src_ref.at[pl.ds(src_row, 1), :], dst_ref.at[pl.ds(dst_row, 1), :], sem)


def _dest_row(dest_ref, tm, k, r):
    return dest_ref[0, 0, k * tm + r]


def _dispatch_kernel(dest_ref, zflag_ref, h_ref, xs_ref, zero_ref, tile_ref, sem, lsem, zsem):
    tm = DISP_TM
    nb = zflag_ref.shape[0]
    i = pl.program_id(0)

    @pl.when(i == 0)
    def _():
        zero_ref[...] = jnp.zeros_like(zero_ref)

        def zblock(i):
            rows = pl.ds(pl.multiple_of(i * MOE_TM, MOE_TM), MOE_TM)
            return pltpu.make_async_copy(zero_ref, xs_ref.at[rows, :], zsem)

        def zstart(i, _):
            @pl.when(zflag_ref[i] != 0)
            def _():
                zblock(i).start()
            return 0

        def zwait(i, _):
            @pl.when(zflag_ref[i] != 0)
            def _():
                zblock(i).wait()
            return 0

        lax.fori_loop(0, nb, zstart, 0)
        lax.fori_loop(0, nb, zwait, 0)

    nslot = tile_ref.shape[0]
    nsteps = pl.num_programs(0)
    slot = i % nslot

    def tile_load(step, s):
        rows = pl.ds(pl.multiple_of(step * tm, tm), tm)
        return pltpu.make_async_copy(h_ref.at[rows, :], tile_ref.at[s], lsem.at[s])

    @pl.when(i == 0)
    def _():
        tile_load(0, 0).start()

    tile_load(i, slot).wait()

    @pl.when(i + 1 < nsteps)
    def _():
        tile_load(i + 1, (i + 1) % nslot).start()

    def issue(r, _):
        for k in range(2):
            _row_copy(tile_ref.at[slot], r, xs_ref, _dest_row(dest_ref, tm, k, r), sem.at[slot]).start()
        return 0

    lax.fori_loop(0, tm, issue, 0, unroll=16)

    def wait_rows(s):
        for k in range(2):
            pltpu.make_async_copy(tile_ref.at[s], xs_ref.at[pl.ds(0, tm), :], sem.at[s]).wait()

    @pl.when(i > 0)
    def _():
        wait_rows((i - 1) % nslot)

    @pl.when(i == nsteps - 1)
    def _():
        wait_rows(slot)


def _dispatch(dest_t, zflag, h, p_rows):
    n, d = h.shape
    tm = DISP_TM
    return pl.pallas_call(
        _dispatch_kernel,
        grid=(n // tm,),
        in_specs=[pl.BlockSpec((1, 1, 2 * tm), lambda i: (i, 0, 0), memory_space=pltpu.SMEM),
                  pl.BlockSpec(memory_space=pltpu.SMEM),
                  pl.BlockSpec(memory_space=pl.ANY)],
        out_specs=pl.BlockSpec(memory_space=pl.ANY),
        out_shape=jax.ShapeDtypeStruct((p_rows, d), f32),
        scratch_shapes=[pltpu.VMEM((MOE_TM, d), f32), pltpu.VMEM((3, tm, d), f32),
                        pltpu.SemaphoreType.DMA((3,)), pltpu.SemaphoreType.DMA((3,)), pltpu.SemaphoreType.DMA(())],
        compiler_params=_cparams(("arbitrary",)),
        name="dispatch",
    )(dest_t, zflag, h)


def _experts_kernel(be_ref, nu_ref, first_ref, next_ref, slot_ref, xs_ref, wg_ref, wu_ref, wd_ref, y_ref,
                    fg_ref, fu_ref, fd_ref, wgb_ref, wub_ref, wdb_ref, sems):
    i = pl.program_id(0)

    def weight_loads(e, s):
        return (pltpu.make_async_copy(wg_ref.at[e], fg_ref.at[s], sems.at[s, 0]),
                pltpu.make_async_copy(wu_ref.at[e], fu_ref.at[s], sems.at[s, 1]),
                pltpu.make_async_copy(wd_ref.at[e], fd_ref.at[s], sems.at[s, 2]))

    @pl.when(i == 0)
    def _():
        for c in weight_loads(be_ref[0], 0):
            c.start()

    @pl.when(first_ref[i] != 0)
    def _():
        s = slot_ref[i]
        for c in weight_loads(be_ref[i], s):
            c.wait()
        wgb_ref[...] = fg_ref[s].astype(bf16)
        wub_ref[...] = fu_ref[s].astype(bf16)
        wdb_ref[...] = fd_ref[s].astype(bf16)

        @pl.when(next_ref[i] >= 0)
        def _():
            for c in weight_loads(next_ref[i], 1 - s):
                c.start()

    @pl.when(i < nu_ref[0])
    def _():
        xb = xs_ref[...].astype(bf16)
        gte = _dot(xb, wgb_ref[...])
        up = _dot(xb, wub_ref[...])
        act = gte * jax.nn.sigmoid(gte) * up
        y_ref[...] = _dot(act.astype(bf16), wdb_ref[...])

    @pl.when(i >= nu_ref[0])
    def _():
        y_ref[...] = jnp.zeros_like(y_ref)


def _experts(block_e, n_used, first, next_e, slot, xs, w_gate, w_up, w_down):
    p_rows, d = xs.shape
    tm = MOE_TM
    nb = p_rows // tm
    de = w_gate.shape[-1]
    xmap = lambda i, be, nu, fi, ne, sl: (jnp.maximum(jnp.minimum(i, nu[0] - 1), 0), 0)
    hbm = pl.BlockSpec(memory_space=pl.ANY)
    return pl.pallas_call(
        _experts_kernel,
        grid_spec=pltpu.PrefetchScalarGridSpec(
            num_scalar_prefetch=5,
            grid=(nb,),
            in_specs=[pl.BlockSpec((tm, d), xmap), hbm, hbm, hbm],
            out_specs=pl.BlockSpec((tm, d), lambda i, be, nu, fi, ne, sl: (i, 0)),
            scratch_shapes=[pltpu.VMEM((2, d, de), f32), pltpu.VMEM((2, d, de), f32), pltpu.VMEM((2, de, d), f32),
                            pltpu.VMEM((d, de), bf16), pltpu.VMEM((d, de), bf16), pltpu.VMEM((de, d), bf16),
                            pltpu.SemaphoreType.DMA((2, 3))],
        ),
        out_shape=jax.ShapeDtypeStruct((p_rows, d), f32),
        compiler_params=_cparams(("arbitrary",)),
        name="experts",
    )(block_e, n_used, first, next_e, slot, xs, w_gate, w_up, w_down)


def _combine_kernel(alpha, dest_ref, ndest_ref, route_ref, h_ref, g_ref, b_ref, y_ref, o_ref, buf_ref, sems):
    tm = h_ref.shape[0]
    i = pl.program_id(0)
    slot = i % 2

    def start_row(ids_ref, s, r):
        for k in range(2):
            _row_copy(y_ref, _dest_row(ids_ref, tm, k, r), buf_ref.at[s, k], r, sems.at[s]).start()

    def wait_tile():
        for k in range(2):
            pltpu.make_async_copy(y_ref.at[pl.ds(0, tm), :], buf_ref.at[slot, k], sems.at[slot]).wait()

    def finish_tile():
        route = route_ref[...]
        ffn = route[:, 2:3] * buf_ref[slot, 0] + route[:, 3:4] * buf_ref[slot, 1]
        o_ref[...] = _layer_norm(alpha * h_ref[...] + ffn, g_ref[...], b_ref[...])

    @pl.when(i == 0)
    def _():
        def issue(r, _):
            start_row(dest_ref, 0, r)
            return 0
        lax.fori_loop(0, tm, issue, 0, unroll=4)

    @pl.when(i + 1 < pl.num_programs(0))
    def _():
        wait_tile()
        for r in range(tm):
            start_row(ndest_ref, 1 - slot, r)
        finish_tile()

    @pl.when(i + 1 == pl.num_programs(0))
    def _():
        wait_tile()
        finish_tile()


def _combine(dest_t, route, h, g, bta, y, alpha):
    n, d = h.shape
    tm = COMB_TM
    const = lambda i: (0, 0)
    return pl.pallas_call(
        functools.partial(_combine_kernel, alpha),
        grid=(n // tm,),
        in_specs=[pl.BlockSpec((1, 1, 2 * tm), lambda i: (i, 0, 0), memory_space=pltpu.SMEM),
                  pl.BlockSpec((1, 1, 2 * tm), lambda i: (jnp.minimum(i + 1, n // tm - 1), 0, 0),
                               memory_space=pltpu.SMEM),
                  pl.BlockSpec((tm, LANES), lambda i: (i, 0)),
                  pl.BlockSpec((tm, d), lambda i: (i, 0)),
                  pl.BlockSpec((1, d), const), pl.BlockSpec((1, d), const),
                  pl.BlockSpec(memory_space=pl.ANY)],
        out_specs=pl.BlockSpec((tm, d), lambda i: (i, 0)),
        out_shape=jax.ShapeDtypeStruct((n, d), f32),
        scratch_shapes=[pltpu.VMEM((2, 2, tm, d), f32), pltpu.SemaphoreType.DMA((2,))],
        compiler_params=_cparams(("arbitrary",)),
        name="combine",
    )(dest_t, dest_t, route, h, g, bta, y)


def _tile_dest(dest, tm):
    n = dest.shape[1]
    return dest.reshape(2, n // tm, tm).transpose(1, 0, 2).reshape(n // tm, 1, 2 * tm)


def _layer(x, positions, w_in, cmp_k_pe, cmp_k_w1, cmp_k_b1, cmp_k_w2, cmp_v_pe, cmp_v_w1, cmp_v_b1, cmp_v_w2,
           conv_w, conv_b, dt_bias, a_log, d_skip, ssm_norm_w, w_out, ln1_g, ln1_b,
           w_router_group, b_router_group, w_router_expert, b_router_expert, w_gate, w_up, w_down, ln2_g, ln2_b,
           alpha):
    b, t, d = x.shape
    n = b * t
    assert t % max(ATT_TK, ATT_TQ, CMP_TQ, CHUNK, PREP_TM) == 0 and n % max(PROJ_TM, OUT_TM, DISP_TM) == 0
    assert t // SEL_BLOCK <= LANES and (t // CMP_STRIDE) % LANES == 0 and t >= WINDOW + ATT_TQ
    x2 = x.reshape(n, d)

    c0 = NSA_WIDTH
    c1 = c0 + 6 * KV_WIDTH
    c2 = c1 + 3 * NSA_HEADS
    c3 = c2 + SSM_WIDTH
    c4 = c3 + XBC_WIDTH
    w_small = jnp.concatenate([w_in[:, c1:c2], w_in[:, c4:], jnp.zeros((d, LANES - 3 * NSA_HEADS - SSM_HEADS), f32)], axis=1)
    w_cat = jnp.concatenate([w_in[:, :c0], w_in[:, c2:c3], w_in[:, c3:c4], w_in[:, c0:c1], w_small,
                             jnp.zeros((d, PROJ_COLS - COL_SMALL - LANES), f32)], axis=1).astype(bf16)
    proj = _proj(x2, w_cat)

    lane = np.arange(LANES) % HEAD_DIM
    inv_freq = ROPE_THETA ** (-jnp.arange(0, ROT_DIM, 2, dtype=f32) / ROT_DIM)
    invf = jnp.where(lane < ROT_DIM, inv_freq[lane % (ROT_DIM // 2)], 0.0).astype(f32)[None, :]
    pos128 = jnp.broadcast_to(positions.reshape(n, 1), (n, LANES))
    q_r, k_cmp, v_cmp, k_sel, v_sel, k_win, v_win = _nsa_prep(proj, pos128, invf, b, t)

    nc = t // CMP_STRIDE
    half_w = CMP_STRIDE * HEAD_DIM
    a = jnp.stack([k_cmp, v_cmp]).reshape(2, b * NSA_KV_GROUPS, nc, half_w)
    pe = jnp.stack([cmp_k_pe, cmp_v_pe]).reshape(2, 2, 1, half_w)
    w1 = jnp.stack([cmp_k_w1, cmp_v_w1]).reshape(2, 2, half_w, CMP_HIDDEN).astype(bf16)
    b1 = jnp.stack([cmp_k_b1, cmp_v_b1]).reshape(2, 1, CMP_HIDDEN)
    w2 = jnp.pad(jnp.stack([cmp_k_w2, cmp_v_w2]), ((0, 0), (0, 0), (0, LANES - HEAD_DIM))).astype(bf16)
    cend = jnp.minimum(jnp.arange(nc) * CMP_STRIDE + CMP_BLOCK - 1, t - 1)
    posc = jnp.broadcast_to(positions[:, cend][:, :, None], (b, nc, LANES))
    kvc, kvc_t = _cmp_mlp(a, pe, w1, b1, w2, posc, invf, b)

    c_start = np.arange(nc)[:, None] * CMP_STRIDE
    s_start = np.arange(LANES)[None, :] * SEL_BLOCK
    cover = ((c_start < s_start + SEL_BLOCK) & (c_start + CMP_BLOCK > s_start)
             & (np.arange(nc)[:, None] < nc - 1) & (np.arange(LANES)[None, :] < t // SEL_BLOCK))
    cover = jnp.asarray(cover, bf16)
    o_cmp, selb = _cmp_attn(q_r, kvc, kvc_t, cover, b, t)

    y_nsa = _nsa_attn(q_r, k_sel, v_sel, k_win, v_win, selb, o_cmp, proj, b, t)

    dt_raw = proj[:, COL_SMALL + SMALL_DT_OFF:COL_SMALL + SMALL_DT_OFF + SSM_HEADS]
    dtt = dt_raw.reshape(b, t, SSM_HEADS).transpose(0, 2, 1)
    y_ssm = _ssd(proj, dtt, conv_w.reshape(CONV_WIDTH, XBC_WIDTH), conv_b.reshape(1, XBC_WIDTH),
                 dt_bias.reshape(1, SSM_HEADS), dt_bias.reshape(SSM_HEADS, 1),
                 a_log.reshape(1, SSM_HEADS), a_log.reshape(SSM_HEADS, 1),
                 jnp.repeat(d_skip, SSM_HEADDIM).reshape(1, SSM_WIDTH), ssm_norm_w.reshape(1, SSM_WIDTH), b, t)

    wr = jnp.concatenate([w_router_group, w_router_expert,
                          jnp.zeros((d, LANES - N_EXPERT_GROUPS - N_EXPERTS), f32)], axis=1)
    br = jnp.concatenate([b_router_group, b_router_expert,
                          jnp.zeros((LANES - N_EXPERT_GROUPS - N_EXPERTS,), f32)])[None, :]
    wr_hi = wr.astype(bf16)
    wr = jnp.concatenate([wr_hi, (wr - wr_hi.astype(f32)).astype(bf16)], axis=1)
    wo = w_out.astype(bf16)
    h, route, counts = _outproj(y_nsa, y_ssm, x2, wo[:NSA_WIDTH], wo[NSA_WIDTH:], ln1_g[None, :], ln1_b[None, :],
                                wr, br, alpha)

    cnt = counts[0, :N_EXPERTS].astype(i32)
    padded = (cnt + MOE_TM - 1) // MOE_TM * MOE_TM
    pad_ends = jnp.cumsum(padded)
    pad_starts = pad_ends - padded
    e01 = route[:, 0:2].astype(i32).T
    start01 = jnp.sum(jnp.where(e01[..., None] == jnp.arange(N_EXPERTS, dtype=i32), pad_starts, 0), axis=-1)
    dest = start01 + route[:, 4:6].astype(i32).T
    p_rows = 2 * n + N_EXPERTS * MOE_TM
    nb = p_rows // MOE_TM
    block_e = jnp.minimum(jnp.sum(jnp.arange(nb, dtype=i32)[:, None] * MOE_TM >= pad_ends[None, :], axis=-1),
                          N_EXPERTS - 1).astype(i32)
    n_used = (pad_ends[-1] // MOE_TM).astype(i32).reshape(1)
    blk = jnp.arange(nb, dtype=i32)
    last_of_expert = jnp.any((blk[:, None] + 1) * MOE_TM == pad_ends[None, :], axis=-1)
    zflag = (last_of_expert | (blk >= n_used[0])).astype(i32)
    block_e = jnp.where(blk < n_used[0], block_e, block_e[jnp.maximum(n_used[0] - 1, 0)])

    xs = _dispatch(_tile_dest(dest, DISP_TM), zflag, h, p_rows)
    prev_e = jnp.concatenate([jnp.full((1,), -1, i32), block_e[:-1]])
    first = ((block_e != prev_e) & (blk < n_used[0])).astype(i32)
    eidx = jnp.arange(N_EXPERTS, dtype=i32)
    later = jnp.where((eidx[None, :] > eidx[:, None]) & (padded[None, :] > 0), eidx[None, :], N_EXPERTS)
    next_of = jnp.min(later, axis=1)
    next_e = jnp.where(next_of[block_e] < N_EXPERTS, next_of[block_e], -1).astype(i32)
    slot = ((jnp.cumsum(first) - 1) % 2).astype(i32)
    y = _experts(block_e, n_used, first, next_e, slot, xs, w_gate, w_up, w_down)
    out = _combine(_tile_dest(dest, COMB_TM), route, h, ln2_g[None, :], ln2_b[None, :], y, alpha)
    return out.reshape(b, t, d)


def kernel(x, positions, w_in, cmp_k_pe, cmp_k_w1, cmp_k_b1, cmp_k_w2, cmp_v_pe, cmp_v_w1, cmp_v_b1, cmp_v_w2, conv_w, conv_b, dt_bias, a_log, d_skip, ssm_norm_w, w_out, ln1_g, ln1_b, w_router_group, b_router_group, w_router_expert, b_router_expert, w_gate, w_up, w_down, ln2_g, ln2_b):
    depth = w_in.shape[0]
    alpha = (2 * depth) ** 0.25
    params = (w_in, cmp_k_pe, cmp_k_w1, cmp_k_b1, cmp_k_w2, cmp_v_pe, cmp_v_w1, cmp_v_b1, cmp_v_w2, conv_w, conv_b,
              dt_bias, a_log, d_skip, ssm_norm_w, w_out, ln1_g, ln1_b, w_router_group, b_router_group,
              w_router_expert, b_router_expert, w_gate, w_up, w_down, ln2_g, ln2_b)
    for l in range(depth):
        x = _layer(x, positions, *[p[l] for p in params], alpha)
    return x
```

```python
import functools

import jax
import jax.numpy as jnp
import numpy as np
from jax import lax
from jax.experimental import pallas as pl
from jax.experimental.pallas import tpu as pltpu

f32 = jnp.float32
bf16 = jnp.bfloat16
i32 = jnp.int32

HEAD_DIM = 64
NSA_HEADS = 16
NSA_KV_GROUPS = 2
NSA_REP = NSA_HEADS // NSA_KV_GROUPS
NSA_WIDTH = NSA_HEADS * HEAD_DIM
KV_WIDTH = NSA_KV_GROUPS * HEAD_DIM
CMP_BLOCK = 32
CMP_STRIDE = 16
CMP_HIDDEN = 256
SEL_BLOCK = 64
SEL_TOPK = 16
WINDOW = 512
FORCED_SCORE = 1.0e4
SSM_HEADDIM = 64
SSM_HEADS = 16
SSM_WIDTH = SSM_HEADS * SSM_HEADDIM
SSM_GROUPS = 4
SSM_REP = SSM_HEADS // SSM_GROUPS
SSM_STATE = 128
CONV_WIDTH = 4
CHUNK = 256
XBC_WIDTH = SSM_WIDTH + 2 * SSM_GROUPS * SSM_STATE
ROPE_THETA = 500000.0
ROT_DIM = HEAD_DIM // 4
N_EXPERT_GROUPS = 4
EXPERTS_PER_GROUP = 8
N_EXPERTS = N_EXPERT_GROUPS * EXPERTS_PER_GROUP
NORM_EPS = 1e-5

LANES = 128
V7X_VMEM_BYTES = 64 * 1024 * 1024
MASK_NEG = -1.0e30

PROJ_TM = 1024
PROJ_TN = 1024
PREP_TM = 512
CMP_TQ = 256
ATT_TQ = 256
ATT_TK = 512
WIN_TK = 256
WIN_HEADS = 2
KX_WIDTH = 2 * LANES
OUT_TM = 512
MOE_TM = 256
DISP_TM = 256
COMB_TM = 128
VMEM_LIMIT = V7X_VMEM_BYTES - 8 * 1024 * 1024

COL_Q = 0
COL_Z = NSA_WIDTH
COL_XBC = COL_Z + SSM_WIDTH
COL_KV = COL_XBC + XBC_WIDTH
COL_SMALL = COL_KV + 6 * KV_WIDTH
PROJ_COLS = -(-(COL_SMALL + LANES) // PROJ_TN) * PROJ_TN
SMALL_DT_OFF = 3 * NSA_HEADS


def _cparams(sem, vmem=VMEM_LIMIT):
    return pltpu.CompilerParams(dimension_semantics=sem, vmem_limit_bytes=vmem)


def _dot(a, b):
    return jnp.dot(a, b, preferred_element_type=f32)


def _dot_t(a, b):
    return lax.dot_general(a, b, (((1,), (1,)), ((), ())), preferred_element_type=f32)


def _dot_hi(a, b):
    return jnp.dot(a, b, preferred_element_type=f32, precision=lax.Precision.HIGHEST)


def _proj_kernel(x_ref, w_ref, o_ref, xb_ref):
    @pl.when(pl.program_id(1) == 0)
    def _():
        xb_ref[...] = x_ref[...].astype(bf16)

    o_ref[...] = _dot(xb_ref[...], w_ref[...])


def _proj(x2, w_cat):
    n, d = x2.shape
    cols = w_cat.shape[1]
    tm = min(PROJ_TM, n)
    return pl.pallas_call(
        _proj_kernel,
        grid=(n // tm, cols // PROJ_TN),
        in_specs=[pl.BlockSpec((tm, d), lambda i, j: (i, 0)),
                  pl.BlockSpec((d, PROJ_TN), lambda i, j: (0, j))],
        out_specs=pl.BlockSpec((tm, PROJ_TN), lambda i, j: (i, j)),
        out_shape=jax.ShapeDtypeStruct((n, cols), f32),
        scratch_shapes=[pltpu.VMEM((tm, d), bf16)],
        compiler_params=_cparams(("parallel", "arbitrary")),
        name="proj",
    )(x2, w_cat)


def _rope_tables(pos_i32, invf):
    ang = pos_i32.astype(f32) * invf
    return jnp.cos(ang), jnp.sin(ang)


def _rope128(x, cos, sin):
    half = ROT_DIM // 2
    d = lax.broadcasted_iota(i32, x.shape, 1) % HEAD_DIM
    up = pltpu.roll(x, LANES - half, 1)
    dn = pltpu.roll(x, half, 1)
    rot = jnp.where(d < half, -up, dn)
    return x * cos + rot * sin


def _nsa_prep_kernel(pos_ref, invf_ref, q_ref, kc_ref, vc_ref, ks_ref, vs_ref, kw_ref, vw_ref,
                     qo_ref, cmp_ref, kso_ref, vso_ref, kwo_ref, vwo_ref):
    cos, sin = _rope_tables(pos_ref[...], invf_ref[...])
    scale = HEAD_DIM ** -0.5
    for c in range(NSA_WIDTH // LANES):
        t = _rope128(q_ref[:, c * LANES:(c + 1) * LANES], cos, sin) * scale
        qo_ref[0, 2 * c] = t[:, :HEAD_DIM].astype(bf16)
        qo_ref[0, 2 * c + 1] = t[:, HEAD_DIM:].astype(bf16)

    tm = pos_ref.shape[0]

    half_rows = tm // CMP_STRIDE
    for kind, src in enumerate((kc_ref, vc_ref)):
        for j in range(CMP_STRIDE):
            tok_j = src[pl.ds(j, half_rows, stride=CMP_STRIDE), :]
            for g in range(NSA_KV_GROUPS):
                cmp_ref[kind, 0, g, :, j * HEAD_DIM:(j + 1) * HEAD_DIM] = tok_j[:, g * HEAD_DIM:(g + 1) * HEAD_DIM]

    def values_with_ones(src, dst):
        t = src[...]
        tail = jnp.ones((tm, HEAD_DIM), bf16)
        for g in range(NSA_KV_GROUPS):
            vg = t[:, g * HEAD_DIM:(g + 1) * HEAD_DIM].astype(bf16)
            dst[0, g] = jnp.concatenate([vg, tail], axis=1)

    values_with_ones(vs_ref, vso_ref)
    values_with_ones(vw_ref, vwo_ref)

    kw_t = _rope128(kw_ref[...], cos, sin).T
    ks_t = _rope128(ks_ref[...], cos, sin).T
    for g in range(NSA_KV_GROUPS):
        for c in range(tm // WIN_TK):
            kwo_ref[0, g, c] = kw_t[g * HEAD_DIM:(g + 1) * HEAD_DIM, c * WIN_TK:(c + 1) * WIN_TK].astype(bf16)
    blk = lax.broadcasted_iota(i32, (LANES, tm), 0)
    tok = pl.program_id(1) * tm + lax.broadcasted_iota(i32, (LANES, tm), 1)
    onehot_t = jnp.where(tok // SEL_BLOCK == blk, 1.0, 0.0).astype(bf16)
    for g in range(NSA_KV_GROUPS):
        kso_ref[0, g, 0, 0:LANES, :] = onehot_t
        kg = ks_t[g * HEAD_DIM:(g + 1) * HEAD_DIM, :].astype(bf16)
        kso_ref[0, g, 0, LANES:2 * LANES, :] = jnp.concatenate([kg, jnp.zeros_like(kg)], axis=0)


def _nsa_prep(proj, pos128, invf, b, t):
    tm = PREP_TM
    assert tm == ATT_TK
    nt = t // tm
    row = lambda bi, ti: (bi * nt + ti, 0)
    kv0 = COL_KV // LANES
    in_specs = [pl.BlockSpec((tm, LANES), row),
                pl.BlockSpec((1, LANES), lambda bi, ti: (0, 0)),
                pl.BlockSpec((tm, NSA_WIDTH), lambda bi, ti: (bi * nt + ti, COL_Q // NSA_WIDTH))]
    for k in range(6):
        in_specs.append(pl.BlockSpec((tm, LANES), functools.partial(lambda bi, ti, k: (bi * nt + ti, kv0 + k), k=k)))
    head = lambda bi, ti: (bi, 0, ti, 0)
    tile5 = lambda bi, ti: (bi, 0, ti, 0, 0)
    g = NSA_KV_GROUPS
    out_specs = [pl.BlockSpec((1, NSA_HEADS, tm, HEAD_DIM), head),
                 pl.BlockSpec((2, 1, g, tm // CMP_STRIDE, CMP_STRIDE * HEAD_DIM), lambda bi, ti: (0, bi, 0, ti, 0)),
                 pl.BlockSpec((1, g, 1, KX_WIDTH, tm), tile5), pl.BlockSpec((1, g, tm, LANES), head),
                 pl.BlockSpec((1, g, tm // WIN_TK, HEAD_DIM, WIN_TK), tile5), pl.BlockSpec((1, g, tm, LANES), head)]
    out_shape = [jax.ShapeDtypeStruct((b, NSA_HEADS, t, HEAD_DIM), bf16),
                 jax.ShapeDtypeStruct((2, b, g, t // CMP_STRIDE, CMP_STRIDE * HEAD_DIM), f32),
                 jax.ShapeDtypeStruct((b, g, nt, KX_WIDTH, tm), bf16), jax.ShapeDtypeStruct((b, g, t, LANES), bf16),
                 jax.ShapeDtypeStruct((b, g, t // WIN_TK, HEAD_DIM, WIN_TK), bf16),
                 jax.ShapeDtypeStruct((b, g, t, LANES), bf16)]
    return pl.pallas_call(
        _nsa_prep_kernel,
        grid=(b, nt),
        in_specs=in_specs,
        out_specs=out_specs,
        out_shape=out_shape,
        compiler_params=_cparams(("parallel", "parallel")),
        name="nsa_prep",
    )(pos128, invf, proj, proj, proj, proj, proj, proj, proj)


def _cmp_mlp_kernel(a_ref, pe_ref, w1_ref, b1_ref, w2_ref, pos_ref, invf_ref, o_ref, ot_ref):
    kind = pl.program_id(0)
    a = a_ref[0, 0]
    nc = a.shape[0]
    u = _dot((a + pe_ref[0, 0]).astype(bf16), w1_ref[0, 0])
    v = _dot((a + pe_ref[0, 1]).astype(bf16), w1_ref[0, 1])
    v_next = pltpu.roll(v, nc - 1, 0)
    hid = jax.nn.gelu(u + v_next + b1_ref[0])
    out = _dot(hid.astype(bf16), w2_ref[0])
    cos, sin = _rope_tables(pos_ref[0], invf_ref[...])
    roped = _rope128(out, cos, sin)
    out = jnp.where(kind == 0, roped, out)
    o_ref[0, 0] = out[:, :HEAD_DIM].astype(bf16)
    ot_ref[0, 0] = out.T[:HEAD_DIM, :].astype(bf16)


def _cmp_mlp(a, pe, w1, b1, w2, posc, invf, b):
    _, bg, nc, hw = a.shape
    g = bg // b
    return pl.pallas_call(
        _cmp_mlp_kernel,
        grid=(2, bg),
        in_specs=[pl.BlockSpec((1, 1, nc, hw), lambda k, i: (k, i, 0, 0)),
                  pl.BlockSpec((1, 2, 1, hw), lambda k, i: (k, 0, 0, 0)),
                  pl.BlockSpec((1, 2, hw, CMP_HIDDEN), lambda k, i: (k, 0, 0, 0)),
                  pl.BlockSpec((1, 1, CMP_HIDDEN), lambda k, i: (k, 0, 0)),
                  pl.BlockSpec((1, CMP_HIDDEN, LANES), lambda k, i: (k, 0, 0)),
                  pl.BlockSpec((1, nc, LANES), lambda k, i: (i // g, 0, 0)),
                  pl.BlockSpec((1, LANES), lambda k, i: (0, 0))],
        out_specs=[pl.BlockSpec((1, 1, nc, HEAD_DIM), lambda k, i: (k, i, 0, 0)),
                   pl.BlockSpec((1, 1, HEAD_DIM, nc), lambda k, i: (k, i, 0, 0))],
        out_shape=[jax.ShapeDtypeStruct((2, bg, nc, HEAD_DIM), bf16),
                   jax.ShapeDtypeStruct((2, bg, HEAD_DIM, nc), bf16)],
        compiler_params=_cparams(("parallel", "parallel")),
        name="cmp_mlp",
    )(a, pe, w1, b1, w2, posc, invf)


def _cmp_attn_kernel(q_ref, kct_ref, vc_ref, cover_ref, oc_ref, sel_ref):
    qi = pl.program_id(2)
    nc = vc_ref.shape[2]
    rows = NSA_REP * CMP_TQ
    tq = qi * CMP_TQ + lax.broadcasted_iota(i32, (CMP_TQ, 1), 0)
    row_live = jnp.where(tq >= CMP_BLOCK - 1, 1.0, 0.0)
    tiny = jnp.finfo(f32).tiny

    def attend(ncols):
        kct = kct_ref[0, 0, :, :ncols]
        vc = vc_ref[0, 0, :ncols, :]
        cend = lax.broadcasted_iota(i32, (1, ncols), 1) * CMP_STRIDE + (CMP_BLOCK - 1)
        bias = jnp.where(cend <= tq, 0.0, MASK_NEG)
        s = _dot(q_ref[0].reshape(rows, HEAD_DIM), kct).reshape(NSA_REP, CMP_TQ, ncols) + bias[None]
        e = jnp.exp(s - jnp.max(s, axis=-1, keepdims=True))
        live = row_live[None]
        p = e * (live / jnp.maximum(live * jnp.sum(e, axis=-1, keepdims=True), tiny))
        oc_ref[0] = _dot(p.reshape(rows, ncols).astype(bf16), vc).reshape(NSA_REP, CMP_TQ, HEAD_DIM)
        psum = jnp.sum(p, axis=0)
        hi = psum.astype(bf16)
        lo = (psum - hi.astype(f32)).astype(bf16)
        cover = cover_ref[:ncols, :]
        imp = _dot(hi, cover) + _dot(lo, cover)

        j = lax.broadcasted_iota(i32, (CMP_TQ, LANES), 1)
        cur = tq // SEL_BLOCK
        forced = (j == 0) | (j == cur) | (j == cur - 1)
        valid = j * SEL_BLOCK <= tq
        imp = jnp.where(valid, jnp.where(forced, FORCED_SCORE, imp), -FORCED_SCORE)

        nblk = ncols * CMP_STRIDE // SEL_BLOCK
        jt = lax.broadcasted_iota(i32, (nblk, CMP_TQ), 0)

        def pick(_, carry):
            work, sel = carry
            m = jnp.max(work, axis=0, keepdims=True)
            first = jnp.min(jnp.where(work == m, jt, LANES), axis=0, keepdims=True)
            hit = jt == first
            return jnp.where(hit, -jnp.inf, work), jnp.where(hit, 1.0, sel)

        _, sel_t = lax.fori_loop(0, SEL_TOPK, pick, (imp.T[:nblk], jnp.zeros((nblk, CMP_TQ), f32)))
        if nblk < LANES:
            sel_t = jnp.concatenate([sel_t, jnp.zeros((LANES - nblk, CMP_TQ), f32)], axis=0)
        sel_ref[0, 0] = jnp.where(valid, jnp.where(sel_t.T > 0.0, 0.0, MASK_NEG), MASK_NEG).astype(bf16)

    need = (qi + 1) * (CMP_TQ // CMP_STRIDE)
    for ncols in range(LANES, nc + 1, LANES):
        @pl.when((need > ncols - LANES) & (need <= ncols))
        def _():
            attend(ncols)


def _cmp_attn(q_r, kvc, kvc_t, cover, b, t):
    g = NSA_KV_GROUPS
    nc = kvc.shape[2]
    nq = t // CMP_TQ
    return pl.pallas_call(
        _cmp_attn_kernel,
        grid=(b, g, nq),
        in_specs=[pl.BlockSpec((1, NSA_REP, CMP_TQ, HEAD_DIM), lambda bi, gi, qi: (bi, gi, qi, 0)),
                  pl.BlockSpec((1, 1, HEAD_DIM, nc), lambda bi, gi, qi: (0, bi * g + gi, 0, 0)),
                  pl.BlockSpec((1, 1, nc, HEAD_DIM), lambda bi, gi, qi: (1, bi * g + gi, 0, 0)),
                  pl.BlockSpec((nc, LANES), lambda bi, gi, qi: (0, 0))],
        out_specs=[pl.BlockSpec((1, NSA_REP, CMP_TQ, HEAD_DIM), lambda bi, gi, qi: (bi, gi, qi, 0)),
                   pl.BlockSpec((1, 1, CMP_TQ, LANES), lambda bi, gi, qi: (bi, gi, qi, 0))],
        out_shape=[jax.ShapeDtypeStruct((b, NSA_HEADS, t, HEAD_DIM), f32),
                   jax.ShapeDtypeStruct((b, g, t, LANES), bf16)],
        compiler_params=_cparams(("parallel", "parallel", "parallel")),
        name="cmp_attn",
    )(q_r, kvc_t, kvc, cover)


def _nsa_attn_kernel(q_ref, kx_ref, vs_ref, kw_ref, vw_ref, sel_ref, oc_ref, gate_ref,
                     o_ref, qx_ref, s_ref, m_ref, acc_ref, yw_ref):
    gi = pl.program_id(1)
    qi = pl.program_id(2)
    start = qi * ATT_TQ
    tq = start + lax.broadcasted_iota(i32, (ATT_TQ, 1), 0)

    selb = sel_ref[0, 0]
    for r in range(NSA_REP):
        qx_ref[r * ATT_TQ:(r + 1) * ATT_TQ, 0:LANES] = selb
        qr = q_ref[0, r]
        qx_ref[r * ATT_TQ:(r + 1) * ATT_TQ, LANES:2 * LANES] = jnp.concatenate([qr, jnp.zeros_like(qr)], axis=1)
    rows = NSA_REP * ATT_TQ

    m_ref[...] = jnp.full(m_ref.shape, MASK_NEG, f32)
    acc_ref[...] = jnp.zeros(acc_ref.shape, f32)

    def consume(kt, v_ref, bias, parts=1):
        k0 = pl.multiple_of(kt * ATT_TK, ATT_TK)
        v = v_ref[0, 0, pl.ds(k0, ATT_TK), :]
        heads = NSA_REP // parts
        for part in range(parts):
            rs = slice(part * heads * ATT_TQ, (part + 1) * heads * ATT_TQ)
            s = s_ref[rs]
            if bias is not None:
                s = (s.reshape(heads, ATT_TQ, ATT_TK) + bias[None]).reshape(heads * ATT_TQ, ATT_TK)
            m_prev = m_ref[rs]
            m_new = jnp.maximum(m_prev, jnp.max(s, axis=-1, keepdims=True))
            alpha = jnp.exp(m_prev - m_new)
            p = jnp.exp(s - jnp.concatenate([m_new] * (ATT_TK // LANES), axis=1))
            acc_ref[rs] = alpha * acc_ref[rs] + _dot(p.astype(bf16), v)
            m_ref[rs] = m_new

    def kpos(kt):
        return kt * ATT_TK + lax.broadcasted_iota(i32, (1, ATT_TK), 1)

    last = start // ATT_TK
    for part in range(2):
        rs = slice(part * rows // 2, (part + 1) * rows // 2)
        s_ref[rs] = _dot(qx_ref[rs], kx_ref[0, 0, 0])

    def sel_step(kt, _):
        consume(kt, vs_ref, None)
        s_ref[...] = _dot(qx_ref[...], kx_ref[0, 0, kt + 1])
        return 0

    n_win = (WINDOW + ATT_TQ) // WIN_TK
    w_first = jnp.maximum(start // WIN_TK - WINDOW // WIN_TK, 0)
    kw = jnp.concatenate([kw_ref[0, 0, w_first + i] for i in range(n_win)], axis=1)
    w0 = pl.multiple_of(w_first * WIN_TK, WIN_TK)
    vw = vw_ref[0, 0, pl.ds(w0, n_win * WIN_TK), :]
    kp = w0 + lax.broadcasted_iota(i32, (1, n_win * WIN_TK), 1)
    wbias = jnp.where(kp <= tq, jnp.where(kp > tq - WINDOW, 0.0, MASK_NEG), MASK_NEG)
    hh = WIN_HEADS
    gates = jax.nn.sigmoid(gate_ref[...])
    per_group = 3 * NSA_REP
    shifted = gates
    for gg in range(1, NSA_KV_GROUPS):
        shifted = jnp.where(gi == gg, pltpu.roll(gates, LANES - gg * per_group, 1), shifted)

    def gate(r, br):
        c = r * 3 + br
        return jnp.broadcast_to(shifted[:, c:c + 1], (ATT_TQ, HEAD_DIM))

    upper = pltpu.roll(shifted, HEAD_DIM, 1)[:, HEAD_DIM:]

    def gated_output(a, r, br):
        c = r * 3 + br
        ratio = upper / a[:, HEAD_DIM:]
        return a[:, :HEAD_DIM] * jnp.broadcast_to(ratio[:, c:c + 1], (ATT_TQ, HEAD_DIM))

    for half in range(NSA_REP // hh):
        qh = q_ref[0, half * hh:(half + 1) * hh].reshape(hh * ATT_TQ, HEAD_DIM)
        sw = _dot(qh, kw).reshape(hh, ATT_TQ, n_win * WIN_TK) + wbias[None]
        sw = sw.reshape(hh * ATT_TQ, n_win * WIN_TK)
        pw = jnp.exp(sw - jnp.max(sw, axis=-1, keepdims=True))
        aw = _dot(pw.astype(bf16), vw)
        for i in range(hh):
            r = half * hh + i
            a = aw[i * ATT_TQ:(i + 1) * ATT_TQ]
            yw_ref[r] = gated_output(a, r, 2)

    def sel_pair(j, _):
        sel_step(2 * j, 0)
        sel_step(2 * j + 1, 0)
        return 0

    lax.fori_loop(0, last // 2, sel_pair, 0)

    @pl.when(last % 2 == 1)
    def _():
        sel_step(last - 1, 0)

    consume(last, vs_ref, jnp.where(kpos(last) <= tq, 0.0, MASK_NEG), parts=2)

    for r in range(NSA_REP):
        osel = gated_output(acc_ref[r * ATT_TQ:(r + 1) * ATT_TQ], r, 1)
        o_ref[:, r * HEAD_DIM:(r + 1) * HEAD_DIM] = (gate(r, 0) * oc_ref[0, r] + osel + yw_ref[r]).astype(o_ref.dtype)


def _nsa_attn(q_r, kx, vs, kw, vw, selb, oc, proj, b, t):
    g = NSA_KV_GROUPS
    nq = t // ATT_TQ
    once = dict(pipeline_mode=pl.Buffered(1))
    vspec = pl.BlockSpec((1, 1, t, LANES), lambda bi, gi, qi: (bi, gi, 0, 0), **once)
    hspec = pl.BlockSpec((1, NSA_REP, ATT_TQ, HEAD_DIM), lambda bi, gi, qi: (bi, gi, qi, 0))
    return pl.pallas_call(
        _nsa_attn_kernel,
        grid=(b, g, nq),
        in_specs=[hspec,
                  pl.BlockSpec((1, 1, t // ATT_TK, KX_WIDTH, ATT_TK), lambda bi, gi, qi: (bi, gi, 0, 0, 0), **once),
                  vspec,
                  pl.BlockSpec((1, 1, t // WIN_TK, HEAD_DIM, WIN_TK), lambda bi, gi, qi: (bi, gi, 0, 0, 0), **once),
                  vspec,
                  pl.BlockSpec((1, 1, ATT_TQ, LANES), lambda bi, gi, qi: (bi, gi, qi, 0)),
                  hspec,
                  pl.BlockSpec((ATT_TQ, LANES), lambda bi, gi, qi: (bi * nq + qi, COL_SMALL // LANES))],
        out_specs=pl.BlockSpec((ATT_TQ, NSA_REP * HEAD_DIM), lambda bi, gi, qi: (bi * nq + qi, gi)),
        out_shape=jax.ShapeDtypeStruct((b * t, NSA_WIDTH), bf16),
        scratch_shapes=[pltpu.VMEM((NSA_REP * ATT_TQ, KX_WIDTH), bf16),
                        pltpu.VMEM((NSA_REP * ATT_TQ, ATT_TK), f32),
                        pltpu.VMEM((NSA_REP * ATT_TQ, LANES), f32),
                        pltpu.VMEM((NSA_REP * ATT_TQ, LANES), f32),
                        pltpu.VMEM((NSA_REP, ATT_TQ, HEAD_DIM), f32)],
        compiler_params=_cparams(("parallel", "parallel", "arbitrary")),
        name="nsa_attn",
    )(q_r, kx, vs, kw, vw, selb, oc, proj)


HALO = 8


def _ssd_kernel(xbc_ref, z_ref, small_ref, dtt_ref, cw_ref, cb_ref, dtb_r_ref, dtb_c_ref,
                alog_r_ref, alog_c_ref, dskip_ref, nw_ref, o_ref, ext_ref, st_ref):
    c = pl.program_id(1)
    L = CHUNK

    @pl.when(c == 0)
    def _():
        ext_ref[0:HALO, :] = jnp.zeros((HALO, XBC_WIDTH), f32)
        st_ref[...] = jnp.zeros_like(st_ref)

    ext_ref[HALO:HALO + L, :] = xbc_ref[...]
    conv = cb_ref[...]
    for k in range(CONV_WIDTH):
        off = HALO - (CONV_WIDTH - 1) + k
        conv = conv + cw_ref[k:k + 1, :] * ext_ref[off:off + L, :]
    ext_ref[0:HALO, :] = ext_ref[L:L + HALO, :]
    act = conv * jax.nn.sigmoid(conv)
    xs = act[:, :SSM_WIDTH]
    bm = act[:, SSM_WIDTH:SSM_WIDTH + SSM_GROUPS * SSM_STATE]
    cm = act[:, SSM_WIDTH + SSM_GROUPS * SSM_STATE:]

    dt_c = jax.nn.softplus(small_ref[:, SMALL_DT_OFF:SMALL_DT_OFF + SSM_HEADS] + dtb_r_ref[...])
    dt_r = jax.nn.softplus(dtt_ref[0] + dtb_c_ref[...])
    a_r = -jnp.exp(alog_r_ref[...])
    a_c = -jnp.exp(alog_c_ref[...])
    row = lax.broadcasted_iota(i32, (L, L), 0)
    col = lax.broadcasted_iota(i32, (L, L), 1)
    causal = col <= row
    tri = jnp.where(causal, 1.0, 0.0)
    acs_c = _dot_hi(tri, dt_c * a_r)
    acs_r = _dot_hi(dt_r * a_c, jnp.where(row <= col, 1.0, 0.0))

    z = z_ref[...]
    for g in range(SSM_GROUPS):
        cg = cm[:, g * SSM_STATE:(g + 1) * SSM_STATE].astype(bf16)
        bg = bm[:, g * SSM_STATE:(g + 1) * SSM_STATE]
        cb = _dot_t(cg, bg.astype(bf16))
        bg_t = bg.T
        ssq = jnp.zeros((L, 1), f32)
        yg = []
        for pp in range(SSM_REP // 2):
            pair = g * (SSM_REP // 2) + pp
            ps = slice(pair * LANES, (pair + 1) * LANES)
            x_p = xs[:, ps]
            low = lax.broadcasted_iota(i32, (L, LANES), 1) < SSM_HEADDIM
            low_n = lax.broadcasted_iota(i32, (SSM_STATE, LANES), 1) < SSM_HEADDIM
            x_own = (jnp.where(low, x_p, 0.0).astype(bf16), jnp.where(low, 0.0, x_p).astype(bf16))
            st = st_ref[pair]
            y = jnp.zeros((L, LANES), f32)
            st_in = jnp.zeros((SSM_STATE, LANES), f32)
            grow, keep = [], []
            for j in range(2):
                h = 2 * pair + j
                a_col = acs_c[:, h:h + 1]
                a_row = acs_r[h:h + 1, :]
                dt_row = dt_r[h:h + 1, :]
                a_last = acs_r[h:h + 1, L - 1:L]
                seg = a_col - a_row
                decay = jnp.where(causal, jnp.exp(jnp.where(causal, seg, 0.0)), 0.0)
                w = cb * decay * dt_row
                y = y + _dot(w.astype(bf16), x_own[j])
                bscaled = bg_t * (jnp.exp(a_last - a_row) * dt_row)
                st_in = st_in + _dot(bscaled.astype(bf16), x_own[j])
                grow.append(jnp.exp(a_col))
                keep.append(jnp.exp(a_last))
            y = y + _dot(cg, st.astype(bf16)) * jnp.where(low, grow[0], grow[1])
            st_ref[pair] = jnp.where(low_n, keep[0], keep[1]) * st + st_in
            y = y + dskip_ref[:, ps] * x_p
            zp = z[:, ps]
            y = y * (zp * jax.nn.sigmoid(zp))
            ssq = ssq + jnp.sum(y * y, axis=-1, keepdims=True)
            yg.append(y)
        rs = lax.rsqrt(ssq / (SSM_REP * SSM_HEADDIM) + NORM_EPS)
        for pp in range(SSM_REP // 2):
            pair = g * (SSM_REP // 2) + pp
            ps = slice(pair * LANES, (pair + 1) * LANES)
            o_ref[:, ps] = (yg[pp] * rs * nw_ref[:, ps]).astype(o_ref.dtype)


def _ssd(proj, dtt, cw, cb, dtb_r, dtb_c, alog_r, alog_c, dskip, nw, b, t):
    nch = t // CHUNK
    row = lambda bi, ci: bi * nch + ci
    const2 = lambda bi, ci: (0, 0)
    return pl.pallas_call(
        _ssd_kernel,
        grid=(b, nch),
        in_specs=[pl.BlockSpec((CHUNK, XBC_WIDTH), lambda bi, ci: (row(bi, ci), COL_XBC // XBC_WIDTH)),
                  pl.BlockSpec((CHUNK, SSM_WIDTH), lambda bi, ci: (row(bi, ci), COL_Z // SSM_WIDTH)),
                  pl.BlockSpec((CHUNK, LANES), lambda bi, ci: (row(bi, ci), COL_SMALL // LANES)),
                  pl.BlockSpec((1, SSM_HEADS, CHUNK), lambda bi, ci: (bi, 0, ci)),
                  pl.BlockSpec((CONV_WIDTH, XBC_WIDTH), const2),
                  pl.BlockSpec((1, XBC_WIDTH), const2),
                  pl.BlockSpec((1, SSM_HEADS), const2),
                  pl.BlockSpec((SSM_HEADS, 1), const2),
                  pl.BlockSpec((1, SSM_HEADS), const2),
                  pl.BlockSpec((SSM_HEADS, 1), const2),
                  pl.BlockSpec((1, SSM_WIDTH), const2),
                  pl.BlockSpec((1, SSM_WIDTH), const2)],
        out_specs=pl.BlockSpec((CHUNK, SSM_WIDTH), lambda bi, ci: (row(bi, ci), 0)),
        out_shape=jax.ShapeDtypeStruct((b * t, SSM_WIDTH), bf16),
        scratch_shapes=[pltpu.VMEM((HALO + CHUNK, XBC_WIDTH), f32),
                        pltpu.VMEM((SSM_HEADS // 2, SSM_STATE, 2 * SSM_HEADDIM), f32)],
        compiler_params=_cparams(("parallel", "arbitrary")),
        name="ssd",
    )(proj, proj, proj, dtt, cw, cb, dtb_r, dtb_c, alog_r, alog_c, dskip, nw)


def _layer_norm(v, g, b):
    mu = jnp.mean(v, axis=-1, keepdims=True)
    d = v - mu
    var = jnp.mean(d * d, axis=-1, keepdims=True)
    return d * lax.rsqrt(var + NORM_EPS) * g + b


def _outproj_kernel(alpha, ya_ref, yb_ref, x_ref, wa_ref, wb_ref, g_ref, b_ref, wr_ref, br_ref,
                    h_ref, route_ref, cnt_ref):
    i = pl.program_id(0)
    tm = x_ref.shape[0]
    mix = _dot(ya_ref[...], wa_ref[...]) + _dot(yb_ref[...], wb_ref[...])
    h = _layer_norm(alpha * x_ref[...] + mix, g_ref[...], b_ref[...])
    h_ref[...] = h

    h_hi = h.astype(bf16)
    h_lo = (h - h_hi.astype(f32)).astype(bf16)
    t = _dot(h_hi, wr_ref[...])
    logits = t[:, :LANES] + t[:, LANES:] + _dot(h_lo, wr_ref[:, :LANES]) + br_ref[...]
    lane = lax.broadcasted_iota(i32, (tm, LANES), 1)
    ninf = -jnp.inf
    gmask = lane < N_EXPERT_GROUPS
    gl = jnp.where(gmask, logits, ninf)
    ge = jnp.where(gmask, jnp.exp(gl - jnp.max(gl, axis=-1, keepdims=True)), 0.0)
    pg = ge / jnp.sum(ge, axis=-1, keepdims=True)
    g_gate = jnp.max(pg, axis=-1, keepdims=True)
    g_sel = jnp.min(jnp.where(gmask & (pg == g_gate), lane, LANES), axis=-1, keepdims=True)
    lo = N_EXPERT_GROUPS + g_sel * EXPERTS_PER_GROUP
    emask = (lane >= lo) & (lane < lo + EXPERTS_PER_GROUP)
    el = jnp.where(emask, logits, ninf)
    ee = jnp.where(emask, jnp.exp(el - jnp.max(el, axis=-1, keepdims=True)), 0.0)
    pe = ee / jnp.sum(ee, axis=-1, keepdims=True)
    p0 = jnp.max(pe, axis=-1, keepdims=True)
    l0 = jnp.min(jnp.where(emask & (pe == p0), lane, LANES), axis=-1, keepdims=True)
    rest = jnp.where(emask & (lane != l0), pe, ninf)
    p1 = jnp.max(rest, axis=-1, keepdims=True)
    l1 = jnp.min(jnp.where(rest == p1, lane, LANES), axis=-1, keepdims=True)
    psum = p0 + p1
    w0 = g_gate * p0 / psum
    w1 = g_gate * p1 / psum
    e0 = l0 - N_EXPERT_GROUPS
    e1 = l1 - N_EXPERT_GROUPS

    @pl.when(i == 0)
    def _():
        cnt_ref[...] = jnp.zeros_like(cnt_ref)

    oh0 = lane == e0
    oh1 = lane == e1
    both = jnp.where(oh0, 1.0, 0.0) + jnp.where(oh1, 1.0, 0.0)
    r_i = lax.broadcasted_iota(i32, (tm, tm), 0)
    c_i = lax.broadcasted_iota(i32, (tm, tm), 1)
    strict = jnp.where(c_i < r_i, 1.0, 0.0).astype(bf16)
    before = _dot(strict, both.astype(bf16)) + cnt_ref[...]
    rank0 = jnp.sum(jnp.where(oh0, before, 0.0), axis=-1, keepdims=True)
    rank1 = jnp.sum(jnp.where(oh1, before, 0.0), axis=-1, keepdims=True)
    cnt_ref[...] = cnt_ref[...] + jnp.sum(both, axis=0, keepdims=True)

    out = jnp.where(lane == 0, e0.astype(f32), 0.0)
    out = jnp.where(lane == 1, e1.astype(f32), out)
    out = jnp.where(lane == 2, w0, out)
    out = jnp.where(lane == 3, w1, out)
    out = jnp.where(lane == 4, rank0, out)
    out = jnp.where(lane == 5, rank1, out)
    route_ref[...] = out


def _outproj(y_nsa, y_ssm, x2, wa, wb, g, bta, wr, br, alpha):
    n, d = x2.shape
    tm = OUT_TM
    const = lambda i: (0, 0)
    rowb = lambda i: (i, 0)
    return pl.pallas_call(
        functools.partial(_outproj_kernel, alpha),
        grid=(n // tm,),
        in_specs=[pl.BlockSpec((tm, NSA_WIDTH), rowb), pl.BlockSpec((tm, SSM_WIDTH), rowb),
                  pl.BlockSpec((tm, d), rowb),
                  pl.BlockSpec((NSA_WIDTH, d), const), pl.BlockSpec((SSM_WIDTH, d), const),
                  pl.BlockSpec((1, d), const), pl.BlockSpec((1, d), const),
                  pl.BlockSpec((d, 2 * LANES), const), pl.BlockSpec((1, LANES), const)],
        out_specs=[pl.BlockSpec((tm, d), rowb), pl.BlockSpec((tm, LANES), rowb),
                   pl.BlockSpec((1, LANES), const)],
        out_shape=[jax.ShapeDtypeStruct((n, d), f32), jax.ShapeDtypeStruct((n, LANES), f32),
                   jax.ShapeDtypeStruct((1, LANES), f32)],
        compiler_params=_cparams(("arbitrary",)),
        name="outproj",
    )(y_nsa, y_ssm, x2, wa, wb, g, bta, wr, br)


def _row_copy(src_ref, src_row, dst_ref, dst_row, sem):
    return pltpu.make_async_copy(src_ref.at[pl.ds(src_row, 1), :], dst_ref.at[pl.ds(dst_row, 1), :], sem)


def _dest_row(dest_ref, tm, k, r):
    return dest_ref[0, 0, k * tm + r]


def _dispatch_kernel(dest_ref, zflag_ref, h_ref, xs_ref, zero_ref, tile_ref, sem, lsem, zsem):
    tm = DISP_TM
    nb = zflag_ref.shape[0]
    i = pl.program_id(0)

    @pl.when(i == 0)
    def _():
        zero_ref[...] = jnp.zeros_like(zero_ref)

        def zblock(i):
            rows = pl.ds(pl.multiple_of(i * MOE_TM, MOE_TM), MOE_TM)
            return pltpu.make_async_copy(zero_ref, xs_ref.at[rows, :], zsem)

        def zstart(i, _):
            @pl.when(zflag_ref[i] != 0)
            def _():
                zblock(i).start()
            return 0

        def zwait(i, _):
            @pl.when(zflag_ref[i] != 0)
            def _():
                zblock(i).wait()
            return 0

        lax.fori_loop(0, nb, zstart, 0)
        lax.fori_loop(0, nb, zwait, 0)

    nslot = tile_ref.shape[0]
    nsteps = pl.num_programs(0)
    slot = i % nslot

    def tile_load(step, s):
        rows = pl.ds(pl.multiple_of(step * tm, tm), tm)
        return pltpu.make_async_copy(h_ref.at[rows, :], tile_ref.at[s], lsem.at[s])

    @pl.when(i == 0)
    def _():
        tile_load(0, 0).start()

    tile_load(i, slot).wait()

    @pl.when(i + 1 < nsteps)
    def _():
        tile_load(i + 1, (i + 1) % nslot).start()

    def issue(r, _):
        for k in range(2):
            _row_copy(tile_ref.at[slot], r, xs_ref, _dest_row(dest_ref, tm, k, r), sem.at[slot]).start()
        return 0

    lax.fori_loop(0, tm, issue, 0, unroll=16)

    def wait_rows(s):
        for k in range(2):
            pltpu.make_async_copy(tile_ref.at[s], xs_ref.at[pl.ds(0, tm), :], sem.at[s]).wait()

    @pl.when(i > 0)
    def _():
        wait_rows((i - 1) % nslot)

    @pl.when(i == nsteps - 1)
    def _():
        wait_rows(slot)


def _dispatch(dest_t, zflag, h, p_rows):
    n, d = h.shape
    tm = DISP_TM
    return pl.pallas_call(
        _dispatch_kernel,
        grid=(n // tm,),
        in_specs=[pl.BlockSpec((1, 1, 2 * tm), lambda i: (i, 0, 0), memory_space=pltpu.SMEM),
                  pl.BlockSpec(memory_space=pltpu.SMEM),
                  pl.BlockSpec(memory_space=pl.ANY)],
        out_specs=pl.BlockSpec(memory_space=pl.ANY),
        out_shape=jax.ShapeDtypeStruct((p_rows, d), f32),
        scratch_shapes=[pltpu.VMEM((MOE_TM, d), f32), pltpu.VMEM((3, tm, d), f32),
                        pltpu.SemaphoreType.DMA((3,)), pltpu.SemaphoreType.DMA((3,)), pltpu.SemaphoreType.DMA(())],
        compiler_params=_cparams(("arbitrary",)),
        name="dispatch",
    )(dest_t, zflag, h)


def _experts_kernel(be_ref, nu_ref, first_ref, next_ref, slot_ref, xs_ref, wg_ref, wu_ref, wd_ref, y_ref,
                    fg_ref, fu_ref, fd_ref, wgb_ref, wub_ref, wdb_ref, sems):
    i = pl.program_id(0)

    def weight_loads(e, s):
        return (pltpu.make_async_copy(wg_ref.at[e], fg_ref.at[s], sems.at[s, 0]),
                pltpu.make_async_copy(wu_ref.at[e], fu_ref.at[s], sems.at[s, 1]),
                pltpu.make_async_copy(wd_ref.at[e], fd_ref.at[s], sems.at[s, 2]))

    @pl.when(i == 0)
    def _():
        for c in weight_loads(be_ref[0], 0):
            c.start()

    @pl.when(first_ref[i] != 0)
    def _():
        s = slot_ref[i]
        for c in weight_loads(be_ref[i], s):
            c.wait()
        wgb_ref[...] = fg_ref[s].astype(bf16)
        wub_ref[...] = fu_ref[s].astype(bf16)
        wdb_ref[...] = fd_ref[s].astype(bf16)

        @pl.when(next_ref[i] >= 0)
        def _():
            for c in weight_loads(next_ref[i], 1 - s):
                c.start()

    @pl.when(i < nu_ref[0])
    def _():
        xb = xs_ref[...].astype(bf16)
        gte = _dot(xb, wgb_ref[...])
        up = _dot(xb, wub_ref[...])
        act = gte * jax.nn.sigmoid(gte) * up
        y_ref[...] = _dot(act.astype(bf16), wdb_ref[...])

    @pl.when(i >= nu_ref[0])
    def _():
        y_ref[...] = jnp.zeros_like(y_ref)


def _experts(block_e, n_used, first, next_e, slot, xs, w_gate, w_up, w_down):
    p_rows, d = xs.shape
    tm = MOE_TM
    nb = p_rows // tm
    de = w_gate.shape[-1]
    xmap = lambda i, be, nu, fi, ne, sl: (jnp.maximum(jnp.minimum(i, nu[0] - 1), 0), 0)
    hbm = pl.BlockSpec(memory_space=pl.ANY)
    return pl.pallas_call(
        _experts_kernel,
        grid_spec=pltpu.PrefetchScalarGridSpec(
            num_scalar_prefetch=5,
            grid=(nb,),
            in_specs=[pl.BlockSpec((tm, d), xmap), hbm, hbm, hbm],
            out_specs=pl.BlockSpec((tm, d), lambda i, be, nu, fi, ne, sl: (i, 0)),
            scratch_shapes=[pltpu.VMEM((2, d, de), f32), pltpu.VMEM((2, d, de), f32), pltpu.VMEM((2, de, d), f32),
                            pltpu.VMEM((d, de), bf16), pltpu.VMEM((d, de), bf16), pltpu.VMEM((de, d), bf16),
                            pltpu.SemaphoreType.DMA((2, 3))],
        ),
        out_shape=jax.ShapeDtypeStruct((p_rows, d), f32),
        compiler_params=_cparams(("arbitrary",)),
        name="experts",
    )(block_e, n_used, first, next_e, slot, xs, w_gate, w_up, w_down)


def _combine_kernel(alpha, dest_ref, ndest_ref, route_ref, h_ref, g_ref, b_ref, y_ref, o_ref, buf_ref, sems):
    tm = h_ref.shape[0]
    i = pl.program_id(0)
    slot = i % 2

    def start_row(ids_ref, s, r):
        for k in range(2):
            _row_copy(y_ref, _dest_row(ids_ref, tm, k, r), buf_ref.at[s, k], r, sems.at[s]).start()

    def wait_tile():
        for k in range(2):
            pltpu.make_async_copy(y_ref.at[pl.ds(0, tm), :], buf_ref.at[slot, k], sems.at[slot]).wait()

    def finish_tile():
        route = route_ref[...]
        ffn = route[:, 2:3] * buf_ref[slot, 0] + route[:, 3:4] * buf_ref[slot, 1]
        o_ref[...] = _layer_norm(alpha * h_ref[...] + ffn, g_ref[...], b_ref[...])

    @pl.when(i == 0)
    def _():
        def issue(r, _):
            start_row(dest_ref, 0, r)
            return 0
        lax.fori_loop(0, tm, issue, 0, unroll=4)

    @pl.when(i + 1 < pl.num_programs(0))
    def _():
        wait_tile()
        for r in range(tm):
            start_row(ndest_ref, 1 - slot, r)
        finish_tile()

    @pl.when(i + 1 == pl.num_programs(0))
    def _():
        wait_tile()
        finish_tile()


def _combine(dest_t, route, h, g, bta, y, alpha):
    n, d = h.shape
    tm = COMB_TM
    const = lambda i: (0, 0)
    return pl.pallas_call(
        functools.partial(_combine_kernel, alpha),
        grid=(n // tm,),
        in_specs=[pl.BlockSpec((1, 1, 2 * tm), lambda i: (i, 0, 0), memory_space=pltpu.SMEM),
                  pl.BlockSpec((1, 1, 2 * tm), lambda i: (jnp.minimum(i + 1, n // tm - 1), 0, 0),
                               memory_space=pltpu.SMEM),
                  pl.BlockSpec((tm, LANES), lambda i: (i, 0)),
                  pl.BlockSpec((tm, d), lambda i: (i, 0)),
                  pl.BlockSpec((1, d), const), pl.BlockSpec((1, d), const),
                  pl.BlockSpec(memory_space=pl.ANY)],
        out_specs=pl.BlockSpec((tm, d), lambda i: (i, 0)),
        out_shape=jax.ShapeDtypeStruct((n, d), f32),
        scratch_shapes=[pltpu.VMEM((2, 2, tm, d), f32), pltpu.SemaphoreType.DMA((2,))],
        compiler_params=_cparams(("arbitrary",)),
        name="combine",
    )(dest_t, dest_t, route, h, g, bta, y)


def _tile_dest(dest, tm):
    n = dest.shape[1]
    return dest.reshape(2, n // tm, tm).transpose(1, 0, 2).reshape(n // tm, 1, 2 * tm)


def _layer(x, positions, w_in, cmp_k_pe, cmp_k_w1, cmp_k_b1, cmp_k_w2, cmp_v_pe, cmp_v_w1, cmp_v_b1, cmp_v_w2,
           conv_w, conv_b, dt_bias, a_log, d_skip, ssm_norm_w, w_out, ln1_g, ln1_b,
           w_router_group, b_router_group, w_router_expert, b_router_expert, w_gate, w_up, w_down, ln2_g, ln2_b,
           alpha):
    b, t, d = x.shape
    n = b * t
    assert t % max(ATT_TK, ATT_TQ, CMP_TQ, CHUNK, PREP_TM) == 0 and n % max(PROJ_TM, OUT_TM, DISP_TM) == 0
    assert t // SEL_BLOCK <= LANES and (t // CMP_STRIDE) % LANES == 0 and t >= WINDOW + ATT_TQ
    x2 = x.reshape(n, d)

    c0 = NSA_WIDTH
    c1 = c0 + 6 * KV_WIDTH
    c2 = c1 + 3 * NSA_HEADS
    c3 = c2 + SSM_WIDTH
    c4 = c3 + XBC_WIDTH
    w_small = jnp.concatenate([w_in[:, c1:c2], w_in[:, c4:], jnp.zeros((d, LANES - 3 * NSA_HEADS - SSM_HEADS), f32)], axis=1)
    w_cat = jnp.concatenate([w_in[:, :c0], w_in[:, c2:c3], w_in[:, c3:c4], w_in[:, c0:c1], w_small,
                             jnp.zeros((d, PROJ_COLS - COL_SMALL - LANES), f32)], axis=1).astype(bf16)
    proj = _proj(x2, w_cat)

    lane = np.arange(LANES) % HEAD_DIM
    inv_freq = ROPE_THETA ** (-jnp.arange(0, ROT_DIM, 2, dtype=f32) / ROT_DIM)
    invf = jnp.where(lane < ROT_DIM, inv_freq[lane % (ROT_DIM // 2)], 0.0).astype(f32)[None, :]
    pos128 = jnp.broadcast_to(positions.reshape(n, 1), (n, LANES))
    q_r, cmp_in, k_sel, v_sel, k_win, v_win = _nsa_prep(proj, pos128, invf, b, t)

    nc = t // CMP_STRIDE
    half_w = CMP_STRIDE * HEAD_DIM
    a = cmp_in.reshape(2, b * NSA_KV_GROUPS, nc, half_w)
    pe = jnp.stack([cmp_k_pe, cmp_v_pe]).reshape(2, 2, 1, half_w)
    w1 = jnp.stack([cmp_k_w1, cmp_v_w1]).reshape(2, 2, half_w, CMP_HIDDEN).astype(bf16)
    b1 = jnp.stack([cmp_k_b1, cmp_v_b1]).reshape(2, 1, CMP_HIDDEN)
    w2 = jnp.pad(jnp.stack([cmp_k_w2, cmp_v_w2]), ((0, 0), (0, 0), (0, LANES - HEAD_DIM))).astype(bf16)
    cend = jnp.minimum(jnp.arange(nc) * CMP_STRIDE + CMP_BLOCK - 1, t - 1)
    posc = jnp.broadcast_to(positions[:, cend][:, :, None], (b, nc, LANES))
    kvc, kvc_t = _cmp_mlp(a, pe, w1, b1, w2, posc, invf, b)

    c_start = np.arange(nc)[:, None] * CMP_STRIDE
    s_start = np.arange(LANES)[None, :] * SEL_BLOCK
    cover = ((c_start < s_start + SEL_BLOCK) & (c_start + CMP_BLOCK > s_start)
             & (np.arange(nc)[:, None] < nc - 1) & (np.arange(LANES)[None, :] < t // SEL_BLOCK))
    cover = jnp.asarray(cover, bf16)
    o_cmp, selb = _cmp_attn(q_r, kvc, kvc_t, cover, b, t)

    y_nsa = _nsa_attn(q_r, k_sel, v_sel, k_win, v_win, selb, o_cmp, proj, b, t)

    dt_raw = proj[:, COL_SMALL + SMALL_DT_OFF:COL_SMALL + SMALL_DT_OFF + SSM_HEADS]
    dtt = dt_raw.reshape(b, t, SSM_HEADS).transpose(0, 2, 1)
    y_ssm = _ssd(proj, dtt, conv_w.reshape(CONV_WIDTH, XBC_WIDTH), conv_b.reshape(1, XBC_WIDTH),
                 dt_bias.reshape(1, SSM_HEADS), dt_bias.reshape(SSM_HEADS, 1),
                 a_log.reshape(1, SSM_HEADS), a_log.reshape(SSM_HEADS, 1),
                 jnp.repeat(d_skip, SSM_HEADDIM).reshape(1, SSM_WIDTH), ssm_norm_w.reshape(1, SSM_WIDTH), b, t)

    wr = jnp.concatenate([w_router_group, w_router_expert,
                          jnp.zeros((d, LANES - N_EXPERT_GROUPS - N_EXPERTS), f32)], axis=1)
    br = jnp.concatenate([b_router_group, b_router_expert,
                          jnp.zeros((LANES - N_EXPERT_GROUPS - N_EXPERTS,), f32)])[None, :]
    wr_hi = wr.astype(bf16)
    wr = jnp.concatenate([wr_hi, (wr - wr_hi.astype(f32)).astype(bf16)], axis=1)
    wo = w_out.astype(bf16)
    h, route, counts = _outproj(y_nsa, y_ssm, x2, wo[:NSA_WIDTH], wo[NSA_WIDTH:], ln1_g[None, :], ln1_b[None, :],
                                wr, br, alpha)

    cnt = counts[0, :N_EXPERTS].astype(i32)
    padded = (cnt + MOE_TM - 1) // MOE_TM * MOE_TM
    pad_ends = jnp.cumsum(padded)
    pad_starts = pad_ends - padded
    e01 = route[:, 0:2].astype(i32).T
    start01 = jnp.sum(jnp.where(e01[..., None] == jnp.arange(N_EXPERTS, dtype=i32), pad_starts, 0), axis=-1)
    dest = start01 + route[:, 4:6].astype(i32).T
    p_rows = 2 * n + N_EXPERTS * MOE_TM
    nb = p_rows // MOE_TM
    block_e = jnp.minimum(jnp.sum(jnp.arange(nb, dtype=i32)[:, None] * MOE_TM >= pad_ends[None, :], axis=-1),
                          N_EXPERTS - 1).astype(i32)
    n_used = (pad_ends[-1] // MOE_TM).astype(i32).reshape(1)
    blk = jnp.arange(nb, dtype=i32)
    last_of_expert = jnp.any((blk[:, None] + 1) * MOE_TM == pad_ends[None, :], axis=-1)
    zflag = (last_of_expert | (blk >= n_used[0])).astype(i32)
    block_e = jnp.where(blk < n_used[0], block_e, block_e[jnp.maximum(n_used[0] - 1, 0)])

    xs = _dispatch(_tile_dest(dest, DISP_TM), zflag, h, p_rows)
    prev_e = jnp.concatenate([jnp.full((1,), -1, i32), block_e[:-1]])
    first = ((block_e != prev_e) & (blk < n_used[0])).astype(i32)
    eidx = jnp.arange(N_EXPERTS, dtype=i32)
    later = jnp.where((eidx[None, :] > eidx[:, None]) & (padded[None, :] > 0), eidx[None, :], N_EXPERTS)
    next_of = jnp.min(later, axis=1)
    next_e = jnp.where(next_of[block_e] < N_EXPERTS, next_of[block_e], -1).astype(i32)
    slot = ((jnp.cumsum(first) - 1) % 2).astype(i32)
    y = _experts(block_e, n_used, first, next_e, slot, xs, w_gate, w_up, w_down)
    out = _combine(_tile_dest(dest, COMB_TM), route, h, ln2_g[None, :], ln2_b[None, :], y, alpha)
    return out.reshape(b, t, d)


def kernel(x, positions, w_in, cmp_k_pe, cmp_k_w1, cmp_k_b1, cmp_k_w2, cmp_v_pe, cmp_v_w1, cmp_v_b1, cmp_v_w2, conv_w, conv_b, dt_bias, a_log, d_skip, ssm_norm_w, w_out, ln1_g, ln1_b, w_router_group, b_router_group, w_router_expert, b_router_expert, w_gate, w_up, w_down, ln2_g, ln2_b):
    depth = w_in.shape[0]
    alpha = (2 * depth) ** 0.25
    params = (w_in, cmp_k_pe, cmp_k_w1, cmp_k_b1, cmp_k_w2, cmp_v_pe, cmp_v_w1, cmp_v_b1, cmp_v_w2, conv_w, conv_b,
              dt_bias, a_log, d_skip, ssm_norm_w, w_out, ln1_g, ln1_b, w_router_group, b_router_group,
              w_router_expert, b_router_expert, w_gate, w_up, w_down, ln2_g, ln2_b)
    for l in range(depth):
        x = _layer(x, positions, *[p[l] for p in params], alpha)
    return x
```

```python
import functools

import jax
import jax.numpy as jnp
import numpy as np
from jax import lax
from jax.experimental import pallas as pl
from jax.experimental.pallas import tpu as pltpu

f32 = jnp.float32
bf16 = jnp.bfloat16
i32 = jnp.int32

HEAD_DIM = 64
NSA_HEADS = 16
NSA_KV_GROUPS = 2
NSA_REP = NSA_HEADS // NSA_KV_GROUPS
NSA_WIDTH = NSA_HEADS * HEAD_DIM
KV_WIDTH = NSA_KV_GROUPS * HEAD_DIM
CMP_BLOCK = 32
CMP_STRIDE = 16
CMP_HIDDEN = 256
SEL_BLOCK = 64
SEL_TOPK = 16
WINDOW = 512
FORCED_SCORE = 1.0e4
SSM_HEADDIM = 64
SSM_HEADS = 16
SSM_WIDTH = SSM_HEADS * SSM_HEADDIM
SSM_GROUPS = 4
SSM_REP = SSM_HEADS // SSM_GROUPS
SSM_STATE = 128
CONV_WIDTH = 4
CHUNK = 256
XBC_WIDTH = SSM_WIDTH + 2 * SSM_GROUPS * SSM_STATE
ROPE_THETA = 500000.0
ROT_DIM = HEAD_DIM // 4
N_EXPERT_GROUPS = 4
EXPERTS_PER_GROUP = 8
N_EXPERTS = N_EXPERT_GROUPS * EXPERTS_PER_GROUP
NORM_EPS = 1e-5

LANES = 128
V7X_VMEM_BYTES = 64 * 1024 * 1024
MASK_NEG = -1.0e30

PROJ_TM = 1024
PROJ_TN = 1024
PREP_TM = 512
CMP_TQ = 256
ATT_TQ = 256
ATT_TK = 512
WIN_TK = 256
WIN_HEADS = 2
KX_WIDTH = 2 * LANES
OUT_TM = 512
MOE_TM = 256
DISP_TM = 256
COMB_TM = 128
VMEM_LIMIT = V7X_VMEM_BYTES - 8 * 1024 * 1024

COL_Q = 0
COL_Z = NSA_WIDTH
COL_XBC = COL_Z + SSM_WIDTH
COL_KV = COL_XBC + XBC_WIDTH
COL_SMALL = COL_KV + 6 * KV_WIDTH
PROJ_COLS = -(-(COL_SMALL + LANES) // PROJ_TN) * PROJ_TN
SMALL_DT_OFF = 3 * NSA_HEADS


def _cparams(sem, vmem=VMEM_LIMIT):
    return pltpu.CompilerParams(dimension_semantics=sem, vmem_limit_bytes=vmem)


def _dot(a, b):
    return jnp.dot(a, b, preferred_element_type=f32)


def _dot_t(a, b):
    return lax.dot_general(a, b, (((1,), (1,)), ((), ())), preferred_element_type=f32)


def _dot_hi(a, b):
    return jnp.dot(a, b, preferred_element_type=f32, precision=lax.Precision.HIGHEST)


def _proj_kernel(x_ref, w_ref, o_ref, xb_ref):
    @pl.when(pl.program_id(1) == 0)
    def _():
        xb_ref[...] = x_ref[...].astype(bf16)

    o_ref[...] = _dot(xb_ref[...], w_ref[...])


def _proj(x2, w_cat):
    n, d = x2.shape
    cols = w_cat.shape[1]
    tm = min(PROJ_TM, n)
    return pl.pallas_call(
        _proj_kernel,
        grid=(n // tm, cols // PROJ_TN),
        in_specs=[pl.BlockSpec((tm, d), lambda i, j: (i, 0)),
                  pl.BlockSpec((d, PROJ_TN), lambda i, j: (0, j))],
        out_specs=pl.BlockSpec((tm, PROJ_TN), lambda i, j: (i, j)),
        out_shape=jax.ShapeDtypeStruct((n, cols), f32),
        scratch_shapes=[pltpu.VMEM((tm, d), bf16)],
        compiler_params=_cparams(("parallel", "arbitrary")),
        name="proj",
    )(x2, w_cat)


def _rope_tables(pos_i32, invf):
    ang = pos_i32.astype(f32) * invf
    return jnp.cos(ang), jnp.sin(ang)


def _rope128(x, cos, sin):
    half = ROT_DIM // 2
    d = lax.broadcasted_iota(i32, x.shape, 1) % HEAD_DIM
    up = pltpu.roll(x, LANES - half, 1)
    dn = pltpu.roll(x, half, 1)
    rot = jnp.where(d < half, -up, dn)
    return x * cos + rot * sin


def _nsa_prep_kernel(pos_ref, invf_ref, q_ref, kc_ref, vc_ref, ks_ref, vs_ref, kw_ref, vw_ref,
                     qo_ref, cmp_ref, kso_ref, vso_ref, kwo_ref, vwo_ref):
    cos, sin = _rope_tables(pos_ref[...], invf_ref[...])
    scale = HEAD_DIM ** -0.5
    for c in range(NSA_WIDTH // LANES):
        t = _rope128(q_ref[:, c * LANES:(c + 1) * LANES], cos, sin) * scale
        qo_ref[0, 2 * c] = t[:, :HEAD_DIM].astype(bf16)
        qo_ref[0, 2 * c + 1] = t[:, HEAD_DIM:].astype(bf16)

    tm = pos_ref.shape[0]

    half_rows = tm // CMP_STRIDE
    for kind, src in enumerate((kc_ref, vc_ref)):
        for j in range(CMP_STRIDE):
            tok_j = src[pl.ds(j, half_rows, stride=CMP_STRIDE), :]
            for g in range(NSA_KV_GROUPS):
                cmp_ref[kind, 0, g, :, j * HEAD_DIM:(j + 1) * HEAD_DIM] = tok_j[:, g * HEAD_DIM:(g + 1) * HEAD_DIM]

    def values_with_ones(src, dst):
        t = src[...]
        tail = jnp.ones((tm, HEAD_DIM), bf16)
        for g in range(NSA_KV_GROUPS):
            vg = t[:, g * HEAD_DIM:(g + 1) * HEAD_DIM].astype(bf16)
            dst[0, g] = jnp.concatenate([vg, tail], axis=1)

    values_with_ones(vs_ref, vso_ref)
    values_with_ones(vw_ref, vwo_ref)

    kw_t = _rope128(kw_ref[...], cos, sin).T
    ks_t = _rope128(ks_ref[...], cos, sin).T
    for g in range(NSA_KV_GROUPS):
        for c in range(tm // WIN_TK):
            kwo_ref[0, g, c] = kw_t[g * HEAD_DIM:(g + 1) * HEAD_DIM, c * WIN_TK:(c + 1) * WIN_TK].astype(bf16)
    blk = lax.broadcasted_iota(i32, (LANES, tm), 0)
    tok = pl.program_id(1) * tm + lax.broadcasted_iota(i32, (LANES, tm), 1)
    onehot_t = jnp.where(tok // SEL_BLOCK == blk, 1.0, 0.0).astype(bf16)
    for g in range(NSA_KV_GROUPS):
        kso_ref[0, g, 0, 0:LANES, :] = onehot_t
        kg = ks_t[g * HEAD_DIM:(g + 1) * HEAD_DIM, :].astype(bf16)
        kso_ref[0, g, 0, LANES:2 * LANES, :] = jnp.concatenate([kg, jnp.zeros_like(kg)], axis=0)


def _nsa_prep(proj, pos128, invf, b, t):
    tm = PREP_TM
    assert tm == ATT_TK
    nt = t // tm
    row = lambda bi, ti: (bi * nt + ti, 0)
    kv0 = COL_KV // LANES
    in_specs = [pl.BlockSpec((tm, LANES), row),
                pl.BlockSpec((1, LANES), lambda bi, ti: (0, 0)),
                pl.BlockSpec((tm, NSA_WIDTH), lambda bi, ti: (bi * nt + ti, COL_Q // NSA_WIDTH))]
    for k in range(6):
        in_specs.append(pl.BlockSpec((tm, LANES), functools.partial(lambda bi, ti, k: (bi * nt + ti, kv0 + k), k=k)))
    head = lambda bi, ti: (bi, 0, ti, 0)
    tile5 = lambda bi, ti: (bi, 0, ti, 0, 0)
    g = NSA_KV_GROUPS
    out_specs = [pl.BlockSpec((1, NSA_HEADS, tm, HEAD_DIM), head),
                 pl.BlockSpec((2, 1, g, tm // CMP_STRIDE, CMP_STRIDE * HEAD_DIM), lambda bi, ti: (0, bi, 0, ti, 0)),
                 pl.BlockSpec((1, g, 1, KX_WIDTH, tm), tile5), pl.BlockSpec((1, g, tm, LANES), head),
                 pl.BlockSpec((1, g, tm // WIN_TK, HEAD_DIM, WIN_TK), tile5), pl.BlockSpec((1, g, tm, LANES), head)]
    out_shape = [jax.ShapeDtypeStruct((b, NSA_HEADS, t, HEAD_DIM), bf16),
                 jax.ShapeDtypeStruct((2, b, g, t // CMP_STRIDE, CMP_STRIDE * HEAD_DIM), f32),
                 jax.ShapeDtypeStruct((b, g, nt, KX_WIDTH, tm), bf16), jax.ShapeDtypeStruct((b, g, t, LANES), bf16),
                 jax.ShapeDtypeStruct((b, g, t // WIN_TK, HEAD_DIM, WIN_TK), bf16),
                 jax.ShapeDtypeStruct((b, g, t, LANES), bf16)]
    return pl.pallas_call(
        _nsa_prep_kernel,
        grid=(b, nt),
        in_specs=in_specs,
        out_specs=out_specs,
        out_shape=out_shape,
        compiler_params=_cparams(("parallel", "parallel")),
        name="nsa_prep",
    )(pos128, invf, proj, proj, proj, proj, proj, proj, proj)


def _cmp_mlp_kernel(a_ref, pe_ref, w1_ref, b1_ref, w2_ref, pos_ref, invf_ref, o_ref, ot_ref):
    kind = pl.program_id(0)
    a = a_ref[0, 0]
    nc = a.shape[0]
    u = _dot((a + pe_ref[0, 0]).astype(bf16), w1_ref[0, 0])
    v = _dot((a + pe_ref[0, 1]).astype(bf16), w1_ref[0, 1])
    v_next = pltpu.roll(v, nc - 1, 0)
    hid = jax.nn.gelu(u + v_next + b1_ref[0])
    out = _dot(hid.astype(bf16), w2_ref[0])
    cos, sin = _rope_tables(pos_ref[0], invf_ref[...])
    roped = _rope128(out, cos, sin)
    out = jnp.where(kind == 0, roped, out)
    o_ref[0, 0] = out[:, :HEAD_DIM].astype(bf16)
    ot_ref[0, 0] = out.T[:HEAD_DIM, :].astype(bf16)


def _cmp_mlp(a, pe, w1, b1, w2, posc, invf, b):
    _, bg, nc, hw = a.shape
    g = bg // b
    return pl.pallas_call(
        _cmp_mlp_kernel,
        grid=(2, bg),
        in_specs=[pl.BlockSpec((1, 1, nc, hw), lambda k, i: (k, i, 0, 0)),
                  pl.BlockSpec((1, 2, 1, hw), lambda k, i: (k, 0, 0, 0)),
                  pl.BlockSpec((1, 2, hw, CMP_HIDDEN), lambda k, i: (k, 0, 0, 0)),
                  pl.BlockSpec((1, 1, CMP_HIDDEN), lambda k, i: (k, 0, 0)),
                  pl.BlockSpec((1, CMP_HIDDEN, LANES), lambda k, i: (k, 0, 0)),
                  pl.BlockSpec((1, nc, LANES), lambda k, i: (i // g, 0, 0)),
                  pl.BlockSpec((1, LANES), lambda k, i: (0, 0))],
        out_specs=[pl.BlockSpec((1, 1, nc, HEAD_DIM), lambda k, i: (k, i, 0, 0)),
                   pl.BlockSpec((1, 1, HEAD_DIM, nc), lambda k, i: (k, i, 0, 0))],
        out_shape=[jax.ShapeDtypeStruct((2, bg, nc, HEAD_DIM), bf16),
                   jax.ShapeDtypeStruct((2, bg, HEAD_DIM, nc), bf16)],
        compiler_params=_cparams(("parallel", "parallel")),
        name="cmp_mlp",
    )(a, pe, w1, b1, w2, posc, invf)


def _cmp_attn_kernel(q_ref, kct_ref, vc_ref, cover_ref, oc_ref, sel_ref):
    qi = pl.program_id(2)
    nc = vc_ref.shape[2]
    rows = NSA_REP * CMP_TQ
    tq = qi * CMP_TQ + lax.broadcasted_iota(i32, (CMP_TQ, 1), 0)
    row_live = jnp.where(tq >= CMP_BLOCK - 1, 1.0, 0.0)
    tiny = jnp.finfo(f32).tiny

    def attend(ncols):
        kct = kct_ref[0, 0, :, :ncols]
        vc = vc_ref[0, 0, :ncols, :]
        cend = lax.broadcasted_iota(i32, (1, ncols), 1) * CMP_STRIDE + (CMP_BLOCK - 1)
        bias = jnp.where(cend <= tq, 0.0, MASK_NEG)
        s = _dot(q_ref[0].reshape(rows, HEAD_DIM), kct).reshape(NSA_REP, CMP_TQ, ncols) + bias[None]
        e = jnp.exp(s - jnp.max(s, axis=-1, keepdims=True))
        live = row_live[None]
        p = e * (live / jnp.maximum(live * jnp.sum(e, axis=-1, keepdims=True), tiny))
        oc_ref[0] = _dot(p.reshape(rows, ncols).astype(bf16), vc).reshape(NSA_REP, CMP_TQ, HEAD_DIM)
        psum = jnp.sum(p, axis=0)
        hi = psum.astype(bf16)
        lo = (psum - hi.astype(f32)).astype(bf16)
        cover = cover_ref[:ncols, :]
        imp = _dot(hi, cover) + _dot(lo, cover)

        j = lax.broadcasted_iota(i32, (CMP_TQ, LANES), 1)
        cur = tq // SEL_BLOCK
        forced = (j == 0) | (j == cur) | (j == cur - 1)
        valid = j * SEL_BLOCK <= tq
        imp = jnp.where(valid, jnp.where(forced, FORCED_SCORE, imp), -FORCED_SCORE)

        nblk = ncols * CMP_STRIDE // SEL_BLOCK
        jt = lax.broadcasted_iota(i32, (nblk, CMP_TQ), 0)

        def pick(_, carry):
            work, sel = carry
            m = jnp.max(work, axis=0, keepdims=True)
            first = jnp.min(jnp.where(work == m, jt, LANES), axis=0, keepdims=True)
            hit = jt == first
            return jnp.where(hit, -jnp.inf, work), jnp.where(hit, 1.0, sel)

        _, sel_t = lax.fori_loop(0, SEL_TOPK, pick, (imp.T[:nblk], jnp.zeros((nblk, CMP_TQ), f32)))
        if nblk < LANES:
            sel_t = jnp.concatenate([sel_t, jnp.zeros((LANES - nblk, CMP_TQ), f32)], axis=0)
        sel_ref[0, 0] = jnp.where(valid, jnp.where(sel_t.T > 0.0, 0.0, MASK_NEG), MASK_NEG).astype(bf16)

    need = (qi + 1) * (CMP_TQ // CMP_STRIDE)
    for ncols in range(LANES, nc + 1, LANES):
        @pl.when((need > ncols - LANES) & (need <= ncols))
        def _():
            attend(ncols)


def _cmp_attn(q_r, kvc, kvc_t, cover, b, t):
    g = NSA_KV_GROUPS
    nc = kvc.shape[2]
    nq = t // CMP_TQ
    return pl.pallas_call(
        _cmp_attn_kernel,
        grid=(b, g, nq),
        in_specs=[pl.BlockSpec((1, NSA_REP, CMP_TQ, HEAD_DIM), lambda bi, gi, qi: (bi, gi, qi, 0)),
                  pl.BlockSpec((1, 1, HEAD_DIM, nc), lambda bi, gi, qi: (0, bi * g + gi, 0, 0)),
                  pl.BlockSpec((1, 1, nc, HEAD_DIM), lambda bi, gi, qi: (1, bi * g + gi, 0, 0)),
                  pl.BlockSpec((nc, LANES), lambda bi, gi, qi: (0, 0))],
        out_specs=[pl.BlockSpec((1, NSA_REP, CMP_TQ, HEAD_DIM), lambda bi, gi, qi: (bi, gi, qi, 0)),
                   pl.BlockSpec((1, 1, CMP_TQ, LANES), lambda bi, gi, qi: (bi, gi, qi, 0))],
        out_shape=[jax.ShapeDtypeStruct((b, NSA_HEADS, t, HEAD_DIM), f32),
                   jax.ShapeDtypeStruct((b, g, t, LANES), bf16)],
        compiler_params=_cparams(("parallel", "parallel", "parallel")),
        name="cmp_attn",
    )(q_r, kvc_t, kvc, cover)


def _nsa_attn_kernel(q_ref, kx_ref, vs_ref, kw_ref, vw_ref, sel_ref, oc_ref, gate_ref,
                     o_ref, qx_ref, s_ref, m_ref, acc_ref, yw_ref):
    gi = pl.program_id(1)
    qi = pl.program_id(2)
    start = qi * ATT_TQ
    tq = start + lax.broadcasted_iota(i32, (ATT_TQ, 1), 0)

    selb = sel_ref[0, 0]
    for r in range(NSA_REP):
        qx_ref[r * ATT_TQ:(r + 1) * ATT_TQ, 0:LANES] = selb
        qr = q_ref[0, r]
        qx_ref[r * ATT_TQ:(r + 1) * ATT_TQ, LANES:2 * LANES] = jnp.concatenate([qr, jnp.zeros_like(qr)], axis=1)
    rows = NSA_REP * ATT_TQ

    m_ref[...] = jnp.full(m_ref.shape, MASK_NEG, f32)
    acc_ref[...] = jnp.zeros(acc_ref.shape, f32)

    def consume(kt, v_ref, bias, parts=1):
        k0 = pl.multiple_of(kt * ATT_TK, ATT_TK)
        v = v_ref[0, 0, pl.ds(k0, ATT_TK), :]
        heads = NSA_REP // parts
        for part in range(parts):
            rs = slice(part * heads * ATT_TQ, (part + 1) * heads * ATT_TQ)
            s = s_ref[rs]
            if bias is not None:
                s = (s.reshape(heads, ATT_TQ, ATT_TK) + bias[None]).reshape(heads * ATT_TQ, ATT_TK)
            m_prev = m_ref[rs]
            m_new = jnp.maximum(m_prev, jnp.max(s, axis=-1, keepdims=True))
            alpha = jnp.exp(m_prev - m_new)
            p = jnp.exp(s - jnp.concatenate([m_new] * (ATT_TK // LANES), axis=1))
            acc_ref[rs] = alpha * acc_ref[rs] + _dot(p.astype(bf16), v)
            m_ref[rs] = m_new

    def kpos(kt):
        return kt * ATT_TK + lax.broadcasted_iota(i32, (1, ATT_TK), 1)

    last = start // ATT_TK
    for part in range(2):
        rs = slice(part * rows // 2, (part + 1) * rows // 2)
        s_ref[rs] = _dot(qx_ref[rs], kx_ref[0, 0, 0])

    def sel_step(kt, _):
        consume(kt, vs_ref, None)
        s_ref[...] = _dot(qx_ref[...], kx_ref[0, 0, kt + 1])
        return 0

    n_win = (WINDOW + ATT_TQ) // WIN_TK
    w_first = jnp.maximum(start // WIN_TK - WINDOW // WIN_TK, 0)
    kw = jnp.concatenate([kw_ref[0, 0, w_first + i] for i in range(n_win)], axis=1)
    w0 = pl.multiple_of(w_first * WIN_TK, WIN_TK)
    vw = vw_ref[0, 0, pl.ds(w0, n_win * WIN_TK), :]
    kp = w0 + lax.broadcasted_iota(i32, (1, n_win * WIN_TK), 1)
    wbias = jnp.where(kp <= tq, jnp.where(kp > tq - WINDOW, 0.0, MASK_NEG), MASK_NEG)
    hh = WIN_HEADS
    gates = jax.nn.sigmoid(gate_ref[...])
    per_group = 3 * NSA_REP
    shifted = gates
    for gg in range(1, NSA_KV_GROUPS):
        shifted = jnp.where(gi == gg, pltpu.roll(gates, LANES - gg * per_group, 1), shifted)

    def gate(r, br):
        c = r * 3 + br
        return jnp.broadcast_to(shifted[:, c:c + 1], (ATT_TQ, HEAD_DIM))

    upper = pltpu.roll(shifted, HEAD_DIM, 1)[:, HEAD_DIM:]

    def gated_output(a, r, br):
        c = r * 3 + br
        ratio = upper / a[:, HEAD_DIM:]
        return a[:, :HEAD_DIM] * jnp.broadcast_to(ratio[:, c:c + 1], (ATT_TQ, HEAD_DIM))

    for half in range(NSA_REP // hh):
        qh = q_ref[0, half * hh:(half + 1) * hh].reshape(hh * ATT_TQ, HEAD_DIM)
        sw = _dot(qh, kw).reshape(hh, ATT_TQ, n_win * WIN_TK) + wbias[None]
        sw = sw.reshape(hh * ATT_TQ, n_win * WIN_TK)
        pw = jnp.exp(sw - jnp.max(sw, axis=-1, keepdims=True))
        aw = _dot(pw.astype(bf16), vw)
        for i in range(hh):
            r = half * hh + i
            a = aw[i * ATT_TQ:(i + 1) * ATT_TQ]
            yw_ref[r] = gated_output(a, r, 2)

    def sel_pair(j, _):
        sel_step(2 * j, 0)
        sel_step(2 * j + 1, 0)
        return 0

    lax.fori_loop(0, last // 2, sel_pair, 0)

    @pl.when(last % 2 == 1)
    def _():
        sel_step(last - 1, 0)

    consume(last, vs_ref, jnp.where(kpos(last) <= tq, 0.0, MASK_NEG), parts=2)

    for r in range(NSA_REP):
        osel = gated_output(acc_ref[r * ATT_TQ:(r + 1) * ATT_TQ], r, 1)
        o_ref[:, r * HEAD_DIM:(r + 1) * HEAD_DIM] = (gate(r, 0) * oc_ref[0, r] + osel + yw_ref[r]).astype(o_ref.dtype)


def _nsa_attn(q_r, kx, vs, kw, vw, selb, oc, proj, b, t):
    g = NSA_KV_GROUPS
    nq = t // ATT_TQ
    once = dict(pipeline_mode=pl.Buffered(1))
    vspec = pl.BlockSpec((1, 1, t, LANES), lambda bi, gi, qi: (bi, gi, 0, 0), **once)
    hspec = pl.BlockSpec((1, NSA_REP, ATT_TQ, HEAD_DIM), lambda bi, gi, qi: (bi, gi, qi, 0))
    return pl.pallas_call(
        _nsa_attn_kernel,
        grid=(b, g, nq),
        in_specs=[hspec,
                  pl.BlockSpec((1, 1, t // ATT_TK, KX_WIDTH, ATT_TK), lambda bi, gi, qi: (bi, gi, 0, 0, 0), **once),
                  vspec,
                  pl.BlockSpec((1, 1, t // WIN_TK, HEAD_DIM, WIN_TK), lambda bi, gi, qi: (bi, gi, 0, 0, 0), **once),
                  vspec,
                  pl.BlockSpec((1, 1, ATT_TQ, LANES), lambda bi, gi, qi: (bi, gi, qi, 0)),
                  hspec,
                  pl.BlockSpec((ATT_TQ, LANES), lambda bi, gi, qi: (bi * nq + qi, COL_SMALL // LANES))],
        out_specs=pl.BlockSpec((ATT_TQ, NSA_REP * HEAD_DIM), lambda bi, gi, qi: (bi * nq + qi, gi)),
        out_shape=jax.ShapeDtypeStruct((b * t, NSA_WIDTH), bf16),
        scratch_shapes=[pltpu.VMEM((NSA_REP * ATT_TQ, KX_WIDTH), bf16),
                        pltpu.VMEM((NSA_REP * ATT_TQ, ATT_TK), f32),
                        pltpu.VMEM((NSA_REP * ATT_TQ, LANES), f32),
                        pltpu.VMEM((NSA_REP * ATT_TQ, LANES), f32),
                        pltpu.VMEM((NSA_REP, ATT_TQ, HEAD_DIM), f32)],
        compiler_params=_cparams(("parallel", "parallel", "arbitrary")),
        name="nsa_attn",
    )(q_r, kx, vs, kw, vw, selb, oc, proj)


HALO = 8


def _ssd_kernel(xbc_ref, z_ref, small_ref, dtt_ref, cw_ref, cb_ref, dtb_r_ref, dtb_c_ref,
                alog_r_ref, alog_c_ref, dskip_ref, nw_ref, o_ref, ext_ref, st_ref):
    c = pl.program_id(1)
    L = CHUNK

    @pl.when(c == 0)
    def _():
        ext_ref[0:HALO, :] = jnp.zeros((HALO, XBC_WIDTH), f32)
        st_ref[...] = jnp.zeros_like(st_ref)

    ext_ref[HALO:HALO + L, :] = xbc_ref[...]
    conv = cb_ref[...]
    for k in range(CONV_WIDTH):
        off = HALO - (CONV_WIDTH - 1) + k
        conv = conv + cw_ref[k:k + 1, :] * ext_ref[off:off + L, :]
    ext_ref[0:HALO, :] = ext_ref[L:L + HALO, :]
    act = conv * jax.nn.sigmoid(conv)
    xs = act[:, :SSM_WIDTH]
    bm = act[:, SSM_WIDTH:SSM_WIDTH + SSM_GROUPS * SSM_STATE]
    cm = act[:, SSM_WIDTH + SSM_GROUPS * SSM_STATE:]

    dt_c = jax.nn.softplus(small_ref[:, SMALL_DT_OFF:SMALL_DT_OFF + SSM_HEADS] + dtb_r_ref[...])
    dt_r = jax.nn.softplus(dtt_ref[0] + dtb_c_ref[...])
    a_r = -jnp.exp(alog_r_ref[...])
    a_c = -jnp.exp(alog_c_ref[...])
    row = lax.broadcasted_iota(i32, (L, L), 0)
    col = lax.broadcasted_iota(i32, (L, L), 1)
    causal = col <= row
    tri = jnp.where(causal, 1.0, 0.0)
    acs_c = _dot_hi(tri, dt_c * a_r)
    acs_r = _dot_hi(dt_r * a_c, jnp.where(row <= col, 1.0, 0.0))

    z = z_ref[...]
    for g in range(SSM_GROUPS):
        cg = cm[:, g * SSM_STATE:(g + 1) * SSM_STATE].astype(bf16)
        bg = bm[:, g * SSM_STATE:(g + 1) * SSM_STATE]
        cb = _dot_t(cg, bg.astype(bf16))
        bg_t = bg.T
        ssq = jnp.zeros((L, 1), f32)
        yg = []
        for pp in range(SSM_REP // 2):
            pair = g * (SSM_REP // 2) + pp
            ps = slice(pair * LANES, (pair + 1) * LANES)
            x_p = xs[:, ps]
            low = lax.broadcasted_iota(i32, (L, LANES), 1) < SSM_HEADDIM
            low_n = lax.broadcasted_iota(i32, (SSM_STATE, LANES), 1) < SSM_HEADDIM
            x_own = (jnp.where(low, x_p, 0.0).astype(bf16), jnp.where(low, 0.0, x_p).astype(bf16))
            st = st_ref[pair]
            y = jnp.zeros((L, LANES), f32)
            st_in = jnp.zeros((SSM_STATE, LANES), f32)
            grow, keep = [], []
            for j in range(2):
                h = 2 * pair + j
                a_col = acs_c[:, h:h + 1]
                a_row = acs_r[h:h + 1, :]
                dt_row = dt_r[h:h + 1, :]
                a_last = acs_r[h:h + 1, L - 1:L]
                seg = a_col - a_row
                decay = jnp.where(causal, jnp.exp(jnp.where(causal, seg, 0.0)), 0.0)
                w = cb * decay * dt_row
                y = y + _dot(w.astype(bf16), x_own[j])
                bscaled = bg_t * (jnp.exp(a_last - a_row) * dt_row)
                st_in = st_in + _dot(bscaled.astype(bf16), x_own[j])
                grow.append(jnp.exp(a_col))
                keep.append(jnp.exp(a_last))
            y = y + _dot(cg, st.astype(bf16)) * jnp.where(low, grow[0], grow[1])
            st_ref[pair] = jnp.where(low_n, keep[0], keep[1]) * st + st_in
            y = y + dskip_ref[:, ps] * x_p
            zp = z[:, ps]
            y = y * (zp * jax.nn.sigmoid(zp))
            ssq = ssq + jnp.sum(y * y, axis=-1, keepdims=True)
            yg.append(y)
        rs = lax.rsqrt(ssq / (SSM_REP * SSM_HEADDIM) + NORM_EPS)
        for pp in range(SSM_REP // 2):
            pair = g * (SSM_REP // 2) + pp
            ps = slice(pair * LANES, (pair + 1) * LANES)
            o_ref[:, ps] = (yg[pp] * rs * nw_ref[:, ps]).astype(o_ref.dtype)


def _ssd(proj, dtt, cw, cb, dtb_r, dtb_c, alog_r, alog_c, dskip, nw, b, t):
    nch = t // CHUNK
    row = lambda bi, ci: bi * nch + ci
    const2 = lambda bi, ci: (0, 0)
    return pl.pallas_call(
        _ssd_kernel,
        grid=(b, nch),
        in_specs=[pl.BlockSpec((CHUNK, XBC_WIDTH), lambda bi, ci: (row(bi, ci), COL_XBC // XBC_WIDTH)),
                  pl.BlockSpec((CHUNK, SSM_WIDTH), lambda bi, ci: (row(bi, ci), COL_Z // SSM_WIDTH)),
                  pl.BlockSpec((CHUNK, LANES), lambda bi, ci: (row(bi, ci), COL_SMALL // LANES)),
                  pl.BlockSpec((1, SSM_HEADS, CHUNK), lambda bi, ci: (bi, 0, ci)),
                  pl.BlockSpec((CONV_WIDTH, XBC_WIDTH), const2),
                  pl.BlockSpec((1, XBC_WIDTH), const2),
                  pl.BlockSpec((1, SSM_HEADS), const2),
                  pl.BlockSpec((SSM_HEADS, 1), const2),
                  pl.BlockSpec((1, SSM_HEADS), const2),
                  pl.BlockSpec((SSM_HEADS, 1), const2),
                  pl.BlockSpec((1, SSM_WIDTH), const2),
                  pl.BlockSpec((1, SSM_WIDTH), const2)],
        out_specs=pl.BlockSpec((CHUNK, SSM_WIDTH), lambda bi, ci: (row(bi, ci), 0)),
        out_shape=jax.ShapeDtypeStruct((b * t, SSM_WIDTH), bf16),
        scratch_shapes=[pltpu.VMEM((HALO + CHUNK, XBC_WIDTH), f32),
                        pltpu.VMEM((SSM_HEADS // 2, SSM_STATE, 2 * SSM_HEADDIM), f32)],
        compiler_params=_cparams(("parallel", "arbitrary")),
        name="ssd",
    )(proj, proj, proj, dtt, cw, cb, dtb_r, dtb_c, alog_r, alog_c, dskip, nw)


def _layer_norm(v, g, b):
    mu = jnp.mean(v, axis=-1, keepdims=True)
    d = v - mu
    var = jnp.mean(d * d, axis=-1, keepdims=True)
    return d * lax.rsqrt(var + NORM_EPS) * g + b


def _outproj_kernel(alpha, ya_ref, yb_ref, x_ref, wa_ref, wb_ref, g_ref, b_ref, wr_ref, br_ref,
                    h_ref, route_ref, cnt_ref):
    i = pl.program_id(0)
    tm = x_ref.shape[0]
    mix = _dot(ya_ref[...], wa_ref[...]) + _dot(yb_ref[...], wb_ref[...])
    h = _layer_norm(alpha * x_ref[...] + mix, g_ref[...], b_ref[...])
    h_ref[...] = h

    h_hi = h.astype(bf16)
    h_lo = (h - h_hi.astype(f32)).astype(bf16)
    t = _dot(h_hi, wr_ref[...])
    logits = t[:, :LANES] + t[:, LANES:] + _dot(h_lo, wr_ref[:, :LANES]) + br_ref[...]
    lane = lax.broadcasted_iota(i32, (tm, LANES), 1)
    ninf = -jnp.inf
    gmask = lane < N_EXPERT_GROUPS
    gl = jnp.where(gmask, logits, ninf)
    ge = jnp.where(gmask, jnp.exp(gl - jnp.max(gl, axis=-1, keepdims=True)), 0.0)
    pg = ge / jnp.sum(ge, axis=-1, keepdims=True)
    g_gate = jnp.max(pg, axis=-1, keepdims=True)
    g_sel = jnp.min(jnp.where(gmask & (pg == g_gate), lane, LANES), axis=-1, keepdims=True)
    lo = N_EXPERT_GROUPS + g_sel * EXPERTS_PER_GROUP
    emask = (lane >= lo) & (lane < lo + EXPERTS_PER_GROUP)
    el = jnp.where(emask, logits, ninf)
    ee = jnp.where(emask, jnp.exp(el - jnp.max(el, axis=-1, keepdims=True)), 0.0)
    pe = ee / jnp.sum(ee, axis=-1, keepdims=True)
    p0 = jnp.max(pe, axis=-1, keepdims=True)
    l0 = jnp.min(jnp.where(emask & (pe == p0), lane, LANES), axis=-1, keepdims=True)
    rest = jnp.where(emask & (lane != l0), pe, ninf)
    p1 = jnp.max(rest, axis=-1, keepdims=True)
    l1 = jnp.min(jnp.where(rest == p1, lane, LANES), axis=-1, keepdims=True)
    psum = p0 + p1
    w0 = g_gate * p0 / psum
    w1 = g_gate * p1 / psum
    e0 = l0 - N_EXPERT_GROUPS
    e1 = l1 - N_EXPERT_GROUPS

    @pl.when(i == 0)
    def _():
        cnt_ref[...] = jnp.zeros_like(cnt_ref)

    oh0 = lane == e0
    oh1 = lane == e1
    both = jnp.where(oh0, 1.0, 0.0) + jnp.where(oh1, 1.0, 0.0)
    r_i = lax.broadcasted_iota(i32, (tm, tm), 0)
    c_i = lax.broadcasted_iota(i32, (tm, tm), 1)
    strict = jnp.where(c_i < r_i, 1.0, 0.0).astype(bf16)
    before = _dot(strict, both.astype(bf16)) + cnt_ref[...]
    rank0 = jnp.sum(jnp.where(oh0, before, 0.0), axis=-1, keepdims=True)
    rank1 = jnp.sum(jnp.where(oh1, before, 0.0), axis=-1, keepdims=True)
    cnt_ref[...] = cnt_ref[...] + jnp.sum(both, axis=0, keepdims=True)

    out = jnp.where(lane == 0, e0.astype(f32), 0.0)
    out = jnp.where(lane == 1, e1.astype(f32), out)
    out = jnp.where(lane == 2, w0, out)
    out = jnp.where(lane == 3, w1, out)
    out = jnp.where(lane == 4, rank0, out)
    out = jnp.where(lane == 5, rank1, out)
    route_ref[...] = out


def _outproj(y_nsa, y_ssm, x2, wa, wb, g, bta, wr, br, alpha):
    n, d = x2.shape
    tm = OUT_TM
    const = lambda i: (0, 0)
    rowb = lambda i: (i, 0)
    return pl.pallas_call(
        functools.partial(_outproj_kernel, alpha),
        grid=(n // tm,),
        in_specs=[pl.BlockSpec((tm, NSA_WIDTH), rowb), pl.BlockSpec((tm, SSM_WIDTH), rowb),
                  pl.BlockSpec((tm, d), rowb),
                  pl.BlockSpec((NSA_WIDTH, d), const), pl.BlockSpec((SSM_WIDTH, d), const),
                  pl.BlockSpec((1, d), const), pl.BlockSpec((1, d), const),
                  pl.BlockSpec((d, 2 * LANES), const), pl.BlockSpec((1, LANES), const)],
        out_specs=[pl.BlockSpec((tm, d), rowb), pl.BlockSpec((tm, LANES), rowb),
                   pl.BlockSpec((1, LANES), const)],
        out_shape=[jax.ShapeDtypeStruct((n, d), f32), jax.ShapeDtypeStruct((n, LANES), f32),
                   jax.ShapeDtypeStruct((1, LANES), f32)],
        compiler_params=_cparams(("arbitrary",)),
        name="outproj",
    )(y_nsa, y_ssm, x2, wa, wb, g, bta, wr, br)


def _row_copy(src_ref, src_row, dst_ref, dst_row, sem):
    return pltpu.make_async_copy(src_ref.at[pl.ds(src_row, 1), :], dst_ref.at[pl.ds(dst_row, 1), :], sem)


def _dest_row(dest_ref, tm, k, r):
    return dest_ref[0, 0, k * tm + r]


def _dispatch_kernel(dest_ref, zflag_ref, h_ref, xs_ref, zero_ref, tile_ref, pk_ref, sem, lsem, zsem):
    tm = DISP_TM
    nb = zflag_ref.shape[0]
    i = pl.program_id(0)

    @pl.when(i == 0)
    def _():
        zero_ref[...] = jnp.zeros_like(zero_ref)

        def zblock(i):
            rows = pl.ds(pl.multiple_of(i * MOE_TM, MOE_TM), MOE_TM)
            return pltpu.make_async_copy(zero_ref, xs_ref.at[rows, :], zsem)

        def zstart(i, _):
            @pl.when(zflag_ref[i] != 0)
            def _():
                zblock(i).start()
            return 0

        def zwait(i, _):
            @pl.when(zflag_ref[i] != 0)
            def _():
                zblock(i).wait()
            return 0

        lax.fori_loop(0, nb, zstart, 0)
        lax.fori_loop(0, nb, zwait, 0)

    nslot = tile_ref.shape[0]
    nsteps = pl.num_programs(0)
    slot = i % nslot

    def tile_load(step, s):
        rows = pl.ds(pl.multiple_of(step * tm, tm), tm)
        return pltpu.make_async_copy(h_ref.at[rows, :], tile_ref.at[s], lsem.at[s])

    @pl.when(i == 0)
    def _():
        tile_load(0, 0).start()

    tile_load(i, slot).wait()

    @pl.when(i + 1 < nsteps)
    def _():
        tile_load(i + 1, (i + 1) % nslot).start()

    t = tile_ref[slot]
    half = t.shape[1] // 2
    hi = pltpu.bitcast(t[:, :half].astype(bf16).astype(f32), jnp.uint32)
    lo = pltpu.bitcast(t[:, half:].astype(bf16).astype(f32), jnp.uint32)
    pk_ref[slot] = hi | (lo >> 16)

    def issue(r, _):
        for k in range(2):
            _row_copy(pk_ref.at[slot], r, xs_ref, _dest_row(dest_ref, tm, k, r), sem.at[slot]).start()
        return 0

    lax.fori_loop(0, tm, issue, 0, unroll=16)

    def wait_rows(s):
        for k in range(2):
            pltpu.make_async_copy(pk_ref.at[s], xs_ref.at[pl.ds(0, tm), :], sem.at[s]).wait()

    @pl.when(i > 0)
    def _():
        wait_rows((i - 1) % nslot)

    @pl.when(i == nsteps - 1)
    def _():
        wait_rows(slot)


def _dispatch(dest_t, zflag, h, p_rows):
    n, d = h.shape
    tm = DISP_TM
    return pl.pallas_call(
        _dispatch_kernel,
        grid=(n // tm,),
        in_specs=[pl.BlockSpec((1, 1, 2 * tm), lambda i: (i, 0, 0), memory_space=pltpu.SMEM),
                  pl.BlockSpec(memory_space=pltpu.SMEM),
                  pl.BlockSpec(memory_space=pl.ANY)],
        out_specs=pl.BlockSpec(memory_space=pl.ANY),
        out_shape=jax.ShapeDtypeStruct((p_rows, d // 2), jnp.uint32),
        scratch_shapes=[pltpu.VMEM((MOE_TM, d // 2), jnp.uint32), pltpu.VMEM((3, tm, d), f32),
                        pltpu.VMEM((3, tm, d // 2), jnp.uint32),
                        pltpu.SemaphoreType.DMA((3,)), pltpu.SemaphoreType.DMA((3,)), pltpu.SemaphoreType.DMA(())],
        compiler_params=_cparams(("arbitrary",)),
        name="dispatch",
    )(dest_t, zflag, h)


def _experts_kernel(be_ref, nu_ref, first_ref, next_ref, slot_ref, xs_ref, wg_ref, wu_ref, wd_ref, y_ref,
                    fg_ref, fu_ref, fd_ref, wgb_ref, wub_ref, wdb_ref, sems):
    i = pl.program_id(0)

    def weight_loads(e, s):
        return (pltpu.make_async_copy(wg_ref.at[e], fg_ref.at[s], sems.at[s, 0]),
                pltpu.make_async_copy(wu_ref.at[e], fu_ref.at[s], sems.at[s, 1]),
                pltpu.make_async_copy(wd_ref.at[e], fd_ref.at[s], sems.at[s, 2]))

    @pl.when(i == 0)
    def _():
        for c in weight_loads(be_ref[0], 0):
            c.start()

    @pl.when(first_ref[i] != 0)
    def _():
        s = slot_ref[i]
        for c in weight_loads(be_ref[i], s):
            c.wait()
        wgb_ref[...] = fg_ref[s].astype(bf16)
        wub_ref[...] = fu_ref[s].astype(bf16)
        wdb_ref[...] = fd_ref[s].astype(bf16)

        @pl.when(next_ref[i] >= 0)
        def _():
            for c in weight_loads(next_ref[i], 1 - s):
                c.start()

    @pl.when(i < nu_ref[0])
    def _():
        packed = xs_ref[...]
        xb = jnp.concatenate([pltpu.bitcast(packed & jnp.uint32(0xFFFF0000), f32).astype(bf16),
                              pltpu.bitcast(packed << 16, f32).astype(bf16)], axis=1)
        gte = _dot(xb, wgb_ref[...])
        up = _dot(xb, wub_ref[...])
        act = gte * jax.nn.sigmoid(gte) * up
        y_ref[...] = _dot(act.astype(bf16), wdb_ref[...])

    @pl.when(i >= nu_ref[0])
    def _():
        y_ref[...] = jnp.zeros_like(y_ref)


def _experts(block_e, n_used, first, next_e, slot, xs, w_gate, w_up, w_down):
    p_rows = xs.shape[0]
    d = w_gate.shape[1]
    tm = MOE_TM
    nb = p_rows // tm
    de = w_gate.shape[-1]
    xmap = lambda i, be, nu, fi, ne, sl: (jnp.maximum(jnp.minimum(i, nu[0] - 1), 0), 0)
    hbm = pl.BlockSpec(memory_space=pl.ANY)
    return pl.pallas_call(
        _experts_kernel,
        grid_spec=pltpu.PrefetchScalarGridSpec(
            num_scalar_prefetch=5,
            grid=(nb,),
            in_specs=[pl.BlockSpec((tm, d // 2), xmap), hbm, hbm, hbm],
            out_specs=pl.BlockSpec((tm, d), lambda i, be, nu, fi, ne, sl: (i, 0)),
            scratch_shapes=[pltpu.VMEM((2, d, de), f32), pltpu.VMEM((2, d, de), f32), pltpu.VMEM((2, de, d), f32),
                            pltpu.VMEM((d, de), bf16), pltpu.VMEM((d, de), bf16), pltpu.VMEM((de, d), bf16),
                            pltpu.SemaphoreType.DMA((2, 3))],
        ),
        out_shape=jax.ShapeDtypeStruct((p_rows, d), f32),
        compiler_params=_cparams(("arbitrary",)),
        name="experts",
    )(block_e, n_used, first, next_e, slot, xs, w_gate, w_up, w_down)


def _combine_kernel(alpha, dest_ref, ndest_ref, route_ref, h_ref, g_ref, b_ref, y_ref, o_ref, buf_ref, sems):
    tm = h_ref.shape[0]
    i = pl.program_id(0)
    slot = i % 2

    def start_row(ids_ref, s, r):
        for k in range(2):
            _row_copy(y_ref, _dest_row(ids_ref, tm, k, r), buf_ref.at[s, k], r, sems.at[s]).start()

    def wait_tile():
        for k in range(2):
            pltpu.make_async_copy(y_ref.at[pl.ds(0, tm), :], buf_ref.at[slot, k], sems.at[slot]).wait()

    def finish_tile():
        route = route_ref[...]
        ffn = route[:, 2:3] * buf_ref[slot, 0] + route[:, 3:4] * buf_ref[slot, 1]
        o_ref[...] = _layer_norm(alpha * h_ref[...] + ffn, g_ref[...], b_ref[...])

    @pl.when(i == 0)
    def _():
        def issue(r, _):
            start_row(dest_ref, 0, r)
            return 0
        lax.fori_loop(0, tm, issue, 0, unroll=4)

    @pl.when(i + 1 < pl.num_programs(0))
    def _():
        wait_tile()
        for r in range(tm):
            start_row(ndest_ref, 1 - slot, r)
        finish_tile()

    @pl.when(i + 1 == pl.num_programs(0))
    def _():
        wait_tile()
        finish_tile()


def _combine(dest_t, route, h, g, bta, y, alpha):
    n, d = h.shape
    tm = COMB_TM
    const = lambda i: (0, 0)
    return pl.pallas_call(
        functools.partial(_combine_kernel, alpha),
        grid=(n // tm,),
        in_specs=[pl.BlockSpec((1, 1, 2 * tm), lambda i: (i, 0, 0), memory_space=pltpu.SMEM),
                  pl.BlockSpec((1, 1, 2 * tm), lambda i: (jnp.minimum(i + 1, n // tm - 1), 0, 0),
                               memory_space=pltpu.SMEM),
                  pl.BlockSpec((tm, LANES), lambda i: (i, 0)),
                  pl.BlockSpec((tm, d), lambda i: (i, 0)),
                  pl.BlockSpec((1, d), const), pl.BlockSpec((1, d), const),
                  pl.BlockSpec(memory_space=pl.ANY)],
        out_specs=pl.BlockSpec((tm, d), lambda i: (i, 0)),
        out_shape=jax.ShapeDtypeStruct((n, d), f32),
        scratch_shapes=[pltpu.VMEM((2, 2, tm, d), f32), pltpu.SemaphoreType.DMA((2,))],
        compiler_params=_cparams(("arbitrary",)),
        name="combine",
    )(dest_t, dest_t, route, h, g, bta, y)


def _tile_dest(dest, tm):
    n = dest.shape[1]
    return dest.reshape(2, n // tm, tm).transpose(1, 0, 2).reshape(n // tm, 1, 2 * tm)


def _layer(x, positions, w_in, cmp_k_pe, cmp_k_w1, cmp_k_b1, cmp_k_w2, cmp_v_pe, cmp_v_w1, cmp_v_b1, cmp_v_w2,
           conv_w, conv_b, dt_bias, a_log, d_skip, ssm_norm_w, w_out, ln1_g, ln1_b,
           w_router_group, b_router_group, w_router_expert, b_router_expert, w_gate, w_up, w_down, ln2_g, ln2_b,
           alpha):
    b, t, d = x.shape
    n = b * t
    assert t % max(ATT_TK, ATT_TQ, CMP_TQ, CHUNK, PREP_TM) == 0 and n % max(PROJ_TM, OUT_TM, DISP_TM) == 0
    assert t // SEL_BLOCK <= LANES and (t // CMP_STRIDE) % LANES == 0 and t >= WINDOW + ATT_TQ
    x2 = x.reshape(n, d)

    c0 = NSA_WIDTH
    c1 = c0 + 6 * KV_WIDTH
    c2 = c1 + 3 * NSA_HEADS
    c3 = c2 + SSM_WIDTH
    c4 = c3 + XBC_WIDTH
    w_small = jnp.concatenate([w_in[:, c1:c2], w_in[:, c4:], jnp.zeros((d, LANES - 3 * NSA_HEADS - SSM_HEADS), f32)], axis=1)
    w_cat = jnp.concatenate([w_in[:, :c0], w_in[:, c2:c3], w_in[:, c3:c4], w_in[:, c0:c1], w_small,
                             jnp.zeros((d, PROJ_COLS - COL_SMALL - LANES), f32)], axis=1).astype(bf16)
    proj = _proj(x2, w_cat)

    lane = np.arange(LANES) % HEAD_DIM
    inv_freq = ROPE_THETA ** (-jnp.arange(0, ROT_DIM, 2, dtype=f32) / ROT_DIM)
    invf = jnp.where(lane < ROT_DIM, inv_freq[lane % (ROT_DIM // 2)], 0.0).astype(f32)[None, :]
    pos128 = jnp.broadcast_to(positions.reshape(n, 1), (n, LANES))
    q_r, cmp_in, k_sel, v_sel, k_win, v_win = _nsa_prep(proj, pos128, invf, b, t)

    nc = t // CMP_STRIDE
    half_w = CMP_STRIDE * HEAD_DIM
    a = cmp_in.reshape(2, b * NSA_KV_GROUPS, nc, half_w)
    pe = jnp.stack([cmp_k_pe, cmp_v_pe]).reshape(2, 2, 1, half_w)
    w1 = jnp.stack([cmp_k_w1, cmp_v_w1]).reshape(2, 2, half_w, CMP_HIDDEN).astype(bf16)
    b1 = jnp.stack([cmp_k_b1, cmp_v_b1]).reshape(2, 1, CMP_HIDDEN)
    w2 = jnp.pad(jnp.stack([cmp_k_w2, cmp_v_w2]), ((0, 0), (0, 0), (0, LANES - HEAD_DIM))).astype(bf16)
    cend = jnp.minimum(jnp.arange(nc) * CMP_STRIDE + CMP_BLOCK - 1, t - 1)
    posc = jnp.broadcast_to(positions[:, cend][:, :, None], (b, nc, LANES))
    kvc, kvc_t = _cmp_mlp(a, pe, w1, b1, w2, posc, invf, b)

    c_start = np.arange(nc)[:, None] * CMP_STRIDE
    s_start = np.arange(LANES)[None, :] * SEL_BLOCK
    cover = ((c_start < s_start + SEL_BLOCK) & (c_start + CMP_BLOCK > s_start)
             & (np.arange(nc)[:, None] < nc - 1) & (np.arange(LANES)[None, :] < t // SEL_BLOCK))
    cover = jnp.asarray(cover, bf16)
    o_cmp, selb = _cmp_attn(q_r, kvc, kvc_t, cover, b, t)

    y_nsa = _nsa_attn(q_r, k_sel, v_sel, k_win, v_win, selb, o_cmp, proj, b, t)

    dt_raw = proj[:, COL_SMALL + SMALL_DT_OFF:COL_SMALL + SMALL_DT_OFF + SSM_HEADS]
    dtt = dt_raw.reshape(b, t, SSM_HEADS).transpose(0, 2, 1)
    y_ssm = _ssd(proj, dtt, conv_w.reshape(CONV_WIDTH, XBC_WIDTH), conv_b.reshape(1, XBC_WIDTH),
                 dt_bias.reshape(1, SSM_HEADS), dt_bias.reshape(SSM_HEADS, 1),
                 a_log.reshape(1, SSM_HEADS), a_log.reshape(SSM_HEADS, 1),
                 jnp.repeat(d_skip, SSM_HEADDIM).reshape(1, SSM_WIDTH), ssm_norm_w.reshape(1, SSM_WIDTH), b, t)

    wr = jnp.concatenate([w_router_group, w_router_expert,
                          jnp.zeros((d, LANES - N_EXPERT_GROUPS - N_EXPERTS), f32)], axis=1)
    br = jnp.concatenate([b_router_group, b_router_expert,
                          jnp.zeros((LANES - N_EXPERT_GROUPS - N_EXPERTS,), f32)])[None, :]
    wr_hi = wr.astype(bf16)
    wr = jnp.concatenate([wr_hi, (wr - wr_hi.astype(f32)).astype(bf16)], axis=1)
    wo = w_out.astype(bf16)
    h, route, counts = _outproj(y_nsa, y_ssm, x2, wo[:NSA_WIDTH], wo[NSA_WIDTH:], ln1_g[None, :], ln1_b[None, :],
                                wr, br, alpha)

    cnt = counts[0, :N_EXPERTS].astype(i32)
    padded = (cnt + MOE_TM - 1) // MOE_TM * MOE_TM
    pad_ends = jnp.cumsum(padded)
    pad_starts = pad_ends - padded
    e01 = route[:, 0:2].astype(i32).T
    start01 = jnp.sum(jnp.where(e01[..., None] == jnp.arange(N_EXPERTS, dtype=i32), pad_starts, 0), axis=-1)
    dest = start01 + route[:, 4:6].astype(i32).T
    p_rows = 2 * n + N_EXPERTS * MOE_TM
    nb = p_rows // MOE_TM
    block_e = jnp.minimum(jnp.sum(jnp.arange(nb, dtype=i32)[:, None] * MOE_TM >= pad_ends[None, :], axis=-1),
                          N_EXPERTS - 1).astype(i32)
    n_used = (pad_ends[-1] // MOE_TM).astype(i32).reshape(1)
    blk = jnp.arange(nb, dtype=i32)
    last_of_expert = jnp.any((blk[:, None] + 1) * MOE_TM == pad_ends[None, :], axis=-1)
    zflag = (last_of_expert | (blk >= n_used[0])).astype(i32)
    block_e = jnp.where(blk < n_used[0], block_e, block_e[jnp.maximum(n_used[0] - 1, 0)])

    xs = _dispatch(_tile_dest(dest, DISP_TM), zflag, h, p_rows)
    prev_e = jnp.concatenate([jnp.full((1,), -1, i32), block_e[:-1]])
    first = ((block_e != prev_e) & (blk < n_used[0])).astype(i32)
    eidx = jnp.arange(N_EXPERTS, dtype=i32)
    later = jnp.where((eidx[None, :] > eidx[:, None]) & (padded[None, :] > 0), eidx[None, :], N_EXPERTS)
    next_of = jnp.min(later, axis=1)
    next_e = jnp.where(next_of[block_e] < N_EXPERTS, next_of[block_e], -1).astype(i32)
    slot = ((jnp.cumsum(first) - 1) % 2).astype(i32)
    y = _experts(block_e, n_used, first, next_e, slot, xs, w_gate, w_up, w_down)
    out = _combine(_tile_dest(dest, COMB_TM), route, h, ln2_g[None, :], ln2_b[None, :], y, alpha)
    return out.reshape(b, t, d)


def kernel(x, positions, w_in, cmp_k_pe, cmp_k_w1, cmp_k_b1, cmp_k_w2, cmp_v_pe, cmp_v_w1, cmp_v_b1, cmp_v_w2, conv_w, conv_b, dt_bias, a_log, d_skip, ssm_norm_w, w_out, ln1_g, ln1_b, w_router_group, b_router_group, w_router_expert, b_router_expert, w_gate, w_up, w_down, ln2_g, ln2_b):
    depth = w_in.shape[0]
    alpha = (2 * depth) ** 0.25
    params = (w_in, cmp_k_pe, cmp_k_w1, cmp_k_b1, cmp_k_w2, cmp_v_pe, cmp_v_w1, cmp_v_b1, cmp_v_w2, conv_w, conv_b,
              dt_bias, a_log, d_skip, ssm_norm_w, w_out, ln1_g, ln1_b, w_router_group, b_router_group,
              w_router_expert, b_router_expert, w_gate, w_up, w_down, ln2_g, ln2_b)
    for l in range(depth):
        x = _layer(x, positions, *[p[l] for p in params], alpha)
    return x
```

```python
import functools

import jax
import jax.numpy as jnp
import numpy as np
from jax import lax
from jax.experimental import pallas as pl
from jax.experimental.pallas import tpu as pltpu

f32 = jnp.float32
bf16 = jnp.bfloat16
i32 = jnp.int32

HEAD_DIM = 64
NSA_HEADS = 16
NSA_KV_GROUPS = 2
NSA_REP = NSA_HEADS // NSA_KV_GROUPS
NSA_WIDTH = NSA_HEADS * HEAD_DIM
KV_WIDTH = NSA_KV_GROUPS * HEAD_DIM
CMP_BLOCK = 32
CMP_STRIDE = 16
CMP_HIDDEN = 256
SEL_BLOCK = 64
SEL_TOPK = 16
WINDOW = 512
FORCED_SCORE = 1.0e4
SSM_HEADDIM = 64
SSM_HEADS = 16
SSM_WIDTH = SSM_HEADS * SSM_HEADDIM
SSM_GROUPS = 4
SSM_REP = SSM_HEADS // SSM_GROUPS
SSM_STATE = 128
CONV_WIDTH = 4
CHUNK = 256
XBC_WIDTH = SSM_WIDTH + 2 * SSM_GROUPS * SSM_STATE
ROPE_THETA = 500000.0
ROT_DIM = HEAD_DIM // 4
N_EXPERT_GROUPS = 4
EXPERTS_PER_GROUP = 8
N_EXPERTS = N_EXPERT_GROUPS * EXPERTS_PER_GROUP
NORM_EPS = 1e-5

LANES = 128
V7X_VMEM_BYTES = 64 * 1024 * 1024
MASK_NEG = -1.0e30

PROJ_TM = 1024
PROJ_TN = 1024
PREP_TM = 512
CMP_TQ = 256
ATT_TQ = 256
ATT_TK = 512
WIN_TK = 256
WIN_HEADS = 2
KX_WIDTH = 2 * LANES
OUT_TM = 512
MOE_TM = 256
DISP_TM = 256
COMB_TM = 128
VMEM_LIMIT = V7X_VMEM_BYTES - 8 * 1024 * 1024

COL_Q = 0
COL_Z = NSA_WIDTH
COL_XBC = COL_Z + SSM_WIDTH
COL_KV = COL_XBC + XBC_WIDTH
COL_SMALL = COL_KV + 6 * KV_WIDTH
PROJ_COLS = -(-(COL_SMALL + LANES) // PROJ_TN) * PROJ_TN
SMALL_DT_OFF = 3 * NSA_HEADS


def _cparams(sem, vmem=VMEM_LIMIT):
    return pltpu.CompilerParams(dimension_semantics=sem, vmem_limit_bytes=vmem)


def _dot(a, b):
    return jnp.dot(a, b, preferred_element_type=f32)


def _dot_t(a, b):
    return lax.dot_general(a, b, (((1,), (1,)), ((), ())), preferred_element_type=f32)


def _dot_hi(a, b):
    return jnp.dot(a, b, preferred_element_type=f32, precision=lax.Precision.HIGHEST)


def _proj_kernel(x_ref, w_ref, o_ref, xb_ref):
    @pl.when(pl.program_id(1) == 0)
    def _():
        xb_ref[...] = x_ref[...].astype(bf16)

    o_ref[...] = _dot(xb_ref[...], w_ref[...])


def _proj(x2, w_cat):
    n, d = x2.shape
    cols = w_cat.shape[1]
    tm = min(PROJ_TM, n)
    return pl.pallas_call(
        _proj_kernel,
        grid=(n // tm, cols // PROJ_TN),
        in_specs=[pl.BlockSpec((tm, d), lambda i, j: (i, 0)),
                  pl.BlockSpec((d, PROJ_TN), lambda i, j: (0, j))],
        out_specs=pl.BlockSpec((tm, PROJ_TN), lambda i, j: (i, j)),
        out_shape=jax.ShapeDtypeStruct((n, cols), f32),
        scratch_shapes=[pltpu.VMEM((tm, d), bf16)],
        compiler_params=_cparams(("parallel", "arbitrary")),
        name="proj",
    )(x2, w_cat)


def _rope_tables(pos_i32, invf):
    ang = pos_i32.astype(f32) * invf
    return jnp.cos(ang), jnp.sin(ang)


def _rope128(x, cos, sin):
    half = ROT_DIM // 2
    d = lax.broadcasted_iota(i32, x.shape, 1) % HEAD_DIM
    up = pltpu.roll(x, LANES - half, 1)
    dn = pltpu.roll(x, half, 1)
    rot = jnp.where(d < half, -up, dn)
    return x * cos + rot * sin


def _nsa_prep_kernel(pos_ref, invf_ref, q_ref, kc_ref, vc_ref, ks_ref, vs_ref, kw_ref, vw_ref,
                     qo_ref, cmp_ref, kso_ref, vso_ref, kwo_ref, vwo_ref):
    cos, sin = _rope_tables(pos_ref[...], invf_ref[...])
    scale = HEAD_DIM ** -0.5
    for c in range(NSA_WIDTH // LANES):
        t = _rope128(q_ref[:, c * LANES:(c + 1) * LANES], cos, sin) * scale
        qo_ref[0, 2 * c] = t[:, :HEAD_DIM].astype(bf16)
        qo_ref[0, 2 * c + 1] = t[:, HEAD_DIM:].astype(bf16)

    tm = pos_ref.shape[0]

    half_rows = tm // CMP_STRIDE
    for kind, src in enumerate((kc_ref, vc_ref)):
        for j in range(CMP_STRIDE):
            tok_j = src[pl.ds(j, half_rows, stride=CMP_STRIDE), :]
            for g in range(NSA_KV_GROUPS):
                cmp_ref[kind, 0, g, :, j * HEAD_DIM:(j + 1) * HEAD_DIM] = tok_j[:, g * HEAD_DIM:(g + 1) * HEAD_DIM]

    def values_with_ones(src, dst):
        t = src[...]
        tail = jnp.ones((tm, HEAD_DIM), bf16)
        for g in range(NSA_KV_GROUPS):
            vg = t[:, g * HEAD_DIM:(g + 1) * HEAD_DIM].astype(bf16)
            dst[0, g] = jnp.concatenate([vg, tail], axis=1)

    values_with_ones(vs_ref, vso_ref)
    values_with_ones(vw_ref, vwo_ref)

    kw_t = _rope128(kw_ref[...], cos, sin).T
    ks_t = _rope128(ks_ref[...], cos, sin).T
    for g in range(NSA_KV_GROUPS):
        for c in range(tm // WIN_TK):
            kwo_ref[0, g, c] = kw_t[g * HEAD_DIM:(g + 1) * HEAD_DIM, c * WIN_TK:(c + 1) * WIN_TK].astype(bf16)
    blk = lax.broadcasted_iota(i32, (LANES, tm), 0)
    tok = pl.program_id(1) * tm + lax.broadcasted_iota(i32, (LANES, tm), 1)
    onehot_t = jnp.where(tok // SEL_BLOCK == blk, 1.0, 0.0).astype(bf16)
    for g in range(NSA_KV_GROUPS):
        kso_ref[0, g, 0, 0:LANES, :] = onehot_t
        kg = ks_t[g * HEAD_DIM:(g + 1) * HEAD_DIM, :].astype(bf16)
        kso_ref[0, g, 0, LANES:2 * LANES, :] = jnp.concatenate([kg, jnp.zeros_like(kg)], axis=0)


def _nsa_prep(proj, pos128, invf, b, t):
    tm = PREP_TM
    assert tm == ATT_TK
    nt = t // tm
    row = lambda bi, ti: (bi * nt + ti, 0)
    kv0 = COL_KV // LANES
    in_specs = [pl.BlockSpec((tm, LANES), row),
                pl.BlockSpec((1, LANES), lambda bi, ti: (0, 0)),
                pl.BlockSpec((tm, NSA_WIDTH), lambda bi, ti: (bi * nt + ti, COL_Q // NSA_WIDTH))]
    for k in range(6):
        in_specs.append(pl.BlockSpec((tm, LANES), functools.partial(lambda bi, ti, k: (bi * nt + ti, kv0 + k), k=k)))
    head = lambda bi, ti: (bi, 0, ti, 0)
    tile5 = lambda bi, ti: (bi, 0, ti, 0, 0)
    g = NSA_KV_GROUPS
    out_specs = [pl.BlockSpec((1, NSA_HEADS, tm, HEAD_DIM), head),
                 pl.BlockSpec((2, 1, g, tm // CMP_STRIDE, CMP_STRIDE * HEAD_DIM), lambda bi, ti: (0, bi, 0, ti, 0)),
                 pl.BlockSpec((1, g, 1, KX_WIDTH, tm), tile5), pl.BlockSpec((1, g, tm, LANES), head),
                 pl.BlockSpec((1, g, tm // WIN_TK, HEAD_DIM, WIN_TK), tile5), pl.BlockSpec((1, g, tm, LANES), head)]
    out_shape = [jax.ShapeDtypeStruct((b, NSA_HEADS, t, HEAD_DIM), bf16),
                 jax.ShapeDtypeStruct((2, b, g, t // CMP_STRIDE, CMP_STRIDE * HEAD_DIM), f32),
                 jax.ShapeDtypeStruct((b, g, nt, KX_WIDTH, tm), bf16), jax.ShapeDtypeStruct((b, g, t, LANES), bf16),
                 jax.ShapeDtypeStruct((b, g, t // WIN_TK, HEAD_DIM, WIN_TK), bf16),
                 jax.ShapeDtypeStruct((b, g, t, LANES), bf16)]
    return pl.pallas_call(
        _nsa_prep_kernel,
        grid=(b, nt),
        in_specs=in_specs,
        out_specs=out_specs,
        out_shape=out_shape,
        compiler_params=_cparams(("parallel", "parallel")),
        name="nsa_prep",
    )(pos128, invf, proj, proj, proj, proj, proj, proj, proj)


def _cmp_mlp_kernel(a_ref, pe_ref, w1_ref, b1_ref, w2_ref, pos_ref, invf_ref, o_ref, ot_ref):
    kind = pl.program_id(0)
    a = a_ref[0, 0]
    nc = a.shape[0]
    u = _dot((a + pe_ref[0, 0]).astype(bf16), w1_ref[0, 0])
    v = _dot((a + pe_ref[0, 1]).astype(bf16), w1_ref[0, 1])
    v_next = pltpu.roll(v, nc - 1, 0)
    hid = jax.nn.gelu(u + v_next + b1_ref[0])
    out = _dot(hid.astype(bf16), w2_ref[0])
    cos, sin = _rope_tables(pos_ref[0], invf_ref[...])
    roped = _rope128(out, cos, sin)
    out = jnp.where(kind == 0, roped, out)
    o_ref[0, 0] = out[:, :HEAD_DIM].astype(bf16)
    ot_ref[0, 0] = out.T[:HEAD_DIM, :].astype(bf16)


def _cmp_mlp(a, pe, w1, b1, w2, posc, invf, b):
    _, bg, nc, hw = a.shape
    g = bg // b
    return pl.pallas_call(
        _cmp_mlp_kernel,
        grid=(2, bg),
        in_specs=[pl.BlockSpec((1, 1, nc, hw), lambda k, i: (k, i, 0, 0)),
                  pl.BlockSpec((1, 2, 1, hw), lambda k, i: (k, 0, 0, 0)),
                  pl.BlockSpec((1, 2, hw, CMP_HIDDEN), lambda k, i: (k, 0, 0, 0)),
                  pl.BlockSpec((1, 1, CMP_HIDDEN), lambda k, i: (k, 0, 0)),
                  pl.BlockSpec((1, CMP_HIDDEN, LANES), lambda k, i: (k, 0, 0)),
                  pl.BlockSpec((1, nc, LANES), lambda k, i: (i // g, 0, 0)),
                  pl.BlockSpec((1, LANES), lambda k, i: (0, 0))],
        out_specs=[pl.BlockSpec((1, 1, nc, HEAD_DIM), lambda k, i: (k, i, 0, 0)),
                   pl.BlockSpec((1, 1, HEAD_DIM, nc), lambda k, i: (k, i, 0, 0))],
        out_shape=[jax.ShapeDtypeStruct((2, bg, nc, HEAD_DIM), bf16),
                   jax.ShapeDtypeStruct((2, bg, HEAD_DIM, nc), bf16)],
        compiler_params=_cparams(("parallel", "parallel")),
        name="cmp_mlp",
    )(a, pe, w1, b1, w2, posc, invf)


def _cmp_attn_kernel(q_ref, kct_ref, vc_ref, cover_ref, oc_ref, sel_ref):
    qi = pl.program_id(2)
    nc = vc_ref.shape[2]
    rows = NSA_REP * CMP_TQ
    tq = qi * CMP_TQ + lax.broadcasted_iota(i32, (CMP_TQ, 1), 0)
    row_live = jnp.where(tq >= CMP_BLOCK - 1, 1.0, 0.0)
    tiny = jnp.finfo(f32).tiny

    def attend(ncols):
        kct = kct_ref[0, 0, :, :ncols]
        vc = vc_ref[0, 0, :ncols, :]
        cend = lax.broadcasted_iota(i32, (1, ncols), 1) * CMP_STRIDE + (CMP_BLOCK - 1)
        bias = jnp.where(cend <= tq, 0.0, MASK_NEG)
        s = _dot(q_ref[0].reshape(rows, HEAD_DIM), kct).reshape(NSA_REP, CMP_TQ, ncols) + bias[None]
        e = jnp.exp(s - jnp.max(s, axis=-1, keepdims=True))
        live = row_live[None]
        p = e * (live / jnp.maximum(live * jnp.sum(e, axis=-1, keepdims=True), tiny))
        oc_ref[0] = _dot(p.reshape(rows, ncols).astype(bf16), vc).reshape(NSA_REP, CMP_TQ, HEAD_DIM)
        psum = jnp.sum(p, axis=0)
        hi = psum.astype(bf16)
        lo = (psum - hi.astype(f32)).astype(bf16)
        cover = cover_ref[:ncols, :]
        imp = _dot(hi, cover) + _dot(lo, cover)

        j = lax.broadcasted_iota(i32, (CMP_TQ, LANES), 1)
        cur = tq // SEL_BLOCK
        forced = (j == 0) | (j == cur) | (j == cur - 1)
        valid = j * SEL_BLOCK <= tq
        imp = jnp.where(valid, jnp.where(forced, FORCED_SCORE, imp), -FORCED_SCORE)

        nblk = ncols * CMP_STRIDE // SEL_BLOCK
        jt = lax.broadcasted_iota(i32, (nblk, CMP_TQ), 0)

        def pick(_, carry):
            work, sel = carry
            m = jnp.max(work, axis=0, keepdims=True)
            first = jnp.min(jnp.where(work == m, jt, LANES), axis=0, keepdims=True)
            hit = jt == first
            return jnp.where(hit, -jnp.inf, work), jnp.where(hit, 1.0, sel)

        _, sel_t = lax.fori_loop(0, SEL_TOPK, pick, (imp.T[:nblk], jnp.zeros((nblk, CMP_TQ), f32)))
        if nblk < LANES:
            sel_t = jnp.concatenate([sel_t, jnp.zeros((LANES - nblk, CMP_TQ), f32)], axis=0)
        sel_ref[0, 0] = jnp.where(valid, jnp.where(sel_t.T > 0.0, 0.0, MASK_NEG), MASK_NEG).astype(bf16)

    need = (qi + 1) * (CMP_TQ // CMP_STRIDE)
    for ncols in range(LANES, nc + 1, LANES):
        @pl.when((need > ncols - LANES) & (need <= ncols))
        def _():
            attend(ncols)


def _cmp_attn(q_r, kvc, kvc_t, cover, b, t):
    g = NSA_KV_GROUPS
    nc = kvc.shape[2]
    nq = t // CMP_TQ
    return pl.pallas_call(
        _cmp_attn_kernel,
        grid=(b, g, nq),
        in_specs=[pl.BlockSpec((1, NSA_REP, CMP_TQ, HEAD_DIM), lambda bi, gi, qi: (bi, gi, qi, 0)),
                  pl.BlockSpec((1, 1, HEAD_DIM, nc), lambda bi, gi, qi: (0, bi * g + gi, 0, 0)),
                  pl.BlockSpec((1, 1, nc, HEAD_DIM), lambda bi, gi, qi: (1, bi * g + gi, 0, 0)),
                  pl.BlockSpec((nc, LANES), lambda bi, gi, qi: (0, 0))],
        out_specs=[pl.BlockSpec((1, NSA_REP, CMP_TQ, HEAD_DIM), lambda bi, gi, qi: (bi, gi, qi, 0)),
                   pl.BlockSpec((1, 1, CMP_TQ, LANES), lambda bi, gi, qi: (bi, gi, qi, 0))],
        out_shape=[jax.ShapeDtypeStruct((b, NSA_HEADS, t, HEAD_DIM), f32),
                   jax.ShapeDtypeStruct((b, g, t, LANES), bf16)],
        compiler_params=_cparams(("parallel", "parallel", "parallel")),
        name="cmp_attn",
    )(q_r, kvc_t, kvc, cover)


def _nsa_attn_kernel(q_ref, kx_ref, vs_ref, kw_ref, vw_ref, sel_ref, oc_ref, gate_ref,
                     o_ref, qx_ref, s_ref, m_ref, acc_ref, yw_ref):
    gi = pl.program_id(1)
    qi = pl.program_id(2)
    start = qi * ATT_TQ
    tq = start + lax.broadcasted_iota(i32, (ATT_TQ, 1), 0)

    selb = sel_ref[0, 0]
    for r in range(NSA_REP):
        qx_ref[r * ATT_TQ:(r + 1) * ATT_TQ, 0:LANES] = selb
        qr = q_ref[0, r]
        qx_ref[r * ATT_TQ:(r + 1) * ATT_TQ, LANES:2 * LANES] = jnp.concatenate([qr, jnp.zeros_like(qr)], axis=1)
    rows = NSA_REP * ATT_TQ

    m_ref[...] = jnp.full(m_ref.shape, MASK_NEG, f32)
    acc_ref[...] = jnp.zeros(acc_ref.shape, f32)

    def consume(kt, v_ref, bias, parts=1):
        k0 = pl.multiple_of(kt * ATT_TK, ATT_TK)
        v = v_ref[0, 0, pl.ds(k0, ATT_TK), :]
        heads = NSA_REP // parts
        for part in range(parts):
            rs = slice(part * heads * ATT_TQ, (part + 1) * heads * ATT_TQ)
            s = s_ref[rs]
            if bias is not None:
                s = (s.reshape(heads, ATT_TQ, ATT_TK) + bias[None]).reshape(heads * ATT_TQ, ATT_TK)
            m_prev = m_ref[rs]
            m_new = jnp.maximum(m_prev, jnp.max(s, axis=-1, keepdims=True))
            alpha = jnp.exp(m_prev - m_new)
            p = jnp.exp(s - jnp.concatenate([m_new] * (ATT_TK // LANES), axis=1))
            acc_ref[rs] = alpha * acc_ref[rs] + _dot(p.astype(bf16), v)
            m_ref[rs] = m_new

    def kpos(kt):
        return kt * ATT_TK + lax.broadcasted_iota(i32, (1, ATT_TK), 1)

    last = start // ATT_TK
    for part in range(2):
        rs = slice(part * rows // 2, (part + 1) * rows // 2)
        s_ref[rs] = _dot(qx_ref[rs], kx_ref[0, 0, 0])

    def sel_step(kt, _):
        consume(kt, vs_ref, None)
        s_ref[...] = _dot(qx_ref[...], kx_ref[0, 0, kt + 1])
        return 0

    n_win = (WINDOW + ATT_TQ) // WIN_TK
    w_first = jnp.maximum(start // WIN_TK - WINDOW // WIN_TK, 0)
    kw = jnp.concatenate([kw_ref[0, 0, w_first + i] for i in range(n_win)], axis=1)
    w0 = pl.multiple_of(w_first * WIN_TK, WIN_TK)
    vw = vw_ref[0, 0, pl.ds(w0, n_win * WIN_TK), :]
    kp = w0 + lax.broadcasted_iota(i32, (1, n_win * WIN_TK), 1)
    wbias = jnp.where(kp <= tq, jnp.where(kp > tq - WINDOW, 0.0, MASK_NEG), MASK_NEG)
    hh = WIN_HEADS
    gates = jax.nn.sigmoid(gate_ref[...])
    per_group = 3 * NSA_REP
    shifted = gates
    for gg in range(1, NSA_KV_GROUPS):
        shifted = jnp.where(gi == gg, pltpu.roll(gates, LANES - gg * per_group, 1), shifted)

    def gate(r, br):
        c = r * 3 + br
        return jnp.broadcast_to(shifted[:, c:c + 1], (ATT_TQ, HEAD_DIM))

    upper = pltpu.roll(shifted, HEAD_DIM, 1)[:, HEAD_DIM:]

    def gated_output(a, r, br):
        c = r * 3 + br
        ratio = upper / a[:, HEAD_DIM:]
        return a[:, :HEAD_DIM] * jnp.broadcast_to(ratio[:, c:c + 1], (ATT_TQ, HEAD_DIM))

    for half in range(NSA_REP // hh):
        qh = q_ref[0, half * hh:(half + 1) * hh].reshape(hh * ATT_TQ, HEAD_DIM)
        sw = _dot(qh, kw).reshape(hh, ATT_TQ, n_win * WIN_TK) + wbias[None]
        sw = sw.reshape(hh * ATT_TQ, n_win * WIN_TK)
        pw = jnp.exp(sw - jnp.max(sw, axis=-1, keepdims=True))
        aw = _dot(pw.astype(bf16), vw)
        for i in range(hh):
            r = half * hh + i
            a = aw[i * ATT_TQ:(i + 1) * ATT_TQ]
            yw_ref[r] = gated_output(a, r, 2)

    def sel_pair(j, _):
        sel_step(2 * j, 0)
        sel_step(2 * j + 1, 0)
        return 0

    lax.fori_loop(0, last // 2, sel_pair, 0)

    @pl.when(last % 2 == 1)
    def _():
        sel_step(last - 1, 0)

    consume(last, vs_ref, jnp.where(kpos(last) <= tq, 0.0, MASK_NEG), parts=2)

    for r in range(NSA_REP):
        osel = gated_output(acc_ref[r * ATT_TQ:(r + 1) * ATT_TQ], r, 1)
        o_ref[:, r * HEAD_DIM:(r + 1) * HEAD_DIM] = (gate(r, 0) * oc_ref[0, r] + osel + yw_ref[r]).astype(o_ref.dtype)


def _nsa_attn(q_r, kx, vs, kw, vw, selb, oc, proj, b, t):
    g = NSA_KV_GROUPS
    nq = t // ATT_TQ
    once = dict(pipeline_mode=pl.Buffered(1))
    vspec = pl.BlockSpec((1, 1, t, LANES), lambda bi, gi, qi: (bi, gi, 0, 0), **once)
    hspec = pl.BlockSpec((1, NSA_REP, ATT_TQ, HEAD_DIM), lambda bi, gi, qi: (bi, gi, qi, 0))
    return pl.pallas_call(
        _nsa_attn_kernel,
        grid=(b, g, nq),
        in_specs=[hspec,
                  pl.BlockSpec((1, 1, t // ATT_TK, KX_WIDTH, ATT_TK), lambda bi, gi, qi: (bi, gi, 0, 0, 0), **once),
                  vspec,
                  pl.BlockSpec((1, 1, t // WIN_TK, HEAD_DIM, WIN_TK), lambda bi, gi, qi: (bi, gi, 0, 0, 0), **once),
                  vspec,
                  pl.BlockSpec((1, 1, ATT_TQ, LANES), lambda bi, gi, qi: (bi, gi, qi, 0)),
                  hspec,
                  pl.BlockSpec((ATT_TQ, LANES), lambda bi, gi, qi: (bi * nq + qi, COL_SMALL // LANES))],
        out_specs=pl.BlockSpec((ATT_TQ, NSA_REP * HEAD_DIM), lambda bi, gi, qi: (bi * nq + qi, gi)),
        out_shape=jax.ShapeDtypeStruct((b * t, NSA_WIDTH), bf16),
        scratch_shapes=[pltpu.VMEM((NSA_REP * ATT_TQ, KX_WIDTH), bf16),
                        pltpu.VMEM((NSA_REP * ATT_TQ, ATT_TK), f32),
                        pltpu.VMEM((NSA_REP * ATT_TQ, LANES), f32),
                        pltpu.VMEM((NSA_REP * ATT_TQ, LANES), f32),
                        pltpu.VMEM((NSA_REP, ATT_TQ, HEAD_DIM), f32)],
        compiler_params=_cparams(("parallel", "parallel", "arbitrary")),
        name="nsa_attn",
    )(q_r, kx, vs, kw, vw, selb, oc, proj)


HALO = 8


def _ssd_kernel(xbc_ref, z_ref, small_ref, dtt_ref, cw_ref, cb_ref, dtb_r_ref, dtb_c_ref,
                alog_r_ref, alog_c_ref, dskip_ref, nw_ref, o_ref, ext_ref, st_ref):
    c = pl.program_id(1)
    L = CHUNK

    @pl.when(c == 0)
    def _():
        ext_ref[0:HALO, :] = jnp.zeros((HALO, XBC_WIDTH), f32)
        st_ref[...] = jnp.zeros_like(st_ref)

    ext_ref[HALO:HALO + L, :] = xbc_ref[...]
    conv = cb_ref[...]
    for k in range(CONV_WIDTH):
        off = HALO - (CONV_WIDTH - 1) + k
        conv = conv + cw_ref[k:k + 1, :] * ext_ref[off:off + L, :]
    ext_ref[0:HALO, :] = ext_ref[L:L + HALO, :]
    act = conv * jax.nn.sigmoid(conv)
    xs = act[:, :SSM_WIDTH]
    bm = act[:, SSM_WIDTH:SSM_WIDTH + SSM_GROUPS * SSM_STATE]
    cm = act[:, SSM_WIDTH + SSM_GROUPS * SSM_STATE:]

    dt_c = jax.nn.softplus(small_ref[:, SMALL_DT_OFF:SMALL_DT_OFF + SSM_HEADS] + dtb_r_ref[...])
    dt_r = jax.nn.softplus(dtt_ref[0] + dtb_c_ref[...])
    a_r = -jnp.exp(alog_r_ref[...])
    a_c = -jnp.exp(alog_c_ref[...])
    row = lax.broadcasted_iota(i32, (L, L), 0)
    col = lax.broadcasted_iota(i32, (L, L), 1)
    causal = col <= row
    tri = jnp.where(causal, 1.0, 0.0)
    acs_c = _dot_hi(tri, dt_c * a_r)
    acs_r = _dot_hi(dt_r * a_c, jnp.where(row <= col, 1.0, 0.0))

    z = z_ref[...]
    for g in range(SSM_GROUPS):
        cg = cm[:, g * SSM_STATE:(g + 1) * SSM_STATE].astype(bf16)
        bg = bm[:, g * SSM_STATE:(g + 1) * SSM_STATE]
        cb = _dot_t(cg, bg.astype(bf16))
        bg_t = bg.T
        ssq = jnp.zeros((L, 1), f32)
        yg = []
        for pp in range(SSM_REP // 2):
            pair = g * (SSM_REP // 2) + pp
            ps = slice(pair * LANES, (pair + 1) * LANES)
            x_p = xs[:, ps]
            low = lax.broadcasted_iota(i32, (L, LANES), 1) < SSM_HEADDIM
            low_n = lax.broadcasted_iota(i32, (SSM_STATE, LANES), 1) < SSM_HEADDIM
            x_own = (jnp.where(low, x_p, 0.0).astype(bf16), jnp.where(low, 0.0, x_p).astype(bf16))
            st = st_ref[pair]
            y = jnp.zeros((L, LANES), f32)
            st_in = jnp.zeros((SSM_STATE, LANES), f32)
            grow, keep = [], []
            for j in range(2):
                h = 2 * pair + j
                a_col = acs_c[:, h:h + 1]
                a_row = acs_r[h:h + 1, :]
                dt_row = dt_r[h:h + 1, :]
                a_last = acs_r[h:h + 1, L - 1:L]
                seg = a_col - a_row
                decay = jnp.where(causal, jnp.exp(jnp.where(causal, seg, 0.0)), 0.0)
                w = cb * decay * dt_row
                y = y + _dot(w.astype(bf16), x_own[j])
                bscaled = bg_t * (jnp.exp(a_last - a_row) * dt_row)
                st_in = st_in + _dot(bscaled.astype(bf16), x_own[j])
                grow.append(jnp.exp(a_col))
                keep.append(jnp.exp(a_last))
            y = y + _dot(cg, st.astype(bf16)) * jnp.where(low, grow[0], grow[1])
            st_ref[pair] = jnp.where(low_n, keep[0], keep[1]) * st + st_in
            y = y + dskip_ref[:, ps] * x_p
            zp = z[:, ps]
            y = y * (zp * jax.nn.sigmoid(zp))
            ssq = ssq + jnp.sum(y * y, axis=-1, keepdims=True)
            yg.append(y)
        rs = lax.rsqrt(ssq / (SSM_REP * SSM_HEADDIM) + NORM_EPS)
        for pp in range(SSM_REP // 2):
            pair = g * (SSM_REP // 2) + pp
            ps = slice(pair * LANES, (pair + 1) * LANES)
            o_ref[:, ps] = (yg[pp] * rs * nw_ref[:, ps]).astype(o_ref.dtype)


def _ssd(proj, dtt, cw, cb, dtb_r, dtb_c, alog_r, alog_c, dskip, nw, b, t):
    nch = t // CHUNK
    row = lambda bi, ci: bi * nch + ci
    const2 = lambda bi, ci: (0, 0)
    return pl.pallas_call(
        _ssd_kernel,
        grid=(b, nch),
        in_specs=[pl.BlockSpec((CHUNK, XBC_WIDTH), lambda bi, ci: (row(bi, ci), COL_XBC // XBC_WIDTH)),
                  pl.BlockSpec((CHUNK, SSM_WIDTH), lambda bi, ci: (row(bi, ci), COL_Z // SSM_WIDTH)),
                  pl.BlockSpec((CHUNK, LANES), lambda bi, ci: (row(bi, ci), COL_SMALL // LANES)),
                  pl.BlockSpec((1, SSM_HEADS, CHUNK), lambda bi, ci: (bi, 0, ci)),
                  pl.BlockSpec((CONV_WIDTH, XBC_WIDTH), const2),
                  pl.BlockSpec((1, XBC_WIDTH), const2),
                  pl.BlockSpec((1, SSM_HEADS), const2),
                  pl.BlockSpec((SSM_HEADS, 1), const2),
                  pl.BlockSpec((1, SSM_HEADS), const2),
                  pl.BlockSpec((SSM_HEADS, 1), const2),
                  pl.BlockSpec((1, SSM_WIDTH), const2),
                  pl.BlockSpec((1, SSM_WIDTH), const2)],
        out_specs=pl.BlockSpec((CHUNK, SSM_WIDTH), lambda bi, ci: (row(bi, ci), 0)),
        out_shape=jax.ShapeDtypeStruct((b * t, SSM_WIDTH), bf16),
        scratch_shapes=[pltpu.VMEM((HALO + CHUNK, XBC_WIDTH), f32),
                        pltpu.VMEM((SSM_HEADS // 2, SSM_STATE, 2 * SSM_HEADDIM), f32)],
        compiler_params=_cparams(("parallel", "arbitrary")),
        name="ssd",
    )(proj, proj, proj, dtt, cw, cb, dtb_r, dtb_c, alog_r, alog_c, dskip, nw)


def _layer_norm(v, g, b):
    mu = jnp.mean(v, axis=-1, keepdims=True)
    d = v - mu
    var = jnp.mean(d * d, axis=-1, keepdims=True)
    return d * lax.rsqrt(var + NORM_EPS) * g + b


def _outproj_kernel(alpha, ya_ref, yb_ref, x_ref, wa_ref, wb_ref, g_ref, b_ref, wr_ref, br_ref,
                    h_ref, route_ref, cnt_ref):
    i = pl.program_id(0)
    tm = x_ref.shape[0]
    mix = _dot(ya_ref[...], wa_ref[...]) + _dot(yb_ref[...], wb_ref[...])
    h = _layer_norm(alpha * x_ref[...] + mix, g_ref[...], b_ref[...])
    h_ref[...] = h

    h_hi = h.astype(bf16)
    h_lo = (h - h_hi.astype(f32)).astype(bf16)
    t = _dot(h_hi, wr_ref[...])
    logits = t[:, :LANES] + t[:, LANES:] + _dot(h_lo, wr_ref[:, :LANES]) + br_ref[...]
    lane = lax.broadcasted_iota(i32, (tm, LANES), 1)
    ninf = -jnp.inf
    gmask = lane < N_EXPERT_GROUPS
    gl = jnp.where(gmask, logits, ninf)
    ge = jnp.where(gmask, jnp.exp(gl - jnp.max(gl, axis=-1, keepdims=True)), 0.0)
    pg = ge / jnp.sum(ge, axis=-1, keepdims=True)
    g_gate = jnp.max(pg, axis=-1, keepdims=True)
    g_sel = jnp.min(jnp.where(gmask & (pg == g_gate), lane, LANES), axis=-1, keepdims=True)
    lo = N_EXPERT_GROUPS + g_sel * EXPERTS_PER_GROUP
    emask = (lane >= lo) & (lane < lo + EXPERTS_PER_GROUP)
    el = jnp.where(emask, logits, ninf)
    ee = jnp.where(emask, jnp.exp(el - jnp.max(el, axis=-1, keepdims=True)), 0.0)
    pe = ee / jnp.sum(ee, axis=-1, keepdims=True)
    p0 = jnp.max(pe, axis=-1, keepdims=True)
    l0 = jnp.min(jnp.where(emask & (pe == p0), lane, LANES), axis=-1, keepdims=True)
    rest = jnp.where(emask & (lane != l0), pe, ninf)
    p1 = jnp.max(rest, axis=-1, keepdims=True)
    l1 = jnp.min(jnp.where(rest == p1, lane, LANES), axis=-1, keepdims=True)
    psum = p0 + p1
    w0 = g_gate * p0 / psum
    w1 = g_gate * p1 / psum
    e0 = l0 - N_EXPERT_GROUPS
    e1 = l1 - N_EXPERT_GROUPS

    @pl.when(i == 0)
    def _():
        cnt_ref[...] = jnp.zeros_like(cnt_ref)

    oh0 = lane == e0
    oh1 = lane == e1
    both = jnp.where(oh0, 1.0, 0.0) + jnp.where(oh1, 1.0, 0.0)
    r_i = lax.broadcasted_iota(i32, (tm, tm), 0)
    c_i = lax.broadcasted_iota(i32, (tm, tm), 1)
    strict = jnp.where(c_i < r_i, 1.0, 0.0).astype(bf16)
    before = _dot(strict, both.astype(bf16)) + cnt_ref[...]
    rank0 = jnp.sum(jnp.where(oh0, before, 0.0), axis=-1, keepdims=True)
    rank1 = jnp.sum(jnp.where(oh1, before, 0.0), axis=-1, keepdims=True)
    cnt_ref[...] = cnt_ref[...] + jnp.sum(both, axis=0, keepdims=True)

    out = jnp.where(lane == 0, e0.astype(f32), 0.0)
    out = jnp.where(lane == 1, e1.astype(f32), out)
    out = jnp.where(lane == 2, w0, out)
    out = jnp.where(lane == 3, w1, out)
    out = jnp.where(lane == 4, rank0, out)
    out = jnp.where(lane == 5, rank1, out)
    route_ref[...] = out


def _outproj(y_nsa, y_ssm, x2, wa, wb, g, bta, wr, br, alpha):
    n, d = x2.shape
    tm = OUT_TM
    const = lambda i: (0, 0)
    rowb = lambda i: (i, 0)
    return pl.pallas_call(
        functools.partial(_outproj_kernel, alpha),
        grid=(n // tm,),
        in_specs=[pl.BlockSpec((tm, NSA_WIDTH), rowb), pl.BlockSpec((tm, SSM_WIDTH), rowb),
                  pl.BlockSpec((tm, d), rowb),
                  pl.BlockSpec((NSA_WIDTH, d), const), pl.BlockSpec((SSM_WIDTH, d), const),
                  pl.BlockSpec((1, d), const), pl.BlockSpec((1, d), const),
                  pl.BlockSpec((d, 2 * LANES), const), pl.BlockSpec((1, LANES), const)],
        out_specs=[pl.BlockSpec((tm, d), rowb), pl.BlockSpec((tm, LANES), rowb),
                   pl.BlockSpec((1, LANES), const)],
        out_shape=[jax.ShapeDtypeStruct((n, d), f32), jax.ShapeDtypeStruct((n, LANES), f32),
                   jax.ShapeDtypeStruct((1, LANES), f32)],
        compiler_params=_cparams(("arbitrary",)),
        name="outproj",
    )(y_nsa, y_ssm, x2, wa, wb, g, bta, wr, br)


def _row_copy(src_ref, src_row, dst_ref, dst_row, sem):
    return pltpu.make_async_copy(src_ref.at[pl.ds(src_row, 1), :], dst_ref.at[pl.ds(dst_row, 1), :], sem)


def _pack_bf16_pairs(x):
    half = x.shape[1] // 2
    hi = pltpu.bitcast(x[:, :half].astype(bf16).astype(f32), jnp.uint32)
    lo = pltpu.bitcast(x[:, half:].astype(bf16).astype(f32), jnp.uint32)
    return hi | (lo >> 16)


def _unpack_bf16_pairs(words):
    return (pltpu.bitcast(words & jnp.uint32(0xFFFF0000), f32), pltpu.bitcast(words << 16, f32))


def _dest_row(dest_ref, tm, k, r):
    return dest_ref[0, 0, k * tm + r]


def _dispatch_kernel(dest_ref, zflag_ref, h_ref, xs_ref, zero_ref, tile_ref, pk_ref, sem, lsem, zsem):
    tm = DISP_TM
    nb = zflag_ref.shape[0]
    i = pl.program_id(0)

    @pl.when(i == 0)
    def _():
        zero_ref[...] = jnp.zeros_like(zero_ref)

        def zblock(i):
            rows = pl.ds(pl.multiple_of(i * MOE_TM, MOE_TM), MOE_TM)
            return pltpu.make_async_copy(zero_ref, xs_ref.at[rows, :], zsem)

        def zstart(i, _):
            @pl.when(zflag_ref[i] != 0)
            def _():
                zblock(i).start()
            return 0

        def zwait(i, _):
            @pl.when(zflag_ref[i] != 0)
            def _():
                zblock(i).wait()
            return 0

        lax.fori_loop(0, nb, zstart, 0)
        lax.fori_loop(0, nb, zwait, 0)

    nslot = tile_ref.shape[0]
    nsteps = pl.num_programs(0)
    slot = i % nslot

    def tile_load(step, s):
        rows = pl.ds(pl.multiple_of(step * tm, tm), tm)
        return pltpu.make_async_copy(h_ref.at[rows, :], tile_ref.at[s], lsem.at[s])

    @pl.when(i == 0)
    def _():
        tile_load(0, 0).start()

    tile_load(i, slot).wait()

    @pl.when(i + 1 < nsteps)
    def _():
        tile_load(i + 1, (i + 1) % nslot).start()

    pk_ref[slot] = _pack_bf16_pairs(tile_ref[slot])

    def issue(r, _):
        for k in range(2):
            _row_copy(pk_ref.at[slot], r, xs_ref, _dest_row(dest_ref, tm, k, r), sem.at[slot]).start()
        return 0

    lax.fori_loop(0, tm, issue, 0, unroll=16)

    def wait_rows(s):
        for k in range(2):
            pltpu.make_async_copy(pk_ref.at[s], xs_ref.at[pl.ds(0, tm), :], sem.at[s]).wait()

    @pl.when(i > 0)
    def _():
        wait_rows((i - 1) % nslot)

    @pl.when(i == nsteps - 1)
    def _():
        wait_rows(slot)


def _dispatch(dest_t, zflag, h, p_rows):
    n, d = h.shape
    tm = DISP_TM
    return pl.pallas_call(
        _dispatch_kernel,
        grid=(n // tm,),
        in_specs=[pl.BlockSpec((1, 1, 2 * tm), lambda i: (i, 0, 0), memory_space=pltpu.SMEM),
                  pl.BlockSpec(memory_space=pltpu.SMEM),
                  pl.BlockSpec(memory_space=pl.ANY)],
        out_specs=pl.BlockSpec(memory_space=pl.ANY),
        out_shape=jax.ShapeDtypeStruct((p_rows, d // 2), jnp.uint32),
        scratch_shapes=[pltpu.VMEM((MOE_TM, d // 2), jnp.uint32), pltpu.VMEM((3, tm, d), f32),
                        pltpu.VMEM((3, tm, d // 2), jnp.uint32),
                        pltpu.SemaphoreType.DMA((3,)), pltpu.SemaphoreType.DMA((3,)), pltpu.SemaphoreType.DMA(())],
        compiler_params=_cparams(("arbitrary",)),
        name="dispatch",
    )(dest_t, zflag, h)


def _experts_kernel(be_ref, nu_ref, first_ref, next_ref, slot_ref, xs_ref, wg_ref, wu_ref, wd_ref, y_ref,
                    fg_ref, fu_ref, fd_ref, wgb_ref, wub_ref, wdb_ref, sems):
    i = pl.program_id(0)

    def weight_loads(e, s):
        return (pltpu.make_async_copy(wg_ref.at[e], fg_ref.at[s], sems.at[s, 0]),
                pltpu.make_async_copy(wu_ref.at[e], fu_ref.at[s], sems.at[s, 1]),
                pltpu.make_async_copy(wd_ref.at[e], fd_ref.at[s], sems.at[s, 2]))

    @pl.when(i == 0)
    def _():
        for c in weight_loads(be_ref[0], 0):
            c.start()

    @pl.when(first_ref[i] != 0)
    def _():
        s = slot_ref[i]
        for c in weight_loads(be_ref[i], s):
            c.wait()
        wgb_ref[...] = fg_ref[s].astype(bf16)
        wub_ref[...] = fu_ref[s].astype(bf16)
        wdb_ref[...] = fd_ref[s].astype(bf16)

        @pl.when(next_ref[i] >= 0)
        def _():
            for c in weight_loads(next_ref[i], 1 - s):
                c.start()

    @pl.when(i < nu_ref[0])
    def _():
        first, second = _unpack_bf16_pairs(xs_ref[...])
        xb = jnp.concatenate([first.astype(bf16), second.astype(bf16)], axis=1)
        gte = _dot(xb, wgb_ref[...])
        up = _dot(xb, wub_ref[...])
        act = gte * jax.nn.sigmoid(gte) * up
        y = _dot(act.astype(bf16), wdb_ref[...])
        y_ref[...] = _pack_bf16_pairs(y)

    @pl.when(i >= nu_ref[0])
    def _():
        y_ref[...] = jnp.zeros_like(y_ref)


def _experts(block_e, n_used, first, next_e, slot, xs, w_gate, w_up, w_down):
    p_rows = xs.shape[0]
    d = w_gate.shape[1]
    tm = MOE_TM
    nb = p_rows // tm
    de = w_gate.shape[-1]
    xmap = lambda i, be, nu, fi, ne, sl: (jnp.maximum(jnp.minimum(i, nu[0] - 1), 0), 0)
    hbm = pl.BlockSpec(memory_space=pl.ANY)
    return pl.pallas_call(
        _experts_kernel,
        grid_spec=pltpu.PrefetchScalarGridSpec(
            num_scalar_prefetch=5,
            grid=(nb,),
            in_specs=[pl.BlockSpec((tm, d // 2), xmap), hbm, hbm, hbm],
            out_specs=pl.BlockSpec((tm, d // 2), lambda i, be, nu, fi, ne, sl: (i, 0)),
            scratch_shapes=[pltpu.VMEM((2, d, de), f32), pltpu.VMEM((2, d, de), f32), pltpu.VMEM((2, de, d), f32),
                            pltpu.VMEM((d, de), bf16), pltpu.VMEM((d, de), bf16), pltpu.VMEM((de, d), bf16),
                            pltpu.SemaphoreType.DMA((2, 3))],
        ),
        out_shape=jax.ShapeDtypeStruct((p_rows, d // 2), jnp.uint32),
        compiler_params=_cparams(("arbitrary",)),
        name="experts",
    )(block_e, n_used, first, next_e, slot, xs, w_gate, w_up, w_down)


def _combine_kernel(alpha, dest_ref, ndest_ref, route_ref, h_ref, g_ref, b_ref, y_ref, o_ref, buf_ref, sems):
    tm = h_ref.shape[0]
    i = pl.program_id(0)
    slot = i % 2

    def start_row(ids_ref, s, r):
        for k in range(2):
            _row_copy(y_ref, _dest_row(ids_ref, tm, k, r), buf_ref.at[s, k], r, sems.at[s]).start()

    def wait_tile():
        for k in range(2):
            pltpu.make_async_copy(y_ref.at[pl.ds(0, tm), :], buf_ref.at[slot, k], sems.at[slot]).wait()

    def finish_tile():
        route = route_ref[...]
        lo0, hi0 = _unpack_bf16_pairs(buf_ref[slot, 0])
        lo1, hi1 = _unpack_bf16_pairs(buf_ref[slot, 1])
        w0, w1 = route[:, 2:3], route[:, 3:4]
        ffn = jnp.concatenate([w0 * lo0 + w1 * lo1, w0 * hi0 + w1 * hi1], axis=1)
        o_ref[...] = _layer_norm(alpha * h_ref[...] + ffn, g_ref[...], b_ref[...])

    @pl.when(i == 0)
    def _():
        def issue(r, _):
            start_row(dest_ref, 0, r)
            return 0
        lax.fori_loop(0, tm, issue, 0, unroll=4)

    @pl.when(i + 1 < pl.num_programs(0))
    def _():
        wait_tile()
        for r in range(tm):
            start_row(ndest_ref, 1 - slot, r)
        finish_tile()

    @pl.when(i + 1 == pl.num_programs(0))
    def _():
        wait_tile()
        finish_tile()


def _combine(dest_t, route, h, g, bta, y, alpha):
    n, d = h.shape
    tm = COMB_TM
    const = lambda i: (0, 0)
    return pl.pallas_call(
        functools.partial(_combine_kernel, alpha),
        grid=(n // tm,),
        in_specs=[pl.BlockSpec((1, 1, 2 * tm), lambda i: (i, 0, 0), memory_space=pltpu.SMEM),
                  pl.BlockSpec((1, 1, 2 * tm), lambda i: (jnp.minimum(i + 1, n // tm - 1), 0, 0),
                               memory_space=pltpu.SMEM),
                  pl.BlockSpec((tm, LANES), lambda i: (i, 0)),
                  pl.BlockSpec((tm, d), lambda i: (i, 0)),
                  pl.BlockSpec((1, d), const), pl.BlockSpec((1, d), const),
                  pl.BlockSpec(memory_space=pl.ANY)],
        out_specs=pl.BlockSpec((tm, d), lambda i: (i, 0)),
        out_shape=jax.ShapeDtypeStruct((n, d), f32),
        scratch_shapes=[pltpu.VMEM((2, 2, tm, d // 2), jnp.uint32), pltpu.SemaphoreType.DMA((2,))],
        compiler_params=_cparams(("arbitrary",)),
        name="combine",
    )(dest_t, dest_t, route, h, g, bta, y)


def _tile_dest(dest, tm):
    n = dest.shape[1]
    return dest.reshape(2, n // tm, tm).transpose(1, 0, 2).reshape(n // tm, 1, 2 * tm)


def _layer(x, positions, w_in, cmp_k_pe, cmp_k_w1, cmp_k_b1, cmp_k_w2, cmp_v_pe, cmp_v_w1, cmp_v_b1, cmp_v_w2,
           conv_w, conv_b, dt_bias, a_log, d_skip, ssm_norm_w, w_out, ln1_g, ln1_b,
           w_router_group, b_router_group, w_router_expert, b_router_expert, w_gate, w_up, w_down, ln2_g, ln2_b,
           alpha):
    b, t, d = x.shape
    n = b * t
    assert t % max(ATT_TK, ATT_TQ, CMP_TQ, CHUNK, PREP_TM) == 0 and n % max(PROJ_TM, OUT_TM, DISP_TM) == 0
    assert t // SEL_BLOCK <= LANES and (t // CMP_STRIDE) % LANES == 0 and t >= WINDOW + ATT_TQ
    x2 = x.reshape(n, d)

    c0 = NSA_WIDTH
    c1 = c0 + 6 * KV_WIDTH
    c2 = c1 + 3 * NSA_HEADS
    c3 = c2 + SSM_WIDTH
    c4 = c3 + XBC_WIDTH
    w_small = jnp.concatenate([w_in[:, c1:c2], w_in[:, c4:], jnp.zeros((d, LANES - 3 * NSA_HEADS - SSM_HEADS), f32)], axis=1)
    w_cat = jnp.concatenate([w_in[:, :c0], w_in[:, c2:c3], w_in[:, c3:c4], w_in[:, c0:c1], w_small,
                             jnp.zeros((d, PROJ_COLS - COL_SMALL - LANES), f32)], axis=1).astype(bf16)
    proj = _proj(x2, w_cat)

    lane = np.arange(LANES) % HEAD_DIM
    inv_freq = ROPE_THETA ** (-jnp.arange(0, ROT_DIM, 2, dtype=f32) / ROT_DIM)
    invf = jnp.where(lane < ROT_DIM, inv_freq[lane % (ROT_DIM // 2)], 0.0).astype(f32)[None, :]
    pos128 = jnp.broadcast_to(positions.reshape(n, 1), (n, LANES))
    q_r, cmp_in, k_sel, v_sel, k_win, v_win = _nsa_prep(proj, pos128, invf, b, t)

    nc = t // CMP_STRIDE
    half_w = CMP_STRIDE * HEAD_DIM
    a = cmp_in.reshape(2, b * NSA_KV_GROUPS, nc, half_w)
    pe = jnp.stack([cmp_k_pe, cmp_v_pe]).reshape(2, 2, 1, half_w)
    w1 = jnp.stack([cmp_k_w1, cmp_v_w1]).reshape(2, 2, half_w, CMP_HIDDEN).astype(bf16)
    b1 = jnp.stack([cmp_k_b1, cmp_v_b1]).reshape(2, 1, CMP_HIDDEN)
    w2 = jnp.pad(jnp.stack([cmp_k_w2, cmp_v_w2]), ((0, 0), (0, 0), (0, LANES - HEAD_DIM))).astype(bf16)
    cend = jnp.minimum(jnp.arange(nc) * CMP_STRIDE + CMP_BLOCK - 1, t - 1)
    posc = jnp.broadcast_to(positions[:, cend][:, :, None], (b, nc, LANES))
    kvc, kvc_t = _cmp_mlp(a, pe, w1, b1, w2, posc, invf, b)

    c_start = np.arange(nc)[:, None] * CMP_STRIDE
    s_start = np.arange(LANES)[None, :] * SEL_BLOCK
    cover = ((c_start < s_start + SEL_BLOCK) & (c_start + CMP_BLOCK > s_start)
             & (np.arange(nc)[:, None] < nc - 1) & (np.arange(LANES)[None, :] < t // SEL_BLOCK))
    cover = jnp.asarray(cover, bf16)
    o_cmp, selb = _cmp_attn(q_r, kvc, kvc_t, cover, b, t)

    y_nsa = _nsa_attn(q_r, k_sel, v_sel, k_win, v_win, selb, o_cmp, proj, b, t)

    dt_raw = proj[:, COL_SMALL + SMALL_DT_OFF:COL_SMALL + SMALL_DT_OFF + SSM_HEADS]
    dtt = dt_raw.reshape(b, t, SSM_HEADS).transpose(0, 2, 1)
    y_ssm = _ssd(proj, dtt, conv_w.reshape(CONV_WIDTH, XBC_WIDTH), conv_b.reshape(1, XBC_WIDTH),
                 dt_bias.reshape(1, SSM_HEADS), dt_bias.reshape(SSM_HEADS, 1),
                 a_log.reshape(1, SSM_HEADS), a_log.reshape(SSM_HEADS, 1),
                 jnp.repeat(d_skip, SSM_HEADDIM).reshape(1, SSM_WIDTH), ssm_norm_w.reshape(1, SSM_WIDTH), b, t)

    wr = jnp.concatenate([w_router_group, w_router_expert,
                          jnp.zeros((d, LANES - N_EXPERT_GROUPS - N_EXPERTS), f32)], axis=1)
    br = jnp.concatenate([b_router_group, b_router_expert,
                          jnp.zeros((LANES - N_EXPERT_GROUPS - N_EXPERTS,), f32)])[None, :]
    wr_hi = wr.astype(bf16)
    wr = jnp.concatenate([wr_hi, (wr - wr_hi.astype(f32)).astype(bf16)], axis=1)
    wo = w_out.astype(bf16)
    h, route, counts = _outproj(y_nsa, y_ssm, x2, wo[:NSA_WIDTH], wo[NSA_WIDTH:], ln1_g[None, :], ln1_b[None, :],
                                wr, br, alpha)

    cnt = counts[0, :N_EXPERTS].astype(i32)
    padded = (cnt + MOE_TM - 1) // MOE_TM * MOE_TM
    pad_ends = jnp.cumsum(padded)
    pad_starts = pad_ends - padded
    e01 = route[:, 0:2].astype(i32).T
    start01 = jnp.sum(jnp.where(e01[..., None] == jnp.arange(N_EXPERTS, dtype=i32), pad_starts, 0), axis=-1)
    dest = start01 + route[:, 4:6].astype(i32).T
    p_rows = 2 * n + N_EXPERTS * MOE_TM
    nb = p_rows // MOE_TM
    block_e = jnp.minimum(jnp.sum(jnp.arange(nb, dtype=i32)[:, None] * MOE_TM >= pad_ends[None, :], axis=-1),
                          N_EXPERTS - 1).astype(i32)
    n_used = (pad_ends[-1] // MOE_TM).astype(i32).reshape(1)
    blk = jnp.arange(nb, dtype=i32)
    last_of_expert = jnp.any((blk[:, None] + 1) * MOE_TM == pad_ends[None, :], axis=-1)
    zflag = (last_of_expert | (blk >= n_used[0])).astype(i32)
    block_e = jnp.where(blk < n_used[0], block_e, block_e[jnp.maximum(n_used[0] - 1, 0)])

    xs = _dispatch(_tile_dest(dest, DISP_TM), zflag, h, p_rows)
    prev_e = jnp.concatenate([jnp.full((1,), -1, i32), block_e[:-1]])
    first = ((block_e != prev_e) & (blk < n_used[0])).astype(i32)
    eidx = jnp.arange(N_EXPERTS, dtype=i32)
    later = jnp.where((eidx[None, :] > eidx[:, None]) & (padded[None, :] > 0), eidx[None, :], N_EXPERTS)
    next_of = jnp.min(later, axis=1)
    next_e = jnp.where(next_of[block_e] < N_EXPERTS, next_of[block_e], -1).astype(i32)
    slot = ((jnp.cumsum(first) - 1) % 2).astype(i32)
    y = _experts(block_e, n_used, first, next_e, slot, xs, w_gate, w_up, w_down)
    out = _combine(_tile_dest(dest, COMB_TM), route, h, ln2_g[None, :], ln2_b[None, :], y, alpha)
    return out.reshape(b, t, d)


def kernel(x, positions, w_in, cmp_k_pe, cmp_k_w1, cmp_k_b1, cmp_k_w2, cmp_v_pe, cmp_v_w1, cmp_v_b1, cmp_v_w2, conv_w, conv_b, dt_bias, a_log, d_skip, ssm_norm_w, w_out, ln1_g, ln1_b, w_router_group, b_router_group, w_router_expert, b_router_expert, w_gate, w_up, w_down, ln2_g, ln2_b):
    depth = w_in.shape[0]
    alpha = (2 * depth) ** 0.25
    params = (w_in, cmp_k_pe, cmp_k_w1, cmp_k_b1, cmp_k_w2, cmp_v_pe, cmp_v_w1, cmp_v_b1, cmp_v_w2, conv_w, conv_b,
              dt_bias, a_log, d_skip, ssm_norm_w, w_out, ln1_g, ln1_b, w_router_group, b_router_group,
              w_router_expert, b_router_expert, w_gate, w_up, w_down, ln2_g, ln2_b)
    for l in range(depth):
        x = _layer(x, positions, *[p[l] for p in params], alpha)
    return x
```

```python
import functools

import jax
import jax.numpy as jnp
import numpy as np
from jax import lax
from jax.experimental import pallas as pl
from jax.experimental.pallas import tpu as pltpu

f32 = jnp.float32
bf16 = jnp.bfloat16
i32 = jnp.int32

HEAD_DIM = 64
NSA_HEADS = 16
NSA_KV_GROUPS = 2
NSA_REP = NSA_HEADS // NSA_KV_GROUPS
NSA_WIDTH = NSA_HEADS * HEAD_DIM
KV_WIDTH = NSA_KV_GROUPS * HEAD_DIM
CMP_BLOCK = 32
CMP_STRIDE = 16
CMP_HIDDEN = 256
SEL_BLOCK = 64
SEL_TOPK = 16
WINDOW = 512
FORCED_SCORE = 1.0e4
SSM_HEADDIM = 64
SSM_HEADS = 16
SSM_WIDTH = SSM_HEADS * SSM_HEADDIM
SSM_GROUPS = 4
SSM_REP = SSM_HEADS // SSM_GROUPS
SSM_STATE = 128
CONV_WIDTH = 4
CHUNK = 256
XBC_WIDTH = SSM_WIDTH + 2 * SSM_GROUPS * SSM_STATE
ROPE_THETA = 500000.0
ROT_DIM = HEAD_DIM // 4
N_EXPERT_GROUPS = 4
EXPERTS_PER_GROUP = 8
N_EXPERTS = N_EXPERT_GROUPS * EXPERTS_PER_GROUP
NORM_EPS = 1e-5

LANES = 128
V7X_VMEM_BYTES = 64 * 1024 * 1024
MASK_NEG = -1.0e30

PROJ_TM = 1024
PROJ_TN = 1024
PREP_TM = 512
CMP_TQ = 256
ATT_TQ = 256
ATT_TK = 512
SEL_UNROLL = 4
WIN_TK = 256
WIN_HEADS = 2
KX_WIDTH = 2 * LANES
OUT_TM = 512
MOE_TM = 256
DISP_TM = 256
COMB_TM = 128
VMEM_LIMIT = V7X_VMEM_BYTES - 8 * 1024 * 1024

COL_Q = 0
COL_Z = NSA_WIDTH
COL_XBC = COL_Z + SSM_WIDTH
COL_KV = COL_XBC + XBC_WIDTH
COL_SMALL = COL_KV + 6 * KV_WIDTH
PROJ_COLS = -(-(COL_SMALL + LANES) // PROJ_TN) * PROJ_TN
SMALL_DT_OFF = 3 * NSA_HEADS


def _cparams(sem, vmem=VMEM_LIMIT):
    return pltpu.CompilerParams(dimension_semantics=sem, vmem_limit_bytes=vmem)


def _dot(a, b):
    return jnp.dot(a, b, preferred_element_type=f32)


def _dot_t(a, b):
    return lax.dot_general(a, b, (((1,), (1,)), ((), ())), preferred_element_type=f32)


def _dot_hi(a, b):
    return jnp.dot(a, b, preferred_element_type=f32, precision=lax.Precision.HIGHEST)


def _proj_kernel(x_ref, w_ref, o_ref, xb_ref):
    @pl.when(pl.program_id(1) == 0)
    def _():
        xb_ref[...] = x_ref[...].astype(bf16)

    o_ref[...] = _dot(xb_ref[...], w_ref[...])


def _proj(x2, w_cat):
    n, d = x2.shape
    cols = w_cat.shape[1]
    tm = min(PROJ_TM, n)
    return pl.pallas_call(
        _proj_kernel,
        grid=(n // tm, cols // PROJ_TN),
        in_specs=[pl.BlockSpec((tm, d), lambda i, j: (i, 0)),
                  pl.BlockSpec((d, PROJ_TN), lambda i, j: (0, j))],
        out_specs=pl.BlockSpec((tm, PROJ_TN), lambda i, j: (i, j)),
        out_shape=jax.ShapeDtypeStruct((n, cols), f32),
        scratch_shapes=[pltpu.VMEM((tm, d), bf16)],
        compiler_params=_cparams(("parallel", "arbitrary")),
        name="proj",
    )(x2, w_cat)


def _rope_tables(pos_i32, invf):
    ang = pos_i32.astype(f32) * invf
    return jnp.cos(ang), jnp.sin(ang)


def _rope128(x, cos, sin):
    half = ROT_DIM // 2
    d = lax.broadcasted_iota(i32, x.shape, 1) % HEAD_DIM
    up = pltpu.roll(x, LANES - half, 1)
    dn = pltpu.roll(x, half, 1)
    rot = jnp.where(d < half, -up, dn)
    return x * cos + rot * sin


def _nsa_prep_kernel(pos_ref, invf_ref, q_ref, kc_ref, vc_ref, ks_ref, vs_ref, kw_ref, vw_ref,
                     qo_ref, cmp_ref, kso_ref, vso_ref, kwo_ref, vwo_ref):
    cos, sin = _rope_tables(pos_ref[...], invf_ref[...])
    scale = HEAD_DIM ** -0.5
    for c in range(NSA_WIDTH // LANES):
        t = _rope128(q_ref[:, c * LANES:(c + 1) * LANES], cos, sin) * scale
        qo_ref[0, 2 * c] = t[:, :HEAD_DIM].astype(bf16)
        qo_ref[0, 2 * c + 1] = t[:, HEAD_DIM:].astype(bf16)

    tm = pos_ref.shape[0]

    half_rows = tm // CMP_STRIDE
    for kind, src in enumerate((kc_ref, vc_ref)):
        for j in range(CMP_STRIDE):
            tok_j = src[pl.ds(j, half_rows, stride=CMP_STRIDE), :]
            for g in range(NSA_KV_GROUPS):
                cmp_ref[kind, 0, g, :, j * HEAD_DIM:(j + 1) * HEAD_DIM] = tok_j[:, g * HEAD_DIM:(g + 1) * HEAD_DIM]

    def values_with_ones(src, dst):
        t = src[...]
        tail = jnp.ones((tm, HEAD_DIM), bf16)
        for g in range(NSA_KV_GROUPS):
            vg = t[:, g * HEAD_DIM:(g + 1) * HEAD_DIM].astype(bf16)
            dst[0, g] = jnp.concatenate([vg, tail], axis=1)

    values_with_ones(vs_ref, vso_ref)
    values_with_ones(vw_ref, vwo_ref)

    kw_t = _rope128(kw_ref[...], cos, sin).T
    ks_t = _rope128(ks_ref[...], cos, sin).T
    for g in range(NSA_KV_GROUPS):
        for c in range(tm // WIN_TK):
            kwo_ref[0, g, c] = kw_t[g * HEAD_DIM:(g + 1) * HEAD_DIM, c * WIN_TK:(c + 1) * WIN_TK].astype(bf16)
    blk = lax.broadcasted_iota(i32, (LANES, tm), 0)
    tok = pl.program_id(1) * tm + lax.broadcasted_iota(i32, (LANES, tm), 1)
    onehot_t = jnp.where(tok // SEL_BLOCK == blk, 1.0, 0.0).astype(bf16)
    for g in range(NSA_KV_GROUPS):
        kso_ref[0, g, 0, 0:LANES, :] = onehot_t
        kg = ks_t[g * HEAD_DIM:(g + 1) * HEAD_DIM, :].astype(bf16)
        kso_ref[0, g, 0, LANES:2 * LANES, :] = jnp.concatenate([kg, jnp.zeros_like(kg)], axis=0)


def _nsa_prep(proj, pos128, invf, b, t):
    tm = PREP_TM
    assert tm == ATT_TK
    nt = t // tm
    row = lambda bi, ti: (bi * nt + ti, 0)
    kv0 = COL_KV // LANES
    in_specs = [pl.BlockSpec((tm, LANES), row),
                pl.BlockSpec((1, LANES), lambda bi, ti: (0, 0)),
                pl.BlockSpec((tm, NSA_WIDTH), lambda bi, ti: (bi * nt + ti, COL_Q // NSA_WIDTH))]
    for k in range(6):
        in_specs.append(pl.BlockSpec((tm, LANES), functools.partial(lambda bi, ti, k: (bi * nt + ti, kv0 + k), k=k)))
    head = lambda bi, ti: (bi, 0, ti, 0)
    tile5 = lambda bi, ti: (bi, 0, ti, 0, 0)
    g = NSA_KV_GROUPS
    out_specs = [pl.BlockSpec((1, NSA_HEADS, tm, HEAD_DIM), head),
                 pl.BlockSpec((2, 1, g, tm // CMP_STRIDE, CMP_STRIDE * HEAD_DIM), lambda bi, ti: (0, bi, 0, ti, 0)),
                 pl.BlockSpec((1, g, 1, KX_WIDTH, tm), tile5), pl.BlockSpec((1, g, tm, LANES), head),
                 pl.BlockSpec((1, g, tm // WIN_TK, HEAD_DIM, WIN_TK), tile5), pl.BlockSpec((1, g, tm, LANES), head)]
    out_shape = [jax.ShapeDtypeStruct((b, NSA_HEADS, t, HEAD_DIM), bf16),
                 jax.ShapeDtypeStruct((2, b, g, t // CMP_STRIDE, CMP_STRIDE * HEAD_DIM), f32),
                 jax.ShapeDtypeStruct((b, g, nt, KX_WIDTH, tm), bf16), jax.ShapeDtypeStruct((b, g, t, LANES), bf16),
                 jax.ShapeDtypeStruct((b, g, t // WIN_TK, HEAD_DIM, WIN_TK), bf16),
                 jax.ShapeDtypeStruct((b, g, t, LANES), bf16)]
    return pl.pallas_call(
        _nsa_prep_kernel,
        grid=(b, nt),
        in_specs=in_specs,
        out_specs=out_specs,
        out_shape=out_shape,
        compiler_params=_cparams(("parallel", "parallel")),
        name="nsa_prep",
    )(pos128, invf, proj, proj, proj, proj, proj, proj, proj)


def _cmp_mlp_kernel(a_ref, pe_ref, w1_ref, b1_ref, w2_ref, pos_ref, invf_ref, o_ref, ot_ref):
    kind = pl.program_id(0)
    a = a_ref[0, 0]
    nc = a.shape[0]
    u = _dot((a + pe_ref[0, 0]).astype(bf16), w1_ref[0, 0])
    v = _dot((a + pe_ref[0, 1]).astype(bf16), w1_ref[0, 1])
    v_next = pltpu.roll(v, nc - 1, 0)
    hid = jax.nn.gelu(u + v_next + b1_ref[0])
    out = _dot(hid.astype(bf16), w2_ref[0])
    cos, sin = _rope_tables(pos_ref[0], invf_ref[...])
    roped = _rope128(out, cos, sin)
    out = jnp.where(kind == 0, roped, out)
    o_ref[0, 0] = out[:, :HEAD_DIM].astype(bf16)
    ot_ref[0, 0] = out.T[:HEAD_DIM, :].astype(bf16)


def _cmp_mlp(a, pe, w1, b1, w2, posc, invf, b):
    _, bg, nc, hw = a.shape
    g = bg // b
    return pl.pallas_call(
        _cmp_mlp_kernel,
        grid=(2, bg),
        in_specs=[pl.BlockSpec((1, 1, nc, hw), lambda k, i: (k, i, 0, 0)),
                  pl.BlockSpec((1, 2, 1, hw), lambda k, i: (k, 0, 0, 0)),
                  pl.BlockSpec((1, 2, hw, CMP_HIDDEN), lambda k, i: (k, 0, 0, 0)),
                  pl.BlockSpec((1, 1, CMP_HIDDEN), lambda k, i: (k, 0, 0)),
                  pl.BlockSpec((1, CMP_HIDDEN, LANES), lambda k, i: (k, 0, 0)),
                  pl.BlockSpec((1, nc, LANES), lambda k, i: (i // g, 0, 0)),
                  pl.BlockSpec((1, LANES), lambda k, i: (0, 0))],
        out_specs=[pl.BlockSpec((1, 1, nc, HEAD_DIM), lambda k, i: (k, i, 0, 0)),
                   pl.BlockSpec((1, 1, HEAD_DIM, nc), lambda k, i: (k, i, 0, 0))],
        out_shape=[jax.ShapeDtypeStruct((2, bg, nc, HEAD_DIM), bf16),
                   jax.ShapeDtypeStruct((2, bg, HEAD_DIM, nc), bf16)],
        compiler_params=_cparams(("parallel", "parallel")),
        name="cmp_mlp",
    )(a, pe, w1, b1, w2, posc, invf)


def _cmp_attn_kernel(q_ref, kct_ref, vc_ref, cover_ref, oc_ref, sel_ref):
    qi = pl.program_id(2)
    nc = vc_ref.shape[2]
    rows = NSA_REP * CMP_TQ
    tq = qi * CMP_TQ + lax.broadcasted_iota(i32, (CMP_TQ, 1), 0)
    row_live = jnp.where(tq >= CMP_BLOCK - 1, 1.0, 0.0)
    tiny = jnp.finfo(f32).tiny

    def attend(ncols):
        kct = kct_ref[0, 0, :, :ncols]
        vc = vc_ref[0, 0, :ncols, :]
        cend = lax.broadcasted_iota(i32, (1, ncols), 1) * CMP_STRIDE + (CMP_BLOCK - 1)
        bias = jnp.where(cend <= tq, 0.0, MASK_NEG)
        s = _dot(q_ref[0].reshape(rows, HEAD_DIM), kct).reshape(NSA_REP, CMP_TQ, ncols) + bias[None]
        e = jnp.exp(s - jnp.max(s, axis=-1, keepdims=True))
        live = row_live[None]
        p = e * (live / jnp.maximum(live * jnp.sum(e, axis=-1, keepdims=True), tiny))
        oc_ref[0] = _dot(p.reshape(rows, ncols).astype(bf16), vc).reshape(NSA_REP, CMP_TQ, HEAD_DIM)
        psum = jnp.sum(p, axis=0)
        hi = psum.astype(bf16)
        lo = (psum - hi.astype(f32)).astype(bf16)
        cover = cover_ref[:ncols, :]
        imp = _dot(hi, cover) + _dot(lo, cover)

        j = lax.broadcasted_iota(i32, (CMP_TQ, LANES), 1)
        cur = tq // SEL_BLOCK
        forced = (j == 0) | (j == cur) | (j == cur - 1)
        valid = j * SEL_BLOCK <= tq
        imp = jnp.where(valid, jnp.where(forced, FORCED_SCORE, imp), -FORCED_SCORE)

        nblk = ncols * CMP_STRIDE // SEL_BLOCK
        jt = lax.broadcasted_iota(i32, (nblk, CMP_TQ), 0)

        def pick(_, carry):
            work, sel = carry
            m = jnp.max(work, axis=0, keepdims=True)
            first = jnp.min(jnp.where(work == m, jt, LANES), axis=0, keepdims=True)
            hit = jt == first
            return jnp.where(hit, -jnp.inf, work), jnp.where(hit, 1.0, sel)

        _, sel_t = lax.fori_loop(0, SEL_TOPK, pick, (imp.T[:nblk], jnp.zeros((nblk, CMP_TQ), f32)))
        if nblk < LANES:
            sel_t = jnp.concatenate([sel_t, jnp.zeros((LANES - nblk, CMP_TQ), f32)], axis=0)
        sel_ref[0, 0] = jnp.where(valid, jnp.where(sel_t.T > 0.0, 0.0, MASK_NEG), MASK_NEG).astype(bf16)

    need = (qi + 1) * (CMP_TQ // CMP_STRIDE)
    for ncols in range(LANES, nc + 1, LANES):
        @pl.when((need > ncols - LANES) & (need <= ncols))
        def _():
            attend(ncols)


def _cmp_attn(q_r, kvc, kvc_t, cover, b, t):
    g = NSA_KV_GROUPS
    nc = kvc.shape[2]
    nq = t // CMP_TQ
    return pl.pallas_call(
        _cmp_attn_kernel,
        grid=(b, g, nq),
        in_specs=[pl.BlockSpec((1, NSA_REP, CMP_TQ, HEAD_DIM), lambda bi, gi, qi: (bi, gi, qi, 0)),
                  pl.BlockSpec((1, 1, HEAD_DIM, nc), lambda bi, gi, qi: (0, bi * g + gi, 0, 0)),
                  pl.BlockSpec((1, 1, nc, HEAD_DIM), lambda bi, gi, qi: (1, bi * g + gi, 0, 0)),
                  pl.BlockSpec((nc, LANES), lambda bi, gi, qi: (0, 0))],
        out_specs=[pl.BlockSpec((1, NSA_REP, CMP_TQ, HEAD_DIM), lambda bi, gi, qi: (bi, gi, qi, 0)),
                   pl.BlockSpec((1, 1, CMP_TQ, LANES), lambda bi, gi, qi: (bi, gi, qi, 0))],
        out_shape=[jax.ShapeDtypeStruct((b, NSA_HEADS, t, HEAD_DIM), f32),
                   jax.ShapeDtypeStruct((b, g, t, LANES), bf16)],
        compiler_params=_cparams(("parallel", "parallel", "parallel")),
        name="cmp_attn",
    )(q_r, kvc_t, kvc, cover)


def _nsa_attn_kernel(q_ref, kx_ref, vs_ref, kw_ref, vw_ref, sel_ref, oc_ref, gate_ref,
                     o_ref, qx_ref, s_ref, m_ref, acc_ref, yw_ref):
    gi = pl.program_id(1)
    qi = pl.program_id(2)
    start = qi * ATT_TQ
    tq = start + lax.broadcasted_iota(i32, (ATT_TQ, 1), 0)

    selb = sel_ref[0, 0]
    for r in range(NSA_REP):
        qx_ref[r * ATT_TQ:(r + 1) * ATT_TQ, 0:LANES] = selb
        qr = q_ref[0, r]
        qx_ref[r * ATT_TQ:(r + 1) * ATT_TQ, LANES:2 * LANES] = jnp.concatenate([qr, jnp.zeros_like(qr)], axis=1)
    rows = NSA_REP * ATT_TQ

    m_ref[...] = jnp.full(m_ref.shape, MASK_NEG, f32)
    acc_ref[...] = jnp.zeros(acc_ref.shape, f32)

    def consume(kt, v_ref, bias, parts=1):
        k0 = pl.multiple_of(kt * ATT_TK, ATT_TK)
        v = v_ref[0, 0, pl.ds(k0, ATT_TK), :]
        heads = NSA_REP // parts
        for part in range(parts):
            rs = slice(part * heads * ATT_TQ, (part + 1) * heads * ATT_TQ)
            s = s_ref[rs]
            if bias is not None:
                s = (s.reshape(heads, ATT_TQ, ATT_TK) + bias[None]).reshape(heads * ATT_TQ, ATT_TK)
            m_prev = m_ref[rs]
            m_new = jnp.maximum(m_prev, jnp.max(s, axis=-1, keepdims=True))
            alpha = jnp.exp(m_prev - m_new)
            p = jnp.exp(s - jnp.concatenate([m_new] * (ATT_TK // LANES), axis=1))
            acc_ref[rs] = alpha * acc_ref[rs] + _dot(p.astype(bf16), v)
            m_ref[rs] = m_new

    def kpos(kt):
        return kt * ATT_TK + lax.broadcasted_iota(i32, (1, ATT_TK), 1)

    last = start // ATT_TK
    for part in range(2):
        rs = slice(part * rows // 2, (part + 1) * rows // 2)
        s_ref[rs] = _dot(qx_ref[rs], kx_ref[0, 0, 0])

    def sel_step(kt, _):
        consume(kt, vs_ref, None)
        s_ref[...] = _dot(qx_ref[...], kx_ref[0, 0, kt + 1])
        return 0

    n_win = (WINDOW + ATT_TQ) // WIN_TK
    w_first = jnp.maximum(start // WIN_TK - WINDOW // WIN_TK, 0)
    kw = jnp.concatenate([kw_ref[0, 0, w_first + i] for i in range(n_win)], axis=1)
    w0 = pl.multiple_of(w_first * WIN_TK, WIN_TK)
    vw = vw_ref[0, 0, pl.ds(w0, n_win * WIN_TK), :]
    kp = w0 + lax.broadcasted_iota(i32, (1, n_win * WIN_TK), 1)
    wbias = jnp.where(kp <= tq, jnp.where(kp > tq - WINDOW, 0.0, MASK_NEG), MASK_NEG)
    hh = WIN_HEADS
    gates = jax.nn.sigmoid(gate_ref[...])
    per_group = 3 * NSA_REP
    shifted = gates
    for gg in range(1, NSA_KV_GROUPS):
        shifted = jnp.where(gi == gg, pltpu.roll(gates, LANES - gg * per_group, 1), shifted)

    def gate(r, br):
        c = r * 3 + br
        return jnp.broadcast_to(shifted[:, c:c + 1], (ATT_TQ, HEAD_DIM))

    upper = pltpu.roll(shifted, HEAD_DIM, 1)[:, HEAD_DIM:]

    def gated_output(a, r, br):
        c = r * 3 + br
        ratio = upper / a[:, HEAD_DIM:]
        return a[:, :HEAD_DIM] * jnp.broadcast_to(ratio[:, c:c + 1], (ATT_TQ, HEAD_DIM))

    for half in range(NSA_REP // hh):
        qh = q_ref[0, half * hh:(half + 1) * hh].reshape(hh * ATT_TQ, HEAD_DIM)
        sw = _dot(qh, kw).reshape(hh, ATT_TQ, n_win * WIN_TK) + wbias[None]
        sw = sw.reshape(hh * ATT_TQ, n_win * WIN_TK)
        pw = jnp.exp(sw - jnp.max(sw, axis=-1, keepdims=True))
        aw = _dot(pw.astype(bf16), vw)
        for i in range(hh):
            r = half * hh + i
            a = aw[i * ATT_TQ:(i + 1) * ATT_TQ]
            yw_ref[r] = gated_output(a, r, 2)

    def sel_group(j, _):
        for i in range(SEL_UNROLL):
            sel_step(SEL_UNROLL * j + i, 0)
        return 0

    lax.fori_loop(0, last // SEL_UNROLL, sel_group, 0)
    lax.fori_loop(last // SEL_UNROLL * SEL_UNROLL, last, sel_step, 0)

    consume(last, vs_ref, jnp.where(kpos(last) <= tq, 0.0, MASK_NEG), parts=2)

    for r in range(NSA_REP):
        osel = gated_output(acc_ref[r * ATT_TQ:(r + 1) * ATT_TQ], r, 1)
        o_ref[:, r * HEAD_DIM:(r + 1) * HEAD_DIM] = (gate(r, 0) * oc_ref[0, r] + osel + yw_ref[r]).astype(o_ref.dtype)


def _nsa_attn(q_r, kx, vs, kw, vw, selb, oc, proj, b, t):
    g = NSA_KV_GROUPS
    nq = t // ATT_TQ
    once = dict(pipeline_mode=pl.Buffered(1))
    vspec = pl.BlockSpec((1, 1, t, LANES), lambda bi, gi, qi: (bi, gi, 0, 0), **once)
    hspec = pl.BlockSpec((1, NSA_REP, ATT_TQ, HEAD_DIM), lambda bi, gi, qi: (bi, gi, qi, 0))
    return pl.pallas_call(
        _nsa_attn_kernel,
        grid=(b, g, nq),
        in_specs=[hspec,
                  pl.BlockSpec((1, 1, t // ATT_TK, KX_WIDTH, ATT_TK), lambda bi, gi, qi: (bi, gi, 0, 0, 0), **once),
                  vspec,
                  pl.BlockSpec((1, 1, t // WIN_TK, HEAD_DIM, WIN_TK), lambda bi, gi, qi: (bi, gi, 0, 0, 0), **once),
                  vspec,
                  pl.BlockSpec((1, 1, ATT_TQ, LANES), lambda bi, gi, qi: (bi, gi, qi, 0)),
                  hspec,
                  pl.BlockSpec((ATT_TQ, LANES), lambda bi, gi, qi: (bi * nq + qi, COL_SMALL // LANES))],
        out_specs=pl.BlockSpec((ATT_TQ, NSA_REP * HEAD_DIM), lambda bi, gi, qi: (bi * nq + qi, gi)),
        out_shape=jax.ShapeDtypeStruct((b * t, NSA_WIDTH), bf16),
        scratch_shapes=[pltpu.VMEM((NSA_REP * ATT_TQ, KX_WIDTH), bf16),
                        pltpu.VMEM((NSA_REP * ATT_TQ, ATT_TK), f32),
                        pltpu.VMEM((NSA_REP * ATT_TQ, LANES), f32),
                        pltpu.VMEM((NSA_REP * ATT_TQ, LANES), f32),
                        pltpu.VMEM((NSA_REP, ATT_TQ, HEAD_DIM), f32)],
        compiler_params=_cparams(("parallel", "parallel", "arbitrary")),
        name="nsa_attn",
    )(q_r, kx, vs, kw, vw, selb, oc, proj)


HALO = 8


def _ssd_kernel(xbc_ref, z_ref, small_ref, dtt_ref, cw_ref, cb_ref, dtb_r_ref, dtb_c_ref,
                alog_r_ref, alog_c_ref, dskip_ref, nw_ref, o_ref, ext_ref, st_ref):
    c = pl.program_id(1)
    L = CHUNK

    @pl.when(c == 0)
    def _():
        ext_ref[0:HALO, :] = jnp.zeros((HALO, XBC_WIDTH), f32)
        st_ref[...] = jnp.zeros_like(st_ref)

    ext_ref[HALO:HALO + L, :] = xbc_ref[...]
    conv = cb_ref[...]
    for k in range(CONV_WIDTH):
        off = HALO - (CONV_WIDTH - 1) + k
        conv = conv + cw_ref[k:k + 1, :] * ext_ref[off:off + L, :]
    ext_ref[0:HALO, :] = ext_ref[L:L + HALO, :]
    act = conv * jax.nn.sigmoid(conv)
    xs = act[:, :SSM_WIDTH]
    bm = act[:, SSM_WIDTH:SSM_WIDTH + SSM_GROUPS * SSM_STATE]
    cm = act[:, SSM_WIDTH + SSM_GROUPS * SSM_STATE:]

    dt_c = jax.nn.softplus(small_ref[:, SMALL_DT_OFF:SMALL_DT_OFF + SSM_HEADS] + dtb_r_ref[...])
    dt_r = jax.nn.softplus(dtt_ref[0] + dtb_c_ref[...])
    a_r = -jnp.exp(alog_r_ref[...])
    a_c = -jnp.exp(alog_c_ref[...])
    row = lax.broadcasted_iota(i32, (L, L), 0)
    col = lax.broadcasted_iota(i32, (L, L), 1)
    causal = col <= row
    tri = jnp.where(causal, 1.0, 0.0)
    acs_c = _dot_hi(tri, dt_c * a_r)
    acs_r = _dot_hi(dt_r * a_c, jnp.where(row <= col, 1.0, 0.0))

    z = z_ref[...]
    for g in range(SSM_GROUPS):
        cg = cm[:, g * SSM_STATE:(g + 1) * SSM_STATE].astype(bf16)
        bg = bm[:, g * SSM_STATE:(g + 1) * SSM_STATE]
        cb = _dot_t(cg, bg.astype(bf16))
        bg_t = bg.T
        ssq = jnp.zeros((L, 1), f32)
        yg = []
        for pp in range(SSM_REP // 2):
            pair = g * (SSM_REP // 2) + pp
            ps = slice(pair * LANES, (pair + 1) * LANES)
            x_p = xs[:, ps]
            low = lax.broadcasted_iota(i32, (L, LANES), 1) < SSM_HEADDIM
            low_n = lax.broadcasted_iota(i32, (SSM_STATE, LANES), 1) < SSM_HEADDIM
            x_own = (jnp.where(low, x_p, 0.0).astype(bf16), jnp.where(low, 0.0, x_p).astype(bf16))
            st = st_ref[pair]
            y = jnp.zeros((L, LANES), f32)
            st_in = jnp.zeros((SSM_STATE, LANES), f32)
            grow, keep = [], []
            for j in range(2):
                h = 2 * pair + j
                a_col = acs_c[:, h:h + 1]
                a_row = acs_r[h:h + 1, :]
                dt_row = dt_r[h:h + 1, :]
                a_last = acs_r[h:h + 1, L - 1:L]
                seg = a_col - a_row
                decay = jnp.where(causal, jnp.exp(jnp.where(causal, seg, 0.0)), 0.0)
                w = cb * decay * dt_row
                y = y + _dot(w.astype(bf16), x_own[j])
                bscaled = bg_t * (jnp.exp(a_last - a_row) * dt_row)
                st_in = st_in + _dot(bscaled.astype(bf16), x_own[j])
                grow.append(jnp.exp(a_col))
                keep.append(jnp.exp(a_last))
            y = y + _dot(cg, st.astype(bf16)) * jnp.where(low, grow[0], grow[1])
            st_ref[pair] = jnp.where(low_n, keep[0], keep[1]) * st + st_in
            y = y + dskip_ref[:, ps] * x_p
            zp = z[:, ps]
            y = y * (zp * jax.nn.sigmoid(zp))
            ssq = ssq + jnp.sum(y * y, axis=-1, keepdims=True)
            yg.append(y)
        rs = lax.rsqrt(ssq / (SSM_REP * SSM_HEADDIM) + NORM_EPS)
        for pp in range(SSM_REP // 2):
            pair = g * (SSM_REP // 2) + pp
            ps = slice(pair * LANES, (pair + 1) * LANES)
            o_ref[:, ps] = (yg[pp] * rs * nw_ref[:, ps]).astype(o_ref.dtype)


def _ssd(proj, dtt, cw, cb, dtb_r, dtb_c, alog_r, alog_c, dskip, nw, b, t):
    nch = t // CHUNK
    row = lambda bi, ci: bi * nch + ci
    const2 = lambda bi, ci: (0, 0)
    return pl.pallas_call(
        _ssd_kernel,
        grid=(b, nch),
        in_specs=[pl.BlockSpec((CHUNK, XBC_WIDTH), lambda bi, ci: (row(bi, ci), COL_XBC // XBC_WIDTH)),
                  pl.BlockSpec((CHUNK, SSM_WIDTH), lambda bi, ci: (row(bi, ci), COL_Z // SSM_WIDTH)),
                  pl.BlockSpec((CHUNK, LANES), lambda bi, ci: (row(bi, ci), COL_SMALL // LANES)),
                  pl.BlockSpec((1, SSM_HEADS, CHUNK), lambda bi, ci: (bi, 0, ci)),
                  pl.BlockSpec((CONV_WIDTH, XBC_WIDTH), const2),
                  pl.BlockSpec((1, XBC_WIDTH), const2),
                  pl.BlockSpec((1, SSM_HEADS), const2),
                  pl.BlockSpec((SSM_HEADS, 1), const2),
                  pl.BlockSpec((1, SSM_HEADS), const2),
                  pl.BlockSpec((SSM_HEADS, 1), const2),
                  pl.BlockSpec((1, SSM_WIDTH), const2),
                  pl.BlockSpec((1, SSM_WIDTH), const2)],
        out_specs=pl.BlockSpec((CHUNK, SSM_WIDTH), lambda bi, ci: (row(bi, ci), 0)),
        out_shape=jax.ShapeDtypeStruct((b * t, SSM_WIDTH), bf16),
        scratch_shapes=[pltpu.VMEM((HALO + CHUNK, XBC_WIDTH), f32),
                        pltpu.VMEM((SSM_HEADS // 2, SSM_STATE, 2 * SSM_HEADDIM), f32)],
        compiler_params=_cparams(("parallel", "arbitrary")),
        name="ssd",
    )(proj, proj, proj, dtt, cw, cb, dtb_r, dtb_c, alog_r, alog_c, dskip, nw)


def _layer_norm(v, g, b):
    mu = jnp.mean(v, axis=-1, keepdims=True)
    d = v - mu
    var = jnp.mean(d * d, axis=-1, keepdims=True)
    return d * lax.rsqrt(var + NORM_EPS) * g + b


def _outproj_kernel(alpha, ya_ref, yb_ref, x_ref, wa_ref, wb_ref, g_ref, b_ref, wr_ref, br_ref,
                    h_ref, route_ref, cnt_ref):
    i = pl.program_id(0)
    tm = x_ref.shape[0]
    mix = _dot(ya_ref[...], wa_ref[...]) + _dot(yb_ref[...], wb_ref[...])
    h = _layer_norm(alpha * x_ref[...] + mix, g_ref[...], b_ref[...])
    h_ref[...] = h

    h_hi = h.astype(bf16)
    h_lo = (h - h_hi.astype(f32)).astype(bf16)
    t = _dot(h_hi, wr_ref[...])
    logits = t[:, :LANES] + t[:, LANES:] + _dot(h_lo, wr_ref[:, :LANES]) + br_ref[...]
    lane = lax.broadcasted_iota(i32, (tm, LANES), 1)
    ninf = -jnp.inf
    gmask = lane < N_EXPERT_GROUPS
    gl = jnp.where(gmask, logits, ninf)
    ge = jnp.where(gmask, jnp.exp(gl - jnp.max(gl, axis=-1, keepdims=True)), 0.0)
    pg = ge / jnp.sum(ge, axis=-1, keepdims=True)
    g_gate = jnp.max(pg, axis=-1, keepdims=True)
    g_sel = jnp.min(jnp.where(gmask & (pg == g_gate), lane, LANES), axis=-1, keepdims=True)
    lo = N_EXPERT_GROUPS + g_sel * EXPERTS_PER_GROUP
    emask = (lane >= lo) & (lane < lo + EXPERTS_PER_GROUP)
    el = jnp.where(emask, logits, ninf)
    ee = jnp.where(emask, jnp.exp(el - jnp.max(el, axis=-1, keepdims=True)), 0.0)
    pe = ee / jnp.sum(ee, axis=-1, keepdims=True)
    p0 = jnp.max(pe, axis=-1, keepdims=True)
    l0 = jnp.min(jnp.where(emask & (pe == p0), lane, LANES), axis=-1, keepdims=True)
    rest = jnp.where(emask & (lane != l0), pe, ninf)
    p1 = jnp.max(rest, axis=-1, keepdims=True)
    l1 = jnp.min(jnp.where(rest == p1, lane, LANES), axis=-1, keepdims=True)
    psum = p0 + p1
    w0 = g_gate * p0 / psum
    w1 = g_gate * p1 / psum
    e0 = l0 - N_EXPERT_GROUPS
    e1 = l1 - N_EXPERT_GROUPS

    @pl.when(i == 0)
    def _():
        cnt_ref[...] = jnp.zeros_like(cnt_ref)

    oh0 = lane == e0
    oh1 = lane == e1
    both = jnp.where(oh0, 1.0, 0.0) + jnp.where(oh1, 1.0, 0.0)
    r_i = lax.broadcasted_iota(i32, (tm, tm), 0)
    c_i = lax.broadcasted_iota(i32, (tm, tm), 1)
    strict = jnp.where(c_i < r_i, 1.0, 0.0).astype(bf16)
    before = _dot(strict, both.astype(bf16)) + cnt_ref[...]
    rank0 = jnp.sum(jnp.where(oh0, before, 0.0), axis=-1, keepdims=True)
    rank1 = jnp.sum(jnp.where(oh1, before, 0.0), axis=-1, keepdims=True)
    cnt_ref[...] = cnt_ref[...] + jnp.sum(both, axis=0, keepdims=True)

    out = jnp.where(lane == 0, e0.astype(f32), 0.0)
    out = jnp.where(lane == 1, e1.astype(f32), out)
    out = jnp.where(lane == 2, w0, out)
    out = jnp.where(lane == 3, w1, out)
    out = jnp.where(lane == 4, rank0, out)
    out = jnp.where(lane == 5, rank1, out)
    route_ref[...] = out


def _outproj(y_nsa, y_ssm, x2, wa, wb, g, bta, wr, br, alpha):
    n, d = x2.shape
    tm = OUT_TM
    const = lambda i: (0, 0)
    rowb = lambda i: (i, 0)
    return pl.pallas_call(
        functools.partial(_outproj_kernel, alpha),
        grid=(n // tm,),
        in_specs=[pl.BlockSpec((tm, NSA_WIDTH), rowb), pl.BlockSpec((tm, SSM_WIDTH), rowb),
                  pl.BlockSpec((tm, d), rowb),
                  pl.BlockSpec((NSA_WIDTH, d), const), pl.BlockSpec((SSM_WIDTH, d), const),
                  pl.BlockSpec((1, d), const), pl.BlockSpec((1, d), const),
                  pl.BlockSpec((d, 2 * LANES), const), pl.BlockSpec((1, LANES), const)],
        out_specs=[pl.BlockSpec((tm, d), rowb), pl.BlockSpec((tm, LANES), rowb),
                   pl.BlockSpec((1, LANES), const)],
        out_shape=[jax.ShapeDtypeStruct((n, d), f32), jax.ShapeDtypeStruct((n, LANES), f32),
                   jax.ShapeDtypeStruct((1, LANES), f32)],
        compiler_params=_cparams(("arbitrary",)),
        name="outproj",
    )(y_nsa, y_ssm, x2, wa, wb, g, bta, wr, br)


def _row_copy(src_ref, src_row, dst_ref, dst_row, sem):
    return pltpu.make_async_copy(src_ref.at[pl.ds(src_row, 1), :], dst_ref.at[pl.ds(dst_row, 1), :], sem)


def _pack_bf16_pairs(x):
    half = x.shape[1] // 2
    hi = pltpu.bitcast(x[:, :half].astype(bf16).astype(f32), jnp.uint32)
    lo = pltpu.bitcast(x[:, half:].astype(bf16).astype(f32), jnp.uint32)
    return hi | (lo >> 16)


def _unpack_bf16_pairs(words):
    return (pltpu.bitcast(words & jnp.uint32(0xFFFF0000), f32), pltpu.bitcast(words << 16, f32))


def _dest_row(dest_ref, tm, k, r):
    return dest_ref[0, 0, k * tm + r]


def _dispatch_kernel(dest_ref, zflag_ref, h_ref, xs_ref, zero_ref, tile_ref, pk_ref, sem, lsem, zsem):
    tm = DISP_TM
    nb = zflag_ref.shape[0]
    i = pl.program_id(0)

    @pl.when(i == 0)
    def _():
        zero_ref[...] = jnp.zeros_like(zero_ref)

        def zblock(i):
            rows = pl.ds(pl.multiple_of(i * MOE_TM, MOE_TM), MOE_TM)
            return pltpu.make_async_copy(zero_ref, xs_ref.at[rows, :], zsem)

        def zstart(i, _):
            @pl.when(zflag_ref[i] != 0)
            def _():
                zblock(i).start()
            return 0

        def zwait(i, _):
            @pl.when(zflag_ref[i] != 0)
            def _():
                zblock(i).wait()
            return 0

        lax.fori_loop(0, nb, zstart, 0)
        lax.fori_loop(0, nb, zwait, 0)

    nslot = tile_ref.shape[0]
    nsteps = pl.num_programs(0)
    slot = i % nslot

    def tile_load(step, s):
        rows = pl.ds(pl.multiple_of(step * tm, tm), tm)
        return pltpu.make_async_copy(h_ref.at[rows, :], tile_ref.at[s], lsem.at[s])

    @pl.when(i == 0)
    def _():
        tile_load(0, 0).start()

    tile_load(i, slot).wait()

    @pl.when(i + 1 < nsteps)
    def _():
        tile_load(i + 1, (i + 1) % nslot).start()

    pk_ref[slot] = _pack_bf16_pairs(tile_ref[slot])

    def issue(r, _):
        for k in range(2):
            _row_copy(pk_ref.at[slot], r, xs_ref, _dest_row(dest_ref, tm, k, r), sem.at[slot]).start()
        return 0

    lax.fori_loop(0, tm, issue, 0, unroll=16)

    def wait_rows(s):
        for k in range(2):
            pltpu.make_async_copy(pk_ref.at[s], xs_ref.at[pl.ds(0, tm), :], sem.at[s]).wait()

    @pl.when(i > 0)
    def _():
        wait_rows((i - 1) % nslot)

    @pl.when(i == nsteps - 1)
    def _():
        wait_rows(slot)


def _dispatch(dest_t, zflag, h, p_rows):
    n, d = h.shape
    tm = DISP_TM
    return pl.pallas_call(
        _dispatch_kernel,
        grid=(n // tm,),
        in_specs=[pl.BlockSpec((1, 1, 2 * tm), lambda i: (i, 0, 0), memory_space=pltpu.SMEM),
                  pl.BlockSpec(memory_space=pltpu.SMEM),
                  pl.BlockSpec(memory_space=pl.ANY)],
        out_specs=pl.BlockSpec(memory_space=pl.ANY),
        out_shape=jax.ShapeDtypeStruct((p_rows, d // 2), jnp.uint32),
        scratch_shapes=[pltpu.VMEM((MOE_TM, d // 2), jnp.uint32), pltpu.VMEM((3, tm, d), f32),
                        pltpu.VMEM((3, tm, d // 2), jnp.uint32),
                        pltpu.SemaphoreType.DMA((3,)), pltpu.SemaphoreType.DMA((3,)), pltpu.SemaphoreType.DMA(())],
        compiler_params=_cparams(("arbitrary",)),
        name="dispatch",
    )(dest_t, zflag, h)


def _experts_kernel(be_ref, nu_ref, first_ref, next_ref, slot_ref, xs_ref, wg_ref, wu_ref, wd_ref, y_ref,
                    fg_ref, fu_ref, fd_ref, wgb_ref, wub_ref, wdb_ref, sems):
    i = pl.program_id(0)

    def weight_loads(e, s):
        return (pltpu.make_async_copy(wg_ref.at[e], fg_ref.at[s], sems.at[s, 0]),
                pltpu.make_async_copy(wu_ref.at[e], fu_ref.at[s], sems.at[s, 1]),
                pltpu.make_async_copy(wd_ref.at[e], fd_ref.at[s], sems.at[s, 2]))

    @pl.when(i == 0)
    def _():
        for c in weight_loads(be_ref[0], 0):
            c.start()

    @pl.when(first_ref[i] != 0)
    def _():
        s = slot_ref[i]
        for c in weight_loads(be_ref[i], s):
            c.wait()
        wgb_ref[...] = fg_ref[s].astype(bf16)
        wub_ref[...] = fu_ref[s].astype(bf16)
        wdb_ref[...] = fd_ref[s].astype(bf16)

        @pl.when(next_ref[i] >= 0)
        def _():
            for c in weight_loads(next_ref[i], 1 - s):
                c.start()

    @pl.when(i < nu_ref[0])
    def _():
        first, second = _unpack_bf16_pairs(xs_ref[...])
        xb = jnp.concatenate([first.astype(bf16), second.astype(bf16)], axis=1)
        gte = _dot(xb, wgb_ref[...])
        up = _dot(xb, wub_ref[...])
        act = gte * jax.nn.sigmoid(gte) * up
        y = _dot(act.astype(bf16), wdb_ref[...])
        y_ref[...] = _pack_bf16_pairs(y)

    @pl.when(i >= nu_ref[0])
    def _():
        y_ref[...] = jnp.zeros_like(y_ref)


def _experts(block_e, n_used, first, next_e, slot, xs, w_gate, w_up, w_down):
    p_rows = xs.shape[0]
    d = w_gate.shape[1]
    tm = MOE_TM
    nb = p_rows // tm
    de = w_gate.shape[-1]
    xmap = lambda i, be, nu, fi, ne, sl: (jnp.maximum(jnp.minimum(i, nu[0] - 1), 0), 0)
    hbm = pl.BlockSpec(memory_space=pl.ANY)
    return pl.pallas_call(
        _experts_kernel,
        grid_spec=pltpu.PrefetchScalarGridSpec(
            num_scalar_prefetch=5,
            grid=(nb,),
            in_specs=[pl.BlockSpec((tm, d // 2), xmap), hbm, hbm, hbm],
            out_specs=pl.BlockSpec((tm, d // 2), lambda i, be, nu, fi, ne, sl: (i, 0)),
            scratch_shapes=[pltpu.VMEM((2, d, de), f32), pltpu.VMEM((2, d, de), f32), pltpu.VMEM((2, de, d), f32),
                            pltpu.VMEM((d, de), bf16), pltpu.VMEM((d, de), bf16), pltpu.VMEM((de, d), bf16),
                            pltpu.SemaphoreType.DMA((2, 3))],
        ),
        out_shape=jax.ShapeDtypeStruct((p_rows, d // 2), jnp.uint32),
        compiler_params=_cparams(("arbitrary",)),
        name="experts",
    )(block_e, n_used, first, next_e, slot, xs, w_gate, w_up, w_down)


def _combine_kernel(alpha, dest_ref, ndest_ref, route_ref, h_ref, g_ref, b_ref, y_ref, o_ref, buf_ref, sems):
    tm = h_ref.shape[0]
    i = pl.program_id(0)
    slot = i % 2

    def start_row(ids_ref, s, r):
        for k in range(2):
            _row_copy(y_ref, _dest_row(ids_ref, tm, k, r), buf_ref.at[s, k], r, sems.at[s]).start()

    def wait_tile():
        for k in range(2):
            pltpu.make_async_copy(y_ref.at[pl.ds(0, tm), :], buf_ref.at[slot, k], sems.at[slot]).wait()

    def finish_tile():
        route = route_ref[...]
        lo0, hi0 = _unpack_bf16_pairs(buf_ref[slot, 0])
        lo1, hi1 = _unpack_bf16_pairs(buf_ref[slot, 1])
        w0, w1 = route[:, 2:3], route[:, 3:4]
        ffn = jnp.concatenate([w0 * lo0 + w1 * lo1, w0 * hi0 + w1 * hi1], axis=1)
        o_ref[...] = _layer_norm(alpha * h_ref[...] + ffn, g_ref[...], b_ref[...])

    @pl.when(i == 0)
    def _():
        def issue(r, _):
            start_row(dest_ref, 0, r)
            return 0
        lax.fori_loop(0, tm, issue, 0, unroll=4)

    @pl.when(i + 1 < pl.num_programs(0))
    def _():
        wait_tile()
        for r in range(tm):
            start_row(ndest_ref, 1 - slot, r)
        finish_tile()

    @pl.when(i + 1 == pl.num_programs(0))
    def _():
        wait_tile()
        finish_tile()


def _combine(dest_t, route, h, g, bta, y, alpha):
    n, d = h.shape
    tm = COMB_TM
    const = lambda i: (0, 0)
    return pl.pallas_call(
        functools.partial(_combine_kernel, alpha),
        grid=(n // tm,),
        in_specs=[pl.BlockSpec((1, 1, 2 * tm), lambda i: (i, 0, 0), memory_space=pltpu.SMEM),
                  pl.BlockSpec((1, 1, 2 * tm), lambda i: (jnp.minimum(i + 1, n // tm - 1), 0, 0),
                               memory_space=pltpu.SMEM),
                  pl.BlockSpec((tm, LANES), lambda i: (i, 0)),
                  pl.BlockSpec((tm, d), lambda i: (i, 0)),
                  pl.BlockSpec((1, d), const), pl.BlockSpec((1, d), const),
                  pl.BlockSpec(memory_space=pl.ANY)],
        out_specs=pl.BlockSpec((tm, d), lambda i: (i, 0)),
        out_shape=jax.ShapeDtypeStruct((n, d), f32),
        scratch_shapes=[pltpu.VMEM((2, 2, tm, d // 2), jnp.uint32), pltpu.SemaphoreType.DMA((2,))],
        compiler_params=_cparams(("arbitrary",)),
        name="combine",
    )(dest_t, dest_t, route, h, g, bta, y)


def _tile_dest(dest, tm):
    n = dest.shape[1]
    return dest.reshape(2, n // tm, tm).transpose(1, 0, 2).reshape(n // tm, 1, 2 * tm)


def _layer(x, positions, w_in, cmp_k_pe, cmp_k_w1, cmp_k_b1, cmp_k_w2, cmp_v_pe, cmp_v_w1, cmp_v_b1, cmp_v_w2,
           conv_w, conv_b, dt_bias, a_log, d_skip, ssm_norm_w, w_out, ln1_g, ln1_b,
           w_router_group, b_router_group, w_router_expert, b_router_expert, w_gate, w_up, w_down, ln2_g, ln2_b,
           alpha):
    b, t, d = x.shape
    n = b * t
    assert t % max(ATT_TK, ATT_TQ, CMP_TQ, CHUNK, PREP_TM) == 0 and n % max(PROJ_TM, OUT_TM, DISP_TM) == 0
    assert t // SEL_BLOCK <= LANES and (t // CMP_STRIDE) % LANES == 0 and t >= WINDOW + ATT_TQ
    x2 = x.reshape(n, d)

    c0 = NSA_WIDTH
    c1 = c0 + 6 * KV_WIDTH
    c2 = c1 + 3 * NSA_HEADS
    c3 = c2 + SSM_WIDTH
    c4 = c3 + XBC_WIDTH
    w_small = jnp.concatenate([w_in[:, c1:c2], w_in[:, c4:], jnp.zeros((d, LANES - 3 * NSA_HEADS - SSM_HEADS), f32)], axis=1)
    w_cat = jnp.concatenate([w_in[:, :c0], w_in[:, c2:c3], w_in[:, c3:c4], w_in[:, c0:c1], w_small,
                             jnp.zeros((d, PROJ_COLS - COL_SMALL - LANES), f32)], axis=1).astype(bf16)
    proj = _proj(x2, w_cat)

    lane = np.arange(LANES) % HEAD_DIM
    inv_freq = ROPE_THETA ** (-jnp.arange(0, ROT_DIM, 2, dtype=f32) / ROT_DIM)
    invf = jnp.where(lane < ROT_DIM, inv_freq[lane % (ROT_DIM // 2)], 0.0).astype(f32)[None, :]
    pos128 = jnp.broadcast_to(positions.reshape(n, 1), (n, LANES))
    q_r, cmp_in, k_sel, v_sel, k_win, v_win = _nsa_prep(proj, pos128, invf, b, t)

    nc = t // CMP_STRIDE
    half_w = CMP_STRIDE * HEAD_DIM
    a = cmp_in.reshape(2, b * NSA_KV_GROUPS, nc, half_w)
    pe = jnp.stack([cmp_k_pe, cmp_v_pe]).reshape(2, 2, 1, half_w)
    w1 = jnp.stack([cmp_k_w1, cmp_v_w1]).reshape(2, 2, half_w, CMP_HIDDEN).astype(bf16)
    b1 = jnp.stack([cmp_k_b1, cmp_v_b1]).reshape(2, 1, CMP_HIDDEN)
    w2 = jnp.pad(jnp.stack([cmp_k_w2, cmp_v_w2]), ((0, 0), (0, 0), (0, LANES - HEAD_DIM))).astype(bf16)
    cend = jnp.minimum(jnp.arange(nc) * CMP_STRIDE + CMP_BLOCK - 1, t - 1)
    posc = jnp.broadcast_to(positions[:, cend][:, :, None], (b, nc, LANES))
    kvc, kvc_t = _cmp_mlp(a, pe, w1, b1, w2, posc, invf, b)

    c_start = np.arange(nc)[:, None] * CMP_STRIDE
    s_start = np.arange(LANES)[None, :] * SEL_BLOCK
    cover = ((c_start < s_start + SEL_BLOCK) & (c_start + CMP_BLOCK > s_start)
             & (np.arange(nc)[:, None] < nc - 1) & (np.arange(LANES)[None, :] < t // SEL_BLOCK))
    cover = jnp.asarray(cover, bf16)
    o_cmp, selb = _cmp_attn(q_r, kvc, kvc_t, cover, b, t)

    y_nsa = _nsa_attn(q_r, k_sel, v_sel, k_win, v_win, selb, o_cmp, proj, b, t)

    dt_raw = proj[:, COL_SMALL + SMALL_DT_OFF:COL_SMALL + SMALL_DT_OFF + SSM_HEADS]
    dtt = dt_raw.reshape(b, t, SSM_HEADS).transpose(0, 2, 1)
    y_ssm = _ssd(proj, dtt, conv_w.reshape(CONV_WIDTH, XBC_WIDTH), conv_b.reshape(1, XBC_WIDTH),
                 dt_bias.reshape(1, SSM_HEADS), dt_bias.reshape(SSM_HEADS, 1),
                 a_log.reshape(1, SSM_HEADS), a_log.reshape(SSM_HEADS, 1),
                 jnp.repeat(d_skip, SSM_HEADDIM).reshape(1, SSM_WIDTH), ssm_norm_w.reshape(1, SSM_WIDTH), b, t)

    wr = jnp.concatenate([w_router_group, w_router_expert,
                          jnp.zeros((d, LANES - N_EXPERT_GROUPS - N_EXPERTS), f32)], axis=1)
    br = jnp.concatenate([b_router_group, b_router_expert,
                          jnp.zeros((LANES - N_EXPERT_GROUPS - N_EXPERTS,), f32)])[None, :]
    wr_hi = wr.astype(bf16)
    wr = jnp.concatenate([wr_hi, (wr - wr_hi.astype(f32)).astype(bf16)], axis=1)
    wo = w_out.astype(bf16)
    h, route, counts = _outproj(y_nsa, y_ssm, x2, wo[:NSA_WIDTH], wo[NSA_WIDTH:], ln1_g[None, :], ln1_b[None, :],
                                wr, br, alpha)

    cnt = counts[0, :N_EXPERTS].astype(i32)
    padded = (cnt + MOE_TM - 1) // MOE_TM * MOE_TM
    pad_ends = jnp.cumsum(padded)
    pad_starts = pad_ends - padded
    e01 = route[:, 0:2].astype(i32).T
    start01 = jnp.sum(jnp.where(e01[..., None] == jnp.arange(N_EXPERTS, dtype=i32), pad_starts, 0), axis=-1)
    dest = start01 + route[:, 4:6].astype(i32).T
    p_rows = 2 * n + N_EXPERTS * MOE_TM
    nb = p_rows // MOE_TM
    block_e = jnp.minimum(jnp.sum(jnp.arange(nb, dtype=i32)[:, None] * MOE_TM >= pad_ends[None, :], axis=-1),
                          N_EXPERTS - 1).astype(i32)
    n_used = (pad_ends[-1] // MOE_TM).astype(i32).reshape(1)
    blk = jnp.arange(nb, dtype=i32)
    last_of_expert = jnp.any((blk[:, None] + 1) * MOE_TM == pad_ends[None, :], axis=-1)
    zflag = (last_of_expert | (blk >= n_used[0])).astype(i32)
    block_e = jnp.where(blk < n_used[0], block_e, block_e[jnp.maximum(n_used[0] - 1, 0)])

    xs = _dispatch(_tile_dest(dest, DISP_TM), zflag, h, p_rows)
    prev_e = jnp.concatenate([jnp.full((1,), -1, i32), block_e[:-1]])
    first = ((block_e != prev_e) & (blk < n_used[0])).astype(i32)
    eidx = jnp.arange(N_EXPERTS, dtype=i32)
    later = jnp.where((eidx[None, :] > eidx[:, None]) & (padded[None, :] > 0), eidx[None, :], N_EXPERTS)
    next_of = jnp.min(later, axis=1)
    next_e = jnp.where(next_of[block_e] < N_EXPERTS, next_of[block_e], -1).astype(i32)
    slot = ((jnp.cumsum(first) - 1) % 2).astype(i32)
    y = _experts(block_e, n_used, first, next_e, slot, xs, w_gate, w_up, w_down)
    out = _combine(_tile_dest(dest, COMB_TM), route, h, ln2_g[None, :], ln2_b[None, :], y, alpha)
    return out.reshape(b, t, d)


def kernel(x, positions, w_in, cmp_k_pe, cmp_k_w1, cmp_k_b1, cmp_k_w2, cmp_v_pe, cmp_v_w1, cmp_v_b1, cmp_v_w2, conv_w, conv_b, dt_bias, a_log, d_skip, ssm_norm_w, w_out, ln1_g, ln1_b, w_router_group, b_router_group, w_router_expert, b_router_expert, w_gate, w_up, w_down, ln2_g, ln2_b):
    depth = w_in.shape[0]
    alpha = (2 * depth) ** 0.25
    params = (w_in, cmp_k_pe, cmp_k_w1, cmp_k_b1, cmp_k_w2, cmp_v_pe, cmp_v_w1, cmp_v_b1, cmp_v_w2, conv_w, conv_b,
              dt_bias, a_log, d_skip, ssm_norm_w, w_out, ln1_g, ln1_b, w_router_group, b_router_group,
              w_router_expert, b_router_expert, w_gate, w_up, w_down, ln2_g, ln2_b)
    for l in range(depth):
        x = _layer(x, positions, *[p[l] for p in params], alpha)
    return x
```

```python
import functools

import jax
import jax.numpy as jnp
import numpy as np
from jax import lax
from jax.experimental import pallas as pl
from jax.experimental.pallas import tpu as pltpu

f32 = jnp.float32
bf16 = jnp.bfloat16
i32 = jnp.int32

HEAD_DIM = 64
NSA_HEADS = 16
NSA_KV_GROUPS = 2
NSA_REP = NSA_HEADS // NSA_KV_GROUPS
NSA_WIDTH = NSA_HEADS * HEAD_DIM
KV_WIDTH = NSA_KV_GROUPS * HEAD_DIM
CMP_BLOCK = 32
CMP_STRIDE = 16
CMP_HIDDEN = 256
SEL_BLOCK = 64
SEL_TOPK = 16
WINDOW = 512
FORCED_SCORE = 1.0e4
SSM_HEADDIM = 64
SSM_HEADS = 16
SSM_WIDTH = SSM_HEADS * SSM_HEADDIM
SSM_GROUPS = 4
SSM_REP = SSM_HEADS // SSM_GROUPS
SSM_STATE = 128
CONV_WIDTH = 4
CHUNK = 256
XBC_WIDTH = SSM_WIDTH + 2 * SSM_GROUPS * SSM_STATE
ROPE_THETA = 500000.0
ROT_DIM = HEAD_DIM // 4
N_EXPERT_GROUPS = 4
EXPERTS_PER_GROUP = 8
N_EXPERTS = N_EXPERT_GROUPS * EXPERTS_PER_GROUP
NORM_EPS = 1e-5

LANES = 128
V7X_VMEM_BYTES = 64 * 1024 * 1024
MASK_NEG = -1.0e30

PROJ_TM = 1024
PROJ_TN = 1024
PREP_TM = 512
CMP_TQ = 512
ATT_TQ = 256
ATT_TK = 512
SEL_UNROLL = 4
WIN_TK = 256
WIN_HEADS = 2
KX_WIDTH = 2 * LANES
OUT_TM = 512
MOE_TM = 256
DISP_TM = 256
COMB_TM = 128
VMEM_LIMIT = V7X_VMEM_BYTES - 8 * 1024 * 1024

COL_Q = 0
COL_Z = NSA_WIDTH
COL_XBC = COL_Z + SSM_WIDTH
COL_KV = COL_XBC + XBC_WIDTH
COL_SMALL = COL_KV + 6 * KV_WIDTH
PROJ_COLS = -(-(COL_SMALL + LANES) // PROJ_TN) * PROJ_TN
SMALL_DT_OFF = 3 * NSA_HEADS


def _cparams(sem, vmem=VMEM_LIMIT):
    return pltpu.CompilerParams(dimension_semantics=sem, vmem_limit_bytes=vmem)


def _dot(a, b):
    return jnp.dot(a, b, preferred_element_type=f32)


def _dot_t(a, b):
    return lax.dot_general(a, b, (((1,), (1,)), ((), ())), preferred_element_type=f32)


def _dot_hi(a, b):
    return jnp.dot(a, b, preferred_element_type=f32, precision=lax.Precision.HIGHEST)


def _proj_kernel(x_ref, w_ref, o_ref, xb_ref):
    @pl.when(pl.program_id(1) == 0)
    def _():
        xb_ref[...] = x_ref[...].astype(bf16)

    o_ref[...] = _dot(xb_ref[...], w_ref[...])


def _proj(x2, w_cat):
    n, d = x2.shape
    cols = w_cat.shape[1]
    tm = min(PROJ_TM, n)
    return pl.pallas_call(
        _proj_kernel,
        grid=(n // tm, cols // PROJ_TN),
        in_specs=[pl.BlockSpec((tm, d), lambda i, j: (i, 0)),
                  pl.BlockSpec((d, PROJ_TN), lambda i, j: (0, j))],
        out_specs=pl.BlockSpec((tm, PROJ_TN), lambda i, j: (i, j)),
        out_shape=jax.ShapeDtypeStruct((n, cols), f32),
        scratch_shapes=[pltpu.VMEM((tm, d), bf16)],
        compiler_params=_cparams(("parallel", "arbitrary")),
        name="proj",
    )(x2, w_cat)


def _rope_tables(pos_i32, invf):
    ang = pos_i32.astype(f32) * invf
    return jnp.cos(ang), jnp.sin(ang)


def _rope128(x, cos, sin):
    half = ROT_DIM // 2
    d = lax.broadcasted_iota(i32, x.shape, 1) % HEAD_DIM
    up = pltpu.roll(x, LANES - half, 1)
    dn = pltpu.roll(x, half, 1)
    rot = jnp.where(d < half, -up, dn)
    return x * cos + rot * sin


def _nsa_prep_kernel(pos_ref, invf_ref, q_ref, kc_ref, vc_ref, ks_ref, vs_ref, kw_ref, vw_ref,
                     qo_ref, cmp_ref, kso_ref, vso_ref, kwo_ref, vwo_ref):
    cos, sin = _rope_tables(pos_ref[...], invf_ref[...])
    scale = HEAD_DIM ** -0.5
    for c in range(NSA_WIDTH // LANES):
        t = _rope128(q_ref[:, c * LANES:(c + 1) * LANES], cos, sin) * scale
        qo_ref[0, 2 * c] = t[:, :HEAD_DIM].astype(bf16)
        qo_ref[0, 2 * c + 1] = t[:, HEAD_DIM:].astype(bf16)

    tm = pos_ref.shape[0]

    half_rows = tm // CMP_STRIDE
    for kind, src in enumerate((kc_ref, vc_ref)):
        for j in range(CMP_STRIDE):
            tok_j = src[pl.ds(j, half_rows, stride=CMP_STRIDE), :]
            for g in range(NSA_KV_GROUPS):
                cmp_ref[kind, 0, g, :, j * HEAD_DIM:(j + 1) * HEAD_DIM] = tok_j[:, g * HEAD_DIM:(g + 1) * HEAD_DIM]

    def values_with_ones(src, dst):
        t = src[...]
        tail = jnp.ones((tm, HEAD_DIM), bf16)
        for g in range(NSA_KV_GROUPS):
            vg = t[:, g * HEAD_DIM:(g + 1) * HEAD_DIM].astype(bf16)
            dst[0, g] = jnp.concatenate([vg, tail], axis=1)

    values_with_ones(vs_ref, vso_ref)
    values_with_ones(vw_ref, vwo_ref)

    kw_t = _rope128(kw_ref[...], cos, sin).T
    ks_t = _rope128(ks_ref[...], cos, sin).T
    for g in range(NSA_KV_GROUPS):
        for c in range(tm // WIN_TK):
            kwo_ref[0, g, c] = kw_t[g * HEAD_DIM:(g + 1) * HEAD_DIM, c * WIN_TK:(c + 1) * WIN_TK].astype(bf16)
    blk = lax.broadcasted_iota(i32, (LANES, tm), 0)
    tok = pl.program_id(1) * tm + lax.broadcasted_iota(i32, (LANES, tm), 1)
    onehot_t = jnp.where(tok // SEL_BLOCK == blk, 1.0, 0.0).astype(bf16)
    for g in range(NSA_KV_GROUPS):
        kso_ref[0, g, 0, 0:LANES, :] = onehot_t
        kg = ks_t[g * HEAD_DIM:(g + 1) * HEAD_DIM, :].astype(bf16)
        kso_ref[0, g, 0, LANES:2 * LANES, :] = jnp.concatenate([kg, jnp.zeros_like(kg)], axis=0)


def _nsa_prep(proj, pos128, invf, b, t):
    tm = PREP_TM
    assert tm == ATT_TK
    nt = t // tm
    row = lambda bi, ti: (bi * nt + ti, 0)
    kv0 = COL_KV // LANES
    in_specs = [pl.BlockSpec((tm, LANES), row),
                pl.BlockSpec((1, LANES), lambda bi, ti: (0, 0)),
                pl.BlockSpec((tm, NSA_WIDTH), lambda bi, ti: (bi * nt + ti, COL_Q // NSA_WIDTH))]
    for k in range(6):
        in_specs.append(pl.BlockSpec((tm, LANES), functools.partial(lambda bi, ti, k: (bi * nt + ti, kv0 + k), k=k)))
    head = lambda bi, ti: (bi, 0, ti, 0)
    tile5 = lambda bi, ti: (bi, 0, ti, 0, 0)
    g = NSA_KV_GROUPS
    out_specs = [pl.BlockSpec((1, NSA_HEADS, tm, HEAD_DIM), head),
                 pl.BlockSpec((2, 1, g, tm // CMP_STRIDE, CMP_STRIDE * HEAD_DIM), lambda bi, ti: (0, bi, 0, ti, 0)),
                 pl.BlockSpec((1, g, 1, KX_WIDTH, tm), tile5), pl.BlockSpec((1, g, tm, LANES), head),
                 pl.BlockSpec((1, g, tm // WIN_TK, HEAD_DIM, WIN_TK), tile5), pl.BlockSpec((1, g, tm, LANES), head)]
    out_shape = [jax.ShapeDtypeStruct((b, NSA_HEADS, t, HEAD_DIM), bf16),
                 jax.ShapeDtypeStruct((2, b, g, t // CMP_STRIDE, CMP_STRIDE * HEAD_DIM), f32),
                 jax.ShapeDtypeStruct((b, g, nt, KX_WIDTH, tm), bf16), jax.ShapeDtypeStruct((b, g, t, LANES), bf16),
                 jax.ShapeDtypeStruct((b, g, t // WIN_TK, HEAD_DIM, WIN_TK), bf16),
                 jax.ShapeDtypeStruct((b, g, t, LANES), bf16)]
    return pl.pallas_call(
        _nsa_prep_kernel,
        grid=(b, nt),
        in_specs=in_specs,
        out_specs=out_specs,
        out_shape=out_shape,
        compiler_params=_cparams(("parallel", "parallel")),
        name="nsa_prep",
    )(pos128, invf, proj, proj, proj, proj, proj, proj, proj)


def _cmp_mlp_kernel(a_ref, pe_ref, w1_ref, b1_ref, w2_ref, pos_ref, invf_ref, o_ref, ot_ref):
    kind = pl.program_id(0)
    a = a_ref[0, 0]
    nc = a.shape[0]
    u = _dot((a + pe_ref[0, 0]).astype(bf16), w1_ref[0, 0])
    v = _dot((a + pe_ref[0, 1]).astype(bf16), w1_ref[0, 1])
    v_next = pltpu.roll(v, nc - 1, 0)
    hid = jax.nn.gelu(u + v_next + b1_ref[0])
    out = _dot(hid.astype(bf16), w2_ref[0])
    cos, sin = _rope_tables(pos_ref[0], invf_ref[...])
    roped = _rope128(out, cos, sin)
    out = jnp.where(kind == 0, roped, out)
    o_ref[0, 0] = out[:, :HEAD_DIM].astype(bf16)
    ot_ref[0, 0] = out.T[:HEAD_DIM, :].astype(bf16)


def _cmp_mlp(a, pe, w1, b1, w2, posc, invf, b):
    _, bg, nc, hw = a.shape
    g = bg // b
    return pl.pallas_call(
        _cmp_mlp_kernel,
        grid=(2, bg),
        in_specs=[pl.BlockSpec((1, 1, nc, hw), lambda k, i: (k, i, 0, 0)),
                  pl.BlockSpec((1, 2, 1, hw), lambda k, i: (k, 0, 0, 0)),
                  pl.BlockSpec((1, 2, hw, CMP_HIDDEN), lambda k, i: (k, 0, 0, 0)),
                  pl.BlockSpec((1, 1, CMP_HIDDEN), lambda k, i: (k, 0, 0)),
                  pl.BlockSpec((1, CMP_HIDDEN, LANES), lambda k, i: (k, 0, 0)),
                  pl.BlockSpec((1, nc, LANES), lambda k, i: (i // g, 0, 0)),
                  pl.BlockSpec((1, LANES), lambda k, i: (0, 0))],
        out_specs=[pl.BlockSpec((1, 1, nc, HEAD_DIM), lambda k, i: (k, i, 0, 0)),
                   pl.BlockSpec((1, 1, HEAD_DIM, nc), lambda k, i: (k, i, 0, 0))],
        out_shape=[jax.ShapeDtypeStruct((2, bg, nc, HEAD_DIM), bf16),
                   jax.ShapeDtypeStruct((2, bg, HEAD_DIM, nc), bf16)],
        compiler_params=_cparams(("parallel", "parallel")),
        name="cmp_mlp",
    )(a, pe, w1, b1, w2, posc, invf)


def _cmp_attn_kernel(q_ref, kct_ref, vc_ref, cover_ref, oc_ref, sel_ref):
    qi = pl.program_id(2)
    nc = vc_ref.shape[2]
    rows = NSA_REP * CMP_TQ
    tq = qi * CMP_TQ + lax.broadcasted_iota(i32, (CMP_TQ, 1), 0)
    row_live = jnp.where(tq >= CMP_BLOCK - 1, 1.0, 0.0)
    tiny = jnp.finfo(f32).tiny

    def attend(ncols):
        kct = kct_ref[0, 0, :, :ncols]
        vc = vc_ref[0, 0, :ncols, :]
        cend = lax.broadcasted_iota(i32, (1, ncols), 1) * CMP_STRIDE + (CMP_BLOCK - 1)
        bias = jnp.where(cend <= tq, 0.0, MASK_NEG)
        s = _dot(q_ref[0].reshape(rows, HEAD_DIM), kct).reshape(NSA_REP, CMP_TQ, ncols) + bias[None]
        e = jnp.exp(s - jnp.max(s, axis=-1, keepdims=True))
        live = row_live[None]
        p = e * (live / jnp.maximum(live * jnp.sum(e, axis=-1, keepdims=True), tiny))
        oc_ref[0] = _dot(p.reshape(rows, ncols).astype(bf16), vc).reshape(NSA_REP, CMP_TQ, HEAD_DIM)
        psum = jnp.sum(p, axis=0)
        hi = psum.astype(bf16)
        lo = (psum - hi.astype(f32)).astype(bf16)
        cover = cover_ref[:ncols, :]
        imp = _dot(hi, cover) + _dot(lo, cover)

        j = lax.broadcasted_iota(i32, (CMP_TQ, LANES), 1)
        cur = tq // SEL_BLOCK
        forced = (j == 0) | (j == cur) | (j == cur - 1)
        valid = j * SEL_BLOCK <= tq
        imp = jnp.where(valid, jnp.where(forced, FORCED_SCORE, imp), -FORCED_SCORE)

        nblk = ncols * CMP_STRIDE // SEL_BLOCK
        jt = lax.broadcasted_iota(i32, (nblk, CMP_TQ), 0)

        def pick(_, carry):
            work, sel = carry
            m = jnp.max(work, axis=0, keepdims=True)
            first = jnp.min(jnp.where(work == m, jt, LANES), axis=0, keepdims=True)
            hit = jt == first
            return jnp.where(hit, -jnp.inf, work), jnp.where(hit, 1.0, sel)

        _, sel_t = lax.fori_loop(0, SEL_TOPK, pick, (imp.T[:nblk], jnp.zeros((nblk, CMP_TQ), f32)))
        if nblk < LANES:
            sel_t = jnp.concatenate([sel_t, jnp.zeros((LANES - nblk, CMP_TQ), f32)], axis=0)
        sel_ref[0, 0] = jnp.where(valid, jnp.where(sel_t.T > 0.0, 0.0, MASK_NEG), MASK_NEG).astype(bf16)

    need = (qi + 1) * (CMP_TQ // CMP_STRIDE)
    for ncols in range(LANES, nc + 1, LANES):
        @pl.when((need > ncols - LANES) & (need <= ncols))
        def _():
            attend(ncols)


def _cmp_attn(q_r, kvc, kvc_t, cover, b, t):
    g = NSA_KV_GROUPS
    nc = kvc.shape[2]
    nq = t // CMP_TQ
    return pl.pallas_call(
        _cmp_attn_kernel,
        grid=(b, g, nq),
        in_specs=[pl.BlockSpec((1, NSA_REP, CMP_TQ, HEAD_DIM), lambda bi, gi, qi: (bi, gi, qi, 0)),
                  pl.BlockSpec((1, 1, HEAD_DIM, nc), lambda bi, gi, qi: (0, bi * g + gi, 0, 0)),
                  pl.BlockSpec((1, 1, nc, HEAD_DIM), lambda bi, gi, qi: (1, bi * g + gi, 0, 0)),
                  pl.BlockSpec((nc, LANES), lambda bi, gi, qi: (0, 0))],
        out_specs=[pl.BlockSpec((1, NSA_REP, CMP_TQ, HEAD_DIM), lambda bi, gi, qi: (bi, gi, qi, 0)),
                   pl.BlockSpec((1, 1, CMP_TQ, LANES), lambda bi, gi, qi: (bi, gi, qi, 0))],
        out_shape=[jax.ShapeDtypeStruct((b, NSA_HEADS, t, HEAD_DIM), f32),
                   jax.ShapeDtypeStruct((b, g, t, LANES), bf16)],
        compiler_params=_cparams(("parallel", "parallel", "parallel")),
        name="cmp_attn",
    )(q_r, kvc_t, kvc, cover)


def _nsa_attn_kernel(q_ref, kx_ref, vs_ref, kw_ref, vw_ref, sel_ref, oc_ref, gate_ref,
                     o_ref, qx_ref, s_ref, m_ref, acc_ref, yw_ref):
    gi = pl.program_id(1)
    qi = pl.program_id(2)
    start = qi * ATT_TQ
    tq = start + lax.broadcasted_iota(i32, (ATT_TQ, 1), 0)

    selb = sel_ref[0, 0]
    for r in range(NSA_REP):
        qx_ref[r * ATT_TQ:(r + 1) * ATT_TQ, 0:LANES] = selb
        qr = q_ref[0, r]
        qx_ref[r * ATT_TQ:(r + 1) * ATT_TQ, LANES:2 * LANES] = jnp.concatenate([qr, jnp.zeros_like(qr)], axis=1)
    rows = NSA_REP * ATT_TQ

    m_ref[...] = jnp.full(m_ref.shape, MASK_NEG, f32)
    acc_ref[...] = jnp.zeros(acc_ref.shape, f32)

    def consume(kt, v_ref, bias, parts=1):
        k0 = pl.multiple_of(kt * ATT_TK, ATT_TK)
        v = v_ref[0, 0, pl.ds(k0, ATT_TK), :]
        heads = NSA_REP // parts
        for part in range(parts):
            rs = slice(part * heads * ATT_TQ, (part + 1) * heads * ATT_TQ)
            s = s_ref[rs]
            if bias is not None:
                s = (s.reshape(heads, ATT_TQ, ATT_TK) + bias[None]).reshape(heads * ATT_TQ, ATT_TK)
            m_prev = m_ref[rs]
            m_new = jnp.maximum(m_prev, jnp.max(s, axis=-1, keepdims=True))
            alpha = jnp.exp(m_prev - m_new)
            p = jnp.exp(s - jnp.concatenate([m_new] * (ATT_TK // LANES), axis=1))
            acc_ref[rs] = alpha * acc_ref[rs] + _dot(p.astype(bf16), v)
            m_ref[rs] = m_new

    def kpos(kt):
        return kt * ATT_TK + lax.broadcasted_iota(i32, (1, ATT_TK), 1)

    last = start // ATT_TK
    for part in range(2):
        rs = slice(part * rows // 2, (part + 1) * rows // 2)
        s_ref[rs] = _dot(qx_ref[rs], kx_ref[0, 0, 0])

    def sel_step(kt, _):
        consume(kt, vs_ref, None)
        s_ref[...] = _dot(qx_ref[...], kx_ref[0, 0, kt + 1])
        return 0

    n_win = (WINDOW + ATT_TQ) // WIN_TK
    w_first = jnp.maximum(start // WIN_TK - WINDOW // WIN_TK, 0)
    kw = jnp.concatenate([kw_ref[0, 0, w_first + i] for i in range(n_win)], axis=1)
    w0 = pl.multiple_of(w_first * WIN_TK, WIN_TK)
    vw = vw_ref[0, 0, pl.ds(w0, n_win * WIN_TK), :]
    kp = w0 + lax.broadcasted_iota(i32, (1, n_win * WIN_TK), 1)
    wbias = jnp.where(kp <= tq, jnp.where(kp > tq - WINDOW, 0.0, MASK_NEG), MASK_NEG)
    hh = WIN_HEADS
    gates = jax.nn.sigmoid(gate_ref[...])
    per_group = 3 * NSA_REP
    shifted = gates
    for gg in range(1, NSA_KV_GROUPS):
        shifted = jnp.where(gi == gg, pltpu.roll(gates, LANES - gg * per_group, 1), shifted)

    def gate(r, br):
        c = r * 3 + br
        return jnp.broadcast_to(shifted[:, c:c + 1], (ATT_TQ, HEAD_DIM))

    upper = pltpu.roll(shifted, HEAD_DIM, 1)[:, HEAD_DIM:]

    def gated_output(a, r, br):
        c = r * 3 + br
        ratio = upper / a[:, HEAD_DIM:]
        return a[:, :HEAD_DIM] * jnp.broadcast_to(ratio[:, c:c + 1], (ATT_TQ, HEAD_DIM))

    for half in range(NSA_REP // hh):
        qh = q_ref[0, half * hh:(half + 1) * hh].reshape(hh * ATT_TQ, HEAD_DIM)
        sw = _dot(qh, kw).reshape(hh, ATT_TQ, n_win * WIN_TK) + wbias[None]
        sw = sw.reshape(hh * ATT_TQ, n_win * WIN_TK)
        pw = jnp.exp(sw - jnp.max(sw, axis=-1, keepdims=True))
        aw = _dot(pw.astype(bf16), vw)
        for i in range(hh):
            r = half * hh + i
            a = aw[i * ATT_TQ:(i + 1) * ATT_TQ]
            yw_ref[r] = gated_output(a, r, 2)

    def sel_group(j, _):
        for i in range(SEL_UNROLL):
            sel_step(SEL_UNROLL * j + i, 0)
        return 0

    lax.fori_loop(0, last // SEL_UNROLL, sel_group, 0)
    lax.fori_loop(last // SEL_UNROLL * SEL_UNROLL, last, sel_step, 0)

    consume(last, vs_ref, jnp.where(kpos(last) <= tq, 0.0, MASK_NEG), parts=2)

    for r in range(NSA_REP):
        osel = gated_output(acc_ref[r * ATT_TQ:(r + 1) * ATT_TQ], r, 1)
        o_ref[:, r * HEAD_DIM:(r + 1) * HEAD_DIM] = (gate(r, 0) * oc_ref[0, r] + osel + yw_ref[r]).astype(o_ref.dtype)


def _nsa_attn(q_r, kx, vs, kw, vw, selb, oc, proj, b, t):
    g = NSA_KV_GROUPS
    nq = t // ATT_TQ
    once = dict(pipeline_mode=pl.Buffered(1))
    vspec = pl.BlockSpec((1, 1, t, LANES), lambda bi, gi, qi: (bi, gi, 0, 0), **once)
    hspec = pl.BlockSpec((1, NSA_REP, ATT_TQ, HEAD_DIM), lambda bi, gi, qi: (bi, gi, qi, 0))
    return pl.pallas_call(
        _nsa_attn_kernel,
        grid=(b, g, nq),
        in_specs=[hspec,
                  pl.BlockSpec((1, 1, t // ATT_TK, KX_WIDTH, ATT_TK), lambda bi, gi, qi: (bi, gi, 0, 0, 0), **once),
                  vspec,
                  pl.BlockSpec((1, 1, t // WIN_TK, HEAD_DIM, WIN_TK), lambda bi, gi, qi: (bi, gi, 0, 0, 0), **once),
                  vspec,
                  pl.BlockSpec((1, 1, ATT_TQ, LANES), lambda bi, gi, qi: (bi, gi, qi, 0)),
                  hspec,
                  pl.BlockSpec((ATT_TQ, LANES), lambda bi, gi, qi: (bi * nq + qi, COL_SMALL // LANES))],
        out_specs=pl.BlockSpec((ATT_TQ, NSA_REP * HEAD_DIM), lambda bi, gi, qi: (bi * nq + qi, gi)),
        out_shape=jax.ShapeDtypeStruct((b * t, NSA_WIDTH), bf16),
        scratch_shapes=[pltpu.VMEM((NSA_REP * ATT_TQ, KX_WIDTH), bf16),
                        pltpu.VMEM((NSA_REP * ATT_TQ, ATT_TK), f32),
                        pltpu.VMEM((NSA_REP * ATT_TQ, LANES), f32),
                        pltpu.VMEM((NSA_REP * ATT_TQ, LANES), f32),
                        pltpu.VMEM((NSA_REP, ATT_TQ, HEAD_DIM), f32)],
        compiler_params=_cparams(("parallel", "parallel", "arbitrary")),
        name="nsa_attn",
    )(q_r, kx, vs, kw, vw, selb, oc, proj)


HALO = 8


def _ssd_kernel(xbc_ref, z_ref, small_ref, dtt_ref, cw_ref, cb_ref, dtb_r_ref, dtb_c_ref,
                alog_r_ref, alog_c_ref, dskip_ref, nw_ref, o_ref, ext_ref, st_ref):
    c = pl.program_id(1)
    L = CHUNK

    @pl.when(c == 0)
    def _():
        ext_ref[0:HALO, :] = jnp.zeros((HALO, XBC_WIDTH), f32)
        st_ref[...] = jnp.zeros_like(st_ref)

    ext_ref[HALO:HALO + L, :] = xbc_ref[...]
    conv = cb_ref[...]
    for k in range(CONV_WIDTH):
        off = HALO - (CONV_WIDTH - 1) + k
        conv = conv + cw_ref[k:k + 1, :] * ext_ref[off:off + L, :]
    ext_ref[0:HALO, :] = ext_ref[L:L + HALO, :]
    act = conv * jax.nn.sigmoid(conv)
    xs = act[:, :SSM_WIDTH]
    bm = act[:, SSM_WIDTH:SSM_WIDTH + SSM_GROUPS * SSM_STATE]
    cm = act[:, SSM_WIDTH + SSM_GROUPS * SSM_STATE:]

    dt_c = jax.nn.softplus(small_ref[:, SMALL_DT_OFF:SMALL_DT_OFF + SSM_HEADS] + dtb_r_ref[...])
    dt_r = jax.nn.softplus(dtt_ref[0] + dtb_c_ref[...])
    a_r = -jnp.exp(alog_r_ref[...])
    a_c = -jnp.exp(alog_c_ref[...])
    row = lax.broadcasted_iota(i32, (L, L), 0)
    col = lax.broadcasted_iota(i32, (L, L), 1)
    causal = col <= row
    tri = jnp.where(causal, 1.0, 0.0)
    acs_c = _dot_hi(tri, dt_c * a_r)
    acs_r = _dot_hi(dt_r * a_c, jnp.where(row <= col, 1.0, 0.0))

    z = z_ref[...]
    for g in range(SSM_GROUPS):
        cg = cm[:, g * SSM_STATE:(g + 1) * SSM_STATE].astype(bf16)
        bg = bm[:, g * SSM_STATE:(g + 1) * SSM_STATE]
        cb = _dot_t(cg, bg.astype(bf16))
        bg_t = bg.T
        ssq = jnp.zeros((L, 1), f32)
        yg = []
        for pp in range(SSM_REP // 2):
            pair = g * (SSM_REP // 2) + pp
            ps = slice(pair * LANES, (pair + 1) * LANES)
            x_p = xs[:, ps]
            low = lax.broadcasted_iota(i32, (L, LANES), 1) < SSM_HEADDIM
            low_n = lax.broadcasted_iota(i32, (SSM_STATE, LANES), 1) < SSM_HEADDIM
            x_own = (jnp.where(low, x_p, 0.0).astype(bf16), jnp.where(low, 0.0, x_p).astype(bf16))
            st = st_ref[pair]
            y = jnp.zeros((L, LANES), f32)
            st_in = jnp.zeros((SSM_STATE, LANES), f32)
            grow, keep = [], []
            for j in range(2):
                h = 2 * pair + j
                a_col = acs_c[:, h:h + 1]
                a_row = acs_r[h:h + 1, :]
                dt_row = dt_r[h:h + 1, :]
                a_last = acs_r[h:h + 1, L - 1:L]
                seg = a_col - a_row
                decay = jnp.where(causal, jnp.exp(jnp.where(causal, seg, 0.0)), 0.0)
                w = cb * decay * dt_row
                y = y + _dot(w.astype(bf16), x_own[j])
                bscaled = bg_t * (jnp.exp(a_last - a_row) * dt_row)
                st_in = st_in + _dot(bscaled.astype(bf16), x_own[j])
                grow.append(jnp.exp(a_col))
                keep.append(jnp.exp(a_last))
            y = y + _dot(cg, st.astype(bf16)) * jnp.where(low, grow[0], grow[1])
            st_ref[pair] = jnp.where(low_n, keep[0], keep[1]) * st + st_in
            y = y + dskip_ref[:, ps] * x_p
            zp = z[:, ps]
            y = y * (zp * jax.nn.sigmoid(zp))
            ssq = ssq + jnp.sum(y * y, axis=-1, keepdims=True)
            yg.append(y)
        rs = lax.rsqrt(ssq / (SSM_REP * SSM_HEADDIM) + NORM_EPS)
        for pp in range(SSM_REP // 2):
            pair = g * (SSM_REP // 2) + pp
            ps = slice(pair * LANES, (pair + 1) * LANES)
            o_ref[:, ps] = (yg[pp] * rs * nw_ref[:, ps]).astype(o_ref.dtype)


def _ssd(proj, dtt, cw, cb, dtb_r, dtb_c, alog_r, alog_c, dskip, nw, b, t):
    nch = t // CHUNK
    row = lambda bi, ci: bi * nch + ci
    const2 = lambda bi, ci: (0, 0)
    return pl.pallas_call(
        _ssd_kernel,
        grid=(b, nch),
        in_specs=[pl.BlockSpec((CHUNK, XBC_WIDTH), lambda bi, ci: (row(bi, ci), COL_XBC // XBC_WIDTH)),
                  pl.BlockSpec((CHUNK, SSM_WIDTH), lambda bi, ci: (row(bi, ci), COL_Z // SSM_WIDTH)),
                  pl.BlockSpec((CHUNK, LANES), lambda bi, ci: (row(bi, ci), COL_SMALL // LANES)),
                  pl.BlockSpec((1, SSM_HEADS, CHUNK), lambda bi, ci: (bi, 0, ci)),
                  pl.BlockSpec((CONV_WIDTH, XBC_WIDTH), const2),
                  pl.BlockSpec((1, XBC_WIDTH), const2),
                  pl.BlockSpec((1, SSM_HEADS), const2),
                  pl.BlockSpec((SSM_HEADS, 1), const2),
                  pl.BlockSpec((1, SSM_HEADS), const2),
                  pl.BlockSpec((SSM_HEADS, 1), const2),
                  pl.BlockSpec((1, SSM_WIDTH), const2),
                  pl.BlockSpec((1, SSM_WIDTH), const2)],
        out_specs=pl.BlockSpec((CHUNK, SSM_WIDTH), lambda bi, ci: (row(bi, ci), 0)),
        out_shape=jax.ShapeDtypeStruct((b * t, SSM_WIDTH), bf16),
        scratch_shapes=[pltpu.VMEM((HALO + CHUNK, XBC_WIDTH), f32),
                        pltpu.VMEM((SSM_HEADS // 2, SSM_STATE, 2 * SSM_HEADDIM), f32)],
        compiler_params=_cparams(("parallel", "arbitrary")),
        name="ssd",
    )(proj, proj, proj, dtt, cw, cb, dtb_r, dtb_c, alog_r, alog_c, dskip, nw)


def _layer_norm(v, g, b):
    mu = jnp.mean(v, axis=-1, keepdims=True)
    d = v - mu
    var = jnp.mean(d * d, axis=-1, keepdims=True)
    return d * lax.rsqrt(var + NORM_EPS) * g + b


def _outproj_kernel(alpha, ya_ref, yb_ref, x_ref, wa_ref, wb_ref, g_ref, b_ref, wr_ref, br_ref,
                    h_ref, route_ref, cnt_ref):
    i = pl.program_id(0)
    tm = x_ref.shape[0]
    mix = _dot(ya_ref[...], wa_ref[...]) + _dot(yb_ref[...], wb_ref[...])
    h = _layer_norm(alpha * x_ref[...] + mix, g_ref[...], b_ref[...])
    h_ref[...] = h

    h_hi = h.astype(bf16)
    h_lo = (h - h_hi.astype(f32)).astype(bf16)
    t = _dot(h_hi, wr_ref[...])
    logits = t[:, :LANES] + t[:, LANES:] + _dot(h_lo, wr_ref[:, :LANES]) + br_ref[...]
    lane = lax.broadcasted_iota(i32, (tm, LANES), 1)
    ninf = -jnp.inf
    gmask = lane < N_EXPERT_GROUPS
    gl = jnp.where(gmask, logits, ninf)
    ge = jnp.where(gmask, jnp.exp(gl - jnp.max(gl, axis=-1, keepdims=True)), 0.0)
    pg = ge / jnp.sum(ge, axis=-1, keepdims=True)
    g_gate = jnp.max(pg, axis=-1, keepdims=True)
    g_sel = jnp.min(jnp.where(gmask & (pg == g_gate), lane, LANES), axis=-1, keepdims=True)
    lo = N_EXPERT_GROUPS + g_sel * EXPERTS_PER_GROUP
    emask = (lane >= lo) & (lane < lo + EXPERTS_PER_GROUP)
    el = jnp.where(emask, logits, ninf)
    ee = jnp.where(emask, jnp.exp(el - jnp.max(el, axis=-1, keepdims=True)), 0.0)
    pe = ee / jnp.sum(ee, axis=-1, keepdims=True)
    p0 = jnp.max(pe, axis=-1, keepdims=True)
    l0 = jnp.min(jnp.where(emask & (pe == p0), lane, LANES), axis=-1, keepdims=True)
    rest = jnp.where(emask & (lane != l0), pe, ninf)
    p1 = jnp.max(rest, axis=-1, keepdims=True)
    l1 = jnp.min(jnp.where(rest == p1, lane, LANES), axis=-1, keepdims=True)
    psum = p0 + p1
    w0 = g_gate * p0 / psum
    w1 = g_gate * p1 / psum
    e0 = l0 - N_EXPERT_GROUPS
    e1 = l1 - N_EXPERT_GROUPS

    @pl.when(i == 0)
    def _():
        cnt_ref[...] = jnp.zeros_like(cnt_ref)

    oh0 = lane == e0
    oh1 = lane == e1
    both = jnp.where(oh0, 1.0, 0.0) + jnp.where(oh1, 1.0, 0.0)
    r_i = lax.broadcasted_iota(i32, (tm, tm), 0)
    c_i = lax.broadcasted_iota(i32, (tm, tm), 1)
    strict = jnp.where(c_i < r_i, 1.0, 0.0).astype(bf16)
    before = _dot(strict, both.astype(bf16)) + cnt_ref[...]
    rank0 = jnp.sum(jnp.where(oh0, before, 0.0), axis=-1, keepdims=True)
    rank1 = jnp.sum(jnp.where(oh1, before, 0.0), axis=-1, keepdims=True)
    cnt_ref[...] = cnt_ref[...] + jnp.sum(both, axis=0, keepdims=True)

    out = jnp.where(lane == 0, e0.astype(f32), 0.0)
    out = jnp.where(lane == 1, e1.astype(f32), out)
    out = jnp.where(lane == 2, w0, out)
    out = jnp.where(lane == 3, w1, out)
    out = jnp.where(lane == 4, rank0, out)
    out = jnp.where(lane == 5, rank1, out)
    route_ref[...] = out


def _outproj(y_nsa, y_ssm, x2, wa, wb, g, bta, wr, br, alpha):
    n, d = x2.shape
    tm = OUT_TM
    const = lambda i: (0, 0)
    rowb = lambda i: (i, 0)
    return pl.pallas_call(
        functools.partial(_outproj_kernel, alpha),
        grid=(n // tm,),
        in_specs=[pl.BlockSpec((tm, NSA_WIDTH), rowb), pl.BlockSpec((tm, SSM_WIDTH), rowb),
                  pl.BlockSpec((tm, d), rowb),
                  pl.BlockSpec((NSA_WIDTH, d), const), pl.BlockSpec((SSM_WIDTH, d), const),
                  pl.BlockSpec((1, d), const), pl.BlockSpec((1, d), const),
                  pl.BlockSpec((d, 2 * LANES), const), pl.BlockSpec((1, LANES), const)],
        out_specs=[pl.BlockSpec((tm, d), rowb), pl.BlockSpec((tm, LANES), rowb),
                   pl.BlockSpec((1, LANES), const)],
        out_shape=[jax.ShapeDtypeStruct((n, d), f32), jax.ShapeDtypeStruct((n, LANES), f32),
                   jax.ShapeDtypeStruct((1, LANES), f32)],
        compiler_params=_cparams(("arbitrary",)),
        name="outproj",
    )(y_nsa, y_ssm, x2, wa, wb, g, bta, wr, br)


def _row_copy(src_ref, src_row, dst_ref, dst_row, sem):
    return pltpu.make_async_copy(src_ref.at[pl.ds(src_row, 1), :], dst_ref.at[pl.ds(dst_row, 1), :], sem)


def _pack_bf16_pairs(x):
    half = x.shape[1] // 2
    hi = pltpu.bitcast(x[:, :half].astype(bf16).astype(f32), jnp.uint32)
    lo = pltpu.bitcast(x[:, half:].astype(bf16).astype(f32), jnp.uint32)
    return hi | (lo >> 16)


def _unpack_bf16_pairs(words):
    return (pltpu.bitcast(words & jnp.uint32(0xFFFF0000), f32), pltpu.bitcast(words << 16, f32))


def _dest_row(dest_ref, tm, k, r):
    return dest_ref[0, 0, k * tm + r]


def _dispatch_kernel(dest_ref, zflag_ref, h_ref, xs_ref, zero_ref, tile_ref, pk_ref, sem, lsem, zsem):
    tm = DISP_TM
    nb = zflag_ref.shape[0]
    i = pl.program_id(0)

    @pl.when(i == 0)
    def _():
        zero_ref[...] = jnp.zeros_like(zero_ref)

        def zblock(i):
            rows = pl.ds(pl.multiple_of(i * MOE_TM, MOE_TM), MOE_TM)
            return pltpu.make_async_copy(zero_ref, xs_ref.at[rows, :], zsem)

        def zstart(i, _):
            @pl.when(zflag_ref[i] != 0)
            def _():
                zblock(i).start()
            return 0

        def zwait(i, _):
            @pl.when(zflag_ref[i] != 0)
            def _():
                zblock(i).wait()
            return 0

        lax.fori_loop(0, nb, zstart, 0)
        lax.fori_loop(0, nb, zwait, 0)

    nslot = tile_ref.shape[0]
    nsteps = pl.num_programs(0)
    slot = i % nslot

    def tile_load(step, s):
        rows = pl.ds(pl.multiple_of(step * tm, tm), tm)
        return pltpu.make_async_copy(h_ref.at[rows, :], tile_ref.at[s], lsem.at[s])

    @pl.when(i == 0)
    def _():
        tile_load(0, 0).start()

    tile_load(i, slot).wait()

    @pl.when(i + 1 < nsteps)
    def _():
        tile_load(i + 1, (i + 1) % nslot).start()

    pk_ref[slot] = _pack_bf16_pairs(tile_ref[slot])

    def issue(r, _):
        for k in range(2):
            _row_copy(pk_ref.at[slot], r, xs_ref, _dest_row(dest_ref, tm, k, r), sem.at[slot]).start()
        return 0

    lax.fori_loop(0, tm, issue, 0, unroll=16)

    def wait_rows(s):
        for k in range(2):
            pltpu.make_async_copy(pk_ref.at[s], xs_ref.at[pl.ds(0, tm), :], sem.at[s]).wait()

    @pl.when(i > 0)
    def _():
        wait_rows((i - 1) % nslot)

    @pl.when(i == nsteps - 1)
    def _():
        wait_rows(slot)


def _dispatch(dest_t, zflag, h, p_rows):
    n, d = h.shape
    tm = DISP_TM
    return pl.pallas_call(
        _dispatch_kernel,
        grid=(n // tm,),
        in_specs=[pl.BlockSpec((1, 1, 2 * tm), lambda i: (i, 0, 0), memory_space=pltpu.SMEM),
                  pl.BlockSpec(memory_space=pltpu.SMEM),
                  pl.BlockSpec(memory_space=pl.ANY)],
        out_specs=pl.BlockSpec(memory_space=pl.ANY),
        out_shape=jax.ShapeDtypeStruct((p_rows, d // 2), jnp.uint32),
        scratch_shapes=[pltpu.VMEM((MOE_TM, d // 2), jnp.uint32), pltpu.VMEM((3, tm, d), f32),
                        pltpu.VMEM((3, tm, d // 2), jnp.uint32),
                        pltpu.SemaphoreType.DMA((3,)), pltpu.SemaphoreType.DMA((3,)), pltpu.SemaphoreType.DMA(())],
        compiler_params=_cparams(("arbitrary",)),
        name="dispatch",
    )(dest_t, zflag, h)


def _experts_kernel(be_ref, nu_ref, first_ref, next_ref, slot_ref, xs_ref, wg_ref, wu_ref, wd_ref, y_ref,
                    fg_ref, fu_ref, fd_ref, wgb_ref, wub_ref, wdb_ref, sems):
    i = pl.program_id(0)

    def weight_loads(e, s):
        return (pltpu.make_async_copy(wg_ref.at[e], fg_ref.at[s], sems.at[s, 0]),
                pltpu.make_async_copy(wu_ref.at[e], fu_ref.at[s], sems.at[s, 1]),
                pltpu.make_async_copy(wd_ref.at[e], fd_ref.at[s], sems.at[s, 2]))

    @pl.when(i == 0)
    def _():
        for c in weight_loads(be_ref[0], 0):
            c.start()

    @pl.when(first_ref[i] != 0)
    def _():
        s = slot_ref[i]
        for c in weight_loads(be_ref[i], s):
            c.wait()
        wgb_ref[...] = fg_ref[s].astype(bf16)
        wub_ref[...] = fu_ref[s].astype(bf16)
        wdb_ref[...] = fd_ref[s].astype(bf16)

        @pl.when(next_ref[i] >= 0)
        def _():
            for c in weight_loads(next_ref[i], 1 - s):
                c.start()

    @pl.when(i < nu_ref[0])
    def _():
        first, second = _unpack_bf16_pairs(xs_ref[...])
        xb = jnp.concatenate([first.astype(bf16), second.astype(bf16)], axis=1)
        gte = _dot(xb, wgb_ref[...])
        up = _dot(xb, wub_ref[...])
        act = gte * jax.nn.sigmoid(gte) * up
        y = _dot(act.astype(bf16), wdb_ref[...])
        y_ref[...] = _pack_bf16_pairs(y)

    @pl.when(i >= nu_ref[0])
    def _():
        y_ref[...] = jnp.zeros_like(y_ref)


def _experts(block_e, n_used, first, next_e, slot, xs, w_gate, w_up, w_down):
    p_rows = xs.shape[0]
    d = w_gate.shape[1]
    tm = MOE_TM
    nb = p_rows // tm
    de = w_gate.shape[-1]
    xmap = lambda i, be, nu, fi, ne, sl: (jnp.maximum(jnp.minimum(i, nu[0] - 1), 0), 0)
    hbm = pl.BlockSpec(memory_space=pl.ANY)
    return pl.pallas_call(
        _experts_kernel,
        grid_spec=pltpu.PrefetchScalarGridSpec(
            num_scalar_prefetch=5,
            grid=(nb,),
            in_specs=[pl.BlockSpec((tm, d // 2), xmap), hbm, hbm, hbm],
            out_specs=pl.BlockSpec((tm, d // 2), lambda i, be, nu, fi, ne, sl: (i, 0)),
            scratch_shapes=[pltpu.VMEM((2, d, de), f32), pltpu.VMEM((2, d, de), f32), pltpu.VMEM((2, de, d), f32),
                            pltpu.VMEM((d, de), bf16), pltpu.VMEM((d, de), bf16), pltpu.VMEM((de, d), bf16),
                            pltpu.SemaphoreType.DMA((2, 3))],
        ),
        out_shape=jax.ShapeDtypeStruct((p_rows, d // 2), jnp.uint32),
        compiler_params=_cparams(("arbitrary",)),
        name="experts",
    )(block_e, n_used, first, next_e, slot, xs, w_gate, w_up, w_down)


def _combine_kernel(alpha, dest_ref, ndest_ref, route_ref, h_ref, g_ref, b_ref, y_ref, o_ref, buf_ref, sems):
    tm = h_ref.shape[0]
    i = pl.program_id(0)
    slot = i % 2

    def start_row(ids_ref, s, r):
        for k in range(2):
            _row_copy(y_ref, _dest_row(ids_ref, tm, k, r), buf_ref.at[s, k], r, sems.at[s]).start()

    def wait_tile():
        for k in range(2):
            pltpu.make_async_copy(y_ref.at[pl.ds(0, tm), :], buf_ref.at[slot, k], sems.at[slot]).wait()

    def finish_tile():
        route = route_ref[...]
        lo0, hi0 = _unpack_bf16_pairs(buf_ref[slot, 0])
        lo1, hi1 = _unpack_bf16_pairs(buf_ref[slot, 1])
        w0, w1 = route[:, 2:3], route[:, 3:4]
        ffn = jnp.concatenate([w0 * lo0 + w1 * lo1, w0 * hi0 + w1 * hi1], axis=1)
        o_ref[...] = _layer_norm(alpha * h_ref[...] + ffn, g_ref[...], b_ref[...])

    @pl.when(i == 0)
    def _():
        def issue(r, _):
            start_row(dest_ref, 0, r)
            return 0
        lax.fori_loop(0, tm, issue, 0, unroll=4)

    @pl.when(i + 1 < pl.num_programs(0))
    def _():
        wait_tile()
        for r in range(tm):
            start_row(ndest_ref, 1 - slot, r)
        finish_tile()

    @pl.when(i + 1 == pl.num_programs(0))
    def _():
        wait_tile()
        finish_tile()


def _combine(dest_t, route, h, g, bta, y, alpha):
    n, d = h.shape
    tm = COMB_TM
    const = lambda i: (0, 0)
    return pl.pallas_call(
        functools.partial(_combine_kernel, alpha),
        grid=(n // tm,),
        in_specs=[pl.BlockSpec((1, 1, 2 * tm), lambda i: (i, 0, 0), memory_space=pltpu.SMEM),
                  pl.BlockSpec((1, 1, 2 * tm), lambda i: (jnp.minimum(i + 1, n // tm - 1), 0, 0),
                               memory_space=pltpu.SMEM),
                  pl.BlockSpec((tm, LANES), lambda i: (i, 0)),
                  pl.BlockSpec((tm, d), lambda i: (i, 0)),
                  pl.BlockSpec((1, d), const), pl.BlockSpec((1, d), const),
                  pl.BlockSpec(memory_space=pl.ANY)],
        out_specs=pl.BlockSpec((tm, d), lambda i: (i, 0)),
        out_shape=jax.ShapeDtypeStruct((n, d), f32),
        scratch_shapes=[pltpu.VMEM((2, 2, tm, d // 2), jnp.uint32), pltpu.SemaphoreType.DMA((2,))],
        compiler_params=_cparams(("arbitrary",)),
        name="combine",
    )(dest_t, dest_t, route, h, g, bta, y)


def _tile_dest(dest, tm):
    n = dest.shape[1]
    return dest.reshape(2, n // tm, tm).transpose(1, 0, 2).reshape(n // tm, 1, 2 * tm)


def _layer(x, positions, w_in, cmp_k_pe, cmp_k_w1, cmp_k_b1, cmp_k_w2, cmp_v_pe, cmp_v_w1, cmp_v_b1, cmp_v_w2,
           conv_w, conv_b, dt_bias, a_log, d_skip, ssm_norm_w, w_out, ln1_g, ln1_b,
           w_router_group, b_router_group, w_router_expert, b_router_expert, w_gate, w_up, w_down, ln2_g, ln2_b,
           alpha):
    b, t, d = x.shape
    n = b * t
    assert t % max(ATT_TK, ATT_TQ, CMP_TQ, CHUNK, PREP_TM) == 0 and n % max(PROJ_TM, OUT_TM, DISP_TM) == 0
    assert t // SEL_BLOCK <= LANES and (t // CMP_STRIDE) % LANES == 0 and t >= WINDOW + ATT_TQ
    x2 = x.reshape(n, d)

    c0 = NSA_WIDTH
    c1 = c0 + 6 * KV_WIDTH
    c2 = c1 + 3 * NSA_HEADS
    c3 = c2 + SSM_WIDTH
    c4 = c3 + XBC_WIDTH
    w_small = jnp.concatenate([w_in[:, c1:c2], w_in[:, c4:], jnp.zeros((d, LANES - 3 * NSA_HEADS - SSM_HEADS), f32)], axis=1)
    w_cat = jnp.concatenate([w_in[:, :c0], w_in[:, c2:c3], w_in[:, c3:c4], w_in[:, c0:c1], w_small,
                             jnp.zeros((d, PROJ_COLS - COL_SMALL - LANES), f32)], axis=1).astype(bf16)
    proj = _proj(x2, w_cat)

    lane = np.arange(LANES) % HEAD_DIM
    inv_freq = ROPE_THETA ** (-jnp.arange(0, ROT_DIM, 2, dtype=f32) / ROT_DIM)
    invf = jnp.where(lane < ROT_DIM, inv_freq[lane % (ROT_DIM // 2)], 0.0).astype(f32)[None, :]
    pos128 = jnp.broadcast_to(positions.reshape(n, 1), (n, LANES))
    q_r, cmp_in, k_sel, v_sel, k_win, v_win = _nsa_prep(proj, pos128, invf, b, t)

    nc = t // CMP_STRIDE
    half_w = CMP_STRIDE * HEAD_DIM
    a = cmp_in.reshape(2, b * NSA_KV_GROUPS, nc, half_w)
    pe = jnp.stack([cmp_k_pe, cmp_v_pe]).reshape(2, 2, 1, half_w)
    w1 = jnp.stack([cmp_k_w1, cmp_v_w1]).reshape(2, 2, half_w, CMP_HIDDEN).astype(bf16)
    b1 = jnp.stack([cmp_k_b1, cmp_v_b1]).reshape(2, 1, CMP_HIDDEN)
    w2 = jnp.pad(jnp.stack([cmp_k_w2, cmp_v_w2]), ((0, 0), (0, 0), (0, LANES - HEAD_DIM))).astype(bf16)
    cend = jnp.minimum(jnp.arange(nc) * CMP_STRIDE + CMP_BLOCK - 1, t - 1)
    posc = jnp.broadcast_to(positions[:, cend][:, :, None], (b, nc, LANES))
    kvc, kvc_t = _cmp_mlp(a, pe, w1, b1, w2, posc, invf, b)

    c_start = np.arange(nc)[:, None] * CMP_STRIDE
    s_start = np.arange(LANES)[None, :] * SEL_BLOCK
    cover = ((c_start < s_start + SEL_BLOCK) & (c_start + CMP_BLOCK > s_start)
             & (np.arange(nc)[:, None] < nc - 1) & (np.arange(LANES)[None, :] < t // SEL_BLOCK))
    cover = jnp.asarray(cover, bf16)
    o_cmp, selb = _cmp_attn(q_r, kvc, kvc_t, cover, b, t)

    y_nsa = _nsa_attn(q_r, k_sel, v_sel, k_win, v_win, selb, o_cmp, proj, b, t)

    dt_raw = proj[:, COL_SMALL + SMALL_DT_OFF:COL_SMALL + SMALL_DT_OFF + SSM_HEADS]
    dtt = dt_raw.reshape(b, t, SSM_HEADS).transpose(0, 2, 1)
    y_ssm = _ssd(proj, dtt, conv_w.reshape(CONV_WIDTH, XBC_WIDTH), conv_b.reshape(1, XBC_WIDTH),
                 dt_bias.reshape(1, SSM_HEADS), dt_bias.reshape(SSM_HEADS, 1),
                 a_log.reshape(1, SSM_HEADS), a_log.reshape(SSM_HEADS, 1),
                 jnp.repeat(d_skip, SSM_HEADDIM).reshape(1, SSM_WIDTH), ssm_norm_w.reshape(1, SSM_WIDTH), b, t)

    wr = jnp.concatenate([w_router_group, w_router_expert,
                          jnp.zeros((d, LANES - N_EXPERT_GROUPS - N_EXPERTS), f32)], axis=1)
    br = jnp.concatenate([b_router_group, b_router_expert,
                          jnp.zeros((LANES - N_EXPERT_GROUPS - N_EXPERTS,), f32)])[None, :]
    wr_hi = wr.astype(bf16)
    wr = jnp.concatenate([wr_hi, (wr - wr_hi.astype(f32)).astype(bf16)], axis=1)
    wo = w_out.astype(bf16)
    h, route, counts = _outproj(y_nsa, y_ssm, x2, wo[:NSA_WIDTH], wo[NSA_WIDTH:], ln1_g[None, :], ln1_b[None, :],
                                wr, br, alpha)

    cnt = counts[0, :N_EXPERTS].astype(i32)
    padded = (cnt + MOE_TM - 1) // MOE_TM * MOE_TM
    pad_ends = jnp.cumsum(padded)
    pad_starts = pad_ends - padded
    e01 = route[:, 0:2].astype(i32).T
    start01 = jnp.sum(jnp.where(e01[..., None] == jnp.arange(N_EXPERTS, dtype=i32), pad_starts, 0), axis=-1)
    dest = start01 + route[:, 4:6].astype(i32).T
    p_rows = 2 * n + N_EXPERTS * MOE_TM
    nb = p_rows // MOE_TM
    block_e = jnp.minimum(jnp.sum(jnp.arange(nb, dtype=i32)[:, None] * MOE_TM >= pad_ends[None, :], axis=-1),
                          N_EXPERTS - 1).astype(i32)
    n_used = (pad_ends[-1] // MOE_TM).astype(i32).reshape(1)
    blk = jnp.arange(nb, dtype=i32)
    last_of_expert = jnp.any((blk[:, None] + 1) * MOE_TM == pad_ends[None, :], axis=-1)
    zflag = (last_of_expert | (blk >= n_used[0])).astype(i32)
    block_e = jnp.where(blk < n_used[0], block_e, block_e[jnp.maximum(n_used[0] - 1, 0)])

    xs = _dispatch(_tile_dest(dest, DISP_TM), zflag, h, p_rows)
    prev_e = jnp.concatenate([jnp.full((1,), -1, i32), block_e[:-1]])
    first = ((block_e != prev_e) & (blk < n_used[0])).astype(i32)
    eidx = jnp.arange(N_EXPERTS, dtype=i32)
    later = jnp.where((eidx[None, :] > eidx[:, None]) & (padded[None, :] > 0), eidx[None, :], N_EXPERTS)
    next_of = jnp.min(later, axis=1)
    next_e = jnp.where(next_of[block_e] < N_EXPERTS, next_of[block_e], -1).astype(i32)
    slot = ((jnp.cumsum(first) - 1) % 2).astype(i32)
    y = _experts(block_e, n_used, first, next_e, slot, xs, w_gate, w_up, w_down)
    out = _combine(_tile_dest(dest, COMB_TM), route, h, ln2_g[None, :], ln2_b[None, :], y, alpha)
    return out.reshape(b, t, d)


def kernel(x, positions, w_in, cmp_k_pe, cmp_k_w1, cmp_k_b1, cmp_k_w2, cmp_v_pe, cmp_v_w1, cmp_v_b1, cmp_v_w2, conv_w, conv_b, dt_bias, a_log, d_skip, ssm_norm_w, w_out, ln1_g, ln1_b, w_router_group, b_router_group, w_router_expert, b_router_expert, w_gate, w_up, w_down, ln2_g, ln2_b):
    depth = w_in.shape[0]
    alpha = (2 * depth) ** 0.25
    params = (w_in, cmp_k_pe, cmp_k_w1, cmp_k_b1, cmp_k_w2, cmp_v_pe, cmp_v_w1, cmp_v_b1, cmp_v_w2, conv_w, conv_b,
              dt_bias, a_log, d_skip, ssm_norm_w, w_out, ln1_g, ln1_b, w_router_group, b_router_group,
              w_router_expert, b_router_expert, w_gate, w_up, w_down, ln2_g, ln2_b)
    for l in range(depth):
        x = _layer(x, positions, *[p[l] for p in params], alpha)
    return x
```

```python
import functools

import jax
import jax.numpy as jnp
import numpy as np
from jax import lax
from jax.experimental import pallas as pl
from jax.experimental.pallas import tpu as pltpu

f32 = jnp.float32
bf16 = jnp.bfloat16
i32 = jnp.int32

HEAD_DIM = 64
NSA_HEADS = 16
NSA_KV_GROUPS = 2
NSA_REP = NSA_HEADS // NSA_KV_GROUPS
NSA_WIDTH = NSA_HEADS * HEAD_DIM
KV_WIDTH = NSA_KV_GROUPS * HEAD_DIM
CMP_BLOCK = 32
CMP_STRIDE = 16
CMP_HIDDEN = 256
SEL_BLOCK = 64
SEL_TOPK = 16
WINDOW = 512
FORCED_SCORE = 1.0e4
SSM_HEADDIM = 64
SSM_HEADS = 16
SSM_WIDTH = SSM_HEADS * SSM_HEADDIM
SSM_GROUPS = 4
SSM_REP = SSM_HEADS // SSM_GROUPS
SSM_STATE = 128
CONV_WIDTH = 4
CHUNK = 256
XBC_WIDTH = SSM_WIDTH + 2 * SSM_GROUPS * SSM_STATE
ROPE_THETA = 500000.0
ROT_DIM = HEAD_DIM // 4
N_EXPERT_GROUPS = 4
EXPERTS_PER_GROUP = 8
N_EXPERTS = N_EXPERT_GROUPS * EXPERTS_PER_GROUP
NORM_EPS = 1e-5

LANES = 128
V7X_VMEM_BYTES = 64 * 1024 * 1024
MASK_NEG = -1.0e30

PROJ_TM = 1024
PROJ_TN = 1024
PREP_TM = 512
CMP_TQ = 512
ATT_TQ = 256
ATT_TK = 512
SEL_UNROLL = 4
WIN_TK = 256
WIN_HEADS = 2
KX_WIDTH = 2 * LANES
OUT_TM = 512
MOE_TM = 256
DISP_TM = 256
COMB_TM = 128
VMEM_LIMIT = V7X_VMEM_BYTES - 8 * 1024 * 1024

COL_Q = 0
COL_Z = NSA_WIDTH
COL_XBC = COL_Z + SSM_WIDTH
COL_KV = COL_XBC + XBC_WIDTH
COL_SMALL = COL_KV + 6 * KV_WIDTH
PROJ_COLS = -(-(COL_SMALL + LANES) // PROJ_TN) * PROJ_TN
SMALL_DT_OFF = 3 * NSA_HEADS


def _cparams(sem, vmem=VMEM_LIMIT):
    return pltpu.CompilerParams(dimension_semantics=sem, vmem_limit_bytes=vmem)


def _dot(a, b):
    return jnp.dot(a, b, preferred_element_type=f32)


def _dot_t(a, b):
    return lax.dot_general(a, b, (((1,), (1,)), ((), ())), preferred_element_type=f32)


def _dot_hi(a, b):
    return jnp.dot(a, b, preferred_element_type=f32, precision=lax.Precision.HIGHEST)


def _proj_kernel(x_ref, w_ref, o_ref, xb_ref):
    @pl.when(pl.program_id(1) == 0)
    def _():
        xb_ref[...] = x_ref[...].astype(bf16)

    o_ref[...] = _dot(xb_ref[...], w_ref[...])


def _proj(x2, w_cat):
    n, d = x2.shape
    cols = w_cat.shape[1]
    tm = min(PROJ_TM, n)
    return pl.pallas_call(
        _proj_kernel,
        grid=(n // tm, cols // PROJ_TN),
        in_specs=[pl.BlockSpec((tm, d), lambda i, j: (i, 0)),
                  pl.BlockSpec((d, PROJ_TN), lambda i, j: (0, j))],
        out_specs=pl.BlockSpec((tm, PROJ_TN), lambda i, j: (i, j)),
        out_shape=jax.ShapeDtypeStruct((n, cols), f32),
        scratch_shapes=[pltpu.VMEM((tm, d), bf16)],
        compiler_params=_cparams(("parallel", "arbitrary")),
        name="proj",
    )(x2, w_cat)


def _rope_tables(pos_i32, invf):
    ang = pos_i32.astype(f32) * invf
    return jnp.cos(ang), jnp.sin(ang)


def _rope128(x, cos, sin):
    half = ROT_DIM // 2
    d = lax.broadcasted_iota(i32, x.shape, 1) % HEAD_DIM
    up = pltpu.roll(x, LANES - half, 1)
    dn = pltpu.roll(x, half, 1)
    rot = jnp.where(d < half, -up, dn)
    return x * cos + rot * sin


def _nsa_prep_kernel(pos_ref, invf_ref, q_ref, kc_ref, vc_ref, ks_ref, vs_ref, kw_ref, vw_ref,
                     qo_ref, cmp_ref, kso_ref, vso_ref, kwo_ref, vwo_ref):
    cos, sin = _rope_tables(pos_ref[...], invf_ref[...])
    scale = HEAD_DIM ** -0.5
    for c in range(NSA_WIDTH // LANES):
        t = _rope128(q_ref[:, c * LANES:(c + 1) * LANES], cos, sin) * scale
        qo_ref[0, 2 * c] = t[:, :HEAD_DIM].astype(bf16)
        qo_ref[0, 2 * c + 1] = t[:, HEAD_DIM:].astype(bf16)

    tm = pos_ref.shape[0]

    half_rows = tm // CMP_STRIDE
    for kind, src in enumerate((kc_ref, vc_ref)):
        for j in range(CMP_STRIDE):
            tok_j = src[pl.ds(j, half_rows, stride=CMP_STRIDE), :]
            for g in range(NSA_KV_GROUPS):
                cmp_ref[kind, 0, g, :, j * HEAD_DIM:(j + 1) * HEAD_DIM] = tok_j[:, g * HEAD_DIM:(g + 1) * HEAD_DIM]

    def values_with_ones(src, dst):
        t = src[...]
        tail = jnp.ones((tm, HEAD_DIM), bf16)
        for g in range(NSA_KV_GROUPS):
            vg = t[:, g * HEAD_DIM:(g + 1) * HEAD_DIM].astype(bf16)
            dst[0, g] = jnp.concatenate([vg, tail], axis=1)

    values_with_ones(vs_ref, vso_ref)
    values_with_ones(vw_ref, vwo_ref)

    kw_t = _rope128(kw_ref[...], cos, sin).T
    ks_t = _rope128(ks_ref[...], cos, sin).T
    for g in range(NSA_KV_GROUPS):
        for c in range(tm // WIN_TK):
            kwo_ref[0, g, c] = kw_t[g * HEAD_DIM:(g + 1) * HEAD_DIM, c * WIN_TK:(c + 1) * WIN_TK].astype(bf16)
    blk = lax.broadcasted_iota(i32, (LANES, tm), 0)
    tok = pl.program_id(1) * tm + lax.broadcasted_iota(i32, (LANES, tm), 1)
    onehot_t = jnp.where(tok // SEL_BLOCK == blk, 1.0, 0.0).astype(bf16)
    for g in range(NSA_KV_GROUPS):
        kso_ref[0, g, 0, 0:LANES, :] = onehot_t
        kg = ks_t[g * HEAD_DIM:(g + 1) * HEAD_DIM, :].astype(bf16)
        kso_ref[0, g, 0, LANES:2 * LANES, :] = jnp.concatenate([kg, jnp.zeros_like(kg)], axis=0)


def _nsa_prep(proj, pos128, invf, b, t):
    tm = PREP_TM
    assert tm == ATT_TK
    nt = t // tm
    row = lambda bi, ti: (bi * nt + ti, 0)
    kv0 = COL_KV // LANES
    in_specs = [pl.BlockSpec((tm, LANES), row),
                pl.BlockSpec((1, LANES), lambda bi, ti: (0, 0)),
                pl.BlockSpec((tm, NSA_WIDTH), lambda bi, ti: (bi * nt + ti, COL_Q // NSA_WIDTH))]
    for k in range(6):
        in_specs.append(pl.BlockSpec((tm, LANES), functools.partial(lambda bi, ti, k: (bi * nt + ti, kv0 + k), k=k)))
    head = lambda bi, ti: (bi, 0, ti, 0)
    tile5 = lambda bi, ti: (bi, 0, ti, 0, 0)
    g = NSA_KV_GROUPS
    out_specs = [pl.BlockSpec((1, NSA_HEADS, tm, HEAD_DIM), head),
                 pl.BlockSpec((2, 1, g, tm // CMP_STRIDE, CMP_STRIDE * HEAD_DIM), lambda bi, ti: (0, bi, 0, ti, 0)),
                 pl.BlockSpec((1, g, 1, KX_WIDTH, tm), tile5), pl.BlockSpec((1, g, tm, LANES), head),
                 pl.BlockSpec((1, g, tm // WIN_TK, HEAD_DIM, WIN_TK), tile5), pl.BlockSpec((1, g, tm, LANES), head)]
    out_shape = [jax.ShapeDtypeStruct((b, NSA_HEADS, t, HEAD_DIM), bf16),
                 jax.ShapeDtypeStruct((2, b, g, t // CMP_STRIDE, CMP_STRIDE * HEAD_DIM), f32),
                 jax.ShapeDtypeStruct((b, g, nt, KX_WIDTH, tm), bf16), jax.ShapeDtypeStruct((b, g, t, LANES), bf16),
                 jax.ShapeDtypeStruct((b, g, t // WIN_TK, HEAD_DIM, WIN_TK), bf16),
                 jax.ShapeDtypeStruct((b, g, t, LANES), bf16)]
    return pl.pallas_call(
        _nsa_prep_kernel,
        grid=(b, nt),
        in_specs=in_specs,
        out_specs=out_specs,
        out_shape=out_shape,
        compiler_params=_cparams(("parallel", "parallel")),
        name="nsa_prep",
    )(pos128, invf, proj, proj, proj, proj, proj, proj, proj)


def _cmp_mlp_kernel(a_ref, pe_ref, w1_ref, b1_ref, w2_ref, pos_ref, invf_ref, o_ref, ot_ref):
    kind = pl.program_id(0)
    a = a_ref[0, 0]
    nc = a.shape[0]
    u = _dot((a + pe_ref[0, 0]).astype(bf16), w1_ref[0, 0])
    v = _dot((a + pe_ref[0, 1]).astype(bf16), w1_ref[0, 1])
    v_next = pltpu.roll(v, nc - 1, 0)
    hid = jax.nn.gelu(u + v_next + b1_ref[0])
    out = _dot(hid.astype(bf16), w2_ref[0])
    cos, sin = _rope_tables(pos_ref[0], invf_ref[...])
    roped = _rope128(out, cos, sin)
    out = jnp.where(kind == 0, roped, out)
    o_ref[0, 0] = out[:, :HEAD_DIM].astype(bf16)
    ot_ref[0, 0] = out.T[:HEAD_DIM, :].astype(bf16)


def _cmp_mlp(a, pe, w1, b1, w2, posc, invf, b):
    _, bg, nc, hw = a.shape
    g = bg // b
    return pl.pallas_call(
        _cmp_mlp_kernel,
        grid=(2, bg),
        in_specs=[pl.BlockSpec((1, 1, nc, hw), lambda k, i: (k, i, 0, 0)),
                  pl.BlockSpec((1, 2, 1, hw), lambda k, i: (k, 0, 0, 0)),
                  pl.BlockSpec((1, 2, hw, CMP_HIDDEN), lambda k, i: (k, 0, 0, 0)),
                  pl.BlockSpec((1, 1, CMP_HIDDEN), lambda k, i: (k, 0, 0)),
                  pl.BlockSpec((1, CMP_HIDDEN, LANES), lambda k, i: (k, 0, 0)),
                  pl.BlockSpec((1, nc, LANES), lambda k, i: (i // g, 0, 0)),
                  pl.BlockSpec((1, LANES), lambda k, i: (0, 0))],
        out_specs=[pl.BlockSpec((1, 1, nc, HEAD_DIM), lambda k, i: (k, i, 0, 0)),
                   pl.BlockSpec((1, 1, HEAD_DIM, nc), lambda k, i: (k, i, 0, 0))],
        out_shape=[jax.ShapeDtypeStruct((2, bg, nc, HEAD_DIM), bf16),
                   jax.ShapeDtypeStruct((2, bg, HEAD_DIM, nc), bf16)],
        compiler_params=_cparams(("parallel", "parallel")),
        name="cmp_mlp",
    )(a, pe, w1, b1, w2, posc, invf)


def _cmp_attn_kernel(q_ref, kct_ref, vc_ref, cover_ref, oc_ref, sel_ref):
    qi = pl.program_id(2)
    nc = vc_ref.shape[2]
    rows = NSA_REP * CMP_TQ
    tq = qi * CMP_TQ + lax.broadcasted_iota(i32, (CMP_TQ, 1), 0)
    row_live = jnp.where(tq >= CMP_BLOCK - 1, 1.0, 0.0)
    tiny = jnp.finfo(f32).tiny

    def attend(ncols):
        kct = kct_ref[0, 0, :, :ncols]
        vc = vc_ref[0, 0, :ncols, :]
        cend = lax.broadcasted_iota(i32, (1, ncols), 1) * CMP_STRIDE + (CMP_BLOCK - 1)
        bias = jnp.where(cend <= tq, 0.0, MASK_NEG)
        s = _dot(q_ref[0].reshape(rows, HEAD_DIM), kct).reshape(NSA_REP, CMP_TQ, ncols) + bias[None]
        e = jnp.exp(s - jnp.max(s, axis=-1, keepdims=True))
        live = row_live[None]
        p = e * (live / jnp.maximum(live * jnp.sum(e, axis=-1, keepdims=True), tiny))
        oc_ref[0] = _dot(p.reshape(rows, ncols).astype(bf16), vc).reshape(NSA_REP, CMP_TQ, HEAD_DIM)
        psum = jnp.sum(p, axis=0)
        hi = psum.astype(bf16)
        lo = (psum - hi.astype(f32)).astype(bf16)
        cover = cover_ref[:ncols, :]
        imp = _dot(hi, cover) + _dot(lo, cover)

        j = lax.broadcasted_iota(i32, (CMP_TQ, LANES), 1)
        cur = tq // SEL_BLOCK
        forced = (j == 0) | (j == cur) | (j == cur - 1)
        valid = j * SEL_BLOCK <= tq
        imp = jnp.where(valid, jnp.where(forced, FORCED_SCORE, imp), -FORCED_SCORE)

        nblk = ncols * CMP_STRIDE // SEL_BLOCK
        jt = lax.broadcasted_iota(i32, (nblk, CMP_TQ), 0)

        def pick(_, carry):
            work, sel = carry
            m = jnp.max(work, axis=0, keepdims=True)
            first = jnp.min(jnp.where(work == m, jt, LANES), axis=0, keepdims=True)
            hit = jt == first
            return jnp.where(hit, -jnp.inf, work), jnp.where(hit, 1.0, sel)

        _, sel_t = lax.fori_loop(0, SEL_TOPK, pick, (imp.T[:nblk], jnp.zeros((nblk, CMP_TQ), f32)))
        if nblk < LANES:
            sel_t = jnp.concatenate([sel_t, jnp.zeros((LANES - nblk, CMP_TQ), f32)], axis=0)
        sel_ref[0, 0] = jnp.where(valid, jnp.where(sel_t.T > 0.0, 0.0, MASK_NEG), MASK_NEG).astype(bf16)

    need = (qi + 1) * (CMP_TQ // CMP_STRIDE)
    for ncols in range(LANES, nc + 1, LANES):
        @pl.when((need > ncols - LANES) & (need <= ncols))
        def _():
            attend(ncols)


def _cmp_attn(q_r, kvc, kvc_t, cover, b, t):
    g = NSA_KV_GROUPS
    nc = kvc.shape[2]
    nq = t // CMP_TQ
    return pl.pallas_call(
        _cmp_attn_kernel,
        grid=(b, g, nq),
        in_specs=[pl.BlockSpec((1, NSA_REP, CMP_TQ, HEAD_DIM), lambda bi, gi, qi: (bi, gi, qi, 0)),
                  pl.BlockSpec((1, 1, HEAD_DIM, nc), lambda bi, gi, qi: (0, bi * g + gi, 0, 0)),
                  pl.BlockSpec((1, 1, nc, HEAD_DIM), lambda bi, gi, qi: (1, bi * g + gi, 0, 0)),
                  pl.BlockSpec((nc, LANES), lambda bi, gi, qi: (0, 0))],
        out_specs=[pl.BlockSpec((1, NSA_REP, CMP_TQ, HEAD_DIM), lambda bi, gi, qi: (bi, gi, qi, 0)),
                   pl.BlockSpec((1, 1, CMP_TQ, LANES), lambda bi, gi, qi: (bi, gi, qi, 0))],
        out_shape=[jax.ShapeDtypeStruct((b, NSA_HEADS, t, HEAD_DIM), f32),
                   jax.ShapeDtypeStruct((b, g, t, LANES), bf16)],
        compiler_params=_cparams(("parallel", "parallel", "parallel")),
        name="cmp_attn",
    )(q_r, kvc_t, kvc, cover)


def _nsa_attn_kernel(q_ref, kx_ref, vs_ref, kw_ref, vw_ref, sel_ref, oc_ref, gate_ref,
                     o_ref, qx_ref, s_ref, m_ref, acc_ref, yw_ref):
    gi = pl.program_id(1)
    qi = pl.program_id(2)
    start = qi * ATT_TQ
    tq = start + lax.broadcasted_iota(i32, (ATT_TQ, 1), 0)

    selb = sel_ref[0, 0]
    for r in range(NSA_REP):
        qx_ref[r * ATT_TQ:(r + 1) * ATT_TQ, 0:LANES] = selb
        qr = q_ref[0, r]
        qx_ref[r * ATT_TQ:(r + 1) * ATT_TQ, LANES:2 * LANES] = jnp.concatenate([qr, jnp.zeros_like(qr)], axis=1)
    rows = NSA_REP * ATT_TQ

    m_ref[...] = jnp.full(m_ref.shape, MASK_NEG, f32)
    acc_ref[...] = jnp.zeros(acc_ref.shape, f32)

    def consume(kt, v_ref, bias, parts=1):
        k0 = pl.multiple_of(kt * ATT_TK, ATT_TK)
        v = v_ref[0, 0, pl.ds(k0, ATT_TK), :]
        heads = NSA_REP // parts
        for part in range(parts):
            rs = slice(part * heads * ATT_TQ, (part + 1) * heads * ATT_TQ)
            s = s_ref[rs]
            if bias is not None:
                s = (s.reshape(heads, ATT_TQ, ATT_TK) + bias[None]).reshape(heads * ATT_TQ, ATT_TK)
            m_prev = m_ref[rs]
            m_new = jnp.maximum(m_prev, jnp.max(s, axis=-1, keepdims=True))
            alpha = jnp.exp(m_prev - m_new)
            p = jnp.exp(s - jnp.concatenate([m_new] * (ATT_TK // LANES), axis=1))
            acc_ref[rs] = alpha * acc_ref[rs] + _dot(p.astype(bf16), v)
            m_ref[rs] = m_new

    def kpos(kt):
        return kt * ATT_TK + lax.broadcasted_iota(i32, (1, ATT_TK), 1)

    last = start // ATT_TK
    for part in range(2):
        rs = slice(part * rows // 2, (part + 1) * rows // 2)
        s_ref[rs] = _dot(qx_ref[rs], kx_ref[0, 0, 0])

    def sel_step(kt, _):
        consume(kt, vs_ref, None)
        s_ref[...] = _dot(qx_ref[...], kx_ref[0, 0, kt + 1])
        return 0

    n_win = (WINDOW + ATT_TQ) // WIN_TK
    w_first = jnp.maximum(start // WIN_TK - WINDOW // WIN_TK, 0)
    kw = jnp.concatenate([kw_ref[0, 0, w_first + i] for i in range(n_win)], axis=1)
    w0 = pl.multiple_of(w_first * WIN_TK, WIN_TK)
    vw = vw_ref[0, 0, pl.ds(w0, n_win * WIN_TK), :]
    kp = w0 + lax.broadcasted_iota(i32, (1, n_win * WIN_TK), 1)
    wbias = jnp.where(kp <= tq, jnp.where(kp > tq - WINDOW, 0.0, MASK_NEG), MASK_NEG)
    hh = WIN_HEADS
    gates = jax.nn.sigmoid(gate_ref[...])
    per_group = 3 * NSA_REP
    shifted = gates
    for gg in range(1, NSA_KV_GROUPS):
        shifted = jnp.where(gi == gg, pltpu.roll(gates, LANES - gg * per_group, 1), shifted)

    def gate(r, br):
        c = r * 3 + br
        return jnp.broadcast_to(shifted[:, c:c + 1], (ATT_TQ, HEAD_DIM))

    upper = pltpu.roll(shifted, HEAD_DIM, 1)[:, HEAD_DIM:]

    def gated_output(a, r, br):
        c = r * 3 + br
        ratio = upper / a[:, HEAD_DIM:]
        return a[:, :HEAD_DIM] * jnp.broadcast_to(ratio[:, c:c + 1], (ATT_TQ, HEAD_DIM))

    for half in range(NSA_REP // hh):
        qh = q_ref[0, half * hh:(half + 1) * hh].reshape(hh * ATT_TQ, HEAD_DIM)
        sw = _dot(qh, kw).reshape(hh, ATT_TQ, n_win * WIN_TK) + wbias[None]
        sw = sw.reshape(hh * ATT_TQ, n_win * WIN_TK)
        pw = jnp.exp(sw - jnp.max(sw, axis=-1, keepdims=True))
        aw = _dot(pw.astype(bf16), vw)
        for i in range(hh):
            r = half * hh + i
            a = aw[i * ATT_TQ:(i + 1) * ATT_TQ]
            yw_ref[r] = gated_output(a, r, 2)

    def sel_group(j, _):
        for i in range(SEL_UNROLL):
            sel_step(SEL_UNROLL * j + i, 0)
        return 0

    lax.fori_loop(0, last // SEL_UNROLL, sel_group, 0)
    lax.fori_loop(last // SEL_UNROLL * SEL_UNROLL, last, sel_step, 0)

    consume(last, vs_ref, jnp.where(kpos(last) <= tq, 0.0, MASK_NEG), parts=2)

    for r in range(NSA_REP):
        osel = gated_output(acc_ref[r * ATT_TQ:(r + 1) * ATT_TQ], r, 1)
        o_ref[:, r * HEAD_DIM:(r + 1) * HEAD_DIM] = (gate(r, 0) * oc_ref[0, r] + osel + yw_ref[r]).astype(o_ref.dtype)


def _nsa_attn(q_r, kx, vs, kw, vw, selb, oc, proj, b, t):
    g = NSA_KV_GROUPS
    nq = t // ATT_TQ
    once = dict(pipeline_mode=pl.Buffered(1))
    vspec = pl.BlockSpec((1, 1, t, LANES), lambda bi, gi, qi: (bi, gi, 0, 0), **once)
    hspec = pl.BlockSpec((1, NSA_REP, ATT_TQ, HEAD_DIM), lambda bi, gi, qi: (bi, gi, qi, 0))
    return pl.pallas_call(
        _nsa_attn_kernel,
        grid=(b, g, nq),
        in_specs=[hspec,
                  pl.BlockSpec((1, 1, t // ATT_TK, KX_WIDTH, ATT_TK), lambda bi, gi, qi: (bi, gi, 0, 0, 0), **once),
                  vspec,
                  pl.BlockSpec((1, 1, t // WIN_TK, HEAD_DIM, WIN_TK), lambda bi, gi, qi: (bi, gi, 0, 0, 0), **once),
                  vspec,
                  pl.BlockSpec((1, 1, ATT_TQ, LANES), lambda bi, gi, qi: (bi, gi, qi, 0)),
                  hspec,
                  pl.BlockSpec((ATT_TQ, LANES), lambda bi, gi, qi: (bi * nq + qi, COL_SMALL // LANES))],
        out_specs=pl.BlockSpec((ATT_TQ, NSA_REP * HEAD_DIM), lambda bi, gi, qi: (bi * nq + qi, gi)),
        out_shape=jax.ShapeDtypeStruct((b * t, NSA_WIDTH), bf16),
        scratch_shapes=[pltpu.VMEM((NSA_REP * ATT_TQ, KX_WIDTH), bf16),
                        pltpu.VMEM((NSA_REP * ATT_TQ, ATT_TK), f32),
                        pltpu.VMEM((NSA_REP * ATT_TQ, LANES), f32),
                        pltpu.VMEM((NSA_REP * ATT_TQ, LANES), f32),
                        pltpu.VMEM((NSA_REP, ATT_TQ, HEAD_DIM), f32)],
        compiler_params=_cparams(("parallel", "parallel", "arbitrary")),
        name="nsa_attn",
    )(q_r, kx, vs, kw, vw, selb, oc, proj)


HALO = 8


def _ssd_kernel(xbc_ref, z_ref, small_ref, dtt_ref, cw_ref, cb_ref, dtb_r_ref, dtb_c_ref,
                alog_r_ref, alog_c_ref, dskip_ref, nw_ref, o_ref, ext_ref, st_ref):
    c = pl.program_id(1)
    L = CHUNK

    @pl.when(c == 0)
    def _():
        ext_ref[0:HALO, :] = jnp.zeros((HALO, XBC_WIDTH), f32)
        st_ref[...] = jnp.zeros_like(st_ref)

    ext_ref[HALO:HALO + L, :] = xbc_ref[...]
    conv = cb_ref[...]
    for k in range(CONV_WIDTH):
        off = HALO - (CONV_WIDTH - 1) + k
        conv = conv + cw_ref[k:k + 1, :] * ext_ref[off:off + L, :]
    ext_ref[0:HALO, :] = ext_ref[L:L + HALO, :]
    act = conv * jax.nn.sigmoid(conv)
    xs = act[:, :SSM_WIDTH]
    bm = act[:, SSM_WIDTH:SSM_WIDTH + SSM_GROUPS * SSM_STATE]
    cm = act[:, SSM_WIDTH + SSM_GROUPS * SSM_STATE:]

    dt_c = jax.nn.softplus(small_ref[:, SMALL_DT_OFF:SMALL_DT_OFF + SSM_HEADS] + dtb_r_ref[...])
    dt_r = jax.nn.softplus(dtt_ref[0] + dtb_c_ref[...])
    a_r = -jnp.exp(alog_r_ref[...])
    a_c = -jnp.exp(alog_c_ref[...])
    row = lax.broadcasted_iota(i32, (L, L), 0)
    col = lax.broadcasted_iota(i32, (L, L), 1)
    causal = col <= row
    tri = jnp.where(causal, 1.0, 0.0)
    acs_c = _dot_hi(tri, dt_c * a_r)
    acs_r = _dot_hi(dt_r * a_c, jnp.where(row <= col, 1.0, 0.0))

    z = z_ref[...]
    for g in range(SSM_GROUPS):
        cg = cm[:, g * SSM_STATE:(g + 1) * SSM_STATE].astype(bf16)
        bg = bm[:, g * SSM_STATE:(g + 1) * SSM_STATE]
        cb = _dot_t(cg, bg.astype(bf16))
        bg_t = bg.T
        ssq = jnp.zeros((L, 1), f32)
        yg = []
        for pp in range(SSM_REP // 2):
            pair = g * (SSM_REP // 2) + pp
            ps = slice(pair * LANES, (pair + 1) * LANES)
            x_p = xs[:, ps]
            low = lax.broadcasted_iota(i32, (L, LANES), 1) < SSM_HEADDIM
            low_n = lax.broadcasted_iota(i32, (SSM_STATE, LANES), 1) < SSM_HEADDIM
            x_own = (jnp.where(low, x_p, 0.0).astype(bf16), jnp.where(low, 0.0, x_p).astype(bf16))
            st = st_ref[pair]
            y = jnp.zeros((L, LANES), f32)
            st_in = jnp.zeros((SSM_STATE, LANES), f32)
            grow, keep = [], []
            for j in range(2):
                h = 2 * pair + j
                a_col = acs_c[:, h:h + 1]
                a_row = acs_r[h:h + 1, :]
                dt_row = dt_r[h:h + 1, :]
                a_last = acs_r[h:h + 1, L - 1:L]
                seg = a_col - a_row
                decay = jnp.where(causal, jnp.exp(jnp.where(causal, seg, 0.0)), 0.0)
                w = cb * decay * dt_row
                y = y + _dot(w.astype(bf16), x_own[j])
                bscaled = bg_t * (jnp.exp(a_last - a_row) * dt_row)
                st_in = st_in + _dot(bscaled.astype(bf16), x_own[j])
                grow.append(jnp.exp(a_col))
                keep.append(jnp.exp(a_last))
            y = y + _dot(cg, st.astype(bf16)) * jnp.where(low, grow[0], grow[1])
            st_ref[pair] = jnp.where(low_n, keep[0], keep[1]) * st + st_in
            y = y + dskip_ref[:, ps] * x_p
            zp = z[:, ps]
            y = y * (zp * jax.nn.sigmoid(zp))
            ssq = ssq + jnp.sum(y * y, axis=-1, keepdims=True)
            yg.append(y)
        rs = lax.rsqrt(ssq / (SSM_REP * SSM_HEADDIM) + NORM_EPS)
        for pp in range(SSM_REP // 2):
            pair = g * (SSM_REP // 2) + pp
            ps = slice(pair * LANES, (pair + 1) * LANES)
            o_ref[:, ps] = (yg[pp] * rs * nw_ref[:, ps]).astype(o_ref.dtype)


def _ssd(proj, dtt, cw, cb, dtb_r, dtb_c, alog_r, alog_c, dskip, nw, b, t):
    nch = t // CHUNK
    row = lambda bi, ci: bi * nch + ci
    const2 = lambda bi, ci: (0, 0)
    return pl.pallas_call(
        _ssd_kernel,
        grid=(b, nch),
        in_specs=[pl.BlockSpec((CHUNK, XBC_WIDTH), lambda bi, ci: (row(bi, ci), COL_XBC // XBC_WIDTH)),
                  pl.BlockSpec((CHUNK, SSM_WIDTH), lambda bi, ci: (row(bi, ci), COL_Z // SSM_WIDTH)),
                  pl.BlockSpec((CHUNK, LANES), lambda bi, ci: (row(bi, ci), COL_SMALL // LANES)),
                  pl.BlockSpec((1, SSM_HEADS, CHUNK), lambda bi, ci: (bi, 0, ci)),
                  pl.BlockSpec((CONV_WIDTH, XBC_WIDTH), const2),
                  pl.BlockSpec((1, XBC_WIDTH), const2),
                  pl.BlockSpec((1, SSM_HEADS), const2),
                  pl.BlockSpec((SSM_HEADS, 1), const2),
                  pl.BlockSpec((1, SSM_HEADS), const2),
                  pl.BlockSpec((SSM_HEADS, 1), const2),
                  pl.BlockSpec((1, SSM_WIDTH), const2),
                  pl.BlockSpec((1, SSM_WIDTH), const2)],
        out_specs=pl.BlockSpec((CHUNK, SSM_WIDTH), lambda bi, ci: (row(bi, ci), 0)),
        out_shape=jax.ShapeDtypeStruct((b * t, SSM_WIDTH), bf16),
        scratch_shapes=[pltpu.VMEM((HALO + CHUNK, XBC_WIDTH), f32),
                        pltpu.VMEM((SSM_HEADS // 2, SSM_STATE, 2 * SSM_HEADDIM), f32)],
        compiler_params=_cparams(("parallel", "arbitrary")),
        name="ssd",
    )(proj, proj, proj, dtt, cw, cb, dtb_r, dtb_c, alog_r, alog_c, dskip, nw)


def _layer_norm(v, g, b):
    mu = jnp.mean(v, axis=-1, keepdims=True)
    d = v - mu
    var = jnp.mean(d * d, axis=-1, keepdims=True)
    return d * lax.rsqrt(var + NORM_EPS) * g + b


def _outproj_kernel(alpha, ya_ref, yb_ref, x_ref, wa_ref, wb_ref, g_ref, b_ref, wr_ref, br_ref,
                    h_ref, route_ref, cnt_ref):
    i = pl.program_id(0)
    tm = x_ref.shape[0]
    mix = _dot(ya_ref[...], wa_ref[...]) + _dot(yb_ref[...], wb_ref[...])
    h = _layer_norm(alpha * x_ref[...] + mix, g_ref[...], b_ref[...])
    h_ref[...] = h

    h_hi = h.astype(bf16)
    h_lo = (h - h_hi.astype(f32)).astype(bf16)
    t = _dot(h_hi, wr_ref[...])
    logits = t[:, :LANES] + t[:, LANES:] + _dot(h_lo, wr_ref[:, :LANES]) + br_ref[...]
    lane = lax.broadcasted_iota(i32, (tm, LANES), 1)
    ninf = -jnp.inf
    gmask = lane < N_EXPERT_GROUPS
    gl = jnp.where(gmask, logits, ninf)
    ge = jnp.where(gmask, jnp.exp(gl - jnp.max(gl, axis=-1, keepdims=True)), 0.0)
    pg = ge / jnp.sum(ge, axis=-1, keepdims=True)
    g_gate = jnp.max(pg, axis=-1, keepdims=True)
    g_sel = jnp.min(jnp.where(gmask & (pg == g_gate), lane, LANES), axis=-1, keepdims=True)
    lo = N_EXPERT_GROUPS + g_sel * EXPERTS_PER_GROUP
    emask = (lane >= lo) & (lane < lo + EXPERTS_PER_GROUP)
    el = jnp.where(emask, logits, ninf)
    ee = jnp.where(emask, jnp.exp(el - jnp.max(el, axis=-1, keepdims=True)), 0.0)
    pe = ee / jnp.sum(ee, axis=-1, keepdims=True)
    p0 = jnp.max(pe, axis=-1, keepdims=True)
    l0 = jnp.min(jnp.where(emask & (pe == p0), lane, LANES), axis=-1, keepdims=True)
    rest = jnp.where(emask & (lane != l0), pe, ninf)
    p1 = jnp.max(rest, axis=-1, keepdims=True)
    l1 = jnp.min(jnp.where(rest == p1, lane, LANES), axis=-1, keepdims=True)
    psum = p0 + p1
    w0 = g_gate * p0 / psum
    w1 = g_gate * p1 / psum
    e0 = l0 - N_EXPERT_GROUPS
    e1 = l1 - N_EXPERT_GROUPS

    @pl.when(i == 0)
    def _():
        cnt_ref[...] = jnp.zeros_like(cnt_ref)

    oh0 = lane == e0
    oh1 = lane == e1
    both = jnp.where(oh0, 1.0, 0.0) + jnp.where(oh1, 1.0, 0.0)
    r_i = lax.broadcasted_iota(i32, (tm, tm), 0)
    c_i = lax.broadcasted_iota(i32, (tm, tm), 1)
    strict = jnp.where(c_i < r_i, 1.0, 0.0).astype(bf16)
    before = _dot(strict, both.astype(bf16)) + cnt_ref[...]
    rank0 = jnp.sum(jnp.where(oh0, before, 0.0), axis=-1, keepdims=True)
    rank1 = jnp.sum(jnp.where(oh1, before, 0.0), axis=-1, keepdims=True)
    cnt_ref[...] = cnt_ref[...] + jnp.sum(both, axis=0, keepdims=True)

    out = jnp.where(lane == 0, e0.astype(f32), 0.0)
    out = jnp.where(lane == 1, e1.astype(f32), out)
    out = jnp.where(lane == 2, w0, out)
    out = jnp.where(lane == 3, w1, out)
    out = jnp.where(lane == 4, rank0, out)
    out = jnp.where(lane == 5, rank1, out)
    route_ref[...] = out


def _outproj(y_nsa, y_ssm, x2, wa, wb, g, bta, wr, br, alpha):
    n, d = x2.shape
    tm = OUT_TM
    const = lambda i: (0, 0)
    rowb = lambda i: (i, 0)
    return pl.pallas_call(
        functools.partial(_outproj_kernel, alpha),
        grid=(n // tm,),
        in_specs=[pl.BlockSpec((tm, NSA_WIDTH), rowb), pl.BlockSpec((tm, SSM_WIDTH), rowb),
                  pl.BlockSpec((tm, d), rowb),
                  pl.BlockSpec((NSA_WIDTH, d), const), pl.BlockSpec((SSM_WIDTH, d), const),
                  pl.BlockSpec((1, d), const), pl.BlockSpec((1, d), const),
                  pl.BlockSpec((d, 2 * LANES), const), pl.BlockSpec((1, LANES), const)],
        out_specs=[pl.BlockSpec((tm, d), rowb), pl.BlockSpec((tm, LANES), rowb),
                   pl.BlockSpec((1, LANES), const)],
        out_shape=[jax.ShapeDtypeStruct((n, d), f32), jax.ShapeDtypeStruct((n, LANES), f32),
                   jax.ShapeDtypeStruct((1, LANES), f32)],
        compiler_params=_cparams(("arbitrary",)),
        name="outproj",
    )(y_nsa, y_ssm, x2, wa, wb, g, bta, wr, br)


def _row_copy(src_ref, src_row, dst_ref, dst_row, sem):
    return pltpu.make_async_copy(src_ref.at[pl.ds(src_row, 1), :], dst_ref.at[pl.ds(dst_row, 1), :], sem)


def _pack_bf16_pairs(x):
    half = x.shape[1] // 2
    hi = pltpu.bitcast(x[:, :half].astype(bf16).astype(f32), jnp.uint32)
    lo = pltpu.bitcast(x[:, half:].astype(bf16).astype(f32), jnp.uint32)
    return hi | (lo >> 16)


def _unpack_bf16_pairs(words):
    return (pltpu.bitcast(words & jnp.uint32(0xFFFF0000), f32), pltpu.bitcast(words << 16, f32))


def _dest_row(dest_ref, tm, k, r):
    return dest_ref[0, 0, k * tm + r]


def _dispatch_kernel(dest_ref, zflag_ref, h_ref, xs_ref, zero_ref, tile_ref, pk_ref, sem, lsem, zsem):
    tm = DISP_TM
    nb = zflag_ref.shape[0]
    i = pl.program_id(0)

    @pl.when(i == 0)
    def _():
        zero_ref[...] = jnp.zeros_like(zero_ref)

        def zblock(i):
            rows = pl.ds(pl.multiple_of(i * MOE_TM, MOE_TM), MOE_TM)
            return pltpu.make_async_copy(zero_ref, xs_ref.at[rows, :], zsem)

        def zstart(i, _):
            @pl.when(zflag_ref[i] != 0)
            def _():
                zblock(i).start()
            return 0

        def zwait(i, _):
            @pl.when(zflag_ref[i] != 0)
            def _():
                zblock(i).wait()
            return 0

        lax.fori_loop(0, nb, zstart, 0)
        lax.fori_loop(0, nb, zwait, 0)

    nslot = tile_ref.shape[0]
    nsteps = pl.num_programs(0)
    slot = i % nslot

    def tile_load(step, s):
        rows = pl.ds(pl.multiple_of(step * tm, tm), tm)
        return pltpu.make_async_copy(h_ref.at[rows, :], tile_ref.at[s], lsem.at[s])

    @pl.when(i == 0)
    def _():
        tile_load(0, 0).start()

    tile_load(i, slot).wait()

    @pl.when(i + 1 < nsteps)
    def _():
        tile_load(i + 1, (i + 1) % nslot).start()

    pk_ref[slot] = _pack_bf16_pairs(tile_ref[slot])

    def issue(r, _):
        for k in range(2):
            _row_copy(pk_ref.at[slot], r, xs_ref, _dest_row(dest_ref, tm, k, r), sem.at[slot]).start(priority=k)
        return 0

    lax.fori_loop(0, tm, issue, 0, unroll=16)

    def wait_rows(s):
        for k in range(2):
            pltpu.make_async_copy(pk_ref.at[s], xs_ref.at[pl.ds(0, tm), :], sem.at[s]).wait()

    @pl.when(i > 0)
    def _():
        wait_rows((i - 1) % nslot)

    @pl.when(i == nsteps - 1)
    def _():
        wait_rows(slot)


def _dispatch(dest_t, zflag, h, p_rows):
    n, d = h.shape
    tm = DISP_TM
    return pl.pallas_call(
        _dispatch_kernel,
        grid=(n // tm,),
        in_specs=[pl.BlockSpec((1, 1, 2 * tm), lambda i: (i, 0, 0), memory_space=pltpu.SMEM),
                  pl.BlockSpec(memory_space=pltpu.SMEM),
                  pl.BlockSpec(memory_space=pl.ANY)],
        out_specs=pl.BlockSpec(memory_space=pl.ANY),
        out_shape=jax.ShapeDtypeStruct((p_rows, d // 2), jnp.uint32),
        scratch_shapes=[pltpu.VMEM((MOE_TM, d // 2), jnp.uint32), pltpu.VMEM((3, tm, d), f32),
                        pltpu.VMEM((3, tm, d // 2), jnp.uint32),
                        pltpu.SemaphoreType.DMA((3,)), pltpu.SemaphoreType.DMA((3,)), pltpu.SemaphoreType.DMA(())],
        compiler_params=_cparams(("arbitrary",)),
        name="dispatch",
    )(dest_t, zflag, h)


def _experts_kernel(be_ref, nu_ref, first_ref, next_ref, slot_ref, xs_ref, wg_ref, wu_ref, wd_ref, y_ref,
                    fg_ref, fu_ref, fd_ref, wgb_ref, wub_ref, wdb_ref, sems):
    i = pl.program_id(0)

    def weight_loads(e, s):
        return (pltpu.make_async_copy(wg_ref.at[e], fg_ref.at[s], sems.at[s, 0]),
                pltpu.make_async_copy(wu_ref.at[e], fu_ref.at[s], sems.at[s, 1]),
                pltpu.make_async_copy(wd_ref.at[e], fd_ref.at[s], sems.at[s, 2]))

    @pl.when(i == 0)
    def _():
        for c in weight_loads(be_ref[0], 0):
            c.start()

    @pl.when(first_ref[i] != 0)
    def _():
        s = slot_ref[i]
        for c in weight_loads(be_ref[i], s):
            c.wait()
        wgb_ref[...] = fg_ref[s].astype(bf16)
        wub_ref[...] = fu_ref[s].astype(bf16)
        wdb_ref[...] = fd_ref[s].astype(bf16)

        @pl.when(next_ref[i] >= 0)
        def _():
            for c in weight_loads(next_ref[i], 1 - s):
                c.start()

    @pl.when(i < nu_ref[0])
    def _():
        first, second = _unpack_bf16_pairs(xs_ref[...])
        xb = jnp.concatenate([first.astype(bf16), second.astype(bf16)], axis=1)
        gte = _dot(xb, wgb_ref[...])
        up = _dot(xb, wub_ref[...])
        act = gte * jax.nn.sigmoid(gte) * up
        y = _dot(act.astype(bf16), wdb_ref[...])
        y_ref[...] = _pack_bf16_pairs(y)

    @pl.when(i >= nu_ref[0])
    def _():
        y_ref[...] = jnp.zeros_like(y_ref)


def _experts(block_e, n_used, first, next_e, slot, xs, w_gate, w_up, w_down):
    p_rows = xs.shape[0]
    d = w_gate.shape[1]
    tm = MOE_TM
    nb = p_rows // tm
    de = w_gate.shape[-1]
    xmap = lambda i, be, nu, fi, ne, sl: (jnp.maximum(jnp.minimum(i, nu[0] - 1), 0), 0)
    hbm = pl.BlockSpec(memory_space=pl.ANY)
    return pl.pallas_call(
        _experts_kernel,
        grid_spec=pltpu.PrefetchScalarGridSpec(
            num_scalar_prefetch=5,
            grid=(nb,),
            in_specs=[pl.BlockSpec((tm, d // 2), xmap), hbm, hbm, hbm],
            out_specs=pl.BlockSpec((tm, d // 2), lambda i, be, nu, fi, ne, sl: (i, 0)),
            scratch_shapes=[pltpu.VMEM((2, d, de), f32), pltpu.VMEM((2, d, de), f32), pltpu.VMEM((2, de, d), f32),
                            pltpu.VMEM((d, de), bf16), pltpu.VMEM((d, de), bf16), pltpu.VMEM((de, d), bf16),
                            pltpu.SemaphoreType.DMA((2, 3))],
        ),
        out_shape=jax.ShapeDtypeStruct((p_rows, d // 2), jnp.uint32),
        compiler_params=_cparams(("arbitrary",)),
        name="experts",
    )(block_e, n_used, first, next_e, slot, xs, w_gate, w_up, w_down)


def _combine_kernel(alpha, dest_ref, ndest_ref, route_ref, h_ref, g_ref, b_ref, y_ref, o_ref, buf_ref, sems):
    tm = h_ref.shape[0]
    i = pl.program_id(0)
    slot = i % 2

    def start_row(ids_ref, s, r):
        for k in range(2):
            _row_copy(y_ref, _dest_row(ids_ref, tm, k, r), buf_ref.at[s, k], r, sems.at[s]).start(priority=k)

    def wait_tile():
        for k in range(2):
            pltpu.make_async_copy(y_ref.at[pl.ds(0, tm), :], buf_ref.at[slot, k], sems.at[slot]).wait()

    def finish_tile():
        route = route_ref[...]
        lo0, hi0 = _unpack_bf16_pairs(buf_ref[slot, 0])
        lo1, hi1 = _unpack_bf16_pairs(buf_ref[slot, 1])
        w0, w1 = route[:, 2:3], route[:, 3:4]
        ffn = jnp.concatenate([w0 * lo0 + w1 * lo1, w0 * hi0 + w1 * hi1], axis=1)
        o_ref[...] = _layer_norm(alpha * h_ref[...] + ffn, g_ref[...], b_ref[...])

    @pl.when(i == 0)
    def _():
        def issue(r, _):
            start_row(dest_ref, 0, r)
            return 0
        lax.fori_loop(0, tm, issue, 0, unroll=4)

    @pl.when(i + 1 < pl.num_programs(0))
    def _():
        wait_tile()
        for r in range(tm):
            start_row(ndest_ref, 1 - slot, r)
        finish_tile()

    @pl.when(i + 1 == pl.num_programs(0))
    def _():
        wait_tile()
        finish_tile()


def _combine(dest_t, route, h, g, bta, y, alpha):
    n, d = h.shape
    tm = COMB_TM
    const = lambda i: (0, 0)
    return pl.pallas_call(
        functools.partial(_combine_kernel, alpha),
        grid=(n // tm,),
        in_specs=[pl.BlockSpec((1, 1, 2 * tm), lambda i: (i, 0, 0), memory_space=pltpu.SMEM),
                  pl.BlockSpec((1, 1, 2 * tm), lambda i: (jnp.minimum(i + 1, n // tm - 1), 0, 0),
                               memory_space=pltpu.SMEM),
                  pl.BlockSpec((tm, LANES), lambda i: (i, 0)),
                  pl.BlockSpec((tm, d), lambda i: (i, 0)),
                  pl.BlockSpec((1, d), const), pl.BlockSpec((1, d), const),
                  pl.BlockSpec(memory_space=pl.ANY)],
        out_specs=pl.BlockSpec((tm, d), lambda i: (i, 0)),
        out_shape=jax.ShapeDtypeStruct((n, d), f32),
        scratch_shapes=[pltpu.VMEM((2, 2, tm, d // 2), jnp.uint32), pltpu.SemaphoreType.DMA((2,))],
        compiler_params=_cparams(("arbitrary",)),
        name="combine",
    )(dest_t, dest_t, route, h, g, bta, y)


def _tile_dest(dest, tm):
    n = dest.shape[1]
    return dest.reshape(2, n // tm, tm).transpose(1, 0, 2).reshape(n // tm, 1, 2 * tm)


def _layer(x, positions, w_in, cmp_k_pe, cmp_k_w1, cmp_k_b1, cmp_k_w2, cmp_v_pe, cmp_v_w1, cmp_v_b1, cmp_v_w2,
           conv_w, conv_b, dt_bias, a_log, d_skip, ssm_norm_w, w_out, ln1_g, ln1_b,
           w_router_group, b_router_group, w_router_expert, b_router_expert, w_gate, w_up, w_down, ln2_g, ln2_b,
           alpha):
    b, t, d = x.shape
    n = b * t
    assert t % max(ATT_TK, ATT_TQ, CMP_TQ, CHUNK, PREP_TM) == 0 and n % max(PROJ_TM, OUT_TM, DISP_TM) == 0
    assert t // SEL_BLOCK <= LANES and (t // CMP_STRIDE) % LANES == 0 and t >= WINDOW + ATT_TQ
    x2 = x.reshape(n, d)

    c0 = NSA_WIDTH
    c1 = c0 + 6 * KV_WIDTH
    c2 = c1 + 3 * NSA_HEADS
    c3 = c2 + SSM_WIDTH
    c4 = c3 + XBC_WIDTH
    w_small = jnp.concatenate([w_in[:, c1:c2], w_in[:, c4:], jnp.zeros((d, LANES - 3 * NSA_HEADS - SSM_HEADS), f32)], axis=1)
    w_cat = jnp.concatenate([w_in[:, :c0], w_in[:, c2:c3], w_in[:, c3:c4], w_in[:, c0:c1], w_small,
                             jnp.zeros((d, PROJ_COLS - COL_SMALL - LANES), f32)], axis=1).astype(bf16)
    proj = _proj(x2, w_cat)

    lane = np.arange(LANES) % HEAD_DIM
    inv_freq = ROPE_THETA ** (-jnp.arange(0, ROT_DIM, 2, dtype=f32) / ROT_DIM)
    invf = jnp.where(lane < ROT_DIM, inv_freq[lane % (ROT_DIM // 2)], 0.0).astype(f32)[None, :]
    pos128 = jnp.broadcast_to(positions.reshape(n, 1), (n, LANES))
    q_r, cmp_in, k_sel, v_sel, k_win, v_win = _nsa_prep(proj, pos128, invf, b, t)

    nc = t // CMP_STRIDE
    half_w = CMP_STRIDE * HEAD_DIM
    a = cmp_in.reshape(2, b * NSA_KV_GROUPS, nc, half_w)
    pe = jnp.stack([cmp_k_pe, cmp_v_pe]).reshape(2, 2, 1, half_w)
    w1 = jnp.stack([cmp_k_w1, cmp_v_w1]).reshape(2, 2, half_w, CMP_HIDDEN).astype(bf16)
    b1 = jnp.stack([cmp_k_b1, cmp_v_b1]).reshape(2, 1, CMP_HIDDEN)
    w2 = jnp.pad(jnp.stack([cmp_k_w2, cmp_v_w2]), ((0, 0), (0, 0), (0, LANES - HEAD_DIM))).astype(bf16)
    cend = jnp.minimum(jnp.arange(nc) * CMP_STRIDE + CMP_BLOCK - 1, t - 1)
    posc = jnp.broadcast_to(positions[:, cend][:, :, None], (b, nc, LANES))
    kvc, kvc_t = _cmp_mlp(a, pe, w1, b1, w2, posc, invf, b)

    c_start = np.arange(nc)[:, None] * CMP_STRIDE
    s_start = np.arange(LANES)[None, :] * SEL_BLOCK
    cover = ((c_start < s_start + SEL_BLOCK) & (c_start + CMP_BLOCK > s_start)
             & (np.arange(nc)[:, None] < nc - 1) & (np.arange(LANES)[None, :] < t // SEL_BLOCK))
    cover = jnp.asarray(cover, bf16)
    o_cmp, selb = _cmp_attn(q_r, kvc, kvc_t, cover, b, t)

    y_nsa = _nsa_attn(q_r, k_sel, v_sel, k_win, v_win, selb, o_cmp, proj, b, t)

    dt_raw = proj[:, COL_SMALL + SMALL_DT_OFF:COL_SMALL + SMALL_DT_OFF + SSM_HEADS]
    dtt = dt_raw.reshape(b, t, SSM_HEADS).transpose(0, 2, 1)
    y_ssm = _ssd(proj, dtt, conv_w.reshape(CONV_WIDTH, XBC_WIDTH), conv_b.reshape(1, XBC_WIDTH),
                 dt_bias.reshape(1, SSM_HEADS), dt_bias.reshape(SSM_HEADS, 1),
                 a_log.reshape(1, SSM_HEADS), a_log.reshape(SSM_HEADS, 1),
                 jnp.repeat(d_skip, SSM_HEADDIM).reshape(1, SSM_WIDTH), ssm_norm_w.reshape(1, SSM_WIDTH), b, t)

    wr = jnp.concatenate([w_router_group, w_router_expert,
                          jnp.zeros((d, LANES - N_EXPERT_GROUPS - N_EXPERTS), f32)], axis=1)
    br = jnp.concatenate([b_router_group, b_router_expert,
                          jnp.zeros((LANES - N_EXPERT_GROUPS - N_EXPERTS,), f32)])[None, :]
    wr_hi = wr.astype(bf16)
    wr = jnp.concatenate([wr_hi, (wr - wr_hi.astype(f32)).astype(bf16)], axis=1)
    wo = w_out.astype(bf16)
    h, route, counts = _outproj(y_nsa, y_ssm, x2, wo[:NSA_WIDTH], wo[NSA_WIDTH:], ln1_g[None, :], ln1_b[None, :],
                                wr, br, alpha)

    cnt = counts[0, :N_EXPERTS].astype(i32)
    padded = (cnt + MOE_TM - 1) // MOE_TM * MOE_TM
    pad_ends = jnp.cumsum(padded)
    pad_starts = pad_ends - padded
    e01 = route[:, 0:2].astype(i32).T
    start01 = jnp.sum(jnp.where(e01[..., None] == jnp.arange(N_EXPERTS, dtype=i32), pad_starts, 0), axis=-1)
    dest = start01 + route[:, 4:6].astype(i32).T
    p_rows = 2 * n + N_EXPERTS * MOE_TM
    nb = p_rows // MOE_TM
    block_e = jnp.minimum(jnp.sum(jnp.arange(nb, dtype=i32)[:, None] * MOE_TM >= pad_ends[None, :], axis=-1),
                          N_EXPERTS - 1).astype(i32)
    n_used = (pad_ends[-1] // MOE_TM).astype(i32).reshape(1)
    blk = jnp.arange(nb, dtype=i32)
    last_of_expert = jnp.any((blk[:, None] + 1) * MOE_TM == pad_ends[None, :], axis=-1)
    zflag = (last_of_expert | (blk >= n_used[0])).astype(i32)
    block_e = jnp.where(blk < n_used[0], block_e, block_e[jnp.maximum(n_used[0] - 1, 0)])

    xs = _dispatch(_tile_dest(dest, DISP_TM), zflag, h, p_rows)
    prev_e = jnp.concatenate([jnp.full((1,), -1, i32), block_e[:-1]])
    first = ((block_e != prev_e) & (blk < n_used[0])).astype(i32)
    eidx = jnp.arange(N_EXPERTS, dtype=i32)
    later = jnp.where((eidx[None, :] > eidx[:, None]) & (padded[None, :] > 0), eidx[None, :], N_EXPERTS)
    next_of = jnp.min(later, axis=1)
    next_e = jnp.where(next_of[block_e] < N_EXPERTS, next_of[block_e], -1).astype(i32)
    slot = ((jnp.cumsum(first) - 1) % 2).astype(i32)
    y = _experts(block_e, n_used, first, next_e, slot, xs, w_gate, w_up, w_down)
    out = _combine(_tile_dest(dest, COMB_TM), route, h, ln2_g[None, :], ln2_b[None, :], y, alpha)
    return out.reshape(b, t, d)


def kernel(x, positions, w_in, cmp_k_pe, cmp_k_w1, cmp_k_b1, cmp_k_w2, cmp_v_pe, cmp_v_w1, cmp_v_b1, cmp_v_w2, conv_w, conv_b, dt_bias, a_log, d_skip, ssm_norm_w, w_out, ln1_g, ln1_b, w_router_group, b_router_group, w_router_expert, b_router_expert, w_gate, w_up, w_down, ln2_g, ln2_b):
    depth = w_in.shape[0]
    alpha = (2 * depth) ** 0.25
    params = (w_in, cmp_k_pe, cmp_k_w1, cmp_k_b1, cmp_k_w2, cmp_v_pe, cmp_v_w1, cmp_v_b1, cmp_v_w2, conv_w, conv_b,
              dt_bias, a_log, d_skip, ssm_norm_w, w_out, ln1_g, ln1_b, w_router_group, b_router_group,
              w_router_expert, b_router_expert, w_gate, w_up, w_down, ln2_g, ln2_b)
    for l in range(depth):
        x = _layer(x, positions, *[p[l] for p in params], alpha)
    return x
```
